```python
import math
import jax, jax.numpy as jnp
from jax import lax
import numpy as np

D_MODEL = 2048
BATCH = 8
SEQ = 4096
DEPTH = 4

GRID_W = 64
N_MIXERS = 2
HEAD_DIM = 128
A_HEADS = 16
A_KV_HEADS = 4
ROPE_THETA = 10000.0
A_Q_BLOCK = 128
A_QKV_WIDTH = (A_HEADS + 2 * A_KV_HEADS) * HEAD_DIM
B_GROUPS = ((128, 1), (512, 4), (2048, 16))
B_HEADS_PER_GROUP = 8
B_Q_BLOCK = 64
B_WIDTH = len(B_GROUPS) * B_HEADS_PER_GROUP * HEAD_DIM
B_QKV_WIDTH = 3 * B_WIDTH
REL_BUCKETS = 32
REL_MAX_DISTANCE = 1024
D_FF = 5632
CONV_WIDTH = 3
EPS = 1e-6
NEG_INF = -1e30
N_A_LAYERS = (DEPTH + 1) // 2
N_B_LAYERS = DEPTH // 2

kernel_name = "hybrid_axial_gqa_dilated_convffn_encoder"


def rms_norm(x, gain):
    xf = x.astype(jnp.float32)
    y = xf * lax.rsqrt(jnp.mean(xf * xf, axis=-1, keepdims=True) + EPS) * gain.astype(jnp.float32)
    return y.astype(x.dtype)


def axial_rope(x):
    seq = x.shape[1]
    rows = seq // GRID_W
    row_ids = jnp.repeat(jnp.arange(rows, dtype=jnp.float32), GRID_W)
    col_ids = jnp.tile(jnp.arange(GRID_W, dtype=jnp.float32), rows)
    half = HEAD_DIM // 2
    quarter = half // 2
    inv_freq = ROPE_THETA ** (-jnp.arange(quarter, dtype=jnp.float32) / quarter)

    def rot(xs, pos):
        ang = pos[:, None] * inv_freq[None, :]
        cos = jnp.cos(ang)[None, :, None, :]
        sin = jnp.sin(ang)[None, :, None, :]
        x1, x2 = xs[..., :quarter], xs[..., quarter:]
        return jnp.concatenate([x1 * cos - x2 * sin, x2 * cos + x1 * sin], axis=-1)

    return jnp.concatenate([rot(x[..., :half], row_ids), rot(x[..., half:], col_ids)], axis=-1)


def mixer_a(h, w_qkv, w_o, q_gain, k_gain):
    b, s, _ = h.shape
    qkv = h @ w_qkv
    nq = A_HEADS * HEAD_DIM
    nk = A_KV_HEADS * HEAD_DIM
    q = qkv[..., :nq].reshape(b, s, A_HEADS, HEAD_DIM)
    k = qkv[..., nq:nq + nk].reshape(b, s, A_KV_HEADS, HEAD_DIM)
    v = qkv[..., nq + nk:].reshape(b, s, A_KV_HEADS, HEAD_DIM)
    q = axial_rope(rms_norm(q.astype(jnp.float32), q_gain)).astype(h.dtype)
    k = axial_rope(rms_norm(k.astype(jnp.float32), k_gain)).astype(h.dtype)
    grp = A_HEADS // A_KV_HEADS
    nblk = s // A_Q_BLOCK
    qb = q.reshape(b, nblk, A_Q_BLOCK, A_KV_HEADS, grp, HEAD_DIM).transpose(1, 0, 2, 3, 4, 5)
    scale = HEAD_DIM ** -0.5

    def attend(q_blk):
        logits = jnp.einsum('bqkgd,bskd->bkgqs', q_blk, k).astype(jnp.float32) * scale
        p = jax.nn.softmax(logits, axis=-1).astype(v.dtype)
        return jnp.einsum('bkgqs,bskd->bqkgd', p, v)

    o = lax.map(attend, qb)
    o = o.transpose(1, 0, 2, 3, 4, 5).reshape(b, s, nq)
    return o @ w_o


def t5_bucket(rel):
    nb = REL_BUCKETS // 2
    max_exact = nb // 2
    base = jnp.where(rel > 0, nb, 0)
    n = jnp.abs(rel)
    nf = jnp.maximum(n, 1).astype(jnp.float32)
    large = max_exact + (jnp.log(nf / max_exact) / math.log(REL_MAX_DISTANCE / max_exact)
                         * (nb - max_exact)).astype(jnp.int32)
    large = jnp.minimum(large, nb - 1)
    return base + jnp.where(n < max_exact, n, large)


def dilated_group(q, k, v, rel_bias_g, window, dilation):
    b, s, h, d = q.shape
    half_span = window // (2 * dilation)
    L = s // dilation
    nblk = -(-L // B_Q_BLOCK)
    Lp = nblk * B_Q_BLOCK
    kv_len = B_Q_BLOCK + 2 * half_span
    qs = q.reshape(b, L, dilation, h, d)
    ks = k.reshape(b, L, dilation, h, d)
    vs = v.reshape(b, L, dilation, h, d)
    qs = jnp.pad(qs, ((0, 0), (0, Lp - L), (0, 0), (0, 0), (0, 0)))
    pad_kv = ((0, 0), (half_span, Lp - L + half_span), (0, 0), (0, 0), (0, 0))
    ks = jnp.pad(ks, pad_kv)
    vs = jnp.pad(vs, pad_kv)
    blk_start = jnp.arange(nblk) * B_Q_BLOCK
    key_idx = blk_start[:, None] + jnp.arange(kv_len)[None, :]
    kb = ks[:, key_idx]
    vb = vs[:, key_idx]
    qb = qs.reshape(b, nblk, B_Q_BLOCK, dilation, h, d)
    rel = jnp.arange(kv_len)[None, :] - half_span - jnp.arange(B_Q_BLOCK)[:, None]
    bias = rel_bias_g[t5_bucket(rel * dilation)].astype(jnp.float32).transpose(2, 0, 1)
    key_pos = key_idx - half_span
    valid = (jnp.abs(rel) <= half_span)[None] & ((key_pos >= 0) & (key_pos < L))[:, None, :]
    scale = d ** -0.5
    logits = jnp.einsum('bnqchd,bnkchd->bnchqk', qb, kb).astype(jnp.float32) * scale + bias[None, None, None]
    logits = jnp.where(valid[None, :, None, None], logits, NEG_INF)
    m = jnp.max(logits, axis=-1, keepdims=True)
    p = jnp.exp(logits - m)
    l = jnp.sum(p, axis=-1, keepdims=True)
    o = jnp.einsum('bnchqk,bnkchd->bnqchd', p.astype(v.dtype), vb).astype(jnp.float32)
    l_t = l[..., 0].transpose(0, 1, 4, 2, 3)
    o = o / l_t[..., None]
    log_z = (m[..., 0] + jnp.log(l[..., 0])).transpose(0, 1, 4, 2, 3)
    o = o.reshape(b, Lp, dilation, h, d)[:, :L].reshape(b, s, h, d)
    log_z = log_z.reshape(b, Lp, dilation, h)[:, :L].reshape(b, s, h)
    return o, log_z


def mixer_b(h, w_qkv, w_o, rel_bias):
    b, s, _ = h.shape
    n_g = len(B_GROUPS)
    hg = B_HEADS_PER_GROUP
    qkv = (h @ w_qkv).reshape(b, s, n_g, 3, hg, HEAD_DIM)
    outs, log_zs = [], []
    for g, (window, dil) in enumerate(B_GROUPS):
        o, lz = dilated_group(qkv[:, :, g, 0], qkv[:, :, g, 1], qkv[:, :, g, 2],
                              rel_bias[:, g * hg:(g + 1) * hg], window, dil)
        outs.append(o)
        log_zs.append(lz)
    wts = jax.nn.softmax(jnp.stack(log_zs, axis=0), axis=0)
    y = jnp.concatenate([wts[g][..., None] * outs[g] for g in range(n_g)], axis=2)
    y = y.reshape(b, s, B_WIDTH).astype(h.dtype)
    return y @ w_o


def conv_ffn(h, w_up, conv_w, conv_b, w_down):
    u = h @ w_up
    c = u.shape[-1]
    pad = CONV_WIDTH // 2
    u = lax.conv_general_dilated(u, conv_w[:, None, :].astype(u.dtype), window_strides=(1,),
                                 padding=((pad, pad),), dimension_numbers=('NWC', 'WIO', 'NWC'),
                                 feature_group_count=c) + conv_b
    gate, val = jnp.split(u, 2, axis=-1)
    return (jax.nn.silu(gate) * val) @ w_down


def _fwd_setup_inputs(seed: int = 0) -> dict:
    key = jax.random.key(seed)
    ks = jax.random.split(key, 16)
    f32 = jnp.float32
    nrm = lambda k, shape, sc: jax.random.normal(k, shape, f32) * sc
    return {
        "x": jax.random.normal(ks[0], (BATCH, SEQ, D_MODEL), f32),
        "a_w_qkv": nrm(ks[1], (N_A_LAYERS, D_MODEL, A_QKV_WIDTH), D_MODEL ** -0.5),
        "a_w_o": nrm(ks[2], (N_A_LAYERS, A_HEADS * HEAD_DIM, D_MODEL), (A_HEADS * HEAD_DIM) ** -0.5),
        "a_q_gain": 1.0 + nrm(ks[3], (N_A_LAYERS, HEAD_DIM), 0.02),
        "a_k_gain": 1.0 + nrm(ks[4], (N_A_LAYERS, HEAD_DIM), 0.02),
        "b_w_qkv": nrm(ks[5], (N_B_LAYERS, D_MODEL, B_QKV_WIDTH), D_MODEL ** -0.5),
        "b_w_o": nrm(ks[6], (N_B_LAYERS, B_WIDTH, D_MODEL), B_WIDTH ** -0.5),
        "rel_bias": nrm(ks[7], (REL_BUCKETS, len(B_GROUPS) * B_HEADS_PER_GROUP), 0.5),
        "mix_norm": 1.0 + nrm(ks[8], (DEPTH, D_MODEL), 0.02),
        "ffn_norm": 1.0 + nrm(ks[9], (DEPTH, D_MODEL), 0.02),
        "w_up": nrm(ks[10], (DEPTH, D_MODEL, 2 * D_FF), D_MODEL ** -0.5),
        "conv_w": nrm(ks[11], (DEPTH, CONV_WIDTH, 2 * D_FF), CONV_WIDTH ** -0.5),
        "conv_b": nrm(ks[12], (DEPTH, 2 * D_FF), 0.01),
        "w_down": nrm(ks[13], (DEPTH, D_FF, D_MODEL), D_FF ** -0.5),
        "final_norm": 1.0 + nrm(ks[14], (D_MODEL,), 0.02),
    }


def _fwd_reference(x, a_w_qkv, a_w_o, a_q_gain, a_k_gain, b_w_qkv, b_w_o, rel_bias,
              mix_norm, ffn_norm, w_up, conv_w, conv_b, w_down, final_norm):
    h = x
    for i in range(DEPTH):
        hn = rms_norm(h, mix_norm[i])
        j = i // N_MIXERS
        if i % N_MIXERS == 0:
            h = h + mixer_a(hn, a_w_qkv[j], a_w_o[j], a_q_gain[j], a_k_gain[j])
        else:
            h = h + mixer_b(hn, b_w_qkv[j], b_w_o[j], rel_bias)
        h = h + conv_ffn(rms_norm(h, ffn_norm[i]), w_up[i], conv_w[i], conv_b[i], w_down[i])
    return rms_norm(h, final_norm)


import jax as _jax
import jax.numpy as _jnp

TWIN_FORMAT = 'train_step'
FWD_PARAMS = ['x', 'a_w_qkv', 'a_w_o', 'a_q_gain', 'a_k_gain', 'b_w_qkv', 'b_w_o', 'rel_bias', 'mix_norm', 'ffn_norm', 'w_up', 'conv_w', 'conv_b', 'w_down', 'final_norm']
TWIN_WEIGHTS = ['a_w_qkv', 'a_w_o', 'a_q_gain', 'a_k_gain', 'b_w_qkv', 'b_w_o', 'rel_bias', 'mix_norm', 'ffn_norm', 'w_up', 'conv_w', 'conv_b', 'w_down', 'final_norm']
TWIN_DIFF_INPUT = 'x'
TWIN_INPUTS = ['x', 'a_w_qkv', 'a_w_o', 'a_q_gain', 'a_k_gain', 'b_w_qkv', 'b_w_o', 'rel_bias', 'mix_norm', 'ffn_norm', 'w_up', 'conv_w', 'conv_b', 'w_down', 'final_norm', 'loss_target', 'm_a_w_qkv', 'm_a_w_o', 'm_a_q_gain', 'm_a_k_gain', 'm_b_w_qkv', 'm_b_w_o', 'm_rel_bias', 'm_mix_norm', 'm_ffn_norm', 'm_w_up', 'm_conv_w', 'm_conv_b', 'm_w_down', 'm_final_norm', 'v_a_w_qkv', 'v_a_w_o', 'v_a_q_gain', 'v_a_k_gain', 'v_b_w_qkv', 'v_b_w_o', 'v_rel_bias', 'v_mix_norm', 'v_ffn_norm', 'v_w_up', 'v_conv_w', 'v_conv_b', 'v_w_down', 'v_final_norm']
TWIN_OUTPUTS = ['loss', 'grad_x', 'grad_a_w_qkv', 'grad_a_w_o', 'grad_a_q_gain', 'grad_a_k_gain', 'grad_b_w_qkv', 'grad_b_w_o', 'grad_rel_bias', 'grad_mix_norm', 'grad_ffn_norm', 'grad_w_up', 'grad_conv_w', 'grad_conv_b', 'grad_w_down', 'grad_final_norm', 'delta_a_w_qkv', 'delta_a_w_o', 'delta_a_q_gain', 'delta_a_k_gain', 'delta_b_w_qkv', 'delta_b_w_o', 'delta_rel_bias', 'delta_mix_norm', 'delta_ffn_norm', 'delta_w_up', 'delta_conv_w', 'delta_conv_b', 'delta_w_down', 'delta_final_norm', 'new_m_a_w_qkv', 'new_m_a_w_o', 'new_m_a_q_gain', 'new_m_a_k_gain', 'new_m_b_w_qkv', 'new_m_b_w_o', 'new_m_rel_bias', 'new_m_mix_norm', 'new_m_ffn_norm', 'new_m_w_up', 'new_m_conv_w', 'new_m_conv_b', 'new_m_w_down', 'new_m_final_norm', 'new_v_a_w_qkv', 'new_v_a_w_o', 'new_v_a_q_gain', 'new_v_a_k_gain', 'new_v_b_w_qkv', 'new_v_b_w_o', 'new_v_rel_bias', 'new_v_mix_norm', 'new_v_ffn_norm', 'new_v_w_up', 'new_v_conv_w', 'new_v_conv_b', 'new_v_w_down', 'new_v_final_norm']
TWIN_LEAF_KINDS = {'loss': 'loss', 'grad_x': 'grad_x', 'grad_a_w_qkv': 'grad_w', 'grad_a_w_o': 'grad_w', 'grad_a_q_gain': 'grad_w', 'grad_a_k_gain': 'grad_w', 'grad_b_w_qkv': 'grad_w', 'grad_b_w_o': 'grad_w', 'grad_rel_bias': 'grad_w', 'grad_mix_norm': 'grad_w', 'grad_ffn_norm': 'grad_w', 'grad_w_up': 'grad_w', 'grad_conv_w': 'grad_w', 'grad_conv_b': 'grad_w', 'grad_w_down': 'grad_w', 'grad_final_norm': 'grad_w', 'delta_a_w_qkv': 'delta_w', 'delta_a_w_o': 'delta_w', 'delta_a_q_gain': 'delta_w', 'delta_a_k_gain': 'delta_w', 'delta_b_w_qkv': 'delta_w', 'delta_b_w_o': 'delta_w', 'delta_rel_bias': 'delta_w', 'delta_mix_norm': 'delta_w', 'delta_ffn_norm': 'delta_w', 'delta_w_up': 'delta_w', 'delta_conv_w': 'delta_w', 'delta_conv_b': 'delta_w', 'delta_w_down': 'delta_w', 'delta_final_norm': 'delta_w', 'new_m_a_w_qkv': 'new_m', 'new_m_a_w_o': 'new_m', 'new_m_a_q_gain': 'new_m', 'new_m_a_k_gain': 'new_m', 'new_m_b_w_qkv': 'new_m', 'new_m_b_w_o': 'new_m', 'new_m_rel_bias': 'new_m', 'new_m_mix_norm': 'new_m', 'new_m_ffn_norm': 'new_m', 'new_m_w_up': 'new_m', 'new_m_conv_w': 'new_m', 'new_m_conv_b': 'new_m', 'new_m_w_down': 'new_m', 'new_m_final_norm': 'new_m', 'new_v_a_w_qkv': 'new_v', 'new_v_a_w_o': 'new_v', 'new_v_a_q_gain': 'new_v', 'new_v_a_k_gain': 'new_v', 'new_v_b_w_qkv': 'new_v', 'new_v_b_w_o': 'new_v', 'new_v_rel_bias': 'new_v', 'new_v_mix_norm': 'new_v', 'new_v_ffn_norm': 'new_v', 'new_v_w_up': 'new_v', 'new_v_conv_w': 'new_v', 'new_v_conv_b': 'new_v', 'new_v_w_down': 'new_v', 'new_v_final_norm': 'new_v'}


def _forward(args):
    return _fwd_reference(*[args[k] for k in FWD_PARAMS])


def _output_shape():
    out = _jax.eval_shape(lambda: _forward(_fwd_setup_inputs(0)))
    return out.shape, out.dtype

N_MICROBATCH = 1
ADAM_LR = 0.001
ADAM_B1 = 0.9
ADAM_B2 = 0.999
ADAM_EPS = 1e-08
ADAM_WD = 0.01
ADAM_STEP = 10
PER_EXAMPLE_BATCH_AXIS = {'x': 0, 'loss_target': 0}
SHARED_INPUTS = []
_WEIGHT_DTYPES = {'a_w_qkv': _jnp.float32, 'a_w_o': _jnp.float32, 'a_q_gain': _jnp.float32, 'a_k_gain': _jnp.float32, 'b_w_qkv': _jnp.float32, 'b_w_o': _jnp.float32, 'rel_bias': _jnp.float32, 'mix_norm': _jnp.float32, 'ffn_norm': _jnp.float32, 'w_up': _jnp.float32, 'conv_w': _jnp.float32, 'conv_b': _jnp.float32, 'w_down': _jnp.float32, 'final_norm': _jnp.float32}
MOMENT_SCALE = {'a_w_qkv': 1.581350e-02, 'a_w_o': 1.200385e-02, 'a_q_gain': 4.080750e-02, 'a_k_gain': 4.224986e-02, 'b_w_qkv': 5.137137e-03, 'b_w_o': 6.756824e-03, 'rel_bias': 1.516448e-02, 'mix_norm': 1.587003e-02, 'ffn_norm': 6.716219e-02, 'w_up': 2.878561e-02, 'conv_w': 2.846429e-02, 'conv_b': 2.892063e-02, 'w_down': 4.691664e-02, 'final_norm': 1.598699e+01}


def _to_microbatches(a, axis):
    t = _jnp.moveaxis(a, axis, 0)
    t = t.reshape((N_MICROBATCH, t.shape[0] // N_MICROBATCH) + t.shape[1:])
    return _jnp.moveaxis(t, 1, axis + 1)


def setup_inputs(seed: int = 0) -> dict:
    inp = _fwd_setup_inputs(seed)
    key = _jax.random.fold_in(_jax.random.key(seed), 7919)
    shape, _ = _output_shape()
    out = dict(inp)
    out["loss_target"] = _jax.random.normal(_jax.random.fold_in(key, 0), shape, _jnp.float32)
    for i, name in enumerate(TWIN_WEIGHTS):
        w = inp[name].astype(_jnp.float32)
        if MOMENT_SCALE is None:
            s = _jnp.sqrt(_jnp.mean(_jnp.square(w)) + 1e-30)
        else:
            s = MOMENT_SCALE[name]
        km, kv = _jax.random.split(_jax.random.fold_in(key, i + 1))
        out[name] = w
        out["m_" + name] = s * _jax.random.normal(km, w.shape, _jnp.float32)
        out["v_" + name] = (s * s) * _jax.random.uniform(kv, w.shape, _jnp.float32, 0.5, 1.5)
    if N_MICROBATCH > 1:
        for name, axis in PER_EXAMPLE_BATCH_AXIS.items():
            out[name] = _to_microbatches(out[name], axis)
    return {'x': out['x'], 'a_w_qkv': out['a_w_qkv'], 'a_w_o': out['a_w_o'], 'a_q_gain': out['a_q_gain'], 'a_k_gain': out['a_k_gain'], 'b_w_qkv': out['b_w_qkv'], 'b_w_o': out['b_w_o'], 'rel_bias': out['rel_bias'], 'mix_norm': out['mix_norm'], 'ffn_norm': out['ffn_norm'], 'w_up': out['w_up'], 'conv_w': out['conv_w'], 'conv_b': out['conv_b'], 'w_down': out['w_down'], 'final_norm': out['final_norm'], 'loss_target': out['loss_target'], 'm_a_w_qkv': out['m_a_w_qkv'], 'm_a_w_o': out['m_a_w_o'], 'm_a_q_gain': out['m_a_q_gain'], 'm_a_k_gain': out['m_a_k_gain'], 'm_b_w_qkv': out['m_b_w_qkv'], 'm_b_w_o': out['m_b_w_o'], 'm_rel_bias': out['m_rel_bias'], 'm_mix_norm': out['m_mix_norm'], 'm_ffn_norm': out['m_ffn_norm'], 'm_w_up': out['m_w_up'], 'm_conv_w': out['m_conv_w'], 'm_conv_b': out['m_conv_b'], 'm_w_down': out['m_w_down'], 'm_final_norm': out['m_final_norm'], 'v_a_w_qkv': out['v_a_w_qkv'], 'v_a_w_o': out['v_a_w_o'], 'v_a_q_gain': out['v_a_q_gain'], 'v_a_k_gain': out['v_a_k_gain'], 'v_b_w_qkv': out['v_b_w_qkv'], 'v_b_w_o': out['v_b_w_o'], 'v_rel_bias': out['v_rel_bias'], 'v_mix_norm': out['v_mix_norm'], 'v_ffn_norm': out['v_ffn_norm'], 'v_w_up': out['v_w_up'], 'v_conv_w': out['v_conv_w'], 'v_conv_b': out['v_conv_b'], 'v_w_down': out['v_w_down'], 'v_final_norm': out['v_final_norm']}


def _loss(weights, diff, rest, loss_target):
    with _jax.named_scope("forward"):
        args = {**rest, TWIN_DIFF_INPUT: diff, **{k: w.astype(_WEIGHT_DTYPES[k]) for k, w in weights.items()}}
        y = _forward(args)
    with _jax.named_scope("loss_head"):
        err = _jnp.square(y.astype(_jnp.float32) - loss_target)
        return 0.5 * _jnp.sum(_jnp.mean(err, axis=-1)) if err.ndim else 0.5 * err


def _adamw(w, g, m, v):
    m = ADAM_B1 * m + (1.0 - ADAM_B1) * g
    v = ADAM_B2 * v + (1.0 - ADAM_B2) * _jnp.square(g)
    m_hat = m / (1.0 - ADAM_B1 ** ADAM_STEP)
    v_hat = v / (1.0 - ADAM_B2 ** ADAM_STEP)
    delta = -ADAM_LR * (m_hat / (_jnp.sqrt(v_hat) + ADAM_EPS) + ADAM_WD * w)
    return delta, m, v


def reference(x, a_w_qkv, a_w_o, a_q_gain, a_k_gain, b_w_qkv, b_w_o, rel_bias, mix_norm, ffn_norm, w_up, conv_w, conv_b, w_down, final_norm, loss_target, m_a_w_qkv, m_a_w_o, m_a_q_gain, m_a_k_gain, m_b_w_qkv, m_b_w_o, m_rel_bias, m_mix_norm, m_ffn_norm, m_w_up, m_conv_w, m_conv_b, m_w_down, m_final_norm, v_a_w_qkv, v_a_w_o, v_a_q_gain, v_a_k_gain, v_b_w_qkv, v_b_w_o, v_rel_bias, v_mix_norm, v_ffn_norm, v_w_up, v_conv_w, v_conv_b, v_w_down, v_final_norm):
    given = dict(x=x, a_w_qkv=a_w_qkv, a_w_o=a_w_o, a_q_gain=a_q_gain, a_k_gain=a_k_gain, b_w_qkv=b_w_qkv, b_w_o=b_w_o, rel_bias=rel_bias, mix_norm=mix_norm, ffn_norm=ffn_norm, w_up=w_up, conv_w=conv_w, conv_b=conv_b, w_down=w_down, final_norm=final_norm, loss_target=loss_target, m_a_w_qkv=m_a_w_qkv, m_a_w_o=m_a_w_o, m_a_q_gain=m_a_q_gain, m_a_k_gain=m_a_k_gain, m_b_w_qkv=m_b_w_qkv, m_b_w_o=m_b_w_o, m_rel_bias=m_rel_bias, m_mix_norm=m_mix_norm, m_ffn_norm=m_ffn_norm, m_w_up=m_w_up, m_conv_w=m_conv_w, m_conv_b=m_conv_b, m_w_down=m_w_down, m_final_norm=m_final_norm, v_a_w_qkv=v_a_w_qkv, v_a_w_o=v_a_w_o, v_a_q_gain=v_a_q_gain, v_a_k_gain=v_a_k_gain, v_b_w_qkv=v_b_w_qkv, v_b_w_o=v_b_w_o, v_rel_bias=v_rel_bias, v_mix_norm=v_mix_norm, v_ffn_norm=v_ffn_norm, v_w_up=v_w_up, v_conv_w=v_conv_w, v_conv_b=v_conv_b, v_w_down=v_w_down, v_final_norm=v_final_norm)
    weights = {n: given[n] for n in TWIN_WEIGHTS}
    shared = {n: given[n] for n in SHARED_INPUTS}
    per_example = {n: given[n] for n in ['x']}
    grad_fn = _jax.value_and_grad(_loss, argnums=(0, 1))

    def one_microbatch(ex, loss_target):
        ex = dict(ex)
        diff = ex.pop(TWIN_DIFF_INPUT)
        return grad_fn(weights, diff, {**shared, **ex}, loss_target)

    if N_MICROBATCH == 1:
        loss, (grad_w, grad_x) = one_microbatch(per_example, given["loss_target"])
    else:
        def body(carry, xs):
            loss_sum, grad_sum = carry
            l_k, (gw_k, gx_k) = one_microbatch(xs[0], xs[1])
            with _jax.named_scope("update"):
                return (loss_sum + l_k, _jax.tree.map(_jnp.add, grad_sum, gw_k)), gx_k

        init = (_jnp.zeros((), _jnp.float32), _jax.tree.map(_jnp.zeros_like, weights))
        (loss, grad_w), grad_x = _jax.lax.scan(body, init, (per_example, given["loss_target"]))
    with _jax.named_scope("update"):
        delta_w, new_m, new_v = {}, {}, {}
        for n in TWIN_WEIGHTS:
            delta_w[n], new_m[n], new_v[n] = _adamw(weights[n], grad_w[n], given["m_" + n], given["v_" + n])
    return (loss, grad_x, *[grad_w[n] for n in TWIN_WEIGHTS], *[delta_w[n] for n in TWIN_WEIGHTS],
            *[new_m[n] for n in TWIN_WEIGHTS], *[new_v[n] for n in TWIN_WEIGHTS])
```

```python
import functools
import math

import numpy as np
import jax
import jax.numpy as jnp
from jax import lax
from jax.experimental import pallas as pl
from jax.experimental.pallas import tpu as pltpu

F32 = jnp.float32
BF16 = jnp.bfloat16

HEAD_DIM = 128
A_HEADS = 16
A_KV_HEADS = 4
GRID_W = 64
ROPE_THETA = 10000.0
B_GROUPS = ((128, 1), (512, 4), (2048, 16))
B_HEADS_PER_GROUP = 8
REL_BUCKETS = 32
REL_MAX_DISTANCE = 1024
EPS = 1e-6
NEG_INF = -1e30
DEPTH = 4
ADAM_LR = 0.001
ADAM_B1 = 0.9
ADAM_B2 = 0.999
ADAM_EPS = 1e-08
ADAM_WD = 0.01
ADAM_STEP = 10

N_CHIPS = 4
LANES = 128
SUBLANES = 8
VMEM_LIMIT = 52 * 1024 * 1024
MESH = pl.DeviceIdType.MESH


def _pick(n, cands):
    for c in cands:
        if c <= n and n % c == 0:
            return c
    return n


def _cp(*sem):
    return pltpu.CompilerParams(dimension_semantics=sem if sem else None, vmem_limit_bytes=VMEM_LIMIT)


def _half_span():
    hs = {w // (2 * d) for w, d in B_GROUPS}
    assert len(hs) == 1
    return hs.pop()


def _rms_fwd(h, gain):
    S, D = h.shape
    ts = _pick(S, (512, 256, 128, 64, 32, 16))

    def body(h_ref, g_ref, o_ref):
        x = h_ref[...]
        r = lax.rsqrt(jnp.mean(x * x, axis=-1, keepdims=True) + EPS)
        o_ref[...] = (x * r * g_ref[...]).astype(o_ref.dtype)

    return pl.pallas_call(
        body, name="rms_fwd", grid=(S // ts,),
        in_specs=[pl.BlockSpec((ts, D), lambda i: (i, 0)), pl.BlockSpec((1, D), lambda i: (0, 0))],
        out_specs=pl.BlockSpec((ts, D), lambda i: (i, 0)),
        out_shape=jax.ShapeDtypeStruct((S, D), BF16), compiler_params=_cp("arbitrary"))(h, gain)


def _rms_bwd(dy, h, gain, dres):
    S, D = h.shape
    ts = _pick(S, (256, 128, 64, 32, 16))

    def body(dy_ref, h_ref, g_ref, dres_ref, dh_ref, dhb_ref, dg_ref):
        @pl.when(pl.program_id(0) == 0)
        def _():
            dg_ref[...] = jnp.zeros_like(dg_ref)
        x = h_ref[...]
        dy = dy_ref[...]
        r = lax.rsqrt(jnp.mean(x * x, axis=-1, keepdims=True) + EPS)
        xn = x * r
        dg_ref[0:1, :] += jnp.sum(dy * xn, axis=0, keepdims=True)
        dxn = dy * g_ref[...]
        dx = r * (dxn - xn * jnp.mean(dxn * xn, axis=-1, keepdims=True))
        dh = dres_ref[...] + dx
        dh_ref[...] = dh
        dhb_ref[...] = dh.astype(BF16)

    row = pl.BlockSpec((ts, D), lambda i: (i, 0))
    return pl.pallas_call(
        body, name="rms_bwd", grid=(S // ts,),
        in_specs=[row, row, pl.BlockSpec((1, D), lambda i: (0, 0)), row],
        out_specs=[row, row, pl.BlockSpec((SUBLANES, D), lambda i: (0, 0))],
        out_shape=[jax.ShapeDtypeStruct((S, D), F32), jax.ShapeDtypeStruct((S, D), BF16),
                   jax.ShapeDtypeStruct((SUBLANES, D), F32)],
        compiler_params=_cp("arbitrary"))(dy, h, gain, dres)


def _final_loss(h, gain, target):
    S, D = h.shape
    ts = _pick(S, (256, 128, 64, 32, 16))

    def body(h_ref, g_ref, t_ref, loss_ref, dh_ref, dhb_ref, dg_ref):
        @pl.when(pl.program_id(0) == 0)
        def _():
            dg_ref[...] = jnp.zeros_like(dg_ref)
            loss_ref[...] = jnp.zeros_like(loss_ref)
        x = h_ref[...]
        g = g_ref[...]
        r = lax.rsqrt(jnp.mean(x * x, axis=-1, keepdims=True) + EPS)
        xn = x * r
        err = xn * g - t_ref[...]
        part = 0.5 * jnp.sum(jnp.mean(err * err, axis=-1, keepdims=True), axis=0, keepdims=True)
        loss_ref[0:1, 0:1] += part
        dy = err * (1.0 / D)
        dg_ref[0:1, :] += jnp.sum(dy * xn, axis=0, keepdims=True)
        dxn = dy * g
        dh = r * (dxn - xn * jnp.mean(dxn * xn, axis=-1, keepdims=True))
        dh_ref[...] = dh
        dhb_ref[...] = dh.astype(BF16)

    row = pl.BlockSpec((ts, D), lambda i: (i, 0))
    return pl.pallas_call(
        body, name="final_loss", grid=(S // ts,),
        in_specs=[row, pl.BlockSpec((1, D), lambda i: (0, 0)), row],
        out_specs=[pl.BlockSpec((SUBLANES, LANES), lambda i: (0, 0)), row, row,
                   pl.BlockSpec((SUBLANES, D), lambda i: (0, 0))],
        out_shape=[jax.ShapeDtypeStruct((SUBLANES, LANES), F32), jax.ShapeDtypeStruct((S, D), F32),
                   jax.ShapeDtypeStruct((S, D), BF16), jax.ShapeDtypeStruct((SUBLANES, D), F32)],
        compiler_params=_cp("arbitrary"))(h, gain, target)


_NN = (((1,), (0,)), ((), ()))
_NT = (((1,), (1,)), ((), ()))
_TN = (((0,), (0,)), ((), ()))


def _mm_nn(a, w, layer, *, blocked, out_dtype=F32, res=None, name):
    M, K = a.shape
    if blocked:
        nq = w.shape[3]
        N = N_CHIPS * nq
        tn = _pick(nq, (256, 128))
        nps = nq // tn
        w_spec = pl.BlockSpec((None, None, K, tn), lambda i, j: (layer, j // nps, 0, j % nps))
    else:
        N = w.shape[2]
        tn = _pick(N, (256, 128))
        w_spec = pl.BlockSpec((None, K, tn), lambda i, j: (layer, 0, j))
    tm = _pick(M, (1024, 512, 256, 128, 64, 32, 16)) if K <= 3072 else _pick(M, (512, 256, 128, 64, 32, 16))

    def body(*refs):
        if res is None:
            a_ref, w_ref, o_ref = refs
            acc = lax.dot_general(a_ref[...], w_ref[...], _NN, preferred_element_type=F32)
        else:
            a_ref, w_ref, r_ref, o_ref = refs
            acc = r_ref[...] + lax.dot_general(a_ref[...], w_ref[...], _NN, preferred_element_type=F32)
        o_ref[...] = acc.astype(o_ref.dtype)

    in_specs = [pl.BlockSpec((tm, K), lambda i, j: (i, 0)), w_spec]
    args = [a, w]
    if res is not None:
        in_specs.append(pl.BlockSpec((tm, tn), lambda i, j: (i, j)))
        args.append(res)
    return pl.pallas_call(
        body, name=name, grid=(M // tm, N // tn), in_specs=in_specs,
        out_specs=pl.BlockSpec((tm, tn), lambda i, j: (i, j)),
        out_shape=jax.ShapeDtypeStruct((M, N), out_dtype),
        compiler_params=_cp("arbitrary", "arbitrary"))(*args)


def _mm_nt(a, w, layer, *, blocked, name):
    M = a.shape[0]
    tm = _pick(M, (1024, 512, 256, 128, 64, 32, 16))
    if blocked:
        K, nq = w.shape[2], w.shape[3]
        tk = _pick(K, (1024, 512, 256, 128))

        def body(a_ref, w_ref, o_ref, acc_ref):
            p = pl.program_id(2)

            @pl.when(p == 0)
            def _():
                acc_ref[...] = jnp.zeros_like(acc_ref)
            acc_ref[...] += lax.dot_general(a_ref[...], w_ref[...], _NT, preferred_element_type=F32)

            @pl.when(p == N_CHIPS - 1)
            def _():
                o_ref[...] = acc_ref[...]

        return pl.pallas_call(
            body, name=name, grid=(M // tm, K // tk, N_CHIPS),
            in_specs=[pl.BlockSpec((tm, nq), lambda i, j, p: (i, p)),
                      pl.BlockSpec((None, None, tk, nq), lambda i, j, p: (layer, p, j, 0))],
            out_specs=pl.BlockSpec((tm, tk), lambda i, j, p: (i, j)),
            out_shape=jax.ShapeDtypeStruct((M, K), F32),
            scratch_shapes=[pltpu.VMEM((tm, tk), F32)],
            compiler_params=_cp("arbitrary", "arbitrary", "arbitrary"))(a, w)
    K, N = w.shape[1], w.shape[2]
    tk = _pick(K, (512, 256, 128))

    def body(a_ref, w_ref, o_ref):
        o_ref[...] = lax.dot_general(a_ref[...], w_ref[...], _NT, preferred_element_type=F32)

    return pl.pallas_call(
        body, name=name, grid=(M // tm, K // tk),
        in_specs=[pl.BlockSpec((tm, N), lambda i, j: (i, 0)),
                  pl.BlockSpec((None, tk, N), lambda i, j: (layer, j, 0))],
        out_specs=pl.BlockSpec((tm, tk), lambda i, j: (i, j)),
        out_shape=jax.ShapeDtypeStruct((M, K), F32),
        compiler_params=_cp("arbitrary", "arbitrary"))(a, w)


def _mm_tn(x, dy, *, blocked, name):
    S, K = x.shape
    N = dy.shape[1]
    tk = _pick(K, (512, 256, 128))
    if blocked:
        nq = N // N_CHIPS
        tn = _pick(nq, (256, 128))
        nps = nq // tn
        out_spec = pl.BlockSpec((None, tk, tn), lambda i, j: (j // nps, i, j % nps))
        out_shape = jax.ShapeDtypeStruct((N_CHIPS, K, nq), BF16)
    else:
        tn = _pick(N, (512, 256, 128))
        out_spec = pl.BlockSpec((tk, tn), lambda i, j: (i, j))
        out_shape = jax.ShapeDtypeStruct((K, N), BF16)

    def body(x_ref, dy_ref, o_ref):
        o_ref[...] = lax.dot_general(x_ref[...], dy_ref[...], _TN, preferred_element_type=F32).astype(o_ref.dtype)

    return pl.pallas_call(
        body, name=name, grid=(K // tk, N // tn),
        in_specs=[pl.BlockSpec((S, tk), lambda i, j: (0, i)), pl.BlockSpec((S, tn), lambda i, j: (0, j))],
        out_specs=out_spec, out_shape=out_shape,
        compiler_params=_cp("arbitrary", "arbitrary"))(x, dy)


def _rope_tables(S):
    rows = S // GRID_W
    row_ids = jnp.repeat(jnp.arange(rows, dtype=F32), GRID_W)
    col_ids = jnp.tile(jnp.arange(GRID_W, dtype=F32), rows)
    quarter = HEAD_DIM // 4
    inv_freq = ROPE_THETA ** (-jnp.arange(quarter, dtype=F32) / quarter)
    ang_r = row_ids[:, None] * inv_freq[None, :]
    ang_c = col_ids[:, None] * inv_freq[None, :]
    cos = jnp.concatenate([jnp.cos(ang_r)] * 2 + [jnp.cos(ang_c)] * 2, axis=-1)
    sin = jnp.concatenate([-jnp.sin(ang_r), jnp.sin(ang_r), -jnp.sin(ang_c), jnp.sin(ang_c)], axis=-1)
    return cos, sin


def _swap_quarters(x):
    lane = lax.broadcasted_iota(jnp.int32, x.shape, 1)
    first = (lane % (HEAD_DIM // 2)) < (HEAD_DIM // 4)
    return jnp.where(first, pltpu.roll(x, HEAD_DIM - HEAD_DIM // 4, 1), pltpu.roll(x, HEAD_DIM // 4, 1))


def _prep_a_fwd(qkv, cos, sin, gq, gk):
    S, W = qkv.shape
    nrm = A_HEADS + A_KV_HEADS
    ts = _pick(S, (256, 128, 64, 32, 16))

    def body(qkv_ref, cos_ref, sin_ref, gq_ref, gk_ref, o_ref):
        cos_t = cos_ref[...]
        sin_t = sin_ref[...]
        for j in range(nrm):
            sl = slice(j * HEAD_DIM, (j + 1) * HEAD_DIM)
            x = qkv_ref[:, sl]
            g = gq_ref[...] if j < A_HEADS else gk_ref[...]
            r = lax.rsqrt(jnp.mean(x * x, axis=-1, keepdims=True) + EPS)
            n = x * r * g
            o_ref[:, sl] = (n * cos_t + _swap_quarters(n) * sin_t).astype(BF16)
        o_ref[:, nrm * HEAD_DIM:] = qkv_ref[:, nrm * HEAD_DIM:].astype(BF16)

    row = lambda w: pl.BlockSpec((ts, w), lambda i: (i, 0))
    one = pl.BlockSpec((1, HEAD_DIM), lambda i: (0, 0))
    return pl.pallas_call(
        body, name="prep_a_fwd", grid=(S // ts,),
        in_specs=[row(W), row(HEAD_DIM), row(HEAD_DIM), one, one], out_specs=row(W),
        out_shape=jax.ShapeDtypeStruct((S, W), BF16), compiler_params=_cp("arbitrary"))(qkv, cos, sin, gq, gk)


def _prep_a_bwd(dq, dk, dv, qkv, cos, sin, gq, gk):
    S, W = qkv.shape
    nrm = A_HEADS + A_KV_HEADS
    nq, nk = A_HEADS * HEAD_DIM, A_KV_HEADS * HEAD_DIM
    ts = _pick(S, (256, 128, 64, 32, 16))

    def body(dq_ref, dk_ref, dv_ref, qkv_ref, cos_ref, sin_ref, gq_ref, gk_ref, o_ref, dg_ref):
        @pl.when(pl.program_id(0) == 0)
        def _():
            dg_ref[...] = jnp.zeros_like(dg_ref)
        cos_t = cos_ref[...]
        sin_t = sin_ref[...]
        for j in range(nrm):
            sl = slice(j * HEAD_DIM, (j + 1) * HEAD_DIM)
            x = qkv_ref[:, sl]
            if j < A_HEADS:
                dy, g, grow = dq_ref[:, sl], gq_ref[...], 0
            else:
                jj = j - A_HEADS
                dy, g, grow = dk_ref[:, jj * HEAD_DIM:(jj + 1) * HEAD_DIM], gk_ref[...], 1
            r = lax.rsqrt(jnp.mean(x * x, axis=-1, keepdims=True) + EPS)
            xn = x * r
            dn = dy * cos_t + _swap_quarters(dy * sin_t)
            dg_ref[grow:grow + 1, :] += jnp.sum(dn * xn, axis=0, keepdims=True)
            dxn = dn * g
            o_ref[:, sl] = (r * (dxn - xn * jnp.mean(dxn * xn, axis=-1, keepdims=True))).astype(BF16)
        o_ref[:, nrm * HEAD_DIM:] = dv_ref[...].astype(BF16)

    row = lambda w: pl.BlockSpec((ts, w), lambda i: (i, 0))
    one = pl.BlockSpec((1, HEAD_DIM), lambda i: (0, 0))
    return pl.pallas_call(
        body, name="prep_a_bwd", grid=(S // ts,),
        in_specs=[row(nq), row(nk), row(nk), row(W), row(HEAD_DIM), row(HEAD_DIM), one, one],
        out_specs=[row(W), pl.BlockSpec((SUBLANES, HEAD_DIM), lambda i: (0, 0))],
        out_shape=[jax.ShapeDtypeStruct((S, W), BF16), jax.ShapeDtypeStruct((SUBLANES, HEAD_DIM), F32)],
        compiler_params=_cp("arbitrary"))(dq, dk, dv, qkv, cos, sin, gq, gk)


def _flash_a_fwd(qkvh):
    S = qkvh.shape[0]
    grp = A_HEADS // A_KV_HEADS
    tq = _pick(S, (256, 128, 64, 32, 16))
    scale = HEAD_DIM ** -0.5

    def body(q_ref, k_ref, v_ref, o_ref, lse_ref):
        s = lax.dot_general(q_ref[...], k_ref[...], _NT, preferred_element_type=F32) * scale
        m = jnp.max(s, axis=-1, keepdims=True)
        p = jnp.exp(s - m)
        l = jnp.sum(p, axis=-1, keepdims=True)
        pn = (p * (1.0 / l)).astype(BF16)
        o_ref[...] = lax.dot_general(pn, v_ref[...], _NN, preferred_element_type=F32).astype(BF16)
        lse_ref[...] = jnp.broadcast_to(m + jnp.log(l), lse_ref.shape)

    qs = pl.BlockSpec((tq, HEAD_DIM), lambda h, i: (i, h))
    return pl.pallas_call(
        body, name="flash_a_fwd", grid=(A_HEADS, S // tq),
        in_specs=[qs,
                  pl.BlockSpec((S, HEAD_DIM), lambda h, i: (0, A_HEADS + h // grp)),
                  pl.BlockSpec((S, HEAD_DIM), lambda h, i: (0, A_HEADS + A_KV_HEADS + h // grp))],
        out_specs=[qs, qs],
        out_shape=[jax.ShapeDtypeStruct((S, A_HEADS * HEAD_DIM), BF16),
                   jax.ShapeDtypeStruct((S, A_HEADS * HEAD_DIM), F32)],
        compiler_params=_cp("arbitrary", "arbitrary"))(qkvh, qkvh, qkvh)


def _flash_a_bwd(qkvh, do, o, lse):
    S = qkvh.shape[0]
    grp = A_HEADS // A_KV_HEADS
    tq = _pick(S, (256, 128, 64, 32, 16))
    scale = HEAD_DIM ** -0.5

    def body(q_ref, k_ref, v_ref, do_ref, o_ref, lse_ref, dq_ref, dk_ref, dv_ref):
        @pl.when((pl.program_id(1) == 0) & (pl.program_id(2) == 0))
        def _():
            dk_ref[...] = jnp.zeros_like(dk_ref)
            dv_ref[...] = jnp.zeros_like(dv_ref)
        q = q_ref[...]
        k = k_ref[...]
        do_f = do_ref[...]
        do_b = do_f.astype(BF16)
        s = lax.dot_general(q, k, _NT, preferred_element_type=F32) * scale
        p = jnp.exp(s - lse_ref[:, 0:1])
        dp = lax.dot_general(do_b, v_ref[...], _NT, preferred_element_type=F32)
        delta = jnp.sum(do_f * o_ref[...].astype(F32), axis=-1, keepdims=True)
        ds_b = (p * (dp - delta) * scale).astype(BF16)
        dq_ref[...] = lax.dot_general(ds_b, k, _NN, preferred_element_type=F32)
        dk_ref[...] += lax.dot_general(ds_b, q, _TN, preferred_element_type=F32)
        dv_ref[...] += lax.dot_general(p.astype(BF16), do_b, _TN, preferred_element_type=F32)

    qs = pl.BlockSpec((tq, HEAD_DIM), lambda kv, g, i: (i, kv * grp + g))
    kvs = lambda off: pl.BlockSpec((S, HEAD_DIM), lambda kv, g, i: (0, off + kv))
    return pl.pallas_call(
        body, name="flash_a_bwd", grid=(A_KV_HEADS, grp, S // tq),
        in_specs=[qs, kvs(A_HEADS), kvs(A_HEADS + A_KV_HEADS), qs, qs, qs],
        out_specs=[qs, kvs(0), kvs(0)],
        out_shape=[jax.ShapeDtypeStruct((S, A_HEADS * HEAD_DIM), F32),
                   jax.ShapeDtypeStruct((S, A_KV_HEADS * HEAD_DIM), F32),
                   jax.ShapeDtypeStruct((S, A_KV_HEADS * HEAD_DIM), F32)],
        compiler_params=_cp("arbitrary", "arbitrary", "arbitrary"))(qkvh, qkvh, qkvh, do, o, lse)


def _bucket_tables():
    hs = _half_span()
    tq, kv = 2 * hs, 4 * hs
    nb = REL_BUCKETS // 2
    max_exact = nb // 2
    out = np.zeros((len(B_GROUPS), 3, tq, kv), np.int32)
    for g, (_, dil) in enumerate(B_GROUPS):
        for case, off in enumerate((0, hs, 2 * hs)):
            rel = np.arange(kv)[None, :] - np.arange(tq)[:, None] - off
            r = rel * dil
            n = np.abs(r)
            nf = np.maximum(n, 1).astype(np.float32)
            large = max_exact + (np.log(nf / np.float32(max_exact)) / np.float32(math.log(REL_MAX_DISTANCE / max_exact))
                                 * np.float32(nb - max_exact)).astype(np.int32)
            large = np.minimum(large, nb - 1)
            bucket = np.where(r > 0, nb, 0) + np.where(n < max_exact, n, large)
            out[g, case] = np.where(np.abs(rel) <= hs, bucket, -1)
    return out


def _bias_build(rel_bias, buckets):
    G, _, tq, kv = buckets.shape
    hg = B_HEADS_PER_GROUP

    def body(rb_ref, bk_ref, o_ref):
        col = pl.program_id(0) * hg + pl.program_id(2)
        bk = bk_ref[...]
        acc = jnp.full((tq, kv), NEG_INF, F32)
        for b in range(REL_BUCKETS):
            acc = jnp.where(bk == b, rb_ref[b, col], acc)
        o_ref[...] = acc

    return pl.pallas_call(
        body, name="bias_build", grid=(G, 3, hg),
        in_specs=[pl.BlockSpec(memory_space=pltpu.SMEM),
                  pl.BlockSpec((None, None, tq, kv), lambda g, c, h: (g, c, 0, 0))],
        out_specs=pl.BlockSpec((None, None, None, tq, kv), lambda g, c, h: (g, c, h, 0, 0)),
        out_shape=jax.ShapeDtypeStruct((G, 3, hg, tq, kv), F32),
        compiler_params=_cp("arbitrary", "arbitrary", "arbitrary"))(rel_bias, buckets)


def _bias_reduce(dbias_list, buckets):
    G, _, tq, kv = buckets.shape
    hg = B_HEADS_PER_GROUP
    n = len(dbias_list)

    def body(*refs):
        bk_ref, o_ref = refs[n], refs[n + 1]
        first = (pl.program_id(0) == 0) & (pl.program_id(1) == 0) & (pl.program_id(2) == 0)

        @pl.when(first)
        def _():
            o_ref[...] = jnp.zeros_like(o_ref)
        col = pl.program_id(0) * hg + pl.program_id(2)
        db = refs[0][...]
        for r in refs[1:n]:
            db = db + r[...]
        bk = bk_ref[...]
        rows = lax.broadcasted_iota(jnp.int32, (REL_BUCKETS, LANES), 0)
        cols = lax.broadcasted_iota(jnp.int32, (REL_BUCKETS, LANES), 1)
        acc = jnp.zeros((REL_BUCKETS, LANES), F32)
        for b in range(REL_BUCKETS):
            val = jnp.sum(jnp.sum(jnp.where(bk == b, db, 0.0), axis=1, keepdims=True), axis=0, keepdims=True)
            acc = acc + jnp.where((rows == b) & (cols == col), val, 0.0)
        o_ref[...] += acc

    tile = pl.BlockSpec((None, None, None, tq, kv), lambda g, c, h: (g, c, h, 0, 0))
    return pl.pallas_call(
        body, name="bias_reduce", grid=(G, 3, hg),
        in_specs=[tile] * n + [pl.BlockSpec((None, None, tq, kv), lambda g, c, h: (g, c, 0, 0))],
        out_specs=pl.BlockSpec((REL_BUCKETS, LANES), lambda g, c, h: (0, 0)),
        out_shape=jax.ShapeDtypeStruct((REL_BUCKETS, LANES), F32),
        compiler_params=_cp("arbitrary", "arbitrary", "arbitrary"))(*dbias_list, buckets)


def _b_specs(g, dil, L):
    hs = _half_span()
    tq = 2 * hs
    hg = B_HEADS_PER_GROUP
    nq = L // tq
    base = g * 3 * hg

    def qkv_col(which):
        return lambda c, h, i: c * (len(B_GROUPS) * 3 * hg) + base + which * hg + h

    q_spec = pl.BlockSpec((tq, HEAD_DIM), lambda c, h, i: (i, qkv_col(0)(c, h, i)))
    k_spec = pl.BlockSpec((L, HEAD_DIM), lambda c, h, i: (0, qkv_col(1)(c, h, i)))
    v_spec = pl.BlockSpec((L, HEAD_DIM), lambda c, h, i: (0, qkv_col(2)(c, h, i)))
    case = lambda i: jnp.where(i == 0, 0, jnp.where(i == nq - 1, 2, 1))
    bias_spec = pl.BlockSpec((None, None, None, tq, 2 * tq), lambda c, h, i: (g, case(i), h, 0, 0))
    blk_spec = pl.BlockSpec((tq, HEAD_DIM), lambda c, h, i: (i, c * hg + h))
    full_spec = pl.BlockSpec((L, HEAD_DIM), lambda c, h, i: (0, c * hg + h))
    return q_spec, k_spec, v_spec, bias_spec, blk_spec, full_spec, nq


def _battn_fwd(qkv, bias, g):
    S, W = qkv.shape
    dil = B_GROUPS[g][1]
    hs = _half_span()
    tq, kvl = 2 * hs, 4 * hs
    hg = B_HEADS_PER_GROUP
    L = S // dil
    q_spec, k_spec, v_spec, bias_spec, blk_spec, _, nq = _b_specs(g, dil, L)
    scale = HEAD_DIM ** -0.5
    view = qkv.reshape(L, dil * W)

    def body(q_ref, k_ref, v_ref, b_ref, o_ref, lz_ref):
        ks = pl.multiple_of(jnp.clip(pl.program_id(2) * tq - hs, 0, L - kvl), hs)
        kw = k_ref[pl.ds(ks, kvl), :].astype(BF16)
        vw = v_ref[pl.ds(ks, kvl), :].astype(BF16)
        s = lax.dot_general(q_ref[...].astype(BF16), kw, _NT, preferred_element_type=F32) * scale + b_ref[...]
        m = jnp.max(s, axis=-1, keepdims=True)
        p = jnp.exp(s - m)
        l = jnp.sum(p, axis=-1, keepdims=True)
        o_ref[...] = lax.dot_general(p.astype(BF16), vw, _NN, preferred_element_type=F32) / l
        lz_ref[...] = jnp.broadcast_to(m + jnp.log(l), lz_ref.shape)

    o, lz = pl.pallas_call(
        body, name="battn_fwd_g%d" % g, grid=(dil, hg, nq),
        in_specs=[q_spec, k_spec, v_spec, bias_spec], out_specs=[blk_spec, blk_spec],
        out_shape=[jax.ShapeDtypeStruct((L, dil * hg * HEAD_DIM), F32)] * 2,
        compiler_params=_cp("arbitrary", "arbitrary", "arbitrary"))(view, view, view, bias)
    return o.reshape(S, hg * HEAD_DIM), lz.reshape(S, hg * HEAD_DIM)


def _battn_bwd(qkv, bias, do, o, lz, dlz, g):
    S, W = qkv.shape
    dil = B_GROUPS[g][1]
    hs = _half_span()
    tq, kvl = 2 * hs, 4 * hs
    hg = B_HEADS_PER_GROUP
    L = S // dil
    q_spec, k_spec, v_spec, bias_spec, blk_spec, full_spec, nq = _b_specs(g, dil, L)
    scale = HEAD_DIM ** -0.5
    view = qkv.reshape(L, dil * W)
    gv = lambda a: a.reshape(L, dil * hg * HEAD_DIM)

    def body(q_ref, k_ref, v_ref, b_ref, do_ref, o_ref, lz_ref, dlz_ref, dq_ref, dk_ref, dv_ref, db_ref):
        c, h, i = pl.program_id(0), pl.program_id(1), pl.program_id(2)

        @pl.when((c == 0) & (h == 0) & (i == 0))
        def _():
            db_ref[...] = jnp.zeros_like(db_ref)

        @pl.when(i == 0)
        def _():
            dk_ref[...] = jnp.zeros_like(dk_ref)
            dv_ref[...] = jnp.zeros_like(dv_ref)
        ks = pl.multiple_of(jnp.clip(i * tq - hs, 0, L - kvl), hs)
        case = jnp.where(i == 0, 0, jnp.where(i == nq - 1, 2, 1))
        q = q_ref[...].astype(BF16)
        kw = k_ref[pl.ds(ks, kvl), :].astype(BF16)
        vw = v_ref[pl.ds(ks, kvl), :].astype(BF16)
        do_f = do_ref[...]
        do_b = do_f.astype(BF16)
        s = lax.dot_general(q, kw, _NT, preferred_element_type=F32) * scale + b_ref[...]
        p = jnp.exp(s - lz_ref[:, 0:1])
        dp = lax.dot_general(do_b, vw, _NT, preferred_element_type=F32)
        delta = jnp.sum(do_f * o_ref[...], axis=-1, keepdims=True)
        ds = p * (dp - delta + dlz_ref[:, 0:1])
        db_ref[case, h] += ds
        ds_b = (ds * scale).astype(BF16)
        dq_ref[...] = lax.dot_general(ds_b, kw, _NN, preferred_element_type=F32)
        dk_ref[pl.ds(ks, kvl), :] += lax.dot_general(ds_b, q, _TN, preferred_element_type=F32)
        dv_ref[pl.ds(ks, kvl), :] += lax.dot_general(p.astype(BF16), do_b, _TN, preferred_element_type=F32)

    grp = jax.ShapeDtypeStruct((L, dil * hg * HEAD_DIM), F32)
    dq, dk, dv, db = pl.pallas_call(
        body, name="battn_bwd_g%d" % g, grid=(dil, hg, nq),
        in_specs=[q_spec, k_spec, v_spec, bias_spec, blk_spec, blk_spec, blk_spec, blk_spec],
        out_specs=[blk_spec, full_spec, full_spec,
                   pl.BlockSpec((3, hg, tq, kvl), lambda c, h, i: (0, 0, 0, 0))],
        out_shape=[grp, grp, grp, jax.ShapeDtypeStruct((3, hg, tq, kvl), F32)],
        compiler_params=_cp("arbitrary", "arbitrary", "arbitrary"))(
            view, view, view, bias, gv(do), gv(o), gv(lz), gv(dlz))
    r = lambda a: a.reshape(S, hg * HEAD_DIM)
    return r(dq), r(dk), r(dv), db


def _group_weights(lz_refs, sl):
    z = [r[:, sl] for r in lz_refs]
    mx = functools.reduce(jnp.maximum, z)
    e = [jnp.exp(v - mx) for v in z]
    inv = 1.0 / functools.reduce(lambda a, b: a + b, e)
    return [v * inv for v in e]


def _combine_fwd(os_, lzs):
    G = len(os_)
    S, Wg = os_[0].shape
    hg = B_HEADS_PER_GROUP
    ts = _pick(S, (256, 128, 64, 32, 16))

    def body(*refs):
        o_refs, lz_refs, y_ref = refs[:G], refs[G:2 * G], refs[2 * G]
        for h in range(hg):
            sl = slice(h * HEAD_DIM, (h + 1) * HEAD_DIM)
            w = _group_weights(lz_refs, sl)
            for g in range(G):
                y_ref[:, (g * hg + h) * HEAD_DIM:(g * hg + h + 1) * HEAD_DIM] = (w[g] * o_refs[g][:, sl]).astype(BF16)

    row = pl.BlockSpec((ts, Wg), lambda i: (i, 0))
    return pl.pallas_call(
        body, name="combine_fwd", grid=(S // ts,), in_specs=[row] * (2 * G),
        out_specs=pl.BlockSpec((ts, G * Wg), lambda i: (i, 0)),
        out_shape=jax.ShapeDtypeStruct((S, G * Wg), BF16), compiler_params=_cp("arbitrary"))(*os_, *lzs)


def _combine_bwd(dy, os_, lzs):
    G = len(os_)
    S, Wg = os_[0].shape
    hg = B_HEADS_PER_GROUP
    ts = _pick(S, (256, 128, 64, 32, 16))

    def body(*refs):
        dy_ref, o_refs, lz_refs = refs[0], refs[1:1 + G], refs[1 + G:1 + 2 * G]
        do_refs, dlz_refs = refs[1 + 2 * G:1 + 3 * G], refs[1 + 3 * G:1 + 4 * G]
        for h in range(hg):
            sl = slice(h * HEAD_DIM, (h + 1) * HEAD_DIM)
            w = _group_weights(lz_refs, sl)
            dw = []
            for g in range(G):
                dyg = dy_ref[:, (g * hg + h) * HEAD_DIM:(g * hg + h + 1) * HEAD_DIM]
                dw.append(jnp.sum(dyg * o_refs[g][:, sl], axis=-1, keepdims=True))
                do_refs[g][:, sl] = w[g] * dyg
            tot = functools.reduce(lambda a, b: a + b, [w[g] * dw[g] for g in range(G)])
            for g in range(G):
                dlz_refs[g][:, sl] = w[g] * (dw[g] - tot)

    row = pl.BlockSpec((ts, Wg), lambda i: (i, 0))
    outs = pl.pallas_call(
        body, name="combine_bwd", grid=(S // ts,),
        in_specs=[pl.BlockSpec((ts, G * Wg), lambda i: (i, 0))] + [row] * (2 * G),
        out_specs=[row] * (2 * G), out_shape=[jax.ShapeDtypeStruct((S, Wg), F32)] * (2 * G),
        compiler_params=_cp("arbitrary"))(dy, *os_, *lzs)
    return outs[:G], outs[G:]


def _concat_cast(parts):
    S = parts[0].shape[0]
    widths = [p.shape[1] for p in parts]
    ts = _pick(S, (256, 128, 64, 32, 16))

    def body(*refs):
        o_ref = refs[len(parts)]
        off = 0
        for r, w in zip(refs, widths):
            o_ref[:, off:off + w] = r[...].astype(BF16)
            off += w

    return pl.pallas_call(
        body, name="concat_cast", grid=(S // ts,),
        in_specs=[pl.BlockSpec((ts, w), lambda i: (i, 0)) for w in widths],
        out_specs=pl.BlockSpec((ts, sum(widths)), lambda i: (i, 0)),
        out_shape=jax.ShapeDtypeStruct((S, sum(widths)), BF16), compiler_params=_cp("arbitrary"))(*parts)


def _ffn_specs(S, dff, cq, ts, tc, layer, order):
    nfc = dff // tc
    nps = cq // tc
    hb = ts // SUBLANES
    nrow8 = S // SUBLANES

    def u_main(half):
        return pl.BlockSpec((ts, tc), lambda *g: (order(*g)[0], order(*g)[1] % nfc + half * nfc))

    def u_prev(half):
        return pl.BlockSpec((SUBLANES, tc), lambda *g: (jnp.maximum(order(*g)[0] * hb - 1, 0),
                                                         order(*g)[1] % nfc + half * nfc))

    def u_next(half):
        return pl.BlockSpec((SUBLANES, tc), lambda *g: (jnp.minimum((order(*g)[0] + 1) * hb, nrow8 - 1),
                                                         order(*g)[1] % nfc + half * nfc))

    def cw(half):
        def im(*g):
            jj = order(*g)[1] % nfc + half * nfc
            return (layer, jj // nps, 0, jj % nps)
        return pl.BlockSpec((None, None, 3, tc), im)

    def cb(half):
        return pl.BlockSpec((None, 1, tc), lambda *g: (layer, 0, order(*g)[1] % nfc + half * nfc))

    return nfc, u_main, u_prev, u_next, cw, cb


def _ffn_act_fwd(u, cw_full, cb3, layer):
    S, two_dff = u.shape
    dff = two_dff // 2
    cq = cw_full.shape[3]
    ts = _pick(S, (512, 256, 128, 64, 32, 16))
    tc = _pick(cq, (256, 128))
    order = lambda j, i: (i, j)
    nfc, u_main, u_prev, u_next, cw, cb = _ffn_specs(S, dff, cq, ts, tc, layer, order)
    nrow = S // ts

    def body(ug, ugp, ugn, uv, uvp, uvn, wg, wv, bg, bv, a_ref):
        i = pl.program_id(1)
        row = lax.broadcasted_iota(jnp.int32, (ts, tc), 0)

        def conv(x_ref, p_ref, n_ref, w_ref, b_ref):
            x = x_ref[...]
            prev = jnp.where(i > 0, p_ref[SUBLANES - 1:SUBLANES, :], 0.0)
            nxt = jnp.where(i < nrow - 1, n_ref[0:1, :], 0.0)
            xm = jnp.where(row == 0, prev, pltpu.roll(x, 1, 0))
            xp = jnp.where(row == ts - 1, nxt, pltpu.roll(x, ts - 1, 0))
            return w_ref[0:1, :] * xm + w_ref[1:2, :] * x + w_ref[2:3, :] * xp + b_ref[...]

        gc = conv(ug, ugp, ugn, wg, bg)
        vc = conv(uv, uvp, uvn, wv, bv)
        a_ref[...] = (gc * (1.0 / (1.0 + jnp.exp(-gc))) * vc).astype(BF16)

    return pl.pallas_call(
        body, name="ffn_act_fwd", grid=(nfc, nrow),
        in_specs=[u_main(0), u_prev(0), u_next(0), u_main(1), u_prev(1), u_next(1), cw(0), cw(1), cb(0), cb(1)],
        out_specs=pl.BlockSpec((ts, tc), lambda j, i: (i, j)),
        out_shape=jax.ShapeDtypeStruct((S, dff), BF16),
        compiler_params=_cp("arbitrary", "arbitrary"))(u, u, u, u, u, u, cw_full, cw_full, cb3, cb3)


def _ffn_act_bwd(u, da, cw_full, cb3, layer):
    S, two_dff = u.shape
    dff = two_dff // 2
    cq = cw_full.shape[3]
    ts = _pick(S, (256, 128, 64, 32, 16))
    tc = _pick(cq, (256, 128))
    order = lambda j, i: (i, j)
    nfc, u_main, u_prev, u_next, cw, cb = _ffn_specs(S, dff, cq, ts, tc, layer, order)
    nrow = S // ts
    hb = ts // SUBLANES
    te = ts + 2 * SUBLANES
    da_main = pl.BlockSpec((ts, tc), lambda j, i: (i, j % nfc))
    da_prev = pl.BlockSpec((SUBLANES, tc), lambda j, i: (jnp.maximum(i * hb - 1, 0), j % nfc))
    da_next = pl.BlockSpec((SUBLANES, tc), lambda j, i: (jnp.minimum((i + 1) * hb, S // SUBLANES - 1), j % nfc))

    def body(ug, ugp, ugn, uv, uvp, uvn, dam, dap, dan, wg, wv, bg, bv, du_ref, acc_ref):
        j, i = pl.program_id(0), pl.program_id(1)

        @pl.when(i == 0)
        def _():
            acc_ref[...] = jnp.zeros_like(acc_ref)
        is_gate = j < nfc
        grow = i * ts - SUBLANES + lax.broadcasted_iota(jnp.int32, (te, tc), 0)
        inside = (grow >= 0) & (grow < S)

        def ext(m, p, n):
            return jnp.where(inside, jnp.concatenate([p[...], m[...], n[...]], axis=0), 0.0)

        def shift(x):
            return pltpu.roll(x, 1, 0), pltpu.roll(x, te - 1, 0)

        def conv(x, w_ref, b_ref):
            xm, xp = shift(x)
            return w_ref[0:1, :] * xm + w_ref[1:2, :] * x + w_ref[2:3, :] * xp + b_ref[...]

        xg = ext(ug, ugp, ugn)
        xv = ext(uv, uvp, uvn)
        dae = ext(dam, dap, dan)
        gc = conv(xg, wg, bg)
        vc = conv(xv, wv, bv)
        sig = 1.0 / (1.0 + jnp.exp(-gc))
        silu = gc * sig
        duc = jnp.where(is_gate, dae * vc * (sig * (1.0 + gc * (1.0 - sig))), dae * silu)
        w_self = jnp.where(is_gate, wg[...], wv[...])
        x_self = jnp.where(is_gate, xg, xv)
        dm, dp = shift(duc)
        du = w_self[0:1, :] * dp + w_self[1:2, :] * duc + w_self[2:3, :] * dm
        du_ref[...] = du[SUBLANES:SUBLANES + ts, :].astype(BF16)
        xm, xp = shift(x_self)
        main = slice(SUBLANES, SUBLANES + ts)
        dcm = duc[main, :]
        acc_ref[0:1, :] += jnp.sum(dcm * xm[main, :], axis=0, keepdims=True)
        acc_ref[1:2, :] += jnp.sum(dcm * x_self[main, :], axis=0, keepdims=True)
        acc_ref[2:3, :] += jnp.sum(dcm * xp[main, :], axis=0, keepdims=True)
        acc_ref[3:4, :] += jnp.sum(dcm, axis=0, keepdims=True)

    return pl.pallas_call(
        body, name="ffn_act_bwd", grid=(2 * nfc, nrow),
        in_specs=[u_main(0), u_prev(0), u_next(0), u_main(1), u_prev(1), u_next(1), da_main, da_prev, da_next,
                  cw(0), cw(1), cb(0), cb(1)],
        out_specs=[pl.BlockSpec((ts, tc), lambda j, i: (i, j)), pl.BlockSpec((SUBLANES, tc), lambda j, i: (0, j))],
        out_shape=[jax.ShapeDtypeStruct((S, two_dff), BF16), jax.ShapeDtypeStruct((SUBLANES, two_dff), F32)],
        compiler_params=_cp("arbitrary", "arbitrary"))(u, u, u, u, u, u, da, da, da, cw_full, cw_full, cb3, cb3)


def _cast_bf16(w2d):
    R, C = w2d.shape
    tr = _pick(R, (512, 256, 128, 64, 32, 16, 8))

    def body(w_ref, o_ref):
        o_ref[...] = w_ref[...].astype(BF16)

    spec = pl.BlockSpec((tr, C), lambda i: (i, 0))
    return pl.pallas_call(body, name="cast_bf16", grid=(R // tr,), in_specs=[spec], out_specs=spec,
                          out_shape=jax.ShapeDtypeStruct((R, C), BF16), compiler_params=_cp("arbitrary"))(w2d)


def _adam_math(w, g, m, v):
    m = ADAM_B1 * m + (1.0 - ADAM_B1) * g
    v = ADAM_B2 * v + (1.0 - ADAM_B2) * (g * g)
    m_hat = m / (1.0 - ADAM_B1 ** ADAM_STEP)
    v_hat = v / (1.0 - ADAM_B2 ** ADAM_STEP)
    delta = -ADAM_LR * (m_hat / (jnp.sqrt(v_hat) + ADAM_EPS) + ADAM_WD * w)
    return delta, m, v


def _adamw(w, g, m, v):
    R, C = w.shape
    tr = _pick(R, (128, 64, 32, 16, 8)) if R % SUBLANES == 0 and C % LANES == 0 else R

    def body(w_ref, g_ref, m_ref, v_ref, d_ref, nm_ref, nv_ref):
        d, nm, nv = _adam_math(w_ref[...], g_ref[...], m_ref[...], v_ref[...])
        d_ref[...] = d
        nm_ref[...] = nm
        nv_ref[...] = nv

    spec = pl.BlockSpec((tr, C), lambda i: (i, 0))
    return pl.pallas_call(
        body, name="adamw", grid=(R // tr,), in_specs=[spec] * 4, out_specs=[spec] * 3,
        out_shape=[jax.ShapeDtypeStruct((R, C), F32)] * 3, compiler_params=_cp("arbitrary"))(w, g, m, v)


ANY = pl.BlockSpec(memory_space=pl.ANY)


def _position():
    x, y, c = lax.axis_index("x"), lax.axis_index("y"), lax.axis_index("c")
    chips = [(1 - x, y), (x, 1 - y), (1 - x, 1 - y)]
    return x, y, c, chips


def _gather_weights(shards):
    n = len(shards)
    for s in shards:
        assert s.shape[0] % 2 == 0

    def body(*refs):
        src, dst = refs[:n], refs[n:2 * n]
        send, recv, local = refs[2 * n], refs[2 * n + 1], refs[2 * n + 2]
        x, y, c, chips = _position()
        me = 2 * x + y
        sib = (x, y, 1 - c)

        def part(t, chip, half):
            lh = shards[t].shape[0] // 2
            return dst[t].at[pl.ds(half * lh, lh), 2 * chip[0] + chip[1]]

        def copy(t, k, chip, half, to, from_src):
            lh = shards[t].shape[0] // 2
            s = src[t].at[pl.ds(half * lh, lh)] if from_src else part(t, chip, half)
            return pltpu.make_async_remote_copy(
                src_ref=s, dst_ref=part(t, chip, half), send_sem=send.at[t * 6 + k], recv_sem=recv.at[t * 6 + k],
                device_id=to, device_id_type=MESH)

        mine = [pltpu.make_async_copy(src[t], dst[t].at[:, me], local.at[t]) for t in range(n)]
        for cp in mine:
            cp.start()
        first = [copy(t, j, (x, y), c, (*chips[j], c), True) for t in range(n) for j in range(3)]
        for cp in first:
            cp.start()
        passed = []
        for t in range(n):
            for j in range(3):
                copy(t, j, chips[j], c, sib, False).wait_recv()
                fwd = copy(t, 3 + j, chips[j], c, sib, False)
                fwd.start()
                passed.append(fwd)
        for t in range(n):
            for j in range(3):
                copy(t, 3 + j, chips[j], 1 - c, sib, False).wait_recv()
        for cp in first + passed:
            cp.wait_send()
        for cp in mine:
            cp.wait()

    return pl.pallas_call(
        body, name="gather_weights", in_specs=[ANY] * n, out_specs=[ANY] * n,
        out_shape=[jax.ShapeDtypeStruct((s.shape[0], N_CHIPS) + s.shape[1:], s.dtype) for s in shards],
        scratch_shapes=[pltpu.SemaphoreType.DMA((6 * n,)), pltpu.SemaphoreType.DMA((6 * n,)),
                        pltpu.SemaphoreType.DMA((n,))])(*shards)


def _allreduce_small(part):
    M, C = part.shape
    n_dev = 2 * N_CHIPS

    def body(x_ref, sum_ref, all_ref, send, recv, local):
        x, y, c, chips = _position()
        me, sib = (x, y, c), (x, y, 1 - c)

        def rows(px, py, pc):
            return all_ref.at[pl.ds((4 * px + 2 * py + pc) * M, M), :]

        def copy(k, block, to, src=None):
            return pltpu.make_async_remote_copy(
                src_ref=rows(*block) if src is None else src, dst_ref=rows(*block),
                send_sem=send.at[k], recv_sem=recv.at[k], device_id=to, device_id_type=MESH)

        mine = pltpu.make_async_copy(x_ref, rows(*me), local)
        mine.start()
        first = [copy(0, me, sib, src=x_ref)] + [copy(1 + j, me, (*chip, c), src=x_ref) for j, chip in enumerate(chips)]
        for cp in first:
            cp.start()
        passed = [copy(4 + j, (*chip, c), sib) for j, chip in enumerate(chips)]
        for j, chip in enumerate(chips):
            copy(1 + j, (*chip, c), me).wait_recv()
            passed[j].start()
        copy(0, sib, me).wait_recv()
        for j, chip in enumerate(chips):
            copy(4 + j, (*chip, 1 - c), me).wait_recv()
        for cp in first + passed:
            cp.wait_send()
        mine.wait()
        acc = all_ref[0:M, :]
        for d in range(1, n_dev):
            acc = acc + all_ref[d * M:(d + 1) * M, :]
        sum_ref[...] = acc

    vm = pl.BlockSpec(memory_space=pltpu.VMEM)
    return pl.pallas_call(
        body, name="allreduce_small", in_specs=[vm], out_specs=[vm],
        out_shape=[jax.ShapeDtypeStruct((M, C), F32)],
        scratch_shapes=[pltpu.VMEM((n_dev * M, C), F32), pltpu.SemaphoreType.DMA((7,)),
                        pltpu.SemaphoreType.DMA((7,)), pltpu.SemaphoreType.DMA],
        compiler_params=pltpu.CompilerParams(vmem_limit_bytes=VMEM_LIMIT))(part)[0]


def _rs_sibling(grads):
    n = len(grads)

    def body(*refs):
        src, dst, send, recv = refs[:n], refs[n:2 * n], refs[2 * n], refs[2 * n + 1]
        x, y, c, _ = _position()
        cps = [pltpu.make_async_remote_copy(
            src_ref=src[t].at[:, 1 - c], dst_ref=dst[t], send_sem=send.at[t], recv_sem=recv.at[t],
            device_id=(x, y, 1 - c), device_id_type=MESH) for t in range(n)]
        for cp in cps:
            cp.start()
        for cp in cps:
            cp.wait()

    return pl.pallas_call(
        body, name="rs_sibling", in_specs=[ANY] * n, out_specs=[ANY] * n,
        out_shape=[jax.ShapeDtypeStruct((g.shape[0],) + g.shape[2:], BF16) for g in grads],
        scratch_shapes=[pltpu.SemaphoreType.DMA((n,)), pltpu.SemaphoreType.DMA((n,))])(*grads)


def _add_sibling(grad, got, c_idx):
    _, _, R, C = grad.shape
    tr = _pick(R, (512, 256, 128, 64, 32, 16))

    def body(c_ref, g_ref, r_ref, o_ref):
        o_ref[...] = (g_ref[...].astype(F32) + r_ref[...].astype(F32)).astype(BF16)

    return pl.pallas_call(
        body, name="add_sibling",
        grid_spec=pltpu.PrefetchScalarGridSpec(
            num_scalar_prefetch=1, grid=(N_CHIPS, R // tr),
            in_specs=[pl.BlockSpec((None, None, tr, C), lambda p, i, c: (p, c[0], i, 0)),
                      pl.BlockSpec((None, tr, C), lambda p, i, c: (p, i, 0))],
            out_specs=pl.BlockSpec((None, tr, C), lambda p, i, c: (p, i, 0))),
        out_shape=jax.ShapeDtypeStruct((N_CHIPS, R, C), BF16),
        compiler_params=_cp("arbitrary", "arbitrary"))(c_idx, grad, got)


def _rs_chips(sums):
    n = len(sums)

    def body(*refs):
        src, dst, send, recv = refs[:n], refs[n:2 * n], refs[2 * n], refs[2 * n + 1]
        x, y, c, chips = _position()
        cps = [pltpu.make_async_remote_copy(
            src_ref=src[t].at[2 * chips[j][0] + chips[j][1]], dst_ref=dst[t].at[j],
            send_sem=send.at[3 * t + j], recv_sem=recv.at[3 * t + j],
            device_id=(*chips[j], c), device_id_type=MESH) for t in range(n) for j in range(3)]
        for cp in cps:
            cp.start()
        for cp in cps:
            cp.wait()

    return pl.pallas_call(
        body, name="rs_chips", in_specs=[ANY] * n, out_specs=[ANY] * n,
        out_shape=[jax.ShapeDtypeStruct((3,) + s.shape[1:], BF16) for s in sums],
        scratch_shapes=[pltpu.SemaphoreType.DMA((3 * n,)), pltpu.SemaphoreType.DMA((3 * n,))])(*sums)


def _add_chips(grad, got_sib, got_chips, pc_idx):
    _, _, R, C = grad.shape
    tr = _pick(R, (256, 128, 64, 32, 16))

    def body(pc_ref, g_ref, s_ref, r_ref, o_ref):
        acc = g_ref[...].astype(F32) + s_ref[...].astype(F32)
        for j in range(3):
            acc = acc + r_ref[j].astype(F32)
        o_ref[...] = acc

    return pl.pallas_call(
        body, name="add_chips",
        grid_spec=pltpu.PrefetchScalarGridSpec(
            num_scalar_prefetch=1, grid=(R // tr,),
            in_specs=[pl.BlockSpec((None, None, tr, C), lambda i, pc: (pc[0], pc[1], i, 0)),
                      pl.BlockSpec((None, tr, C), lambda i, pc: (pc[0], i, 0)),
                      pl.BlockSpec((3, tr, C), lambda i, pc: (0, i, 0))],
            out_specs=pl.BlockSpec((tr, C), lambda i, pc: (i, 0))),
        out_shape=jax.ShapeDtypeStruct((R, C), F32),
        compiler_params=_cp("arbitrary"))(pc_idx, grad, got_sib, got_chips)


def _ag_sibling(groups):
    flat = [(gi, l, p) for gi, grp in enumerate(groups) for l, p in enumerate(grp)]
    n = len(flat)
    ng = len(groups)

    def body(*refs):
        src, dst = refs[:n], refs[n:n + ng]
        send, recv, local = refs[n + ng], refs[n + ng + 1], refs[n + ng + 2]
        x, y, c, _ = _position()
        mine, cps = [], []
        for t, (gi, l, _) in enumerate(flat):
            mine.append(pltpu.make_async_copy(src[t], dst[gi].at[l, c], local.at[t]))
            cps.append(pltpu.make_async_remote_copy(
                src_ref=src[t], dst_ref=dst[gi].at[l, c], send_sem=send.at[t], recv_sem=recv.at[t],
                device_id=(x, y, 1 - c), device_id_type=MESH))
        for cp in mine + cps:
            cp.start()
        for t, (gi, l, _) in enumerate(flat):
            pltpu.make_async_remote_copy(
                src_ref=src[t], dst_ref=dst[gi].at[l, 1 - c], send_sem=send.at[t], recv_sem=recv.at[t],
                device_id=(x, y, 1 - c), device_id_type=MESH).wait_recv()
        for cp in cps:
            cp.wait_send()
        for cp in mine:
            cp.wait()

    return pl.pallas_call(
        body, name="ag_sibling", in_specs=[ANY] * n, out_specs=[ANY] * ng,
        out_shape=[jax.ShapeDtypeStruct((len(grp), 2) + grp[0].shape, F32) for grp in groups],
        scratch_shapes=[pltpu.SemaphoreType.DMA((n,)), pltpu.SemaphoreType.DMA((n,)),
                        pltpu.SemaphoreType.DMA((n,))])(*[p for _, _, p in flat])


def _split8(dw, blocked):
    if blocked:
        p, k, nq = dw.shape
        return dw.reshape(p, 2, k // 2, nq)
    k, n = dw.shape
    return dw.reshape(N_CHIPS, 2, k // (2 * N_CHIPS), n)


def kernel(x, a_w_qkv, a_w_o, a_q_gain, a_k_gain, b_w_qkv, b_w_o, rel_bias, mix_norm, ffn_norm, w_up, conv_w, conv_b, w_down, final_norm, loss_target, m_a_w_qkv, m_a_w_o, m_a_q_gain, m_a_k_gain, m_b_w_qkv, m_b_w_o, m_rel_bias, m_mix_norm, m_ffn_norm, m_w_up, m_conv_w, m_conv_b, m_w_down, m_final_norm, v_a_w_qkv, v_a_w_o, v_a_q_gain, v_a_k_gain, v_b_w_qkv, v_b_w_o, v_rel_bias, v_mix_norm, v_ffn_norm, v_w_up, v_conv_w, v_conv_b, v_w_down, v_final_norm):
    S, D = x.shape[1], x.shape[2]
    h = x.reshape(S, D)
    target = loss_target.reshape(S, D)
    hg = B_HEADS_PER_GROUP
    G = len(B_GROUPS)
    n_a, n_b = a_w_qkv.shape[0], b_w_qkv.shape[0]
    depth = w_up.shape[0]
    cx, cy, cc = lax.axis_index("x"), lax.axis_index("y"), lax.axis_index("c")
    c_idx = jnp.reshape(cc, (1,)).astype(jnp.int32)
    pc_idx = jnp.stack([2 * cx + cy, cc]).astype(jnp.int32)

    big = dict(a_w_qkv=a_w_qkv, a_w_o=a_w_o, b_w_qkv=b_w_qkv, b_w_o=b_w_o, w_up=w_up, w_down=w_down)
    blocked = dict(a_w_qkv=True, a_w_o=False, b_w_qkv=True, b_w_o=False, w_up=True, w_down=False)
    names = list(big)
    shards16 = [_cast_bf16(big[k].reshape(-1, big[k].shape[-1])).reshape(big[k].shape) for k in names]
    gathered = _gather_weights(shards16 + [conv_w])
    full = {}
    for k, gth in zip(names, gathered[:-1]):
        L, _, a, b = gth.shape
        full[k] = gth if blocked[k] else gth.reshape(L, N_CHIPS * a, b)
    cw_full = gathered[-1]
    cb3 = conv_b.reshape(depth, 1, conv_b.shape[1])

    cos, sin = _rope_tables(S)
    buckets = jnp.asarray(_bucket_tables())
    bias = _bias_build(rel_bias, buckets)

    saved = []
    for i in range(depth):
        j = i // 2
        sv = dict(h0=h)
        hn = _rms_fwd(h, mix_norm[i:i + 1])
        sv["hn"] = hn
        if i % 2 == 0:
            qkv = _mm_nn(hn, full["a_w_qkv"], j, blocked=True, name="a_qkv")
            qkvh = _prep_a_fwd(qkv, cos, sin, a_q_gain[j:j + 1], a_k_gain[j:j + 1])
            o, lse = _flash_a_fwd(qkvh)
            sv.update(qkv=qkv, qkvh=qkvh, o=o, lse=lse)
            h = _mm_nn(o, full["a_w_o"], j, blocked=False, res=h, name="a_out")
        else:
            qkv = _mm_nn(hn, full["b_w_qkv"], j, blocked=True, name="b_qkv")
            os_, lzs = [], []
            for g in range(G):
                o_g, lz_g = _battn_fwd(qkv, bias, g)
                os_.append(o_g)
                lzs.append(lz_g)
            y = _combine_fwd(os_, lzs)
            sv.update(qkv=qkv, os=os_, lzs=lzs, y=y)
            h = _mm_nn(y, full["b_w_o"], j, blocked=False, res=h, name="b_out")
        sv["h1"] = h
        hf = _rms_fwd(h, ffn_norm[i:i + 1])
        u = _mm_nn(hf, full["w_up"], i, blocked=True, name="ffn_up")
        act = _ffn_act_fwd(u, cw_full, cb3, i)
        sv.update(hf=hf, u=u, act=act)
        h = _mm_nn(act, full["w_down"], i, blocked=False, res=h, name="ffn_down")
        saved.append(sv)

    loss_blk, dh, dh_b, dg_final = _final_loss(h, final_norm.reshape(1, D), target)

    dws = {k: [None] * big[k].shape[0] for k in names}
    d_mix, d_ffn, d_convw, d_convb = [None] * depth, [None] * depth, [None] * depth, [None] * depth
    d_gq, d_gk = [None] * n_a, [None] * n_a
    dbias_list = []
    for i in reversed(range(depth)):
        j = i // 2
        sv = saved[i]
        da = _mm_nt(dh_b, full["w_down"], i, blocked=False, name="ffn_down_dx")
        dws["w_down"][i] = _mm_tn(sv["act"], dh_b, blocked=False, name="ffn_down_dw")
        du, dconv = _ffn_act_bwd(sv["u"], da, cw_full, cb3, i)
        d_convw[i], d_convb[i] = dconv[0:3], dconv[3]
        dhf = _mm_nt(du, full["w_up"], i, blocked=True, name="ffn_up_dx")
        dws["w_up"][i] = _mm_tn(sv["hf"], du, blocked=True, name="ffn_up_dw")
        dh, dh_b, dg = _rms_bwd(dhf, sv["h1"], ffn_norm[i:i + 1], dh)
        d_ffn[i] = dg[0]
        if i % 2 == 0:
            do = _mm_nt(dh_b, full["a_w_o"], j, blocked=False, name="a_out_dx")
            dws["a_w_o"][j] = _mm_tn(sv["o"], dh_b, blocked=False, name="a_out_dw")
            dq, dk, dv = _flash_a_bwd(sv["qkvh"], do, sv["o"], sv["lse"])
            dqkv, dgain = _prep_a_bwd(dq, dk, dv, sv["qkv"], cos, sin, a_q_gain[j:j + 1], a_k_gain[j:j + 1])
            d_gq[j], d_gk[j] = dgain[0], dgain[1]
            dhn = _mm_nt(dqkv, full["a_w_qkv"], j, blocked=True, name="a_qkv_dx")
            dws["a_w_qkv"][j] = _mm_tn(sv["hn"], dqkv, blocked=True, name="a_qkv_dw")
        else:
            dy = _mm_nt(dh_b, full["b_w_o"], j, blocked=False, name="b_out_dx")
            dws["b_w_o"][j] = _mm_tn(sv["y"], dh_b, blocked=False, name="b_out_dw")
            dos, dlzs = _combine_bwd(dy, sv["os"], sv["lzs"])
            parts = []
            for g in range(G):
                dq, dk, dv, db = _battn_bwd(sv["qkv"], bias, dos[g], sv["os"][g], sv["lzs"][g], dlzs[g], g)
                parts += [dq, dk, dv]
                dbias_list.append((g, db))
            dqkv = _concat_cast(parts)
            dhn = _mm_nt(dqkv, full["b_w_qkv"], j, blocked=True, name="b_qkv_dx")
            dws["b_w_qkv"][j] = _mm_tn(sv["hn"], dqkv, blocked=True, name="b_qkv_dw")
        dh, dh_b, dg = _rms_bwd(dhn, sv["h0"], mix_norm[i:i + 1], dh)
        d_mix[i] = dg[0]
    grad_x = dh.reshape(x.shape)

    dbias_layers = [jnp.stack([db for g2, db in dbias_list[l * G:(l + 1) * G]]) for l in range(n_b)]
    d_rel = _bias_reduce(dbias_layers, buckets)[:, :G * hg]

    small = [jnp.stack(d_gq), jnp.stack(d_gk), d_rel, jnp.stack(d_mix), jnp.stack(d_ffn), jnp.stack(d_convw),
             jnp.stack(d_convb), dg_final[0]]
    sizes = [int(np.prod(s.shape)) for s in small]
    flat = jnp.concatenate([s.reshape(-1) for s in small])
    rows = -(-flat.shape[0] // (LANES * SUBLANES)) * SUBLANES
    flat = jnp.pad(flat, (0, rows * LANES - flat.shape[0])).reshape(rows, LANES)
    tot = _allreduce_small(flat).reshape(-1)
    offs = np.cumsum([0] + sizes)
    g_gq, g_gk, g_rel, g_mix, g_ffn, g_convw_full, g_convb, g_final = [
        tot[offs[k]:offs[k + 1]].reshape(small[k].shape) for k in range(len(small))]
    cq = conv_w.shape[2]
    g_convw = lax.dynamic_slice_in_dim(g_convw_full, (2 * cx + cy) * cq, cq, axis=2)

    order = [(k, l) for k in names for l in range(big[k].shape[0])]
    pieces = [_split8(dws[k][l], blocked[k]) for k, l in order]
    got_sib = _rs_sibling(pieces)
    chip_sums = [_add_sibling(p, r, c_idx) for p, r in zip(pieces, got_sib)]
    got_chips = _rs_chips(chip_sums)
    reduced = [_add_chips(p, r, rc, pc_idx) for p, r, rc in zip(pieces, got_sib, got_chips)]
    groups, at = [], 0
    for k in names:
        groups.append(reduced[at:at + big[k].shape[0]])
        at += big[k].shape[0]
    shard_grads = {k: gs.reshape(big[k].shape) for k, gs in zip(names, _ag_sibling(groups))}

    grads = dict(shard_grads, a_q_gain=g_gq, a_k_gain=g_gk, rel_bias=g_rel, mix_norm=g_mix, ffn_norm=g_ffn,
                 conv_w=g_convw, conv_b=g_convb, final_norm=g_final)
    weights = dict(a_w_qkv=a_w_qkv, a_w_o=a_w_o, a_q_gain=a_q_gain, a_k_gain=a_k_gain, b_w_qkv=b_w_qkv, b_w_o=b_w_o,
                   rel_bias=rel_bias, mix_norm=mix_norm, ffn_norm=ffn_norm, w_up=w_up, conv_w=conv_w, conv_b=conv_b,
                   w_down=w_down, final_norm=final_norm)
    ms = dict(a_w_qkv=m_a_w_qkv, a_w_o=m_a_w_o, a_q_gain=m_a_q_gain, a_k_gain=m_a_k_gain, b_w_qkv=m_b_w_qkv,
              b_w_o=m_b_w_o, rel_bias=m_rel_bias, mix_norm=m_mix_norm, ffn_norm=m_ffn_norm, w_up=m_w_up,
              conv_w=m_conv_w, conv_b=m_conv_b, w_down=m_w_down, final_norm=m_final_norm)
    vs = dict(a_w_qkv=v_a_w_qkv, a_w_o=v_a_w_o, a_q_gain=v_a_q_gain, a_k_gain=v_a_k_gain, b_w_qkv=v_b_w_qkv,
              b_w_o=v_b_w_o, rel_bias=v_rel_bias, mix_norm=v_mix_norm, ffn_norm=v_ffn_norm, w_up=v_w_up,
              conv_w=v_conv_w, conv_b=v_conv_b, w_down=v_w_down, final_norm=v_final_norm)
    deltas, new_m, new_v = {}, {}, {}
    for k, w in weights.items():
        two_d = (-1, w.shape[-1])
        d, nm, nv = _adamw(w.reshape(two_d), grads[k].reshape(two_d), ms[k].reshape(two_d), vs[k].reshape(two_d))
        deltas[k], new_m[k], new_v[k] = d.reshape(w.shape), nm.reshape(w.shape), nv.reshape(w.shape)

    loss = lax.psum(loss_blk[0, 0], ("x", "y", "c"))
    keys = list(weights)
    return (loss, grad_x, *[grads[k].reshape(weights[k].shape) for k in keys], *[deltas[k] for k in keys],
            *[new_m[k] for k in keys], *[new_v[k] for k in keys])
```

```python
import functools
import math

import numpy as np
import jax
import jax.numpy as jnp
from jax import lax
from jax.experimental import pallas as pl
from jax.experimental.pallas import tpu as pltpu

F32 = jnp.float32
BF16 = jnp.bfloat16

HEAD_DIM = 128
A_HEADS = 16
A_KV_HEADS = 4
GRID_W = 64
ROPE_THETA = 10000.0
B_GROUPS = ((128, 1), (512, 4), (2048, 16))
B_HEADS_PER_GROUP = 8
REL_BUCKETS = 32
REL_MAX_DISTANCE = 1024
EPS = 1e-6
NEG_INF = -1e30
DEPTH = 4
ADAM_LR = 0.001
ADAM_B1 = 0.9
ADAM_B2 = 0.999
ADAM_EPS = 1e-08
ADAM_WD = 0.01
ADAM_STEP = 10

N_CHIPS = 4
LANES = 128
SUBLANES = 8
VMEM_LIMIT = 52 * 1024 * 1024
MESH = pl.DeviceIdType.MESH


def _pick(n, cands):
    for c in cands:
        if c <= n and n % c == 0:
            return c
    return n


def _cp(*sem):
    return pltpu.CompilerParams(dimension_semantics=sem if sem else None, vmem_limit_bytes=VMEM_LIMIT)


def _half_span():
    hs = {w // (2 * d) for w, d in B_GROUPS}
    assert len(hs) == 1
    return hs.pop()


def _rms_fwd(h, gain):
    S, D = h.shape
    ts = _pick(S, (512, 256, 128, 64, 32, 16))

    def body(h_ref, g_ref, o_ref):
        x = h_ref[...]
        r = lax.rsqrt(jnp.mean(x * x, axis=-1, keepdims=True) + EPS)
        o_ref[...] = (x * r * g_ref[...]).astype(o_ref.dtype)

    return pl.pallas_call(
        body, name="rms_fwd", grid=(S // ts,),
        in_specs=[pl.BlockSpec((ts, D), lambda i: (i, 0)), pl.BlockSpec((1, D), lambda i: (0, 0))],
        out_specs=pl.BlockSpec((ts, D), lambda i: (i, 0)),
        out_shape=jax.ShapeDtypeStruct((S, D), BF16), compiler_params=_cp("arbitrary"))(h, gain)


def _rms_bwd(dy, h, gain, dres):
    S, D = h.shape
    ts = _pick(S, (256, 128, 64, 32, 16))

    def body(dy_ref, h_ref, g_ref, dres_ref, dh_ref, dhb_ref, dg_ref):
        @pl.when(pl.program_id(0) == 0)
        def _():
            dg_ref[...] = jnp.zeros_like(dg_ref)
        x = h_ref[...]
        dy = dy_ref[...]
        r = lax.rsqrt(jnp.mean(x * x, axis=-1, keepdims=True) + EPS)
        xn = x * r
        dg_ref[0:1, :] += jnp.sum(dy * xn, axis=0, keepdims=True)
        dxn = dy * g_ref[...]
        dx = r * (dxn - xn * jnp.mean(dxn * xn, axis=-1, keepdims=True))
        dh = dres_ref[...] + dx
        dh_ref[...] = dh
        dhb_ref[...] = dh.astype(BF16)

    row = pl.BlockSpec((ts, D), lambda i: (i, 0))
    return pl.pallas_call(
        body, name="rms_bwd", grid=(S // ts,),
        in_specs=[row, row, pl.BlockSpec((1, D), lambda i: (0, 0)), row],
        out_specs=[row, row, pl.BlockSpec((SUBLANES, D), lambda i: (0, 0))],
        out_shape=[jax.ShapeDtypeStruct((S, D), F32), jax.ShapeDtypeStruct((S, D), BF16),
                   jax.ShapeDtypeStruct((SUBLANES, D), F32)],
        compiler_params=_cp("arbitrary"))(dy, h, gain, dres)


def _final_loss(h, gain, target):
    S, D = h.shape
    ts = _pick(S, (256, 128, 64, 32, 16))

    def body(h_ref, g_ref, t_ref, loss_ref, dh_ref, dhb_ref, dg_ref):
        @pl.when(pl.program_id(0) == 0)
        def _():
            dg_ref[...] = jnp.zeros_like(dg_ref)
            loss_ref[...] = jnp.zeros_like(loss_ref)
        x = h_ref[...]
        g = g_ref[...]
        r = lax.rsqrt(jnp.mean(x * x, axis=-1, keepdims=True) + EPS)
        xn = x * r
        err = xn * g - t_ref[...]
        part = 0.5 * jnp.sum(jnp.mean(err * err, axis=-1, keepdims=True), axis=0, keepdims=True)
        loss_ref[0:1, 0:1] += part
        dy = err * (1.0 / D)
        dg_ref[0:1, :] += jnp.sum(dy * xn, axis=0, keepdims=True)
        dxn = dy * g
        dh = r * (dxn - xn * jnp.mean(dxn * xn, axis=-1, keepdims=True))
        dh_ref[...] = dh
        dhb_ref[...] = dh.astype(BF16)

    row = pl.BlockSpec((ts, D), lambda i: (i, 0))
    return pl.pallas_call(
        body, name="final_loss", grid=(S // ts,),
        in_specs=[row, pl.BlockSpec((1, D), lambda i: (0, 0)), row],
        out_specs=[pl.BlockSpec((SUBLANES, LANES), lambda i: (0, 0)), row, row,
                   pl.BlockSpec((SUBLANES, D), lambda i: (0, 0))],
        out_shape=[jax.ShapeDtypeStruct((SUBLANES, LANES), F32), jax.ShapeDtypeStruct((S, D), F32),
                   jax.ShapeDtypeStruct((S, D), BF16), jax.ShapeDtypeStruct((SUBLANES, D), F32)],
        compiler_params=_cp("arbitrary"))(h, gain, target)


_NN = (((1,), (0,)), ((), ()))
_NT = (((1,), (1,)), ((), ()))
_TN = (((0,), (0,)), ((), ()))


def _mm_nn(a, w, layer, *, blocked, out_dtype=F32, res=None, name):
    M, K = a.shape
    if blocked:
        nq = w.shape[3]
        N = N_CHIPS * nq
        tn = _pick(nq, (256, 128))
        nps = nq // tn
        w_spec = pl.BlockSpec((None, None, K, tn), lambda i, j: (layer, j // nps, 0, j % nps))
    else:
        N = w.shape[2]
        tn = _pick(N, (256, 128))
        w_spec = pl.BlockSpec((None, K, tn), lambda i, j: (layer, 0, j))
    tm = _pick(M, (1024, 512, 256, 128, 64, 32, 16)) if K <= 3072 else _pick(M, (512, 256, 128, 64, 32, 16))

    def body(*refs):
        if res is None:
            a_ref, w_ref, o_ref = refs
            acc = lax.dot_general(a_ref[...], w_ref[...], _NN, preferred_element_type=F32)
        else:
            a_ref, w_ref, r_ref, o_ref = refs
            acc = r_ref[...] + lax.dot_general(a_ref[...], w_ref[...], _NN, preferred_element_type=F32)
        o_ref[...] = acc.astype(o_ref.dtype)

    in_specs = [pl.BlockSpec((tm, K), lambda i, j: (i, 0)), w_spec]
    args = [a, w]
    if res is not None:
        in_specs.append(pl.BlockSpec((tm, tn), lambda i, j: (i, j)))
        args.append(res)
    return pl.pallas_call(
        body, name=name, grid=(M // tm, N // tn), in_specs=in_specs,
        out_specs=pl.BlockSpec((tm, tn), lambda i, j: (i, j)),
        out_shape=jax.ShapeDtypeStruct((M, N), out_dtype),
        compiler_params=_cp("arbitrary", "arbitrary"))(*args)


def _mm_nt(a, w, layer, *, blocked, name):
    pair = isinstance(a, tuple)
    M = a[0].shape[0] if pair else a.shape[0]
    tm = _pick(M, (1024, 512, 256, 128, 64, 32, 16))
    if blocked:
        K, nq = w.shape[2], w.shape[3]
        tk = _pick(K, (1024, 512, 256, 128))
        half = N_CHIPS // 2

        def body(*refs):
            a_refs, (w_ref, o_ref, acc_ref) = refs[:-3], refs[-3:]
            p = pl.program_id(2)

            @pl.when(p == 0)
            def _():
                acc_ref[...] = jnp.zeros_like(acc_ref)
            if pair:
                @pl.when(p < half)
                def _():
                    acc_ref[...] += lax.dot_general(a_refs[0][...], w_ref[...], _NT, preferred_element_type=F32)

                @pl.when(p >= half)
                def _():
                    acc_ref[...] += lax.dot_general(a_refs[1][...], w_ref[...], _NT, preferred_element_type=F32)
            else:
                acc_ref[...] += lax.dot_general(a_refs[0][...], w_ref[...], _NT, preferred_element_type=F32)

            @pl.when(p == N_CHIPS - 1)
            def _():
                o_ref[...] = acc_ref[...]

        if pair:
            a_specs = [pl.BlockSpec((tm, nq), lambda i, j, p: (i, jnp.minimum(p, half - 1))),
                       pl.BlockSpec((tm, nq), lambda i, j, p: (i, jnp.maximum(p - half, 0)))]
            a_args = list(a)
        else:
            a_specs = [pl.BlockSpec((tm, nq), lambda i, j, p: (i, p))]
            a_args = [a]
        return pl.pallas_call(
            body, name=name, grid=(M // tm, K // tk, N_CHIPS),
            in_specs=a_specs + [pl.BlockSpec((None, None, tk, nq), lambda i, j, p: (layer, p, j, 0))],
            out_specs=pl.BlockSpec((tm, tk), lambda i, j, p: (i, j)),
            out_shape=jax.ShapeDtypeStruct((M, K), F32),
            scratch_shapes=[pltpu.VMEM((tm, tk), F32)],
            compiler_params=_cp("arbitrary", "arbitrary", "arbitrary"))(*a_args, w)
    K, N = w.shape[1], w.shape[2]
    tk = _pick(K, (512, 256, 128))

    def body(a_ref, w_ref, o_ref):
        o_ref[...] = lax.dot_general(a_ref[...], w_ref[...], _NT, preferred_element_type=F32)

    return pl.pallas_call(
        body, name=name, grid=(M // tm, K // tk),
        in_specs=[pl.BlockSpec((tm, N), lambda i, j: (i, 0)),
                  pl.BlockSpec((None, tk, N), lambda i, j: (layer, j, 0))],
        out_specs=pl.BlockSpec((tm, tk), lambda i, j: (i, j)),
        out_shape=jax.ShapeDtypeStruct((M, K), F32),
        compiler_params=_cp("arbitrary", "arbitrary"))(a, w)


def _mm_tn(x, dy, *, blocked, name):
    pair = isinstance(dy, tuple)
    S, K = x.shape
    N = 2 * dy[0].shape[1] if pair else dy.shape[1]
    tk = _pick(K, (512, 256, 128))
    if blocked:
        nq = N // N_CHIPS
        tn = _pick(nq, (256, 128))
        nps = nq // tn
        out_spec = pl.BlockSpec((None, tk, tn), lambda i, j: (j // nps, i, j % nps))
        out_shape = jax.ShapeDtypeStruct((N_CHIPS, K, nq), BF16)
    else:
        tn = _pick(N, (512, 256, 128))
        out_spec = pl.BlockSpec((tk, tn), lambda i, j: (i, j))
        out_shape = jax.ShapeDtypeStruct((K, N), BF16)
    nj = N // tn
    njh = nj // 2

    def body(x_ref, *refs):
        o_ref = refs[-1]
        if pair:
            j = pl.program_id(1)

            @pl.when(j < njh)
            def _():
                o_ref[...] = lax.dot_general(x_ref[...], refs[0][...], _TN, preferred_element_type=F32).astype(o_ref.dtype)

            @pl.when(j >= njh)
            def _():
                o_ref[...] = lax.dot_general(x_ref[...], refs[1][...], _TN, preferred_element_type=F32).astype(o_ref.dtype)
        else:
            o_ref[...] = lax.dot_general(x_ref[...], refs[0][...], _TN, preferred_element_type=F32).astype(o_ref.dtype)

    if pair:
        assert nj % 2 == 0
        dy_specs = [pl.BlockSpec((S, tn), lambda i, j: (0, jnp.minimum(j, njh - 1))),
                    pl.BlockSpec((S, tn), lambda i, j: (0, jnp.maximum(j - njh, 0)))]
        dy_args = list(dy)
    else:
        dy_specs = [pl.BlockSpec((S, tn), lambda i, j: (0, j))]
        dy_args = [dy]
    return pl.pallas_call(
        body, name=name, grid=(K // tk, nj),
        in_specs=[pl.BlockSpec((S, tk), lambda i, j: (0, i))] + dy_specs,
        out_specs=out_spec, out_shape=out_shape,
        compiler_params=_cp("arbitrary", "arbitrary"))(x, *dy_args)


def _rope_tables(S):
    rows = S // GRID_W
    row_ids = jnp.repeat(jnp.arange(rows, dtype=F32), GRID_W)
    col_ids = jnp.tile(jnp.arange(GRID_W, dtype=F32), rows)
    quarter = HEAD_DIM // 4
    inv_freq = ROPE_THETA ** (-jnp.arange(quarter, dtype=F32) / quarter)
    ang_r = row_ids[:, None] * inv_freq[None, :]
    ang_c = col_ids[:, None] * inv_freq[None, :]
    cos = jnp.concatenate([jnp.cos(ang_r)] * 2 + [jnp.cos(ang_c)] * 2, axis=-1)
    sin = jnp.concatenate([-jnp.sin(ang_r), jnp.sin(ang_r), -jnp.sin(ang_c), jnp.sin(ang_c)], axis=-1)
    return cos, sin


def _swap_quarters(x):
    lane = lax.broadcasted_iota(jnp.int32, x.shape, 1)
    first = (lane % (HEAD_DIM // 2)) < (HEAD_DIM // 4)
    return jnp.where(first, pltpu.roll(x, HEAD_DIM - HEAD_DIM // 4, 1), pltpu.roll(x, HEAD_DIM // 4, 1))


def _prep_a_fwd(qkv, cos, sin, gq, gk):
    S, W = qkv.shape
    nrm = A_HEADS + A_KV_HEADS
    ts = _pick(S, (256, 128, 64, 32, 16))

    def body(qkv_ref, cos_ref, sin_ref, gq_ref, gk_ref, o_ref):
        cos_t = cos_ref[...]
        sin_t = sin_ref[...]
        for j in range(nrm):
            sl = slice(j * HEAD_DIM, (j + 1) * HEAD_DIM)
            x = qkv_ref[:, sl]
            g = gq_ref[...] if j < A_HEADS else gk_ref[...]
            r = lax.rsqrt(jnp.mean(x * x, axis=-1, keepdims=True) + EPS)
            n = x * r * g
            o_ref[:, sl] = (n * cos_t + _swap_quarters(n) * sin_t).astype(BF16)
        o_ref[:, nrm * HEAD_DIM:] = qkv_ref[:, nrm * HEAD_DIM:].astype(BF16)

    row = lambda w: pl.BlockSpec((ts, w), lambda i: (i, 0))
    one = pl.BlockSpec((1, HEAD_DIM), lambda i: (0, 0))
    return pl.pallas_call(
        body, name="prep_a_fwd", grid=(S // ts,),
        in_specs=[row(W), row(HEAD_DIM), row(HEAD_DIM), one, one], out_specs=row(W),
        out_shape=jax.ShapeDtypeStruct((S, W), BF16), compiler_params=_cp("arbitrary"))(qkv, cos, sin, gq, gk)


def _prep_a_bwd(dq, dk, dv, qkv, cos, sin, gq, gk):
    S, W = qkv.shape
    nrm = A_HEADS + A_KV_HEADS
    nq, nk = A_HEADS * HEAD_DIM, A_KV_HEADS * HEAD_DIM
    ts = _pick(S, (256, 128, 64, 32, 16))

    def body(dq_ref, dk_ref, dv_ref, qkv_ref, cos_ref, sin_ref, gq_ref, gk_ref, o_ref, dg_ref):
        @pl.when(pl.program_id(0) == 0)
        def _():
            dg_ref[...] = jnp.zeros_like(dg_ref)
        cos_t = cos_ref[...]
        sin_t = sin_ref[...]
        for j in range(nrm):
            sl = slice(j * HEAD_DIM, (j + 1) * HEAD_DIM)
            x = qkv_ref[:, sl]
            if j < A_HEADS:
                dy, g, grow = dq_ref[:, sl], gq_ref[...], 0
            else:
                jj = j - A_HEADS
                dy, g, grow = dk_ref[:, jj * HEAD_DIM:(jj + 1) * HEAD_DIM], gk_ref[...], 1
            r = lax.rsqrt(jnp.mean(x * x, axis=-1, keepdims=True) + EPS)
            xn = x * r
            dn = dy * cos_t + _swap_quarters(dy * sin_t)
            dg_ref[grow:grow + 1, :] += jnp.sum(dn * xn, axis=0, keepdims=True)
            dxn = dn * g
            o_ref[:, sl] = (r * (dxn - xn * jnp.mean(dxn * xn, axis=-1, keepdims=True))).astype(BF16)
        o_ref[:, nrm * HEAD_DIM:] = dv_ref[...].astype(BF16)

    row = lambda w: pl.BlockSpec((ts, w), lambda i: (i, 0))
    one = pl.BlockSpec((1, HEAD_DIM), lambda i: (0, 0))
    return pl.pallas_call(
        body, name="prep_a_bwd", grid=(S // ts,),
        in_specs=[row(nq), row(nk), row(nk), row(W), row(HEAD_DIM), row(HEAD_DIM), one, one],
        out_specs=[row(W), pl.BlockSpec((SUBLANES, HEAD_DIM), lambda i: (0, 0))],
        out_shape=[jax.ShapeDtypeStruct((S, W), BF16), jax.ShapeDtypeStruct((SUBLANES, HEAD_DIM), F32)],
        compiler_params=_cp("arbitrary"))(dq, dk, dv, qkv, cos, sin, gq, gk)


def _flash_a_fwd(qkvh):
    S = qkvh.shape[0]
    grp = A_HEADS // A_KV_HEADS
    tq = _pick(S, (256, 128, 64, 32, 16))
    scale = HEAD_DIM ** -0.5

    def body(q_ref, k_ref, v_ref, o_ref, lse_ref):
        s = lax.dot_general(q_ref[...], k_ref[...], _NT, preferred_element_type=F32) * scale
        m = jnp.max(s, axis=-1, keepdims=True)
        p = jnp.exp(s - m)
        l = jnp.sum(p, axis=-1, keepdims=True)
        pn = (p * (1.0 / l)).astype(BF16)
        o_ref[...] = lax.dot_general(pn, v_ref[...], _NN, preferred_element_type=F32).astype(BF16)
        lse_ref[...] = jnp.broadcast_to(m + jnp.log(l), lse_ref.shape)

    qs = pl.BlockSpec((tq, HEAD_DIM), lambda h, i: (i, h))
    return pl.pallas_call(
        body, name="flash_a_fwd", grid=(A_HEADS, S // tq),
        in_specs=[qs,
                  pl.BlockSpec((S, HEAD_DIM), lambda h, i: (0, A_HEADS + h // grp)),
                  pl.BlockSpec((S, HEAD_DIM), lambda h, i: (0, A_HEADS + A_KV_HEADS + h // grp))],
        out_specs=[qs, qs],
        out_shape=[jax.ShapeDtypeStruct((S, A_HEADS * HEAD_DIM), BF16),
                   jax.ShapeDtypeStruct((S, A_HEADS * HEAD_DIM), F32)],
        compiler_params=_cp("arbitrary", "arbitrary"))(qkvh, qkvh, qkvh)


def _flash_a_bwd(qkvh, do, o, lse):
    S = qkvh.shape[0]
    grp = A_HEADS // A_KV_HEADS
    tq = _pick(S, (256, 128, 64, 32, 16))
    scale = HEAD_DIM ** -0.5

    def body(q_ref, k_ref, v_ref, do_ref, o_ref, lse_ref, dq_ref, dk_ref, dv_ref):
        @pl.when((pl.program_id(1) == 0) & (pl.program_id(2) == 0))
        def _():
            dk_ref[...] = jnp.zeros_like(dk_ref)
            dv_ref[...] = jnp.zeros_like(dv_ref)
        q = q_ref[...]
        k = k_ref[...]
        do_f = do_ref[...]
        do_b = do_f.astype(BF16)
        s = lax.dot_general(q, k, _NT, preferred_element_type=F32) * scale
        p = jnp.exp(s - lse_ref[:, 0:1])
        dp = lax.dot_general(do_b, v_ref[...], _NT, preferred_element_type=F32)
        delta = jnp.sum(do_f * o_ref[...].astype(F32), axis=-1, keepdims=True)
        ds_b = (p * (dp - delta) * scale).astype(BF16)
        dq_ref[...] = lax.dot_general(ds_b, k, _NN, preferred_element_type=F32)
        dk_ref[...] += lax.dot_general(ds_b, q, _TN, preferred_element_type=F32)
        dv_ref[...] += lax.dot_general(p.astype(BF16), do_b, _TN, preferred_element_type=F32)

    qs = pl.BlockSpec((tq, HEAD_DIM), lambda kv, g, i: (i, kv * grp + g))
    kvs = lambda off: pl.BlockSpec((S, HEAD_DIM), lambda kv, g, i: (0, off + kv))
    return pl.pallas_call(
        body, name="flash_a_bwd", grid=(A_KV_HEADS, grp, S // tq),
        in_specs=[qs, kvs(A_HEADS), kvs(A_HEADS + A_KV_HEADS), qs, qs, qs],
        out_specs=[qs, kvs(0), kvs(0)],
        out_shape=[jax.ShapeDtypeStruct((S, A_HEADS * HEAD_DIM), F32),
                   jax.ShapeDtypeStruct((S, A_KV_HEADS * HEAD_DIM), F32),
                   jax.ShapeDtypeStruct((S, A_KV_HEADS * HEAD_DIM), F32)],
        compiler_params=_cp("arbitrary", "arbitrary", "arbitrary"))(qkvh, qkvh, qkvh, do, o, lse)


def _bucket_tables():
    hs = _half_span()
    tq, kv = 2 * hs, 4 * hs
    nb = REL_BUCKETS // 2
    max_exact = nb // 2
    out = np.zeros((len(B_GROUPS), 3, tq, kv), np.int32)
    for g, (_, dil) in enumerate(B_GROUPS):
        for case, off in enumerate((0, hs, 2 * hs)):
            rel = np.arange(kv)[None, :] - np.arange(tq)[:, None] - off
            r = rel * dil
            n = np.abs(r)
            nf = np.maximum(n, 1).astype(np.float32)
            large = max_exact + (np.log(nf / np.float32(max_exact)) / np.float32(math.log(REL_MAX_DISTANCE / max_exact))
                                 * np.float32(nb - max_exact)).astype(np.int32)
            large = np.minimum(large, nb - 1)
            bucket = np.where(r > 0, nb, 0) + np.where(n < max_exact, n, large)
            out[g, case] = np.where(np.abs(rel) <= hs, bucket, -1)
    return out


def _bias_build(rel_bias, buckets):
    G, _, tq, kv = buckets.shape
    hg = B_HEADS_PER_GROUP

    def body(rb_ref, bk_ref, o_ref):
        col = pl.program_id(0) * hg + pl.program_id(2)
        bk = bk_ref[...]
        acc = jnp.full((tq, kv), NEG_INF, F32)
        for b in range(REL_BUCKETS):
            acc = jnp.where(bk == b, rb_ref[b, col], acc)
        o_ref[...] = acc

    return pl.pallas_call(
        body, name="bias_build", grid=(G, 3, hg),
        in_specs=[pl.BlockSpec(memory_space=pltpu.SMEM),
                  pl.BlockSpec((None, None, tq, kv), lambda g, c, h: (g, c, 0, 0))],
        out_specs=pl.BlockSpec((None, None, None, tq, kv), lambda g, c, h: (g, c, h, 0, 0)),
        out_shape=jax.ShapeDtypeStruct((G, 3, hg, tq, kv), F32),
        compiler_params=_cp("arbitrary", "arbitrary", "arbitrary"))(rel_bias, buckets)


def _bias_reduce(dbias_list, buckets):
    G, _, tq, kv = buckets.shape
    hg = B_HEADS_PER_GROUP
    n = len(dbias_list)

    def body(*refs):
        bk_ref, o_ref = refs[n], refs[n + 1]
        first = (pl.program_id(0) == 0) & (pl.program_id(1) == 0) & (pl.program_id(2) == 0)

        @pl.when(first)
        def _():
            o_ref[...] = jnp.zeros_like(o_ref)
        col = pl.program_id(0) * hg + pl.program_id(2)
        db = refs[0][...]
        for r in refs[1:n]:
            db = db + r[...]
        bk = bk_ref[...]
        rows = lax.broadcasted_iota(jnp.int32, (REL_BUCKETS, LANES), 0)
        cols = lax.broadcasted_iota(jnp.int32, (REL_BUCKETS, LANES), 1)
        acc = jnp.zeros((REL_BUCKETS, LANES), F32)
        for b in range(REL_BUCKETS):
            val = jnp.sum(jnp.sum(jnp.where(bk == b, db, 0.0), axis=1, keepdims=True), axis=0, keepdims=True)
            acc = acc + jnp.where((rows == b) & (cols == col), val, 0.0)
        o_ref[...] += acc

    tile = pl.BlockSpec((None, None, None, tq, kv), lambda g, c, h: (g, c, h, 0, 0))
    return pl.pallas_call(
        body, name="bias_reduce", grid=(G, 3, hg),
        in_specs=[tile] * n + [pl.BlockSpec((None, None, tq, kv), lambda g, c, h: (g, c, 0, 0))],
        out_specs=pl.BlockSpec((REL_BUCKETS, LANES), lambda g, c, h: (0, 0)),
        out_shape=jax.ShapeDtypeStruct((REL_BUCKETS, LANES), F32),
        compiler_params=_cp("arbitrary", "arbitrary", "arbitrary"))(*dbias_list, buckets)


def _b_specs(g, dil, L):
    hs = _half_span()
    tq = 2 * hs
    hg = B_HEADS_PER_GROUP
    nq = L // tq
    base = g * 3 * hg

    def qkv_col(which):
        return lambda c, h, i: c * (len(B_GROUPS) * 3 * hg) + base + which * hg + h

    q_spec = pl.BlockSpec((tq, HEAD_DIM), lambda c, h, i: (i, qkv_col(0)(c, h, i)))
    k_spec = pl.BlockSpec((L, HEAD_DIM), lambda c, h, i: (0, qkv_col(1)(c, h, i)))
    v_spec = pl.BlockSpec((L, HEAD_DIM), lambda c, h, i: (0, qkv_col(2)(c, h, i)))
    case = lambda i: jnp.where(i == 0, 0, jnp.where(i == nq - 1, 2, 1))
    bias_spec = pl.BlockSpec((None, None, None, tq, 2 * tq), lambda c, h, i: (g, case(i), h, 0, 0))
    blk_spec = pl.BlockSpec((tq, HEAD_DIM), lambda c, h, i: (i, c * hg + h))
    full_spec = pl.BlockSpec((L, HEAD_DIM), lambda c, h, i: (0, c * hg + h))
    return q_spec, k_spec, v_spec, bias_spec, blk_spec, full_spec, nq


def _battn_fwd(qkv, bias, g):
    S, W = qkv.shape
    dil = B_GROUPS[g][1]
    hs = _half_span()
    tq, kvl = 2 * hs, 4 * hs
    hg = B_HEADS_PER_GROUP
    L = S // dil
    q_spec, k_spec, v_spec, bias_spec, blk_spec, _, nq = _b_specs(g, dil, L)
    scale = HEAD_DIM ** -0.5
    view = qkv.reshape(L, dil * W)

    def body(q_ref, k_ref, v_ref, b_ref, o_ref, lz_ref):
        ks = pl.multiple_of(jnp.clip(pl.program_id(2) * tq - hs, 0, L - kvl), hs)
        kw = k_ref[pl.ds(ks, kvl), :].astype(BF16)
        vw = v_ref[pl.ds(ks, kvl), :].astype(BF16)
        s = lax.dot_general(q_ref[...].astype(BF16), kw, _NT, preferred_element_type=F32) * scale + b_ref[...]
        m = jnp.max(s, axis=-1, keepdims=True)
        p = jnp.exp(s - m)
        l = jnp.sum(p, axis=-1, keepdims=True)
        o_ref[...] = lax.dot_general(p.astype(BF16), vw, _NN, preferred_element_type=F32) / l
        lz_ref[...] = jnp.broadcast_to(m + jnp.log(l), lz_ref.shape)

    o, lz = pl.pallas_call(
        body, name="battn_fwd_g%d" % g, grid=(dil, hg, nq),
        in_specs=[q_spec, k_spec, v_spec, bias_spec], out_specs=[blk_spec, blk_spec],
        out_shape=[jax.ShapeDtypeStruct((L, dil * hg * HEAD_DIM), F32)] * 2,
        compiler_params=_cp("arbitrary", "arbitrary", "arbitrary"))(view, view, view, bias)
    return o.reshape(S, hg * HEAD_DIM), lz.reshape(S, hg * HEAD_DIM)


def _battn_bwd(qkv, bias, do, o, lz, dlz, g):
    S, W = qkv.shape
    dil = B_GROUPS[g][1]
    hs = _half_span()
    tq, kvl = 2 * hs, 4 * hs
    hg = B_HEADS_PER_GROUP
    L = S // dil
    q_spec, k_spec, v_spec, bias_spec, blk_spec, full_spec, nq = _b_specs(g, dil, L)
    scale = HEAD_DIM ** -0.5
    view = qkv.reshape(L, dil * W)
    gv = lambda a: a.reshape(L, dil * hg * HEAD_DIM)

    def body(q_ref, k_ref, v_ref, b_ref, do_ref, o_ref, lz_ref, dlz_ref, dq_ref, dk_ref, dv_ref, db_ref):
        c, h, i = pl.program_id(0), pl.program_id(1), pl.program_id(2)

        @pl.when((c == 0) & (h == 0) & (i == 0))
        def _():
            db_ref[...] = jnp.zeros_like(db_ref)

        @pl.when(i == 0)
        def _():
            dk_ref[...] = jnp.zeros_like(dk_ref)
            dv_ref[...] = jnp.zeros_like(dv_ref)
        ks = pl.multiple_of(jnp.clip(i * tq - hs, 0, L - kvl), hs)
        case = jnp.where(i == 0, 0, jnp.where(i == nq - 1, 2, 1))
        q = q_ref[...].astype(BF16)
        kw = k_ref[pl.ds(ks, kvl), :].astype(BF16)
        vw = v_ref[pl.ds(ks, kvl), :].astype(BF16)
        do_f = do_ref[...]
        do_b = do_f.astype(BF16)
        s = lax.dot_general(q, kw, _NT, preferred_element_type=F32) * scale + b_ref[...]
        p = jnp.exp(s - lz_ref[:, 0:1])
        dp = lax.dot_general(do_b, vw, _NT, preferred_element_type=F32)
        delta = jnp.sum(do_f * o_ref[...], axis=-1, keepdims=True)
        ds = p * (dp - delta + dlz_ref[:, 0:1])
        db_ref[case, h] += ds
        ds_b = (ds * scale).astype(BF16)
        dq_ref[...] = lax.dot_general(ds_b, kw, _NN, preferred_element_type=F32)
        dk_ref[pl.ds(ks, kvl), :] += lax.dot_general(ds_b, q, _TN, preferred_element_type=F32)
        dv_ref[pl.ds(ks, kvl), :] += lax.dot_general(p.astype(BF16), do_b, _TN, preferred_element_type=F32)

    grp = jax.ShapeDtypeStruct((L, dil * hg * HEAD_DIM), F32)
    dq, dk, dv, db = pl.pallas_call(
        body, name="battn_bwd_g%d" % g, grid=(dil, hg, nq),
        in_specs=[q_spec, k_spec, v_spec, bias_spec, blk_spec, blk_spec, blk_spec, blk_spec],
        out_specs=[blk_spec, full_spec, full_spec,
                   pl.BlockSpec((3, hg, tq, kvl), lambda c, h, i: (0, 0, 0, 0))],
        out_shape=[grp, grp, grp, jax.ShapeDtypeStruct((3, hg, tq, kvl), F32)],
        compiler_params=_cp("arbitrary", "arbitrary", "arbitrary"))(
            view, view, view, bias, gv(do), gv(o), gv(lz), gv(dlz))
    r = lambda a: a.reshape(S, hg * HEAD_DIM)
    return r(dq), r(dk), r(dv), db


def _group_weights(lz_refs, sl):
    z = [r[:, sl] for r in lz_refs]
    mx = functools.reduce(jnp.maximum, z)
    e = [jnp.exp(v - mx) for v in z]
    inv = 1.0 / functools.reduce(lambda a, b: a + b, e)
    return [v * inv for v in e]


def _combine_fwd(os_, lzs):
    G = len(os_)
    S, Wg = os_[0].shape
    hg = B_HEADS_PER_GROUP
    ts = _pick(S, (256, 128, 64, 32, 16))

    def body(*refs):
        o_refs, lz_refs, y_ref = refs[:G], refs[G:2 * G], refs[2 * G]
        for h in range(hg):
            sl = slice(h * HEAD_DIM, (h + 1) * HEAD_DIM)
            w = _group_weights(lz_refs, sl)
            for g in range(G):
                y_ref[:, (g * hg + h) * HEAD_DIM:(g * hg + h + 1) * HEAD_DIM] = (w[g] * o_refs[g][:, sl]).astype(BF16)

    row = pl.BlockSpec((ts, Wg), lambda i: (i, 0))
    return pl.pallas_call(
        body, name="combine_fwd", grid=(S // ts,), in_specs=[row] * (2 * G),
        out_specs=pl.BlockSpec((ts, G * Wg), lambda i: (i, 0)),
        out_shape=jax.ShapeDtypeStruct((S, G * Wg), BF16), compiler_params=_cp("arbitrary"))(*os_, *lzs)


def _combine_bwd(dy, os_, lzs):
    G = len(os_)
    S, Wg = os_[0].shape
    hg = B_HEADS_PER_GROUP
    ts = _pick(S, (256, 128, 64, 32, 16))

    def body(*refs):
        dy_ref, o_refs, lz_refs = refs[0], refs[1:1 + G], refs[1 + G:1 + 2 * G]
        do_refs, dlz_refs = refs[1 + 2 * G:1 + 3 * G], refs[1 + 3 * G:1 + 4 * G]
        for h in range(hg):
            sl = slice(h * HEAD_DIM, (h + 1) * HEAD_DIM)
            w = _group_weights(lz_refs, sl)
            dw = []
            for g in range(G):
                dyg = dy_ref[:, (g * hg + h) * HEAD_DIM:(g * hg + h + 1) * HEAD_DIM]
                dw.append(jnp.sum(dyg * o_refs[g][:, sl], axis=-1, keepdims=True))
                do_refs[g][:, sl] = w[g] * dyg
            tot = functools.reduce(lambda a, b: a + b, [w[g] * dw[g] for g in range(G)])
            for g in range(G):
                dlz_refs[g][:, sl] = w[g] * (dw[g] - tot)

    row = pl.BlockSpec((ts, Wg), lambda i: (i, 0))
    outs = pl.pallas_call(
        body, name="combine_bwd", grid=(S // ts,),
        in_specs=[pl.BlockSpec((ts, G * Wg), lambda i: (i, 0))] + [row] * (2 * G),
        out_specs=[row] * (2 * G), out_shape=[jax.ShapeDtypeStruct((S, Wg), F32)] * (2 * G),
        compiler_params=_cp("arbitrary"))(dy, *os_, *lzs)
    return outs[:G], outs[G:]


def _concat_cast(parts):
    S = parts[0].shape[0]
    widths = [p.shape[1] for p in parts]
    ts = _pick(S, (256, 128, 64, 32, 16))

    def body(*refs):
        o_ref = refs[len(parts)]
        off = 0
        for r, w in zip(refs, widths):
            o_ref[:, off:off + w] = r[...].astype(BF16)
            off += w

    return pl.pallas_call(
        body, name="concat_cast", grid=(S // ts,),
        in_specs=[pl.BlockSpec((ts, w), lambda i: (i, 0)) for w in widths],
        out_specs=pl.BlockSpec((ts, sum(widths)), lambda i: (i, 0)),
        out_shape=jax.ShapeDtypeStruct((S, sum(widths)), BF16), compiler_params=_cp("arbitrary"))(*parts)


def _ffn_specs(S, dff, cq, ts, tc, layer, order):
    nfc = dff // tc
    nps = cq // tc
    hb = ts // SUBLANES
    nrow8 = S // SUBLANES

    def u_main(half):
        return pl.BlockSpec((ts, tc), lambda *g: (order(*g)[0], order(*g)[1] % nfc + half * nfc))

    def u_prev(half):
        return pl.BlockSpec((SUBLANES, tc), lambda *g: (jnp.maximum(order(*g)[0] * hb - 1, 0),
                                                         order(*g)[1] % nfc + half * nfc))

    def u_next(half):
        return pl.BlockSpec((SUBLANES, tc), lambda *g: (jnp.minimum((order(*g)[0] + 1) * hb, nrow8 - 1),
                                                         order(*g)[1] % nfc + half * nfc))

    def cw(half):
        def im(*g):
            jj = order(*g)[1] % nfc + half * nfc
            return (layer, jj // nps, 0, jj % nps)
        return pl.BlockSpec((None, None, 3, tc), im)

    def cb(half):
        return pl.BlockSpec((None, 1, tc), lambda *g: (layer, 0, order(*g)[1] % nfc + half * nfc))

    return nfc, u_main, u_prev, u_next, cw, cb


def _ffn_act_fwd(u, cw_full, cb3, layer):
    S, two_dff = u.shape
    dff = two_dff // 2
    cq = cw_full.shape[3]
    ts = _pick(S, (512, 256, 128, 64, 32, 16))
    tc = _pick(cq, (256, 128))
    order = lambda j, i: (i, j)
    nfc, u_main, u_prev, u_next, cw, cb = _ffn_specs(S, dff, cq, ts, tc, layer, order)
    nrow = S // ts

    def body(ug, ugp, ugn, uv, uvp, uvn, wg, wv, bg, bv, a_ref):
        i = pl.program_id(1)
        row = lax.broadcasted_iota(jnp.int32, (ts, tc), 0)

        def conv(x_ref, p_ref, n_ref, w_ref, b_ref):
            x = x_ref[...]
            prev = jnp.where(i > 0, p_ref[SUBLANES - 1:SUBLANES, :], 0.0)
            nxt = jnp.where(i < nrow - 1, n_ref[0:1, :], 0.0)
            xm = jnp.where(row == 0, prev, pltpu.roll(x, 1, 0))
            xp = jnp.where(row == ts - 1, nxt, pltpu.roll(x, ts - 1, 0))
            return w_ref[0:1, :] * xm + w_ref[1:2, :] * x + w_ref[2:3, :] * xp + b_ref[...]

        gc = conv(ug, ugp, ugn, wg, bg)
        vc = conv(uv, uvp, uvn, wv, bv)
        a_ref[...] = (gc * (1.0 / (1.0 + jnp.exp(-gc))) * vc).astype(BF16)

    return pl.pallas_call(
        body, name="ffn_act_fwd", grid=(nfc, nrow),
        in_specs=[u_main(0), u_prev(0), u_next(0), u_main(1), u_prev(1), u_next(1), cw(0), cw(1), cb(0), cb(1)],
        out_specs=pl.BlockSpec((ts, tc), lambda j, i: (i, j)),
        out_shape=jax.ShapeDtypeStruct((S, dff), BF16),
        compiler_params=_cp("arbitrary", "arbitrary"))(u, u, u, u, u, u, cw_full, cw_full, cb3, cb3)


def _ffn_act_bwd(u, da, cw_full, cb3, layer):
    S, two_dff = u.shape
    dff = two_dff // 2
    cq = cw_full.shape[3]
    ts = _pick(S, (512, 256, 128, 64, 32, 16))
    tc = _pick(cq, (256, 128))
    order = lambda j, i: (i, j)
    nfc, u_main, u_prev, u_next, cw, cb = _ffn_specs(S, dff, cq, ts, tc, layer, order)
    nrow = S // ts
    hb = ts // SUBLANES
    te = ts + 2 * SUBLANES
    da_main = pl.BlockSpec((ts, tc), lambda j, i: (i, j))
    da_prev = pl.BlockSpec((SUBLANES, tc), lambda j, i: (jnp.maximum(i * hb - 1, 0), j))
    da_next = pl.BlockSpec((SUBLANES, tc), lambda j, i: (jnp.minimum((i + 1) * hb, S // SUBLANES - 1), j))
    main = slice(SUBLANES, SUBLANES + ts)

    def body(ug, ugp, ugn, uv, uvp, uvn, dam, dap, dan, wg, wv, bg, bv, dug_ref, duv_ref, accg_ref, accv_ref):
        i = pl.program_id(1)

        @pl.when(i == 0)
        def _():
            accg_ref[...] = jnp.zeros_like(accg_ref)
            accv_ref[...] = jnp.zeros_like(accv_ref)

        def ext(m, p, n):
            return jnp.concatenate([jnp.where(i > 0, p[...], 0.0), m[...], jnp.where(i < nrow - 1, n[...], 0.0)], axis=0)

        def shift(x):
            return pltpu.roll(x, 1, 0), pltpu.roll(x, te - 1, 0)

        xg, xv, dae = ext(ug, ugp, ugn), ext(uv, uvp, uvn), ext(dam, dap, dan)
        xgm, xgp = shift(xg)
        xvm, xvp = shift(xv)
        gc = wg[0:1, :] * xgm + wg[1:2, :] * xg + wg[2:3, :] * xgp + bg[...]
        vc = wv[0:1, :] * xvm + wv[1:2, :] * xv + wv[2:3, :] * xvp + bv[...]
        sig = 1.0 / (1.0 + jnp.exp(-gc))
        silu = gc * sig
        dcg = dae * vc * (sig * (1.0 + gc * (1.0 - sig)))
        dcv = dae * silu

        def finish(dc, x, xm, xp, w_ref, du_ref, acc_ref):
            dm, dp = shift(dc)
            du = w_ref[0:1, :] * dp + w_ref[1:2, :] * dc + w_ref[2:3, :] * dm
            du_ref[...] = du[main, :].astype(BF16)
            dcm = dc[main, :]
            acc_ref[0:1, :] += jnp.sum(dcm * xm[main, :], axis=0, keepdims=True)
            acc_ref[1:2, :] += jnp.sum(dcm * x[main, :], axis=0, keepdims=True)
            acc_ref[2:3, :] += jnp.sum(dcm * xp[main, :], axis=0, keepdims=True)
            acc_ref[3:4, :] += jnp.sum(dcm, axis=0, keepdims=True)

        finish(dcg, xg, xgm, xgp, wg, dug_ref, accg_ref)
        finish(dcv, xv, xvm, xvp, wv, duv_ref, accv_ref)

    blk = pl.BlockSpec((ts, tc), lambda j, i: (i, j))
    acc = pl.BlockSpec((SUBLANES, tc), lambda j, i: (0, j))
    dug, duv, accg, accv = pl.pallas_call(
        body, name="ffn_act_bwd", grid=(nfc, nrow),
        in_specs=[u_main(0), u_prev(0), u_next(0), u_main(1), u_prev(1), u_next(1), da_main, da_prev, da_next,
                  cw(0), cw(1), cb(0), cb(1)],
        out_specs=[blk, blk, acc, acc],
        out_shape=[jax.ShapeDtypeStruct((S, dff), BF16)] * 2 + [jax.ShapeDtypeStruct((SUBLANES, dff), F32)] * 2,
        compiler_params=_cp("arbitrary", "arbitrary"))(u, u, u, u, u, u, da, da, da, cw_full, cw_full, cb3, cb3)
    return (dug, duv), jnp.concatenate([accg, accv], axis=1)


def _my_chip():
    return 2 * lax.axis_index("x") + lax.axis_index("y")


def _into_full(w, dtype):
    L, a, b = w.shape
    tr = _pick(a, (512, 256, 128, 64, 32, 16, 8))

    def body(w_ref, o_ref):
        o_ref[...] = w_ref[...].astype(dtype)

    return pl.pallas_call(
        body, name="into_full", grid=(L, a // tr),
        in_specs=[pl.BlockSpec((None, tr, b), lambda l, i: (l, i, 0))],
        out_specs=pl.BlockSpec((None, None, tr, b), lambda l, i: (l, _my_chip(), i, 0)),
        out_shape=jax.ShapeDtypeStruct((L, N_CHIPS, a, b), dtype),
        compiler_params=_cp("arbitrary", "arbitrary"))(w)


def _adam_math(w, g, m, v):
    m = ADAM_B1 * m + (1.0 - ADAM_B1) * g
    v = ADAM_B2 * v + (1.0 - ADAM_B2) * (g * g)
    m_hat = m / (1.0 - ADAM_B1 ** ADAM_STEP)
    v_hat = v / (1.0 - ADAM_B2 ** ADAM_STEP)
    delta = -ADAM_LR * (m_hat / (jnp.sqrt(v_hat) + ADAM_EPS) + ADAM_WD * w)
    return delta, m, v


def _adamw(w, g, m, v):
    R, C = w.shape
    tr = _pick(R, (128, 64, 32, 16, 8)) if R % SUBLANES == 0 and C % LANES == 0 else R

    def body(w_ref, g_ref, m_ref, v_ref, d_ref, nm_ref, nv_ref):
        d, nm, nv = _adam_math(w_ref[...], g_ref[...], m_ref[...], v_ref[...])
        d_ref[...] = d
        nm_ref[...] = nm
        nv_ref[...] = nv

    spec = pl.BlockSpec((tr, C), lambda i: (i, 0))
    return pl.pallas_call(
        body, name="adamw", grid=(R // tr,), in_specs=[spec] * 4, out_specs=[spec] * 3,
        out_shape=[jax.ShapeDtypeStruct((R, C), F32)] * 3, compiler_params=_cp("arbitrary"))(w, g, m, v)


ANY = pl.BlockSpec(memory_space=pl.ANY)


def _position():
    x, y, c = lax.axis_index("x"), lax.axis_index("y"), lax.axis_index("c")
    chips = [(1 - x, y), (x, 1 - y), (1 - x, 1 - y)]
    return x, y, c, chips


def _gather_weights(fulls):
    n = len(fulls)
    for f in fulls:
        assert f.shape[0] % 2 == 0

    def body(*refs):
        buf, send, recv = refs[n:2 * n], refs[2 * n], refs[2 * n + 1]
        x, y, c, chips = _position()
        sib = (x, y, 1 - c)

        def copy(t, k, chip, half, to):
            lh = fulls[t].shape[0] // 2
            part = buf[t].at[pl.ds(half * lh, lh), 2 * chip[0] + chip[1]]
            return pltpu.make_async_remote_copy(
                src_ref=part, dst_ref=part, send_sem=send.at[t * 6 + k], recv_sem=recv.at[t * 6 + k],
                device_id=to, device_id_type=MESH)

        first = [copy(t, j, (x, y), c, (*chips[j], c)) for t in range(n) for j in range(3)]
        for cp in first:
            cp.start()
        passed = []
        for t in range(n):
            for j in range(3):
                copy(t, j, chips[j], c, sib).wait_recv()
                fwd = copy(t, 3 + j, chips[j], c, sib)
                fwd.start()
                passed.append(fwd)
        for t in range(n):
            for j in range(3):
                copy(t, 3 + j, chips[j], 1 - c, sib).wait_recv()
        for cp in first + passed:
            cp.wait_send()

    return pl.pallas_call(
        body, name="gather_weights", in_specs=[ANY] * n, out_specs=[ANY] * n,
        out_shape=[jax.ShapeDtypeStruct(f.shape, f.dtype) for f in fulls],
        input_output_aliases={t: t for t in range(n)},
        scratch_shapes=[pltpu.SemaphoreType.DMA((6 * n,)), pltpu.SemaphoreType.DMA((6 * n,))])(*fulls)


def _allreduce_small(part):
    M, C = part.shape
    n_dev = 2 * N_CHIPS

    def body(x_ref, sum_ref, all_ref, send, recv, local):
        x, y, c, chips = _position()
        me, sib = (x, y, c), (x, y, 1 - c)

        def rows(px, py, pc):
            return all_ref.at[pl.ds((4 * px + 2 * py + pc) * M, M), :]

        def copy(k, block, to, src=None):
            return pltpu.make_async_remote_copy(
                src_ref=rows(*block) if src is None else src, dst_ref=rows(*block),
                send_sem=send.at[k], recv_sem=recv.at[k], device_id=to, device_id_type=MESH)

        mine = pltpu.make_async_copy(x_ref, rows(*me), local)
        mine.start()
        first = [copy(0, me, sib, src=x_ref)] + [copy(1 + j, me, (*chip, c), src=x_ref) for j, chip in enumerate(chips)]
        for cp in first:
            cp.start()
        passed = [copy(4 + j, (*chip, c), sib) for j, chip in enumerate(chips)]
        for j, chip in enumerate(chips):
            copy(1 + j, (*chip, c), me).wait_recv()
            passed[j].start()
        copy(0, sib, me).wait_recv()
        for j, chip in enumerate(chips):
            copy(4 + j, (*chip, 1 - c), me).wait_recv()
        for cp in first + passed:
            cp.wait_send()
        mine.wait()
        acc = all_ref[0:M, :]
        for d in range(1, n_dev):
            acc = acc + all_ref[d * M:(d + 1) * M, :]
        sum_ref[...] = acc

    vm = pl.BlockSpec(memory_space=pltpu.VMEM)
    return pl.pallas_call(
        body, name="allreduce_small", in_specs=[vm], out_specs=[vm],
        out_shape=[jax.ShapeDtypeStruct((M, C), F32)],
        scratch_shapes=[pltpu.VMEM((n_dev * M, C), F32), pltpu.SemaphoreType.DMA((7,)),
                        pltpu.SemaphoreType.DMA((7,)), pltpu.SemaphoreType.DMA],
        compiler_params=pltpu.CompilerParams(vmem_limit_bytes=VMEM_LIMIT))(part)[0]


def _rs_sibling(grads):
    n = len(grads)

    def body(*refs):
        src, dst, send, recv = refs[:n], refs[n:2 * n], refs[2 * n], refs[2 * n + 1]
        x, y, c, _ = _position()
        cps = [pltpu.make_async_remote_copy(
            src_ref=src[t].at[:, 1 - c], dst_ref=dst[t], send_sem=send.at[t], recv_sem=recv.at[t],
            device_id=(x, y, 1 - c), device_id_type=MESH) for t in range(n)]
        for cp in cps:
            cp.start()
        for cp in cps:
            cp.wait()

    return pl.pallas_call(
        body, name="rs_sibling", in_specs=[ANY] * n, out_specs=[ANY] * n,
        out_shape=[jax.ShapeDtypeStruct((g.shape[0],) + g.shape[2:], BF16) for g in grads],
        scratch_shapes=[pltpu.SemaphoreType.DMA((n,)), pltpu.SemaphoreType.DMA((n,))])(*grads)


def _add_sibling(grad, got):
    _, _, R, C = grad.shape
    tr = _pick(R, (512, 256, 128, 64, 32, 16))

    def body(g_ref, r_ref, o_ref):
        o_ref[...] = (g_ref[...].astype(F32) + r_ref[...].astype(F32)).astype(BF16)

    return pl.pallas_call(
        body, name="add_sibling", grid=(N_CHIPS, R // tr),
        in_specs=[pl.BlockSpec((None, None, tr, C), lambda p, i: (p, lax.axis_index("c"), i, 0)),
                  pl.BlockSpec((None, tr, C), lambda p, i: (p, i, 0))],
        out_specs=pl.BlockSpec((None, tr, C), lambda p, i: (p, i, 0)),
        out_shape=jax.ShapeDtypeStruct((N_CHIPS, R, C), BF16),
        compiler_params=_cp("arbitrary", "arbitrary"))(grad, got)


def _rs_chips(sums):
    n = len(sums)

    def body(*refs):
        src, dst, send, recv = refs[:n], refs[n:2 * n], refs[2 * n], refs[2 * n + 1]
        x, y, c, chips = _position()
        cps = [pltpu.make_async_remote_copy(
            src_ref=src[t].at[2 * chips[j][0] + chips[j][1]], dst_ref=dst[t].at[j],
            send_sem=send.at[3 * t + j], recv_sem=recv.at[3 * t + j],
            device_id=(*chips[j], c), device_id_type=MESH) for t in range(n) for j in range(3)]
        for cp in cps:
            cp.start()
        for cp in cps:
            cp.wait()

    return pl.pallas_call(
        body, name="rs_chips", in_specs=[ANY] * n, out_specs=[ANY] * n,
        out_shape=[jax.ShapeDtypeStruct((3,) + s.shape[1:], BF16) for s in sums],
        scratch_shapes=[pltpu.SemaphoreType.DMA((3 * n,)), pltpu.SemaphoreType.DMA((3 * n,))])(*sums)


def _add_chips(grad, got_sib, got_chips, stack, layer):
    _, _, R, C = grad.shape
    tr = _pick(R, (256, 128, 64, 32, 16))

    def body(g_ref, s_ref, r_ref, stack_ref, o_ref):
        acc = g_ref[...].astype(F32) + s_ref[...].astype(F32)
        for j in range(3):
            acc = acc + r_ref[j].astype(F32)
        o_ref[...] = acc

    return pl.pallas_call(
        body, name="add_chips", grid=(R // tr,),
        in_specs=[pl.BlockSpec((None, None, tr, C), lambda i: (_my_chip(), lax.axis_index("c"), i, 0)),
                  pl.BlockSpec((None, tr, C), lambda i: (_my_chip(), i, 0)),
                  pl.BlockSpec((3, tr, C), lambda i: (0, i, 0)),
                  ANY],
        out_specs=pl.BlockSpec((None, None, tr, C), lambda i: (layer, lax.axis_index("c"), i, 0)),
        out_shape=jax.ShapeDtypeStruct(stack.shape, F32), input_output_aliases={3: 0},
        compiler_params=_cp("arbitrary"))(grad, got_sib, got_chips, stack)


def _ag_sibling(stacks):
    n = len(stacks)
    offs = np.cumsum([0] + [s.shape[0] for s in stacks])

    def body(*refs):
        buf, send, recv = refs[n:2 * n], refs[2 * n], refs[2 * n + 1]
        x, y, c, _ = _position()

        def copy(t, l, half):
            part = buf[t].at[l, half]
            return pltpu.make_async_remote_copy(
                src_ref=part, dst_ref=part, send_sem=send.at[int(offs[t]) + l], recv_sem=recv.at[int(offs[t]) + l],
                device_id=(x, y, 1 - c), device_id_type=MESH)

        cps = [copy(t, l, c) for t in range(n) for l in range(stacks[t].shape[0])]
        for cp in cps:
            cp.start()
        for t in range(n):
            for l in range(stacks[t].shape[0]):
                copy(t, l, 1 - c).wait_recv()
        for cp in cps:
            cp.wait_send()

    return pl.pallas_call(
        body, name="ag_sibling", in_specs=[ANY] * n, out_specs=[ANY] * n,
        out_shape=[jax.ShapeDtypeStruct(s.shape, F32) for s in stacks],
        input_output_aliases={t: t for t in range(n)},
        scratch_shapes=[pltpu.SemaphoreType.DMA((int(offs[-1]),)), pltpu.SemaphoreType.DMA((int(offs[-1]),))])(*stacks)


def _split8(dw, blocked):
    if blocked:
        p, k, nq = dw.shape
        return dw.reshape(p, 2, k // 2, nq)
    k, n = dw.shape
    return dw.reshape(N_CHIPS, 2, k // (2 * N_CHIPS), n)


def kernel(x, a_w_qkv, a_w_o, a_q_gain, a_k_gain, b_w_qkv, b_w_o, rel_bias, mix_norm, ffn_norm, w_up, conv_w, conv_b, w_down, final_norm, loss_target, m_a_w_qkv, m_a_w_o, m_a_q_gain, m_a_k_gain, m_b_w_qkv, m_b_w_o, m_rel_bias, m_mix_norm, m_ffn_norm, m_w_up, m_conv_w, m_conv_b, m_w_down, m_final_norm, v_a_w_qkv, v_a_w_o, v_a_q_gain, v_a_k_gain, v_b_w_qkv, v_b_w_o, v_rel_bias, v_mix_norm, v_ffn_norm, v_w_up, v_conv_w, v_conv_b, v_w_down, v_final_norm):
    S, D = x.shape[1], x.shape[2]
    h = x.reshape(S, D)
    target = loss_target.reshape(S, D)
    hg = B_HEADS_PER_GROUP
    G = len(B_GROUPS)
    n_a, n_b = a_w_qkv.shape[0], b_w_qkv.shape[0]
    depth = w_up.shape[0]
    cx, cy = lax.axis_index("x"), lax.axis_index("y")

    big = dict(a_w_qkv=a_w_qkv, a_w_o=a_w_o, b_w_qkv=b_w_qkv, b_w_o=b_w_o, w_up=w_up, w_down=w_down)
    blocked = dict(a_w_qkv=True, a_w_o=False, b_w_qkv=True, b_w_o=False, w_up=True, w_down=False)
    names = list(big)
    gathered = _gather_weights([_into_full(big[k], BF16) for k in names] + [_into_full(conv_w, F32)])
    full = {}
    for k, gth in zip(names, gathered[:-1]):
        L, _, a, b = gth.shape
        full[k] = gth if blocked[k] else gth.reshape(L, N_CHIPS * a, b)
    cw_full = gathered[-1]
    cb3 = conv_b.reshape(depth, 1, conv_b.shape[1])

    cos, sin = _rope_tables(S)
    buckets = jnp.asarray(_bucket_tables())
    bias = _bias_build(rel_bias, buckets)

    saved = []
    for i in range(depth):
        j = i // 2
        sv = dict(h0=h)
        hn = _rms_fwd(h, mix_norm[i:i + 1])
        sv["hn"] = hn
        if i % 2 == 0:
            qkv = _mm_nn(hn, full["a_w_qkv"], j, blocked=True, name="a_qkv")
            qkvh = _prep_a_fwd(qkv, cos, sin, a_q_gain[j:j + 1], a_k_gain[j:j + 1])
            o, lse = _flash_a_fwd(qkvh)
            sv.update(qkv=qkv, qkvh=qkvh, o=o, lse=lse)
            h = _mm_nn(o, full["a_w_o"], j, blocked=False, res=h, name="a_out")
        else:
            qkv = _mm_nn(hn, full["b_w_qkv"], j, blocked=True, name="b_qkv")
            os_, lzs = [], []
            for g in range(G):
                o_g, lz_g = _battn_fwd(qkv, bias, g)
                os_.append(o_g)
                lzs.append(lz_g)
            y = _combine_fwd(os_, lzs)
            sv.update(qkv=qkv, os=os_, lzs=lzs, y=y)
            h = _mm_nn(y, full["b_w_o"], j, blocked=False, res=h, name="b_out")
        sv["h1"] = h
        hf = _rms_fwd(h, ffn_norm[i:i + 1])
        u = _mm_nn(hf, full["w_up"], i, blocked=True, name="ffn_up")
        act = _ffn_act_fwd(u, cw_full, cb3, i)
        sv.update(hf=hf, u=u, act=act)
        h = _mm_nn(act, full["w_down"], i, blocked=False, res=h, name="ffn_down")
        saved.append(sv)

    loss_blk, dh, dh_b, dg_final = _final_loss(h, final_norm.reshape(1, D), target)

    dws = {k: [None] * big[k].shape[0] for k in names}
    d_mix, d_ffn, d_convw, d_convb = [None] * depth, [None] * depth, [None] * depth, [None] * depth
    d_gq, d_gk = [None] * n_a, [None] * n_a
    dbias_list = []
    for i in reversed(range(depth)):
        j = i // 2
        sv = saved[i]
        da = _mm_nt(dh_b, full["w_down"], i, blocked=False, name="ffn_down_dx")
        dws["w_down"][i] = _mm_tn(sv["act"], dh_b, blocked=False, name="ffn_down_dw")
        du, dconv = _ffn_act_bwd(sv["u"], da, cw_full, cb3, i)
        d_convw[i], d_convb[i] = dconv[0:3], dconv[3]
        dhf = _mm_nt(du, full["w_up"], i, blocked=True, name="ffn_up_dx")
        dws["w_up"][i] = _mm_tn(sv["hf"], du, blocked=True, name="ffn_up_dw")
        dh, dh_b, dg = _rms_bwd(dhf, sv["h1"], ffn_norm[i:i + 1], dh)
        d_ffn[i] = dg[0]
        if i % 2 == 0:
            do = _mm_nt(dh_b, full["a_w_o"], j, blocked=False, name="a_out_dx")
            dws["a_w_o"][j] = _mm_tn(sv["o"], dh_b, blocked=False, name="a_out_dw")
            dq, dk, dv = _flash_a_bwd(sv["qkvh"], do, sv["o"], sv["lse"])
            dqkv, dgain = _prep_a_bwd(dq, dk, dv, sv["qkv"], cos, sin, a_q_gain[j:j + 1], a_k_gain[j:j + 1])
            d_gq[j], d_gk[j] = dgain[0], dgain[1]
            dhn = _mm_nt(dqkv, full["a_w_qkv"], j, blocked=True, name="a_qkv_dx")
            dws["a_w_qkv"][j] = _mm_tn(sv["hn"], dqkv, blocked=True, name="a_qkv_dw")
        else:
            dy = _mm_nt(dh_b, full["b_w_o"], j, blocked=False, name="b_out_dx")
            dws["b_w_o"][j] = _mm_tn(sv["y"], dh_b, blocked=False, name="b_out_dw")
            dos, dlzs = _combine_bwd(dy, sv["os"], sv["lzs"])
            parts = []
            for g in range(G):
                dq, dk, dv, db = _battn_bwd(sv["qkv"], bias, dos[g], sv["os"][g], sv["lzs"][g], dlzs[g], g)
                parts += [dq, dk, dv]
                dbias_list.append((g, db))
            dqkv = _concat_cast(parts)
            dhn = _mm_nt(dqkv, full["b_w_qkv"], j, blocked=True, name="b_qkv_dx")
            dws["b_w_qkv"][j] = _mm_tn(sv["hn"], dqkv, blocked=True, name="b_qkv_dw")
        dh, dh_b, dg = _rms_bwd(dhn, sv["h0"], mix_norm[i:i + 1], dh)
        d_mix[i] = dg[0]
    grad_x = dh.reshape(x.shape)

    dbias_layers = [jnp.stack([db for g2, db in dbias_list[l * G:(l + 1) * G]]) for l in range(n_b)]
    d_rel = _bias_reduce(dbias_layers, buckets)[:, :G * hg]

    small = [jnp.stack(d_gq), jnp.stack(d_gk), d_rel, jnp.stack(d_mix), jnp.stack(d_ffn), jnp.stack(d_convw),
             jnp.stack(d_convb), dg_final[0]]
    sizes = [int(np.prod(s.shape)) for s in small]
    flat = jnp.concatenate([s.reshape(-1) for s in small])
    rows = -(-flat.shape[0] // (LANES * SUBLANES)) * SUBLANES
    flat = jnp.pad(flat, (0, rows * LANES - flat.shape[0])).reshape(rows, LANES)
    tot = _allreduce_small(flat).reshape(-1)
    offs = np.cumsum([0] + sizes)
    g_gq, g_gk, g_rel, g_mix, g_ffn, g_convw_full, g_convb, g_final = [
        tot[offs[k]:offs[k + 1]].reshape(small[k].shape) for k in range(len(small))]
    cq = conv_w.shape[2]
    g_convw = lax.dynamic_slice_in_dim(g_convw_full, (2 * cx + cy) * cq, cq, axis=2)

    order = [(k, l) for k in names for l in range(big[k].shape[0])]
    pieces = [_split8(dws[k][l], blocked[k]) for k, l in order]
    got_sib = _rs_sibling(pieces)
    chip_sums = [_add_sibling(p, r) for p, r in zip(pieces, got_sib)]
    got_chips = _rs_chips(chip_sums)
    stacks = {k: lax.empty((big[k].shape[0], 2) + pieces[t].shape[2:], F32)
              for t, (k, l) in enumerate(order) if l == 0}
    for (k, l), p, r, rc in zip(order, pieces, got_sib, got_chips):
        stacks[k] = _add_chips(p, r, rc, stacks[k], l)
    shard_grads = {k: gs.reshape(big[k].shape) for k, gs in zip(names, _ag_sibling([stacks[k] for k in names]))}

    grads = dict(shard_grads, a_q_gain=g_gq, a_k_gain=g_gk, rel_bias=g_rel, mix_norm=g_mix, ffn_norm=g_ffn,
                 conv_w=g_convw, conv_b=g_convb, final_norm=g_final)
    weights = dict(a_w_qkv=a_w_qkv, a_w_o=a_w_o, a_q_gain=a_q_gain, a_k_gain=a_k_gain, b_w_qkv=b_w_qkv, b_w_o=b_w_o,
                   rel_bias=rel_bias, mix_norm=mix_norm, ffn_norm=ffn_norm, w_up=w_up, conv_w=conv_w, conv_b=conv_b,
                   w_down=w_down, final_norm=final_norm)
    ms = dict(a_w_qkv=m_a_w_qkv, a_w_o=m_a_w_o, a_q_gain=m_a_q_gain, a_k_gain=m_a_k_gain, b_w_qkv=m_b_w_qkv,
              b_w_o=m_b_w_o, rel_bias=m_rel_bias, mix_norm=m_mix_norm, ffn_norm=m_ffn_norm, w_up=m_w_up,
              conv_w=m_conv_w, conv_b=m_conv_b, w_down=m_w_down, final_norm=m_final_norm)
    vs = dict(a_w_qkv=v_a_w_qkv, a_w_o=v_a_w_o, a_q_gain=v_a_q_gain, a_k_gain=v_a_k_gain, b_w_qkv=v_b_w_qkv,
              b_w_o=v_b_w_o, rel_bias=v_rel_bias, mix_norm=v_mix_norm, ffn_norm=v_ffn_norm, w_up=v_w_up,
              conv_w=v_conv_w, conv_b=v_conv_b, w_down=v_w_down, final_norm=v_final_norm)
    deltas, new_m, new_v = {}, {}, {}
    for k, w in weights.items():
        two_d = (-1, w.shape[-1])
        d, nm, nv = _adamw(w.reshape(two_d), grads[k].reshape(two_d), ms[k].reshape(two_d), vs[k].reshape(two_d))
        deltas[k], new_m[k], new_v[k] = d.reshape(w.shape), nm.reshape(w.shape), nv.reshape(w.shape)

    loss = lax.psum(loss_blk[0, 0], ("x", "y", "c"))
    keys = list(weights)
    return (loss, grad_x, *[grads[k].reshape(weights[k].shape) for k in keys], *[deltas[k] for k in keys],
            *[new_m[k] for k in keys], *[new_v[k] for k in keys])
```

```python
import functools
import math

import numpy as np
import jax
import jax.numpy as jnp
from jax import lax
from jax.experimental import pallas as pl
from jax.experimental.pallas import tpu as pltpu

F32 = jnp.float32
BF16 = jnp.bfloat16

HEAD_DIM = 128
A_HEADS = 16
A_KV_HEADS = 4
GRID_W = 64
ROPE_THETA = 10000.0
B_GROUPS = ((128, 1), (512, 4), (2048, 16))
B_HEADS_PER_GROUP = 8
REL_BUCKETS = 32
REL_MAX_DISTANCE = 1024
EPS = 1e-6
NEG_INF = -1e30
DEPTH = 4
ADAM_LR = 0.001
ADAM_B1 = 0.9
ADAM_B2 = 0.999
ADAM_EPS = 1e-08
ADAM_WD = 0.01
ADAM_STEP = 10

N_CHIPS = 4
LANES = 128
SUBLANES = 8
VMEM_LIMIT = 52 * 1024 * 1024
MESH = pl.DeviceIdType.MESH


def _pick(n, cands):
    for c in cands:
        if c <= n and n % c == 0:
            return c
    return n


def _cp(*sem):
    return pltpu.CompilerParams(dimension_semantics=sem if sem else None, vmem_limit_bytes=VMEM_LIMIT)


def _half_span():
    hs = {w // (2 * d) for w, d in B_GROUPS}
    assert len(hs) == 1
    return hs.pop()


def _rms_fwd(h, gain, after=()):
    S, D = h.shape
    ts = _pick(S, (512, 256, 128, 64, 32, 16))

    def body(h_ref, g_ref, *rest):
        o_ref = rest[-1]
        x = h_ref[...]
        r = lax.rsqrt(jnp.mean(x * x, axis=-1, keepdims=True) + EPS)
        o_ref[...] = (x * r * g_ref[...]).astype(o_ref.dtype)

    return pl.pallas_call(
        body, name="rms_fwd", grid=(S // ts,),
        in_specs=[pl.BlockSpec((ts, D), lambda i: (i, 0)), pl.BlockSpec((1, D), lambda i: (0, 0))]
        + [pl.BlockSpec(memory_space=pl.ANY)] * len(after),
        out_specs=pl.BlockSpec((ts, D), lambda i: (i, 0)),
        out_shape=jax.ShapeDtypeStruct((S, D), BF16), compiler_params=_cp("arbitrary"))(h, gain, *after)


def _rms_bwd(dy, h, gain, dres):
    S, D = h.shape
    ts = _pick(S, (256, 128, 64, 32, 16))

    def body(dy_ref, h_ref, g_ref, dres_ref, dh_ref, dhb_ref, dg_ref):
        @pl.when(pl.program_id(0) == 0)
        def _():
            dg_ref[...] = jnp.zeros_like(dg_ref)
        x = h_ref[...]
        dy = dy_ref[...]
        r = lax.rsqrt(jnp.mean(x * x, axis=-1, keepdims=True) + EPS)
        xn = x * r
        dg_ref[0:1, :] += jnp.sum(dy * xn, axis=0, keepdims=True)
        dxn = dy * g_ref[...]
        dx = r * (dxn - xn * jnp.mean(dxn * xn, axis=-1, keepdims=True))
        dh = dres_ref[...] + dx
        dh_ref[...] = dh
        dhb_ref[...] = dh.astype(BF16)

    row = pl.BlockSpec((ts, D), lambda i: (i, 0))
    return pl.pallas_call(
        body, name="rms_bwd", grid=(S // ts,),
        in_specs=[row, row, pl.BlockSpec((1, D), lambda i: (0, 0)), row],
        out_specs=[row, row, pl.BlockSpec((SUBLANES, D), lambda i: (0, 0))],
        out_shape=[jax.ShapeDtypeStruct((S, D), F32), jax.ShapeDtypeStruct((S, D), BF16),
                   jax.ShapeDtypeStruct((SUBLANES, D), F32)],
        compiler_params=_cp("arbitrary"))(dy, h, gain, dres)


def _final_loss(h, gain, target):
    S, D = h.shape
    ts = _pick(S, (256, 128, 64, 32, 16))

    def body(h_ref, g_ref, t_ref, loss_ref, dh_ref, dhb_ref, dg_ref):
        @pl.when(pl.program_id(0) == 0)
        def _():
            dg_ref[...] = jnp.zeros_like(dg_ref)
            loss_ref[...] = jnp.zeros_like(loss_ref)
        x = h_ref[...]
        g = g_ref[...]
        r = lax.rsqrt(jnp.mean(x * x, axis=-1, keepdims=True) + EPS)
        xn = x * r
        err = xn * g - t_ref[...]
        part = 0.5 * jnp.sum(jnp.mean(err * err, axis=-1, keepdims=True), axis=0, keepdims=True)
        loss_ref[0:1, 0:1] += part
        dy = err * (1.0 / D)
        dg_ref[0:1, :] += jnp.sum(dy * xn, axis=0, keepdims=True)
        dxn = dy * g
        dh = r * (dxn - xn * jnp.mean(dxn * xn, axis=-1, keepdims=True))
        dh_ref[...] = dh
        dhb_ref[...] = dh.astype(BF16)

    row = pl.BlockSpec((ts, D), lambda i: (i, 0))
    return pl.pallas_call(
        body, name="final_loss", grid=(S // ts,),
        in_specs=[row, pl.BlockSpec((1, D), lambda i: (0, 0)), row],
        out_specs=[pl.BlockSpec((SUBLANES, LANES), lambda i: (0, 0)), row, row,
                   pl.BlockSpec((SUBLANES, D), lambda i: (0, 0))],
        out_shape=[jax.ShapeDtypeStruct((SUBLANES, LANES), F32), jax.ShapeDtypeStruct((S, D), F32),
                   jax.ShapeDtypeStruct((S, D), BF16), jax.ShapeDtypeStruct((SUBLANES, D), F32)],
        compiler_params=_cp("arbitrary"))(h, gain, target)


_NN = (((1,), (0,)), ((), ()))
_NT = (((1,), (1,)), ((), ()))
_TN = (((0,), (0,)), ((), ()))


def _mm_nn(a, w, layer, *, blocked, out_dtype=F32, res=None, name):
    M, K = a.shape
    if blocked:
        nq = w.shape[3]
        N = N_CHIPS * nq
        tn = _pick(nq, (256, 128))
        nps = nq // tn
        w_spec = pl.BlockSpec((None, None, K, tn), lambda i, j: (layer, j // nps, 0, j % nps))
    else:
        N = w.shape[2]
        tn = _pick(N, (256, 128))
        w_spec = pl.BlockSpec((None, K, tn), lambda i, j: (layer, 0, j))
    tm = _pick(M, (1024, 512, 256, 128, 64, 32, 16)) if K <= 3072 else _pick(M, (512, 256, 128, 64, 32, 16))

    def body(*refs):
        if res is None:
            a_ref, w_ref, o_ref = refs
            acc = lax.dot_general(a_ref[...], w_ref[...], _NN, preferred_element_type=F32)
        else:
            a_ref, w_ref, r_ref, o_ref = refs
            acc = r_ref[...] + lax.dot_general(a_ref[...], w_ref[...], _NN, preferred_element_type=F32)
        o_ref[...] = acc.astype(o_ref.dtype)

    in_specs = [pl.BlockSpec((tm, K), lambda i, j: (i, 0)), w_spec]
    args = [a, w]
    if res is not None:
        in_specs.append(pl.BlockSpec((tm, tn), lambda i, j: (i, j)))
        args.append(res)
    return pl.pallas_call(
        body, name=name, grid=(M // tm, N // tn), in_specs=in_specs,
        out_specs=pl.BlockSpec((tm, tn), lambda i, j: (i, j)),
        out_shape=jax.ShapeDtypeStruct((M, N), out_dtype),
        compiler_params=_cp("arbitrary", "arbitrary"))(*args)


def _mm_nt(a, w, layer, *, blocked, name, after=()):
    pair = isinstance(a, tuple)
    M = a[0].shape[0] if pair else a.shape[0]
    tm = _pick(M, (1024, 512, 256, 128, 64, 32, 16))
    if blocked:
        K, nq = w.shape[2], w.shape[3]
        tk = _pick(K, (1024, 512, 256, 128))
        half = N_CHIPS // 2

        def body(*refs):
            a_refs, (w_ref, o_ref, acc_ref) = refs[:-3], refs[-3:]
            p = pl.program_id(2)

            @pl.when(p == 0)
            def _():
                acc_ref[...] = jnp.zeros_like(acc_ref)
            if pair:
                @pl.when(p < half)
                def _():
                    acc_ref[...] += lax.dot_general(a_refs[0][...], w_ref[...], _NT, preferred_element_type=F32)

                @pl.when(p >= half)
                def _():
                    acc_ref[...] += lax.dot_general(a_refs[1][...], w_ref[...], _NT, preferred_element_type=F32)
            else:
                acc_ref[...] += lax.dot_general(a_refs[0][...], w_ref[...], _NT, preferred_element_type=F32)

            @pl.when(p == N_CHIPS - 1)
            def _():
                o_ref[...] = acc_ref[...]

        if pair:
            a_specs = [pl.BlockSpec((tm, nq), lambda i, j, p: (i, jnp.minimum(p, half - 1))),
                       pl.BlockSpec((tm, nq), lambda i, j, p: (i, jnp.maximum(p - half, 0)))]
            a_args = list(a)
        else:
            a_specs = [pl.BlockSpec((tm, nq), lambda i, j, p: (i, p))]
            a_args = [a]
        return pl.pallas_call(
            body, name=name, grid=(M // tm, K // tk, N_CHIPS),
            in_specs=a_specs + [pl.BlockSpec((None, None, tk, nq), lambda i, j, p: (layer, p, j, 0))],
            out_specs=pl.BlockSpec((tm, tk), lambda i, j, p: (i, j)),
            out_shape=jax.ShapeDtypeStruct((M, K), F32),
            scratch_shapes=[pltpu.VMEM((tm, tk), F32)],
            compiler_params=_cp("arbitrary", "arbitrary", "arbitrary"))(*a_args, w)
    K, N = w.shape[1], w.shape[2]
    tk = _pick(K, (512, 256, 128))

    def body(a_ref, w_ref, *rest):
        rest[-1][...] = lax.dot_general(a_ref[...], w_ref[...], _NT, preferred_element_type=F32)

    return pl.pallas_call(
        body, name=name, grid=(M // tm, K // tk),
        in_specs=[pl.BlockSpec((tm, N), lambda i, j: (i, 0)),
                  pl.BlockSpec((None, tk, N), lambda i, j: (layer, j, 0))]
        + [pl.BlockSpec(memory_space=pl.ANY)] * len(after),
        out_specs=pl.BlockSpec((tm, tk), lambda i, j: (i, j)),
        out_shape=jax.ShapeDtypeStruct((M, K), F32),
        compiler_params=_cp("arbitrary", "arbitrary"))(a, w, *after)


def _mm_tn(x, dy, *, blocked, name):
    pair = isinstance(dy, tuple)
    S, K = x.shape
    N = 2 * dy[0].shape[1] if pair else dy.shape[1]
    tk = _pick(K, (512, 256, 128))
    if blocked:
        nq = N // N_CHIPS
        tn = _pick(nq, (256, 128))
        nps = nq // tn
        out_spec = pl.BlockSpec((None, tk, tn), lambda i, j: (j // nps, i, j % nps))
        out_shape = jax.ShapeDtypeStruct((N_CHIPS, K, nq), BF16)
    else:
        tn = _pick(N, (512, 256, 128))
        out_spec = pl.BlockSpec((tk, tn), lambda i, j: (i, j))
        out_shape = jax.ShapeDtypeStruct((K, N), BF16)
    nj = N // tn
    njh = nj // 2

    def body(x_ref, *refs):
        o_ref = refs[-1]
        if pair:
            j = pl.program_id(1)

            @pl.when(j < njh)
            def _():
                o_ref[...] = lax.dot_general(x_ref[...], refs[0][...], _TN, preferred_element_type=F32).astype(o_ref.dtype)

            @pl.when(j >= njh)
            def _():
                o_ref[...] = lax.dot_general(x_ref[...], refs[1][...], _TN, preferred_element_type=F32).astype(o_ref.dtype)
        else:
            o_ref[...] = lax.dot_general(x_ref[...], refs[0][...], _TN, preferred_element_type=F32).astype(o_ref.dtype)

    if pair:
        assert nj % 2 == 0
        dy_specs = [pl.BlockSpec((S, tn), lambda i, j: (0, jnp.minimum(j, njh - 1))),
                    pl.BlockSpec((S, tn), lambda i, j: (0, jnp.maximum(j - njh, 0)))]
        dy_args = list(dy)
    else:
        dy_specs = [pl.BlockSpec((S, tn), lambda i, j: (0, j))]
        dy_args = [dy]
    return pl.pallas_call(
        body, name=name, grid=(K // tk, nj),
        in_specs=[pl.BlockSpec((S, tk), lambda i, j: (0, i))] + dy_specs,
        out_specs=out_spec, out_shape=out_shape,
        compiler_params=_cp("arbitrary", "arbitrary"))(x, *dy_args)


def _rope_tables(S):
    rows = S // GRID_W
    row_ids = jnp.repeat(jnp.arange(rows, dtype=F32), GRID_W)
    col_ids = jnp.tile(jnp.arange(GRID_W, dtype=F32), rows)
    quarter = HEAD_DIM // 4
    inv_freq = ROPE_THETA ** (-jnp.arange(quarter, dtype=F32) / quarter)
    ang_r = row_ids[:, None] * inv_freq[None, :]
    ang_c = col_ids[:, None] * inv_freq[None, :]
    cos = jnp.concatenate([jnp.cos(ang_r)] * 2 + [jnp.cos(ang_c)] * 2, axis=-1)
    sin = jnp.concatenate([-jnp.sin(ang_r), jnp.sin(ang_r), -jnp.sin(ang_c), jnp.sin(ang_c)], axis=-1)
    return cos, sin


def _swap_quarters(x):
    lane = lax.broadcasted_iota(jnp.int32, x.shape, 1)
    first = (lane % (HEAD_DIM // 2)) < (HEAD_DIM // 4)
    return jnp.where(first, pltpu.roll(x, HEAD_DIM - HEAD_DIM // 4, 1), pltpu.roll(x, HEAD_DIM // 4, 1))


def _prep_a_fwd(qkv, cos, sin, gq, gk):
    S, W = qkv.shape
    nrm = A_HEADS + A_KV_HEADS
    ts = _pick(S, (256, 128, 64, 32, 16))

    def body(qkv_ref, cos_ref, sin_ref, gq_ref, gk_ref, o_ref):
        cos_t = cos_ref[...]
        sin_t = sin_ref[...]
        for j in range(nrm):
            sl = slice(j * HEAD_DIM, (j + 1) * HEAD_DIM)
            x = qkv_ref[:, sl]
            g = gq_ref[...] if j < A_HEADS else gk_ref[...]
            r = lax.rsqrt(jnp.mean(x * x, axis=-1, keepdims=True) + EPS)
            n = x * r * g
            o_ref[:, sl] = (n * cos_t + _swap_quarters(n) * sin_t).astype(BF16)
        o_ref[:, nrm * HEAD_DIM:] = qkv_ref[:, nrm * HEAD_DIM:].astype(BF16)

    row = lambda w: pl.BlockSpec((ts, w), lambda i: (i, 0))
    one = pl.BlockSpec((1, HEAD_DIM), lambda i: (0, 0))
    return pl.pallas_call(
        body, name="prep_a_fwd", grid=(S // ts,),
        in_specs=[row(W), row(HEAD_DIM), row(HEAD_DIM), one, one], out_specs=row(W),
        out_shape=jax.ShapeDtypeStruct((S, W), BF16), compiler_params=_cp("arbitrary"))(qkv, cos, sin, gq, gk)


def _prep_a_bwd(dq, dk, dv, qkv, cos, sin, gq, gk):
    S, W = qkv.shape
    nrm = A_HEADS + A_KV_HEADS
    nq, nk = A_HEADS * HEAD_DIM, A_KV_HEADS * HEAD_DIM
    ts = _pick(S, (256, 128, 64, 32, 16))

    def body(dq_ref, dk_ref, dv_ref, qkv_ref, cos_ref, sin_ref, gq_ref, gk_ref, o_ref, dg_ref):
        @pl.when(pl.program_id(0) == 0)
        def _():
            dg_ref[...] = jnp.zeros_like(dg_ref)
        cos_t = cos_ref[...]
        sin_t = sin_ref[...]
        for j in range(nrm):
            sl = slice(j * HEAD_DIM, (j + 1) * HEAD_DIM)
            x = qkv_ref[:, sl]
            if j < A_HEADS:
                dy, g, grow = dq_ref[:, sl], gq_ref[...], 0
            else:
                jj = j - A_HEADS
                dy, g, grow = dk_ref[:, jj * HEAD_DIM:(jj + 1) * HEAD_DIM], gk_ref[...], 1
            r = lax.rsqrt(jnp.mean(x * x, axis=-1, keepdims=True) + EPS)
            xn = x * r
            dn = dy * cos_t + _swap_quarters(dy * sin_t)
            dg_ref[grow:grow + 1, :] += jnp.sum(dn * xn, axis=0, keepdims=True)
            dxn = dn * g
            o_ref[:, sl] = (r * (dxn - xn * jnp.mean(dxn * xn, axis=-1, keepdims=True))).astype(BF16)
        o_ref[:, nrm * HEAD_DIM:] = dv_ref[...].astype(BF16)

    row = lambda w: pl.BlockSpec((ts, w), lambda i: (i, 0))
    one = pl.BlockSpec((1, HEAD_DIM), lambda i: (0, 0))
    return pl.pallas_call(
        body, name="prep_a_bwd", grid=(S // ts,),
        in_specs=[row(nq), row(nk), row(nk), row(W), row(HEAD_DIM), row(HEAD_DIM), one, one],
        out_specs=[row(W), pl.BlockSpec((SUBLANES, HEAD_DIM), lambda i: (0, 0))],
        out_shape=[jax.ShapeDtypeStruct((S, W), BF16), jax.ShapeDtypeStruct((SUBLANES, HEAD_DIM), F32)],
        compiler_params=_cp("arbitrary"))(dq, dk, dv, qkv, cos, sin, gq, gk)


def _flash_a_fwd(qkvh):
    S = qkvh.shape[0]
    grp = A_HEADS // A_KV_HEADS
    tq = _pick(S, (256, 128, 64, 32, 16))
    scale = HEAD_DIM ** -0.5

    def body(q_ref, k_ref, v_ref, o_ref, lse_ref):
        s = lax.dot_general(q_ref[...], k_ref[...], _NT, preferred_element_type=F32) * scale
        m = jnp.max(s, axis=-1, keepdims=True)
        p = jnp.exp(s - m)
        l = jnp.sum(p, axis=-1, keepdims=True)
        pn = (p * (1.0 / l)).astype(BF16)
        o_ref[...] = lax.dot_general(pn, v_ref[...], _NN, preferred_element_type=F32).astype(BF16)
        lse_ref[...] = jnp.broadcast_to(m + jnp.log(l), lse_ref.shape)

    qs = pl.BlockSpec((tq, HEAD_DIM), lambda h, i: (i, h))
    return pl.pallas_call(
        body, name="flash_a_fwd", grid=(A_HEADS, S // tq),
        in_specs=[qs,
                  pl.BlockSpec((S, HEAD_DIM), lambda h, i: (0, A_HEADS + h // grp)),
                  pl.BlockSpec((S, HEAD_DIM), lambda h, i: (0, A_HEADS + A_KV_HEADS + h // grp))],
        out_specs=[qs, qs],
        out_shape=[jax.ShapeDtypeStruct((S, A_HEADS * HEAD_DIM), BF16),
                   jax.ShapeDtypeStruct((S, A_HEADS * HEAD_DIM), F32)],
        compiler_params=_cp("arbitrary", "arbitrary"))(qkvh, qkvh, qkvh)


def _flash_a_bwd(qkvh, do, o, lse):
    S = qkvh.shape[0]
    grp = A_HEADS // A_KV_HEADS
    tq = _pick(S, (256, 128, 64, 32, 16))
    scale = HEAD_DIM ** -0.5

    def body(q_ref, k_ref, v_ref, do_ref, o_ref, lse_ref, dq_ref, dk_ref, dv_ref):
        @pl.when((pl.program_id(1) == 0) & (pl.program_id(2) == 0))
        def _():
            dk_ref[...] = jnp.zeros_like(dk_ref)
            dv_ref[...] = jnp.zeros_like(dv_ref)
        q = q_ref[...]
        k = k_ref[...]
        do_f = do_ref[...]
        do_b = do_f.astype(BF16)
        s = lax.dot_general(q, k, _NT, preferred_element_type=F32) * scale
        p = jnp.exp(s - lse_ref[:, 0:1])
        dp = lax.dot_general(do_b, v_ref[...], _NT, preferred_element_type=F32)
        delta = jnp.sum(do_f * o_ref[...].astype(F32), axis=-1, keepdims=True)
        ds_b = (p * (dp - delta) * scale).astype(BF16)
        dq_ref[...] = lax.dot_general(ds_b, k, _NN, preferred_element_type=F32)
        dk_ref[...] += lax.dot_general(ds_b, q, _TN, preferred_element_type=F32)
        dv_ref[...] += lax.dot_general(p.astype(BF16), do_b, _TN, preferred_element_type=F32)

    qs = pl.BlockSpec((tq, HEAD_DIM), lambda kv, g, i: (i, kv * grp + g))
    kvs = lambda off: pl.BlockSpec((S, HEAD_DIM), lambda kv, g, i: (0, off + kv))
    return pl.pallas_call(
        body, name="flash_a_bwd", grid=(A_KV_HEADS, grp, S // tq),
        in_specs=[qs, kvs(A_HEADS), kvs(A_HEADS + A_KV_HEADS), qs, qs, qs],
        out_specs=[qs, kvs(0), kvs(0)],
        out_shape=[jax.ShapeDtypeStruct((S, A_HEADS * HEAD_DIM), F32),
                   jax.ShapeDtypeStruct((S, A_KV_HEADS * HEAD_DIM), F32),
                   jax.ShapeDtypeStruct((S, A_KV_HEADS * HEAD_DIM), F32)],
        compiler_params=_cp("arbitrary", "arbitrary", "arbitrary"))(qkvh, qkvh, qkvh, do, o, lse)


def _bucket_tables():
    hs = _half_span()
    tq, kv = 2 * hs, 4 * hs
    nb = REL_BUCKETS // 2
    max_exact = nb // 2
    out = np.zeros((len(B_GROUPS), 3, tq, kv), np.int32)
    for g, (_, dil) in enumerate(B_GROUPS):
        for case, off in enumerate((0, hs, 2 * hs)):
            rel = np.arange(kv)[None, :] - np.arange(tq)[:, None] - off
            r = rel * dil
            n = np.abs(r)
            nf = np.maximum(n, 1).astype(np.float32)
            large = max_exact + (np.log(nf / np.float32(max_exact)) / np.float32(math.log(REL_MAX_DISTANCE / max_exact))
                                 * np.float32(nb - max_exact)).astype(np.int32)
            large = np.minimum(large, nb - 1)
            bucket = np.where(r > 0, nb, 0) + np.where(n < max_exact, n, large)
            out[g, case] = np.where(np.abs(rel) <= hs, bucket, -1)
    return out


def _bias_build(rel_bias, buckets):
    G, _, tq, kv = buckets.shape
    hg = B_HEADS_PER_GROUP

    def body(rb_ref, bk_ref, o_ref):
        col = pl.program_id(0) * hg + pl.program_id(2)
        bk = bk_ref[...]
        acc = jnp.full((tq, kv), NEG_INF, F32)
        for b in range(REL_BUCKETS):
            acc = jnp.where(bk == b, rb_ref[b, col], acc)
        o_ref[...] = acc

    return pl.pallas_call(
        body, name="bias_build", grid=(G, 3, hg),
        in_specs=[pl.BlockSpec(memory_space=pltpu.SMEM),
                  pl.BlockSpec((None, None, tq, kv), lambda g, c, h: (g, c, 0, 0))],
        out_specs=pl.BlockSpec((None, None, None, tq, kv), lambda g, c, h: (g, c, h, 0, 0)),
        out_shape=jax.ShapeDtypeStruct((G, 3, hg, tq, kv), F32),
        compiler_params=_cp("arbitrary", "arbitrary", "arbitrary"))(rel_bias, buckets)


def _bias_reduce(dbias_list, buckets):
    G, _, tq, kv = buckets.shape
    hg = B_HEADS_PER_GROUP
    n = len(dbias_list)

    def body(*refs):
        bk_ref, o_ref = refs[n], refs[n + 1]
        first = (pl.program_id(0) == 0) & (pl.program_id(1) == 0) & (pl.program_id(2) == 0)

        @pl.when(first)
        def _():
            o_ref[...] = jnp.zeros_like(o_ref)
        col = pl.program_id(0) * hg + pl.program_id(2)
        db = refs[0][...]
        for r in refs[1:n]:
            db = db + r[...]
        bk = bk_ref[...]
        rows = lax.broadcasted_iota(jnp.int32, (REL_BUCKETS, LANES), 0)
        cols = lax.broadcasted_iota(jnp.int32, (REL_BUCKETS, LANES), 1)
        acc = jnp.zeros((REL_BUCKETS, LANES), F32)
        for b in range(REL_BUCKETS):
            val = jnp.sum(jnp.sum(jnp.where(bk == b, db, 0.0), axis=1, keepdims=True), axis=0, keepdims=True)
            acc = acc + jnp.where((rows == b) & (cols == col), val, 0.0)
        o_ref[...] += acc

    tile = pl.BlockSpec((None, None, None, tq, kv), lambda g, c, h: (g, c, h, 0, 0))
    return pl.pallas_call(
        body, name="bias_reduce", grid=(G, 3, hg),
        in_specs=[tile] * n + [pl.BlockSpec((None, None, tq, kv), lambda g, c, h: (g, c, 0, 0))],
        out_specs=pl.BlockSpec((REL_BUCKETS, LANES), lambda g, c, h: (0, 0)),
        out_shape=jax.ShapeDtypeStruct((REL_BUCKETS, LANES), F32),
        compiler_params=_cp("arbitrary", "arbitrary", "arbitrary"))(*dbias_list, buckets)


def _b_specs(g, dil, L):
    hs = _half_span()
    tq = 2 * hs
    hg = B_HEADS_PER_GROUP
    nq = L // tq
    base = g * 3 * hg

    def qkv_col(which):
        return lambda c, h, i: c * (len(B_GROUPS) * 3 * hg) + base + which * hg + h

    q_spec = pl.BlockSpec((tq, HEAD_DIM), lambda c, h, i: (i, qkv_col(0)(c, h, i)))
    k_spec = pl.BlockSpec((L, HEAD_DIM), lambda c, h, i: (0, qkv_col(1)(c, h, i)))
    v_spec = pl.BlockSpec((L, HEAD_DIM), lambda c, h, i: (0, qkv_col(2)(c, h, i)))
    case = lambda i: jnp.where(i == 0, 0, jnp.where(i == nq - 1, 2, 1))
    bias_spec = pl.BlockSpec((None, None, None, tq, 2 * tq), lambda c, h, i: (g, case(i), h, 0, 0))
    blk_spec = pl.BlockSpec((tq, HEAD_DIM), lambda c, h, i: (i, c * hg + h))
    full_spec = pl.BlockSpec((L, HEAD_DIM), lambda c, h, i: (0, c * hg + h))
    return q_spec, k_spec, v_spec, bias_spec, blk_spec, full_spec, nq


def _battn_fwd(qkv, bias, g):
    S, W = qkv.shape
    dil = B_GROUPS[g][1]
    hs = _half_span()
    tq, kvl = 2 * hs, 4 * hs
    hg = B_HEADS_PER_GROUP
    L = S // dil
    q_spec, k_spec, v_spec, bias_spec, blk_spec, _, nq = _b_specs(g, dil, L)
    scale = HEAD_DIM ** -0.5
    view = qkv.reshape(L, dil * W)

    def body(q_ref, k_ref, v_ref, b_ref, o_ref, lz_ref):
        ks = pl.multiple_of(jnp.clip(pl.program_id(2) * tq - hs, 0, L - kvl), hs)
        kw = k_ref[pl.ds(ks, kvl), :].astype(BF16)
        vw = v_ref[pl.ds(ks, kvl), :].astype(BF16)
        s = lax.dot_general(q_ref[...].astype(BF16), kw, _NT, preferred_element_type=F32) * scale + b_ref[...]
        m = jnp.max(s, axis=-1, keepdims=True)
        p = jnp.exp(s - m)
        l = jnp.sum(p, axis=-1, keepdims=True)
        o_ref[...] = lax.dot_general(p.astype(BF16), vw, _NN, preferred_element_type=F32) / l
        lz_ref[...] = jnp.broadcast_to(m + jnp.log(l), lz_ref.shape)

    o, lz = pl.pallas_call(
        body, name="battn_fwd_g%d" % g, grid=(dil, hg, nq),
        in_specs=[q_spec, k_spec, v_spec, bias_spec], out_specs=[blk_spec, blk_spec],
        out_shape=[jax.ShapeDtypeStruct((L, dil * hg * HEAD_DIM), F32)] * 2,
        compiler_params=_cp("arbitrary", "arbitrary", "arbitrary"))(view, view, view, bias)
    return o.reshape(S, hg * HEAD_DIM), lz.reshape(S, hg * HEAD_DIM)


def _battn_bwd(qkv, bias, do, o, lz, dlz, g):
    S, W = qkv.shape
    dil = B_GROUPS[g][1]
    hs = _half_span()
    tq, kvl = 2 * hs, 4 * hs
    hg = B_HEADS_PER_GROUP
    L = S // dil
    q_spec, k_spec, v_spec, bias_spec, blk_spec, full_spec, nq = _b_specs(g, dil, L)
    scale = HEAD_DIM ** -0.5
    view = qkv.reshape(L, dil * W)
    gv = lambda a: a.reshape(L, dil * hg * HEAD_DIM)

    def body(q_ref, k_ref, v_ref, b_ref, do_ref, o_ref, lz_ref, dlz_ref, dq_ref, dk_ref, dv_ref, db_ref):
        c, h, i = pl.program_id(0), pl.program_id(1), pl.program_id(2)

        @pl.when((c == 0) & (h == 0) & (i == 0))
        def _():
            db_ref[...] = jnp.zeros_like(db_ref)

        @pl.when(i == 0)
        def _():
            dk_ref[...] = jnp.zeros_like(dk_ref)
            dv_ref[...] = jnp.zeros_like(dv_ref)
        ks = pl.multiple_of(jnp.clip(i * tq - hs, 0, L - kvl), hs)
        case = jnp.where(i == 0, 0, jnp.where(i == nq - 1, 2, 1))
        q = q_ref[...].astype(BF16)
        kw = k_ref[pl.ds(ks, kvl), :].astype(BF16)
        vw = v_ref[pl.ds(ks, kvl), :].astype(BF16)
        do_f = do_ref[...]
        do_b = do_f.astype(BF16)
        s = lax.dot_general(q, kw, _NT, preferred_element_type=F32) * scale + b_ref[...]
        p = jnp.exp(s - lz_ref[:, 0:1])
        dp = lax.dot_general(do_b, vw, _NT, preferred_element_type=F32)
        delta = jnp.sum(do_f * o_ref[...], axis=-1, keepdims=True)
        ds = p * (dp - delta + dlz_ref[:, 0:1])
        db_ref[case, h] += ds
        ds_b = (ds * scale).astype(BF16)
        dq_ref[...] = lax.dot_general(ds_b, kw, _NN, preferred_element_type=F32)
        dk_ref[pl.ds(ks, kvl), :] += lax.dot_general(ds_b, q, _TN, preferred_element_type=F32)
        dv_ref[pl.ds(ks, kvl), :] += lax.dot_general(p.astype(BF16), do_b, _TN, preferred_element_type=F32)

    grp = jax.ShapeDtypeStruct((L, dil * hg * HEAD_DIM), F32)
    dq, dk, dv, db = pl.pallas_call(
        body, name="battn_bwd_g%d" % g, grid=(dil, hg, nq),
        in_specs=[q_spec, k_spec, v_spec, bias_spec, blk_spec, blk_spec, blk_spec, blk_spec],
        out_specs=[blk_spec, full_spec, full_spec,
                   pl.BlockSpec((3, hg, tq, kvl), lambda c, h, i: (0, 0, 0, 0))],
        out_shape=[grp, grp, grp, jax.ShapeDtypeStruct((3, hg, tq, kvl), F32)],
        compiler_params=_cp("arbitrary", "arbitrary", "arbitrary"))(
            view, view, view, bias, gv(do), gv(o), gv(lz), gv(dlz))
    r = lambda a: a.reshape(S, hg * HEAD_DIM)
    return r(dq), r(dk), r(dv), db


def _group_weights(lz_refs, sl):
    z = [r[:, sl] for r in lz_refs]
    mx = functools.reduce(jnp.maximum, z)
    e = [jnp.exp(v - mx) for v in z]
    inv = 1.0 / functools.reduce(lambda a, b: a + b, e)
    return [v * inv for v in e]


def _combine_fwd(os_, lzs):
    G = len(os_)
    S, Wg = os_[0].shape
    hg = B_HEADS_PER_GROUP
    ts = _pick(S, (256, 128, 64, 32, 16))

    def body(*refs):
        o_refs, lz_refs, y_ref = refs[:G], refs[G:2 * G], refs[2 * G]
        for h in range(hg):
            sl = slice(h * HEAD_DIM, (h + 1) * HEAD_DIM)
            w = _group_weights(lz_refs, sl)
            for g in range(G):
                y_ref[:, (g * hg + h) * HEAD_DIM:(g * hg + h + 1) * HEAD_DIM] = (w[g] * o_refs[g][:, sl]).astype(BF16)

    row = pl.BlockSpec((ts, Wg), lambda i: (i, 0))
    return pl.pallas_call(
        body, name="combine_fwd", grid=(S // ts,), in_specs=[row] * (2 * G),
        out_specs=pl.BlockSpec((ts, G * Wg), lambda i: (i, 0)),
        out_shape=jax.ShapeDtypeStruct((S, G * Wg), BF16), compiler_params=_cp("arbitrary"))(*os_, *lzs)


def _combine_bwd(dy, os_, lzs):
    G = len(os_)
    S, Wg = os_[0].shape
    hg = B_HEADS_PER_GROUP
    ts = _pick(S, (256, 128, 64, 32, 16))

    def body(*refs):
        dy_ref, o_refs, lz_refs = refs[0], refs[1:1 + G], refs[1 + G:1 + 2 * G]
        do_refs, dlz_refs = refs[1 + 2 * G:1 + 3 * G], refs[1 + 3 * G:1 + 4 * G]
        for h in range(hg):
            sl = slice(h * HEAD_DIM, (h + 1) * HEAD_DIM)
            w = _group_weights(lz_refs, sl)
            dw = []
            for g in range(G):
                dyg = dy_ref[:, (g * hg + h) * HEAD_DIM:(g * hg + h + 1) * HEAD_DIM]
                dw.append(jnp.sum(dyg * o_refs[g][:, sl], axis=-1, keepdims=True))
                do_refs[g][:, sl] = w[g] * dyg
            tot = functools.reduce(lambda a, b: a + b, [w[g] * dw[g] for g in range(G)])
            for g in range(G):
                dlz_refs[g][:, sl] = w[g] * (dw[g] - tot)

    row = pl.BlockSpec((ts, Wg), lambda i: (i, 0))
    outs = pl.pallas_call(
        body, name="combine_bwd", grid=(S // ts,),
        in_specs=[pl.BlockSpec((ts, G * Wg), lambda i: (i, 0))] + [row] * (2 * G),
        out_specs=[row] * (2 * G), out_shape=[jax.ShapeDtypeStruct((S, Wg), F32)] * (2 * G),
        compiler_params=_cp("arbitrary"))(dy, *os_, *lzs)
    return outs[:G], outs[G:]


def _concat_cast(parts):
    S = parts[0].shape[0]
    widths = [p.shape[1] for p in parts]
    ts = _pick(S, (256, 128, 64, 32, 16))

    def body(*refs):
        o_ref = refs[len(parts)]
        off = 0
        for r, w in zip(refs, widths):
            o_ref[:, off:off + w] = r[...].astype(BF16)
            off += w

    return pl.pallas_call(
        body, name="concat_cast", grid=(S // ts,),
        in_specs=[pl.BlockSpec((ts, w), lambda i: (i, 0)) for w in widths],
        out_specs=pl.BlockSpec((ts, sum(widths)), lambda i: (i, 0)),
        out_shape=jax.ShapeDtypeStruct((S, sum(widths)), BF16), compiler_params=_cp("arbitrary"))(*parts)


def _ffn_specs(S, dff, cq, ts, tc, layer, order):
    nfc = dff // tc
    nps = cq // tc
    hb = ts // SUBLANES
    nrow8 = S // SUBLANES

    def u_main(half):
        return pl.BlockSpec((ts, tc), lambda *g: (order(*g)[0], order(*g)[1] % nfc + half * nfc))

    def u_prev(half):
        return pl.BlockSpec((SUBLANES, tc), lambda *g: (jnp.maximum(order(*g)[0] * hb - 1, 0),
                                                         order(*g)[1] % nfc + half * nfc))

    def u_next(half):
        return pl.BlockSpec((SUBLANES, tc), lambda *g: (jnp.minimum((order(*g)[0] + 1) * hb, nrow8 - 1),
                                                         order(*g)[1] % nfc + half * nfc))

    def cw(half):
        def im(*g):
            jj = order(*g)[1] % nfc + half * nfc
            return (layer, jj // nps, 0, jj % nps)
        return pl.BlockSpec((None, None, 3, tc), im)

    def cb(half):
        return pl.BlockSpec((None, 1, tc), lambda *g: (layer, 0, order(*g)[1] % nfc + half * nfc))

    return nfc, u_main, u_prev, u_next, cw, cb


def _ffn_act_fwd(u, cw_full, cb3, layer):
    S, two_dff = u.shape
    dff = two_dff // 2
    cq = cw_full.shape[3]
    ts = _pick(S, (512, 256, 128, 64, 32, 16))
    tc = _pick(cq, (256, 128))
    order = lambda j, i: (i, j)
    nfc, u_main, u_prev, u_next, cw, cb = _ffn_specs(S, dff, cq, ts, tc, layer, order)
    nrow = S // ts

    def body(ug, ugp, ugn, uv, uvp, uvn, wg, wv, bg, bv, a_ref):
        i = pl.program_id(1)
        row = lax.broadcasted_iota(jnp.int32, (ts, tc), 0)

        def conv(x_ref, p_ref, n_ref, w_ref, b_ref):
            x = x_ref[...]
            prev = jnp.where(i > 0, p_ref[SUBLANES - 1:SUBLANES, :], 0.0)
            nxt = jnp.where(i < nrow - 1, n_ref[0:1, :], 0.0)
            xm = jnp.where(row == 0, prev, pltpu.roll(x, 1, 0))
            xp = jnp.where(row == ts - 1, nxt, pltpu.roll(x, ts - 1, 0))
            return w_ref[0:1, :] * xm + w_ref[1:2, :] * x + w_ref[2:3, :] * xp + b_ref[...]

        gc = conv(ug, ugp, ugn, wg, bg)
        vc = conv(uv, uvp, uvn, wv, bv)
        a_ref[...] = (gc * (1.0 / (1.0 + jnp.exp(-gc))) * vc).astype(BF16)

    return pl.pallas_call(
        body, name="ffn_act_fwd", grid=(nfc, nrow),
        in_specs=[u_main(0), u_prev(0), u_next(0), u_main(1), u_prev(1), u_next(1), cw(0), cw(1), cb(0), cb(1)],
        out_specs=pl.BlockSpec((ts, tc), lambda j, i: (i, j)),
        out_shape=jax.ShapeDtypeStruct((S, dff), BF16),
        compiler_params=_cp("arbitrary", "arbitrary"))(u, u, u, u, u, u, cw_full, cw_full, cb3, cb3)


def _ffn_act_bwd(u, da, cw_full, cb3, layer):
    S, two_dff = u.shape
    dff = two_dff // 2
    cq = cw_full.shape[3]
    ts = _pick(S, (512, 256, 128, 64, 32, 16))
    tc = _pick(cq, (256, 128))
    order = lambda j, i: (i, j)
    nfc, u_main, u_prev, u_next, cw, cb = _ffn_specs(S, dff, cq, ts, tc, layer, order)
    nrow = S // ts
    hb = ts // SUBLANES
    te = ts + 2 * SUBLANES
    da_main = pl.BlockSpec((ts, tc), lambda j, i: (i, j))
    da_prev = pl.BlockSpec((SUBLANES, tc), lambda j, i: (jnp.maximum(i * hb - 1, 0), j))
    da_next = pl.BlockSpec((SUBLANES, tc), lambda j, i: (jnp.minimum((i + 1) * hb, S // SUBLANES - 1), j))
    main = slice(SUBLANES, SUBLANES + ts)

    def body(ug, ugp, ugn, uv, uvp, uvn, dam, dap, dan, wg, wv, bg, bv, dug_ref, duv_ref, accg_ref, accv_ref):
        i = pl.program_id(1)

        @pl.when(i == 0)
        def _():
            accg_ref[...] = jnp.zeros_like(accg_ref)
            accv_ref[...] = jnp.zeros_like(accv_ref)

        def ext(m, p, n):
            return jnp.concatenate([jnp.where(i > 0, p[...], 0.0), m[...], jnp.where(i < nrow - 1, n[...], 0.0)], axis=0)

        def shift(x):
            return pltpu.roll(x, 1, 0), pltpu.roll(x, te - 1, 0)

        xg, xv, dae = ext(ug, ugp, ugn), ext(uv, uvp, uvn), ext(dam, dap, dan)
        xgm, xgp = shift(xg)
        xvm, xvp = shift(xv)
        gc = wg[0:1, :] * xgm + wg[1:2, :] * xg + wg[2:3, :] * xgp + bg[...]
        vc = wv[0:1, :] * xvm + wv[1:2, :] * xv + wv[2:3, :] * xvp + bv[...]
        sig = 1.0 / (1.0 + jnp.exp(-gc))
        silu = gc * sig
        dcg = dae * vc * (sig * (1.0 + gc * (1.0 - sig)))
        dcv = dae * silu

        def finish(dc, x, xm, xp, w_ref, du_ref, acc_ref):
            dm, dp = shift(dc)
            du = w_ref[0:1, :] * dp + w_ref[1:2, :] * dc + w_ref[2:3, :] * dm
            du_ref[...] = du[main, :].astype(BF16)
            dcm = dc[main, :]
            acc_ref[0:1, :] += jnp.sum(dcm * xm[main, :], axis=0, keepdims=True)
            acc_ref[1:2, :] += jnp.sum(dcm * x[main, :], axis=0, keepdims=True)
            acc_ref[2:3, :] += jnp.sum(dcm * xp[main, :], axis=0, keepdims=True)
            acc_ref[3:4, :] += jnp.sum(dcm, axis=0, keepdims=True)

        finish(dcg, xg, xgm, xgp, wg, dug_ref, accg_ref)
        finish(dcv, xv, xvm, xvp, wv, duv_ref, accv_ref)

    blk = pl.BlockSpec((ts, tc), lambda j, i: (i, j))
    acc = pl.BlockSpec((SUBLANES, tc), lambda j, i: (0, j))
    dug, duv, accg, accv = pl.pallas_call(
        body, name="ffn_act_bwd", grid=(nfc, nrow),
        in_specs=[u_main(0), u_prev(0), u_next(0), u_main(1), u_prev(1), u_next(1), da_main, da_prev, da_next,
                  cw(0), cw(1), cb(0), cb(1)],
        out_specs=[blk, blk, acc, acc],
        out_shape=[jax.ShapeDtypeStruct((S, dff), BF16)] * 2 + [jax.ShapeDtypeStruct((SUBLANES, dff), F32)] * 2,
        compiler_params=_cp("arbitrary", "arbitrary"))(u, u, u, u, u, u, da, da, da, cw_full, cw_full, cb3, cb3)
    return (dug, duv), jnp.concatenate([accg, accv], axis=1)


def _my_chip():
    return 2 * lax.axis_index("x") + lax.axis_index("y")


def _into_full(w, layer, dtype):
    L, a, b = w.shape
    tr = _pick(a, (512, 256, 128, 64, 32, 16, 8))

    def body(w_ref, o_ref):
        o_ref[...] = w_ref[...].astype(dtype)

    return pl.pallas_call(
        body, name="into_full", grid=(a // tr,),
        in_specs=[pl.BlockSpec((None, tr, b), lambda i: (layer, i, 0))],
        out_specs=pl.BlockSpec((None, None, tr, b), lambda i: (0, _my_chip(), i, 0)),
        out_shape=jax.ShapeDtypeStruct((1, N_CHIPS, a, b), dtype),
        compiler_params=_cp("arbitrary"))(w)


def _adam_math(w, g, m, v):
    m = ADAM_B1 * m + (1.0 - ADAM_B1) * g
    v = ADAM_B2 * v + (1.0 - ADAM_B2) * (g * g)
    m_hat = m / (1.0 - ADAM_B1 ** ADAM_STEP)
    v_hat = v / (1.0 - ADAM_B2 ** ADAM_STEP)
    delta = -ADAM_LR * (m_hat / (jnp.sqrt(v_hat) + ADAM_EPS) + ADAM_WD * w)
    return delta, m, v


def _adamw(w, g, m, v):
    R, C = w.shape
    tr = _pick(R, (128, 64, 32, 16, 8)) if R % SUBLANES == 0 and C % LANES == 0 else R

    def body(w_ref, g_ref, m_ref, v_ref, d_ref, nm_ref, nv_ref):
        d, nm, nv = _adam_math(w_ref[...], g_ref[...], m_ref[...], v_ref[...])
        d_ref[...] = d
        nm_ref[...] = nm
        nv_ref[...] = nv

    spec = pl.BlockSpec((tr, C), lambda i: (i, 0))
    return pl.pallas_call(
        body, name="adamw", grid=(R // tr,), in_specs=[spec] * 4, out_specs=[spec] * 3,
        out_shape=[jax.ShapeDtypeStruct((R, C), F32)] * 3, compiler_params=_cp("arbitrary"))(w, g, m, v)


ANY = pl.BlockSpec(memory_space=pl.ANY)


def _position():
    x, y, c = lax.axis_index("x"), lax.axis_index("y"), lax.axis_index("c")
    chips = [(1 - x, y), (x, 1 - y), (1 - x, 1 - y)]
    return x, y, c, chips


HBM = pl.BlockSpec(memory_space=pltpu.HBM)
SEM = pl.BlockSpec(memory_space=pltpu.SEMAPHORE)
EFFECT = pltpu.SideEffectType.DATAFLOW_SIDE_EFFECTING


def _in_hbm(a):
    return pltpu.with_memory_space_constraint(a, pltpu.HBM)


def _shard_half(buf, shape, chip, half):
    _, _, a, b = shape
    p = 2 * chip[0] + chip[1]
    if a % (4 * SUBLANES) == 0:
        return buf.at[0, p, pl.ds(half * (a // 2), a // 2)]
    return buf.at[0, p, :, pl.ds(half * (b // 2), b // 2)]


def _gather_copy(buf, shape, chip, half, to, send, recv, k):
    part = _shard_half(buf, shape, chip, half)
    return pltpu.make_async_remote_copy(src_ref=part, dst_ref=part, send_sem=send.at[k], recv_sem=recv.at[k],
                                        device_id=to, device_id_type=MESH)


def _gather_start(fulls, name):
    n = len(fulls)

    def body(*refs):
        send, recv, buf, token = refs[n], refs[n + 1], refs[n + 2:2 * n + 2], refs[2 * n + 2]
        x, y, c, chips = _position()
        for t in range(n):
            for j in range(3):
                _gather_copy(buf[t], fulls[t].shape, (x, y), c, (*chips[j], c), send, recv, 3 * t + j).start()
        token[...] = jnp.zeros_like(token)

    outs = pl.pallas_call(
        body, name=name, in_specs=[HBM] * n,
        out_specs=[SEM, SEM] + [HBM] * n + [pl.BlockSpec(memory_space=pltpu.VMEM)],
        out_shape=[pltpu.SemaphoreType.DMA((3 * n,)), pltpu.SemaphoreType.DMA((3 * n,))]
        + [pltpu.HBM(f.shape, f.dtype) for f in fulls] + [jax.ShapeDtypeStruct((SUBLANES, LANES), F32)],
        input_output_aliases={t: 2 + t for t in range(n)},
        compiler_params=pltpu.CompilerParams(has_side_effects=EFFECT))(*[_in_hbm(f) for f in fulls])
    return outs[0], outs[1], list(outs[2:2 + n]), outs[2 + n]


def _gather_wait(send, recv, fulls, after, name):
    n = len(fulls)

    def body(*refs):
        buf, send_ref, recv_ref = refs[:n], refs[n], refs[n + 1]
        x, y, c, chips = _position()
        for t in range(n):
            for j in range(3):
                _gather_copy(buf[t], fulls[t].shape, (x, y), c, (*chips[j], c), send_ref, recv_ref, 3 * t + j).wait_send()
                _gather_copy(buf[t], fulls[t].shape, chips[j], c, (*chips[j], c), send_ref, recv_ref, 3 * t + j).wait_recv()

    outs = pl.pallas_call(
        body, name=name, in_specs=[HBM] * n + [SEM, SEM, ANY], out_specs=[HBM] * n,
        out_shape=[pltpu.HBM(f.shape, f.dtype) for f in fulls],
        input_output_aliases={t: t for t in range(n)},
        compiler_params=pltpu.CompilerParams(has_side_effects=EFFECT))(*fulls, send, recv, after)
    return list(outs)


def _gather_forward(fulls):
    n = len(fulls)

    def body(*refs):
        buf, send, recv = refs[n:2 * n], refs[2 * n], refs[2 * n + 1]
        x, y, c, chips = _position()
        sib = (x, y, 1 - c)
        cps = [_gather_copy(buf[t], fulls[t].shape, chips[j], c, sib, send, recv, 3 * t + j)
               for t in range(n) for j in range(3)]
        for cp in cps:
            cp.start()
        for t in range(n):
            for j in range(3):
                _gather_copy(buf[t], fulls[t].shape, chips[j], 1 - c, sib, send, recv, 3 * t + j).wait_recv()
        for cp in cps:
            cp.wait_send()

    return pl.pallas_call(
        body, name="gather_forward", in_specs=[ANY] * n, out_specs=[ANY] * n,
        out_shape=[jax.ShapeDtypeStruct(f.shape, f.dtype) for f in fulls],
        input_output_aliases={t: t for t in range(n)},
        scratch_shapes=[pltpu.SemaphoreType.DMA((3 * n,)), pltpu.SemaphoreType.DMA((3 * n,))])(*fulls)


def _allreduce_small(part):
    M, C = part.shape
    n_dev = 2 * N_CHIPS

    def body(x_ref, sum_ref, all_ref, send, recv, local):
        x, y, c, chips = _position()
        me, sib = (x, y, c), (x, y, 1 - c)

        def rows(px, py, pc):
            return all_ref.at[pl.ds((4 * px + 2 * py + pc) * M, M), :]

        def copy(k, block, to, src=None):
            return pltpu.make_async_remote_copy(
                src_ref=rows(*block) if src is None else src, dst_ref=rows(*block),
                send_sem=send.at[k], recv_sem=recv.at[k], device_id=to, device_id_type=MESH)

        mine = pltpu.make_async_copy(x_ref, rows(*me), local)
        mine.start()
        first = [copy(0, me, sib, src=x_ref)] + [copy(1 + j, me, (*chip, c), src=x_ref) for j, chip in enumerate(chips)]
        for cp in first:
            cp.start()
        passed = [copy(4 + j, (*chip, c), sib) for j, chip in enumerate(chips)]
        for j, chip in enumerate(chips):
            copy(1 + j, (*chip, c), me).wait_recv()
            passed[j].start()
        copy(0, sib, me).wait_recv()
        for j, chip in enumerate(chips):
            copy(4 + j, (*chip, 1 - c), me).wait_recv()
        for cp in first + passed:
            cp.wait_send()
        mine.wait()
        acc = all_ref[0:M, :]
        for d in range(1, n_dev):
            acc = acc + all_ref[d * M:(d + 1) * M, :]
        sum_ref[...] = acc

    vm = pl.BlockSpec(memory_space=pltpu.VMEM)
    return pl.pallas_call(
        body, name="allreduce_small", in_specs=[vm], out_specs=[vm],
        out_shape=[jax.ShapeDtypeStruct((M, C), F32)],
        scratch_shapes=[pltpu.VMEM((n_dev * M, C), F32), pltpu.SemaphoreType.DMA((7,)),
                        pltpu.SemaphoreType.DMA((7,)), pltpu.SemaphoreType.DMA],
        compiler_params=pltpu.CompilerParams(vmem_limit_bytes=VMEM_LIMIT))(part)[0]


def _rs_sibling(grads):
    n = len(grads)

    def body(*refs):
        src, dst, send, recv = refs[:n], refs[n:2 * n], refs[2 * n], refs[2 * n + 1]
        x, y, c, _ = _position()
        cps = [pltpu.make_async_remote_copy(
            src_ref=src[t].at[:, 1 - c], dst_ref=dst[t], send_sem=send.at[t], recv_sem=recv.at[t],
            device_id=(x, y, 1 - c), device_id_type=MESH) for t in range(n)]
        for cp in cps:
            cp.start()
        for cp in cps:
            cp.wait()

    return pl.pallas_call(
        body, name="rs_sibling", in_specs=[ANY] * n, out_specs=[ANY] * n,
        out_shape=[jax.ShapeDtypeStruct((g.shape[0],) + g.shape[2:], BF16) for g in grads],
        scratch_shapes=[pltpu.SemaphoreType.DMA((n,)), pltpu.SemaphoreType.DMA((n,))])(*grads)


def _add_sibling(grad, got):
    _, _, R, C = grad.shape
    tr = _pick(R, (512, 256, 128, 64, 32, 16))

    def body(g_ref, r_ref, o_ref):
        o_ref[...] = (g_ref[...].astype(F32) + r_ref[...].astype(F32)).astype(BF16)

    return pl.pallas_call(
        body, name="add_sibling", grid=(N_CHIPS, R // tr),
        in_specs=[pl.BlockSpec((None, None, tr, C), lambda p, i: (p, lax.axis_index("c"), i, 0)),
                  pl.BlockSpec((None, tr, C), lambda p, i: (p, i, 0))],
        out_specs=pl.BlockSpec((None, tr, C), lambda p, i: (p, i, 0)),
        out_shape=jax.ShapeDtypeStruct((N_CHIPS, R, C), BF16),
        compiler_params=_cp("arbitrary", "arbitrary"))(grad, got)


def _rs_chips_copy(src, dst, chips, c, send, recv, t, j):
    return pltpu.make_async_remote_copy(
        src_ref=src.at[2 * chips[j][0] + chips[j][1]], dst_ref=dst.at[j],
        send_sem=send.at[3 * t + j], recv_sem=recv.at[3 * t + j], device_id=(*chips[j], c), device_id_type=MESH)


def _rs_chips_start(sums, name):
    n = len(sums)
    lands = [lax.empty((3,) + s.shape[1:], BF16) for s in sums]

    def body(*refs):
        send, recv = refs[2 * n], refs[2 * n + 1]
        src, dst, token = refs[2 * n + 2:3 * n + 2], refs[3 * n + 2:4 * n + 2], refs[4 * n + 2]
        x, y, c, chips = _position()
        for t in range(n):
            for j in range(3):
                _rs_chips_copy(src[t], dst[t], chips, c, send, recv, t, j).start()
        token[...] = jnp.zeros_like(token)

    outs = pl.pallas_call(
        body, name=name, in_specs=[HBM] * (2 * n),
        out_specs=[SEM, SEM] + [HBM] * (2 * n) + [pl.BlockSpec(memory_space=pltpu.VMEM)],
        out_shape=[pltpu.SemaphoreType.DMA((3 * n,)), pltpu.SemaphoreType.DMA((3 * n,))]
        + [pltpu.HBM(a.shape, a.dtype) for a in sums + lands] + [jax.ShapeDtypeStruct((SUBLANES, LANES), F32)],
        input_output_aliases={t: 2 + t for t in range(2 * n)},
        compiler_params=pltpu.CompilerParams(has_side_effects=EFFECT))(*[_in_hbm(a) for a in sums + lands])
    return outs[0], outs[1], list(outs[2:2 + n]), list(outs[2 + n:2 + 2 * n]), outs[2 + 2 * n]


def _rs_chips_wait(send, recv, sums, lands, after, name):
    n = len(sums)

    def body(*refs):
        src, dst, send_ref, recv_ref = refs[:n], refs[n:2 * n], refs[2 * n], refs[2 * n + 1]
        x, y, c, chips = _position()
        for t in range(n):
            for j in range(3):
                cp = _rs_chips_copy(src[t], dst[t], chips, c, send_ref, recv_ref, t, j)
                cp.wait_send()
                cp.wait_recv()

    outs = pl.pallas_call(
        body, name=name, in_specs=[HBM] * (2 * n) + [SEM, SEM, ANY], out_specs=[HBM] * (2 * n),
        out_shape=[pltpu.HBM(a.shape, a.dtype) for a in sums + lands],
        input_output_aliases={t: t for t in range(2 * n)},
        compiler_params=pltpu.CompilerParams(has_side_effects=EFFECT))(*sums, *lands, send, recv, after)
    return list(outs[n:])


def _add_chips(grad, got_sib, got_chips, stack, layer):
    _, _, R, C = grad.shape
    tr = _pick(R, (256, 128, 64, 32, 16))

    def body(g_ref, s_ref, r_ref, stack_ref, o_ref):
        acc = g_ref[...].astype(F32) + s_ref[...].astype(F32)
        for j in range(3):
            acc = acc + r_ref[j].astype(F32)
        o_ref[...] = acc

    return pl.pallas_call(
        body, name="add_chips", grid=(R // tr,),
        in_specs=[pl.BlockSpec((None, None, tr, C), lambda i: (_my_chip(), lax.axis_index("c"), i, 0)),
                  pl.BlockSpec((None, tr, C), lambda i: (_my_chip(), i, 0)),
                  pl.BlockSpec((3, tr, C), lambda i: (0, i, 0)),
                  ANY],
        out_specs=pl.BlockSpec((None, None, tr, C), lambda i: (layer, lax.axis_index("c"), i, 0)),
        out_shape=jax.ShapeDtypeStruct(stack.shape, F32), input_output_aliases={3: 0},
        compiler_params=_cp("arbitrary"))(grad, got_sib, got_chips, stack)


def _ag_sibling(stacks):
    n = len(stacks)
    offs = np.cumsum([0] + [s.shape[0] for s in stacks])

    def body(*refs):
        buf, send, recv = refs[n:2 * n], refs[2 * n], refs[2 * n + 1]
        x, y, c, _ = _position()

        def copy(t, l, half):
            part = buf[t].at[l, half]
            return pltpu.make_async_remote_copy(
                src_ref=part, dst_ref=part, send_sem=send.at[int(offs[t]) + l], recv_sem=recv.at[int(offs[t]) + l],
                device_id=(x, y, 1 - c), device_id_type=MESH)

        cps = [copy(t, l, c) for t in range(n) for l in range(stacks[t].shape[0])]
        for cp in cps:
            cp.start()
        for t in range(n):
            for l in range(stacks[t].shape[0]):
                copy(t, l, 1 - c).wait_recv()
        for cp in cps:
            cp.wait_send()

    return pl.pallas_call(
        body, name="ag_sibling", in_specs=[ANY] * n, out_specs=[ANY] * n,
        out_shape=[jax.ShapeDtypeStruct(s.shape, F32) for s in stacks],
        input_output_aliases={t: t for t in range(n)},
        scratch_shapes=[pltpu.SemaphoreType.DMA((int(offs[-1]),)), pltpu.SemaphoreType.DMA((int(offs[-1]),))])(*stacks)


def _split8(dw, blocked):
    if blocked:
        p, k, nq = dw.shape
        return dw.reshape(p, 2, k // 2, nq)
    k, n = dw.shape
    return dw.reshape(N_CHIPS, 2, k // (2 * N_CHIPS), n)


def kernel(x, a_w_qkv, a_w_o, a_q_gain, a_k_gain, b_w_qkv, b_w_o, rel_bias, mix_norm, ffn_norm, w_up, conv_w, conv_b, w_down, final_norm, loss_target, m_a_w_qkv, m_a_w_o, m_a_q_gain, m_a_k_gain, m_b_w_qkv, m_b_w_o, m_rel_bias, m_mix_norm, m_ffn_norm, m_w_up, m_conv_w, m_conv_b, m_w_down, m_final_norm, v_a_w_qkv, v_a_w_o, v_a_q_gain, v_a_k_gain, v_b_w_qkv, v_b_w_o, v_rel_bias, v_mix_norm, v_ffn_norm, v_w_up, v_conv_w, v_conv_b, v_w_down, v_final_norm):
    S, D = x.shape[1], x.shape[2]
    h = x.reshape(S, D)
    target = loss_target.reshape(S, D)
    hg = B_HEADS_PER_GROUP
    G = len(B_GROUPS)
    n_a, n_b = a_w_qkv.shape[0], b_w_qkv.shape[0]
    depth = w_up.shape[0]
    cx, cy = lax.axis_index("x"), lax.axis_index("y")

    big = dict(a_w_qkv=a_w_qkv, a_w_o=a_w_o, b_w_qkv=b_w_qkv, b_w_o=b_w_o, w_up=w_up, w_down=w_down)
    blocked = dict(a_w_qkv=True, a_w_o=False, b_w_qkv=True, b_w_o=False, w_up=True, w_down=False)
    names = list(big)
    srcs = dict(big, conv_w=conv_w)
    started = []
    for i in range(depth):
        mix = [("a_w_qkv", i // 2), ("a_w_o", i // 2)] if i % 2 == 0 else [("b_w_qkv", i // 2), ("b_w_o", i // 2)]
        keys_i = mix + [("w_up", i), ("conv_w", i), ("w_down", i)]
        bufs = [_into_full(srcs[k], l, F32 if k == "conv_w" else BF16) for k, l in keys_i]
        started.append((keys_i,) + _gather_start(bufs, "gather_start_%d" % i))
    cb3 = conv_b.reshape(depth, 1, conv_b.shape[1])

    cos, sin = _rope_tables(S)
    buckets = jnp.asarray(_bucket_tables())
    bias = _bias_build(rel_bias, buckets)

    saved = []
    for i in range(depth):
        j = i // 2
        keys_i, send, recv, bufs, _ = started[i]
        bufs = _gather_forward(_gather_wait(send, recv, bufs, h, "gather_wait_%d" % i))
        wl = {}
        for (k, _), buf in zip(keys_i, bufs):
            _, _, a, b = buf.shape
            wl[k] = buf if k == "conv_w" or blocked[k] else buf.reshape(1, N_CHIPS * a, b)
        sv = dict(h0=h, w=wl)
        hn = _rms_fwd(h, mix_norm[i:i + 1], after=[s[4] for s in started] if i == 0 else ())
        sv["hn"] = hn
        if i % 2 == 0:
            qkv = _mm_nn(hn, wl["a_w_qkv"], 0, blocked=True, name="a_qkv")
            qkvh = _prep_a_fwd(qkv, cos, sin, a_q_gain[j:j + 1], a_k_gain[j:j + 1])
            o, lse = _flash_a_fwd(qkvh)
            sv.update(qkv=qkv, qkvh=qkvh, o=o, lse=lse)
            h = _mm_nn(o, wl["a_w_o"], 0, blocked=False, res=h, name="a_out")
        else:
            qkv = _mm_nn(hn, wl["b_w_qkv"], 0, blocked=True, name="b_qkv")
            os_, lzs = [], []
            for g in range(G):
                o_g, lz_g = _battn_fwd(qkv, bias, g)
                os_.append(o_g)
                lzs.append(lz_g)
            y = _combine_fwd(os_, lzs)
            sv.update(qkv=qkv, os=os_, lzs=lzs, y=y)
            h = _mm_nn(y, wl["b_w_o"], 0, blocked=False, res=h, name="b_out")
        sv["h1"] = h
        hf = _rms_fwd(h, ffn_norm[i:i + 1])
        u = _mm_nn(hf, wl["w_up"], 0, blocked=True, name="ffn_up")
        act = _ffn_act_fwd(u, wl["conv_w"], cb3[i:i + 1], 0)
        sv.update(hf=hf, u=u, act=act)
        h = _mm_nn(act, wl["w_down"], 0, blocked=False, res=h, name="ffn_down")
        saved.append(sv)

    loss_blk, dh, dh_b, dg_final = _final_loss(h, final_norm.reshape(1, D), target)

    dws = {k: [None] * big[k].shape[0] for k in names}
    d_mix, d_ffn, d_convw, d_convb = [None] * depth, [None] * depth, [None] * depth, [None] * depth
    d_gq, d_gk = [None] * n_a, [None] * n_a
    dbias_list = []
    pending = []

    def start_reduce(keys, tag):
        pieces = [_split8(dws[k][l], blocked[k]) for k, l in keys]
        got_sib = _rs_sibling(pieces)
        sums = [_add_sibling(p, r) for p, r in zip(pieces, got_sib)]
        send, recv, sums, lands, token = _rs_chips_start(sums, "rs_chips_start_" + tag)
        pending.append((keys, pieces, got_sib, send, recv, sums, lands, tag))
        return (token,)

    tok = ()
    for i in reversed(range(depth)):
        j = i // 2
        sv = saved[i]
        wl = sv["w"]
        da = _mm_nt(dh_b, wl["w_down"], 0, blocked=False, name="ffn_down_dx", after=tok)
        dws["w_down"][i] = _mm_tn(sv["act"], dh_b, blocked=False, name="ffn_down_dw")
        du, dconv = _ffn_act_bwd(sv["u"], da, wl["conv_w"], cb3[i:i + 1], 0)
        d_convw[i], d_convb[i] = dconv[0:3], dconv[3]
        dhf = _mm_nt(du, wl["w_up"], 0, blocked=True, name="ffn_up_dx")
        dws["w_up"][i] = _mm_tn(sv["hf"], du, blocked=True, name="ffn_up_dw")
        dh, dh_b, dg = _rms_bwd(dhf, sv["h1"], ffn_norm[i:i + 1], dh)
        d_ffn[i] = dg[0]
        tok = start_reduce([("w_down", i), ("w_up", i)], "ffn%d" % i)
        if i % 2 == 0:
            do = _mm_nt(dh_b, wl["a_w_o"], 0, blocked=False, name="a_out_dx", after=tok)
            dws["a_w_o"][j] = _mm_tn(sv["o"], dh_b, blocked=False, name="a_out_dw")
            dq, dk, dv = _flash_a_bwd(sv["qkvh"], do, sv["o"], sv["lse"])
            dqkv, dgain = _prep_a_bwd(dq, dk, dv, sv["qkv"], cos, sin, a_q_gain[j:j + 1], a_k_gain[j:j + 1])
            d_gq[j], d_gk[j] = dgain[0], dgain[1]
            dhn = _mm_nt(dqkv, wl["a_w_qkv"], 0, blocked=True, name="a_qkv_dx")
            dws["a_w_qkv"][j] = _mm_tn(sv["hn"], dqkv, blocked=True, name="a_qkv_dw")
            mix_keys = [("a_w_o", j), ("a_w_qkv", j)]
        else:
            dy = _mm_nt(dh_b, wl["b_w_o"], 0, blocked=False, name="b_out_dx", after=tok)
            dws["b_w_o"][j] = _mm_tn(sv["y"], dh_b, blocked=False, name="b_out_dw")
            dos, dlzs = _combine_bwd(dy, sv["os"], sv["lzs"])
            parts = []
            for g in range(G):
                dq, dk, dv, db = _battn_bwd(sv["qkv"], bias, dos[g], sv["os"][g], sv["lzs"][g], dlzs[g], g)
                parts += [dq, dk, dv]
                dbias_list.append((g, db))
            dqkv = _concat_cast(parts)
            dhn = _mm_nt(dqkv, wl["b_w_qkv"], 0, blocked=True, name="b_qkv_dx")
            dws["b_w_qkv"][j] = _mm_tn(sv["hn"], dqkv, blocked=True, name="b_qkv_dw")
            mix_keys = [("b_w_o", j), ("b_w_qkv", j)]
        dh, dh_b, dg = _rms_bwd(dhn, sv["h0"], mix_norm[i:i + 1], dh)
        d_mix[i] = dg[0]
        tok = start_reduce(mix_keys, "mix%d" % i)
    grad_x = dh.reshape(x.shape)

    dbias_layers = [jnp.stack([db for g2, db in dbias_list[l * G:(l + 1) * G]]) for l in range(n_b)]
    d_rel = _bias_reduce(dbias_layers, buckets)[:, :G * hg]

    small = [jnp.stack(d_gq), jnp.stack(d_gk), d_rel, jnp.stack(d_mix), jnp.stack(d_ffn), jnp.stack(d_convw),
             jnp.stack(d_convb), dg_final[0]]
    sizes = [int(np.prod(s.shape)) for s in small]
    flat = jnp.concatenate([s.reshape(-1) for s in small])
    rows = -(-flat.shape[0] // (LANES * SUBLANES)) * SUBLANES
    flat = jnp.pad(flat, (0, rows * LANES - flat.shape[0])).reshape(rows, LANES)
    tot = _allreduce_small(flat).reshape(-1)
    offs = np.cumsum([0] + sizes)
    g_gq, g_gk, g_rel, g_mix, g_ffn, g_convw_full, g_convb, g_final = [
        tot[offs[k]:offs[k + 1]].reshape(small[k].shape) for k in range(len(small))]
    cq = conv_w.shape[2]
    g_convw = lax.dynamic_slice_in_dim(g_convw_full, (2 * cx + cy) * cq, cq, axis=2)

    stacks = {}
    for keys, pieces, got_sib, send, recv, sums, lands, tag in pending:
        got_chips = _rs_chips_wait(send, recv, sums, lands, dh, "rs_chips_wait_" + tag)
        for (k, l), p, r, rc in zip(keys, pieces, got_sib, got_chips):
            if k not in stacks:
                stacks[k] = lax.empty((big[k].shape[0], 2) + p.shape[2:], F32)
            stacks[k] = _add_chips(p, r, rc, stacks[k], l)
    shard_grads = {k: gs.reshape(big[k].shape) for k, gs in zip(names, _ag_sibling([stacks[k] for k in names]))}

    grads = dict(shard_grads, a_q_gain=g_gq, a_k_gain=g_gk, rel_bias=g_rel, mix_norm=g_mix, ffn_norm=g_ffn,
                 conv_w=g_convw, conv_b=g_convb, final_norm=g_final)
    weights = dict(a_w_qkv=a_w_qkv, a_w_o=a_w_o, a_q_gain=a_q_gain, a_k_gain=a_k_gain, b_w_qkv=b_w_qkv, b_w_o=b_w_o,
                   rel_bias=rel_bias, mix_norm=mix_norm, ffn_norm=ffn_norm, w_up=w_up, conv_w=conv_w, conv_b=conv_b,
                   w_down=w_down, final_norm=final_norm)
    ms = dict(a_w_qkv=m_a_w_qkv, a_w_o=m_a_w_o, a_q_gain=m_a_q_gain, a_k_gain=m_a_k_gain, b_w_qkv=m_b_w_qkv,
              b_w_o=m_b_w_o, rel_bias=m_rel_bias, mix_norm=m_mix_norm, ffn_norm=m_ffn_norm, w_up=m_w_up,
              conv_w=m_conv_w, conv_b=m_conv_b, w_down=m_w_down, final_norm=m_final_norm)
    vs = dict(a_w_qkv=v_a_w_qkv, a_w_o=v_a_w_o, a_q_gain=v_a_q_gain, a_k_gain=v_a_k_gain, b_w_qkv=v_b_w_qkv,
              b_w_o=v_b_w_o, rel_bias=v_rel_bias, mix_norm=v_mix_norm, ffn_norm=v_ffn_norm, w_up=v_w_up,
              conv_w=v_conv_w, conv_b=v_conv_b, w_down=v_w_down, final_norm=v_final_norm)
    deltas, new_m, new_v = {}, {}, {}
    for k, w in weights.items():
        two_d = (-1, w.shape[-1])
        d, nm, nv = _adamw(w.reshape(two_d), grads[k].reshape(two_d), ms[k].reshape(two_d), vs[k].reshape(two_d))
        deltas[k], new_m[k], new_v[k] = d.reshape(w.shape), nm.reshape(w.shape), nv.reshape(w.shape)

    loss = lax.psum(loss_blk[0, 0], ("x", "y", "c"))
    keys = list(weights)
    return (loss, grad_x, *[grads[k].reshape(weights[k].shape) for k in keys], *[deltas[k] for k in keys],
            *[new_m[k] for k in keys], *[new_v[k] for k in keys])
```

```python
import functools
import math

import numpy as np
import jax
import jax.numpy as jnp
from jax import lax
from jax.experimental import pallas as pl
from jax.experimental.pallas import tpu as pltpu

F32 = jnp.float32
BF16 = jnp.bfloat16

HEAD_DIM = 128
A_HEADS = 16
A_KV_HEADS = 4
GRID_W = 64
ROPE_THETA = 10000.0
B_GROUPS = ((128, 1), (512, 4), (2048, 16))
B_HEADS_PER_GROUP = 8
REL_BUCKETS = 32
REL_MAX_DISTANCE = 1024
EPS = 1e-6
NEG_INF = -1e30
DEPTH = 4
ADAM_LR = 0.001
ADAM_B1 = 0.9
ADAM_B2 = 0.999
ADAM_EPS = 1e-08
ADAM_WD = 0.01
ADAM_STEP = 10

N_CHIPS = 4
LANES = 128
SUBLANES = 8
VMEM_LIMIT = 52 * 1024 * 1024
MESH = pl.DeviceIdType.MESH


def _pick(n, cands):
    for c in cands:
        if c <= n and n % c == 0:
            return c
    return n


def _cp(*sem):
    return pltpu.CompilerParams(dimension_semantics=sem if sem else None, vmem_limit_bytes=VMEM_LIMIT)


def _half_span():
    hs = {w // (2 * d) for w, d in B_GROUPS}
    assert len(hs) == 1
    return hs.pop()


def _rms_fwd(h, gain, after=()):
    S, D = h.shape
    ts = _pick(S, (512, 256, 128, 64, 32, 16))

    def body(h_ref, g_ref, *rest):
        o_ref = rest[-1]
        x = h_ref[...]
        r = lax.rsqrt(jnp.mean(x * x, axis=-1, keepdims=True) + EPS)
        o_ref[...] = (x * r * g_ref[...]).astype(o_ref.dtype)

    return pl.pallas_call(
        body, name="rms_fwd", grid=(S // ts,),
        in_specs=[pl.BlockSpec((ts, D), lambda i: (i, 0)), pl.BlockSpec((1, D), lambda i: (0, 0))]
        + [pl.BlockSpec(memory_space=pl.ANY)] * len(after),
        out_specs=pl.BlockSpec((ts, D), lambda i: (i, 0)),
        out_shape=jax.ShapeDtypeStruct((S, D), BF16), compiler_params=_cp("arbitrary"))(h, gain, *after)


def _rms_bwd(dy, h, gain, dres):
    S, D = h.shape
    ts = _pick(S, (256, 128, 64, 32, 16))

    def body(dy_ref, h_ref, g_ref, dres_ref, dh_ref, dhb_ref, dg_ref):
        @pl.when(pl.program_id(0) == 0)
        def _():
            dg_ref[...] = jnp.zeros_like(dg_ref)
        x = h_ref[...]
        dy = dy_ref[...]
        r = lax.rsqrt(jnp.mean(x * x, axis=-1, keepdims=True) + EPS)
        xn = x * r
        dg_ref[0:1, :] += jnp.sum(dy * xn, axis=0, keepdims=True)
        dxn = dy * g_ref[...]
        dx = r * (dxn - xn * jnp.mean(dxn * xn, axis=-1, keepdims=True))
        dh = dres_ref[...] + dx
        dh_ref[...] = dh
        dhb_ref[...] = dh.astype(BF16)

    row = pl.BlockSpec((ts, D), lambda i: (i, 0))
    return pl.pallas_call(
        body, name="rms_bwd", grid=(S // ts,),
        in_specs=[row, row, pl.BlockSpec((1, D), lambda i: (0, 0)), row],
        out_specs=[row, row, pl.BlockSpec((SUBLANES, D), lambda i: (0, 0))],
        out_shape=[jax.ShapeDtypeStruct((S, D), F32), jax.ShapeDtypeStruct((S, D), BF16),
                   jax.ShapeDtypeStruct((SUBLANES, D), F32)],
        compiler_params=_cp("arbitrary"))(dy, h, gain, dres)


def _final_loss(h, gain, target):
    S, D = h.shape
    ts = _pick(S, (256, 128, 64, 32, 16))

    def body(h_ref, g_ref, t_ref, loss_ref, dh_ref, dhb_ref, dg_ref):
        @pl.when(pl.program_id(0) == 0)
        def _():
            dg_ref[...] = jnp.zeros_like(dg_ref)
            loss_ref[...] = jnp.zeros_like(loss_ref)
        x = h_ref[...]
        g = g_ref[...]
        r = lax.rsqrt(jnp.mean(x * x, axis=-1, keepdims=True) + EPS)
        xn = x * r
        err = xn * g - t_ref[...]
        part = 0.5 * jnp.sum(jnp.mean(err * err, axis=-1, keepdims=True), axis=0, keepdims=True)
        loss_ref[0:1, 0:1] += part
        dy = err * (1.0 / D)
        dg_ref[0:1, :] += jnp.sum(dy * xn, axis=0, keepdims=True)
        dxn = dy * g
        dh = r * (dxn - xn * jnp.mean(dxn * xn, axis=-1, keepdims=True))
        dh_ref[...] = dh
        dhb_ref[...] = dh.astype(BF16)

    row = pl.BlockSpec((ts, D), lambda i: (i, 0))
    return pl.pallas_call(
        body, name="final_loss", grid=(S // ts,),
        in_specs=[row, pl.BlockSpec((1, D), lambda i: (0, 0)), row],
        out_specs=[pl.BlockSpec((SUBLANES, LANES), lambda i: (0, 0)), row, row,
                   pl.BlockSpec((SUBLANES, D), lambda i: (0, 0))],
        out_shape=[jax.ShapeDtypeStruct((SUBLANES, LANES), F32), jax.ShapeDtypeStruct((S, D), F32),
                   jax.ShapeDtypeStruct((S, D), BF16), jax.ShapeDtypeStruct((SUBLANES, D), F32)],
        compiler_params=_cp("arbitrary"))(h, gain, target)


_NN = (((1,), (0,)), ((), ()))
_NT = (((1,), (1,)), ((), ()))
_TN = (((0,), (0,)), ((), ()))


def _mm_nn(a, w, layer, *, blocked, out_dtype=F32, res=None, name):
    M, K = a.shape
    if blocked:
        nq = w.shape[3]
        N = N_CHIPS * nq
        tn = _pick(nq, (256, 128))
        nps = nq // tn
        w_spec = pl.BlockSpec((None, None, K, tn), lambda i, j: (layer, j // nps, 0, j % nps))
    else:
        N = w.shape[2]
        tn = _pick(N, (256, 128))
        w_spec = pl.BlockSpec((None, K, tn), lambda i, j: (layer, 0, j))
    tm = _pick(M, (1024, 512, 256, 128, 64, 32, 16)) if K <= 3072 else _pick(M, (512, 256, 128, 64, 32, 16))

    def body(*refs):
        if res is None:
            a_ref, w_ref, o_ref = refs
            acc = lax.dot_general(a_ref[...], w_ref[...], _NN, preferred_element_type=F32)
        else:
            a_ref, w_ref, r_ref, o_ref = refs
            acc = r_ref[...] + lax.dot_general(a_ref[...], w_ref[...], _NN, preferred_element_type=F32)
        o_ref[...] = acc.astype(o_ref.dtype)

    in_specs = [pl.BlockSpec((tm, K), lambda i, j: (i, 0)), w_spec]
    args = [a, w]
    if res is not None:
        in_specs.append(pl.BlockSpec((tm, tn), lambda i, j: (i, j)))
        args.append(res)
    return pl.pallas_call(
        body, name=name, grid=(M // tm, N // tn), in_specs=in_specs,
        out_specs=pl.BlockSpec((tm, tn), lambda i, j: (i, j)),
        out_shape=jax.ShapeDtypeStruct((M, N), out_dtype),
        compiler_params=_cp("arbitrary", "arbitrary"))(*args)


def _mm_nt(a, w, layer, *, blocked, name, after=()):
    pair = isinstance(a, tuple)
    M = a[0].shape[0] if pair else a.shape[0]
    tm = _pick(M, (1024, 512, 256, 128, 64, 32, 16))
    if blocked:
        K, nq = w.shape[2], w.shape[3]
        tk = _pick(K, (1024, 512, 256, 128))
        half = N_CHIPS // 2

        def body(*refs):
            a_refs, (w_ref, o_ref, acc_ref) = refs[:-3], refs[-3:]
            p = pl.program_id(2)

            @pl.when(p == 0)
            def _():
                acc_ref[...] = jnp.zeros_like(acc_ref)
            if pair:
                @pl.when(p < half)
                def _():
                    acc_ref[...] += lax.dot_general(a_refs[0][...], w_ref[...], _NT, preferred_element_type=F32)

                @pl.when(p >= half)
                def _():
                    acc_ref[...] += lax.dot_general(a_refs[1][...], w_ref[...], _NT, preferred_element_type=F32)
            else:
                acc_ref[...] += lax.dot_general(a_refs[0][...], w_ref[...], _NT, preferred_element_type=F32)

            @pl.when(p == N_CHIPS - 1)
            def _():
                o_ref[...] = acc_ref[...]

        if pair:
            a_specs = [pl.BlockSpec((tm, nq), lambda i, j, p: (i, jnp.minimum(p, half - 1))),
                       pl.BlockSpec((tm, nq), lambda i, j, p: (i, jnp.maximum(p - half, 0)))]
            a_args = list(a)
        else:
            a_specs = [pl.BlockSpec((tm, nq), lambda i, j, p: (i, p))]
            a_args = [a]
        return pl.pallas_call(
            body, name=name, grid=(M // tm, K // tk, N_CHIPS),
            in_specs=a_specs + [pl.BlockSpec((None, None, tk, nq), lambda i, j, p: (layer, p, j, 0))],
            out_specs=pl.BlockSpec((tm, tk), lambda i, j, p: (i, j)),
            out_shape=jax.ShapeDtypeStruct((M, K), F32),
            scratch_shapes=[pltpu.VMEM((tm, tk), F32)],
            compiler_params=_cp("arbitrary", "arbitrary", "arbitrary"))(*a_args, w)
    K, N = w.shape[1], w.shape[2]
    tk = _pick(K, (512, 256, 128))

    def body(a_ref, w_ref, *rest):
        rest[-1][...] = lax.dot_general(a_ref[...], w_ref[...], _NT, preferred_element_type=F32)

    return pl.pallas_call(
        body, name=name, grid=(M // tm, K // tk),
        in_specs=[pl.BlockSpec((tm, N), lambda i, j: (i, 0)),
                  pl.BlockSpec((None, tk, N), lambda i, j: (layer, j, 0))]
        + [pl.BlockSpec(memory_space=pl.ANY)] * len(after),
        out_specs=pl.BlockSpec((tm, tk), lambda i, j: (i, j)),
        out_shape=jax.ShapeDtypeStruct((M, K), F32),
        compiler_params=_cp("arbitrary", "arbitrary"))(a, w, *after)


def _mm_tn(x, dy, *, blocked, name):
    pair = isinstance(dy, tuple)
    S, K = x.shape
    N = 2 * dy[0].shape[1] if pair else dy.shape[1]
    tk = _pick(K, (512, 256, 128))
    if blocked:
        nq = N // N_CHIPS
        tn = _pick(nq, (256, 128))
        nps = nq // tn
        out_spec = pl.BlockSpec((None, tk, tn), lambda i, j: (j // nps, i, j % nps))
        out_shape = jax.ShapeDtypeStruct((N_CHIPS, K, nq), BF16)
    else:
        tn = _pick(N, (512, 256, 128))
        out_spec = pl.BlockSpec((tk, tn), lambda i, j: (i, j))
        out_shape = jax.ShapeDtypeStruct((K, N), BF16)
    nj = N // tn
    njh = nj // 2

    def body(x_ref, *refs):
        o_ref = refs[-1]
        if pair:
            j = pl.program_id(1)

            @pl.when(j < njh)
            def _():
                o_ref[...] = lax.dot_general(x_ref[...], refs[0][...], _TN, preferred_element_type=F32).astype(o_ref.dtype)

            @pl.when(j >= njh)
            def _():
                o_ref[...] = lax.dot_general(x_ref[...], refs[1][...], _TN, preferred_element_type=F32).astype(o_ref.dtype)
        else:
            o_ref[...] = lax.dot_general(x_ref[...], refs[0][...], _TN, preferred_element_type=F32).astype(o_ref.dtype)

    if pair:
        assert nj % 2 == 0
        dy_specs = [pl.BlockSpec((S, tn), lambda i, j: (0, jnp.minimum(j, njh - 1))),
                    pl.BlockSpec((S, tn), lambda i, j: (0, jnp.maximum(j - njh, 0)))]
        dy_args = list(dy)
    else:
        dy_specs = [pl.BlockSpec((S, tn), lambda i, j: (0, j))]
        dy_args = [dy]
    return pl.pallas_call(
        body, name=name, grid=(K // tk, nj),
        in_specs=[pl.BlockSpec((S, tk), lambda i, j: (0, i))] + dy_specs,
        out_specs=out_spec, out_shape=out_shape,
        compiler_params=_cp("arbitrary", "arbitrary"))(x, *dy_args)


def _rope_tables(S):
    rows = S // GRID_W
    row_ids = jnp.repeat(jnp.arange(rows, dtype=F32), GRID_W)
    col_ids = jnp.tile(jnp.arange(GRID_W, dtype=F32), rows)
    quarter = HEAD_DIM // 4
    inv_freq = ROPE_THETA ** (-jnp.arange(quarter, dtype=F32) / quarter)
    ang_r = row_ids[:, None] * inv_freq[None, :]
    ang_c = col_ids[:, None] * inv_freq[None, :]
    cos = jnp.concatenate([jnp.cos(ang_r)] * 2 + [jnp.cos(ang_c)] * 2, axis=-1)
    sin = jnp.concatenate([-jnp.sin(ang_r), jnp.sin(ang_r), -jnp.sin(ang_c), jnp.sin(ang_c)], axis=-1)
    return cos, sin


def _swap_quarters(x):
    lane = lax.broadcasted_iota(jnp.int32, x.shape, 1)
    first = (lane % (HEAD_DIM // 2)) < (HEAD_DIM // 4)
    return jnp.where(first, pltpu.roll(x, HEAD_DIM - HEAD_DIM // 4, 1), pltpu.roll(x, HEAD_DIM // 4, 1))


def _prep_a_fwd(qkv, cos, sin, gq, gk):
    S, W = qkv.shape
    nrm = A_HEADS + A_KV_HEADS
    ts = _pick(S, (256, 128, 64, 32, 16))

    def body(qkv_ref, cos_ref, sin_ref, gq_ref, gk_ref, o_ref):
        cos_t = cos_ref[...]
        sin_t = sin_ref[...]
        for j in range(nrm):
            sl = slice(j * HEAD_DIM, (j + 1) * HEAD_DIM)
            x = qkv_ref[:, sl]
            g = gq_ref[...] if j < A_HEADS else gk_ref[...]
            r = lax.rsqrt(jnp.mean(x * x, axis=-1, keepdims=True) + EPS)
            n = x * r * g
            o_ref[:, sl] = (n * cos_t + _swap_quarters(n) * sin_t).astype(BF16)
        o_ref[:, nrm * HEAD_DIM:] = qkv_ref[:, nrm * HEAD_DIM:].astype(BF16)

    row = lambda w: pl.BlockSpec((ts, w), lambda i: (i, 0))
    one = pl.BlockSpec((1, HEAD_DIM), lambda i: (0, 0))
    return pl.pallas_call(
        body, name="prep_a_fwd", grid=(S // ts,),
        in_specs=[row(W), row(HEAD_DIM), row(HEAD_DIM), one, one], out_specs=row(W),
        out_shape=jax.ShapeDtypeStruct((S, W), BF16), compiler_params=_cp("arbitrary"))(qkv, cos, sin, gq, gk)


def _prep_a_bwd(dq, dk, dv, qkv, cos, sin, gq, gk):
    S, W = qkv.shape
    nrm = A_HEADS + A_KV_HEADS
    nq, nk = A_HEADS * HEAD_DIM, A_KV_HEADS * HEAD_DIM
    ts = _pick(S, (256, 128, 64, 32, 16))

    def body(dq_ref, dk_ref, dv_ref, qkv_ref, cos_ref, sin_ref, gq_ref, gk_ref, o_ref, dg_ref):
        @pl.when(pl.program_id(0) == 0)
        def _():
            dg_ref[...] = jnp.zeros_like(dg_ref)
        cos_t = cos_ref[...]
        sin_t = sin_ref[...]
        for j in range(nrm):
            sl = slice(j * HEAD_DIM, (j + 1) * HEAD_DIM)
            x = qkv_ref[:, sl]
            if j < A_HEADS:
                dy, g, grow = dq_ref[:, sl], gq_ref[...], 0
            else:
                jj = j - A_HEADS
                dy, g, grow = dk_ref[:, jj * HEAD_DIM:(jj + 1) * HEAD_DIM], gk_ref[...], 1
            r = lax.rsqrt(jnp.mean(x * x, axis=-1, keepdims=True) + EPS)
            xn = x * r
            dn = dy * cos_t + _swap_quarters(dy * sin_t)
            dg_ref[grow:grow + 1, :] += jnp.sum(dn * xn, axis=0, keepdims=True)
            dxn = dn * g
            o_ref[:, sl] = (r * (dxn - xn * jnp.mean(dxn * xn, axis=-1, keepdims=True))).astype(BF16)
        o_ref[:, nrm * HEAD_DIM:] = dv_ref[...].astype(BF16)

    row = lambda w: pl.BlockSpec((ts, w), lambda i: (i, 0))
    one = pl.BlockSpec((1, HEAD_DIM), lambda i: (0, 0))
    return pl.pallas_call(
        body, name="prep_a_bwd", grid=(S // ts,),
        in_specs=[row(nq), row(nk), row(nk), row(W), row(HEAD_DIM), row(HEAD_DIM), one, one],
        out_specs=[row(W), pl.BlockSpec((SUBLANES, HEAD_DIM), lambda i: (0, 0))],
        out_shape=[jax.ShapeDtypeStruct((S, W), BF16), jax.ShapeDtypeStruct((SUBLANES, HEAD_DIM), F32)],
        compiler_params=_cp("arbitrary"))(dq, dk, dv, qkv, cos, sin, gq, gk)


def _flash_a_fwd(qkvh):
    S = qkvh.shape[0]
    grp = A_HEADS // A_KV_HEADS
    tq = _pick(S, (256, 128, 64, 32, 16))
    scale = HEAD_DIM ** -0.5

    def body(q_ref, k_ref, v_ref, o_ref, lse_ref):
        s = lax.dot_general(q_ref[...], k_ref[...], _NT, preferred_element_type=F32) * scale
        m = jnp.max(s, axis=-1, keepdims=True)
        p = jnp.exp(s - m)
        l = jnp.sum(p, axis=-1, keepdims=True)
        pn = (p * (1.0 / l)).astype(BF16)
        o_ref[...] = lax.dot_general(pn, v_ref[...], _NN, preferred_element_type=F32).astype(BF16)
        lse_ref[...] = jnp.broadcast_to(m + jnp.log(l), lse_ref.shape)

    qs = pl.BlockSpec((tq, HEAD_DIM), lambda h, i: (i, h))
    return pl.pallas_call(
        body, name="flash_a_fwd", grid=(A_HEADS, S // tq),
        in_specs=[qs,
                  pl.BlockSpec((S, HEAD_DIM), lambda h, i: (0, A_HEADS + h // grp)),
                  pl.BlockSpec((S, HEAD_DIM), lambda h, i: (0, A_HEADS + A_KV_HEADS + h // grp))],
        out_specs=[qs, qs],
        out_shape=[jax.ShapeDtypeStruct((S, A_HEADS * HEAD_DIM), BF16),
                   jax.ShapeDtypeStruct((S, A_HEADS * HEAD_DIM), F32)],
        compiler_params=_cp("arbitrary", "arbitrary"))(qkvh, qkvh, qkvh)


def _flash_a_bwd(qkvh, do, o, lse):
    S = qkvh.shape[0]
    grp = A_HEADS // A_KV_HEADS
    tq = _pick(S, (256, 128, 64, 32, 16))
    scale = HEAD_DIM ** -0.5

    def body(q_ref, k_ref, v_ref, do_ref, o_ref, lse_ref, dq_ref, dk_ref, dv_ref):
        @pl.when((pl.program_id(1) == 0) & (pl.program_id(2) == 0))
        def _():
            dk_ref[...] = jnp.zeros_like(dk_ref)
            dv_ref[...] = jnp.zeros_like(dv_ref)
        q = q_ref[...]
        k = k_ref[...]
        do_f = do_ref[...]
        do_b = do_f.astype(BF16)
        s = lax.dot_general(q, k, _NT, preferred_element_type=F32) * scale
        p = jnp.exp(s - lse_ref[:, 0:1])
        dp = lax.dot_general(do_b, v_ref[...], _NT, preferred_element_type=F32)
        delta = jnp.sum(do_f * o_ref[...].astype(F32), axis=-1, keepdims=True)
        ds_b = (p * (dp - delta) * scale).astype(BF16)
        dq_ref[...] = lax.dot_general(ds_b, k, _NN, preferred_element_type=F32)
        dk_ref[...] += lax.dot_general(ds_b, q, _TN, preferred_element_type=F32)
        dv_ref[...] += lax.dot_general(p.astype(BF16), do_b, _TN, preferred_element_type=F32)

    qs = pl.BlockSpec((tq, HEAD_DIM), lambda kv, g, i: (i, kv * grp + g))
    kvs = lambda off: pl.BlockSpec((S, HEAD_DIM), lambda kv, g, i: (0, off + kv))
    return pl.pallas_call(
        body, name="flash_a_bwd", grid=(A_KV_HEADS, grp, S // tq),
        in_specs=[qs, kvs(A_HEADS), kvs(A_HEADS + A_KV_HEADS), qs, qs, qs],
        out_specs=[qs, kvs(0), kvs(0)],
        out_shape=[jax.ShapeDtypeStruct((S, A_HEADS * HEAD_DIM), F32),
                   jax.ShapeDtypeStruct((S, A_KV_HEADS * HEAD_DIM), F32),
                   jax.ShapeDtypeStruct((S, A_KV_HEADS * HEAD_DIM), F32)],
        compiler_params=_cp("arbitrary", "arbitrary", "arbitrary"))(qkvh, qkvh, qkvh, do, o, lse)


def _bucket_tables(transposed):
    hs = _half_span()
    tq, kv = 2 * hs, 4 * hs
    nb = REL_BUCKETS // 2
    max_exact = nb // 2
    shape = (kv, tq) if transposed else (tq, kv)
    out = np.zeros((len(B_GROUPS), 3) + shape, np.int32)
    win = np.arange(kv) - hs
    blk = np.arange(tq)
    for g, (_, dil) in enumerate(B_GROUPS):
        for case in range(3):
            inside = ((win >= 0) | (case != 0)) & ((win < tq) | (case != 2))
            if transposed:
                rel = blk[None, :] - win[:, None]
                ok = inside[:, None]
            else:
                rel = win[None, :] - blk[:, None]
                ok = inside[None, :]
            r = rel * dil
            n = np.abs(r)
            nf = np.maximum(n, 1).astype(np.float32)
            large = max_exact + (np.log(nf / np.float32(max_exact)) / np.float32(math.log(REL_MAX_DISTANCE / max_exact))
                                 * np.float32(nb - max_exact)).astype(np.int32)
            large = np.minimum(large, nb - 1)
            bucket = np.where(r > 0, nb, 0) + np.where(n < max_exact, n, large)
            out[g, case] = np.where((np.abs(rel) <= hs) & ok, bucket, -1)
    return out


def _bias_build(rel_bias, buckets):
    G, _, tq, kv = buckets.shape
    hg = B_HEADS_PER_GROUP

    def body(rb_ref, bk_ref, o_ref):
        col = pl.program_id(0) * hg + pl.program_id(2)
        bk = bk_ref[...]
        acc = jnp.full((tq, kv), NEG_INF, F32)
        for b in range(REL_BUCKETS):
            acc = jnp.where(bk == b, rb_ref[b, col], acc)
        o_ref[...] = acc

    return pl.pallas_call(
        body, name="bias_build", grid=(G, 3, hg),
        in_specs=[pl.BlockSpec(memory_space=pltpu.SMEM),
                  pl.BlockSpec((None, None, tq, kv), lambda g, c, h: (g, c, 0, 0))],
        out_specs=pl.BlockSpec((None, None, None, tq, kv), lambda g, c, h: (g, c, h, 0, 0)),
        out_shape=jax.ShapeDtypeStruct((G, 3, hg, tq, kv), F32),
        compiler_params=_cp("arbitrary", "arbitrary", "arbitrary"))(rel_bias, buckets)


def _bias_reduce(dbias_list, buckets):
    G, _, tq, kv = buckets.shape
    hg = B_HEADS_PER_GROUP
    n = len(dbias_list)

    def body(*refs):
        bk_ref, o_ref = refs[n], refs[n + 1]
        first = (pl.program_id(0) == 0) & (pl.program_id(1) == 0) & (pl.program_id(2) == 0)

        @pl.when(first)
        def _():
            o_ref[...] = jnp.zeros_like(o_ref)
        col = pl.program_id(0) * hg + pl.program_id(2)
        db = refs[0][...]
        for r in refs[1:n]:
            db = db + r[...]
        bk = bk_ref[...]
        rows = lax.broadcasted_iota(jnp.int32, (REL_BUCKETS, LANES), 0)
        cols = lax.broadcasted_iota(jnp.int32, (REL_BUCKETS, LANES), 1)
        acc = jnp.zeros((REL_BUCKETS, LANES), F32)
        for b in range(REL_BUCKETS):
            val = jnp.sum(jnp.sum(jnp.where(bk == b, db, 0.0), axis=1, keepdims=True), axis=0, keepdims=True)
            acc = acc + jnp.where((rows == b) & (cols == col), val, 0.0)
        o_ref[...] += acc

    tile = pl.BlockSpec((None, None, None, tq, kv), lambda g, c, h: (g, c, h, 0, 0))
    return pl.pallas_call(
        body, name="bias_reduce", grid=(G, 3, hg),
        in_specs=[tile] * n + [pl.BlockSpec((None, None, tq, kv), lambda g, c, h: (g, c, 0, 0))],
        out_specs=pl.BlockSpec((REL_BUCKETS, LANES), lambda g, c, h: (0, 0)),
        out_shape=jax.ShapeDtypeStruct((REL_BUCKETS, LANES), F32),
        compiler_params=_cp("arbitrary", "arbitrary", "arbitrary"))(*dbias_list, buckets)


def _mm_nn_perm(a, w, g):
    S, K = a.shape
    nq = w.shape[3]
    dil = B_GROUPS[g][1]
    wg3 = 3 * B_HEADS_PER_GROUP * HEAD_DIM
    tn = _pick(nq, (256, 128))
    assert wg3 % tn == 0
    nps, ntile = nq // tn, wg3 // tn
    tm = _pick(S, (1024, 512, 256))
    rows = tm // dil

    def body(a_ref, w_ref, o_ref, acc_ref):
        acc = lax.dot_general(a_ref[...], w_ref[...], _NN, preferred_element_type=F32)
        if dil == 1:
            o_ref[0] = acc.astype(BF16)
        else:
            for k in range(tn // LANES):
                acc_ref[k] = acc[:, k * LANES:(k + 1) * LANES]
            for c in range(dil):
                for k in range(tn // LANES):
                    o_ref[c, :, k * LANES:(k + 1) * LANES] = acc_ref[k, pl.ds(c, rows, stride=dil), :].astype(BF16)

    def w_map(i, j):
        t = g * ntile + j
        return (0, t // nps, 0, t % nps)

    return pl.pallas_call(
        body, name="b_qkv_g%d" % g, grid=(S // tm, ntile),
        in_specs=[pl.BlockSpec((tm, K), lambda i, j: (i, 0)), pl.BlockSpec((None, None, K, tn), w_map)],
        out_specs=pl.BlockSpec((dil, rows, tn), lambda i, j: (0, i, j)),
        out_shape=jax.ShapeDtypeStruct((dil, S // dil, wg3), BF16),
        scratch_shapes=[pltpu.VMEM((tn // LANES, tm, LANES), F32)],
        compiler_params=_cp("arbitrary", "arbitrary"))(a, w)


def _window_specs(S, wg, col):
    hs = _half_span()
    tq = 2 * hs
    per = tq // hs
    return (pl.BlockSpec((tq, wg), lambda i: (i, col)),
            pl.BlockSpec((hs, wg), lambda i: (jnp.maximum(i * per - 1, 0), col)),
            pl.BlockSpec((hs, wg), lambda i: (jnp.minimum((i + 1) * per, S // hs - 1), col)))


def _window_case(i, L):
    per = L // (2 * _half_span())
    r = i % per
    return jnp.where(r == 0, 0, jnp.where(r == per - 1, 2, 1))


def _window(prev_ref, main_ref, next_ref, sl):
    return jnp.concatenate([prev_ref[:, sl], main_ref[:, sl], next_ref[:, sl]], axis=0)


def _battn_fwd(qkvp, bias, g):
    dil, L, wg3 = qkvp.shape
    S = dil * L
    hs = _half_span()
    tq, kvl = 2 * hs, 4 * hs
    hg = B_HEADS_PER_GROUP
    wg = hg * HEAD_DIM
    scale = HEAD_DIM ** -0.5
    flat = qkvp.reshape(S, wg3)

    def body(q_ref, km, kp, kn, vm, vp, vn, b_ref, o_ref, lz_ref):
        case = _window_case(pl.program_id(0), L)
        for h in range(hg):
            sl = slice(h * HEAD_DIM, (h + 1) * HEAD_DIM)
            s = lax.dot_general(q_ref[:, sl], _window(kp, km, kn, sl), _NT, preferred_element_type=F32) * scale
            s = s + b_ref[case, h]
            m = jnp.max(s, axis=-1, keepdims=True)
            p = jnp.exp(s - m)
            l = jnp.sum(p, axis=-1, keepdims=True)
            o_ref[:, sl] = lax.dot_general(p.astype(BF16), _window(vp, vm, vn, sl), _NN, preferred_element_type=F32) / l
            lz_ref[:, sl] = jnp.broadcast_to(m + jnp.log(l), (tq, HEAD_DIM))

    blk = pl.BlockSpec((tq, wg), lambda i: (i, 0))
    o, lz = pl.pallas_call(
        body, name="battn_fwd_g%d" % g, grid=(S // tq,),
        in_specs=[_window_specs(S, wg, 0)[0], *_window_specs(S, wg, 1), *_window_specs(S, wg, 2),
                  pl.BlockSpec((None, 3, hg, tq, kvl), lambda i: (g, 0, 0, 0, 0))],
        out_specs=[blk, blk], out_shape=[jax.ShapeDtypeStruct((S, wg), F32)] * 2,
        compiler_params=_cp("arbitrary"))(flat, flat, flat, flat, flat, flat, flat, bias)
    return o, lz


def _battn_bwd_dq(qkvp, bias, do, o, lz, dlz, g):
    dil, L, wg3 = qkvp.shape
    S = dil * L
    hs = _half_span()
    tq, kvl = 2 * hs, 4 * hs
    hg = B_HEADS_PER_GROUP
    wg = hg * HEAD_DIM
    scale = HEAD_DIM ** -0.5
    flat = qkvp.reshape(S, wg3)

    def body(q_ref, km, kp, kn, vm, vp, vn, b_ref, do_ref, o_ref, lz_ref, dlz_ref, dq_ref, rt_ref, db_ref):
        i = pl.program_id(0)

        @pl.when(i == 0)
        def _():
            db_ref[...] = jnp.zeros_like(db_ref)
        case = _window_case(i, L)
        for h in range(hg):
            sl = slice(h * HEAD_DIM, (h + 1) * HEAD_DIM)
            kw = _window(kp, km, kn, sl)
            do_f = do_ref[:, sl]
            s = lax.dot_general(q_ref[:, sl], kw, _NT, preferred_element_type=F32) * scale + b_ref[case, h]
            p = jnp.exp(s - lz_ref[:, sl][:, 0:1])
            dp = lax.dot_general(do_f.astype(BF16), _window(vp, vm, vn, sl), _NT, preferred_element_type=F32)
            rt = dlz_ref[:, sl][:, 0:1] - jnp.sum(do_f * o_ref[:, sl], axis=-1, keepdims=True)
            ds = p * (dp + rt)
            db_ref[case, h] += ds
            dq_ref[:, sl] = lax.dot_general((ds * scale).astype(BF16), kw, _NN, preferred_element_type=F32)
            rt_ref[:, sl] = jnp.broadcast_to(rt, (tq, HEAD_DIM))

    blk = pl.BlockSpec((tq, wg), lambda i: (i, 0))
    row = jax.ShapeDtypeStruct((S, wg), F32)
    return pl.pallas_call(
        body, name="battn_bwd_dq_g%d" % g, grid=(S // tq,),
        in_specs=[_window_specs(S, wg, 0)[0], *_window_specs(S, wg, 1), *_window_specs(S, wg, 2),
                  pl.BlockSpec((None, 3, hg, tq, kvl), lambda i: (g, 0, 0, 0, 0)), blk, blk, blk, blk],
        out_specs=[blk, blk, pl.BlockSpec((3, hg, tq, kvl), lambda i: (0, 0, 0, 0))],
        out_shape=[row, row, jax.ShapeDtypeStruct((3, hg, tq, kvl), F32)],
        compiler_params=_cp("arbitrary"))(flat, flat, flat, flat, flat, flat, flat, bias, do, o, lz, dlz)


def _battn_bwd_dkv(qkvp, bias_t, do, lz, rt, g):
    dil, L, wg3 = qkvp.shape
    S = dil * L
    hs = _half_span()
    tq, kvl = 2 * hs, 4 * hs
    hg = B_HEADS_PER_GROUP
    wg = hg * HEAD_DIM
    scale = HEAD_DIM ** -0.5
    flat = qkvp.reshape(S, wg3)

    def body(k_ref, v_ref, qm, qp, qn, dom, dop, don, lzm, lzp, lzn, rtm, rtp, rtn, b_ref, dk_ref, dv_ref):
        case = _window_case(pl.program_id(0), L)
        for h in range(hg):
            sl = slice(h * HEAD_DIM, (h + 1) * HEAD_DIM)
            qw = _window(qp, qm, qn, sl)
            dow = _window(dop, dom, don, sl).astype(BF16)
            s = lax.dot_general(qw, k_ref[:, sl], _NT, preferred_element_type=F32) * scale + b_ref[case, h]
            p = jnp.exp(s - _window(lzp, lzm, lzn, sl)[:, 0:1])
            dp = lax.dot_general(dow, v_ref[:, sl], _NT, preferred_element_type=F32)
            ds_b = (p * (dp + _window(rtp, rtm, rtn, sl)[:, 0:1]) * scale).astype(BF16)
            dk_ref[:, sl] = lax.dot_general(ds_b, qw, _TN, preferred_element_type=F32)
            dv_ref[:, sl] = lax.dot_general(p.astype(BF16), dow, _TN, preferred_element_type=F32)

    blk = pl.BlockSpec((tq, wg), lambda i: (i, 0))
    row = jax.ShapeDtypeStruct((S, wg), F32)
    return pl.pallas_call(
        body, name="battn_bwd_dkv_g%d" % g, grid=(S // tq,),
        in_specs=[_window_specs(S, wg, 1)[0], _window_specs(S, wg, 2)[0], *_window_specs(S, wg, 0),
                  *_window_specs(S, wg, 0), *_window_specs(S, wg, 0), *_window_specs(S, wg, 0),
                  pl.BlockSpec((None, 3, hg, kvl, tq), lambda i: (g, 0, 0, 0, 0))],
        out_specs=[blk, blk], out_shape=[row, row],
        compiler_params=_cp("arbitrary"))(flat, flat, flat, flat, flat, do, do, do, lz, lz, lz, rt, rt, rt, bias_t)


def _group_weights(lz_refs, h):
    z = [r[h] for r in lz_refs]
    mx = functools.reduce(jnp.maximum, z)
    e = [jnp.exp(v - mx) for v in z]
    inv = 1.0 / functools.reduce(lambda a, b: a + b, e)
    return [v * inv for v in e]


def _to_token_order(src_ref, dst_ref, dil):
    rows = src_ref.shape[1]
    for k in range(dst_ref.shape[0]):
        sl = slice(k * LANES, (k + 1) * LANES)
        if dil == 1:
            dst_ref[k] = src_ref[0, :, sl]
        else:
            for c in range(dil):
                dst_ref[k, pl.ds(c, rows, stride=dil), :] = src_ref[c, :, sl]


def _to_subsequence_order(src_ref, dst_ref, dil):
    rows = dst_ref.shape[1]
    for k in range(src_ref.shape[0]):
        sl = slice(k * LANES, (k + 1) * LANES)
        if dil == 1:
            dst_ref[0, :, sl] = src_ref[k]
        else:
            for c in range(dil):
                dst_ref[c, :, sl] = src_ref[k, pl.ds(c, rows, stride=dil), :]


def _sub_view(a, dil):
    S, w = a.shape
    return a.reshape(dil, S // dil, w)


def _sub_spec(dil, ts, w):
    return pl.BlockSpec((dil, ts // dil, w), lambda i: (0, i, 0))


def _combine_fwd(os_, lzs):
    G = len(os_)
    S, Wg = os_[0].shape
    hg = B_HEADS_PER_GROUP
    dils = [d for _, d in B_GROUPS]
    ts = _pick(S, (256, 128))

    def body(*refs):
        o_in, lz_in, y_ref = refs[:G], refs[G:2 * G], refs[2 * G]
        o_nat, lz_nat = refs[2 * G + 1:3 * G + 1], refs[3 * G + 1:4 * G + 1]
        for g in range(G):
            _to_token_order(o_in[g], o_nat[g], dils[g])
            _to_token_order(lz_in[g], lz_nat[g], dils[g])
        for h in range(hg):
            w = _group_weights(lz_nat, h)
            for g in range(G):
                y_ref[:, (g * hg + h) * HEAD_DIM:(g * hg + h + 1) * HEAD_DIM] = (w[g] * o_nat[g][h]).astype(BF16)

    specs = [_sub_spec(d, ts, Wg) for d in dils]
    return pl.pallas_call(
        body, name="combine_fwd", grid=(S // ts,), in_specs=specs + specs,
        out_specs=pl.BlockSpec((ts, G * Wg), lambda i: (i, 0)),
        out_shape=jax.ShapeDtypeStruct((S, G * Wg), BF16),
        scratch_shapes=[pltpu.VMEM((hg, ts, HEAD_DIM), F32)] * (2 * G),
        compiler_params=_cp("arbitrary"))(*[_sub_view(a, d) for a, d in zip(os_, dils)],
                                          *[_sub_view(a, d) for a, d in zip(lzs, dils)])


def _combine_bwd(dy, os_, lzs):
    G = len(os_)
    S, Wg = os_[0].shape
    hg = B_HEADS_PER_GROUP
    dils = [d for _, d in B_GROUPS]
    ts = _pick(S, (128,))

    def body(*refs):
        dy_ref, o_in, lz_in = refs[0], refs[1:1 + G], refs[1 + G:1 + 2 * G]
        do_out, dlz_out = refs[1 + 2 * G:1 + 3 * G], refs[1 + 3 * G:1 + 4 * G]
        scr = refs[1 + 4 * G:]
        o_nat, lz_nat, do_nat, dlz_nat = scr[:G], scr[G:2 * G], scr[2 * G:3 * G], scr[3 * G:4 * G]
        for g in range(G):
            _to_token_order(o_in[g], o_nat[g], dils[g])
            _to_token_order(lz_in[g], lz_nat[g], dils[g])
        for h in range(hg):
            w = _group_weights(lz_nat, h)
            dw = []
            for g in range(G):
                dyg = dy_ref[:, (g * hg + h) * HEAD_DIM:(g * hg + h + 1) * HEAD_DIM]
                dw.append(jnp.sum(dyg * o_nat[g][h], axis=-1, keepdims=True))
                do_nat[g][h] = w[g] * dyg
            tot = functools.reduce(lambda a, b: a + b, [w[g] * dw[g] for g in range(G)])
            for g in range(G):
                dlz_nat[g][h] = w[g] * (dw[g] - tot)
        for g in range(G):
            _to_subsequence_order(do_nat[g], do_out[g], dils[g])
            _to_subsequence_order(dlz_nat[g], dlz_out[g], dils[g])

    specs = [_sub_spec(d, ts, Wg) for d in dils]
    outs = pl.pallas_call(
        body, name="combine_bwd", grid=(S // ts,),
        in_specs=[pl.BlockSpec((ts, G * Wg), lambda i: (i, 0))] + specs + specs,
        out_specs=specs + specs,
        out_shape=[jax.ShapeDtypeStruct((d, S // d, Wg), F32) for d in dils] * 2,
        scratch_shapes=[pltpu.VMEM((hg, ts, HEAD_DIM), F32)] * (4 * G),
        compiler_params=_cp("arbitrary"))(dy, *[_sub_view(a, d) for a, d in zip(os_, dils)],
                                          *[_sub_view(a, d) for a, d in zip(lzs, dils)])
    flat = [a.reshape(S, Wg) for a in outs]
    return flat[:G], flat[G:]


def _concat_cast(parts, dils):
    S = parts[0].shape[0]
    widths = [p.shape[1] for p in parts]
    n = len(parts)
    ts = _pick(S, (256, 128))

    def body(*refs):
        o_ref, nat = refs[n], refs[n + 1]
        off = 0
        for r, w, d in zip(refs, widths, dils):
            _to_token_order(r, nat, d)
            for k in range(w // LANES):
                o_ref[:, off + k * LANES:off + (k + 1) * LANES] = nat[k].astype(BF16)
            off += w

    assert len(set(widths)) == 1
    return pl.pallas_call(
        body, name="concat_cast", grid=(S // ts,),
        in_specs=[_sub_spec(d, ts, w) for w, d in zip(widths, dils)],
        out_specs=pl.BlockSpec((ts, sum(widths)), lambda i: (i, 0)),
        out_shape=jax.ShapeDtypeStruct((S, sum(widths)), BF16),
        scratch_shapes=[pltpu.VMEM((widths[0] // LANES, ts, LANES), F32)],
        compiler_params=_cp("arbitrary"))(*[_sub_view(p, d) for p, d in zip(parts, dils)])


def _ffn_specs(S, dff, cq, ts, tc, layer, order):
    nfc = dff // tc
    nps = cq // tc
    hb = ts // SUBLANES
    nrow8 = S // SUBLANES

    def u_main(half):
        return pl.BlockSpec((ts, tc), lambda *g: (order(*g)[0], order(*g)[1] % nfc + half * nfc))

    def u_prev(half):
        return pl.BlockSpec((SUBLANES, tc), lambda *g: (jnp.maximum(order(*g)[0] * hb - 1, 0),
                                                         order(*g)[1] % nfc + half * nfc))

    def u_next(half):
        return pl.BlockSpec((SUBLANES, tc), lambda *g: (jnp.minimum((order(*g)[0] + 1) * hb, nrow8 - 1),
                                                         order(*g)[1] % nfc + half * nfc))

    def cw(half):
        def im(*g):
            jj = order(*g)[1] % nfc + half * nfc
            return (layer, jj // nps, 0, jj % nps)
        return pl.BlockSpec((None, None, 3, tc), im)

    def cb(half):
        return pl.BlockSpec((None, 1, tc), lambda *g: (layer, 0, order(*g)[1] % nfc + half * nfc))

    return nfc, u_main, u_prev, u_next, cw, cb


def _ffn_act_fwd(u, cw_full, cb3, layer):
    S, two_dff = u.shape
    dff = two_dff // 2
    cq = cw_full.shape[3]
    ts = _pick(S, (512, 256, 128, 64, 32, 16))
    tc = _pick(cq, (256, 128))
    order = lambda j, i: (i, j)
    nfc, u_main, u_prev, u_next, cw, cb = _ffn_specs(S, dff, cq, ts, tc, layer, order)
    nrow = S // ts

    def body(ug, ugp, ugn, uv, uvp, uvn, wg, wv, bg, bv, a_ref):
        i = pl.program_id(1)
        row = lax.broadcasted_iota(jnp.int32, (ts, tc), 0)

        def conv(x_ref, p_ref, n_ref, w_ref, b_ref):
            x = x_ref[...]
            prev = jnp.where(i > 0, p_ref[SUBLANES - 1:SUBLANES, :], 0.0)
            nxt = jnp.where(i < nrow - 1, n_ref[0:1, :], 0.0)
            xm = jnp.where(row == 0, prev, pltpu.roll(x, 1, 0))
            xp = jnp.where(row == ts - 1, nxt, pltpu.roll(x, ts - 1, 0))
            return w_ref[0:1, :] * xm + w_ref[1:2, :] * x + w_ref[2:3, :] * xp + b_ref[...]

        gc = conv(ug, ugp, ugn, wg, bg)
        vc = conv(uv, uvp, uvn, wv, bv)
        a_ref[...] = (gc * (1.0 / (1.0 + jnp.exp(-gc))) * vc).astype(BF16)

    return pl.pallas_call(
        body, name="ffn_act_fwd", grid=(nfc, nrow),
        in_specs=[u_main(0), u_prev(0), u_next(0), u_main(1), u_prev(1), u_next(1), cw(0), cw(1), cb(0), cb(1)],
        out_specs=pl.BlockSpec((ts, tc), lambda j, i: (i, j)),
        out_shape=jax.ShapeDtypeStruct((S, dff), BF16),
        compiler_params=_cp("arbitrary", "arbitrary"))(u, u, u, u, u, u, cw_full, cw_full, cb3, cb3)


def _ffn_act_bwd(u, da, cw_full, cb3, layer):
    S, two_dff = u.shape
    dff = two_dff // 2
    cq = cw_full.shape[3]
    ts = _pick(S, (512, 256, 128, 64, 32, 16))
    tc = _pick(cq, (256, 128))
    order = lambda j, i: (i, j)
    nfc, u_main, u_prev, u_next, cw, cb = _ffn_specs(S, dff, cq, ts, tc, layer, order)
    nrow = S // ts
    hb = ts // SUBLANES
    te = ts + 2 * SUBLANES
    da_main = pl.BlockSpec((ts, tc), lambda j, i: (i, j))
    da_prev = pl.BlockSpec((SUBLANES, tc), lambda j, i: (jnp.maximum(i * hb - 1, 0), j))
    da_next = pl.BlockSpec((SUBLANES, tc), lambda j, i: (jnp.minimum((i + 1) * hb, S // SUBLANES - 1), j))
    main = slice(SUBLANES, SUBLANES + ts)

    def body(ug, ugp, ugn, uv, uvp, uvn, dam, dap, dan, wg, wv, bg, bv, dug_ref, duv_ref, accg_ref, accv_ref):
        i = pl.program_id(1)

        @pl.when(i == 0)
        def _():
            accg_ref[...] = jnp.zeros_like(accg_ref)
            accv_ref[...] = jnp.zeros_like(accv_ref)

        def ext(m, p, n):
            return jnp.concatenate([jnp.where(i > 0, p[...], 0.0), m[...], jnp.where(i < nrow - 1, n[...], 0.0)], axis=0)

        def shift(x):
            return pltpu.roll(x, 1, 0), pltpu.roll(x, te - 1, 0)

        xg, xv, dae = ext(ug, ugp, ugn), ext(uv, uvp, uvn), ext(dam, dap, dan)
        xgm, xgp = shift(xg)
        xvm, xvp = shift(xv)
        gc = wg[0:1, :] * xgm + wg[1:2, :] * xg + wg[2:3, :] * xgp + bg[...]
        vc = wv[0:1, :] * xvm + wv[1:2, :] * xv + wv[2:3, :] * xvp + bv[...]
        sig = 1.0 / (1.0 + jnp.exp(-gc))
        silu = gc * sig
        dcg = dae * vc * (sig * (1.0 + gc * (1.0 - sig)))
        dcv = dae * silu

        def finish(dc, x, xm, xp, w_ref, du_ref, acc_ref):
            dm, dp = shift(dc)
            du = w_ref[0:1, :] * dp + w_ref[1:2, :] * dc + w_ref[2:3, :] * dm
            du_ref[...] = du[main, :].astype(BF16)
            dcm = dc[main, :]
            acc_ref[0:1, :] += jnp.sum(dcm * xm[main, :], axis=0, keepdims=True)
            acc_ref[1:2, :] += jnp.sum(dcm * x[main, :], axis=0, keepdims=True)
            acc_ref[2:3, :] += jnp.sum(dcm * xp[main, :], axis=0, keepdims=True)
            acc_ref[3:4, :] += jnp.sum(dcm, axis=0, keepdims=True)

        finish(dcg, xg, xgm, xgp, wg, dug_ref, accg_ref)
        finish(dcv, xv, xvm, xvp, wv, duv_ref, accv_ref)

    blk = pl.BlockSpec((ts, tc), lambda j, i: (i, j))
    acc = pl.BlockSpec((SUBLANES, tc), lambda j, i: (0, j))
    dug, duv, accg, accv = pl.pallas_call(
        body, name="ffn_act_bwd", grid=(nfc, nrow),
        in_specs=[u_main(0), u_prev(0), u_next(0), u_main(1), u_prev(1), u_next(1), da_main, da_prev, da_next,
                  cw(0), cw(1), cb(0), cb(1)],
        out_specs=[blk, blk, acc, acc],
        out_shape=[jax.ShapeDtypeStruct((S, dff), BF16)] * 2 + [jax.ShapeDtypeStruct((SUBLANES, dff), F32)] * 2,
        compiler_params=_cp("arbitrary", "arbitrary"))(u, u, u, u, u, u, da, da, da, cw_full, cw_full, cb3, cb3)
    return (dug, duv), jnp.concatenate([accg, accv], axis=1)


def _my_chip():
    return 2 * lax.axis_index("x") + lax.axis_index("y")


def _into_full(w, layer, dtype):
    L, a, b = w.shape
    tr = _pick(a, (512, 256, 128, 64, 32, 16, 8))

    def body(w_ref, o_ref):
        o_ref[...] = w_ref[...].astype(dtype)

    return pl.pallas_call(
        body, name="into_full", grid=(a // tr,),
        in_specs=[pl.BlockSpec((None, tr, b), lambda i: (layer, i, 0))],
        out_specs=pl.BlockSpec((None, None, tr, b), lambda i: (0, _my_chip(), i, 0)),
        out_shape=jax.ShapeDtypeStruct((1, N_CHIPS, a, b), dtype),
        compiler_params=_cp("arbitrary"))(w)


def _adam_math(w, g, m, v):
    m = ADAM_B1 * m + (1.0 - ADAM_B1) * g
    v = ADAM_B2 * v + (1.0 - ADAM_B2) * (g * g)
    m_hat = m / (1.0 - ADAM_B1 ** ADAM_STEP)
    v_hat = v / (1.0 - ADAM_B2 ** ADAM_STEP)
    delta = -ADAM_LR * (m_hat / (jnp.sqrt(v_hat) + ADAM_EPS) + ADAM_WD * w)
    return delta, m, v


def _adamw(w, g, m, v):
    R, C = w.shape
    tr = _pick(R, (128, 64, 32, 16, 8)) if R % SUBLANES == 0 and C % LANES == 0 else R

    def body(w_ref, g_ref, m_ref, v_ref, d_ref, nm_ref, nv_ref):
        d, nm, nv = _adam_math(w_ref[...], g_ref[...], m_ref[...], v_ref[...])
        d_ref[...] = d
        nm_ref[...] = nm
        nv_ref[...] = nv

    spec = pl.BlockSpec((tr, C), lambda i: (i, 0))
    return pl.pallas_call(
        body, name="adamw", grid=(R // tr,), in_specs=[spec] * 4, out_specs=[spec] * 3,
        out_shape=[jax.ShapeDtypeStruct((R, C), F32)] * 3, compiler_params=_cp("arbitrary"))(w, g, m, v)


ANY = pl.BlockSpec(memory_space=pl.ANY)


def _position():
    x, y, c = lax.axis_index("x"), lax.axis_index("y"), lax.axis_index("c")
    chips = [(1 - x, y), (x, 1 - y), (1 - x, 1 - y)]
    return x, y, c, chips


HBM = pl.BlockSpec(memory_space=pltpu.HBM)
SEM = pl.BlockSpec(memory_space=pltpu.SEMAPHORE)
EFFECT = pltpu.SideEffectType.DATAFLOW_SIDE_EFFECTING


def _in_hbm(a):
    return pltpu.with_memory_space_constraint(a, pltpu.HBM)


def _shard_half(buf, shape, chip, half):
    _, _, a, b = shape
    p = 2 * chip[0] + chip[1]
    if a % (4 * SUBLANES) == 0:
        return buf.at[0, p, pl.ds(half * (a // 2), a // 2)]
    return buf.at[0, p, :, pl.ds(half * (b // 2), b // 2)]


def _gather_copy(buf, shape, chip, half, to, send, recv, k):
    part = _shard_half(buf, shape, chip, half)
    return pltpu.make_async_remote_copy(src_ref=part, dst_ref=part, send_sem=send.at[k], recv_sem=recv.at[k],
                                        device_id=to, device_id_type=MESH)


def _gather_start(fulls, name):
    n = len(fulls)

    def body(*refs):
        send, recv, buf, token = refs[n], refs[n + 1], refs[n + 2:2 * n + 2], refs[2 * n + 2]
        x, y, c, chips = _position()
        for t in range(n):
            for j in range(3):
                _gather_copy(buf[t], fulls[t].shape, (x, y), c, (*chips[j], c), send, recv, 3 * t + j).start()
        token[...] = jnp.zeros_like(token)

    outs = pl.pallas_call(
        body, name=name, in_specs=[HBM] * n,
        out_specs=[SEM, SEM] + [HBM] * n + [pl.BlockSpec(memory_space=pltpu.VMEM)],
        out_shape=[pltpu.SemaphoreType.DMA((3 * n,)), pltpu.SemaphoreType.DMA((3 * n,))]
        + [pltpu.HBM(f.shape, f.dtype) for f in fulls] + [jax.ShapeDtypeStruct((SUBLANES, LANES), F32)],
        input_output_aliases={t: 2 + t for t in range(n)},
        compiler_params=pltpu.CompilerParams(has_side_effects=EFFECT))(*[_in_hbm(f) for f in fulls])
    return outs[0], outs[1], list(outs[2:2 + n]), outs[2 + n]


def _gather_wait(send, recv, fulls, after, name):
    n = len(fulls)

    def body(*refs):
        buf, send_ref, recv_ref = refs[:n], refs[n], refs[n + 1]
        x, y, c, chips = _position()
        for t in range(n):
            for j in range(3):
                _gather_copy(buf[t], fulls[t].shape, (x, y), c, (*chips[j], c), send_ref, recv_ref, 3 * t + j).wait_send()
                _gather_copy(buf[t], fulls[t].shape, chips[j], c, (*chips[j], c), send_ref, recv_ref, 3 * t + j).wait_recv()

    outs = pl.pallas_call(
        body, name=name, in_specs=[HBM] * n + [SEM, SEM, ANY], out_specs=[HBM] * n,
        out_shape=[pltpu.HBM(f.shape, f.dtype) for f in fulls],
        input_output_aliases={t: t for t in range(n)},
        compiler_params=pltpu.CompilerParams(has_side_effects=EFFECT))(*fulls, send, recv, after)
    return list(outs)


def _gather_forward(fulls):
    n = len(fulls)

    def body(*refs):
        buf, send, recv = refs[n:2 * n], refs[2 * n], refs[2 * n + 1]
        x, y, c, chips = _position()
        sib = (x, y, 1 - c)
        cps = [_gather_copy(buf[t], fulls[t].shape, chips[j], c, sib, send, recv, 3 * t + j)
               for t in range(n) for j in range(3)]
        for cp in cps:
            cp.start()
        for t in range(n):
            for j in range(3):
                _gather_copy(buf[t], fulls[t].shape, chips[j], 1 - c, sib, send, recv, 3 * t + j).wait_recv()
        for cp in cps:
            cp.wait_send()

    return pl.pallas_call(
        body, name="gather_forward", in_specs=[ANY] * n, out_specs=[ANY] * n,
        out_shape=[jax.ShapeDtypeStruct(f.shape, f.dtype) for f in fulls],
        input_output_aliases={t: t for t in range(n)},
        scratch_shapes=[pltpu.SemaphoreType.DMA((3 * n,)), pltpu.SemaphoreType.DMA((3 * n,))])(*fulls)


def _allreduce_small(part):
    M, C = part.shape
    n_dev = 2 * N_CHIPS

    def body(x_ref, sum_ref, all_ref, send, recv, local):
        x, y, c, chips = _position()
        me, sib = (x, y, c), (x, y, 1 - c)

        def rows(px, py, pc):
            return all_ref.at[pl.ds((4 * px + 2 * py + pc) * M, M), :]

        def copy(k, block, to, src=None):
            return pltpu.make_async_remote_copy(
                src_ref=rows(*block) if src is None else src, dst_ref=rows(*block),
                send_sem=send.at[k], recv_sem=recv.at[k], device_id=to, device_id_type=MESH)

        mine = pltpu.make_async_copy(x_ref, rows(*me), local)
        mine.start()
        first = [copy(0, me, sib, src=x_ref)] + [copy(1 + j, me, (*chip, c), src=x_ref) for j, chip in enumerate(chips)]
        for cp in first:
            cp.start()
        passed = [copy(4 + j, (*chip, c), sib) for j, chip in enumerate(chips)]
        for j, chip in enumerate(chips):
            copy(1 + j, (*chip, c), me).wait_recv()
            passed[j].start()
        copy(0, sib, me).wait_recv()
        for j, chip in enumerate(chips):
            copy(4 + j, (*chip, 1 - c), me).wait_recv()
        for cp in first + passed:
            cp.wait_send()
        mine.wait()
        acc = all_ref[0:M, :]
        for d in range(1, n_dev):
            acc = acc + all_ref[d * M:(d + 1) * M, :]
        sum_ref[...] = acc

    vm = pl.BlockSpec(memory_space=pltpu.VMEM)
    return pl.pallas_call(
        body, name="allreduce_small", in_specs=[vm], out_specs=[vm],
        out_shape=[jax.ShapeDtypeStruct((M, C), F32)],
        scratch_shapes=[pltpu.VMEM((n_dev * M, C), F32), pltpu.SemaphoreType.DMA((7,)),
                        pltpu.SemaphoreType.DMA((7,)), pltpu.SemaphoreType.DMA],
        compiler_params=pltpu.CompilerParams(vmem_limit_bytes=VMEM_LIMIT))(part)[0]


def _rs_sibling(grads):
    n = len(grads)

    def body(*refs):
        src, dst, send, recv = refs[:n], refs[n:2 * n], refs[2 * n], refs[2 * n + 1]
        x, y, c, _ = _position()
        cps = [pltpu.make_async_remote_copy(
            src_ref=src[t].at[:, 1 - c], dst_ref=dst[t], send_sem=send.at[t], recv_sem=recv.at[t],
            device_id=(x, y, 1 - c), device_id_type=MESH) for t in range(n)]
        for cp in cps:
            cp.start()
        for cp in cps:
            cp.wait()

    return pl.pallas_call(
        body, name="rs_sibling", in_specs=[ANY] * n, out_specs=[ANY] * n,
        out_shape=[jax.ShapeDtypeStruct((g.shape[0],) + g.shape[2:], BF16) for g in grads],
        scratch_shapes=[pltpu.SemaphoreType.DMA((n,)), pltpu.SemaphoreType.DMA((n,))])(*grads)


def _add_sibling(grad, got):
    _, _, R, C = grad.shape
    tr = _pick(R, (512, 256, 128, 64, 32, 16))

    def body(g_ref, r_ref, o_ref):
        o_ref[...] = (g_ref[...].astype(F32) + r_ref[...].astype(F32)).astype(BF16)

    return pl.pallas_call(
        body, name="add_sibling", grid=(N_CHIPS, R // tr),
        in_specs=[pl.BlockSpec((None, None, tr, C), lambda p, i: (p, lax.axis_index("c"), i, 0)),
                  pl.BlockSpec((None, tr, C), lambda p, i: (p, i, 0))],
        out_specs=pl.BlockSpec((None, tr, C), lambda p, i: (p, i, 0)),
        out_shape=jax.ShapeDtypeStruct((N_CHIPS, R, C), BF16),
        compiler_params=_cp("arbitrary", "arbitrary"))(grad, got)


def _rs_chips_copy(src, dst, chips, c, send, recv, t, j):
    return pltpu.make_async_remote_copy(
        src_ref=src.at[2 * chips[j][0] + chips[j][1]], dst_ref=dst.at[j],
        send_sem=send.at[3 * t + j], recv_sem=recv.at[3 * t + j], device_id=(*chips[j], c), device_id_type=MESH)


def _rs_chips_start(sums, name):
    n = len(sums)
    lands = [lax.empty((3,) + s.shape[1:], BF16) for s in sums]

    def body(*refs):
        send, recv = refs[2 * n], refs[2 * n + 1]
        src, dst, token = refs[2 * n + 2:3 * n + 2], refs[3 * n + 2:4 * n + 2], refs[4 * n + 2]
        x, y, c, chips = _position()
        for t in range(n):
            for j in range(3):
                _rs_chips_copy(src[t], dst[t], chips, c, send, recv, t, j).start()
        token[...] = jnp.zeros_like(token)

    outs = pl.pallas_call(
        body, name=name, in_specs=[HBM] * (2 * n),
        out_specs=[SEM, SEM] + [HBM] * (2 * n) + [pl.BlockSpec(memory_space=pltpu.VMEM)],
        out_shape=[pltpu.SemaphoreType.DMA((3 * n,)), pltpu.SemaphoreType.DMA((3 * n,))]
        + [pltpu.HBM(a.shape, a.dtype) for a in sums + lands] + [jax.ShapeDtypeStruct((SUBLANES, LANES), F32)],
        input_output_aliases={t: 2 + t for t in range(2 * n)},
        compiler_params=pltpu.CompilerParams(has_side_effects=EFFECT))(*[_in_hbm(a) for a in sums + lands])
    return outs[0], outs[1], list(outs[2:2 + n]), list(outs[2 + n:2 + 2 * n]), outs[2 + 2 * n]


def _rs_chips_wait(send, recv, sums, lands, after, name):
    n = len(sums)

    def body(*refs):
        src, dst, send_ref, recv_ref = refs[:n], refs[n:2 * n], refs[2 * n], refs[2 * n + 1]
        x, y, c, chips = _position()
        for t in range(n):
            for j in range(3):
                cp = _rs_chips_copy(src[t], dst[t], chips, c, send_ref, recv_ref, t, j)
                cp.wait_send()
                cp.wait_recv()

    outs = pl.pallas_call(
        body, name=name, in_specs=[HBM] * (2 * n) + [SEM, SEM, ANY], out_specs=[HBM] * (2 * n),
        out_shape=[pltpu.HBM(a.shape, a.dtype) for a in sums + lands],
        input_output_aliases={t: t for t in range(2 * n)},
        compiler_params=pltpu.CompilerParams(has_side_effects=EFFECT))(*sums, *lands, send, recv, after)
    return list(outs[n:])


def _add_chips(grad, got_sib, got_chips, stack, layer):
    _, _, R, C = grad.shape
    tr = _pick(R, (256, 128, 64, 32, 16))

    def body(g_ref, s_ref, r_ref, stack_ref, o_ref):
        acc = g_ref[...].astype(F32) + s_ref[...].astype(F32)
        for j in range(3):
            acc = acc + r_ref[j].astype(F32)
        o_ref[...] = acc

    return pl.pallas_call(
        body, name="add_chips", grid=(R // tr,),
        in_specs=[pl.BlockSpec((None, None, tr, C), lambda i: (_my_chip(), lax.axis_index("c"), i, 0)),
                  pl.BlockSpec((None, tr, C), lambda i: (_my_chip(), i, 0)),
                  pl.BlockSpec((3, tr, C), lambda i: (0, i, 0)),
                  ANY],
        out_specs=pl.BlockSpec((None, None, tr, C), lambda i: (layer, lax.axis_index("c"), i, 0)),
        out_shape=jax.ShapeDtypeStruct(stack.shape, F32), input_output_aliases={3: 0},
        compiler_params=_cp("arbitrary"))(grad, got_sib, got_chips, stack)


def _ag_sibling(stacks):
    n = len(stacks)
    offs = np.cumsum([0] + [s.shape[0] for s in stacks])

    def body(*refs):
        buf, send, recv = refs[n:2 * n], refs[2 * n], refs[2 * n + 1]
        x, y, c, _ = _position()

        def copy(t, l, half):
            part = buf[t].at[l, half]
            return pltpu.make_async_remote_copy(
                src_ref=part, dst_ref=part, send_sem=send.at[int(offs[t]) + l], recv_sem=recv.at[int(offs[t]) + l],
                device_id=(x, y, 1 - c), device_id_type=MESH)

        cps = [copy(t, l, c) for t in range(n) for l in range(stacks[t].shape[0])]
        for cp in cps:
            cp.start()
        for t in range(n):
            for l in range(stacks[t].shape[0]):
                copy(t, l, 1 - c).wait_recv()
        for cp in cps:
            cp.wait_send()

    return pl.pallas_call(
        body, name="ag_sibling", in_specs=[ANY] * n, out_specs=[ANY] * n,
        out_shape=[jax.ShapeDtypeStruct(s.shape, F32) for s in stacks],
        input_output_aliases={t: t for t in range(n)},
        scratch_shapes=[pltpu.SemaphoreType.DMA((int(offs[-1]),)), pltpu.SemaphoreType.DMA((int(offs[-1]),))])(*stacks)


def _split8(dw, blocked):
    if blocked:
        p, k, nq = dw.shape
        return dw.reshape(p, 2, k // 2, nq)
    k, n = dw.shape
    return dw.reshape(N_CHIPS, 2, k // (2 * N_CHIPS), n)


def kernel(x, a_w_qkv, a_w_o, a_q_gain, a_k_gain, b_w_qkv, b_w_o, rel_bias, mix_norm, ffn_norm, w_up, conv_w, conv_b, w_down, final_norm, loss_target, m_a_w_qkv, m_a_w_o, m_a_q_gain, m_a_k_gain, m_b_w_qkv, m_b_w_o, m_rel_bias, m_mix_norm, m_ffn_norm, m_w_up, m_conv_w, m_conv_b, m_w_down, m_final_norm, v_a_w_qkv, v_a_w_o, v_a_q_gain, v_a_k_gain, v_b_w_qkv, v_b_w_o, v_rel_bias, v_mix_norm, v_ffn_norm, v_w_up, v_conv_w, v_conv_b, v_w_down, v_final_norm):
    S, D = x.shape[1], x.shape[2]
    h = x.reshape(S, D)
    target = loss_target.reshape(S, D)
    hg = B_HEADS_PER_GROUP
    G = len(B_GROUPS)
    n_a, n_b = a_w_qkv.shape[0], b_w_qkv.shape[0]
    depth = w_up.shape[0]
    cx, cy = lax.axis_index("x"), lax.axis_index("y")

    big = dict(a_w_qkv=a_w_qkv, a_w_o=a_w_o, b_w_qkv=b_w_qkv, b_w_o=b_w_o, w_up=w_up, w_down=w_down)
    blocked = dict(a_w_qkv=True, a_w_o=False, b_w_qkv=True, b_w_o=False, w_up=True, w_down=False)
    names = list(big)
    srcs = dict(big, conv_w=conv_w)
    started = []
    for i in range(depth):
        mix = [("a_w_qkv", i // 2), ("a_w_o", i // 2)] if i % 2 == 0 else [("b_w_qkv", i // 2), ("b_w_o", i // 2)]
        keys_i = mix + [("w_up", i), ("conv_w", i), ("w_down", i)]
        bufs = [_into_full(srcs[k], l, F32 if k == "conv_w" else BF16) for k, l in keys_i]
        started.append((keys_i,) + _gather_start(bufs, "gather_start_%d" % i))
    cb3 = conv_b.reshape(depth, 1, conv_b.shape[1])

    cos, sin = _rope_tables(S)
    buckets = jnp.asarray(_bucket_tables(False))
    bias = _bias_build(rel_bias, buckets)
    bias_t = _bias_build(rel_bias, jnp.asarray(_bucket_tables(True)))

    saved = []
    for i in range(depth):
        j = i // 2
        keys_i, send, recv, bufs, _ = started[i]
        bufs = _gather_forward(_gather_wait(send, recv, bufs, h, "gather_wait_%d" % i))
        wl = {}
        for (k, _), buf in zip(keys_i, bufs):
            _, _, a, b = buf.shape
            wl[k] = buf if k == "conv_w" or blocked[k] else buf.reshape(1, N_CHIPS * a, b)
        sv = dict(h0=h, w=wl)
        hn = _rms_fwd(h, mix_norm[i:i + 1], after=[s[4] for s in started] if i == 0 else ())
        sv["hn"] = hn
        if i % 2 == 0:
            qkv = _mm_nn(hn, wl["a_w_qkv"], 0, blocked=True, name="a_qkv")
            qkvh = _prep_a_fwd(qkv, cos, sin, a_q_gain[j:j + 1], a_k_gain[j:j + 1])
            o, lse = _flash_a_fwd(qkvh)
            sv.update(qkv=qkv, qkvh=qkvh, o=o, lse=lse)
            h = _mm_nn(o, wl["a_w_o"], 0, blocked=False, res=h, name="a_out")
        else:
            qkvp = [_mm_nn_perm(hn, wl["b_w_qkv"], g) for g in range(G)]
            os_, lzs = [], []
            for g in range(G):
                o_g, lz_g = _battn_fwd(qkvp[g], bias, g)
                os_.append(o_g)
                lzs.append(lz_g)
            y = _combine_fwd(os_, lzs)
            sv.update(qkvp=qkvp, os=os_, lzs=lzs, y=y)
            h = _mm_nn(y, wl["b_w_o"], 0, blocked=False, res=h, name="b_out")
        sv["h1"] = h
        hf = _rms_fwd(h, ffn_norm[i:i + 1])
        u = _mm_nn(hf, wl["w_up"], 0, blocked=True, name="ffn_up")
        act = _ffn_act_fwd(u, wl["conv_w"], cb3[i:i + 1], 0)
        sv.update(hf=hf, u=u, act=act)
        h = _mm_nn(act, wl["w_down"], 0, blocked=False, res=h, name="ffn_down")
        saved.append(sv)

    loss_blk, dh, dh_b, dg_final = _final_loss(h, final_norm.reshape(1, D), target)

    dws = {k: [None] * big[k].shape[0] for k in names}
    d_mix, d_ffn, d_convw, d_convb = [None] * depth, [None] * depth, [None] * depth, [None] * depth
    d_gq, d_gk = [None] * n_a, [None] * n_a
    dbias_list = []
    pending = []

    def start_reduce(keys, tag):
        pieces = [_split8(dws[k][l], blocked[k]) for k, l in keys]
        got_sib = _rs_sibling(pieces)
        sums = [_add_sibling(p, r) for p, r in zip(pieces, got_sib)]
        send, recv, sums, lands, token = _rs_chips_start(sums, "rs_chips_start_" + tag)
        pending.append((keys, pieces, got_sib, send, recv, sums, lands, tag))
        return (token,)

    tok = ()
    for i in reversed(range(depth)):
        j = i // 2
        sv = saved[i]
        wl = sv["w"]
        da = _mm_nt(dh_b, wl["w_down"], 0, blocked=False, name="ffn_down_dx", after=tok)
        dws["w_down"][i] = _mm_tn(sv["act"], dh_b, blocked=False, name="ffn_down_dw")
        du, dconv = _ffn_act_bwd(sv["u"], da, wl["conv_w"], cb3[i:i + 1], 0)
        d_convw[i], d_convb[i] = dconv[0:3], dconv[3]
        dhf = _mm_nt(du, wl["w_up"], 0, blocked=True, name="ffn_up_dx")
        dws["w_up"][i] = _mm_tn(sv["hf"], du, blocked=True, name="ffn_up_dw")
        dh, dh_b, dg = _rms_bwd(dhf, sv["h1"], ffn_norm[i:i + 1], dh)
        d_ffn[i] = dg[0]
        tok = start_reduce([("w_down", i), ("w_up", i)], "ffn%d" % i)
        if i % 2 == 0:
            do = _mm_nt(dh_b, wl["a_w_o"], 0, blocked=False, name="a_out_dx", after=tok)
            dws["a_w_o"][j] = _mm_tn(sv["o"], dh_b, blocked=False, name="a_out_dw")
            dq, dk, dv = _flash_a_bwd(sv["qkvh"], do, sv["o"], sv["lse"])
            dqkv, dgain = _prep_a_bwd(dq, dk, dv, sv["qkv"], cos, sin, a_q_gain[j:j + 1], a_k_gain[j:j + 1])
            d_gq[j], d_gk[j] = dgain[0], dgain[1]
            dhn = _mm_nt(dqkv, wl["a_w_qkv"], 0, blocked=True, name="a_qkv_dx")
            dws["a_w_qkv"][j] = _mm_tn(sv["hn"], dqkv, blocked=True, name="a_qkv_dw")
            mix_keys = [("a_w_o", j), ("a_w_qkv", j)]
        else:
            dy = _mm_nt(dh_b, wl["b_w_o"], 0, blocked=False, name="b_out_dx", after=tok)
            dws["b_w_o"][j] = _mm_tn(sv["y"], dh_b, blocked=False, name="b_out_dw")
            dos, dlzs = _combine_bwd(dy, sv["os"], sv["lzs"])
            parts = []
            for g in range(G):
                dq, rt, db = _battn_bwd_dq(sv["qkvp"][g], bias, dos[g], sv["os"][g], sv["lzs"][g], dlzs[g], g)
                dk, dv = _battn_bwd_dkv(sv["qkvp"][g], bias_t, dos[g], sv["lzs"][g], rt, g)
                parts += [dq, dk, dv]
                dbias_list.append((g, db))
            dqkv = _concat_cast(parts, [d for _, d in B_GROUPS for _ in range(3)])
            dhn = _mm_nt(dqkv, wl["b_w_qkv"], 0, blocked=True, name="b_qkv_dx")
            dws["b_w_qkv"][j] = _mm_tn(sv["hn"], dqkv, blocked=True, name="b_qkv_dw")
            mix_keys = [("b_w_o", j), ("b_w_qkv", j)]
        dh, dh_b, dg = _rms_bwd(dhn, sv["h0"], mix_norm[i:i + 1], dh)
        d_mix[i] = dg[0]
        tok = start_reduce(mix_keys, "mix%d" % i)
    grad_x = dh.reshape(x.shape)

    dbias_layers = [jnp.stack([db for g2, db in dbias_list[l * G:(l + 1) * G]]) for l in range(n_b)]
    d_rel = _bias_reduce(dbias_layers, buckets)[:, :G * hg]

    small = [jnp.stack(d_gq), jnp.stack(d_gk), d_rel, jnp.stack(d_mix), jnp.stack(d_ffn), jnp.stack(d_convw),
             jnp.stack(d_convb), dg_final[0]]
    sizes = [int(np.prod(s.shape)) for s in small]
    flat = jnp.concatenate([s.reshape(-1) for s in small])
    rows = -(-flat.shape[0] // (LANES * SUBLANES)) * SUBLANES
    flat = jnp.pad(flat, (0, rows * LANES - flat.shape[0])).reshape(rows, LANES)
    tot = _allreduce_small(flat).reshape(-1)
    offs = np.cumsum([0] + sizes)
    g_gq, g_gk, g_rel, g_mix, g_ffn, g_convw_full, g_convb, g_final = [
        tot[offs[k]:offs[k + 1]].reshape(small[k].shape) for k in range(len(small))]
    cq = conv_w.shape[2]
    g_convw = lax.dynamic_slice_in_dim(g_convw_full, (2 * cx + cy) * cq, cq, axis=2)

    stacks = {}
    for keys, pieces, got_sib, send, recv, sums, lands, tag in pending:
        got_chips = _rs_chips_wait(send, recv, sums, lands, dh, "rs_chips_wait_" + tag)
        for (k, l), p, r, rc in zip(keys, pieces, got_sib, got_chips):
            if k not in stacks:
                stacks[k] = lax.empty((big[k].shape[0], 2) + p.shape[2:], F32)
            stacks[k] = _add_chips(p, r, rc, stacks[k], l)
    shard_grads = {k: gs.reshape(big[k].shape) for k, gs in zip(names, _ag_sibling([stacks[k] for k in names]))}

    grads = dict(shard_grads, a_q_gain=g_gq, a_k_gain=g_gk, rel_bias=g_rel, mix_norm=g_mix, ffn_norm=g_ffn,
                 conv_w=g_convw, conv_b=g_convb, final_norm=g_final)
    weights = dict(a_w_qkv=a_w_qkv, a_w_o=a_w_o, a_q_gain=a_q_gain, a_k_gain=a_k_gain, b_w_qkv=b_w_qkv, b_w_o=b_w_o,
                   rel_bias=rel_bias, mix_norm=mix_norm, ffn_norm=ffn_norm, w_up=w_up, conv_w=conv_w, conv_b=conv_b,
                   w_down=w_down, final_norm=final_norm)
    ms = dict(a_w_qkv=m_a_w_qkv, a_w_o=m_a_w_o, a_q_gain=m_a_q_gain, a_k_gain=m_a_k_gain, b_w_qkv=m_b_w_qkv,
              b_w_o=m_b_w_o, rel_bias=m_rel_bias, mix_norm=m_mix_norm, ffn_norm=m_ffn_norm, w_up=m_w_up,
              conv_w=m_conv_w, conv_b=m_conv_b, w_down=m_w_down, final_norm=m_final_norm)
    vs = dict(a_w_qkv=v_a_w_qkv, a_w_o=v_a_w_o, a_q_gain=v_a_q_gain, a_k_gain=v_a_k_gain, b_w_qkv=v_b_w_qkv,
              b_w_o=v_b_w_o, rel_bias=v_rel_bias, mix_norm=v_mix_norm, ffn_norm=v_ffn_norm, w_up=v_w_up,
              conv_w=v_conv_w, conv_b=v_conv_b, w_down=v_w_down, final_norm=v_final_norm)
    deltas, new_m, new_v = {}, {}, {}
    for k, w in weights.items():
        two_d = (-1, w.shape[-1])
        d, nm, nv = _adamw(w.reshape(two_d), grads[k].reshape(two_d), ms[k].reshape(two_d), vs[k].reshape(two_d))
        deltas[k], new_m[k], new_v[k] = d.reshape(w.shape), nm.reshape(w.shape), nv.reshape(w.shape)

    loss = lax.psum(loss_blk[0, 0], ("x", "y", "c"))
    keys = list(weights)
    return (loss, grad_x, *[grads[k].reshape(weights[k].shape) for k in keys], *[deltas[k] for k in keys],
            *[new_m[k] for k in keys], *[new_v[k] for k in keys])
```

```python
import functools
import math

import numpy as np
import jax
import jax.numpy as jnp
from jax import lax
from jax.experimental import pallas as pl
from jax.experimental.pallas import tpu as pltpu

F32 = jnp.float32
BF16 = jnp.bfloat16

HEAD_DIM = 128
A_HEADS = 16
A_KV_HEADS = 4
GRID_W = 64
ROPE_THETA = 10000.0
B_GROUPS = ((128, 1), (512, 4), (2048, 16))
B_HEADS_PER_GROUP = 8
REL_BUCKETS = 32
REL_MAX_DISTANCE = 1024
EPS = 1e-6
NEG_INF = -1e30
DEPTH = 4
ADAM_LR = 0.001
ADAM_B1 = 0.9
ADAM_B2 = 0.999
ADAM_EPS = 1e-08
ADAM_WD = 0.01
ADAM_STEP = 10

N_CHIPS = 4
LANES = 128
SUBLANES = 8
VMEM_LIMIT = 52 * 1024 * 1024
MESH = pl.DeviceIdType.MESH


def _pick(n, cands):
    for c in cands:
        if c <= n and n % c == 0:
            return c
    return n


def _lane_tile(n, cap):
    best = None
    for t in range(LANES, min(n, cap) + 1, LANES):
        if n % t == 0:
            best = t
    return best or n


def _cp(*sem):
    return pltpu.CompilerParams(dimension_semantics=sem if sem else None, vmem_limit_bytes=VMEM_LIMIT)


def _half_span():
    hs = {w // (2 * d) for w, d in B_GROUPS}
    assert len(hs) == 1
    return hs.pop()


def _rms_fwd(h, gain, after=()):
    S, D = h.shape
    ts = _pick(S, (512, 256, 128, 64, 32, 16))

    def body(h_ref, g_ref, *rest):
        o_ref = rest[-1]
        x = h_ref[...]
        r = lax.rsqrt(jnp.mean(x * x, axis=-1, keepdims=True) + EPS)
        o_ref[...] = (x * r * g_ref[...]).astype(o_ref.dtype)

    return pl.pallas_call(
        body, name="rms_fwd", grid=(S // ts,),
        in_specs=[pl.BlockSpec((ts, D), lambda i: (i, 0)), pl.BlockSpec((1, D), lambda i: (0, 0))]
        + [pl.BlockSpec(memory_space=pl.ANY)] * len(after),
        out_specs=pl.BlockSpec((ts, D), lambda i: (i, 0)),
        out_shape=jax.ShapeDtypeStruct((S, D), BF16), compiler_params=_cp("arbitrary"))(h, gain, *after)


def _rms_bwd(dy, h, gain, dres):
    S, D = h.shape
    ts = _pick(S, (256, 128, 64, 32, 16))

    def body(dy_ref, h_ref, g_ref, dres_ref, dh_ref, dhb_ref, dg_ref):
        @pl.when(pl.program_id(0) == 0)
        def _():
            dg_ref[...] = jnp.zeros_like(dg_ref)
        x = h_ref[...]
        dy = dy_ref[...]
        r = lax.rsqrt(jnp.mean(x * x, axis=-1, keepdims=True) + EPS)
        xn = x * r
        dg_ref[0:1, :] += jnp.sum(dy * xn, axis=0, keepdims=True)
        dxn = dy * g_ref[...]
        dx = r * (dxn - xn * jnp.mean(dxn * xn, axis=-1, keepdims=True))
        dh = dres_ref[...] + dx
        dh_ref[...] = dh
        dhb_ref[...] = dh.astype(BF16)

    row = pl.BlockSpec((ts, D), lambda i: (i, 0))
    return pl.pallas_call(
        body, name="rms_bwd", grid=(S // ts,),
        in_specs=[row, row, pl.BlockSpec((1, D), lambda i: (0, 0)), row],
        out_specs=[row, row, pl.BlockSpec((SUBLANES, D), lambda i: (0, 0))],
        out_shape=[jax.ShapeDtypeStruct((S, D), F32), jax.ShapeDtypeStruct((S, D), BF16),
                   jax.ShapeDtypeStruct((SUBLANES, D), F32)],
        compiler_params=_cp("arbitrary"))(dy, h, gain, dres)


def _final_loss(h, gain, target):
    S, D = h.shape
    ts = _pick(S, (256, 128, 64, 32, 16))

    def body(h_ref, g_ref, t_ref, loss_ref, dh_ref, dhb_ref, dg_ref):
        @pl.when(pl.program_id(0) == 0)
        def _():
            dg_ref[...] = jnp.zeros_like(dg_ref)
            loss_ref[...] = jnp.zeros_like(loss_ref)
        x = h_ref[...]
        g = g_ref[...]
        r = lax.rsqrt(jnp.mean(x * x, axis=-1, keepdims=True) + EPS)
        xn = x * r
        err = xn * g - t_ref[...]
        part = 0.5 * jnp.sum(jnp.mean(err * err, axis=-1, keepdims=True), axis=0, keepdims=True)
        loss_ref[0:1, 0:1] += part
        dy = err * (1.0 / D)
        dg_ref[0:1, :] += jnp.sum(dy * xn, axis=0, keepdims=True)
        dxn = dy * g
        dh = r * (dxn - xn * jnp.mean(dxn * xn, axis=-1, keepdims=True))
        dh_ref[...] = dh
        dhb_ref[...] = dh.astype(BF16)

    row = pl.BlockSpec((ts, D), lambda i: (i, 0))
    return pl.pallas_call(
        body, name="final_loss", grid=(S // ts,),
        in_specs=[row, pl.BlockSpec((1, D), lambda i: (0, 0)), row],
        out_specs=[pl.BlockSpec((SUBLANES, LANES), lambda i: (0, 0)), row, row,
                   pl.BlockSpec((SUBLANES, D), lambda i: (0, 0))],
        out_shape=[jax.ShapeDtypeStruct((SUBLANES, LANES), F32), jax.ShapeDtypeStruct((S, D), F32),
                   jax.ShapeDtypeStruct((S, D), BF16), jax.ShapeDtypeStruct((SUBLANES, D), F32)],
        compiler_params=_cp("arbitrary"))(h, gain, target)


_NN = (((1,), (0,)), ((), ()))
_NT = (((1,), (1,)), ((), ()))
_TN = (((0,), (0,)), ((), ()))


def _mm_nn(a, w, layer, *, blocked, out_dtype=F32, res=None, name):
    M, K = a.shape
    if blocked:
        nq = w.shape[3]
        N = N_CHIPS * nq
        tn = _lane_tile(nq, 1408)
        nps = nq // tn
        w_spec = pl.BlockSpec((None, None, K, tn), lambda i, j: (layer, j // nps, 0, j % nps))
    else:
        N = w.shape[2]
        tn = _lane_tile(N, 512)
        w_spec = pl.BlockSpec((None, K, tn), lambda i, j: (layer, 0, j))
    tm = _pick(M, (1024, 512, 256, 128, 64, 32, 16)) if K <= 3072 else _pick(M, (512, 256, 128, 64, 32, 16))

    def body(*refs):
        if res is None:
            a_ref, w_ref, o_ref = refs
            acc = lax.dot_general(a_ref[...], w_ref[...], _NN, preferred_element_type=F32)
        else:
            a_ref, w_ref, r_ref, o_ref = refs
            acc = r_ref[...] + lax.dot_general(a_ref[...], w_ref[...], _NN, preferred_element_type=F32)
        o_ref[...] = acc.astype(o_ref.dtype)

    in_specs = [pl.BlockSpec((tm, K), lambda i, j: (i, 0)), w_spec]
    args = [a, w]
    if res is not None:
        in_specs.append(pl.BlockSpec((tm, tn), lambda i, j: (i, j)))
        args.append(res)
    return pl.pallas_call(
        body, name=name, grid=(M // tm, N // tn), in_specs=in_specs,
        out_specs=pl.BlockSpec((tm, tn), lambda i, j: (i, j)),
        out_shape=jax.ShapeDtypeStruct((M, N), out_dtype),
        compiler_params=_cp("arbitrary", "arbitrary"))(*args)


def _mm_nt(a, w, layer, *, blocked, name, after=()):
    pair = isinstance(a, tuple)
    M = a[0].shape[0] if pair else a.shape[0]
    tm = _pick(M, (1024, 512, 256, 128, 64, 32, 16))
    if blocked:
        K, nq = w.shape[2], w.shape[3]
        tk = _pick(K, (1024, 512, 256, 128))
        half = N_CHIPS // 2

        def body(*refs):
            a_refs, (w_ref, o_ref, acc_ref) = refs[:-3], refs[-3:]
            p = pl.program_id(2)

            @pl.when(p == 0)
            def _():
                acc_ref[...] = jnp.zeros_like(acc_ref)
            if pair:
                @pl.when(p < half)
                def _():
                    acc_ref[...] += lax.dot_general(a_refs[0][...], w_ref[...], _NT, preferred_element_type=F32)

                @pl.when(p >= half)
                def _():
                    acc_ref[...] += lax.dot_general(a_refs[1][...], w_ref[...], _NT, preferred_element_type=F32)
            else:
                acc_ref[...] += lax.dot_general(a_refs[0][...], w_ref[...], _NT, preferred_element_type=F32)

            @pl.when(p == N_CHIPS - 1)
            def _():
                o_ref[...] = acc_ref[...]

        if pair:
            a_specs = [pl.BlockSpec((tm, nq), lambda i, j, p: (i, jnp.minimum(p, half - 1))),
                       pl.BlockSpec((tm, nq), lambda i, j, p: (i, jnp.maximum(p - half, 0)))]
            a_args = list(a)
        else:
            a_specs = [pl.BlockSpec((tm, nq), lambda i, j, p: (i, p))]
            a_args = [a]
        return pl.pallas_call(
            body, name=name, grid=(M // tm, K // tk, N_CHIPS),
            in_specs=a_specs + [pl.BlockSpec((None, None, tk, nq), lambda i, j, p: (layer, p, j, 0))],
            out_specs=pl.BlockSpec((tm, tk), lambda i, j, p: (i, j)),
            out_shape=jax.ShapeDtypeStruct((M, K), F32),
            scratch_shapes=[pltpu.VMEM((tm, tk), F32)],
            compiler_params=_cp("arbitrary", "arbitrary", "arbitrary"))(*a_args, w)
    K, N = w.shape[1], w.shape[2]
    tk = _pick(K, (1024, 512, 256, 128))

    def body(a_ref, w_ref, *rest):
        rest[-1][...] = lax.dot_general(a_ref[...], w_ref[...], _NT, preferred_element_type=F32)

    return pl.pallas_call(
        body, name=name, grid=(M // tm, K // tk),
        in_specs=[pl.BlockSpec((tm, N), lambda i, j: (i, 0)),
                  pl.BlockSpec((None, tk, N), lambda i, j: (layer, j, 0))]
        + [pl.BlockSpec(memory_space=pl.ANY)] * len(after),
        out_specs=pl.BlockSpec((tm, tk), lambda i, j: (i, j)),
        out_shape=jax.ShapeDtypeStruct((M, K), F32),
        compiler_params=_cp("arbitrary", "arbitrary"))(a, w, *after)


def _mm_tn(x, dy, *, blocked, name):
    pair = isinstance(dy, tuple)
    S, K = x.shape
    N = 2 * dy[0].shape[1] if pair else dy.shape[1]
    tk = _pick(K, (512, 256, 128))
    if blocked:
        nq = N // N_CHIPS
        tn = _lane_tile(nq, 256 if pair else 1408)
        nps = nq // tn
        out_spec = pl.BlockSpec((None, tk, tn), lambda i, j: (j // nps, i, j % nps))
        out_shape = jax.ShapeDtypeStruct((N_CHIPS, K, nq), BF16)
    else:
        tn = _lane_tile(N, 1024)
        out_spec = pl.BlockSpec((tk, tn), lambda i, j: (i, j))
        out_shape = jax.ShapeDtypeStruct((K, N), BF16)
    nj = N // tn
    njh = nj // 2

    def body(x_ref, *refs):
        o_ref = refs[-1]
        if pair:
            j = pl.program_id(1)

            @pl.when(j < njh)
            def _():
                o_ref[...] = lax.dot_general(x_ref[...], refs[0][...], _TN, preferred_element_type=F32).astype(o_ref.dtype)

            @pl.when(j >= njh)
            def _():
                o_ref[...] = lax.dot_general(x_ref[...], refs[1][...], _TN, preferred_element_type=F32).astype(o_ref.dtype)
        else:
            o_ref[...] = lax.dot_general(x_ref[...], refs[0][...], _TN, preferred_element_type=F32).astype(o_ref.dtype)

    if pair:
        assert nj % 2 == 0
        dy_specs = [pl.BlockSpec((S, tn), lambda i, j: (0, jnp.minimum(j, njh - 1))),
                    pl.BlockSpec((S, tn), lambda i, j: (0, jnp.maximum(j - njh, 0)))]
        dy_args = list(dy)
    else:
        dy_specs = [pl.BlockSpec((S, tn), lambda i, j: (0, j))]
        dy_args = [dy]
    return pl.pallas_call(
        body, name=name, grid=(K // tk, nj),
        in_specs=[pl.BlockSpec((S, tk), lambda i, j: (0, i))] + dy_specs,
        out_specs=out_spec, out_shape=out_shape,
        compiler_params=_cp("arbitrary", "arbitrary"))(x, *dy_args)


def _rope_tables(S):
    rows = S // GRID_W
    row_ids = jnp.repeat(jnp.arange(rows, dtype=F32), GRID_W)
    col_ids = jnp.tile(jnp.arange(GRID_W, dtype=F32), rows)
    quarter = HEAD_DIM // 4
    inv_freq = ROPE_THETA ** (-jnp.arange(quarter, dtype=F32) / quarter)
    ang_r = row_ids[:, None] * inv_freq[None, :]
    ang_c = col_ids[:, None] * inv_freq[None, :]
    cos = jnp.concatenate([jnp.cos(ang_r)] * 2 + [jnp.cos(ang_c)] * 2, axis=-1)
    sin = jnp.concatenate([-jnp.sin(ang_r), jnp.sin(ang_r), -jnp.sin(ang_c), jnp.sin(ang_c)], axis=-1)
    return cos, sin


def _swap_quarters(x):
    lane = lax.broadcasted_iota(jnp.int32, x.shape, 1)
    first = (lane % (HEAD_DIM // 2)) < (HEAD_DIM // 4)
    return jnp.where(first, pltpu.roll(x, HEAD_DIM - HEAD_DIM // 4, 1), pltpu.roll(x, HEAD_DIM // 4, 1))


def _prep_a_fwd(qkv, cos, sin, gq, gk):
    S, W = qkv.shape
    nrm = A_HEADS + A_KV_HEADS
    ts = _pick(S, (256, 128, 64, 32, 16))

    def body(qkv_ref, cos_ref, sin_ref, gq_ref, gk_ref, o_ref):
        cos_t = cos_ref[...]
        sin_t = sin_ref[...]
        for j in range(nrm):
            sl = slice(j * HEAD_DIM, (j + 1) * HEAD_DIM)
            x = qkv_ref[:, sl]
            g = gq_ref[...] if j < A_HEADS else gk_ref[...]
            r = lax.rsqrt(jnp.mean(x * x, axis=-1, keepdims=True) + EPS)
            n = x * r * g
            o_ref[:, sl] = (n * cos_t + _swap_quarters(n) * sin_t).astype(BF16)
        o_ref[:, nrm * HEAD_DIM:] = qkv_ref[:, nrm * HEAD_DIM:].astype(BF16)

    row = lambda w: pl.BlockSpec((ts, w), lambda i: (i, 0))
    one = pl.BlockSpec((1, HEAD_DIM), lambda i: (0, 0))
    return pl.pallas_call(
        body, name="prep_a_fwd", grid=(S // ts,),
        in_specs=[row(W), row(HEAD_DIM), row(HEAD_DIM), one, one], out_specs=row(W),
        out_shape=jax.ShapeDtypeStruct((S, W), BF16), compiler_params=_cp("arbitrary"))(qkv, cos, sin, gq, gk)


def _prep_a_bwd(dq, dk, dv, qkv, cos, sin, gq, gk):
    S, W = qkv.shape
    nrm = A_HEADS + A_KV_HEADS
    nq, nk = A_HEADS * HEAD_DIM, A_KV_HEADS * HEAD_DIM
    ts = _pick(S, (256, 128, 64, 32, 16))

    def body(dq_ref, dk_ref, dv_ref, qkv_ref, cos_ref, sin_ref, gq_ref, gk_ref, o_ref, dg_ref):
        @pl.when(pl.program_id(0) == 0)
        def _():
            dg_ref[...] = jnp.zeros_like(dg_ref)
        cos_t = cos_ref[...]
        sin_t = sin_ref[...]
        for j in range(nrm):
            sl = slice(j * HEAD_DIM, (j + 1) * HEAD_DIM)
            x = qkv_ref[:, sl]
            if j < A_HEADS:
                dy, g, grow = dq_ref[:, sl], gq_ref[...], 0
            else:
                jj = j - A_HEADS
                dy, g, grow = dk_ref[:, jj * HEAD_DIM:(jj + 1) * HEAD_DIM], gk_ref[...], 1
            r = lax.rsqrt(jnp.mean(x * x, axis=-1, keepdims=True) + EPS)
            xn = x * r
            dn = dy * cos_t + _swap_quarters(dy * sin_t)
            dg_ref[grow:grow + 1, :] += jnp.sum(dn * xn, axis=0, keepdims=True)
            dxn = dn * g
            o_ref[:, sl] = (r * (dxn - xn * jnp.mean(dxn * xn, axis=-1, keepdims=True))).astype(BF16)
        o_ref[:, nrm * HEAD_DIM:] = dv_ref[...].astype(BF16)

    row = lambda w: pl.BlockSpec((ts, w), lambda i: (i, 0))
    one = pl.BlockSpec((1, HEAD_DIM), lambda i: (0, 0))
    return pl.pallas_call(
        body, name="prep_a_bwd", grid=(S // ts,),
        in_specs=[row(nq), row(nk), row(nk), row(W), row(HEAD_DIM), row(HEAD_DIM), one, one],
        out_specs=[row(W), pl.BlockSpec((SUBLANES, HEAD_DIM), lambda i: (0, 0))],
        out_shape=[jax.ShapeDtypeStruct((S, W), BF16), jax.ShapeDtypeStruct((SUBLANES, HEAD_DIM), F32)],
        compiler_params=_cp("arbitrary"))(dq, dk, dv, qkv, cos, sin, gq, gk)


def _flash_a_fwd(qkvh):
    S = qkvh.shape[0]
    grp = A_HEADS // A_KV_HEADS
    tq = _pick(S, (256, 128, 64, 32, 16))
    scale = HEAD_DIM ** -0.5

    def body(q_ref, k_ref, v_ref, o_ref, lse_ref):
        s = lax.dot_general(q_ref[...], k_ref[...], _NT, preferred_element_type=F32) * scale
        m = jnp.max(s, axis=-1, keepdims=True)
        p = jnp.exp(s - m)
        l = jnp.sum(p, axis=-1, keepdims=True)
        pn = (p * (1.0 / l)).astype(BF16)
        o_ref[...] = lax.dot_general(pn, v_ref[...], _NN, preferred_element_type=F32).astype(BF16)
        lse_ref[...] = jnp.broadcast_to(m + jnp.log(l), lse_ref.shape)

    qs = pl.BlockSpec((tq, HEAD_DIM), lambda h, i: (i, h))
    return pl.pallas_call(
        body, name="flash_a_fwd", grid=(A_HEADS, S // tq),
        in_specs=[qs,
                  pl.BlockSpec((S, HEAD_DIM), lambda h, i: (0, A_HEADS + h // grp)),
                  pl.BlockSpec((S, HEAD_DIM), lambda h, i: (0, A_HEADS + A_KV_HEADS + h // grp))],
        out_specs=[qs, qs],
        out_shape=[jax.ShapeDtypeStruct((S, A_HEADS * HEAD_DIM), BF16),
                   jax.ShapeDtypeStruct((S, A_HEADS * HEAD_DIM), F32)],
        compiler_params=_cp("arbitrary", "arbitrary"))(qkvh, qkvh, qkvh)


def _flash_a_bwd(qkvh, do, o, lse):
    S = qkvh.shape[0]
    grp = A_HEADS // A_KV_HEADS
    tq = _pick(S, (256, 128, 64, 32, 16))
    scale = HEAD_DIM ** -0.5

    def body(q_ref, k_ref, v_ref, do_ref, o_ref, lse_ref, dq_ref, dk_ref, dv_ref):
        @pl.when((pl.program_id(1) == 0) & (pl.program_id(2) == 0))
        def _():
            dk_ref[...] = jnp.zeros_like(dk_ref)
            dv_ref[...] = jnp.zeros_like(dv_ref)
        q = q_ref[...]
        k = k_ref[...]
        do_f = do_ref[...]
        do_b = do_f.astype(BF16)
        s = lax.dot_general(q, k, _NT, preferred_element_type=F32) * scale
        p = jnp.exp(s - lse_ref[:, 0:1])
        dp = lax.dot_general(do_b, v_ref[...], _NT, preferred_element_type=F32)
        delta = jnp.sum(do_f * o_ref[...].astype(F32), axis=-1, keepdims=True)
        ds_b = (p * (dp - delta) * scale).astype(BF16)
        dq_ref[...] = lax.dot_general(ds_b, k, _NN, preferred_element_type=F32)
        dk_ref[...] += lax.dot_general(ds_b, q, _TN, preferred_element_type=F32)
        dv_ref[...] += lax.dot_general(p.astype(BF16), do_b, _TN, preferred_element_type=F32)

    qs = pl.BlockSpec((tq, HEAD_DIM), lambda kv, g, i: (i, kv * grp + g))
    kvs = lambda off: pl.BlockSpec((S, HEAD_DIM), lambda kv, g, i: (0, off + kv))
    return pl.pallas_call(
        body, name="flash_a_bwd", grid=(A_KV_HEADS, grp, S // tq),
        in_specs=[qs, kvs(A_HEADS), kvs(A_HEADS + A_KV_HEADS), qs, qs, qs],
        out_specs=[qs, kvs(0), kvs(0)],
        out_shape=[jax.ShapeDtypeStruct((S, A_HEADS * HEAD_DIM), F32),
                   jax.ShapeDtypeStruct((S, A_KV_HEADS * HEAD_DIM), F32),
                   jax.ShapeDtypeStruct((S, A_KV_HEADS * HEAD_DIM), F32)],
        compiler_params=_cp("arbitrary", "arbitrary", "arbitrary"))(qkvh, qkvh, qkvh, do, o, lse)


def _bucket_tables(transposed):
    hs = _half_span()
    tq, kv = 2 * hs, 4 * hs
    nb = REL_BUCKETS // 2
    max_exact = nb // 2
    shape = (kv, tq) if transposed else (tq, kv)
    out = np.zeros((len(B_GROUPS), 3) + shape, np.int32)
    win = np.arange(kv) - hs
    blk = np.arange(tq)
    for g, (_, dil) in enumerate(B_GROUPS):
        for case in range(3):
            inside = ((win >= 0) | (case != 0)) & ((win < tq) | (case != 2))
            if transposed:
                rel = blk[None, :] - win[:, None]
                ok = inside[:, None]
            else:
                rel = win[None, :] - blk[:, None]
                ok = inside[None, :]
            r = rel * dil
            n = np.abs(r)
            nf = np.maximum(n, 1).astype(np.float32)
            large = max_exact + (np.log(nf / np.float32(max_exact)) / np.float32(math.log(REL_MAX_DISTANCE / max_exact))
                                 * np.float32(nb - max_exact)).astype(np.int32)
            large = np.minimum(large, nb - 1)
            bucket = np.where(r > 0, nb, 0) + np.where(n < max_exact, n, large)
            out[g, case] = np.where((np.abs(rel) <= hs) & ok, bucket, -1)
    return out


def _bias_build(rel_bias, buckets):
    G, _, tq, kv = buckets.shape
    hg = B_HEADS_PER_GROUP

    def body(rb_ref, bk_ref, o_ref):
        col = pl.program_id(0) * hg + pl.program_id(2)
        bk = bk_ref[...]
        acc = jnp.full((tq, kv), NEG_INF, F32)
        for b in range(REL_BUCKETS):
            acc = jnp.where(bk == b, rb_ref[b, col], acc)
        o_ref[...] = acc

    return pl.pallas_call(
        body, name="bias_build", grid=(G, 3, hg),
        in_specs=[pl.BlockSpec(memory_space=pltpu.SMEM),
                  pl.BlockSpec((None, None, tq, kv), lambda g, c, h: (g, c, 0, 0))],
        out_specs=pl.BlockSpec((None, None, None, tq, kv), lambda g, c, h: (g, c, h, 0, 0)),
        out_shape=jax.ShapeDtypeStruct((G, 3, hg, tq, kv), F32),
        compiler_params=_cp("arbitrary", "arbitrary", "arbitrary"))(rel_bias, buckets)


def _bias_reduce(dbias_list, buckets):
    G, _, tq, kv = buckets.shape
    hg = B_HEADS_PER_GROUP
    n = len(dbias_list)

    def body(*refs):
        bk_ref, o_ref = refs[n], refs[n + 1]
        first = (pl.program_id(0) == 0) & (pl.program_id(1) == 0) & (pl.program_id(2) == 0)

        @pl.when(first)
        def _():
            o_ref[...] = jnp.zeros_like(o_ref)
        col = pl.program_id(0) * hg + pl.program_id(2)
        db = refs[0][...]
        for r in refs[1:n]:
            db = db + r[...]
        bk = bk_ref[...]
        rows = lax.broadcasted_iota(jnp.int32, (REL_BUCKETS, LANES), 0)
        cols = lax.broadcasted_iota(jnp.int32, (REL_BUCKETS, LANES), 1)
        acc = jnp.zeros((REL_BUCKETS, LANES), F32)
        for b in range(REL_BUCKETS):
            val = jnp.sum(jnp.sum(jnp.where(bk == b, db, 0.0), axis=1, keepdims=True), axis=0, keepdims=True)
            acc = acc + jnp.where((rows == b) & (cols == col), val, 0.0)
        o_ref[...] += acc

    tile = pl.BlockSpec((None, None, None, tq, kv), lambda g, c, h: (g, c, h, 0, 0))
    return pl.pallas_call(
        body, name="bias_reduce", grid=(G, 3, hg),
        in_specs=[tile] * n + [pl.BlockSpec((None, None, tq, kv), lambda g, c, h: (g, c, 0, 0))],
        out_specs=pl.BlockSpec((REL_BUCKETS, LANES), lambda g, c, h: (0, 0)),
        out_shape=jax.ShapeDtypeStruct((REL_BUCKETS, LANES), F32),
        compiler_params=_cp("arbitrary", "arbitrary", "arbitrary"))(*dbias_list, buckets)


def _mm_nn_perm(a, w, g):
    S, K = a.shape
    nq = w.shape[3]
    dil = B_GROUPS[g][1]
    wg3 = 3 * B_HEADS_PER_GROUP * HEAD_DIM
    tn = _pick(nq, (256, 128))
    assert wg3 % tn == 0
    nps, ntile = nq // tn, wg3 // tn
    tm = _pick(S, (1024, 512, 256))
    rows = tm // dil

    def body(a_ref, w_ref, o_ref, acc_ref):
        acc = lax.dot_general(a_ref[...], w_ref[...], _NN, preferred_element_type=F32)
        if dil == 1:
            o_ref[0] = acc.astype(BF16)
        else:
            for k in range(tn // LANES):
                acc_ref[k] = acc[:, k * LANES:(k + 1) * LANES]
            for c in range(dil):
                for k in range(tn // LANES):
                    o_ref[c, :, k * LANES:(k + 1) * LANES] = acc_ref[k, pl.ds(c, rows, stride=dil), :].astype(BF16)

    def w_map(i, j):
        t = g * ntile + j
        return (0, t // nps, 0, t % nps)

    return pl.pallas_call(
        body, name="b_qkv_g%d" % g, grid=(S // tm, ntile),
        in_specs=[pl.BlockSpec((tm, K), lambda i, j: (i, 0)), pl.BlockSpec((None, None, K, tn), w_map)],
        out_specs=pl.BlockSpec((dil, rows, tn), lambda i, j: (0, i, j)),
        out_shape=jax.ShapeDtypeStruct((dil, S // dil, wg3), BF16),
        scratch_shapes=[pltpu.VMEM((tn // LANES, tm, LANES), F32)],
        compiler_params=_cp("arbitrary", "arbitrary"))(a, w)


def _window_specs(S, wg, col):
    hs = _half_span()
    tq = 2 * hs
    per = tq // hs
    return (pl.BlockSpec((tq, wg), lambda i: (i, col)),
            pl.BlockSpec((hs, wg), lambda i: (jnp.maximum(i * per - 1, 0), col)),
            pl.BlockSpec((hs, wg), lambda i: (jnp.minimum((i + 1) * per, S // hs - 1), col)))


def _window_case(i, L):
    per = L // (2 * _half_span())
    r = i % per
    return jnp.where(r == 0, 0, jnp.where(r == per - 1, 2, 1))


def _window(prev_ref, main_ref, next_ref, sl):
    return jnp.concatenate([prev_ref[:, sl], main_ref[:, sl], next_ref[:, sl]], axis=0)


def _battn_fwd(qkvp, bias, g):
    dil, L, wg3 = qkvp.shape
    S = dil * L
    hs = _half_span()
    tq, kvl = 2 * hs, 4 * hs
    hg = B_HEADS_PER_GROUP
    wg = hg * HEAD_DIM
    scale = HEAD_DIM ** -0.5
    flat = qkvp.reshape(S, wg3)

    def body(q_ref, km, kp, kn, vm, vp, vn, b_ref, o_ref, lz_ref):
        case = _window_case(pl.program_id(0), L)
        for h in range(hg):
            sl = slice(h * HEAD_DIM, (h + 1) * HEAD_DIM)
            s = lax.dot_general(q_ref[:, sl], _window(kp, km, kn, sl), _NT, preferred_element_type=F32) * scale
            s = s + b_ref[case, h]
            m = jnp.max(s, axis=-1, keepdims=True)
            p = jnp.exp(s - m)
            l = jnp.sum(p, axis=-1, keepdims=True)
            o_ref[:, sl] = lax.dot_general(p.astype(BF16), _window(vp, vm, vn, sl), _NN, preferred_element_type=F32) / l
            lz_ref[:, sl] = jnp.broadcast_to(m + jnp.log(l), (tq, HEAD_DIM))

    blk = pl.BlockSpec((tq, wg), lambda i: (i, 0))
    o, lz = pl.pallas_call(
        body, name="battn_fwd_g%d" % g, grid=(S // tq,),
        in_specs=[_window_specs(S, wg, 0)[0], *_window_specs(S, wg, 1), *_window_specs(S, wg, 2),
                  pl.BlockSpec((None, 3, hg, tq, kvl), lambda i: (g, 0, 0, 0, 0))],
        out_specs=[blk, blk], out_shape=[jax.ShapeDtypeStruct((S, wg), F32)] * 2,
        compiler_params=_cp("arbitrary"))(flat, flat, flat, flat, flat, flat, flat, bias)
    return o, lz


def _battn_bwd_dq(qkvp, bias, do, o, lz, dlz, g):
    dil, L, wg3 = qkvp.shape
    S = dil * L
    hs = _half_span()
    tq, kvl = 2 * hs, 4 * hs
    hg = B_HEADS_PER_GROUP
    wg = hg * HEAD_DIM
    scale = HEAD_DIM ** -0.5
    flat = qkvp.reshape(S, wg3)

    def body(q_ref, km, kp, kn, vm, vp, vn, b_ref, do_ref, o_ref, lz_ref, dlz_ref, dq_ref, rt_ref, db_ref):
        i = pl.program_id(0)

        @pl.when(i == 0)
        def _():
            db_ref[...] = jnp.zeros_like(db_ref)
        case = _window_case(i, L)
        for h in range(hg):
            sl = slice(h * HEAD_DIM, (h + 1) * HEAD_DIM)
            kw = _window(kp, km, kn, sl)
            do_f = do_ref[:, sl]
            s = lax.dot_general(q_ref[:, sl], kw, _NT, preferred_element_type=F32) * scale + b_ref[case, h]
            p = jnp.exp(s - lz_ref[:, sl][:, 0:1])
            dp = lax.dot_general(do_f.astype(BF16), _window(vp, vm, vn, sl), _NT, preferred_element_type=F32)
            rt = dlz_ref[:, sl][:, 0:1] - jnp.sum(do_f * o_ref[:, sl], axis=-1, keepdims=True)
            ds = p * (dp + rt)
            db_ref[case, h] += ds
            dq_ref[:, sl] = lax.dot_general((ds * scale).astype(BF16), kw, _NN, preferred_element_type=F32)
            rt_ref[:, sl] = jnp.broadcast_to(rt, (tq, HEAD_DIM))

    blk = pl.BlockSpec((tq, wg), lambda i: (i, 0))
    row = jax.ShapeDtypeStruct((S, wg), F32)
    return pl.pallas_call(
        body, name="battn_bwd_dq_g%d" % g, grid=(S // tq,),
        in_specs=[_window_specs(S, wg, 0)[0], *_window_specs(S, wg, 1), *_window_specs(S, wg, 2),
                  pl.BlockSpec((None, 3, hg, tq, kvl), lambda i: (g, 0, 0, 0, 0)), blk, blk, blk, blk],
        out_specs=[blk, blk, pl.BlockSpec((3, hg, tq, kvl), lambda i: (0, 0, 0, 0))],
        out_shape=[row, row, jax.ShapeDtypeStruct((3, hg, tq, kvl), F32)],
        compiler_params=_cp("arbitrary"))(flat, flat, flat, flat, flat, flat, flat, bias, do, o, lz, dlz)


def _battn_bwd_dkv(qkvp, bias_t, do, lz, rt, g):
    dil, L, wg3 = qkvp.shape
    S = dil * L
    hs = _half_span()
    tq, kvl = 2 * hs, 4 * hs
    hg = B_HEADS_PER_GROUP
    wg = hg * HEAD_DIM
    scale = HEAD_DIM ** -0.5
    flat = qkvp.reshape(S, wg3)

    def body(k_ref, v_ref, qm, qp, qn, dom, dop, don, lzm, lzp, lzn, rtm, rtp, rtn, b_ref, dk_ref, dv_ref):
        case = _window_case(pl.program_id(0), L)
        for h in range(hg):
            sl = slice(h * HEAD_DIM, (h + 1) * HEAD_DIM)
            qw = _window(qp, qm, qn, sl)
            dow = _window(dop, dom, don, sl).astype(BF16)
            s = lax.dot_general(qw, k_ref[:, sl], _NT, preferred_element_type=F32) * scale + b_ref[case, h]
            p = jnp.exp(s - _window(lzp, lzm, lzn, sl)[:, 0:1])
            dp = lax.dot_general(dow, v_ref[:, sl], _NT, preferred_element_type=F32)
            ds_b = (p * (dp + _window(rtp, rtm, rtn, sl)[:, 0:1]) * scale).astype(BF16)
            dk_ref[:, sl] = lax.dot_general(ds_b, qw, _TN, preferred_element_type=F32)
            dv_ref[:, sl] = lax.dot_general(p.astype(BF16), dow, _TN, preferred_element_type=F32)

    blk = pl.BlockSpec((tq, wg), lambda i: (i, 0))
    row = jax.ShapeDtypeStruct((S, wg), F32)
    return pl.pallas_call(
        body, name="battn_bwd_dkv_g%d" % g, grid=(S // tq,),
        in_specs=[_window_specs(S, wg, 1)[0], _window_specs(S, wg, 2)[0], *_window_specs(S, wg, 0),
                  *_window_specs(S, wg, 0), *_window_specs(S, wg, 0), *_window_specs(S, wg, 0),
                  pl.BlockSpec((None, 3, hg, kvl, tq), lambda i: (g, 0, 0, 0, 0))],
        out_specs=[blk, blk], out_shape=[row, row],
        compiler_params=_cp("arbitrary"))(flat, flat, flat, flat, flat, do, do, do, lz, lz, lz, rt, rt, rt, bias_t)


def _group_weights(lz_refs, h):
    z = [r[h] for r in lz_refs]
    mx = functools.reduce(jnp.maximum, z)
    e = [jnp.exp(v - mx) for v in z]
    inv = 1.0 / functools.reduce(lambda a, b: a + b, e)
    return [v * inv for v in e]


def _to_token_order(src_ref, dst_ref, dil):
    rows = src_ref.shape[1]
    for k in range(dst_ref.shape[0]):
        sl = slice(k * LANES, (k + 1) * LANES)
        if dil == 1:
            dst_ref[k] = src_ref[0, :, sl]
        else:
            for c in range(dil):
                dst_ref[k, pl.ds(c, rows, stride=dil), :] = src_ref[c, :, sl]


def _to_subsequence_order(src_ref, dst_ref, dil):
    rows = dst_ref.shape[1]
    for k in range(src_ref.shape[0]):
        sl = slice(k * LANES, (k + 1) * LANES)
        if dil == 1:
            dst_ref[0, :, sl] = src_ref[k]
        else:
            for c in range(dil):
                dst_ref[c, :, sl] = src_ref[k, pl.ds(c, rows, stride=dil), :]


def _sub_view(a, dil):
    S, w = a.shape
    return a.reshape(dil, S // dil, w)


def _sub_spec(dil, ts, w):
    return pl.BlockSpec((dil, ts // dil, w), lambda i: (0, i, 0))


def _combine_fwd(os_, lzs):
    G = len(os_)
    S, Wg = os_[0].shape
    hg = B_HEADS_PER_GROUP
    dils = [d for _, d in B_GROUPS]
    ts = _pick(S, (256, 128))

    def body(*refs):
        o_in, lz_in, y_ref = refs[:G], refs[G:2 * G], refs[2 * G]
        o_nat, lz_nat = refs[2 * G + 1:3 * G + 1], refs[3 * G + 1:4 * G + 1]
        for g in range(G):
            _to_token_order(o_in[g], o_nat[g], dils[g])
            _to_token_order(lz_in[g], lz_nat[g], dils[g])
        for h in range(hg):
            w = _group_weights(lz_nat, h)
            for g in range(G):
                y_ref[:, (g * hg + h) * HEAD_DIM:(g * hg + h + 1) * HEAD_DIM] = (w[g] * o_nat[g][h]).astype(BF16)

    specs = [_sub_spec(d, ts, Wg) for d in dils]
    return pl.pallas_call(
        body, name="combine_fwd", grid=(S // ts,), in_specs=specs + specs,
        out_specs=pl.BlockSpec((ts, G * Wg), lambda i: (i, 0)),
        out_shape=jax.ShapeDtypeStruct((S, G * Wg), BF16),
        scratch_shapes=[pltpu.VMEM((hg, ts, HEAD_DIM), F32)] * (2 * G),
        compiler_params=_cp("arbitrary"))(*[_sub_view(a, d) for a, d in zip(os_, dils)],
                                          *[_sub_view(a, d) for a, d in zip(lzs, dils)])


def _combine_bwd(dy, os_, lzs):
    G = len(os_)
    S, Wg = os_[0].shape
    hg = B_HEADS_PER_GROUP
    dils = [d for _, d in B_GROUPS]
    ts = _pick(S, (128,))

    def body(*refs):
        dy_ref, o_in, lz_in = refs[0], refs[1:1 + G], refs[1 + G:1 + 2 * G]
        do_out, dlz_out = refs[1 + 2 * G:1 + 3 * G], refs[1 + 3 * G:1 + 4 * G]
        scr = refs[1 + 4 * G:]
        o_nat, lz_nat, do_nat, dlz_nat = scr[:G], scr[G:2 * G], scr[2 * G:3 * G], scr[3 * G:4 * G]
        for g in range(G):
            _to_token_order(o_in[g], o_nat[g], dils[g])
            _to_token_order(lz_in[g], lz_nat[g], dils[g])
        for h in range(hg):
            w = _group_weights(lz_nat, h)
            dw = []
            for g in range(G):
                dyg = dy_ref[:, (g * hg + h) * HEAD_DIM:(g * hg + h + 1) * HEAD_DIM]
                dw.append(jnp.sum(dyg * o_nat[g][h], axis=-1, keepdims=True))
                do_nat[g][h] = w[g] * dyg
            tot = functools.reduce(lambda a, b: a + b, [w[g] * dw[g] for g in range(G)])
            for g in range(G):
                dlz_nat[g][h] = w[g] * (dw[g] - tot)
        for g in range(G):
            _to_subsequence_order(do_nat[g], do_out[g], dils[g])
            _to_subsequence_order(dlz_nat[g], dlz_out[g], dils[g])

    specs = [_sub_spec(d, ts, Wg) for d in dils]
    outs = pl.pallas_call(
        body, name="combine_bwd", grid=(S // ts,),
        in_specs=[pl.BlockSpec((ts, G * Wg), lambda i: (i, 0))] + specs + specs,
        out_specs=specs + specs,
        out_shape=[jax.ShapeDtypeStruct((d, S // d, Wg), F32) for d in dils] * 2,
        scratch_shapes=[pltpu.VMEM((hg, ts, HEAD_DIM), F32)] * (4 * G),
        compiler_params=_cp("arbitrary"))(dy, *[_sub_view(a, d) for a, d in zip(os_, dils)],
                                          *[_sub_view(a, d) for a, d in zip(lzs, dils)])
    flat = [a.reshape(S, Wg) for a in outs]
    return flat[:G], flat[G:]


def _concat_cast(parts, dils):
    S = parts[0].shape[0]
    widths = [p.shape[1] for p in parts]
    n = len(parts)
    ts = _pick(S, (256, 128))

    def body(*refs):
        o_ref, nat = refs[n], refs[n + 1]
        off = 0
        for r, w, d in zip(refs, widths, dils):
            _to_token_order(r, nat, d)
            for k in range(w // LANES):
                o_ref[:, off + k * LANES:off + (k + 1) * LANES] = nat[k].astype(BF16)
            off += w

    assert len(set(widths)) == 1
    return pl.pallas_call(
        body, name="concat_cast", grid=(S // ts,),
        in_specs=[_sub_spec(d, ts, w) for w, d in zip(widths, dils)],
        out_specs=pl.BlockSpec((ts, sum(widths)), lambda i: (i, 0)),
        out_shape=jax.ShapeDtypeStruct((S, sum(widths)), BF16),
        scratch_shapes=[pltpu.VMEM((widths[0] // LANES, ts, LANES), F32)],
        compiler_params=_cp("arbitrary"))(*[_sub_view(p, d) for p, d in zip(parts, dils)])


def _ffn_specs(S, dff, cq, ts, tc, layer, order):
    nfc = dff // tc
    nps = cq // tc
    hb = ts // SUBLANES
    nrow8 = S // SUBLANES

    def u_main(half):
        return pl.BlockSpec((ts, tc), lambda *g: (order(*g)[0], order(*g)[1] % nfc + half * nfc))

    def u_prev(half):
        return pl.BlockSpec((SUBLANES, tc), lambda *g: (jnp.maximum(order(*g)[0] * hb - 1, 0),
                                                         order(*g)[1] % nfc + half * nfc))

    def u_next(half):
        return pl.BlockSpec((SUBLANES, tc), lambda *g: (jnp.minimum((order(*g)[0] + 1) * hb, nrow8 - 1),
                                                         order(*g)[1] % nfc + half * nfc))

    def cw(half):
        def im(*g):
            jj = order(*g)[1] % nfc + half * nfc
            return (layer, jj // nps, 0, jj % nps)
        return pl.BlockSpec((None, None, 3, tc), im)

    def cb(half):
        return pl.BlockSpec((None, 1, tc), lambda *g: (layer, 0, order(*g)[1] % nfc + half * nfc))

    return nfc, u_main, u_prev, u_next, cw, cb


def _ffn_act_fwd(u, cw_full, cb3, layer):
    S, two_dff = u.shape
    dff = two_dff // 2
    cq = cw_full.shape[3]
    ts = _pick(S, (512, 256, 128, 64, 32, 16))
    tc = _pick(cq, (256, 128))
    order = lambda j, i: (i, j)
    nfc, u_main, u_prev, u_next, cw, cb = _ffn_specs(S, dff, cq, ts, tc, layer, order)
    nrow = S // ts

    def body(ug, ugp, ugn, uv, uvp, uvn, wg, wv, bg, bv, a_ref):
        i = pl.program_id(1)
        row = lax.broadcasted_iota(jnp.int32, (ts, tc), 0)

        def conv(x_ref, p_ref, n_ref, w_ref, b_ref):
            x = x_ref[...]
            prev = jnp.where(i > 0, p_ref[SUBLANES - 1:SUBLANES, :], 0.0)
            nxt = jnp.where(i < nrow - 1, n_ref[0:1, :], 0.0)
            xm = jnp.where(row == 0, prev, pltpu.roll(x, 1, 0))
            xp = jnp.where(row == ts - 1, nxt, pltpu.roll(x, ts - 1, 0))
            return w_ref[0:1, :] * xm + w_ref[1:2, :] * x + w_ref[2:3, :] * xp + b_ref[...]

        gc = conv(ug, ugp, ugn, wg, bg)
        vc = conv(uv, uvp, uvn, wv, bv)
        a_ref[...] = (gc * (1.0 / (1.0 + jnp.exp(-gc))) * vc).astype(BF16)

    return pl.pallas_call(
        body, name="ffn_act_fwd", grid=(nfc, nrow),
        in_specs=[u_main(0), u_prev(0), u_next(0), u_main(1), u_prev(1), u_next(1), cw(0), cw(1), cb(0), cb(1)],
        out_specs=pl.BlockSpec((ts, tc), lambda j, i: (i, j)),
        out_shape=jax.ShapeDtypeStruct((S, dff), BF16),
        compiler_params=_cp("arbitrary", "arbitrary"))(u, u, u, u, u, u, cw_full, cw_full, cb3, cb3)


def _ffn_act_bwd(u, da, cw_full, cb3, layer):
    S, two_dff = u.shape
    dff = two_dff // 2
    cq = cw_full.shape[3]
    ts = _pick(S, (512, 256, 128, 64, 32, 16))
    tc = _pick(cq, (256, 128))
    order = lambda j, i: (i, j)
    nfc, u_main, u_prev, u_next, cw, cb = _ffn_specs(S, dff, cq, ts, tc, layer, order)
    nrow = S // ts
    hb = ts // SUBLANES
    te = ts + 2 * SUBLANES
    da_main = pl.BlockSpec((ts, tc), lambda j, i: (i, j))
    da_prev = pl.BlockSpec((SUBLANES, tc), lambda j, i: (jnp.maximum(i * hb - 1, 0), j))
    da_next = pl.BlockSpec((SUBLANES, tc), lambda j, i: (jnp.minimum((i + 1) * hb, S // SUBLANES - 1), j))
    main = slice(SUBLANES, SUBLANES + ts)

    def body(ug, ugp, ugn, uv, uvp, uvn, dam, dap, dan, wg, wv, bg, bv, dug_ref, duv_ref, accg_ref, accv_ref):
        i = pl.program_id(1)

        @pl.when(i == 0)
        def _():
            accg_ref[...] = jnp.zeros_like(accg_ref)
            accv_ref[...] = jnp.zeros_like(accv_ref)

        def ext(m, p, n):
            return jnp.concatenate([jnp.where(i > 0, p[...], 0.0), m[...], jnp.where(i < nrow - 1, n[...], 0.0)], axis=0)

        def shift(x):
            return pltpu.roll(x, 1, 0), pltpu.roll(x, te - 1, 0)

        xg, xv, dae = ext(ug, ugp, ugn), ext(uv, uvp, uvn), ext(dam, dap, dan)
        xgm, xgp = shift(xg)
        xvm, xvp = shift(xv)
        gc = wg[0:1, :] * xgm + wg[1:2, :] * xg + wg[2:3, :] * xgp + bg[...]
        vc = wv[0:1, :] * xvm + wv[1:2, :] * xv + wv[2:3, :] * xvp + bv[...]
        sig = 1.0 / (1.0 + jnp.exp(-gc))
        silu = gc * sig
        dcg = dae * vc * (sig * (1.0 + gc * (1.0 - sig)))
        dcv = dae * silu

        def finish(dc, x, xm, xp, w_ref, du_ref, acc_ref):
            dm, dp = shift(dc)
            du = w_ref[0:1, :] * dp + w_ref[1:2, :] * dc + w_ref[2:3, :] * dm
            du_ref[...] = du[main, :].astype(BF16)
            dcm = dc[main, :]
            acc_ref[0:1, :] += jnp.sum(dcm * xm[main, :], axis=0, keepdims=True)
            acc_ref[1:2, :] += jnp.sum(dcm * x[main, :], axis=0, keepdims=True)
            acc_ref[2:3, :] += jnp.sum(dcm * xp[main, :], axis=0, keepdims=True)
            acc_ref[3:4, :] += jnp.sum(dcm, axis=0, keepdims=True)

        finish(dcg, xg, xgm, xgp, wg, dug_ref, accg_ref)
        finish(dcv, xv, xvm, xvp, wv, duv_ref, accv_ref)

    blk = pl.BlockSpec((ts, tc), lambda j, i: (i, j))
    acc = pl.BlockSpec((SUBLANES, tc), lambda j, i: (0, j))
    dug, duv, accg, accv = pl.pallas_call(
        body, name="ffn_act_bwd", grid=(nfc, nrow),
        in_specs=[u_main(0), u_prev(0), u_next(0), u_main(1), u_prev(1), u_next(1), da_main, da_prev, da_next,
                  cw(0), cw(1), cb(0), cb(1)],
        out_specs=[blk, blk, acc, acc],
        out_shape=[jax.ShapeDtypeStruct((S, dff), BF16)] * 2 + [jax.ShapeDtypeStruct((SUBLANES, dff), F32)] * 2,
        compiler_params=_cp("arbitrary", "arbitrary"))(u, u, u, u, u, u, da, da, da, cw_full, cw_full, cb3, cb3)
    return (dug, duv), jnp.concatenate([accg, accv], axis=1)


def _my_chip():
    return 2 * lax.axis_index("x") + lax.axis_index("y")


def _into_full(w, layer, dtype):
    L, a, b = w.shape
    tr = _pick(a, (512, 256, 128, 64, 32, 16, 8))

    def body(w_ref, o_ref):
        o_ref[...] = w_ref[...].astype(dtype)

    return pl.pallas_call(
        body, name="into_full", grid=(a // tr,),
        in_specs=[pl.BlockSpec((None, tr, b), lambda i: (layer, i, 0))],
        out_specs=pl.BlockSpec((None, None, tr, b), lambda i: (0, _my_chip(), i, 0)),
        out_shape=jax.ShapeDtypeStruct((1, N_CHIPS, a, b), dtype),
        compiler_params=_cp("arbitrary"))(w)


def _adam_math(w, g, m, v):
    m = ADAM_B1 * m + (1.0 - ADAM_B1) * g
    v = ADAM_B2 * v + (1.0 - ADAM_B2) * (g * g)
    m_hat = m / (1.0 - ADAM_B1 ** ADAM_STEP)
    v_hat = v / (1.0 - ADAM_B2 ** ADAM_STEP)
    delta = -ADAM_LR * (m_hat / (jnp.sqrt(v_hat) + ADAM_EPS) + ADAM_WD * w)
    return delta, m, v


def _adamw(w, g, m, v):
    R, C = w.shape
    tr = _pick(R, (128, 64, 32, 16, 8)) if R % SUBLANES == 0 and C % LANES == 0 else R

    def body(w_ref, g_ref, m_ref, v_ref, d_ref, nm_ref, nv_ref):
        d, nm, nv = _adam_math(w_ref[...], g_ref[...], m_ref[...], v_ref[...])
        d_ref[...] = d
        nm_ref[...] = nm
        nv_ref[...] = nv

    spec = pl.BlockSpec((tr, C), lambda i: (i, 0))
    return pl.pallas_call(
        body, name="adamw", grid=(R // tr,), in_specs=[spec] * 4, out_specs=[spec] * 3,
        out_shape=[jax.ShapeDtypeStruct((R, C), F32)] * 3, compiler_params=_cp("arbitrary"))(w, g, m, v)


ANY = pl.BlockSpec(memory_space=pl.ANY)


def _position():
    x, y, c = lax.axis_index("x"), lax.axis_index("y"), lax.axis_index("c")
    chips = [(1 - x, y), (x, 1 - y), (1 - x, 1 - y)]
    return x, y, c, chips


HBM = pl.BlockSpec(memory_space=pltpu.HBM)
SEM = pl.BlockSpec(memory_space=pltpu.SEMAPHORE)
EFFECT = pltpu.SideEffectType.DATAFLOW_SIDE_EFFECTING


def _in_hbm(a):
    return pltpu.with_memory_space_constraint(a, pltpu.HBM)


def _shard_half(buf, shape, chip, half):
    _, _, a, b = shape
    p = 2 * chip[0] + chip[1]
    if a % (4 * SUBLANES) == 0:
        return buf.at[0, p, pl.ds(half * (a // 2), a // 2)]
    return buf.at[0, p, :, pl.ds(half * (b // 2), b // 2)]


def _gather_copy(buf, shape, chip, half, to, send, recv, k):
    part = _shard_half(buf, shape, chip, half)
    return pltpu.make_async_remote_copy(src_ref=part, dst_ref=part, send_sem=send.at[k], recv_sem=recv.at[k],
                                        device_id=to, device_id_type=MESH)


def _gather_start(fulls, name):
    n = len(fulls)

    def body(*refs):
        send, recv, buf, token = refs[n], refs[n + 1], refs[n + 2:2 * n + 2], refs[2 * n + 2]
        x, y, c, chips = _position()
        for t in range(n):
            for j in range(3):
                _gather_copy(buf[t], fulls[t].shape, (x, y), c, (*chips[j], c), send, recv, 3 * t + j).start()
        token[...] = jnp.zeros_like(token)

    outs = pl.pallas_call(
        body, name=name, in_specs=[HBM] * n,
        out_specs=[SEM, SEM] + [HBM] * n + [pl.BlockSpec(memory_space=pltpu.VMEM)],
        out_shape=[pltpu.SemaphoreType.DMA((3 * n,)), pltpu.SemaphoreType.DMA((3 * n,))]
        + [pltpu.HBM(f.shape, f.dtype) for f in fulls] + [jax.ShapeDtypeStruct((SUBLANES, LANES), F32)],
        input_output_aliases={t: 2 + t for t in range(n)},
        compiler_params=pltpu.CompilerParams(has_side_effects=EFFECT))(*[_in_hbm(f) for f in fulls])
    return outs[0], outs[1], list(outs[2:2 + n]), outs[2 + n]


def _gather_wait(send, recv, fulls, after, name):
    n = len(fulls)

    def body(*refs):
        buf, send_ref, recv_ref = refs[:n], refs[n], refs[n + 1]
        x, y, c, chips = _position()
        for t in range(n):
            for j in range(3):
                _gather_copy(buf[t], fulls[t].shape, (x, y), c, (*chips[j], c), send_ref, recv_ref, 3 * t + j).wait_send()
                _gather_copy(buf[t], fulls[t].shape, chips[j], c, (*chips[j], c), send_ref, recv_ref, 3 * t + j).wait_recv()

    outs = pl.pallas_call(
        body, name=name, in_specs=[HBM] * n + [SEM, SEM, ANY], out_specs=[HBM] * n,
        out_shape=[pltpu.HBM(f.shape, f.dtype) for f in fulls],
        input_output_aliases={t: t for t in range(n)},
        compiler_params=pltpu.CompilerParams(has_side_effects=EFFECT))(*fulls, send, recv, after)
    return list(outs)


def _gather_forward(fulls):
    n = len(fulls)

    def body(*refs):
        buf, send, recv = refs[n:2 * n], refs[2 * n], refs[2 * n + 1]
        x, y, c, chips = _position()
        sib = (x, y, 1 - c)
        cps = [_gather_copy(buf[t], fulls[t].shape, chips[j], c, sib, send, recv, 3 * t + j)
               for t in range(n) for j in range(3)]
        for cp in cps:
            cp.start()
        for t in range(n):
            for j in range(3):
                _gather_copy(buf[t], fulls[t].shape, chips[j], 1 - c, sib, send, recv, 3 * t + j).wait_recv()
        for cp in cps:
            cp.wait_send()

    return pl.pallas_call(
        body, name="gather_forward", in_specs=[ANY] * n, out_specs=[ANY] * n,
        out_shape=[jax.ShapeDtypeStruct(f.shape, f.dtype) for f in fulls],
        input_output_aliases={t: t for t in range(n)},
        scratch_shapes=[pltpu.SemaphoreType.DMA((3 * n,)), pltpu.SemaphoreType.DMA((3 * n,))])(*fulls)


def _allreduce_small(part):
    M, C = part.shape
    n_dev = 2 * N_CHIPS

    def body(x_ref, sum_ref, all_ref, send, recv, local):
        x, y, c, chips = _position()
        me, sib = (x, y, c), (x, y, 1 - c)

        def rows(px, py, pc):
            return all_ref.at[pl.ds((4 * px + 2 * py + pc) * M, M), :]

        def copy(k, block, to, src=None):
            return pltpu.make_async_remote_copy(
                src_ref=rows(*block) if src is None else src, dst_ref=rows(*block),
                send_sem=send.at[k], recv_sem=recv.at[k], device_id=to, device_id_type=MESH)

        mine = pltpu.make_async_copy(x_ref, rows(*me), local)
        mine.start()
        first = [copy(0, me, sib, src=x_ref)] + [copy(1 + j, me, (*chip, c), src=x_ref) for j, chip in enumerate(chips)]
        for cp in first:
            cp.start()
        passed = [copy(4 + j, (*chip, c), sib) for j, chip in enumerate(chips)]
        for j, chip in enumerate(chips):
            copy(1 + j, (*chip, c), me).wait_recv()
            passed[j].start()
        copy(0, sib, me).wait_recv()
        for j, chip in enumerate(chips):
            copy(4 + j, (*chip, 1 - c), me).wait_recv()
        for cp in first + passed:
            cp.wait_send()
        mine.wait()
        acc = all_ref[0:M, :]
        for d in range(1, n_dev):
            acc = acc + all_ref[d * M:(d + 1) * M, :]
        sum_ref[...] = acc

    vm = pl.BlockSpec(memory_space=pltpu.VMEM)
    return pl.pallas_call(
        body, name="allreduce_small", in_specs=[vm], out_specs=[vm],
        out_shape=[jax.ShapeDtypeStruct((M, C), F32)],
        scratch_shapes=[pltpu.VMEM((n_dev * M, C), F32), pltpu.SemaphoreType.DMA((7,)),
                        pltpu.SemaphoreType.DMA((7,)), pltpu.SemaphoreType.DMA],
        compiler_params=pltpu.CompilerParams(vmem_limit_bytes=VMEM_LIMIT))(part)[0]


N_PEERS = 2 * N_CHIPS - 1


def _peers():
    x, y, c, chips = _position()
    return [(x, y, 1 - c)] + [(*ch, c) for ch in chips] + [(*ch, 1 - c) for ch in chips]


def _reduce_copy(src, dst, peers, send, recv, t, r):
    px, py, pc = peers[r]
    return pltpu.make_async_remote_copy(
        src_ref=src.at[2 * px + py, pc], dst_ref=dst.at[r], send_sem=send.at[N_PEERS * t + r],
        recv_sem=recv.at[N_PEERS * t + r], device_id=peers[r], device_id_type=MESH)


def _reduce_start(grads, name):
    n = len(grads)
    lands = [lax.empty((N_PEERS,) + g.shape[2:], BF16) for g in grads]

    def body(*refs):
        send, recv = refs[2 * n], refs[2 * n + 1]
        src, dst, token = refs[2 * n + 2:3 * n + 2], refs[3 * n + 2:4 * n + 2], refs[4 * n + 2]
        peers = _peers()
        for t in range(n):
            for r in range(N_PEERS):
                _reduce_copy(src[t], dst[t], peers, send, recv, t, r).start()
        token[...] = jnp.zeros_like(token)

    outs = pl.pallas_call(
        body, name=name, in_specs=[HBM] * (2 * n),
        out_specs=[SEM, SEM] + [HBM] * (2 * n) + [pl.BlockSpec(memory_space=pltpu.VMEM)],
        out_shape=[pltpu.SemaphoreType.DMA((N_PEERS * n,)), pltpu.SemaphoreType.DMA((N_PEERS * n,))]
        + [pltpu.HBM(a.shape, a.dtype) for a in grads + lands] + [jax.ShapeDtypeStruct((SUBLANES, LANES), F32)],
        input_output_aliases={t: 2 + t for t in range(2 * n)},
        compiler_params=pltpu.CompilerParams(has_side_effects=EFFECT))(*[_in_hbm(a) for a in grads + lands])
    return outs[0], outs[1], list(outs[2:2 + n]), list(outs[2 + n:2 + 2 * n]), outs[2 + 2 * n]


def _reduce_wait(send, recv, grads, lands, after, name):
    n = len(grads)

    def body(*refs):
        src, dst, send_ref, recv_ref = refs[:n], refs[n:2 * n], refs[2 * n], refs[2 * n + 1]
        peers = _peers()
        for t in range(n):
            for r in range(N_PEERS):
                cp = _reduce_copy(src[t], dst[t], peers, send_ref, recv_ref, t, r)
                cp.wait_send()
                cp.wait_recv()

    outs = pl.pallas_call(
        body, name=name, in_specs=[HBM] * (2 * n) + [SEM, SEM, ANY], out_specs=[HBM] * (2 * n),
        out_shape=[pltpu.HBM(a.shape, a.dtype) for a in grads + lands],
        input_output_aliases={t: t for t in range(2 * n)},
        compiler_params=pltpu.CompilerParams(has_side_effects=EFFECT))(*grads, *lands, send, recv, after)
    return list(outs[:n]), list(outs[n:])


def _add_pieces(grad, land, stack, layer):
    _, _, R, C = grad.shape
    tr = _pick(R, (256, 128, 64, 32, 16))

    def body(g_ref, r_ref, stack_ref, o_ref):
        acc = g_ref[...].astype(F32)
        for r in range(N_PEERS):
            acc = acc + r_ref[r].astype(F32)
        o_ref[...] = acc

    return pl.pallas_call(
        body, name="add_pieces", grid=(R // tr,),
        in_specs=[pl.BlockSpec((None, None, tr, C), lambda i: (_my_chip(), lax.axis_index("c"), i, 0)),
                  pl.BlockSpec((N_PEERS, tr, C), lambda i: (0, i, 0)),
                  ANY],
        out_specs=pl.BlockSpec((None, None, tr, C), lambda i: (layer, lax.axis_index("c"), i, 0)),
        out_shape=jax.ShapeDtypeStruct(stack.shape, F32), input_output_aliases={2: 0},
        compiler_params=_cp("arbitrary"))(grad, land, stack)


def _ag_sibling(stacks):
    n = len(stacks)
    offs = np.cumsum([0] + [s.shape[0] for s in stacks])

    def body(*refs):
        buf, send, recv = refs[n:2 * n], refs[2 * n], refs[2 * n + 1]
        x, y, c, _ = _position()

        def copy(t, l, half):
            part = buf[t].at[l, half]
            return pltpu.make_async_remote_copy(
                src_ref=part, dst_ref=part, send_sem=send.at[int(offs[t]) + l], recv_sem=recv.at[int(offs[t]) + l],
                device_id=(x, y, 1 - c), device_id_type=MESH)

        cps = [copy(t, l, c) for t in range(n) for l in range(stacks[t].shape[0])]
        for cp in cps:
            cp.start()
        for t in range(n):
            for l in range(stacks[t].shape[0]):
                copy(t, l, 1 - c).wait_recv()
        for cp in cps:
            cp.wait_send()

    return pl.pallas_call(
        body, name="ag_sibling", in_specs=[ANY] * n, out_specs=[ANY] * n,
        out_shape=[jax.ShapeDtypeStruct(s.shape, F32) for s in stacks],
        input_output_aliases={t: t for t in range(n)},
        scratch_shapes=[pltpu.SemaphoreType.DMA((int(offs[-1]),)), pltpu.SemaphoreType.DMA((int(offs[-1]),))])(*stacks)


def _split8(dw, blocked):
    if blocked:
        p, k, nq = dw.shape
        return dw.reshape(p, 2, k // 2, nq)
    k, n = dw.shape
    return dw.reshape(N_CHIPS, 2, k // (2 * N_CHIPS), n)


def kernel(x, a_w_qkv, a_w_o, a_q_gain, a_k_gain, b_w_qkv, b_w_o, rel_bias, mix_norm, ffn_norm, w_up, conv_w, conv_b, w_down, final_norm, loss_target, m_a_w_qkv, m_a_w_o, m_a_q_gain, m_a_k_gain, m_b_w_qkv, m_b_w_o, m_rel_bias, m_mix_norm, m_ffn_norm, m_w_up, m_conv_w, m_conv_b, m_w_down, m_final_norm, v_a_w_qkv, v_a_w_o, v_a_q_gain, v_a_k_gain, v_b_w_qkv, v_b_w_o, v_rel_bias, v_mix_norm, v_ffn_norm, v_w_up, v_conv_w, v_conv_b, v_w_down, v_final_norm):
    S, D = x.shape[1], x.shape[2]
    h = x.reshape(S, D)
    target = loss_target.reshape(S, D)
    hg = B_HEADS_PER_GROUP
    G = len(B_GROUPS)
    n_a, n_b = a_w_qkv.shape[0], b_w_qkv.shape[0]
    depth = w_up.shape[0]
    cx, cy = lax.axis_index("x"), lax.axis_index("y")

    big = dict(a_w_qkv=a_w_qkv, a_w_o=a_w_o, b_w_qkv=b_w_qkv, b_w_o=b_w_o, w_up=w_up, w_down=w_down)
    blocked = dict(a_w_qkv=True, a_w_o=False, b_w_qkv=True, b_w_o=False, w_up=True, w_down=False)
    names = list(big)
    srcs = dict(big, conv_w=conv_w)
    started = []
    for i in range(depth):
        mix = [("a_w_qkv", i // 2), ("a_w_o", i // 2)] if i % 2 == 0 else [("b_w_qkv", i // 2), ("b_w_o", i // 2)]
        rest = [("w_up", i), ("conv_w", i), ("w_down", i)]
        stages = [mix[:1], mix[1:] + rest] if i == 0 else [mix + rest]
        started.append([])
        for s, keys in enumerate(stages):
            bufs = [_into_full(srcs[k], l, F32 if k == "conv_w" else BF16) for k, l in keys]
            started[i].append((keys,) + _gather_start(bufs, "gather_start_%d_%d" % (i, s)))
    cb3 = conv_b.reshape(depth, 1, conv_b.shape[1])

    cos, sin = _rope_tables(S)
    buckets = jnp.asarray(_bucket_tables(False))
    bias = _bias_build(rel_bias, buckets)
    bias_t = _bias_build(rel_bias, jnp.asarray(_bucket_tables(True)))

    saved = []
    for i in range(depth):
        j = i // 2
        wl = {}

        def arrive(s, after):
            keys, send, recv, bufs, _ = started[i][s]
            bufs = _gather_forward(_gather_wait(send, recv, bufs, after, "gather_wait_%d_%d" % (i, s)))
            for (k, _), buf in zip(keys, bufs):
                _, _, a, b = buf.shape
                wl[k] = buf if k == "conv_w" or blocked[k] else buf.reshape(1, N_CHIPS * a, b)

        arrive(0, h)
        sv = dict(h0=h, w=wl)
        hn = _rms_fwd(h, mix_norm[i:i + 1], after=[st[4] for layer in started for st in layer] if i == 0 else ())
        sv["hn"] = hn
        if i % 2 == 0:
            qkv = _mm_nn(hn, wl["a_w_qkv"], 0, blocked=True, name="a_qkv")
            qkvh = _prep_a_fwd(qkv, cos, sin, a_q_gain[j:j + 1], a_k_gain[j:j + 1])
            o, lse = _flash_a_fwd(qkvh)
            if len(started[i]) > 1:
                arrive(1, o)
            sv.update(qkv=qkv, qkvh=qkvh, o=o, lse=lse)
            h = _mm_nn(o, wl["a_w_o"], 0, blocked=False, res=h, name="a_out")
        else:
            qkvp = [_mm_nn_perm(hn, wl["b_w_qkv"], g) for g in range(G)]
            os_, lzs = [], []
            for g in range(G):
                o_g, lz_g = _battn_fwd(qkvp[g], bias, g)
                os_.append(o_g)
                lzs.append(lz_g)
            y = _combine_fwd(os_, lzs)
            sv.update(qkvp=qkvp, os=os_, lzs=lzs, y=y)
            h = _mm_nn(y, wl["b_w_o"], 0, blocked=False, res=h, name="b_out")
        sv["h1"] = h
        hf = _rms_fwd(h, ffn_norm[i:i + 1])
        u = _mm_nn(hf, wl["w_up"], 0, blocked=True, name="ffn_up")
        act = _ffn_act_fwd(u, wl["conv_w"], cb3[i:i + 1], 0)
        sv.update(hf=hf, u=u, act=act)
        h = _mm_nn(act, wl["w_down"], 0, blocked=False, res=h, name="ffn_down")
        saved.append(sv)

    loss_blk, dh, dh_b, dg_final = _final_loss(h, final_norm.reshape(1, D), target)

    dws = {k: [None] * big[k].shape[0] for k in names}
    d_mix, d_ffn, d_convw, d_convb = [None] * depth, [None] * depth, [None] * depth, [None] * depth
    d_gq, d_gk = [None] * n_a, [None] * n_a
    dbias_list = []
    pending = []

    def start_reduce(keys, tag):
        pieces = [_split8(dws[k][l], blocked[k]) for k, l in keys]
        send, recv, pieces, lands, token = _reduce_start(pieces, "reduce_start_" + tag)
        pending.append((keys, send, recv, pieces, lands, tag))
        return (token,)

    tok = ()
    for i in reversed(range(depth)):
        j = i // 2
        sv = saved[i]
        wl = sv["w"]
        da = _mm_nt(dh_b, wl["w_down"], 0, blocked=False, name="ffn_down_dx", after=tok)
        dws["w_down"][i] = _mm_tn(sv["act"], dh_b, blocked=False, name="ffn_down_dw")
        du, dconv = _ffn_act_bwd(sv["u"], da, wl["conv_w"], cb3[i:i + 1], 0)
        d_convw[i], d_convb[i] = dconv[0:3], dconv[3]
        dhf = _mm_nt(du, wl["w_up"], 0, blocked=True, name="ffn_up_dx")
        dws["w_up"][i] = _mm_tn(sv["hf"], du, blocked=True, name="ffn_up_dw")
        dh, dh_b, dg = _rms_bwd(dhf, sv["h1"], ffn_norm[i:i + 1], dh)
        d_ffn[i] = dg[0]
        tok = start_reduce([("w_down", i), ("w_up", i)], "ffn%d" % i)
        if i % 2 == 0:
            do = _mm_nt(dh_b, wl["a_w_o"], 0, blocked=False, name="a_out_dx", after=tok)
            dws["a_w_o"][j] = _mm_tn(sv["o"], dh_b, blocked=False, name="a_out_dw")
            dq, dk, dv = _flash_a_bwd(sv["qkvh"], do, sv["o"], sv["lse"])
            dqkv, dgain = _prep_a_bwd(dq, dk, dv, sv["qkv"], cos, sin, a_q_gain[j:j + 1], a_k_gain[j:j + 1])
            d_gq[j], d_gk[j] = dgain[0], dgain[1]
            dhn = _mm_nt(dqkv, wl["a_w_qkv"], 0, blocked=True, name="a_qkv_dx")
            dws["a_w_qkv"][j] = _mm_tn(sv["hn"], dqkv, blocked=True, name="a_qkv_dw")
            mix_keys = [("a_w_o", j), ("a_w_qkv", j)]
        else:
            dy = _mm_nt(dh_b, wl["b_w_o"], 0, blocked=False, name="b_out_dx", after=tok)
            dws["b_w_o"][j] = _mm_tn(sv["y"], dh_b, blocked=False, name="b_out_dw")
            dos, dlzs = _combine_bwd(dy, sv["os"], sv["lzs"])
            parts = []
            for g in range(G):
                dq, rt, db = _battn_bwd_dq(sv["qkvp"][g], bias, dos[g], sv["os"][g], sv["lzs"][g], dlzs[g], g)
                dk, dv = _battn_bwd_dkv(sv["qkvp"][g], bias_t, dos[g], sv["lzs"][g], rt, g)
                parts += [dq, dk, dv]
                dbias_list.append((g, db))
            dqkv = _concat_cast(parts, [d for _, d in B_GROUPS for _ in range(3)])
            dhn = _mm_nt(dqkv, wl["b_w_qkv"], 0, blocked=True, name="b_qkv_dx")
            dws["b_w_qkv"][j] = _mm_tn(sv["hn"], dqkv, blocked=True, name="b_qkv_dw")
            mix_keys = [("b_w_o", j), ("b_w_qkv", j)]
        dh, dh_b, dg = _rms_bwd(dhn, sv["h0"], mix_norm[i:i + 1], dh)
        d_mix[i] = dg[0]
        tok = start_reduce(mix_keys, "mix%d" % i)
    grad_x = dh.reshape(x.shape)

    dbias_layers = [jnp.stack([db for g2, db in dbias_list[l * G:(l + 1) * G]]) for l in range(n_b)]
    d_rel = _bias_reduce(dbias_layers, buckets)[:, :G * hg]

    small = [jnp.stack(d_gq), jnp.stack(d_gk), d_rel, jnp.stack(d_mix), jnp.stack(d_ffn), jnp.stack(d_convw),
             jnp.stack(d_convb), dg_final[0]]
    sizes = [int(np.prod(s.shape)) for s in small]
    flat = jnp.concatenate([s.reshape(-1) for s in small])
    rows = -(-flat.shape[0] // (LANES * SUBLANES)) * SUBLANES
    flat = jnp.pad(flat, (0, rows * LANES - flat.shape[0])).reshape(rows, LANES)
    tot = _allreduce_small(flat).reshape(-1)
    offs = np.cumsum([0] + sizes)
    g_gq, g_gk, g_rel, g_mix, g_ffn, g_convw_full, g_convb, g_final = [
        tot[offs[k]:offs[k + 1]].reshape(small[k].shape) for k in range(len(small))]
    cq = conv_w.shape[2]
    g_convw = lax.dynamic_slice_in_dim(g_convw_full, (2 * cx + cy) * cq, cq, axis=2)

    stacks = {}
    for keys, send, recv, pieces, lands, tag in pending:
        pieces, lands = _reduce_wait(send, recv, pieces, lands, dh, "reduce_wait_" + tag)
        for (k, l), p, land in zip(keys, pieces, lands):
            if k not in stacks:
                stacks[k] = lax.empty((big[k].shape[0], 2) + p.shape[2:], F32)
            stacks[k] = _add_pieces(p, land, stacks[k], l)
    shard_grads = {k: gs.reshape(big[k].shape) for k, gs in zip(names, _ag_sibling([stacks[k] for k in names]))}

    grads = dict(shard_grads, a_q_gain=g_gq, a_k_gain=g_gk, rel_bias=g_rel, mix_norm=g_mix, ffn_norm=g_ffn,
                 conv_w=g_convw, conv_b=g_convb, final_norm=g_final)
    weights = dict(a_w_qkv=a_w_qkv, a_w_o=a_w_o, a_q_gain=a_q_gain, a_k_gain=a_k_gain, b_w_qkv=b_w_qkv, b_w_o=b_w_o,
                   rel_bias=rel_bias, mix_norm=mix_norm, ffn_norm=ffn_norm, w_up=w_up, conv_w=conv_w, conv_b=conv_b,
                   w_down=w_down, final_norm=final_norm)
    ms = dict(a_w_qkv=m_a_w_qkv, a_w_o=m_a_w_o, a_q_gain=m_a_q_gain, a_k_gain=m_a_k_gain, b_w_qkv=m_b_w_qkv,
              b_w_o=m_b_w_o, rel_bias=m_rel_bias, mix_norm=m_mix_norm, ffn_norm=m_ffn_norm, w_up=m_w_up,
              conv_w=m_conv_w, conv_b=m_conv_b, w_down=m_w_down, final_norm=m_final_norm)
    vs = dict(a_w_qkv=v_a_w_qkv, a_w_o=v_a_w_o, a_q_gain=v_a_q_gain, a_k_gain=v_a_k_gain, b_w_qkv=v_b_w_qkv,
              b_w_o=v_b_w_o, rel_bias=v_rel_bias, mix_norm=v_mix_norm, ffn_norm=v_ffn_norm, w_up=v_w_up,
              conv_w=v_conv_w, conv_b=v_conv_b, w_down=v_w_down, final_norm=v_final_norm)
    deltas, new_m, new_v = {}, {}, {}
    for k, w in weights.items():
        two_d = (-1, w.shape[-1])
        d, nm, nv = _adamw(w.reshape(two_d), grads[k].reshape(two_d), ms[k].reshape(two_d), vs[k].reshape(two_d))
        deltas[k], new_m[k], new_v[k] = d.reshape(w.shape), nm.reshape(w.shape), nv.reshape(w.shape)

    loss = lax.psum(loss_blk[0, 0], ("x", "y", "c"))
    keys = list(weights)
    return (loss, grad_x, *[grads[k].reshape(weights[k].shape) for k in keys], *[deltas[k] for k in keys],
            *[new_m[k] for k in keys], *[new_v[k] for k in keys])
```

```python
import functools
import math

import numpy as np
import jax
import jax.numpy as jnp
from jax import lax
from jax.experimental import pallas as pl
from jax.experimental.pallas import tpu as pltpu

F32 = jnp.float32
BF16 = jnp.bfloat16

HEAD_DIM = 128
A_HEADS = 16
A_KV_HEADS = 4
GRID_W = 64
ROPE_THETA = 10000.0
B_GROUPS = ((128, 1), (512, 4), (2048, 16))
B_HEADS_PER_GROUP = 8
REL_BUCKETS = 32
REL_MAX_DISTANCE = 1024
EPS = 1e-6
NEG_INF = -1e30
DEPTH = 4
ADAM_LR = 0.001
ADAM_B1 = 0.9
ADAM_B2 = 0.999
ADAM_EPS = 1e-08
ADAM_WD = 0.01
ADAM_STEP = 10

N_CHIPS = 4
LANES = 128
SUBLANES = 8
VMEM_LIMIT = 52 * 1024 * 1024
MESH = pl.DeviceIdType.MESH


def _pick(n, cands):
    for c in cands:
        if c <= n and n % c == 0:
            return c
    return n


def _lane_tile(n, cap):
    best = None
    for t in range(LANES, min(n, cap) + 1, LANES):
        if n % t == 0:
            best = t
    return best or n


def _cp(*sem):
    return pltpu.CompilerParams(dimension_semantics=sem if sem else None, vmem_limit_bytes=VMEM_LIMIT)


def _half_span():
    hs = {w // (2 * d) for w, d in B_GROUPS}
    assert len(hs) == 1
    return hs.pop()


def _rms_fwd(h, gain, after=()):
    S, D = h.shape
    ts = _pick(S, (512, 256, 128, 64, 32, 16))

    def body(h_ref, g_ref, *rest):
        o_ref = rest[-1]
        x = h_ref[...]
        r = lax.rsqrt(jnp.mean(x * x, axis=-1, keepdims=True) + EPS)
        o_ref[...] = (x * r * g_ref[...]).astype(o_ref.dtype)

    return pl.pallas_call(
        body, name="rms_fwd", grid=(S // ts,),
        in_specs=[pl.BlockSpec((ts, D), lambda i: (i, 0)), pl.BlockSpec((1, D), lambda i: (0, 0))]
        + [pl.BlockSpec(memory_space=pl.ANY)] * len(after),
        out_specs=pl.BlockSpec((ts, D), lambda i: (i, 0)),
        out_shape=jax.ShapeDtypeStruct((S, D), BF16), compiler_params=_cp("arbitrary"))(h, gain, *after)


def _rms_bwd(dy, h, gain, dres):
    S, D = h.shape
    ts = _pick(S, (256, 128, 64, 32, 16))

    def body(dy_ref, h_ref, g_ref, dres_ref, dh_ref, dhb_ref, dg_ref):
        @pl.when(pl.program_id(0) == 0)
        def _():
            dg_ref[...] = jnp.zeros_like(dg_ref)
        x = h_ref[...]
        dy = dy_ref[...]
        r = lax.rsqrt(jnp.mean(x * x, axis=-1, keepdims=True) + EPS)
        xn = x * r
        dg_ref[0:1, :] += jnp.sum(dy * xn, axis=0, keepdims=True)
        dxn = dy * g_ref[...]
        dx = r * (dxn - xn * jnp.mean(dxn * xn, axis=-1, keepdims=True))
        dh = dres_ref[...] + dx
        dh_ref[...] = dh
        dhb_ref[...] = dh.astype(BF16)

    row = pl.BlockSpec((ts, D), lambda i: (i, 0))
    return pl.pallas_call(
        body, name="rms_bwd", grid=(S // ts,),
        in_specs=[row, row, pl.BlockSpec((1, D), lambda i: (0, 0)), row],
        out_specs=[row, row, pl.BlockSpec((SUBLANES, D), lambda i: (0, 0))],
        out_shape=[jax.ShapeDtypeStruct((S, D), F32), jax.ShapeDtypeStruct((S, D), BF16),
                   jax.ShapeDtypeStruct((SUBLANES, D), F32)],
        compiler_params=_cp("arbitrary"))(dy, h, gain, dres)


def _final_loss(h, gain, target):
    S, D = h.shape
    ts = _pick(S, (256, 128, 64, 32, 16))

    def body(h_ref, g_ref, t_ref, loss_ref, dh_ref, dhb_ref, dg_ref):
        @pl.when(pl.program_id(0) == 0)
        def _():
            dg_ref[...] = jnp.zeros_like(dg_ref)
            loss_ref[...] = jnp.zeros_like(loss_ref)
        x = h_ref[...]
        g = g_ref[...]
        r = lax.rsqrt(jnp.mean(x * x, axis=-1, keepdims=True) + EPS)
        xn = x * r
        err = xn * g - t_ref[...]
        part = 0.5 * jnp.sum(jnp.mean(err * err, axis=-1, keepdims=True), axis=0, keepdims=True)
        loss_ref[0:1, 0:1] += part
        dy = err * (1.0 / D)
        dg_ref[0:1, :] += jnp.sum(dy * xn, axis=0, keepdims=True)
        dxn = dy * g
        dh = r * (dxn - xn * jnp.mean(dxn * xn, axis=-1, keepdims=True))
        dh_ref[...] = dh
        dhb_ref[...] = dh.astype(BF16)

    row = pl.BlockSpec((ts, D), lambda i: (i, 0))
    return pl.pallas_call(
        body, name="final_loss", grid=(S // ts,),
        in_specs=[row, pl.BlockSpec((1, D), lambda i: (0, 0)), row],
        out_specs=[pl.BlockSpec((SUBLANES, LANES), lambda i: (0, 0)), row, row,
                   pl.BlockSpec((SUBLANES, D), lambda i: (0, 0))],
        out_shape=[jax.ShapeDtypeStruct((SUBLANES, LANES), F32), jax.ShapeDtypeStruct((S, D), F32),
                   jax.ShapeDtypeStruct((S, D), BF16), jax.ShapeDtypeStruct((SUBLANES, D), F32)],
        compiler_params=_cp("arbitrary"))(h, gain, target)


_NN = (((1,), (0,)), ((), ()))
_NT = (((1,), (1,)), ((), ()))
_TN = (((0,), (0,)), ((), ()))


def _mm_nn(a, w, layer, *, blocked, out_dtype=F32, res=None, name):
    M, K = a.shape
    if blocked:
        nq = w.shape[3]
        N = N_CHIPS * nq
        tn = _lane_tile(nq, 1408)
        nps = nq // tn
        w_spec = pl.BlockSpec((None, None, K, tn), lambda i, j: (layer, j // nps, 0, j % nps))
    else:
        N = w.shape[2]
        tn = _lane_tile(N, 512)
        w_spec = pl.BlockSpec((None, K, tn), lambda i, j: (layer, 0, j))
    tm = _pick(M, (1024, 512, 256, 128, 64, 32, 16)) if K <= 3072 else _pick(M, (512, 256, 128, 64, 32, 16))

    def body(*refs):
        if res is None:
            a_ref, w_ref, o_ref = refs
            acc = lax.dot_general(a_ref[...], w_ref[...], _NN, preferred_element_type=F32)
        else:
            a_ref, w_ref, r_ref, o_ref = refs
            acc = r_ref[...] + lax.dot_general(a_ref[...], w_ref[...], _NN, preferred_element_type=F32)
        o_ref[...] = acc.astype(o_ref.dtype)

    in_specs = [pl.BlockSpec((tm, K), lambda i, j: (i, 0)), w_spec]
    args = [a, w]
    if res is not None:
        in_specs.append(pl.BlockSpec((tm, tn), lambda i, j: (i, j)))
        args.append(res)
    return pl.pallas_call(
        body, name=name, grid=(M // tm, N // tn), in_specs=in_specs,
        out_specs=pl.BlockSpec((tm, tn), lambda i, j: (i, j)),
        out_shape=jax.ShapeDtypeStruct((M, N), out_dtype),
        compiler_params=_cp("arbitrary", "arbitrary"))(*args)


def _mm_nt(a, w, layer, *, blocked, name, after=()):
    pair = isinstance(a, tuple)
    M = a[0].shape[0] if pair else a.shape[0]
    tm = _pick(M, (1024, 512, 256, 128, 64, 32, 16))
    if blocked:
        K, nq = w.shape[2], w.shape[3]
        tk = _pick(K, (1024, 512, 256, 128))
        half = N_CHIPS // 2

        def body(*refs):
            a_refs, (w_ref, o_ref, acc_ref) = refs[:-3], refs[-3:]
            p = pl.program_id(2)

            @pl.when(p == 0)
            def _():
                acc_ref[...] = jnp.zeros_like(acc_ref)
            if pair:
                @pl.when(p < half)
                def _():
                    acc_ref[...] += lax.dot_general(a_refs[0][...], w_ref[...], _NT, preferred_element_type=F32)

                @pl.when(p >= half)
                def _():
                    acc_ref[...] += lax.dot_general(a_refs[1][...], w_ref[...], _NT, preferred_element_type=F32)
            else:
                acc_ref[...] += lax.dot_general(a_refs[0][...], w_ref[...], _NT, preferred_element_type=F32)

            @pl.when(p == N_CHIPS - 1)
            def _():
                o_ref[...] = acc_ref[...]

        if pair:
            a_specs = [pl.BlockSpec((tm, nq), lambda i, j, p: (i, jnp.minimum(p, half - 1))),
                       pl.BlockSpec((tm, nq), lambda i, j, p: (i, jnp.maximum(p - half, 0)))]
            a_args = list(a)
        else:
            a_specs = [pl.BlockSpec((tm, nq), lambda i, j, p: (i, p))]
            a_args = [a]
        return pl.pallas_call(
            body, name=name, grid=(M // tm, K // tk, N_CHIPS),
            in_specs=a_specs + [pl.BlockSpec((None, None, tk, nq), lambda i, j, p: (layer, p, j, 0))],
            out_specs=pl.BlockSpec((tm, tk), lambda i, j, p: (i, j)),
            out_shape=jax.ShapeDtypeStruct((M, K), F32),
            scratch_shapes=[pltpu.VMEM((tm, tk), F32)],
            compiler_params=_cp("arbitrary", "arbitrary", "arbitrary"))(*a_args, w)
    K, N = w.shape[1], w.shape[2]
    tk = _pick(K, (1024, 512, 256, 128))

    def body(a_ref, w_ref, *rest):
        rest[-1][...] = lax.dot_general(a_ref[...], w_ref[...], _NT, preferred_element_type=F32)

    return pl.pallas_call(
        body, name=name, grid=(M // tm, K // tk),
        in_specs=[pl.BlockSpec((tm, N), lambda i, j: (i, 0)),
                  pl.BlockSpec((None, tk, N), lambda i, j: (layer, j, 0))]
        + [pl.BlockSpec(memory_space=pl.ANY)] * len(after),
        out_specs=pl.BlockSpec((tm, tk), lambda i, j: (i, j)),
        out_shape=jax.ShapeDtypeStruct((M, K), F32),
        compiler_params=_cp("arbitrary", "arbitrary"))(a, w, *after)


def _mm_tn(x, dy, *, blocked, name):
    pair = isinstance(dy, tuple)
    S, K = x.shape
    N = 2 * dy[0].shape[1] if pair else dy.shape[1]
    tk = _pick(K, (512, 256, 128))
    if blocked:
        nq = N // N_CHIPS
        tn = _lane_tile(nq, 1408)
        nps = nq // tn
        out_spec = pl.BlockSpec((None, tk, tn), lambda i, j, s: (j // nps, i, j % nps))
        out_shape = jax.ShapeDtypeStruct((N_CHIPS, K, nq), BF16)
    else:
        tn = _lane_tile(N, 1024)
        out_spec = pl.BlockSpec((tk, tn), lambda i, j, s: (i, j))
        out_shape = jax.ShapeDtypeStruct((K, N), BF16)
    nj = N // tn
    njh = nj // 2
    ns = 2 if pair else 1
    sh = S // ns

    def body(x_ref, *refs):
        o_ref, acc_ref = refs[-2], refs[-1]

        def product(dy_ref):
            part = lax.dot_general(x_ref[...], dy_ref[...], _TN, preferred_element_type=F32)
            if ns == 1:
                o_ref[...] = part.astype(o_ref.dtype)
            else:
                s = pl.program_id(2)

                @pl.when(s == 0)
                def _():
                    acc_ref[...] = part

                @pl.when(s == ns - 1)
                def _():
                    o_ref[...] = (acc_ref[...] + part).astype(o_ref.dtype)

        if pair:
            j = pl.program_id(1)
            pl.when(j < njh)(lambda: product(refs[0]))
            pl.when(j >= njh)(lambda: product(refs[1]))
        else:
            product(refs[0])

    if pair:
        assert nj % 2 == 0
        dy_specs = [pl.BlockSpec((sh, tn), lambda i, j, s: (jnp.where(j < njh, s, ns - 1), jnp.minimum(j, njh - 1))),
                    pl.BlockSpec((sh, tn), lambda i, j, s: (jnp.where(j < njh, 0, s), jnp.maximum(j - njh, 0)))]
        dy_args = list(dy)
    else:
        dy_specs = [pl.BlockSpec((sh, tn), lambda i, j, s: (s, j))]
        dy_args = [dy]
    return pl.pallas_call(
        body, name=name, grid=(K // tk, nj, ns),
        in_specs=[pl.BlockSpec((sh, tk), lambda i, j, s: (s, i))] + dy_specs,
        out_specs=out_spec, out_shape=out_shape,
        scratch_shapes=[pltpu.VMEM((tk, tn) if ns > 1 else (SUBLANES, LANES), F32)],
        compiler_params=_cp("arbitrary", "arbitrary", "arbitrary"))(x, *dy_args)


def _rope_tables(S):
    rows = S // GRID_W
    row_ids = jnp.repeat(jnp.arange(rows, dtype=F32), GRID_W)
    col_ids = jnp.tile(jnp.arange(GRID_W, dtype=F32), rows)
    quarter = HEAD_DIM // 4
    inv_freq = ROPE_THETA ** (-jnp.arange(quarter, dtype=F32) / quarter)
    ang_r = row_ids[:, None] * inv_freq[None, :]
    ang_c = col_ids[:, None] * inv_freq[None, :]
    cos = jnp.concatenate([jnp.cos(ang_r)] * 2 + [jnp.cos(ang_c)] * 2, axis=-1)
    sin = jnp.concatenate([-jnp.sin(ang_r), jnp.sin(ang_r), -jnp.sin(ang_c), jnp.sin(ang_c)], axis=-1)
    return cos, sin


def _swap_quarters(x):
    lane = lax.broadcasted_iota(jnp.int32, x.shape, 1)
    first = (lane % (HEAD_DIM // 2)) < (HEAD_DIM // 4)
    return jnp.where(first, pltpu.roll(x, HEAD_DIM - HEAD_DIM // 4, 1), pltpu.roll(x, HEAD_DIM // 4, 1))


A_SCALE = HEAD_DIM ** -0.5
A_QSCALE = A_SCALE * math.log2(math.e)


def _prep_a_fwd(qkv, cos, sin, gq, gk):
    S, W = qkv.shape
    nrm = A_HEADS + A_KV_HEADS
    ts = _pick(S, (256, 128, 64, 32, 16))

    def body(qkv_ref, cos_ref, sin_ref, gq_ref, gk_ref, o_ref):
        cos_t = cos_ref[...]
        sin_t = sin_ref[...]
        for j in range(nrm):
            sl = slice(j * HEAD_DIM, (j + 1) * HEAD_DIM)
            x = qkv_ref[:, sl]
            g = gq_ref[...] if j < A_HEADS else gk_ref[...]
            r = lax.rsqrt(jnp.mean(x * x, axis=-1, keepdims=True) + EPS)
            n = x * r * g
            y = n * cos_t + _swap_quarters(n) * sin_t
            o_ref[:, sl] = (y * A_QSCALE if j < A_HEADS else y).astype(BF16)
        o_ref[:, nrm * HEAD_DIM:] = qkv_ref[:, nrm * HEAD_DIM:].astype(BF16)

    row = lambda w: pl.BlockSpec((ts, w), lambda i: (i, 0))
    one = pl.BlockSpec((1, HEAD_DIM), lambda i: (0, 0))
    return pl.pallas_call(
        body, name="prep_a_fwd", grid=(S // ts,),
        in_specs=[row(W), row(HEAD_DIM), row(HEAD_DIM), one, one], out_specs=row(W),
        out_shape=jax.ShapeDtypeStruct((S, W), BF16), compiler_params=_cp("arbitrary"))(qkv, cos, sin, gq, gk)


def _prep_a_bwd(dq, dk, dv, qkv, cos, sin, gq, gk):
    S, W = qkv.shape
    nrm = A_HEADS + A_KV_HEADS
    nq, nk = A_HEADS * HEAD_DIM, A_KV_HEADS * HEAD_DIM
    ts = _pick(S, (256, 128, 64, 32, 16))

    def body(dq_ref, dk_ref, dv_ref, qkv_ref, cos_ref, sin_ref, gq_ref, gk_ref, o_ref, dg_ref):
        @pl.when(pl.program_id(0) == 0)
        def _():
            dg_ref[...] = jnp.zeros_like(dg_ref)
        cos_t = cos_ref[...]
        sin_t = sin_ref[...]
        for j in range(nrm):
            sl = slice(j * HEAD_DIM, (j + 1) * HEAD_DIM)
            x = qkv_ref[:, sl]
            if j < A_HEADS:
                dy, g, grow = dq_ref[:, sl], gq_ref[...], 0
            else:
                jj = j - A_HEADS
                dy, g, grow = dk_ref[:, jj * HEAD_DIM:(jj + 1) * HEAD_DIM], gk_ref[...], 1
            r = lax.rsqrt(jnp.mean(x * x, axis=-1, keepdims=True) + EPS)
            xn = x * r
            dn = dy * cos_t + _swap_quarters(dy * sin_t)
            dg_ref[grow:grow + 1, :] += jnp.sum(dn * xn, axis=0, keepdims=True)
            dxn = dn * g
            o_ref[:, sl] = (r * (dxn - xn * jnp.mean(dxn * xn, axis=-1, keepdims=True))).astype(BF16)
        o_ref[:, nrm * HEAD_DIM:] = dv_ref[...].astype(BF16)

    row = lambda w: pl.BlockSpec((ts, w), lambda i: (i, 0))
    one = pl.BlockSpec((1, HEAD_DIM), lambda i: (0, 0))
    return pl.pallas_call(
        body, name="prep_a_bwd", grid=(S // ts,),
        in_specs=[row(nq), row(nk), row(nk), row(W), row(HEAD_DIM), row(HEAD_DIM), one, one],
        out_specs=[row(W), pl.BlockSpec((SUBLANES, HEAD_DIM), lambda i: (0, 0))],
        out_shape=[jax.ShapeDtypeStruct((S, W), BF16), jax.ShapeDtypeStruct((SUBLANES, HEAD_DIM), F32)],
        compiler_params=_cp("arbitrary"))(dq, dk, dv, qkv, cos, sin, gq, gk)


def _flash_a_fwd(qkvh):
    S = qkvh.shape[0]
    grp = A_HEADS // A_KV_HEADS
    tq = _pick(S, (256, 128, 64, 32, 16))
    kc = _pick(S, (512, 256, 128))
    lanes = [slice(b * LANES, (b + 1) * LANES) for b in range(kc // LANES)]

    def body(q_ref, k_ref, v_ref, o_ref, lse_ref):
        q = q_ref[...]
        m_t = jnp.full((tq, LANES), -jnp.inf, F32)
        for c in range(S // kc):
            s = lax.dot_general(q, k_ref[c * kc:(c + 1) * kc, :], _NT, preferred_element_type=F32)
            for sl in lanes:
                m_t = jnp.maximum(m_t, s[:, sl])
        m = jnp.max(m_t, axis=-1, keepdims=True)
        l_t = jnp.zeros((tq, LANES), F32)
        acc = jnp.zeros((tq, HEAD_DIM), F32)
        for c in range(S // kc):
            rows = slice(c * kc, (c + 1) * kc)
            p = jnp.exp2(lax.dot_general(q, k_ref[rows, :], _NT, preferred_element_type=F32) - m)
            for sl in lanes:
                l_t = l_t + p[:, sl]
            acc = acc + lax.dot_general(p.astype(BF16), v_ref[rows, :], _NN, preferred_element_type=F32)
        l = jnp.sum(l_t, axis=-1, keepdims=True)
        o_ref[...] = (acc * (1.0 / l)).astype(BF16)
        lse_ref[...] = jnp.broadcast_to(m + jnp.log2(l), lse_ref.shape)

    qs = pl.BlockSpec((tq, HEAD_DIM), lambda h, i: (i, h))
    return pl.pallas_call(
        body, name="flash_a_fwd", grid=(A_HEADS, S // tq),
        in_specs=[qs,
                  pl.BlockSpec((S, HEAD_DIM), lambda h, i: (0, A_HEADS + h // grp)),
                  pl.BlockSpec((S, HEAD_DIM), lambda h, i: (0, A_HEADS + A_KV_HEADS + h // grp))],
        out_specs=[qs, qs],
        out_shape=[jax.ShapeDtypeStruct((S, A_HEADS * HEAD_DIM), BF16),
                   jax.ShapeDtypeStruct((S, A_HEADS * HEAD_DIM), F32)],
        compiler_params=_cp("arbitrary", "arbitrary"))(qkvh, qkvh, qkvh)


def _flash_a_bwd(qkvh, do, o, lse):
    S = qkvh.shape[0]
    grp = A_HEADS // A_KV_HEADS
    tq = _pick(S, (256, 128, 64, 32, 16))
    nq = S // tq

    def body(q_ref, k_ref, v_ref, do_ref, o_ref, lse_ref, dq_ref, dk_ref, dv_ref):
        g, i = pl.program_id(1), pl.program_id(2)

        @pl.when((g == 0) & (i == 0))
        def _():
            dk_ref[...] = jnp.zeros_like(dk_ref)
            dv_ref[...] = jnp.zeros_like(dv_ref)
        q = q_ref[...]
        k = k_ref[...]
        do_f = do_ref[...]
        do_b = do_f.astype(BF16)
        delta = jnp.sum(do_f * o_ref[...].astype(F32), axis=-1, keepdims=True)
        p = jnp.exp2(lax.dot_general(q, k, _NT, preferred_element_type=F32) - lse_ref[:, 0:1])
        dp = lax.dot_general(do_b, v_ref[...], _NT, preferred_element_type=F32)
        ds_b = (p * (dp - delta)).astype(BF16)
        dq_ref[...] = lax.dot_general(ds_b, k, _NN, preferred_element_type=F32) * A_SCALE
        dk_ref[...] += lax.dot_general(ds_b, q, _TN, preferred_element_type=F32)
        dv_ref[...] += lax.dot_general(p.astype(BF16), do_b, _TN, preferred_element_type=F32)

        @pl.when((g == grp - 1) & (i == nq - 1))
        def _():
            dk_ref[...] = dk_ref[...] * (A_SCALE / A_QSCALE)

    qs = pl.BlockSpec((tq, HEAD_DIM), lambda kv, g, i: (i, kv * grp + g))
    kvs = lambda off: pl.BlockSpec((S, HEAD_DIM), lambda kv, g, i: (0, off + kv))
    return pl.pallas_call(
        body, name="flash_a_bwd", grid=(A_KV_HEADS, grp, S // tq),
        in_specs=[qs, kvs(A_HEADS), kvs(A_HEADS + A_KV_HEADS), qs, qs, qs],
        out_specs=[qs, kvs(0), kvs(0)],
        out_shape=[jax.ShapeDtypeStruct((S, A_HEADS * HEAD_DIM), F32),
                   jax.ShapeDtypeStruct((S, A_KV_HEADS * HEAD_DIM), F32),
                   jax.ShapeDtypeStruct((S, A_KV_HEADS * HEAD_DIM), F32)],
        compiler_params=_cp("arbitrary", "arbitrary", "arbitrary"))(qkvh, qkvh, qkvh, do, o, lse)


def _bucket_tables(transposed):
    hs = _half_span()
    tq, kv = 2 * hs, 4 * hs
    nb = REL_BUCKETS // 2
    max_exact = nb // 2
    shape = (kv, tq) if transposed else (tq, kv)
    out = np.zeros((len(B_GROUPS), 3) + shape, np.int32)
    win = np.arange(kv) - hs
    blk = np.arange(tq)
    for g, (_, dil) in enumerate(B_GROUPS):
        for case in range(3):
            inside = ((win >= 0) | (case != 0)) & ((win < tq) | (case != 2))
            if transposed:
                rel = blk[None, :] - win[:, None]
                ok = inside[:, None]
            else:
                rel = win[None, :] - blk[:, None]
                ok = inside[None, :]
            r = rel * dil
            n = np.abs(r)
            nf = np.maximum(n, 1).astype(np.float32)
            large = max_exact + (np.log(nf / np.float32(max_exact)) / np.float32(math.log(REL_MAX_DISTANCE / max_exact))
                                 * np.float32(nb - max_exact)).astype(np.int32)
            large = np.minimum(large, nb - 1)
            bucket = np.where(r > 0, nb, 0) + np.where(n < max_exact, n, large)
            out[g, case] = np.where((np.abs(rel) <= hs) & ok, bucket, -1)
    return out


def _bias_build(rel_bias, buckets):
    G, _, tq, kv = buckets.shape
    hg = B_HEADS_PER_GROUP

    def body(rb_ref, bk_ref, o_ref):
        col = pl.program_id(0) * hg + pl.program_id(2)
        bk = bk_ref[...]
        acc = jnp.full((tq, kv), NEG_INF, F32)
        for b in range(REL_BUCKETS):
            acc = jnp.where(bk == b, rb_ref[b, col], acc)
        o_ref[...] = acc

    return pl.pallas_call(
        body, name="bias_build", grid=(G, 3, hg),
        in_specs=[pl.BlockSpec(memory_space=pltpu.SMEM),
                  pl.BlockSpec((None, None, tq, kv), lambda g, c, h: (g, c, 0, 0))],
        out_specs=pl.BlockSpec((None, None, None, tq, kv), lambda g, c, h: (g, c, h, 0, 0)),
        out_shape=jax.ShapeDtypeStruct((G, 3, hg, tq, kv), F32),
        compiler_params=_cp("arbitrary", "arbitrary", "arbitrary"))(rel_bias, buckets)


def _bias_reduce(dbias_list, buckets):
    G, _, tq, kv = buckets.shape
    hg = B_HEADS_PER_GROUP
    n = len(dbias_list)

    def body(*refs):
        bk_ref, o_ref = refs[n], refs[n + 1]
        first = (pl.program_id(0) == 0) & (pl.program_id(1) == 0) & (pl.program_id(2) == 0)

        @pl.when(first)
        def _():
            o_ref[...] = jnp.zeros_like(o_ref)
        col = pl.program_id(0) * hg + pl.program_id(2)
        db = refs[0][...]
        for r in refs[1:n]:
            db = db + r[...]
        bk = bk_ref[...]
        rows = lax.broadcasted_iota(jnp.int32, (REL_BUCKETS, LANES), 0)
        cols = lax.broadcasted_iota(jnp.int32, (REL_BUCKETS, LANES), 1)
        acc = jnp.zeros((REL_BUCKETS, LANES), F32)
        for b in range(REL_BUCKETS):
            val = jnp.sum(jnp.sum(jnp.where(bk == b, db, 0.0), axis=1, keepdims=True), axis=0, keepdims=True)
            acc = acc + jnp.where((rows == b) & (cols == col), val, 0.0)
        o_ref[...] += acc

    tile = pl.BlockSpec((None, None, None, tq, kv), lambda g, c, h: (g, c, h, 0, 0))
    return pl.pallas_call(
        body, name="bias_reduce", grid=(G, 3, hg),
        in_specs=[tile] * n + [pl.BlockSpec((None, None, tq, kv), lambda g, c, h: (g, c, 0, 0))],
        out_specs=pl.BlockSpec((REL_BUCKETS, LANES), lambda g, c, h: (0, 0)),
        out_shape=jax.ShapeDtypeStruct((REL_BUCKETS, LANES), F32),
        compiler_params=_cp("arbitrary", "arbitrary", "arbitrary"))(*dbias_list, buckets)


def _mm_nn_perm(a, w, g):
    S, K = a.shape
    nq = w.shape[3]
    dil = B_GROUPS[g][1]
    wg3 = 3 * B_HEADS_PER_GROUP * HEAD_DIM
    tn = _pick(nq, (256, 128))
    assert wg3 % tn == 0
    nps, ntile = nq // tn, wg3 // tn
    tm = _pick(S, (1024, 512, 256))
    rows = tm // dil

    def body(a_ref, w_ref, o_ref, acc_ref):
        acc = lax.dot_general(a_ref[...], w_ref[...], _NN, preferred_element_type=F32)
        if dil == 1:
            o_ref[0] = acc.astype(BF16)
        else:
            for k in range(tn // LANES):
                acc_ref[k] = acc[:, k * LANES:(k + 1) * LANES]
            for c in range(dil):
                for k in range(tn // LANES):
                    o_ref[c, :, k * LANES:(k + 1) * LANES] = acc_ref[k, pl.ds(c, rows, stride=dil), :].astype(BF16)

    def w_map(i, j):
        t = g * ntile + j
        return (0, t // nps, 0, t % nps)

    return pl.pallas_call(
        body, name="b_qkv_g%d" % g, grid=(S // tm, ntile),
        in_specs=[pl.BlockSpec((tm, K), lambda i, j: (i, 0)), pl.BlockSpec((None, None, K, tn), w_map)],
        out_specs=pl.BlockSpec((dil, rows, tn), lambda i, j: (0, i, j)),
        out_shape=jax.ShapeDtypeStruct((dil, S // dil, wg3), BF16),
        scratch_shapes=[pltpu.VMEM((tn // LANES, tm, LANES), F32)],
        compiler_params=_cp("arbitrary", "arbitrary"))(a, w)


def _window_specs(S, wg, col):
    hs = _half_span()
    tq = 2 * hs
    per = tq // hs
    return (pl.BlockSpec((tq, wg), lambda i: (i, col)),
            pl.BlockSpec((hs, wg), lambda i: (jnp.maximum(i * per - 1, 0), col)),
            pl.BlockSpec((hs, wg), lambda i: (jnp.minimum((i + 1) * per, S // hs - 1), col)))


def _window_case(i, L):
    per = L // (2 * _half_span())
    r = i % per
    return jnp.where(r == 0, 0, jnp.where(r == per - 1, 2, 1))


def _window(prev_ref, main_ref, next_ref, sl):
    return jnp.concatenate([prev_ref[:, sl], main_ref[:, sl], next_ref[:, sl]], axis=0)


def _battn_fwd(qkvp, bias, g):
    dil, L, wg3 = qkvp.shape
    S = dil * L
    hs = _half_span()
    tq, kvl = 2 * hs, 4 * hs
    hg = B_HEADS_PER_GROUP
    wg = hg * HEAD_DIM
    scale = HEAD_DIM ** -0.5
    flat = qkvp.reshape(S, wg3)

    def body(q_ref, km, kp, kn, vm, vp, vn, b_ref, o_ref, lz_ref):
        case = _window_case(pl.program_id(0), L)
        for h in range(hg):
            sl = slice(h * HEAD_DIM, (h + 1) * HEAD_DIM)
            s = lax.dot_general(q_ref[:, sl], _window(kp, km, kn, sl), _NT, preferred_element_type=F32) * scale
            s = s + b_ref[case, h]
            m = jnp.max(s, axis=-1, keepdims=True)
            p = jnp.exp(s - m)
            l = jnp.sum(p, axis=-1, keepdims=True)
            o_ref[:, sl] = lax.dot_general(p.astype(BF16), _window(vp, vm, vn, sl), _NN, preferred_element_type=F32) / l
            lz_ref[:, sl] = jnp.broadcast_to(m + jnp.log(l), (tq, HEAD_DIM))

    blk = pl.BlockSpec((tq, wg), lambda i: (i, 0))
    o, lz = pl.pallas_call(
        body, name="battn_fwd_g%d" % g, grid=(S // tq,),
        in_specs=[_window_specs(S, wg, 0)[0], *_window_specs(S, wg, 1), *_window_specs(S, wg, 2),
                  pl.BlockSpec((None, 3, hg, tq, kvl), lambda i: (g, 0, 0, 0, 0))],
        out_specs=[blk, blk], out_shape=[jax.ShapeDtypeStruct((S, wg), F32)] * 2,
        compiler_params=_cp("arbitrary"))(flat, flat, flat, flat, flat, flat, flat, bias)
    return o, lz


def _battn_bwd_dq(qkvp, bias, do, o, lz, dlz, g):
    dil, L, wg3 = qkvp.shape
    S = dil * L
    hs = _half_span()
    tq, kvl = 2 * hs, 4 * hs
    hg = B_HEADS_PER_GROUP
    wg = hg * HEAD_DIM
    scale = HEAD_DIM ** -0.5
    flat = qkvp.reshape(S, wg3)

    def body(q_ref, km, kp, kn, vm, vp, vn, b_ref, do_ref, o_ref, lz_ref, dlz_ref, dq_ref, rt_ref, db_ref):
        i = pl.program_id(0)

        @pl.when(i == 0)
        def _():
            db_ref[...] = jnp.zeros_like(db_ref)
        case = _window_case(i, L)
        for h in range(hg):
            sl = slice(h * HEAD_DIM, (h + 1) * HEAD_DIM)
            kw = _window(kp, km, kn, sl)
            do_f = do_ref[:, sl]
            s = lax.dot_general(q_ref[:, sl], kw, _NT, preferred_element_type=F32) * scale + b_ref[case, h]
            p = jnp.exp(s - lz_ref[:, sl][:, 0:1])
            dp = lax.dot_general(do_f.astype(BF16), _window(vp, vm, vn, sl), _NT, preferred_element_type=F32)
            rt = dlz_ref[:, sl][:, 0:1] - jnp.sum(do_f * o_ref[:, sl], axis=-1, keepdims=True)
            ds = p * (dp + rt)
            db_ref[case, h] += ds
            dq_ref[:, sl] = lax.dot_general((ds * scale).astype(BF16), kw, _NN, preferred_element_type=F32)
            rt_ref[:, sl] = jnp.broadcast_to(rt, (tq, HEAD_DIM))

    blk = pl.BlockSpec((tq, wg), lambda i: (i, 0))
    row = jax.ShapeDtypeStruct((S, wg), F32)
    return pl.pallas_call(
        body, name="battn_bwd_dq_g%d" % g, grid=(S // tq,),
        in_specs=[_window_specs(S, wg, 0)[0], *_window_specs(S, wg, 1), *_window_specs(S, wg, 2),
                  pl.BlockSpec((None, 3, hg, tq, kvl), lambda i: (g, 0, 0, 0, 0)), blk, blk, blk, blk],
        out_specs=[blk, blk, pl.BlockSpec((3, hg, tq, kvl), lambda i: (0, 0, 0, 0))],
        out_shape=[row, row, jax.ShapeDtypeStruct((3, hg, tq, kvl), F32)],
        compiler_params=_cp("arbitrary"))(flat, flat, flat, flat, flat, flat, flat, bias, do, o, lz, dlz)


def _battn_bwd_dkv(qkvp, bias_t, do, lz, rt, g):
    dil, L, wg3 = qkvp.shape
    S = dil * L
    hs = _half_span()
    tq, kvl = 2 * hs, 4 * hs
    hg = B_HEADS_PER_GROUP
    wg = hg * HEAD_DIM
    scale = HEAD_DIM ** -0.5
    flat = qkvp.reshape(S, wg3)

    def body(k_ref, v_ref, qm, qp, qn, dom, dop, don, lzm, lzp, lzn, rtm, rtp, rtn, b_ref, dk_ref, dv_ref):
        case = _window_case(pl.program_id(0), L)
        for h in range(hg):
            sl = slice(h * HEAD_DIM, (h + 1) * HEAD_DIM)
            qw = _window(qp, qm, qn, sl)
            dow = _window(dop, dom, don, sl).astype(BF16)
            s = lax.dot_general(qw, k_ref[:, sl], _NT, preferred_element_type=F32) * scale + b_ref[case, h]
            p = jnp.exp(s - _window(lzp, lzm, lzn, sl)[:, 0:1])
            dp = lax.dot_general(dow, v_ref[:, sl], _NT, preferred_element_type=F32)
            ds_b = (p * (dp + _window(rtp, rtm, rtn, sl)[:, 0:1]) * scale).astype(BF16)
            dk_ref[:, sl] = lax.dot_general(ds_b, qw, _TN, preferred_element_type=F32)
            dv_ref[:, sl] = lax.dot_general(p.astype(BF16), dow, _TN, preferred_element_type=F32)

    blk = pl.BlockSpec((tq, wg), lambda i: (i, 0))
    row = jax.ShapeDtypeStruct((S, wg), F32)
    return pl.pallas_call(
        body, name="battn_bwd_dkv_g%d" % g, grid=(S // tq,),
        in_specs=[_window_specs(S, wg, 1)[0], _window_specs(S, wg, 2)[0], *_window_specs(S, wg, 0),
                  *_window_specs(S, wg, 0), *_window_specs(S, wg, 0), *_window_specs(S, wg, 0),
                  pl.BlockSpec((None, 3, hg, kvl, tq), lambda i: (g, 0, 0, 0, 0))],
        out_specs=[blk, blk], out_shape=[row, row],
        compiler_params=_cp("arbitrary"))(flat, flat, flat, flat, flat, do, do, do, lz, lz, lz, rt, rt, rt, bias_t)


def _group_weights(lz_refs, h):
    z = [r[h] for r in lz_refs]
    mx = functools.reduce(jnp.maximum, z)
    e = [jnp.exp(v - mx) for v in z]
    inv = 1.0 / functools.reduce(lambda a, b: a + b, e)
    return [v * inv for v in e]


def _to_token_order(src_ref, dst_ref, dil):
    rows = src_ref.shape[1]
    for k in range(dst_ref.shape[0]):
        sl = slice(k * LANES, (k + 1) * LANES)
        if dil == 1:
            dst_ref[k] = src_ref[0, :, sl]
        else:
            for c in range(dil):
                dst_ref[k, pl.ds(c, rows, stride=dil), :] = src_ref[c, :, sl]


def _to_subsequence_order(src_ref, dst_ref, dil):
    rows = dst_ref.shape[1]
    for k in range(src_ref.shape[0]):
        sl = slice(k * LANES, (k + 1) * LANES)
        if dil == 1:
            dst_ref[0, :, sl] = src_ref[k]
        else:
            for c in range(dil):
                dst_ref[c, :, sl] = src_ref[k, pl.ds(c, rows, stride=dil), :]


def _sub_view(a, dil):
    S, w = a.shape
    return a.reshape(dil, S // dil, w)


def _sub_spec(dil, ts, w):
    return pl.BlockSpec((dil, ts // dil, w), lambda i: (0, i, 0))


def _combine_fwd(os_, lzs):
    G = len(os_)
    S, Wg = os_[0].shape
    hg = B_HEADS_PER_GROUP
    dils = [d for _, d in B_GROUPS]
    ts = _pick(S, (256, 128))

    def body(*refs):
        o_in, lz_in, y_ref = refs[:G], refs[G:2 * G], refs[2 * G]
        o_nat, lz_nat = refs[2 * G + 1:3 * G + 1], refs[3 * G + 1:4 * G + 1]
        for g in range(G):
            _to_token_order(o_in[g], o_nat[g], dils[g])
            _to_token_order(lz_in[g], lz_nat[g], dils[g])
        for h in range(hg):
            w = _group_weights(lz_nat, h)
            for g in range(G):
                y_ref[:, (g * hg + h) * HEAD_DIM:(g * hg + h + 1) * HEAD_DIM] = (w[g] * o_nat[g][h]).astype(BF16)

    specs = [_sub_spec(d, ts, Wg) for d in dils]
    return pl.pallas_call(
        body, name="combine_fwd", grid=(S // ts,), in_specs=specs + specs,
        out_specs=pl.BlockSpec((ts, G * Wg), lambda i: (i, 0)),
        out_shape=jax.ShapeDtypeStruct((S, G * Wg), BF16),
        scratch_shapes=[pltpu.VMEM((hg, ts, HEAD_DIM), F32)] * (2 * G),
        compiler_params=_cp("arbitrary"))(*[_sub_view(a, d) for a, d in zip(os_, dils)],
                                          *[_sub_view(a, d) for a, d in zip(lzs, dils)])


def _combine_bwd(dy, os_, lzs):
    G = len(os_)
    S, Wg = os_[0].shape
    hg = B_HEADS_PER_GROUP
    dils = [d for _, d in B_GROUPS]
    ts = _pick(S, (128,))

    def body(*refs):
        dy_ref, o_in, lz_in = refs[0], refs[1:1 + G], refs[1 + G:1 + 2 * G]
        do_out, dlz_out = refs[1 + 2 * G:1 + 3 * G], refs[1 + 3 * G:1 + 4 * G]
        scr = refs[1 + 4 * G:]
        o_nat, lz_nat, do_nat, dlz_nat = scr[:G], scr[G:2 * G], scr[2 * G:3 * G], scr[3 * G:4 * G]
        for g in range(G):
            _to_token_order(o_in[g], o_nat[g], dils[g])
            _to_token_order(lz_in[g], lz_nat[g], dils[g])
        for h in range(hg):
            w = _group_weights(lz_nat, h)
            dw = []
            for g in range(G):
                dyg = dy_ref[:, (g * hg + h) * HEAD_DIM:(g * hg + h + 1) * HEAD_DIM]
                dw.append(jnp.sum(dyg * o_nat[g][h], axis=-1, keepdims=True))
                do_nat[g][h] = w[g] * dyg
            tot = functools.reduce(lambda a, b: a + b, [w[g] * dw[g] for g in range(G)])
            for g in range(G):
                dlz_nat[g][h] = w[g] * (dw[g] - tot)
        for g in range(G):
            _to_subsequence_order(do_nat[g], do_out[g], dils[g])
            _to_subsequence_order(dlz_nat[g], dlz_out[g], dils[g])

    specs = [_sub_spec(d, ts, Wg) for d in dils]
    outs = pl.pallas_call(
        body, name="combine_bwd", grid=(S // ts,),
        in_specs=[pl.BlockSpec((ts, G * Wg), lambda i: (i, 0))] + specs + specs,
        out_specs=specs + specs,
        out_shape=[jax.ShapeDtypeStruct((d, S // d, Wg), F32) for d in dils] * 2,
        scratch_shapes=[pltpu.VMEM((hg, ts, HEAD_DIM), F32)] * (4 * G),
        compiler_params=_cp("arbitrary"))(dy, *[_sub_view(a, d) for a, d in zip(os_, dils)],
                                          *[_sub_view(a, d) for a, d in zip(lzs, dils)])
    flat = [a.reshape(S, Wg) for a in outs]
    return flat[:G], flat[G:]


def _concat_cast(parts, dils):
    S = parts[0].shape[0]
    widths = [p.shape[1] for p in parts]
    n = len(parts)
    ts = _pick(S, (256, 128))

    def body(*refs):
        o_ref, nat = refs[n], refs[n + 1]
        off = 0
        for r, w, d in zip(refs, widths, dils):
            _to_token_order(r, nat, d)
            for k in range(w // LANES):
                o_ref[:, off + k * LANES:off + (k + 1) * LANES] = nat[k].astype(BF16)
            off += w

    assert len(set(widths)) == 1
    return pl.pallas_call(
        body, name="concat_cast", grid=(S // ts,),
        in_specs=[_sub_spec(d, ts, w) for w, d in zip(widths, dils)],
        out_specs=pl.BlockSpec((ts, sum(widths)), lambda i: (i, 0)),
        out_shape=jax.ShapeDtypeStruct((S, sum(widths)), BF16),
        scratch_shapes=[pltpu.VMEM((widths[0] // LANES, ts, LANES), F32)],
        compiler_params=_cp("arbitrary"))(*[_sub_view(p, d) for p, d in zip(parts, dils)])


def _ffn_specs(S, dff, cq, ts, tc, layer, order):
    nfc = dff // tc
    nps = cq // tc
    hb = ts // SUBLANES
    nrow8 = S // SUBLANES

    def u_main(half):
        return pl.BlockSpec((ts, tc), lambda *g: (order(*g)[0], order(*g)[1] % nfc + half * nfc))

    def u_prev(half):
        return pl.BlockSpec((SUBLANES, tc), lambda *g: (jnp.maximum(order(*g)[0] * hb - 1, 0),
                                                         order(*g)[1] % nfc + half * nfc))

    def u_next(half):
        return pl.BlockSpec((SUBLANES, tc), lambda *g: (jnp.minimum((order(*g)[0] + 1) * hb, nrow8 - 1),
                                                         order(*g)[1] % nfc + half * nfc))

    def cw(half):
        def im(*g):
            jj = order(*g)[1] % nfc + half * nfc
            return (layer, jj // nps, 0, jj % nps)
        return pl.BlockSpec((None, None, 3, tc), im)

    def cb(half):
        return pl.BlockSpec((None, 1, tc), lambda *g: (layer, 0, order(*g)[1] % nfc + half * nfc))

    return nfc, u_main, u_prev, u_next, cw, cb


def _ffn_act_fwd(u, cw_full, cb3, layer):
    S, two_dff = u.shape
    dff = two_dff // 2
    cq = cw_full.shape[3]
    ts = _pick(S, (1024, 512, 256, 128, 64, 32, 16))
    tc = _pick(cq, (256, 128))
    order = lambda j, i: (i, j)
    nfc, u_main, u_prev, u_next, cw, cb = _ffn_specs(S, dff, cq, ts, tc, layer, order)
    nrow = S // ts

    def body(ug, ugp, ugn, uv, uvp, uvn, wg, wv, bg, bv, a_ref):
        i = pl.program_id(1)
        row = lax.broadcasted_iota(jnp.int32, (ts, tc), 0)

        def conv(x_ref, p_ref, n_ref, w_ref, b_ref):
            x = x_ref[...]
            prev = jnp.where(i > 0, p_ref[SUBLANES - 1:SUBLANES, :], 0.0)
            nxt = jnp.where(i < nrow - 1, n_ref[0:1, :], 0.0)
            xm = jnp.where(row == 0, prev, pltpu.roll(x, 1, 0))
            xp = jnp.where(row == ts - 1, nxt, pltpu.roll(x, ts - 1, 0))
            return w_ref[0:1, :] * xm + w_ref[1:2, :] * x + w_ref[2:3, :] * xp + b_ref[...]

        gc = conv(ug, ugp, ugn, wg, bg)
        vc = conv(uv, uvp, uvn, wv, bv)
        a_ref[...] = (gc * (1.0 / (1.0 + jnp.exp(-gc))) * vc).astype(BF16)

    return pl.pallas_call(
        body, name="ffn_act_fwd", grid=(nfc, nrow),
        in_specs=[u_main(0), u_prev(0), u_next(0), u_main(1), u_prev(1), u_next(1), cw(0), cw(1), cb(0), cb(1)],
        out_specs=pl.BlockSpec((ts, tc), lambda j, i: (i, j)),
        out_shape=jax.ShapeDtypeStruct((S, dff), BF16),
        compiler_params=_cp("arbitrary", "arbitrary"))(u, u, u, u, u, u, cw_full, cw_full, cb3, cb3)


def _ffn_act_bwd(u, da, cw_full, cb3, layer):
    S, two_dff = u.shape
    dff = two_dff // 2
    cq = cw_full.shape[3]
    ts = _pick(S, (1024, 512, 256, 128, 64, 32, 16))
    tc = _pick(cq, (256, 128))
    order = lambda j, i: (i, j)
    nfc, u_main, u_prev, u_next, cw, cb = _ffn_specs(S, dff, cq, ts, tc, layer, order)
    nrow = S // ts
    hb = ts // SUBLANES
    te = ts + 2 * SUBLANES
    da_main = pl.BlockSpec((ts, tc), lambda j, i: (i, j))
    da_prev = pl.BlockSpec((SUBLANES, tc), lambda j, i: (jnp.maximum(i * hb - 1, 0), j))
    da_next = pl.BlockSpec((SUBLANES, tc), lambda j, i: (jnp.minimum((i + 1) * hb, S // SUBLANES - 1), j))
    main = slice(SUBLANES, SUBLANES + ts)

    def body(ug, ugp, ugn, uv, uvp, uvn, dam, dap, dan, wg, wv, bg, bv, dug_ref, duv_ref, accg_ref, accv_ref):
        i = pl.program_id(1)

        @pl.when(i == 0)
        def _():
            accg_ref[...] = jnp.zeros_like(accg_ref)
            accv_ref[...] = jnp.zeros_like(accv_ref)

        def ext(m, p, n):
            return jnp.concatenate([jnp.where(i > 0, p[...], 0.0), m[...], jnp.where(i < nrow - 1, n[...], 0.0)], axis=0)

        def shift(x):
            return pltpu.roll(x, 1, 0), pltpu.roll(x, te - 1, 0)

        xg, xv, dae = ext(ug, ugp, ugn), ext(uv, uvp, uvn), ext(dam, dap, dan)
        xgm, xgp = shift(xg)
        xvm, xvp = shift(xv)
        gc = wg[0:1, :] * xgm + wg[1:2, :] * xg + wg[2:3, :] * xgp + bg[...]
        vc = wv[0:1, :] * xvm + wv[1:2, :] * xv + wv[2:3, :] * xvp + bv[...]
        sig = 1.0 / (1.0 + jnp.exp(-gc))
        silu = gc * sig
        dcg = dae * vc * (sig * (1.0 + gc * (1.0 - sig)))
        dcv = dae * silu

        def finish(dc, x, xm, xp, w_ref, du_ref, acc_ref):
            dm, dp = shift(dc)
            du = w_ref[0:1, :] * dp + w_ref[1:2, :] * dc + w_ref[2:3, :] * dm
            du_ref[...] = du[main, :].astype(BF16)
            dcm = dc[main, :]
            acc_ref[0:1, :] += jnp.sum(dcm * xm[main, :], axis=0, keepdims=True)
            acc_ref[1:2, :] += jnp.sum(dcm * x[main, :], axis=0, keepdims=True)
            acc_ref[2:3, :] += jnp.sum(dcm * xp[main, :], axis=0, keepdims=True)
            acc_ref[3:4, :] += jnp.sum(dcm, axis=0, keepdims=True)

        finish(dcg, xg, xgm, xgp, wg, dug_ref, accg_ref)
        finish(dcv, xv, xvm, xvp, wv, duv_ref, accv_ref)

    blk = pl.BlockSpec((ts, tc), lambda j, i: (i, j))
    acc = pl.BlockSpec((SUBLANES, tc), lambda j, i: (0, j))
    dug, duv, accg, accv = pl.pallas_call(
        body, name="ffn_act_bwd", grid=(nfc, nrow),
        in_specs=[u_main(0), u_prev(0), u_next(0), u_main(1), u_prev(1), u_next(1), da_main, da_prev, da_next,
                  cw(0), cw(1), cb(0), cb(1)],
        out_specs=[blk, blk, acc, acc],
        out_shape=[jax.ShapeDtypeStruct((S, dff), BF16)] * 2 + [jax.ShapeDtypeStruct((SUBLANES, dff), F32)] * 2,
        compiler_params=_cp("arbitrary", "arbitrary"))(u, u, u, u, u, u, da, da, da, cw_full, cw_full, cb3, cb3)
    return (dug, duv), jnp.concatenate([accg, accv], axis=1)


def _my_chip():
    return 2 * lax.axis_index("x") + lax.axis_index("y")


def _into_full(w, layer, dtype):
    L, a, b = w.shape
    tr = _pick(a, (512, 256, 128, 64, 32, 16, 8))

    def body(w_ref, o_ref):
        o_ref[...] = w_ref[...].astype(dtype)

    return pl.pallas_call(
        body, name="into_full", grid=(a // tr,),
        in_specs=[pl.BlockSpec((None, tr, b), lambda i: (layer, i, 0))],
        out_specs=pl.BlockSpec((None, None, tr, b), lambda i: (0, _my_chip(), i, 0)),
        out_shape=jax.ShapeDtypeStruct((1, N_CHIPS, a, b), dtype),
        compiler_params=_cp("arbitrary"))(w)


def _adam_math(w, g, m, v):
    m = ADAM_B1 * m + (1.0 - ADAM_B1) * g
    v = ADAM_B2 * v + (1.0 - ADAM_B2) * (g * g)
    m_hat = m / (1.0 - ADAM_B1 ** ADAM_STEP)
    v_hat = v / (1.0 - ADAM_B2 ** ADAM_STEP)
    delta = -ADAM_LR * (m_hat / (jnp.sqrt(v_hat) + ADAM_EPS) + ADAM_WD * w)
    return delta, m, v


def _adamw(w, g, m, v):
    R, C = w.shape
    tr = _pick(R, (128, 64, 32, 16, 8)) if R % SUBLANES == 0 and C % LANES == 0 else R

    def body(w_ref, g_ref, m_ref, v_ref, d_ref, nm_ref, nv_ref):
        d, nm, nv = _adam_math(w_ref[...], g_ref[...], m_ref[...], v_ref[...])
        d_ref[...] = d
        nm_ref[...] = nm
        nv_ref[...] = nv

    spec = pl.BlockSpec((tr, C), lambda i: (i, 0))
    return pl.pallas_call(
        body, name="adamw", grid=(R // tr,), in_specs=[spec] * 4, out_specs=[spec] * 3,
        out_shape=[jax.ShapeDtypeStruct((R, C), F32)] * 3, compiler_params=_cp("arbitrary"))(w, g, m, v)


ANY = pl.BlockSpec(memory_space=pl.ANY)


def _position():
    x, y, c = lax.axis_index("x"), lax.axis_index("y"), lax.axis_index("c")
    chips = [(1 - x, y), (x, 1 - y), (1 - x, 1 - y)]
    return x, y, c, chips


HBM = pl.BlockSpec(memory_space=pltpu.HBM)
SEM = pl.BlockSpec(memory_space=pltpu.SEMAPHORE)
EFFECT = pltpu.SideEffectType.DATAFLOW_SIDE_EFFECTING


def _in_hbm(a):
    return pltpu.with_memory_space_constraint(a, pltpu.HBM)


def _shard_half(buf, shape, chip, half):
    _, _, a, b = shape
    p = 2 * chip[0] + chip[1]
    if a % (4 * SUBLANES) == 0:
        return buf.at[0, p, pl.ds(half * (a // 2), a // 2)]
    return buf.at[0, p, :, pl.ds(half * (b // 2), b // 2)]


def _gather_copy(buf, shape, chip, half, to, send, recv, k):
    part = _shard_half(buf, shape, chip, half)
    return pltpu.make_async_remote_copy(src_ref=part, dst_ref=part, send_sem=send.at[k], recv_sem=recv.at[k],
                                        device_id=to, device_id_type=MESH)


def _gather_start(fulls, name):
    n = len(fulls)

    def body(*refs):
        send, recv, buf, token = refs[n], refs[n + 1], refs[n + 2:2 * n + 2], refs[2 * n + 2]
        x, y, c, chips = _position()
        for t in range(n):
            for j in range(3):
                _gather_copy(buf[t], fulls[t].shape, (x, y), c, (*chips[j], c), send, recv, 3 * t + j).start()
        token[...] = jnp.zeros_like(token)

    outs = pl.pallas_call(
        body, name=name, in_specs=[HBM] * n,
        out_specs=[SEM, SEM] + [HBM] * n + [pl.BlockSpec(memory_space=pltpu.VMEM)],
        out_shape=[pltpu.SemaphoreType.DMA((3 * n,)), pltpu.SemaphoreType.DMA((3 * n,))]
        + [pltpu.HBM(f.shape, f.dtype) for f in fulls] + [jax.ShapeDtypeStruct((SUBLANES, LANES), F32)],
        input_output_aliases={t: 2 + t for t in range(n)},
        compiler_params=pltpu.CompilerParams(has_side_effects=EFFECT))(*[_in_hbm(f) for f in fulls])
    return outs[0], outs[1], list(outs[2:2 + n]), outs[2 + n]


def _gather_wait(send, recv, fulls, after, name):
    n = len(fulls)

    def body(*refs):
        buf, send_ref, recv_ref = refs[:n], refs[n], refs[n + 1]
        x, y, c, chips = _position()
        for t in range(n):
            for j in range(3):
                _gather_copy(buf[t], fulls[t].shape, (x, y), c, (*chips[j], c), send_ref, recv_ref, 3 * t + j).wait_send()
                _gather_copy(buf[t], fulls[t].shape, chips[j], c, (*chips[j], c), send_ref, recv_ref, 3 * t + j).wait_recv()

    outs = pl.pallas_call(
        body, name=name, in_specs=[HBM] * n + [SEM, SEM, ANY], out_specs=[HBM] * n,
        out_shape=[pltpu.HBM(f.shape, f.dtype) for f in fulls],
        input_output_aliases={t: t for t in range(n)},
        compiler_params=pltpu.CompilerParams(has_side_effects=EFFECT))(*fulls, send, recv, after)
    return list(outs)


def _gather_forward(fulls):
    n = len(fulls)

    def body(*refs):
        buf, send, recv = refs[n:2 * n], refs[2 * n], refs[2 * n + 1]
        x, y, c, chips = _position()
        sib = (x, y, 1 - c)
        cps = [_gather_copy(buf[t], fulls[t].shape, chips[j], c, sib, send, recv, 3 * t + j)
               for t in range(n) for j in range(3)]
        for cp in cps:
            cp.start()
        for t in range(n):
            for j in range(3):
                _gather_copy(buf[t], fulls[t].shape, chips[j], 1 - c, sib, send, recv, 3 * t + j).wait_recv()
        for cp in cps:
            cp.wait_send()

    return pl.pallas_call(
        body, name="gather_forward", in_specs=[ANY] * n, out_specs=[ANY] * n,
        out_shape=[jax.ShapeDtypeStruct(f.shape, f.dtype) for f in fulls],
        input_output_aliases={t: t for t in range(n)},
        scratch_shapes=[pltpu.SemaphoreType.DMA((3 * n,)), pltpu.SemaphoreType.DMA((3 * n,))])(*fulls)


def _allreduce_small(part):
    M, C = part.shape
    n_dev = 2 * N_CHIPS

    def body(x_ref, sum_ref, all_ref, send, recv, local):
        x, y, c, chips = _position()
        me, sib = (x, y, c), (x, y, 1 - c)

        def rows(px, py, pc):
            return all_ref.at[pl.ds((4 * px + 2 * py + pc) * M, M), :]

        def copy(k, block, to, src=None):
            return pltpu.make_async_remote_copy(
                src_ref=rows(*block) if src is None else src, dst_ref=rows(*block),
                send_sem=send.at[k], recv_sem=recv.at[k], device_id=to, device_id_type=MESH)

        mine = pltpu.make_async_copy(x_ref, rows(*me), local)
        mine.start()
        first = [copy(0, me, sib, src=x_ref)] + [copy(1 + j, me, (*chip, c), src=x_ref) for j, chip in enumerate(chips)]
        for cp in first:
            cp.start()
        passed = [copy(4 + j, (*chip, c), sib) for j, chip in enumerate(chips)]
        for j, chip in enumerate(chips):
            copy(1 + j, (*chip, c), me).wait_recv()
            passed[j].start()
        copy(0, sib, me).wait_recv()
        for j, chip in enumerate(chips):
            copy(4 + j, (*chip, 1 - c), me).wait_recv()
        for cp in first + passed:
            cp.wait_send()
        mine.wait()
        acc = all_ref[0:M, :]
        for d in range(1, n_dev):
            acc = acc + all_ref[d * M:(d + 1) * M, :]
        sum_ref[...] = acc

    vm = pl.BlockSpec(memory_space=pltpu.VMEM)
    return pl.pallas_call(
        body, name="allreduce_small", in_specs=[vm], out_specs=[vm],
        out_shape=[jax.ShapeDtypeStruct((M, C), F32)],
        scratch_shapes=[pltpu.VMEM((n_dev * M, C), F32), pltpu.SemaphoreType.DMA((7,)),
                        pltpu.SemaphoreType.DMA((7,)), pltpu.SemaphoreType.DMA],
        compiler_params=pltpu.CompilerParams(vmem_limit_bytes=VMEM_LIMIT))(part)[0]


N_PEERS = 2 * N_CHIPS - 1


def _peers():
    x, y, c, chips = _position()
    return [(x, y, 1 - c)] + [(*ch, c) for ch in chips] + [(*ch, 1 - c) for ch in chips]


def _reduce_copy(src, dst, peers, send, recv, t, r):
    px, py, pc = peers[r]
    return pltpu.make_async_remote_copy(
        src_ref=src.at[2 * px + py, pc], dst_ref=dst.at[r], send_sem=send.at[N_PEERS * t + r],
        recv_sem=recv.at[N_PEERS * t + r], device_id=peers[r], device_id_type=MESH)


def _reduce_start(grads, name):
    n = len(grads)
    lands = [lax.empty((N_PEERS,) + g.shape[2:], BF16) for g in grads]

    def body(*refs):
        send, recv = refs[2 * n], refs[2 * n + 1]
        src, dst, token = refs[2 * n + 2:3 * n + 2], refs[3 * n + 2:4 * n + 2], refs[4 * n + 2]
        peers = _peers()
        for t in range(n):
            for r in range(N_PEERS):
                _reduce_copy(src[t], dst[t], peers, send, recv, t, r).start()
        token[...] = jnp.zeros_like(token)

    outs = pl.pallas_call(
        body, name=name, in_specs=[HBM] * (2 * n),
        out_specs=[SEM, SEM] + [HBM] * (2 * n) + [pl.BlockSpec(memory_space=pltpu.VMEM)],
        out_shape=[pltpu.SemaphoreType.DMA((N_PEERS * n,)), pltpu.SemaphoreType.DMA((N_PEERS * n,))]
        + [pltpu.HBM(a.shape, a.dtype) for a in grads + lands] + [jax.ShapeDtypeStruct((SUBLANES, LANES), F32)],
        input_output_aliases={t: 2 + t for t in range(2 * n)},
        compiler_params=pltpu.CompilerParams(has_side_effects=EFFECT))(*[_in_hbm(a) for a in grads + lands])
    return outs[0], outs[1], list(outs[2:2 + n]), list(outs[2 + n:2 + 2 * n]), outs[2 + 2 * n]


def _reduce_wait(send, recv, grads, lands, after, name):
    n = len(grads)

    def body(*refs):
        src, dst, send_ref, recv_ref = refs[:n], refs[n:2 * n], refs[2 * n], refs[2 * n + 1]
        peers = _peers()
        for t in range(n):
            for r in range(N_PEERS):
                cp = _reduce_copy(src[t], dst[t], peers, send_ref, recv_ref, t, r)
                cp.wait_send()
                cp.wait_recv()

    outs = pl.pallas_call(
        body, name=name, in_specs=[HBM] * (2 * n) + [SEM, SEM] + [ANY] * len(after), out_specs=[HBM] * (2 * n),
        out_shape=[pltpu.HBM(a.shape, a.dtype) for a in grads + lands],
        input_output_aliases={t: t for t in range(2 * n)},
        compiler_params=pltpu.CompilerParams(has_side_effects=EFFECT))(*grads, *lands, send, recv, *after)
    return list(outs[:n]), list(outs[n:])


def _add_pieces(grad, land, stack, layer):
    _, _, R, C = grad.shape
    tr = _pick(R, (256, 128, 64, 32, 16))

    def body(g_ref, r_ref, stack_ref, o_ref):
        acc = g_ref[...].astype(F32)
        for r in range(N_PEERS):
            acc = acc + r_ref[r].astype(F32)
        o_ref[...] = acc

    return pl.pallas_call(
        body, name="add_pieces", grid=(R // tr,),
        in_specs=[pl.BlockSpec((None, None, tr, C), lambda i: (_my_chip(), lax.axis_index("c"), i, 0)),
                  pl.BlockSpec((N_PEERS, tr, C), lambda i: (0, i, 0)),
                  ANY],
        out_specs=pl.BlockSpec((None, None, tr, C), lambda i: (layer, lax.axis_index("c"), i, 0)),
        out_shape=jax.ShapeDtypeStruct(stack.shape, F32), input_output_aliases={2: 0},
        compiler_params=_cp("arbitrary"))(grad, land, stack)


def _ag_sibling(stacks):
    n = len(stacks)
    offs = np.cumsum([0] + [s.shape[0] for s in stacks])

    def body(*refs):
        buf, send, recv = refs[n:2 * n], refs[2 * n], refs[2 * n + 1]
        x, y, c, _ = _position()

        def copy(t, l, half):
            part = buf[t].at[l, half]
            return pltpu.make_async_remote_copy(
                src_ref=part, dst_ref=part, send_sem=send.at[int(offs[t]) + l], recv_sem=recv.at[int(offs[t]) + l],
                device_id=(x, y, 1 - c), device_id_type=MESH)

        cps = [copy(t, l, c) for t in range(n) for l in range(stacks[t].shape[0])]
        for cp in cps:
            cp.start()
        for t in range(n):
            for l in range(stacks[t].shape[0]):
                copy(t, l, 1 - c).wait_recv()
        for cp in cps:
            cp.wait_send()

    return pl.pallas_call(
        body, name="ag_sibling", in_specs=[ANY] * n, out_specs=[ANY] * n,
        out_shape=[jax.ShapeDtypeStruct(s.shape, F32) for s in stacks],
        input_output_aliases={t: t for t in range(n)},
        scratch_shapes=[pltpu.SemaphoreType.DMA((int(offs[-1]),)), pltpu.SemaphoreType.DMA((int(offs[-1]),))])(*stacks)


def _split8(dw, blocked):
    if blocked:
        p, k, nq = dw.shape
        return dw.reshape(p, 2, k // 2, nq)
    k, n = dw.shape
    return dw.reshape(N_CHIPS, 2, k // (2 * N_CHIPS), n)


def kernel(x, a_w_qkv, a_w_o, a_q_gain, a_k_gain, b_w_qkv, b_w_o, rel_bias, mix_norm, ffn_norm, w_up, conv_w, conv_b, w_down, final_norm, loss_target, m_a_w_qkv, m_a_w_o, m_a_q_gain, m_a_k_gain, m_b_w_qkv, m_b_w_o, m_rel_bias, m_mix_norm, m_ffn_norm, m_w_up, m_conv_w, m_conv_b, m_w_down, m_final_norm, v_a_w_qkv, v_a_w_o, v_a_q_gain, v_a_k_gain, v_b_w_qkv, v_b_w_o, v_rel_bias, v_mix_norm, v_ffn_norm, v_w_up, v_conv_w, v_conv_b, v_w_down, v_final_norm):
    S, D = x.shape[1], x.shape[2]
    h = x.reshape(S, D)
    target = loss_target.reshape(S, D)
    hg = B_HEADS_PER_GROUP
    G = len(B_GROUPS)
    n_a, n_b = a_w_qkv.shape[0], b_w_qkv.shape[0]
    depth = w_up.shape[0]
    cx, cy = lax.axis_index("x"), lax.axis_index("y")

    big = dict(a_w_qkv=a_w_qkv, a_w_o=a_w_o, b_w_qkv=b_w_qkv, b_w_o=b_w_o, w_up=w_up, w_down=w_down)
    blocked = dict(a_w_qkv=True, a_w_o=False, b_w_qkv=True, b_w_o=False, w_up=True, w_down=False)
    names = list(big)
    srcs = dict(big, conv_w=conv_w)
    started = []
    for i in range(depth):
        mix = [("a_w_qkv", i // 2), ("a_w_o", i // 2)] if i % 2 == 0 else [("b_w_qkv", i // 2), ("b_w_o", i // 2)]
        rest = [("w_up", i), ("conv_w", i), ("w_down", i)]
        stages = [mix[:1], mix[1:] + rest] if i == 0 else [mix + rest]
        started.append([])
        for s, keys in enumerate(stages):
            bufs = [_into_full(srcs[k], l, F32 if k == "conv_w" else BF16) for k, l in keys]
            started[i].append((keys,) + _gather_start(bufs, "gather_start_%d_%d" % (i, s)))
    cb3 = conv_b.reshape(depth, 1, conv_b.shape[1])

    cos, sin = _rope_tables(S)
    buckets = jnp.asarray(_bucket_tables(False))
    bias = _bias_build(rel_bias, buckets)
    bias_t = _bias_build(rel_bias, jnp.asarray(_bucket_tables(True)))

    saved = []
    for i in range(depth):
        j = i // 2
        wl = {}

        def arrive(s, after):
            keys, send, recv, bufs, _ = started[i][s]
            bufs = _gather_forward(_gather_wait(send, recv, bufs, after, "gather_wait_%d_%d" % (i, s)))
            for (k, _), buf in zip(keys, bufs):
                _, _, a, b = buf.shape
                wl[k] = buf if k == "conv_w" or blocked[k] else buf.reshape(1, N_CHIPS * a, b)

        arrive(0, h)
        sv = dict(h0=h, w=wl)
        hn = _rms_fwd(h, mix_norm[i:i + 1], after=[st[4] for layer in started for st in layer] if i == 0 else ())
        sv["hn"] = hn
        if i % 2 == 0:
            qkv = _mm_nn(hn, wl["a_w_qkv"], 0, blocked=True, name="a_qkv")
            qkvh = _prep_a_fwd(qkv, cos, sin, a_q_gain[j:j + 1], a_k_gain[j:j + 1])
            o, lse = _flash_a_fwd(qkvh)
            if len(started[i]) > 1:
                arrive(1, o)
            sv.update(qkv=qkv, qkvh=qkvh, o=o, lse=lse)
            h = _mm_nn(o, wl["a_w_o"], 0, blocked=False, res=h, name="a_out")
        else:
            qkvp = [_mm_nn_perm(hn, wl["b_w_qkv"], g) for g in range(G)]
            os_, lzs = [], []
            for g in range(G):
                o_g, lz_g = _battn_fwd(qkvp[g], bias, g)
                os_.append(o_g)
                lzs.append(lz_g)
            y = _combine_fwd(os_, lzs)
            sv.update(qkvp=qkvp, os=os_, lzs=lzs, y=y)
            h = _mm_nn(y, wl["b_w_o"], 0, blocked=False, res=h, name="b_out")
        sv["h1"] = h
        hf = _rms_fwd(h, ffn_norm[i:i + 1])
        u = _mm_nn(hf, wl["w_up"], 0, blocked=True, name="ffn_up")
        act = _ffn_act_fwd(u, wl["conv_w"], cb3[i:i + 1], 0)
        sv.update(hf=hf, u=u, act=act)
        h = _mm_nn(act, wl["w_down"], 0, blocked=False, res=h, name="ffn_down")
        saved.append(sv)

    loss_blk, dh, dh_b, dg_final = _final_loss(h, final_norm.reshape(1, D), target)

    dws = {k: [None] * big[k].shape[0] for k in names}
    d_mix, d_ffn, d_convw, d_convb = [None] * depth, [None] * depth, [None] * depth, [None] * depth
    d_gq, d_gk = [None] * n_a, [None] * n_a
    dbias_list = []
    pending = []

    def start_reduce(keys, tag):
        pieces = [_split8(dws[k][l], blocked[k]) for k, l in keys]
        send, recv, pieces, lands, token = _reduce_start(pieces, "reduce_start_" + tag)
        pending.append((keys, send, recv, pieces, lands, tag))
        return (token,)

    tok = ()
    for i in reversed(range(depth)):
        j = i // 2
        sv = saved[i]
        wl = sv["w"]
        da = _mm_nt(dh_b, wl["w_down"], 0, blocked=False, name="ffn_down_dx", after=tok)
        dws["w_down"][i] = _mm_tn(sv["act"], dh_b, blocked=False, name="ffn_down_dw")
        du, dconv = _ffn_act_bwd(sv["u"], da, wl["conv_w"], cb3[i:i + 1], 0)
        d_convw[i], d_convb[i] = dconv[0:3], dconv[3]
        dhf = _mm_nt(du, wl["w_up"], 0, blocked=True, name="ffn_up_dx")
        dws["w_up"][i] = _mm_tn(sv["hf"], du, blocked=True, name="ffn_up_dw")
        dh, dh_b, dg = _rms_bwd(dhf, sv["h1"], ffn_norm[i:i + 1], dh)
        d_ffn[i] = dg[0]
        tok = start_reduce([("w_down", i), ("w_up", i)], "ffn%d" % i)
        if i % 2 == 0:
            do = _mm_nt(dh_b, wl["a_w_o"], 0, blocked=False, name="a_out_dx", after=tok)
            dws["a_w_o"][j] = _mm_tn(sv["o"], dh_b, blocked=False, name="a_out_dw")
            dq, dk, dv = _flash_a_bwd(sv["qkvh"], do, sv["o"], sv["lse"])
            dqkv, dgain = _prep_a_bwd(dq, dk, dv, sv["qkv"], cos, sin, a_q_gain[j:j + 1], a_k_gain[j:j + 1])
            d_gq[j], d_gk[j] = dgain[0], dgain[1]
            dhn = _mm_nt(dqkv, wl["a_w_qkv"], 0, blocked=True, name="a_qkv_dx")
            dws["a_w_qkv"][j] = _mm_tn(sv["hn"], dqkv, blocked=True, name="a_qkv_dw")
            mix_keys = [("a_w_o", j), ("a_w_qkv", j)]
        else:
            dy = _mm_nt(dh_b, wl["b_w_o"], 0, blocked=False, name="b_out_dx", after=tok)
            dws["b_w_o"][j] = _mm_tn(sv["y"], dh_b, blocked=False, name="b_out_dw")
            dos, dlzs = _combine_bwd(dy, sv["os"], sv["lzs"])
            parts = []
            for g in range(G):
                dq, rt, db = _battn_bwd_dq(sv["qkvp"][g], bias, dos[g], sv["os"][g], sv["lzs"][g], dlzs[g], g)
                dk, dv = _battn_bwd_dkv(sv["qkvp"][g], bias_t, dos[g], sv["lzs"][g], rt, g)
                parts += [dq, dk, dv]
                dbias_list.append((g, db))
            dqkv = _concat_cast(parts, [d for _, d in B_GROUPS for _ in range(3)])
            dhn = _mm_nt(dqkv, wl["b_w_qkv"], 0, blocked=True, name="b_qkv_dx")
            dws["b_w_qkv"][j] = _mm_tn(sv["hn"], dqkv, blocked=True, name="b_qkv_dw")
            mix_keys = [("b_w_o", j), ("b_w_qkv", j)]
        dh, dh_b, dg = _rms_bwd(dhn, sv["h0"], mix_norm[i:i + 1], dh)
        d_mix[i] = dg[0]
        tok = start_reduce(mix_keys, "mix%d" % i)
    grad_x = dh.reshape(x.shape)

    dbias_layers = [jnp.stack([db for g2, db in dbias_list[l * G:(l + 1) * G]]) for l in range(n_b)]
    d_rel = _bias_reduce(dbias_layers, buckets)[:, :G * hg]

    small = [jnp.stack(d_gq), jnp.stack(d_gk), d_rel, jnp.stack(d_mix), jnp.stack(d_ffn), jnp.stack(d_convw),
             jnp.stack(d_convb), dg_final[0]]
    sizes = [int(np.prod(s.shape)) for s in small]
    flat = jnp.concatenate([s.reshape(-1) for s in small])
    rows = -(-flat.shape[0] // (LANES * SUBLANES)) * SUBLANES
    flat = jnp.pad(flat, (0, rows * LANES - flat.shape[0])).reshape(rows, LANES)
    tot = _allreduce_small(flat).reshape(-1)
    offs = np.cumsum([0] + sizes)
    g_gq, g_gk, g_rel, g_mix, g_ffn, g_convw_full, g_convb, g_final = [
        tot[offs[k]:offs[k + 1]].reshape(small[k].shape) for k in range(len(small))]
    cq = conv_w.shape[2]
    g_convw = lax.dynamic_slice_in_dim(g_convw_full, (2 * cx + cy) * cq, cq, axis=2)

    grads = dict(a_q_gain=g_gq, a_k_gain=g_gk, rel_bias=g_rel, mix_norm=g_mix, ffn_norm=g_ffn,
                 conv_w=g_convw, conv_b=g_convb, final_norm=g_final)
    weights = dict(a_w_qkv=a_w_qkv, a_w_o=a_w_o, a_q_gain=a_q_gain, a_k_gain=a_k_gain, b_w_qkv=b_w_qkv, b_w_o=b_w_o,
                   rel_bias=rel_bias, mix_norm=mix_norm, ffn_norm=ffn_norm, w_up=w_up, conv_w=conv_w, conv_b=conv_b,
                   w_down=w_down, final_norm=final_norm)
    ms = dict(a_w_qkv=m_a_w_qkv, a_w_o=m_a_w_o, a_q_gain=m_a_q_gain, a_k_gain=m_a_k_gain, b_w_qkv=m_b_w_qkv,
              b_w_o=m_b_w_o, rel_bias=m_rel_bias, mix_norm=m_mix_norm, ffn_norm=m_ffn_norm, w_up=m_w_up,
              conv_w=m_conv_w, conv_b=m_conv_b, w_down=m_w_down, final_norm=m_final_norm)
    vs = dict(a_w_qkv=v_a_w_qkv, a_w_o=v_a_w_o, a_q_gain=v_a_q_gain, a_k_gain=v_a_k_gain, b_w_qkv=v_b_w_qkv,
              b_w_o=v_b_w_o, rel_bias=v_rel_bias, mix_norm=v_mix_norm, ffn_norm=v_ffn_norm, w_up=v_w_up,
              conv_w=v_conv_w, conv_b=v_conv_b, w_down=v_w_down, final_norm=v_final_norm)
    deltas, new_m, new_v, stacks = {}, {}, {}, {}

    def update(k):
        w = weights[k]
        two_d = (-1, w.shape[-1])
        d, nm, nv = _adamw(w.reshape(two_d), grads[k].reshape(two_d), ms[k].reshape(two_d), vs[k].reshape(two_d))
        deltas[k], new_m[k], new_v[k] = d.reshape(w.shape), nm.reshape(w.shape), nv.reshape(w.shape)
        return nv

    def collect(items, after):
        for keys, send, recv, pieces, lands, tag in items:
            pieces, lands = _reduce_wait(send, recv, pieces, lands, after, "reduce_wait_" + tag)
            for (k, l), p, land in zip(keys, pieces, lands):
                if k not in stacks:
                    stacks[k] = lax.empty((big[k].shape[0], 2) + p.shape[2:], F32)
                stacks[k] = _add_pieces(p, land, stacks[k], l)

    def share(ks):
        for k, gs in zip(ks, _ag_sibling([stacks[k] for k in ks])):
            grads[k] = gs.reshape(big[k].shape)

    late = [k for k in names if k in {kk for kk, _ in pending[-1][0]}]
    collect(pending[:-1], [dh])
    share([k for k in names if k not in late])
    done = [update(k) for k in weights if k not in late]
    collect(pending[-1:], done)
    share(late)
    for k in late:
        update(k)

    loss = lax.psum(loss_blk[0, 0], ("x", "y", "c"))
    keys = list(weights)
    return (loss, grad_x, *[grads[k].reshape(weights[k].shape) for k in keys], *[deltas[k] for k in keys],
            *[new_m[k] for k in keys], *[new_v[k] for k in keys])
```

```python
import functools
import math

import numpy as np
import jax
import jax.numpy as jnp
from jax import lax
from jax.experimental import pallas as pl
from jax.experimental.pallas import tpu as pltpu

F32 = jnp.float32
BF16 = jnp.bfloat16

HEAD_DIM = 128
A_HEADS = 16
A_KV_HEADS = 4
GRID_W = 64
ROPE_THETA = 10000.0
B_GROUPS = ((128, 1), (512, 4), (2048, 16))
B_HEADS_PER_GROUP = 8
REL_BUCKETS = 32
REL_MAX_DISTANCE = 1024
EPS = 1e-6
NEG_INF = -1e30
DEPTH = 4
ADAM_LR = 0.001
ADAM_B1 = 0.9
ADAM_B2 = 0.999
ADAM_EPS = 1e-08
ADAM_WD = 0.01
ADAM_STEP = 10

N_CHIPS = 4
LANES = 128
SUBLANES = 8
VMEM_LIMIT = 52 * 1024 * 1024
MESH = pl.DeviceIdType.MESH


def _pick(n, cands):
    for c in cands:
        if c <= n and n % c == 0:
            return c
    return n


def _lane_tile(n, cap):
    best = None
    for t in range(LANES, min(n, cap) + 1, LANES):
        if n % t == 0:
            best = t
    return best or n


def _cp(*sem):
    return pltpu.CompilerParams(dimension_semantics=sem if sem else None, vmem_limit_bytes=VMEM_LIMIT)


def _half_span():
    hs = {w // (2 * d) for w, d in B_GROUPS}
    assert len(hs) == 1
    return hs.pop()


def _rms_fwd(h, gain, after=()):
    S, D = h.shape
    ts = _pick(S, (512, 256, 128, 64, 32, 16))

    def body(h_ref, g_ref, *rest):
        o_ref = rest[-1]
        x = h_ref[...]
        r = lax.rsqrt(jnp.mean(x * x, axis=-1, keepdims=True) + EPS)
        o_ref[...] = (x * r * g_ref[...]).astype(o_ref.dtype)

    return pl.pallas_call(
        body, name="rms_fwd", grid=(S // ts,),
        in_specs=[pl.BlockSpec((ts, D), lambda i: (i, 0)), pl.BlockSpec((1, D), lambda i: (0, 0))]
        + [pl.BlockSpec(memory_space=pl.ANY)] * len(after),
        out_specs=pl.BlockSpec((ts, D), lambda i: (i, 0)),
        out_shape=jax.ShapeDtypeStruct((S, D), BF16), compiler_params=_cp("arbitrary"))(h, gain, *after)


def _rms_bwd(dy, h, gain, dres):
    S, D = h.shape
    ts = _pick(S, (256, 128, 64, 32, 16))

    def body(dy_ref, h_ref, g_ref, dres_ref, dh_ref, dhb_ref, dg_ref):
        @pl.when(pl.program_id(0) == 0)
        def _():
            dg_ref[...] = jnp.zeros_like(dg_ref)
        x = h_ref[...]
        dy = dy_ref[...]
        r = lax.rsqrt(jnp.mean(x * x, axis=-1, keepdims=True) + EPS)
        xn = x * r
        dg_ref[0:1, :] += jnp.sum(dy * xn, axis=0, keepdims=True)
        dxn = dy * g_ref[...]
        dx = r * (dxn - xn * jnp.mean(dxn * xn, axis=-1, keepdims=True))
        dh = dres_ref[...] + dx
        dh_ref[...] = dh
        dhb_ref[...] = dh.astype(BF16)

    row = pl.BlockSpec((ts, D), lambda i: (i, 0))
    return pl.pallas_call(
        body, name="rms_bwd", grid=(S // ts,),
        in_specs=[row, row, pl.BlockSpec((1, D), lambda i: (0, 0)), row],
        out_specs=[row, row, pl.BlockSpec((SUBLANES, D), lambda i: (0, 0))],
        out_shape=[jax.ShapeDtypeStruct((S, D), F32), jax.ShapeDtypeStruct((S, D), BF16),
                   jax.ShapeDtypeStruct((SUBLANES, D), F32)],
        compiler_params=_cp("arbitrary"))(dy, h, gain, dres)


def _final_loss(h, gain, target):
    S, D = h.shape
    ts = _pick(S, (256, 128, 64, 32, 16))

    def body(h_ref, g_ref, t_ref, loss_ref, dh_ref, dhb_ref, dg_ref):
        @pl.when(pl.program_id(0) == 0)
        def _():
            dg_ref[...] = jnp.zeros_like(dg_ref)
            loss_ref[...] = jnp.zeros_like(loss_ref)
        x = h_ref[...]
        g = g_ref[...]
        r = lax.rsqrt(jnp.mean(x * x, axis=-1, keepdims=True) + EPS)
        xn = x * r
        err = xn * g - t_ref[...]
        part = 0.5 * jnp.sum(jnp.mean(err * err, axis=-1, keepdims=True), axis=0, keepdims=True)
        loss_ref[0:1, 0:1] += part
        dy = err * (1.0 / D)
        dg_ref[0:1, :] += jnp.sum(dy * xn, axis=0, keepdims=True)
        dxn = dy * g
        dh = r * (dxn - xn * jnp.mean(dxn * xn, axis=-1, keepdims=True))
        dh_ref[...] = dh
        dhb_ref[...] = dh.astype(BF16)

    row = pl.BlockSpec((ts, D), lambda i: (i, 0))
    return pl.pallas_call(
        body, name="final_loss", grid=(S // ts,),
        in_specs=[row, pl.BlockSpec((1, D), lambda i: (0, 0)), row],
        out_specs=[pl.BlockSpec((SUBLANES, LANES), lambda i: (0, 0)), row, row,
                   pl.BlockSpec((SUBLANES, D), lambda i: (0, 0))],
        out_shape=[jax.ShapeDtypeStruct((SUBLANES, LANES), F32), jax.ShapeDtypeStruct((S, D), F32),
                   jax.ShapeDtypeStruct((S, D), BF16), jax.ShapeDtypeStruct((SUBLANES, D), F32)],
        compiler_params=_cp("arbitrary"))(h, gain, target)


_NN = (((1,), (0,)), ((), ()))
_NT = (((1,), (1,)), ((), ()))
_TN = (((0,), (0,)), ((), ()))


def _mm_nn(a, w, layer, *, blocked, out_dtype=F32, res=None, name, after=()):
    M, K = a.shape
    if blocked:
        nq = w.shape[3]
        N = N_CHIPS * nq
        tn = _lane_tile(nq, 1408)
        nps = nq // tn
        w_spec = pl.BlockSpec((None, None, K, tn), lambda i, j: (layer, j // nps, 0, j % nps))
    else:
        N = w.shape[2]
        tn = _lane_tile(N, 512)
        w_spec = pl.BlockSpec((None, K, tn), lambda i, j: (layer, 0, j))
    tm = _pick(M, (1024, 512, 256, 128, 64, 32, 16)) if K <= 3072 else _pick(M, (512, 256, 128, 64, 32, 16))

    def body(*refs):
        a_ref, w_ref, o_ref = refs[0], refs[1], refs[-1]
        acc = lax.dot_general(a_ref[...], w_ref[...], _NN, preferred_element_type=F32)
        if res is not None:
            acc = refs[2][...] + acc
        o_ref[...] = acc.astype(o_ref.dtype)

    in_specs = [pl.BlockSpec((tm, K), lambda i, j: (i, 0)), w_spec]
    args = [a, w]
    if res is not None:
        in_specs.append(pl.BlockSpec((tm, tn), lambda i, j: (i, j)))
        args.append(res)
    return pl.pallas_call(
        body, name=name, grid=(M // tm, N // tn), in_specs=in_specs + [pl.BlockSpec(memory_space=pl.ANY)] * len(after),
        out_specs=pl.BlockSpec((tm, tn), lambda i, j: (i, j)),
        out_shape=jax.ShapeDtypeStruct((M, N), out_dtype),
        compiler_params=_cp("arbitrary", "arbitrary"))(*args, *after)


def _mm_nt(a, w, layer, *, blocked, name, after=()):
    pair = isinstance(a, tuple)
    M = a[0].shape[0] if pair else a.shape[0]
    tm = _pick(M, (1024, 512, 256, 128, 64, 32, 16))
    if blocked:
        K, nq = w.shape[2], w.shape[3]
        tk = _pick(K, (1024, 512, 256, 128))
        half = N_CHIPS // 2

        def body(*refs):
            a_refs, (w_ref, o_ref, acc_ref) = refs[:-3], refs[-3:]
            p = pl.program_id(2)

            @pl.when(p == 0)
            def _():
                acc_ref[...] = jnp.zeros_like(acc_ref)
            if pair:
                @pl.when(p < half)
                def _():
                    acc_ref[...] += lax.dot_general(a_refs[0][...], w_ref[...], _NT, preferred_element_type=F32)

                @pl.when(p >= half)
                def _():
                    acc_ref[...] += lax.dot_general(a_refs[1][...], w_ref[...], _NT, preferred_element_type=F32)
            else:
                acc_ref[...] += lax.dot_general(a_refs[0][...], w_ref[...], _NT, preferred_element_type=F32)

            @pl.when(p == N_CHIPS - 1)
            def _():
                o_ref[...] = acc_ref[...]

        if pair:
            a_specs = [pl.BlockSpec((tm, nq), lambda i, j, p: (i, jnp.minimum(p, half - 1))),
                       pl.BlockSpec((tm, nq), lambda i, j, p: (i, jnp.maximum(p - half, 0)))]
            a_args = list(a)
        else:
            a_specs = [pl.BlockSpec((tm, nq), lambda i, j, p: (i, p))]
            a_args = [a]
        return pl.pallas_call(
            body, name=name, grid=(M // tm, K // tk, N_CHIPS),
            in_specs=a_specs + [pl.BlockSpec((None, None, tk, nq), lambda i, j, p: (layer, p, j, 0))],
            out_specs=pl.BlockSpec((tm, tk), lambda i, j, p: (i, j)),
            out_shape=jax.ShapeDtypeStruct((M, K), F32),
            scratch_shapes=[pltpu.VMEM((tm, tk), F32)],
            compiler_params=_cp("arbitrary", "arbitrary", "arbitrary"))(*a_args, w)
    K, N = w.shape[1], w.shape[2]
    tk = _pick(K, (1024, 512, 256, 128))

    def body(a_ref, w_ref, *rest):
        rest[-1][...] = lax.dot_general(a_ref[...], w_ref[...], _NT, preferred_element_type=F32)

    return pl.pallas_call(
        body, name=name, grid=(M // tm, K // tk),
        in_specs=[pl.BlockSpec((tm, N), lambda i, j: (i, 0)),
                  pl.BlockSpec((None, tk, N), lambda i, j: (layer, j, 0))]
        + [pl.BlockSpec(memory_space=pl.ANY)] * len(after),
        out_specs=pl.BlockSpec((tm, tk), lambda i, j: (i, j)),
        out_shape=jax.ShapeDtypeStruct((M, K), F32),
        compiler_params=_cp("arbitrary", "arbitrary"))(a, w, *after)


def _mm_tn(x, dy, *, blocked, name):
    pair = isinstance(dy, tuple)
    S, K = x.shape
    N = 2 * dy[0].shape[1] if pair else dy.shape[1]
    tk = _pick(K, (512, 256, 128))
    if blocked:
        nq = N // N_CHIPS
        tn = _lane_tile(nq, 1408)
        nps = nq // tn
        out_spec = pl.BlockSpec((None, tk, tn), lambda i, j, s: (j // nps, i, j % nps))
        out_shape = jax.ShapeDtypeStruct((N_CHIPS, K, nq), BF16)
    else:
        tn = _lane_tile(N, 1024)
        out_spec = pl.BlockSpec((tk, tn), lambda i, j, s: (i, j))
        out_shape = jax.ShapeDtypeStruct((K, N), BF16)
    nj = N // tn
    njh = nj // 2
    ns = 2 if pair else 1
    sh = S // ns

    def body(x_ref, *refs):
        o_ref, acc_ref = refs[-2], refs[-1]

        def product(dy_ref):
            part = lax.dot_general(x_ref[...], dy_ref[...], _TN, preferred_element_type=F32)
            if ns == 1:
                o_ref[...] = part.astype(o_ref.dtype)
            else:
                s = pl.program_id(2)

                @pl.when(s == 0)
                def _():
                    acc_ref[...] = part

                @pl.when(s == ns - 1)
                def _():
                    o_ref[...] = (acc_ref[...] + part).astype(o_ref.dtype)

        if pair:
            j = pl.program_id(1)
            pl.when(j < njh)(lambda: product(refs[0]))
            pl.when(j >= njh)(lambda: product(refs[1]))
        else:
            product(refs[0])

    if pair:
        assert nj % 2 == 0
        dy_specs = [pl.BlockSpec((sh, tn), lambda i, j, s: (jnp.where(j < njh, s, ns - 1), jnp.minimum(j, njh - 1))),
                    pl.BlockSpec((sh, tn), lambda i, j, s: (jnp.where(j < njh, 0, s), jnp.maximum(j - njh, 0)))]
        dy_args = list(dy)
    else:
        dy_specs = [pl.BlockSpec((sh, tn), lambda i, j, s: (s, j))]
        dy_args = [dy]
    return pl.pallas_call(
        body, name=name, grid=(K // tk, nj, ns),
        in_specs=[pl.BlockSpec((sh, tk), lambda i, j, s: (s, i))] + dy_specs,
        out_specs=out_spec, out_shape=out_shape,
        scratch_shapes=[pltpu.VMEM((tk, tn) if ns > 1 else (SUBLANES, LANES), F32)],
        compiler_params=_cp("arbitrary", "arbitrary", "arbitrary"))(x, *dy_args)


def _rope_tables(S):
    rows = S // GRID_W
    row_ids = jnp.repeat(jnp.arange(rows, dtype=F32), GRID_W)
    col_ids = jnp.tile(jnp.arange(GRID_W, dtype=F32), rows)
    quarter = HEAD_DIM // 4
    inv_freq = ROPE_THETA ** (-jnp.arange(quarter, dtype=F32) / quarter)
    ang_r = row_ids[:, None] * inv_freq[None, :]
    ang_c = col_ids[:, None] * inv_freq[None, :]
    cos = jnp.concatenate([jnp.cos(ang_r)] * 2 + [jnp.cos(ang_c)] * 2, axis=-1)
    sin = jnp.concatenate([-jnp.sin(ang_r), jnp.sin(ang_r), -jnp.sin(ang_c), jnp.sin(ang_c)], axis=-1)
    return cos, sin


def _swap_quarters(x):
    lane = lax.broadcasted_iota(jnp.int32, x.shape, 1)
    first = (lane % (HEAD_DIM // 2)) < (HEAD_DIM // 4)
    return jnp.where(first, pltpu.roll(x, HEAD_DIM - HEAD_DIM // 4, 1), pltpu.roll(x, HEAD_DIM // 4, 1))


A_SCALE = HEAD_DIM ** -0.5
A_QSCALE = A_SCALE * math.log2(math.e)


def _prep_a_fwd(qkv, cos, sin, gq, gk):
    S, W = qkv.shape
    nrm = A_HEADS + A_KV_HEADS
    ts = _pick(S, (256, 128, 64, 32, 16))

    def body(qkv_ref, cos_ref, sin_ref, gq_ref, gk_ref, o_ref):
        cos_t = cos_ref[...]
        sin_t = sin_ref[...]
        for j in range(nrm):
            sl = slice(j * HEAD_DIM, (j + 1) * HEAD_DIM)
            x = qkv_ref[:, sl]
            g = gq_ref[...] if j < A_HEADS else gk_ref[...]
            r = lax.rsqrt(jnp.mean(x * x, axis=-1, keepdims=True) + EPS)
            n = x * r * g
            y = n * cos_t + _swap_quarters(n) * sin_t
            o_ref[:, sl] = (y * A_QSCALE if j < A_HEADS else y).astype(BF16)
        o_ref[:, nrm * HEAD_DIM:] = qkv_ref[:, nrm * HEAD_DIM:].astype(BF16)

    row = lambda w: pl.BlockSpec((ts, w), lambda i: (i, 0))
    one = pl.BlockSpec((1, HEAD_DIM), lambda i: (0, 0))
    return pl.pallas_call(
        body, name="prep_a_fwd", grid=(S // ts,),
        in_specs=[row(W), row(HEAD_DIM), row(HEAD_DIM), one, one], out_specs=row(W),
        out_shape=jax.ShapeDtypeStruct((S, W), BF16), compiler_params=_cp("arbitrary"))(qkv, cos, sin, gq, gk)


def _prep_a_bwd(dq, dk, dv, qkv, cos, sin, gq, gk):
    S, W = qkv.shape
    nrm = A_HEADS + A_KV_HEADS
    nq, nk = A_HEADS * HEAD_DIM, A_KV_HEADS * HEAD_DIM
    ts = _pick(S, (256, 128, 64, 32, 16))

    def body(dq_ref, dk_ref, dv_ref, qkv_ref, cos_ref, sin_ref, gq_ref, gk_ref, o_ref, dg_ref):
        @pl.when(pl.program_id(0) == 0)
        def _():
            dg_ref[...] = jnp.zeros_like(dg_ref)
        cos_t = cos_ref[...]
        sin_t = sin_ref[...]
        for j in range(nrm):
            sl = slice(j * HEAD_DIM, (j + 1) * HEAD_DIM)
            x = qkv_ref[:, sl]
            if j < A_HEADS:
                dy, g, grow = dq_ref[:, sl], gq_ref[...], 0
            else:
                jj = j - A_HEADS
                dy, g, grow = dk_ref[:, jj * HEAD_DIM:(jj + 1) * HEAD_DIM], gk_ref[...], 1
            r = lax.rsqrt(jnp.mean(x * x, axis=-1, keepdims=True) + EPS)
            xn = x * r
            dn = dy * cos_t + _swap_quarters(dy * sin_t)
            dg_ref[grow:grow + 1, :] += jnp.sum(dn * xn, axis=0, keepdims=True)
            dxn = dn * g
            o_ref[:, sl] = (r * (dxn - xn * jnp.mean(dxn * xn, axis=-1, keepdims=True))).astype(BF16)
        o_ref[:, nrm * HEAD_DIM:] = dv_ref[...].astype(BF16)

    row = lambda w: pl.BlockSpec((ts, w), lambda i: (i, 0))
    one = pl.BlockSpec((1, HEAD_DIM), lambda i: (0, 0))
    return pl.pallas_call(
        body, name="prep_a_bwd", grid=(S // ts,),
        in_specs=[row(nq), row(nk), row(nk), row(W), row(HEAD_DIM), row(HEAD_DIM), one, one],
        out_specs=[row(W), pl.BlockSpec((SUBLANES, HEAD_DIM), lambda i: (0, 0))],
        out_shape=[jax.ShapeDtypeStruct((S, W), BF16), jax.ShapeDtypeStruct((SUBLANES, HEAD_DIM), F32)],
        compiler_params=_cp("arbitrary"))(dq, dk, dv, qkv, cos, sin, gq, gk)


def _flash_a_fwd(qkvh, after=()):
    S = qkvh.shape[0]
    grp = A_HEADS // A_KV_HEADS
    tq = _pick(S, (256, 128, 64, 32, 16))
    kc = _pick(S, (512, 256, 128))
    lanes = [slice(b * LANES, (b + 1) * LANES) for b in range(kc // LANES)]

    def body(q_ref, k_ref, v_ref, *rest):
        o_ref, lse_ref = rest[-2], rest[-1]
        q = q_ref[...]
        m_t = jnp.full((tq, LANES), -jnp.inf, F32)
        for c in range(S // kc):
            s = lax.dot_general(q, k_ref[c * kc:(c + 1) * kc, :], _NT, preferred_element_type=F32)
            for sl in lanes:
                m_t = jnp.maximum(m_t, s[:, sl])
        m = jnp.max(m_t, axis=-1, keepdims=True)
        l_t = jnp.zeros((tq, LANES), F32)
        acc = jnp.zeros((tq, HEAD_DIM), F32)
        for c in range(S // kc):
            rows = slice(c * kc, (c + 1) * kc)
            p = jnp.exp2(lax.dot_general(q, k_ref[rows, :], _NT, preferred_element_type=F32) - m)
            for sl in lanes:
                l_t = l_t + p[:, sl]
            acc = acc + lax.dot_general(p.astype(BF16), v_ref[rows, :], _NN, preferred_element_type=F32)
        l = jnp.sum(l_t, axis=-1, keepdims=True)
        o_ref[...] = (acc * (1.0 / l)).astype(BF16)
        lse_ref[...] = jnp.broadcast_to(m + jnp.log2(l), lse_ref.shape)

    qs = pl.BlockSpec((tq, HEAD_DIM), lambda h, i: (i, h))
    return pl.pallas_call(
        body, name="flash_a_fwd", grid=(A_HEADS, S // tq),
        in_specs=[qs,
                  pl.BlockSpec((S, HEAD_DIM), lambda h, i: (0, A_HEADS + h // grp)),
                  pl.BlockSpec((S, HEAD_DIM), lambda h, i: (0, A_HEADS + A_KV_HEADS + h // grp))]
        + [pl.BlockSpec(memory_space=pl.ANY)] * len(after),
        out_specs=[qs, qs],
        out_shape=[jax.ShapeDtypeStruct((S, A_HEADS * HEAD_DIM), BF16),
                   jax.ShapeDtypeStruct((S, A_HEADS * HEAD_DIM), F32)],
        compiler_params=_cp("arbitrary", "arbitrary"))(qkvh, qkvh, qkvh, *after)


def _flash_a_bwd(qkvh, do, o, lse):
    S = qkvh.shape[0]
    grp = A_HEADS // A_KV_HEADS
    tq = _pick(S, (256, 128, 64, 32, 16))
    nq = S // tq

    def body(q_ref, k_ref, v_ref, do_ref, o_ref, lse_ref, dq_ref, dk_ref, dv_ref):
        g, i = pl.program_id(1), pl.program_id(2)

        @pl.when((g == 0) & (i == 0))
        def _():
            dk_ref[...] = jnp.zeros_like(dk_ref)
            dv_ref[...] = jnp.zeros_like(dv_ref)
        q = q_ref[...]
        k = k_ref[...]
        do_f = do_ref[...]
        do_b = do_f.astype(BF16)
        delta = jnp.sum(do_f * o_ref[...].astype(F32), axis=-1, keepdims=True)
        p = jnp.exp2(lax.dot_general(q, k, _NT, preferred_element_type=F32) - lse_ref[:, 0:1])
        dp = lax.dot_general(do_b, v_ref[...], _NT, preferred_element_type=F32)
        ds_b = (p * (dp - delta)).astype(BF16)
        dq_ref[...] = lax.dot_general(ds_b, k, _NN, preferred_element_type=F32) * A_SCALE
        dk_ref[...] += lax.dot_general(ds_b, q, _TN, preferred_element_type=F32)
        dv_ref[...] += lax.dot_general(p.astype(BF16), do_b, _TN, preferred_element_type=F32)

        @pl.when((g == grp - 1) & (i == nq - 1))
        def _():
            dk_ref[...] = dk_ref[...] * (A_SCALE / A_QSCALE)

    qs = pl.BlockSpec((tq, HEAD_DIM), lambda kv, g, i: (i, kv * grp + g))
    kvs = lambda off: pl.BlockSpec((S, HEAD_DIM), lambda kv, g, i: (0, off + kv))
    return pl.pallas_call(
        body, name="flash_a_bwd", grid=(A_KV_HEADS, grp, S // tq),
        in_specs=[qs, kvs(A_HEADS), kvs(A_HEADS + A_KV_HEADS), qs, qs, qs],
        out_specs=[qs, kvs(0), kvs(0)],
        out_shape=[jax.ShapeDtypeStruct((S, A_HEADS * HEAD_DIM), F32),
                   jax.ShapeDtypeStruct((S, A_KV_HEADS * HEAD_DIM), F32),
                   jax.ShapeDtypeStruct((S, A_KV_HEADS * HEAD_DIM), F32)],
        compiler_params=_cp("arbitrary", "arbitrary", "arbitrary"))(qkvh, qkvh, qkvh, do, o, lse)


def _bucket_tables(transposed):
    hs = _half_span()
    tq, kv = 2 * hs, 4 * hs
    nb = REL_BUCKETS // 2
    max_exact = nb // 2
    shape = (kv, tq) if transposed else (tq, kv)
    out = np.zeros((len(B_GROUPS), 3) + shape, np.int32)
    win = np.arange(kv) - hs
    blk = np.arange(tq)
    for g, (_, dil) in enumerate(B_GROUPS):
        for case in range(3):
            inside = ((win >= 0) | (case != 0)) & ((win < tq) | (case != 2))
            if transposed:
                rel = blk[None, :] - win[:, None]
                ok = inside[:, None]
            else:
                rel = win[None, :] - blk[:, None]
                ok = inside[None, :]
            r = rel * dil
            n = np.abs(r)
            nf = np.maximum(n, 1).astype(np.float32)
            large = max_exact + (np.log(nf / np.float32(max_exact)) / np.float32(math.log(REL_MAX_DISTANCE / max_exact))
                                 * np.float32(nb - max_exact)).astype(np.int32)
            large = np.minimum(large, nb - 1)
            bucket = np.where(r > 0, nb, 0) + np.where(n < max_exact, n, large)
            out[g, case] = np.where((np.abs(rel) <= hs) & ok, bucket, -1)
    return out


def _bias_build(rel_bias, buckets):
    G, _, tq, kv = buckets.shape
    hg = B_HEADS_PER_GROUP

    def body(rb_ref, bk_ref, o_ref):
        col = pl.program_id(0) * hg + pl.program_id(2)
        bk = bk_ref[...]
        acc = jnp.full((tq, kv), NEG_INF, F32)
        for b in range(REL_BUCKETS):
            acc = jnp.where(bk == b, rb_ref[b, col], acc)
        o_ref[...] = acc

    return pl.pallas_call(
        body, name="bias_build", grid=(G, 3, hg),
        in_specs=[pl.BlockSpec(memory_space=pltpu.SMEM),
                  pl.BlockSpec((None, None, tq, kv), lambda g, c, h: (g, c, 0, 0))],
        out_specs=pl.BlockSpec((None, None, None, tq, kv), lambda g, c, h: (g, c, h, 0, 0)),
        out_shape=jax.ShapeDtypeStruct((G, 3, hg, tq, kv), F32),
        compiler_params=_cp("arbitrary", "arbitrary", "arbitrary"))(rel_bias, buckets)


def _bias_reduce(dbias_list, buckets):
    G, _, tq, kv = buckets.shape
    hg = B_HEADS_PER_GROUP
    n = len(dbias_list)

    def body(*refs):
        bk_ref, o_ref = refs[n], refs[n + 1]
        first = (pl.program_id(0) == 0) & (pl.program_id(1) == 0) & (pl.program_id(2) == 0)

        @pl.when(first)
        def _():
            o_ref[...] = jnp.zeros_like(o_ref)
        col = pl.program_id(0) * hg + pl.program_id(2)
        db = refs[0][...]
        for r in refs[1:n]:
            db = db + r[...]
        bk = bk_ref[...]
        rows = lax.broadcasted_iota(jnp.int32, (REL_BUCKETS, LANES), 0)
        cols = lax.broadcasted_iota(jnp.int32, (REL_BUCKETS, LANES), 1)
        acc = jnp.zeros((REL_BUCKETS, LANES), F32)
        for b in range(REL_BUCKETS):
            val = jnp.sum(jnp.sum(jnp.where(bk == b, db, 0.0), axis=1, keepdims=True), axis=0, keepdims=True)
            acc = acc + jnp.where((rows == b) & (cols == col), val, 0.0)
        o_ref[...] += acc

    tile = pl.BlockSpec((None, None, None, tq, kv), lambda g, c, h: (g, c, h, 0, 0))
    return pl.pallas_call(
        body, name="bias_reduce", grid=(G, 3, hg),
        in_specs=[tile] * n + [pl.BlockSpec((None, None, tq, kv), lambda g, c, h: (g, c, 0, 0))],
        out_specs=pl.BlockSpec((REL_BUCKETS, LANES), lambda g, c, h: (0, 0)),
        out_shape=jax.ShapeDtypeStruct((REL_BUCKETS, LANES), F32),
        compiler_params=_cp("arbitrary", "arbitrary", "arbitrary"))(*dbias_list, buckets)


def _mm_nn_perm(a, w, g):
    S, K = a.shape
    nq = w.shape[3]
    dil = B_GROUPS[g][1]
    wg3 = 3 * B_HEADS_PER_GROUP * HEAD_DIM
    tn = _pick(nq, (256, 128))
    assert wg3 % tn == 0
    nps, ntile = nq // tn, wg3 // tn
    tm = _pick(S, (1024, 512, 256))
    rows = tm // dil

    def body(a_ref, w_ref, o_ref, acc_ref):
        acc = lax.dot_general(a_ref[...], w_ref[...], _NN, preferred_element_type=F32)
        if dil == 1:
            o_ref[0] = acc.astype(BF16)
        else:
            for k in range(tn // LANES):
                acc_ref[k] = acc[:, k * LANES:(k + 1) * LANES]
            for c in range(dil):
                for k in range(tn // LANES):
                    o_ref[c, :, k * LANES:(k + 1) * LANES] = acc_ref[k, pl.ds(c, rows, stride=dil), :].astype(BF16)

    def w_map(i, j):
        t = g * ntile + j
        return (0, t // nps, 0, t % nps)

    return pl.pallas_call(
        body, name="b_qkv_g%d" % g, grid=(S // tm, ntile),
        in_specs=[pl.BlockSpec((tm, K), lambda i, j: (i, 0)), pl.BlockSpec((None, None, K, tn), w_map)],
        out_specs=pl.BlockSpec((dil, rows, tn), lambda i, j: (0, i, j)),
        out_shape=jax.ShapeDtypeStruct((dil, S // dil, wg3), BF16),
        scratch_shapes=[pltpu.VMEM((tn // LANES, tm, LANES), F32)],
        compiler_params=_cp("arbitrary", "arbitrary"))(a, w)


def _window_specs(S, wg, col):
    hs = _half_span()
    tq = 2 * hs
    per = tq // hs
    return (pl.BlockSpec((tq, wg), lambda i: (i, col)),
            pl.BlockSpec((hs, wg), lambda i: (jnp.maximum(i * per - 1, 0), col)),
            pl.BlockSpec((hs, wg), lambda i: (jnp.minimum((i + 1) * per, S // hs - 1), col)))


def _window_case(i, L):
    per = L // (2 * _half_span())
    r = i % per
    return jnp.where(r == 0, 0, jnp.where(r == per - 1, 2, 1))


def _window(prev_ref, main_ref, next_ref, sl):
    return jnp.concatenate([prev_ref[:, sl], main_ref[:, sl], next_ref[:, sl]], axis=0)


def _battn_fwd(qkvp, bias, g):
    dil, L, wg3 = qkvp.shape
    S = dil * L
    hs = _half_span()
    tq, kvl = 2 * hs, 4 * hs
    hg = B_HEADS_PER_GROUP
    wg = hg * HEAD_DIM
    scale = HEAD_DIM ** -0.5
    flat = qkvp.reshape(S, wg3)

    def body(q_ref, km, kp, kn, vm, vp, vn, b_ref, o_ref, lz_ref):
        case = _window_case(pl.program_id(0), L)
        for h in range(hg):
            sl = slice(h * HEAD_DIM, (h + 1) * HEAD_DIM)
            s = lax.dot_general(q_ref[:, sl], _window(kp, km, kn, sl), _NT, preferred_element_type=F32) * scale
            s = s + b_ref[case, h]
            m = jnp.max(s, axis=-1, keepdims=True)
            p = jnp.exp(s - m)
            l = jnp.sum(p, axis=-1, keepdims=True)
            o_ref[:, sl] = lax.dot_general(p.astype(BF16), _window(vp, vm, vn, sl), _NN, preferred_element_type=F32) / l
            lz_ref[:, sl] = jnp.broadcast_to(m + jnp.log(l), (tq, HEAD_DIM))

    blk = pl.BlockSpec((tq, wg), lambda i: (i, 0))
    o, lz = pl.pallas_call(
        body, name="battn_fwd_g%d" % g, grid=(S // tq,),
        in_specs=[_window_specs(S, wg, 0)[0], *_window_specs(S, wg, 1), *_window_specs(S, wg, 2),
                  pl.BlockSpec((None, 3, hg, tq, kvl), lambda i: (g, 0, 0, 0, 0))],
        out_specs=[blk, blk], out_shape=[jax.ShapeDtypeStruct((S, wg), F32)] * 2,
        compiler_params=_cp("arbitrary"))(flat, flat, flat, flat, flat, flat, flat, bias)
    return o, lz


def _battn_bwd_dq(qkvp, bias, do, o, lz, dlz, g):
    dil, L, wg3 = qkvp.shape
    S = dil * L
    hs = _half_span()
    tq, kvl = 2 * hs, 4 * hs
    hg = B_HEADS_PER_GROUP
    wg = hg * HEAD_DIM
    scale = HEAD_DIM ** -0.5
    flat = qkvp.reshape(S, wg3)

    def body(q_ref, km, kp, kn, vm, vp, vn, b_ref, do_ref, o_ref, lz_ref, dlz_ref, dq_ref, rt_ref, db_ref):
        i = pl.program_id(0)

        @pl.when(i == 0)
        def _():
            db_ref[...] = jnp.zeros_like(db_ref)
        case = _window_case(i, L)
        for h in range(hg):
            sl = slice(h * HEAD_DIM, (h + 1) * HEAD_DIM)
            kw = _window(kp, km, kn, sl)
            do_f = do_ref[:, sl]
            s = lax.dot_general(q_ref[:, sl], kw, _NT, preferred_element_type=F32) * scale + b_ref[case, h]
            p = jnp.exp(s - lz_ref[:, sl][:, 0:1])
            dp = lax.dot_general(do_f.astype(BF16), _window(vp, vm, vn, sl), _NT, preferred_element_type=F32)
            rt = dlz_ref[:, sl][:, 0:1] - jnp.sum(do_f * o_ref[:, sl], axis=-1, keepdims=True)
            ds = p * (dp + rt)
            db_ref[case, h] += ds
            dq_ref[:, sl] = lax.dot_general((ds * scale).astype(BF16), kw, _NN, preferred_element_type=F32)
            rt_ref[:, sl] = jnp.broadcast_to(rt, (tq, HEAD_DIM))

    blk = pl.BlockSpec((tq, wg), lambda i: (i, 0))
    row = jax.ShapeDtypeStruct((S, wg), F32)
    return pl.pallas_call(
        body, name="battn_bwd_dq_g%d" % g, grid=(S // tq,),
        in_specs=[_window_specs(S, wg, 0)[0], *_window_specs(S, wg, 1), *_window_specs(S, wg, 2),
                  pl.BlockSpec((None, 3, hg, tq, kvl), lambda i: (g, 0, 0, 0, 0)), blk, blk, blk, blk],
        out_specs=[blk, blk, pl.BlockSpec((3, hg, tq, kvl), lambda i: (0, 0, 0, 0))],
        out_shape=[row, row, jax.ShapeDtypeStruct((3, hg, tq, kvl), F32)],
        compiler_params=_cp("arbitrary"))(flat, flat, flat, flat, flat, flat, flat, bias, do, o, lz, dlz)


def _battn_bwd_dkv(qkvp, bias_t, do, lz, rt, g):
    dil, L, wg3 = qkvp.shape
    S = dil * L
    hs = _half_span()
    tq, kvl = 2 * hs, 4 * hs
    hg = B_HEADS_PER_GROUP
    wg = hg * HEAD_DIM
    scale = HEAD_DIM ** -0.5
    flat = qkvp.reshape(S, wg3)

    def body(k_ref, v_ref, qm, qp, qn, dom, dop, don, lzm, lzp, lzn, rtm, rtp, rtn, b_ref, dk_ref, dv_ref):
        case = _window_case(pl.program_id(0), L)
        for h in range(hg):
            sl = slice(h * HEAD_DIM, (h + 1) * HEAD_DIM)
            qw = _window(qp, qm, qn, sl)
            dow = _window(dop, dom, don, sl).astype(BF16)
            s = lax.dot_general(qw, k_ref[:, sl], _NT, preferred_element_type=F32) * scale + b_ref[case, h]
            p = jnp.exp(s - _window(lzp, lzm, lzn, sl)[:, 0:1])
            dp = lax.dot_general(dow, v_ref[:, sl], _NT, preferred_element_type=F32)
            ds_b = (p * (dp + _window(rtp, rtm, rtn, sl)[:, 0:1]) * scale).astype(BF16)
            dk_ref[:, sl] = lax.dot_general(ds_b, qw, _TN, preferred_element_type=F32)
            dv_ref[:, sl] = lax.dot_general(p.astype(BF16), dow, _TN, preferred_element_type=F32)

    blk = pl.BlockSpec((tq, wg), lambda i: (i, 0))
    row = jax.ShapeDtypeStruct((S, wg), F32)
    return pl.pallas_call(
        body, name="battn_bwd_dkv_g%d" % g, grid=(S // tq,),
        in_specs=[_window_specs(S, wg, 1)[0], _window_specs(S, wg, 2)[0], *_window_specs(S, wg, 0),
                  *_window_specs(S, wg, 0), *_window_specs(S, wg, 0), *_window_specs(S, wg, 0),
                  pl.BlockSpec((None, 3, hg, kvl, tq), lambda i: (g, 0, 0, 0, 0))],
        out_specs=[blk, blk], out_shape=[row, row],
        compiler_params=_cp("arbitrary"))(flat, flat, flat, flat, flat, do, do, do, lz, lz, lz, rt, rt, rt, bias_t)


def _group_weights(lz_refs, h):
    z = [r[h] for r in lz_refs]
    mx = functools.reduce(jnp.maximum, z)
    e = [jnp.exp(v - mx) for v in z]
    inv = 1.0 / functools.reduce(lambda a, b: a + b, e)
    return [v * inv for v in e]


def _to_token_order(src_ref, dst_ref, dil):
    rows = src_ref.shape[1]
    for k in range(dst_ref.shape[0]):
        sl = slice(k * LANES, (k + 1) * LANES)
        if dil == 1:
            dst_ref[k] = src_ref[0, :, sl]
        else:
            for c in range(dil):
                dst_ref[k, pl.ds(c, rows, stride=dil), :] = src_ref[c, :, sl]


def _to_subsequence_order(src_ref, dst_ref, dil):
    rows = dst_ref.shape[1]
    for k in range(src_ref.shape[0]):
        sl = slice(k * LANES, (k + 1) * LANES)
        if dil == 1:
            dst_ref[0, :, sl] = src_ref[k]
        else:
            for c in range(dil):
                dst_ref[c, :, sl] = src_ref[k, pl.ds(c, rows, stride=dil), :]


def _sub_view(a, dil):
    S, w = a.shape
    return a.reshape(dil, S // dil, w)


def _sub_spec(dil, ts, w):
    return pl.BlockSpec((dil, ts // dil, w), lambda i: (0, i, 0))


def _combine_fwd(os_, lzs):
    G = len(os_)
    S, Wg = os_[0].shape
    hg = B_HEADS_PER_GROUP
    dils = [d for _, d in B_GROUPS]
    ts = _pick(S, (256, 128))

    def body(*refs):
        o_in, lz_in, y_ref = refs[:G], refs[G:2 * G], refs[2 * G]
        o_nat, lz_nat = refs[2 * G + 1:3 * G + 1], refs[3 * G + 1:4 * G + 1]
        for g in range(G):
            _to_token_order(o_in[g], o_nat[g], dils[g])
            _to_token_order(lz_in[g], lz_nat[g], dils[g])
        for h in range(hg):
            w = _group_weights(lz_nat, h)
            for g in range(G):
                y_ref[:, (g * hg + h) * HEAD_DIM:(g * hg + h + 1) * HEAD_DIM] = (w[g] * o_nat[g][h]).astype(BF16)

    specs = [_sub_spec(d, ts, Wg) for d in dils]
    return pl.pallas_call(
        body, name="combine_fwd", grid=(S // ts,), in_specs=specs + specs,
        out_specs=pl.BlockSpec((ts, G * Wg), lambda i: (i, 0)),
        out_shape=jax.ShapeDtypeStruct((S, G * Wg), BF16),
        scratch_shapes=[pltpu.VMEM((hg, ts, HEAD_DIM), F32)] * (2 * G),
        compiler_params=_cp("arbitrary"))(*[_sub_view(a, d) for a, d in zip(os_, dils)],
                                          *[_sub_view(a, d) for a, d in zip(lzs, dils)])


def _combine_bwd(dy, os_, lzs):
    G = len(os_)
    S, Wg = os_[0].shape
    hg = B_HEADS_PER_GROUP
    dils = [d for _, d in B_GROUPS]
    ts = _pick(S, (128,))

    def body(*refs):
        dy_ref, o_in, lz_in = refs[0], refs[1:1 + G], refs[1 + G:1 + 2 * G]
        do_out, dlz_out = refs[1 + 2 * G:1 + 3 * G], refs[1 + 3 * G:1 + 4 * G]
        scr = refs[1 + 4 * G:]
        o_nat, lz_nat, do_nat, dlz_nat = scr[:G], scr[G:2 * G], scr[2 * G:3 * G], scr[3 * G:4 * G]
        for g in range(G):
            _to_token_order(o_in[g], o_nat[g], dils[g])
            _to_token_order(lz_in[g], lz_nat[g], dils[g])
        for h in range(hg):
            w = _group_weights(lz_nat, h)
            dw = []
            for g in range(G):
                dyg = dy_ref[:, (g * hg + h) * HEAD_DIM:(g * hg + h + 1) * HEAD_DIM]
                dw.append(jnp.sum(dyg * o_nat[g][h], axis=-1, keepdims=True))
                do_nat[g][h] = w[g] * dyg
            tot = functools.reduce(lambda a, b: a + b, [w[g] * dw[g] for g in range(G)])
            for g in range(G):
                dlz_nat[g][h] = w[g] * (dw[g] - tot)
        for g in range(G):
            _to_subsequence_order(do_nat[g], do_out[g], dils[g])
            _to_subsequence_order(dlz_nat[g], dlz_out[g], dils[g])

    specs = [_sub_spec(d, ts, Wg) for d in dils]
    outs = pl.pallas_call(
        body, name="combine_bwd", grid=(S // ts,),
        in_specs=[pl.BlockSpec((ts, G * Wg), lambda i: (i, 0))] + specs + specs,
        out_specs=specs + specs,
        out_shape=[jax.ShapeDtypeStruct((d, S // d, Wg), F32) for d in dils] * 2,
        scratch_shapes=[pltpu.VMEM((hg, ts, HEAD_DIM), F32)] * (4 * G),
        compiler_params=_cp("arbitrary"))(dy, *[_sub_view(a, d) for a, d in zip(os_, dils)],
                                          *[_sub_view(a, d) for a, d in zip(lzs, dils)])
    flat = [a.reshape(S, Wg) for a in outs]
    return flat[:G], flat[G:]


def _concat_cast(parts, dils):
    S = parts[0].shape[0]
    widths = [p.shape[1] for p in parts]
    n = len(parts)
    ts = _pick(S, (256, 128))

    def body(*refs):
        o_ref, nat = refs[n], refs[n + 1]
        off = 0
        for r, w, d in zip(refs, widths, dils):
            _to_token_order(r, nat, d)
            for k in range(w // LANES):
                o_ref[:, off + k * LANES:off + (k + 1) * LANES] = nat[k].astype(BF16)
            off += w

    assert len(set(widths)) == 1
    return pl.pallas_call(
        body, name="concat_cast", grid=(S // ts,),
        in_specs=[_sub_spec(d, ts, w) for w, d in zip(widths, dils)],
        out_specs=pl.BlockSpec((ts, sum(widths)), lambda i: (i, 0)),
        out_shape=jax.ShapeDtypeStruct((S, sum(widths)), BF16),
        scratch_shapes=[pltpu.VMEM((widths[0] // LANES, ts, LANES), F32)],
        compiler_params=_cp("arbitrary"))(*[_sub_view(p, d) for p, d in zip(parts, dils)])


def _ffn_specs(S, dff, cq, ts, tc, layer, order):
    nfc = dff // tc
    nps = cq // tc
    hb = ts // SUBLANES
    nrow8 = S // SUBLANES

    def u_main(half):
        return pl.BlockSpec((ts, tc), lambda *g: (order(*g)[0], order(*g)[1] % nfc + half * nfc))

    def u_prev(half):
        return pl.BlockSpec((SUBLANES, tc), lambda *g: (jnp.maximum(order(*g)[0] * hb - 1, 0),
                                                         order(*g)[1] % nfc + half * nfc))

    def u_next(half):
        return pl.BlockSpec((SUBLANES, tc), lambda *g: (jnp.minimum((order(*g)[0] + 1) * hb, nrow8 - 1),
                                                         order(*g)[1] % nfc + half * nfc))

    def cw(half):
        def im(*g):
            jj = order(*g)[1] % nfc + half * nfc
            return (layer, jj // nps, 0, jj % nps)
        return pl.BlockSpec((None, None, 3, tc), im)

    def cb(half):
        return pl.BlockSpec((None, 1, tc), lambda *g: (layer, 0, order(*g)[1] % nfc + half * nfc))

    return nfc, u_main, u_prev, u_next, cw, cb


def _ffn_act_fwd(u, cw_full, cb3, layer):
    S, two_dff = u.shape
    dff = two_dff // 2
    cq = cw_full.shape[3]
    ts = _pick(S, (1024, 512, 256, 128, 64, 32, 16))
    tc = _pick(cq, (256, 128))
    order = lambda j, i: (i, j)
    nfc, u_main, u_prev, u_next, cw, cb = _ffn_specs(S, dff, cq, ts, tc, layer, order)
    nrow = S // ts

    def body(ug, ugp, ugn, uv, uvp, uvn, wg, wv, bg, bv, a_ref):
        i = pl.program_id(1)
        row = lax.broadcasted_iota(jnp.int32, (ts, tc), 0)

        def conv(x_ref, p_ref, n_ref, w_ref, b_ref):
            x = x_ref[...]
            prev = jnp.where(i > 0, p_ref[SUBLANES - 1:SUBLANES, :], 0.0)
            nxt = jnp.where(i < nrow - 1, n_ref[0:1, :], 0.0)
            xm = jnp.where(row == 0, prev, pltpu.roll(x, 1, 0))
            xp = jnp.where(row == ts - 1, nxt, pltpu.roll(x, ts - 1, 0))
            return w_ref[0:1, :] * xm + w_ref[1:2, :] * x + w_ref[2:3, :] * xp + b_ref[...]

        gc = conv(ug, ugp, ugn, wg, bg)
        vc = conv(uv, uvp, uvn, wv, bv)
        a_ref[...] = (gc * (1.0 / (1.0 + jnp.exp(-gc))) * vc).astype(BF16)

    return pl.pallas_call(
        body, name="ffn_act_fwd", grid=(nfc, nrow),
        in_specs=[u_main(0), u_prev(0), u_next(0), u_main(1), u_prev(1), u_next(1), cw(0), cw(1), cb(0), cb(1)],
        out_specs=pl.BlockSpec((ts, tc), lambda j, i: (i, j)),
        out_shape=jax.ShapeDtypeStruct((S, dff), BF16),
        compiler_params=_cp("arbitrary", "arbitrary"))(u, u, u, u, u, u, cw_full, cw_full, cb3, cb3)


def _ffn_act_bwd(u, da, cw_full, cb3, layer):
    S, two_dff = u.shape
    dff = two_dff // 2
    cq = cw_full.shape[3]
    ts = _pick(S, (1024, 512, 256, 128, 64, 32, 16))
    tc = _pick(cq, (256, 128))
    order = lambda j, i: (i, j)
    nfc, u_main, u_prev, u_next, cw, cb = _ffn_specs(S, dff, cq, ts, tc, layer, order)
    nrow = S // ts
    hb = ts // SUBLANES
    te = ts + 2 * SUBLANES
    da_main = pl.BlockSpec((ts, tc), lambda j, i: (i, j))
    da_prev = pl.BlockSpec((SUBLANES, tc), lambda j, i: (jnp.maximum(i * hb - 1, 0), j))
    da_next = pl.BlockSpec((SUBLANES, tc), lambda j, i: (jnp.minimum((i + 1) * hb, S // SUBLANES - 1), j))
    main = slice(SUBLANES, SUBLANES + ts)

    def body(ug, ugp, ugn, uv, uvp, uvn, dam, dap, dan, wg, wv, bg, bv, dug_ref, duv_ref, accg_ref, accv_ref):
        i = pl.program_id(1)

        @pl.when(i == 0)
        def _():
            accg_ref[...] = jnp.zeros_like(accg_ref)
            accv_ref[...] = jnp.zeros_like(accv_ref)

        def ext(m, p, n):
            return jnp.concatenate([jnp.where(i > 0, p[...], 0.0), m[...], jnp.where(i < nrow - 1, n[...], 0.0)], axis=0)

        def shift(x):
            return pltpu.roll(x, 1, 0), pltpu.roll(x, te - 1, 0)

        xg, xv, dae = ext(ug, ugp, ugn), ext(uv, uvp, uvn), ext(dam, dap, dan)
        xgm, xgp = shift(xg)
        xvm, xvp = shift(xv)
        gc = wg[0:1, :] * xgm + wg[1:2, :] * xg + wg[2:3, :] * xgp + bg[...]
        vc = wv[0:1, :] * xvm + wv[1:2, :] * xv + wv[2:3, :] * xvp + bv[...]
        sig = 1.0 / (1.0 + jnp.exp(-gc))
        silu = gc * sig
        dcg = dae * vc * (sig * (1.0 + gc * (1.0 - sig)))
        dcv = dae * silu

        def finish(dc, x, xm, xp, w_ref, du_ref, acc_ref):
            dm, dp = shift(dc)
            du = w_ref[0:1, :] * dp + w_ref[1:2, :] * dc + w_ref[2:3, :] * dm
            du_ref[...] = du[main, :].astype(BF16)
            dcm = dc[main, :]
            acc_ref[0:1, :] += jnp.sum(dcm * xm[main, :], axis=0, keepdims=True)
            acc_ref[1:2, :] += jnp.sum(dcm * x[main, :], axis=0, keepdims=True)
            acc_ref[2:3, :] += jnp.sum(dcm * xp[main, :], axis=0, keepdims=True)
            acc_ref[3:4, :] += jnp.sum(dcm, axis=0, keepdims=True)

        finish(dcg, xg, xgm, xgp, wg, dug_ref, accg_ref)
        finish(dcv, xv, xvm, xvp, wv, duv_ref, accv_ref)

    blk = pl.BlockSpec((ts, tc), lambda j, i: (i, j))
    acc = pl.BlockSpec((SUBLANES, tc), lambda j, i: (0, j))
    dug, duv, accg, accv = pl.pallas_call(
        body, name="ffn_act_bwd", grid=(nfc, nrow),
        in_specs=[u_main(0), u_prev(0), u_next(0), u_main(1), u_prev(1), u_next(1), da_main, da_prev, da_next,
                  cw(0), cw(1), cb(0), cb(1)],
        out_specs=[blk, blk, acc, acc],
        out_shape=[jax.ShapeDtypeStruct((S, dff), BF16)] * 2 + [jax.ShapeDtypeStruct((SUBLANES, dff), F32)] * 2,
        compiler_params=_cp("arbitrary", "arbitrary"))(u, u, u, u, u, u, da, da, da, cw_full, cw_full, cb3, cb3)
    return (dug, duv), jnp.concatenate([accg, accv], axis=1)


def _my_chip():
    return 2 * lax.axis_index("x") + lax.axis_index("y")


def _into_full(w, layer, dtype):
    L, a, b = w.shape
    tr = _pick(a, (512, 256, 128, 64, 32, 16, 8))

    def body(w_ref, o_ref):
        o_ref[...] = w_ref[...].astype(dtype)

    return pl.pallas_call(
        body, name="into_full", grid=(a // tr,),
        in_specs=[pl.BlockSpec((None, tr, b), lambda i: (layer, i, 0))],
        out_specs=pl.BlockSpec((None, None, tr, b), lambda i: (0, _my_chip(), i, 0)),
        out_shape=jax.ShapeDtypeStruct((1, N_CHIPS, a, b), dtype),
        compiler_params=_cp("arbitrary"))(w)


def _adam_math(w, g, m, v):
    m = ADAM_B1 * m + (1.0 - ADAM_B1) * g
    v = ADAM_B2 * v + (1.0 - ADAM_B2) * (g * g)
    m_hat = m / (1.0 - ADAM_B1 ** ADAM_STEP)
    v_hat = v / (1.0 - ADAM_B2 ** ADAM_STEP)
    delta = -ADAM_LR * (m_hat / (jnp.sqrt(v_hat) + ADAM_EPS) + ADAM_WD * w)
    return delta, m, v


def _adamw(w, g, m, v):
    R, C = w.shape
    tr = _pick(R, (128, 64, 32, 16, 8)) if R % SUBLANES == 0 and C % LANES == 0 else R

    def body(w_ref, g_ref, m_ref, v_ref, d_ref, nm_ref, nv_ref):
        d, nm, nv = _adam_math(w_ref[...], g_ref[...], m_ref[...], v_ref[...])
        d_ref[...] = d
        nm_ref[...] = nm
        nv_ref[...] = nv

    spec = pl.BlockSpec((tr, C), lambda i: (i, 0))
    return pl.pallas_call(
        body, name="adamw", grid=(R // tr,), in_specs=[spec] * 4, out_specs=[spec] * 3,
        out_shape=[jax.ShapeDtypeStruct((R, C), F32)] * 3, compiler_params=_cp("arbitrary"))(w, g, m, v)


ANY = pl.BlockSpec(memory_space=pl.ANY)


def _position():
    x, y, c = lax.axis_index("x"), lax.axis_index("y"), lax.axis_index("c")
    chips = [(1 - x, y), (x, 1 - y), (1 - x, 1 - y)]
    return x, y, c, chips


HBM = pl.BlockSpec(memory_space=pltpu.HBM)
SEM = pl.BlockSpec(memory_space=pltpu.SEMAPHORE)
EFFECT = pltpu.SideEffectType.DATAFLOW_SIDE_EFFECTING


def _in_hbm(a):
    return pltpu.with_memory_space_constraint(a, pltpu.HBM)


def _shard_half(buf, shape, chip, half):
    _, _, a, b = shape
    p = 2 * chip[0] + chip[1]
    if a % (4 * SUBLANES) == 0:
        return buf.at[0, p, pl.ds(half * (a // 2), a // 2)]
    return buf.at[0, p, :, pl.ds(half * (b // 2), b // 2)]


def _gather_copy(buf, shape, chip, half, to, send, recv, k):
    part = _shard_half(buf, shape, chip, half)
    return pltpu.make_async_remote_copy(src_ref=part, dst_ref=part, send_sem=send.at[k], recv_sem=recv.at[k],
                                        device_id=to, device_id_type=MESH)


def _gather_hop(buf, shape, hop, j, incoming, send, recv, k):
    x, y, c, chips = _position()
    if hop == "chips":
        chip, half, to = (chips[j] if incoming else (x, y)), c, (*chips[j], c)
    else:
        chip, half, to = chips[j], (1 - c if incoming else c), (x, y, 1 - c)
    return _gather_copy(buf, shape, chip, half, to, send, recv, k)


def _gather_start(fulls, name, hop, after=()):
    n = len(fulls)
    na = len(after)

    def body(*refs):
        send, recv = refs[n + na], refs[n + na + 1]
        buf, token = refs[n + na + 2:2 * n + na + 2], refs[2 * n + na + 2]
        for t in range(n):
            for j in range(3):
                _gather_hop(buf[t], fulls[t].shape, hop, j, False, send, recv, 3 * t + j).start()
        token[...] = jnp.zeros_like(token)

    outs = pl.pallas_call(
        body, name=name, in_specs=[HBM] * n + [ANY] * na,
        out_specs=[SEM, SEM] + [HBM] * n + [pl.BlockSpec(memory_space=pltpu.VMEM)],
        out_shape=[pltpu.SemaphoreType.DMA((3 * n,)), pltpu.SemaphoreType.DMA((3 * n,))]
        + [pltpu.HBM(f.shape, f.dtype) for f in fulls] + [jax.ShapeDtypeStruct((SUBLANES, LANES), F32)],
        input_output_aliases={t: 2 + t for t in range(n)},
        compiler_params=pltpu.CompilerParams(has_side_effects=EFFECT))(*[_in_hbm(f) for f in fulls], *after)
    return outs[0], outs[1], list(outs[2:2 + n]), outs[2 + n]


def _gather_wait(send, recv, fulls, after, name, hop):
    n = len(fulls)

    def body(*refs):
        buf, send_ref, recv_ref = refs[:n], refs[n], refs[n + 1]
        for t in range(n):
            for j in range(3):
                _gather_hop(buf[t], fulls[t].shape, hop, j, False, send_ref, recv_ref, 3 * t + j).wait_send()
                _gather_hop(buf[t], fulls[t].shape, hop, j, True, send_ref, recv_ref, 3 * t + j).wait_recv()

    outs = pl.pallas_call(
        body, name=name, in_specs=[HBM] * n + [SEM, SEM] + [ANY] * len(after), out_specs=[HBM] * n,
        out_shape=[pltpu.HBM(f.shape, f.dtype) for f in fulls],
        input_output_aliases={t: t for t in range(n)},
        compiler_params=pltpu.CompilerParams(has_side_effects=EFFECT))(*fulls, send, recv, *after)
    return list(outs)


def _allreduce_small(part):
    M, C = part.shape
    n_dev = 2 * N_CHIPS

    def body(x_ref, sum_ref, all_ref, send, recv, local):
        x, y, c, chips = _position()
        me, sib = (x, y, c), (x, y, 1 - c)

        def rows(px, py, pc):
            return all_ref.at[pl.ds((4 * px + 2 * py + pc) * M, M), :]

        def copy(k, block, to, src=None):
            return pltpu.make_async_remote_copy(
                src_ref=rows(*block) if src is None else src, dst_ref=rows(*block),
                send_sem=send.at[k], recv_sem=recv.at[k], device_id=to, device_id_type=MESH)

        mine = pltpu.make_async_copy(x_ref, rows(*me), local)
        mine.start()
        first = [copy(0, me, sib, src=x_ref)] + [copy(1 + j, me, (*chip, c), src=x_ref) for j, chip in enumerate(chips)]
        for cp in first:
            cp.start()
        passed = [copy(4 + j, (*chip, c), sib) for j, chip in enumerate(chips)]
        for j, chip in enumerate(chips):
            copy(1 + j, (*chip, c), me).wait_recv()
            passed[j].start()
        copy(0, sib, me).wait_recv()
        for j, chip in enumerate(chips):
            copy(4 + j, (*chip, 1 - c), me).wait_recv()
        for cp in first + passed:
            cp.wait_send()
        mine.wait()
        acc = all_ref[0:M, :]
        for d in range(1, n_dev):
            acc = acc + all_ref[d * M:(d + 1) * M, :]
        sum_ref[...] = acc

    vm = pl.BlockSpec(memory_space=pltpu.VMEM)
    return pl.pallas_call(
        body, name="allreduce_small", in_specs=[vm], out_specs=[vm],
        out_shape=[jax.ShapeDtypeStruct((M, C), F32)],
        scratch_shapes=[pltpu.VMEM((n_dev * M, C), F32), pltpu.SemaphoreType.DMA((7,)),
                        pltpu.SemaphoreType.DMA((7,)), pltpu.SemaphoreType.DMA],
        compiler_params=pltpu.CompilerParams(vmem_limit_bytes=VMEM_LIMIT))(part)[0]


N_PEERS = 2 * N_CHIPS - 1


def _peers():
    x, y, c, chips = _position()
    return [(x, y, 1 - c)] + [(*ch, c) for ch in chips] + [(*ch, 1 - c) for ch in chips]


def _reduce_copy(src, dst, peers, send, recv, t, r):
    px, py, pc = peers[r]
    return pltpu.make_async_remote_copy(
        src_ref=src.at[2 * px + py, pc], dst_ref=dst.at[r], send_sem=send.at[N_PEERS * t + r],
        recv_sem=recv.at[N_PEERS * t + r], device_id=peers[r], device_id_type=MESH)


def _reduce_start(grads, name, after=()):
    n = len(grads)
    na = len(after)
    lands = [lax.empty((N_PEERS,) + g.shape[2:], BF16) for g in grads]

    def body(*refs):
        send, recv = refs[2 * n + na], refs[2 * n + na + 1]
        src, dst = refs[2 * n + na + 2:3 * n + na + 2], refs[3 * n + na + 2:4 * n + na + 2]
        token = refs[4 * n + na + 2]
        peers = _peers()
        for t in range(n):
            for r in range(N_PEERS):
                _reduce_copy(src[t], dst[t], peers, send, recv, t, r).start()
        token[...] = jnp.zeros_like(token)

    outs = pl.pallas_call(
        body, name=name, in_specs=[HBM] * (2 * n) + [ANY] * na,
        out_specs=[SEM, SEM] + [HBM] * (2 * n) + [pl.BlockSpec(memory_space=pltpu.VMEM)],
        out_shape=[pltpu.SemaphoreType.DMA((N_PEERS * n,)), pltpu.SemaphoreType.DMA((N_PEERS * n,))]
        + [pltpu.HBM(a.shape, a.dtype) for a in grads + lands] + [jax.ShapeDtypeStruct((SUBLANES, LANES), F32)],
        input_output_aliases={t: 2 + t for t in range(2 * n)},
        compiler_params=pltpu.CompilerParams(has_side_effects=EFFECT))(*[_in_hbm(a) for a in grads + lands], *after)
    return outs[0], outs[1], list(outs[2:2 + n]), list(outs[2 + n:2 + 2 * n]), outs[2 + 2 * n]


def _reduce_wait(send, recv, grads, lands, after, name):
    n = len(grads)

    def body(*refs):
        src, dst, send_ref, recv_ref = refs[:n], refs[n:2 * n], refs[2 * n], refs[2 * n + 1]
        peers = _peers()
        for t in range(n):
            for r in range(N_PEERS):
                cp = _reduce_copy(src[t], dst[t], peers, send_ref, recv_ref, t, r)
                cp.wait_send()
                cp.wait_recv()

    outs = pl.pallas_call(
        body, name=name, in_specs=[HBM] * (2 * n) + [SEM, SEM] + [ANY] * len(after), out_specs=[HBM] * (2 * n),
        out_shape=[pltpu.HBM(a.shape, a.dtype) for a in grads + lands],
        input_output_aliases={t: t for t in range(2 * n)},
        compiler_params=pltpu.CompilerParams(has_side_effects=EFFECT))(*grads, *lands, send, recv, *after)
    return list(outs[:n]), list(outs[n:])


def _add_pieces(grad, land, stack, layer):
    _, _, R, C = grad.shape
    tr = _pick(R, (256, 128, 64, 32, 16))

    def body(g_ref, r_ref, stack_ref, o_ref):
        acc = g_ref[...].astype(F32)
        for r in range(N_PEERS):
            acc = acc + r_ref[r].astype(F32)
        o_ref[...] = acc

    return pl.pallas_call(
        body, name="add_pieces", grid=(R // tr,),
        in_specs=[pl.BlockSpec((None, None, tr, C), lambda i: (_my_chip(), lax.axis_index("c"), i, 0)),
                  pl.BlockSpec((N_PEERS, tr, C), lambda i: (0, i, 0)),
                  ANY],
        out_specs=pl.BlockSpec((None, None, tr, C), lambda i: (layer, lax.axis_index("c"), i, 0)),
        out_shape=jax.ShapeDtypeStruct(stack.shape, F32), input_output_aliases={2: 0},
        compiler_params=_cp("arbitrary"))(grad, land, stack)


def _ag_sibling(stacks):
    n = len(stacks)
    offs = np.cumsum([0] + [s.shape[0] for s in stacks])

    def body(*refs):
        buf, send, recv = refs[n:2 * n], refs[2 * n], refs[2 * n + 1]
        x, y, c, _ = _position()

        def copy(t, l, half):
            part = buf[t].at[l, half]
            return pltpu.make_async_remote_copy(
                src_ref=part, dst_ref=part, send_sem=send.at[int(offs[t]) + l], recv_sem=recv.at[int(offs[t]) + l],
                device_id=(x, y, 1 - c), device_id_type=MESH)

        cps = [copy(t, l, c) for t in range(n) for l in range(stacks[t].shape[0])]
        for cp in cps:
            cp.start()
        for t in range(n):
            for l in range(stacks[t].shape[0]):
                copy(t, l, 1 - c).wait_recv()
        for cp in cps:
            cp.wait_send()

    return pl.pallas_call(
        body, name="ag_sibling", in_specs=[ANY] * n, out_specs=[ANY] * n,
        out_shape=[jax.ShapeDtypeStruct(s.shape, F32) for s in stacks],
        input_output_aliases={t: t for t in range(n)},
        scratch_shapes=[pltpu.SemaphoreType.DMA((int(offs[-1]),)), pltpu.SemaphoreType.DMA((int(offs[-1]),))])(*stacks)


def _split8(dw, blocked):
    if blocked:
        p, k, nq = dw.shape
        return dw.reshape(p, 2, k // 2, nq)
    k, n = dw.shape
    return dw.reshape(N_CHIPS, 2, k // (2 * N_CHIPS), n)


def kernel(x, a_w_qkv, a_w_o, a_q_gain, a_k_gain, b_w_qkv, b_w_o, rel_bias, mix_norm, ffn_norm, w_up, conv_w, conv_b, w_down, final_norm, loss_target, m_a_w_qkv, m_a_w_o, m_a_q_gain, m_a_k_gain, m_b_w_qkv, m_b_w_o, m_rel_bias, m_mix_norm, m_ffn_norm, m_w_up, m_conv_w, m_conv_b, m_w_down, m_final_norm, v_a_w_qkv, v_a_w_o, v_a_q_gain, v_a_k_gain, v_b_w_qkv, v_b_w_o, v_rel_bias, v_mix_norm, v_ffn_norm, v_w_up, v_conv_w, v_conv_b, v_w_down, v_final_norm):
    S, D = x.shape[1], x.shape[2]
    h = x.reshape(S, D)
    target = loss_target.reshape(S, D)
    hg = B_HEADS_PER_GROUP
    G = len(B_GROUPS)
    n_a, n_b = a_w_qkv.shape[0], b_w_qkv.shape[0]
    depth = w_up.shape[0]
    cx, cy = lax.axis_index("x"), lax.axis_index("y")

    big = dict(a_w_qkv=a_w_qkv, a_w_o=a_w_o, b_w_qkv=b_w_qkv, b_w_o=b_w_o, w_up=w_up, w_down=w_down)
    blocked = dict(a_w_qkv=True, a_w_o=False, b_w_qkv=True, b_w_o=False, w_up=True, w_down=False)
    names = list(big)
    srcs = dict(big, conv_w=conv_w)
    started = []
    for i in range(depth):
        mix = [("a_w_qkv", i // 2), ("a_w_o", i // 2)] if i % 2 == 0 else [("b_w_qkv", i // 2), ("b_w_o", i // 2)]
        rest = [("w_up", i), ("conv_w", i), ("w_down", i)]
        stages = [mix[:1], mix[1:] + rest] if i == 0 else [mix + rest]
        started.append([])
        for s, keys in enumerate(stages):
            bufs = [_into_full(srcs[k], l, F32 if k == "conv_w" else BF16) for k, l in keys]
            started[i].append((keys,) + _gather_start(bufs, "gather_start_%d_%d" % (i, s), "chips"))
    cb3 = conv_b.reshape(depth, 1, conv_b.shape[1])

    cos, sin = _rope_tables(S)
    buckets = jnp.asarray(_bucket_tables(False))
    bias = _bias_build(rel_bias, buckets)
    bias_t = _bias_build(rel_bias, jnp.asarray(_bucket_tables(True)))

    saved = []
    passing = {}

    def land(i, s, after):
        keys, send, recv, bufs, _ = started[i][s]
        bufs = _gather_wait(send, recv, bufs, after, "gather_wait_%d_%d" % (i, s), "chips")
        send, recv, bufs, token = _gather_start(bufs, "pass_start_%d_%d" % (i, s), "sibling")
        passing[i, s] = (keys, send, recv, bufs)
        return token

    def arrive(i, s, after, wl):
        keys, send, recv, bufs = passing.pop((i, s))
        bufs = _gather_wait(send, recv, bufs, after, "pass_wait_%d_%d" % (i, s), "sibling")
        for (k, _), buf in zip(keys, bufs):
            _, _, a, b = buf.shape
            wl[k] = buf if k == "conv_w" or blocked[k] else buf.reshape(1, N_CHIPS * a, b)

    first = [land(0, 0, [h])]
    for i in range(depth):
        j = i // 2
        wl = {}
        arrive(i, 0, [h], wl)
        sv = dict(h0=h, w=wl)
        hn = _rms_fwd(h, mix_norm[i:i + 1], after=[st[4] for layer in started for st in layer] + first if i == 0 else ())
        sv["hn"] = hn
        if i % 2 == 0:
            qkv = _mm_nn(hn, wl["a_w_qkv"], 0, blocked=True, name="a_qkv")
            qkvh = _prep_a_fwd(qkv, cos, sin, a_q_gain[j:j + 1], a_k_gain[j:j + 1])
            two = len(started[i]) > 1
            o, lse = _flash_a_fwd(qkvh, after=[land(i, 1, [qkvh])] if two else ())
            if two:
                arrive(i, 1, [o], wl)
            sv.update(qkv=qkv, qkvh=qkvh, o=o, lse=lse)
            h = _mm_nn(o, wl["a_w_o"], 0, blocked=False, res=h, name="a_out")
        else:
            qkvp = [_mm_nn_perm(hn, wl["b_w_qkv"], g) for g in range(G)]
            os_, lzs = [], []
            for g in range(G):
                o_g, lz_g = _battn_fwd(qkvp[g], bias, g)
                os_.append(o_g)
                lzs.append(lz_g)
            y = _combine_fwd(os_, lzs)
            sv.update(qkvp=qkvp, os=os_, lzs=lzs, y=y)
            h = _mm_nn(y, wl["b_w_o"], 0, blocked=False, res=h, name="b_out")
        sv["h1"] = h
        hf = _rms_fwd(h, ffn_norm[i:i + 1])
        u = _mm_nn(hf, wl["w_up"], 0, blocked=True, name="ffn_up")
        act = _ffn_act_fwd(u, wl["conv_w"], cb3[i:i + 1], 0)
        sv.update(hf=hf, u=u, act=act)
        nxt = [land(i + 1, 0, [act])] if i + 1 < depth else ()
        h = _mm_nn(act, wl["w_down"], 0, blocked=False, res=h, name="ffn_down", after=nxt)
        saved.append(sv)

    loss_blk, dh, dh_b, dg_final = _final_loss(h, final_norm.reshape(1, D), target)

    dws = {k: [None] * big[k].shape[0] for k in names}
    d_mix, d_ffn, d_convw, d_convb = [None] * depth, [None] * depth, [None] * depth, [None] * depth
    d_gq, d_gk = [None] * n_a, [None] * n_a
    dbias_list = []
    pending = []

    def start_reduce(keys, tag, after=()):
        pieces = [_split8(dws[k][l], blocked[k]) for k, l in keys]
        send, recv, pieces, lands, token = _reduce_start(pieces, "reduce_start_" + tag, after)
        pending.append((keys, send, recv, pieces, lands, tag))
        return (token,)

    tok = ()
    for i in reversed(range(depth)):
        j = i // 2
        sv = saved[i]
        wl = sv["w"]
        da = _mm_nt(dh_b, wl["w_down"], 0, blocked=False, name="ffn_down_dx", after=tok)
        dws["w_down"][i] = _mm_tn(sv["act"], dh_b, blocked=False, name="ffn_down_dw")
        du, dconv = _ffn_act_bwd(sv["u"], da, wl["conv_w"], cb3[i:i + 1], 0)
        d_convw[i], d_convb[i] = dconv[0:3], dconv[3]
        dhf = _mm_nt(du, wl["w_up"], 0, blocked=True, name="ffn_up_dx")
        dws["w_up"][i] = _mm_tn(sv["hf"], du, blocked=True, name="ffn_up_dw")
        dh, dh_b, dg = _rms_bwd(dhf, sv["h1"], ffn_norm[i:i + 1], dh)
        d_ffn[i] = dg[0]
        tok = start_reduce([("w_down", i), ("w_up", i)], "ffn%d" % i)
        if i % 2 == 0:
            do = _mm_nt(dh_b, wl["a_w_o"], 0, blocked=False, name="a_out_dx", after=tok)
            dws["a_w_o"][j] = _mm_tn(sv["o"], dh_b, blocked=False, name="a_out_dw")
            dq, dk, dv = _flash_a_bwd(sv["qkvh"], do, sv["o"], sv["lse"])
            dqkv, dgain = _prep_a_bwd(dq, dk, dv, sv["qkv"], cos, sin, a_q_gain[j:j + 1], a_k_gain[j:j + 1])
            d_gq[j], d_gk[j] = dgain[0], dgain[1]
            dhn = _mm_nt(dqkv, wl["a_w_qkv"], 0, blocked=True, name="a_qkv_dx")
            dws["a_w_qkv"][j] = _mm_tn(sv["hn"], dqkv, blocked=True, name="a_qkv_dw")
            mix_keys = [("a_w_o", j), ("a_w_qkv", j)]
        else:
            dy = _mm_nt(dh_b, wl["b_w_o"], 0, blocked=False, name="b_out_dx", after=tok)
            dws["b_w_o"][j] = _mm_tn(sv["y"], dh_b, blocked=False, name="b_out_dw")
            dos, dlzs = _combine_bwd(dy, sv["os"], sv["lzs"])
            parts = []
            for g in range(G):
                dq, rt, db = _battn_bwd_dq(sv["qkvp"][g], bias, dos[g], sv["os"][g], sv["lzs"][g], dlzs[g], g)
                dk, dv = _battn_bwd_dkv(sv["qkvp"][g], bias_t, dos[g], sv["lzs"][g], rt, g)
                parts += [dq, dk, dv]
                dbias_list.append((g, db))
            dqkv = _concat_cast(parts, [d for _, d in B_GROUPS for _ in range(3)])
            dhn = _mm_nt(dqkv, wl["b_w_qkv"], 0, blocked=True, name="b_qkv_dx")
            dws["b_w_qkv"][j] = _mm_tn(sv["hn"], dqkv, blocked=True, name="b_qkv_dw")
            mix_keys = [("b_w_o", j), ("b_w_qkv", j)]
        dh, dh_b, dg = _rms_bwd(dhn, sv["h0"], mix_norm[i:i + 1], dh)
        d_mix[i] = dg[0]
        if i > 0:
            tok = start_reduce(mix_keys, "mix%d" % i)
    grad_x = dh.reshape(x.shape)

    dbias_layers = [jnp.stack([db for g2, db in dbias_list[l * G:(l + 1) * G]]) for l in range(n_b)]
    d_rel = _bias_reduce(dbias_layers, buckets)[:, :G * hg]

    small = [jnp.stack(d_gq), jnp.stack(d_gk), d_rel, jnp.stack(d_mix), jnp.stack(d_ffn), jnp.stack(d_convw),
             jnp.stack(d_convb), dg_final[0]]
    sizes = [int(np.prod(s.shape)) for s in small]
    flat = jnp.concatenate([s.reshape(-1) for s in small])
    rows = -(-flat.shape[0] // (LANES * SUBLANES)) * SUBLANES
    flat = jnp.pad(flat, (0, rows * LANES - flat.shape[0])).reshape(rows, LANES)
    tot = _allreduce_small(flat)
    start_reduce(mix_keys, "mix0", after=[tot])
    tot = tot.reshape(-1)
    offs = np.cumsum([0] + sizes)
    g_gq, g_gk, g_rel, g_mix, g_ffn, g_convw_full, g_convb, g_final = [
        tot[offs[k]:offs[k + 1]].reshape(small[k].shape) for k in range(len(small))]
    cq = conv_w.shape[2]
    g_convw = lax.dynamic_slice_in_dim(g_convw_full, (2 * cx + cy) * cq, cq, axis=2)

    grads = dict(a_q_gain=g_gq, a_k_gain=g_gk, rel_bias=g_rel, mix_norm=g_mix, ffn_norm=g_ffn,
                 conv_w=g_convw, conv_b=g_convb, final_norm=g_final)
    weights = dict(a_w_qkv=a_w_qkv, a_w_o=a_w_o, a_q_gain=a_q_gain, a_k_gain=a_k_gain, b_w_qkv=b_w_qkv, b_w_o=b_w_o,
                   rel_bias=rel_bias, mix_norm=mix_norm, ffn_norm=ffn_norm, w_up=w_up, conv_w=conv_w, conv_b=conv_b,
                   w_down=w_down, final_norm=final_norm)
    ms = dict(a_w_qkv=m_a_w_qkv, a_w_o=m_a_w_o, a_q_gain=m_a_q_gain, a_k_gain=m_a_k_gain, b_w_qkv=m_b_w_qkv,
              b_w_o=m_b_w_o, rel_bias=m_rel_bias, mix_norm=m_mix_norm, ffn_norm=m_ffn_norm, w_up=m_w_up,
              conv_w=m_conv_w, conv_b=m_conv_b, w_down=m_w_down, final_norm=m_final_norm)
    vs = dict(a_w_qkv=v_a_w_qkv, a_w_o=v_a_w_o, a_q_gain=v_a_q_gain, a_k_gain=v_a_k_gain, b_w_qkv=v_b_w_qkv,
              b_w_o=v_b_w_o, rel_bias=v_rel_bias, mix_norm=v_mix_norm, ffn_norm=v_ffn_norm, w_up=v_w_up,
              conv_w=v_conv_w, conv_b=v_conv_b, w_down=v_w_down, final_norm=v_final_norm)
    deltas, new_m, new_v, stacks = {}, {}, {}, {}

    def update(k):
        w = weights[k]
        two_d = (-1, w.shape[-1])
        d, nm, nv = _adamw(w.reshape(two_d), grads[k].reshape(two_d), ms[k].reshape(two_d), vs[k].reshape(two_d))
        deltas[k], new_m[k], new_v[k] = d.reshape(w.shape), nm.reshape(w.shape), nv.reshape(w.shape)
        return nv

    def collect(items, after):
        for keys, send, recv, pieces, lands, tag in items:
            pieces, lands = _reduce_wait(send, recv, pieces, lands, after, "reduce_wait_" + tag)
            for (k, l), p, land in zip(keys, pieces, lands):
                if k not in stacks:
                    stacks[k] = lax.empty((big[k].shape[0], 2) + p.shape[2:], F32)
                stacks[k] = _add_pieces(p, land, stacks[k], l)

    def share(ks):
        for k, gs in zip(ks, _ag_sibling([stacks[k] for k in ks])):
            grads[k] = gs.reshape(big[k].shape)

    late = [k for k in names if k in {kk for kk, _ in pending[-1][0]}]
    collect(pending[:-1], [dh])
    share([k for k in names if k not in late])
    done = [update(k) for k in weights if k not in late]
    collect(pending[-1:], done)
    share(late)
    for k in late:
        update(k)

    loss = lax.psum(loss_blk[0, 0], ("x", "y", "c"))
    keys = list(weights)
    return (loss, grad_x, *[grads[k].reshape(weights[k].shape) for k in keys], *[deltas[k] for k in keys],
            *[new_m[k] for k in keys], *[new_v[k] for k in keys])
```

```python
import functools
import math

import numpy as np
import jax
import jax.numpy as jnp
from jax import lax
from jax.experimental import pallas as pl
from jax.experimental.pallas import tpu as pltpu

F32 = jnp.float32
BF16 = jnp.bfloat16

HEAD_DIM = 128
A_HEADS = 16
A_KV_HEADS = 4
GRID_W = 64
ROPE_THETA = 10000.0
B_GROUPS = ((128, 1), (512, 4), (2048, 16))
B_HEADS_PER_GROUP = 8
REL_BUCKETS = 32
REL_MAX_DISTANCE = 1024
EPS = 1e-6
NEG_INF = -1e30
DEPTH = 4
ADAM_LR = 0.001
ADAM_B1 = 0.9
ADAM_B2 = 0.999
ADAM_EPS = 1e-08
ADAM_WD = 0.01
ADAM_STEP = 10

N_CHIPS = 4
LANES = 128
SUBLANES = 8
VMEM_LIMIT = 52 * 1024 * 1024
MESH = pl.DeviceIdType.MESH


def _pick(n, cands):
    for c in cands:
        if c <= n and n % c == 0:
            return c
    return n


def _lane_tile(n, cap):
    best = None
    for t in range(LANES, min(n, cap) + 1, LANES):
        if n % t == 0:
            best = t
    return best or n


def _cp(*sem):
    return pltpu.CompilerParams(dimension_semantics=sem if sem else None, vmem_limit_bytes=VMEM_LIMIT)


def _half_span():
    hs = {w // (2 * d) for w, d in B_GROUPS}
    assert len(hs) == 1
    return hs.pop()


def _rms_fwd(h, gain, after=()):
    S, D = h.shape
    ts = _pick(S, (512, 256, 128, 64, 32, 16))

    def body(h_ref, g_ref, *rest):
        o_ref = rest[-1]
        x = h_ref[...]
        r = lax.rsqrt(jnp.mean(x * x, axis=-1, keepdims=True) + EPS)
        o_ref[...] = (x * r * g_ref[...]).astype(o_ref.dtype)

    return pl.pallas_call(
        body, name="rms_fwd", grid=(S // ts,),
        in_specs=[pl.BlockSpec((ts, D), lambda i: (i, 0)), pl.BlockSpec((1, D), lambda i: (0, 0))]
        + [pl.BlockSpec(memory_space=pl.ANY)] * len(after),
        out_specs=pl.BlockSpec((ts, D), lambda i: (i, 0)),
        out_shape=jax.ShapeDtypeStruct((S, D), BF16), compiler_params=_cp("arbitrary"))(h, gain, *after)


def _rms_bwd(dy, h, gain, dres):
    S, D = h.shape
    ts = _pick(S, (256, 128, 64, 32, 16))

    def body(dy_ref, h_ref, g_ref, dres_ref, dh_ref, dhb_ref, dg_ref):
        @pl.when(pl.program_id(0) == 0)
        def _():
            dg_ref[...] = jnp.zeros_like(dg_ref)
        x = h_ref[...]
        dy = dy_ref[...]
        r = lax.rsqrt(jnp.mean(x * x, axis=-1, keepdims=True) + EPS)
        xn = x * r
        dg_ref[0:1, :] += jnp.sum(dy * xn, axis=0, keepdims=True)
        dxn = dy * g_ref[...]
        dx = r * (dxn - xn * jnp.mean(dxn * xn, axis=-1, keepdims=True))
        dh = dres_ref[...] + dx
        dh_ref[...] = dh
        dhb_ref[...] = dh.astype(BF16)

    row = pl.BlockSpec((ts, D), lambda i: (i, 0))
    return pl.pallas_call(
        body, name="rms_bwd", grid=(S // ts,),
        in_specs=[row, row, pl.BlockSpec((1, D), lambda i: (0, 0)), row],
        out_specs=[row, row, pl.BlockSpec((SUBLANES, D), lambda i: (0, 0))],
        out_shape=[jax.ShapeDtypeStruct((S, D), F32), jax.ShapeDtypeStruct((S, D), BF16),
                   jax.ShapeDtypeStruct((SUBLANES, D), F32)],
        compiler_params=_cp("arbitrary"))(dy, h, gain, dres)


def _final_loss(h, gain, target):
    S, D = h.shape
    ts = _pick(S, (256, 128, 64, 32, 16))

    def body(h_ref, g_ref, t_ref, loss_ref, dh_ref, dhb_ref, dg_ref):
        @pl.when(pl.program_id(0) == 0)
        def _():
            dg_ref[...] = jnp.zeros_like(dg_ref)
            loss_ref[...] = jnp.zeros_like(loss_ref)
        x = h_ref[...]
        g = g_ref[...]
        r = lax.rsqrt(jnp.mean(x * x, axis=-1, keepdims=True) + EPS)
        xn = x * r
        err = xn * g - t_ref[...]
        part = 0.5 * jnp.sum(jnp.mean(err * err, axis=-1, keepdims=True), axis=0, keepdims=True)
        loss_ref[0:1, 0:1] += part
        dy = err * (1.0 / D)
        dg_ref[0:1, :] += jnp.sum(dy * xn, axis=0, keepdims=True)
        dxn = dy * g
        dh = r * (dxn - xn * jnp.mean(dxn * xn, axis=-1, keepdims=True))
        dh_ref[...] = dh
        dhb_ref[...] = dh.astype(BF16)

    row = pl.BlockSpec((ts, D), lambda i: (i, 0))
    return pl.pallas_call(
        body, name="final_loss", grid=(S // ts,),
        in_specs=[row, pl.BlockSpec((1, D), lambda i: (0, 0)), row],
        out_specs=[pl.BlockSpec((SUBLANES, LANES), lambda i: (0, 0)), row, row,
                   pl.BlockSpec((SUBLANES, D), lambda i: (0, 0))],
        out_shape=[jax.ShapeDtypeStruct((SUBLANES, LANES), F32), jax.ShapeDtypeStruct((S, D), F32),
                   jax.ShapeDtypeStruct((S, D), BF16), jax.ShapeDtypeStruct((SUBLANES, D), F32)],
        compiler_params=_cp("arbitrary"))(h, gain, target)


_NN = (((1,), (0,)), ((), ()))
_NT = (((1,), (1,)), ((), ()))
_TN = (((0,), (0,)), ((), ()))


def _mm_nn(a, w, layer, *, blocked, out_dtype=F32, res=None, name, after=()):
    M, K = a.shape
    if blocked:
        nq = w.shape[3]
        N = N_CHIPS * nq
        tn = _lane_tile(nq, 1408)
        nps = nq // tn
        w_spec = pl.BlockSpec((None, None, K, tn), lambda i, j: (layer, j // nps, 0, j % nps))
    else:
        N = w.shape[2]
        tn = _lane_tile(N, 512)
        w_spec = pl.BlockSpec((None, K, tn), lambda i, j: (layer, 0, j))
    tm = _pick(M, (1024, 512, 256, 128, 64, 32, 16)) if K <= 3072 else _pick(M, (512, 256, 128, 64, 32, 16))

    def body(*refs):
        a_ref, w_ref, o_ref = refs[0], refs[1], refs[-1]
        acc = lax.dot_general(a_ref[...], w_ref[...], _NN, preferred_element_type=F32)
        if res is not None:
            acc = refs[2][...] + acc
        o_ref[...] = acc.astype(o_ref.dtype)

    in_specs = [pl.BlockSpec((tm, K), lambda i, j: (i, 0)), w_spec]
    args = [a, w]
    if res is not None:
        in_specs.append(pl.BlockSpec((tm, tn), lambda i, j: (i, j)))
        args.append(res)
    return pl.pallas_call(
        body, name=name, grid=(M // tm, N // tn), in_specs=in_specs + [pl.BlockSpec(memory_space=pl.ANY)] * len(after),
        out_specs=pl.BlockSpec((tm, tn), lambda i, j: (i, j)),
        out_shape=jax.ShapeDtypeStruct((M, N), out_dtype),
        compiler_params=_cp("arbitrary", "arbitrary"))(*args, *after)


def _mm_nt(a, w, layer, *, blocked, name, after=()):
    pair = isinstance(a, tuple)
    M = a[0].shape[0] if pair else a.shape[0]
    tm = _pick(M, (1024, 512, 256, 128, 64, 32, 16))
    if blocked:
        K, nq = w.shape[2], w.shape[3]
        tk = _pick(K, (1024, 512, 256, 128))
        half = N_CHIPS // 2

        def body(*refs):
            a_refs, (w_ref, o_ref, acc_ref) = refs[:-3], refs[-3:]
            p = pl.program_id(2)

            @pl.when(p == 0)
            def _():
                acc_ref[...] = jnp.zeros_like(acc_ref)
            if pair:
                @pl.when(p < half)
                def _():
                    acc_ref[...] += lax.dot_general(a_refs[0][...], w_ref[...], _NT, preferred_element_type=F32)

                @pl.when(p >= half)
                def _():
                    acc_ref[...] += lax.dot_general(a_refs[1][...], w_ref[...], _NT, preferred_element_type=F32)
            else:
                acc_ref[...] += lax.dot_general(a_refs[0][...], w_ref[...], _NT, preferred_element_type=F32)

            @pl.when(p == N_CHIPS - 1)
            def _():
                o_ref[...] = acc_ref[...]

        if pair:
            a_specs = [pl.BlockSpec((tm, nq), lambda i, j, p: (i, jnp.minimum(p, half - 1))),
                       pl.BlockSpec((tm, nq), lambda i, j, p: (i, jnp.maximum(p - half, 0)))]
            a_args = list(a)
        else:
            a_specs = [pl.BlockSpec((tm, nq), lambda i, j, p: (i, p))]
            a_args = [a]
        return pl.pallas_call(
            body, name=name, grid=(M // tm, K // tk, N_CHIPS),
            in_specs=a_specs + [pl.BlockSpec((None, None, tk, nq), lambda i, j, p: (layer, p, j, 0))],
            out_specs=pl.BlockSpec((tm, tk), lambda i, j, p: (i, j)),
            out_shape=jax.ShapeDtypeStruct((M, K), F32),
            scratch_shapes=[pltpu.VMEM((tm, tk), F32)],
            compiler_params=_cp("arbitrary", "arbitrary", "arbitrary"))(*a_args, w)
    K, N = w.shape[1], w.shape[2]
    tk = _pick(K, (1024, 512, 256, 128))

    def body(a_ref, w_ref, *rest):
        rest[-1][...] = lax.dot_general(a_ref[...], w_ref[...], _NT, preferred_element_type=F32)

    return pl.pallas_call(
        body, name=name, grid=(M // tm, K // tk),
        in_specs=[pl.BlockSpec((tm, N), lambda i, j: (i, 0)),
                  pl.BlockSpec((None, tk, N), lambda i, j: (layer, j, 0))]
        + [pl.BlockSpec(memory_space=pl.ANY)] * len(after),
        out_specs=pl.BlockSpec((tm, tk), lambda i, j: (i, j)),
        out_shape=jax.ShapeDtypeStruct((M, K), F32),
        compiler_params=_cp("arbitrary", "arbitrary"))(a, w, *after)


def _mm_tn(x, dy, *, blocked, name):
    pair = isinstance(dy, tuple)
    S, K = x.shape
    N = 2 * dy[0].shape[1] if pair else dy.shape[1]
    tk = _pick(K, (512, 256, 128))
    if blocked:
        nq = N // N_CHIPS
        tn = _lane_tile(nq, 1408)
        nps = nq // tn
        out_spec = pl.BlockSpec((None, tk, tn), lambda i, j, s: (j // nps, i, j % nps))
        out_shape = jax.ShapeDtypeStruct((N_CHIPS, K, nq), BF16)
    else:
        tn = _lane_tile(N, 1024)
        out_spec = pl.BlockSpec((tk, tn), lambda i, j, s: (i, j))
        out_shape = jax.ShapeDtypeStruct((K, N), BF16)
    nj = N // tn
    njh = nj // 2
    ns = 2 if pair else 1
    sh = S // ns

    def body(x_ref, *refs):
        o_ref, acc_ref = refs[-2], refs[-1]

        def product(dy_ref):
            part = lax.dot_general(x_ref[...], dy_ref[...], _TN, preferred_element_type=F32)
            if ns == 1:
                o_ref[...] = part.astype(o_ref.dtype)
            else:
                s = pl.program_id(2)

                @pl.when(s == 0)
                def _():
                    acc_ref[...] = part

                @pl.when(s == ns - 1)
                def _():
                    o_ref[...] = (acc_ref[...] + part).astype(o_ref.dtype)

        if pair:
            j = pl.program_id(1)
            pl.when(j < njh)(lambda: product(refs[0]))
            pl.when(j >= njh)(lambda: product(refs[1]))
        else:
            product(refs[0])

    if pair:
        assert nj % 2 == 0
        dy_specs = [pl.BlockSpec((sh, tn), lambda i, j, s: (jnp.where(j < njh, s, ns - 1), jnp.minimum(j, njh - 1))),
                    pl.BlockSpec((sh, tn), lambda i, j, s: (jnp.where(j < njh, 0, s), jnp.maximum(j - njh, 0)))]
        dy_args = list(dy)
    else:
        dy_specs = [pl.BlockSpec((sh, tn), lambda i, j, s: (s, j))]
        dy_args = [dy]
    return pl.pallas_call(
        body, name=name, grid=(K // tk, nj, ns),
        in_specs=[pl.BlockSpec((sh, tk), lambda i, j, s: (s, i))] + dy_specs,
        out_specs=out_spec, out_shape=out_shape,
        scratch_shapes=[pltpu.VMEM((tk, tn) if ns > 1 else (SUBLANES, LANES), F32)],
        compiler_params=_cp("arbitrary", "arbitrary", "arbitrary"))(x, *dy_args)


def _rope_tables(S):
    rows = S // GRID_W
    row_ids = jnp.repeat(jnp.arange(rows, dtype=F32), GRID_W)
    col_ids = jnp.tile(jnp.arange(GRID_W, dtype=F32), rows)
    quarter = HEAD_DIM // 4
    inv_freq = ROPE_THETA ** (-jnp.arange(quarter, dtype=F32) / quarter)
    ang_r = row_ids[:, None] * inv_freq[None, :]
    ang_c = col_ids[:, None] * inv_freq[None, :]
    cos = jnp.concatenate([jnp.cos(ang_r)] * 2 + [jnp.cos(ang_c)] * 2, axis=-1)
    sin = jnp.concatenate([-jnp.sin(ang_r), jnp.sin(ang_r), -jnp.sin(ang_c), jnp.sin(ang_c)], axis=-1)
    return cos, sin


def _swap_quarters(x):
    lane = lax.broadcasted_iota(jnp.int32, x.shape, 1)
    first = (lane % (HEAD_DIM // 2)) < (HEAD_DIM // 4)
    return jnp.where(first, pltpu.roll(x, HEAD_DIM - HEAD_DIM // 4, 1), pltpu.roll(x, HEAD_DIM // 4, 1))


A_SCALE = HEAD_DIM ** -0.5
A_QSCALE = A_SCALE * math.log2(math.e)


def _prep_a_fwd(qkv, cos, sin, gq, gk):
    S, W = qkv.shape
    nrm = A_HEADS + A_KV_HEADS
    ts = _pick(S, (256, 128, 64, 32, 16))

    def body(qkv_ref, cos_ref, sin_ref, gq_ref, gk_ref, o_ref):
        cos_t = cos_ref[...]
        sin_t = sin_ref[...]
        for j in range(nrm):
            sl = slice(j * HEAD_DIM, (j + 1) * HEAD_DIM)
            x = qkv_ref[:, sl]
            g = gq_ref[...] if j < A_HEADS else gk_ref[...]
            r = lax.rsqrt(jnp.mean(x * x, axis=-1, keepdims=True) + EPS)
            n = x * r * g
            y = n * cos_t + _swap_quarters(n) * sin_t
            o_ref[:, sl] = (y * A_QSCALE if j < A_HEADS else y).astype(BF16)
        o_ref[:, nrm * HEAD_DIM:] = qkv_ref[:, nrm * HEAD_DIM:].astype(BF16)

    row = lambda w: pl.BlockSpec((ts, w), lambda i: (i, 0))
    one = pl.BlockSpec((1, HEAD_DIM), lambda i: (0, 0))
    return pl.pallas_call(
        body, name="prep_a_fwd", grid=(S // ts,),
        in_specs=[row(W), row(HEAD_DIM), row(HEAD_DIM), one, one], out_specs=row(W),
        out_shape=jax.ShapeDtypeStruct((S, W), BF16), compiler_params=_cp("arbitrary"))(qkv, cos, sin, gq, gk)


def _prep_a_bwd(dq, dk, dv, qkv, cos, sin, gq, gk):
    S, W = qkv.shape
    nrm = A_HEADS + A_KV_HEADS
    nq, nk = A_HEADS * HEAD_DIM, A_KV_HEADS * HEAD_DIM
    ts = _pick(S, (256, 128, 64, 32, 16))

    def body(dq_ref, dk_ref, dv_ref, qkv_ref, cos_ref, sin_ref, gq_ref, gk_ref, o_ref, dg_ref):
        @pl.when(pl.program_id(0) == 0)
        def _():
            dg_ref[...] = jnp.zeros_like(dg_ref)
        cos_t = cos_ref[...]
        sin_t = sin_ref[...]
        for j in range(nrm):
            sl = slice(j * HEAD_DIM, (j + 1) * HEAD_DIM)
            x = qkv_ref[:, sl]
            if j < A_HEADS:
                dy, g, grow = dq_ref[:, sl], gq_ref[...], 0
            else:
                jj = j - A_HEADS
                dy, g, grow = dk_ref[:, jj * HEAD_DIM:(jj + 1) * HEAD_DIM], gk_ref[...], 1
            r = lax.rsqrt(jnp.mean(x * x, axis=-1, keepdims=True) + EPS)
            xn = x * r
            dn = dy * cos_t + _swap_quarters(dy * sin_t)
            dg_ref[grow:grow + 1, :] += jnp.sum(dn * xn, axis=0, keepdims=True)
            dxn = dn * g
            o_ref[:, sl] = (r * (dxn - xn * jnp.mean(dxn * xn, axis=-1, keepdims=True))).astype(BF16)
        o_ref[:, nrm * HEAD_DIM:] = dv_ref[...].astype(BF16)

    row = lambda w: pl.BlockSpec((ts, w), lambda i: (i, 0))
    one = pl.BlockSpec((1, HEAD_DIM), lambda i: (0, 0))
    return pl.pallas_call(
        body, name="prep_a_bwd", grid=(S // ts,),
        in_specs=[row(nq), row(nk), row(nk), row(W), row(HEAD_DIM), row(HEAD_DIM), one, one],
        out_specs=[row(W), pl.BlockSpec((SUBLANES, HEAD_DIM), lambda i: (0, 0))],
        out_shape=[jax.ShapeDtypeStruct((S, W), BF16), jax.ShapeDtypeStruct((SUBLANES, HEAD_DIM), F32)],
        compiler_params=_cp("arbitrary"))(dq, dk, dv, qkv, cos, sin, gq, gk)


def _flash_a_fwd(qkvh, after=()):
    S = qkvh.shape[0]
    grp = A_HEADS // A_KV_HEADS
    tq = _pick(S, (256, 128, 64, 32, 16))
    kc = _pick(S, (512, 256, 128))
    lanes = [slice(b * LANES, (b + 1) * LANES) for b in range(kc // LANES)]

    def body(q_ref, k_ref, v_ref, *rest):
        o_ref, lse_ref = rest[-2], rest[-1]
        q = q_ref[...]
        m_t = jnp.full((tq, LANES), -jnp.inf, F32)
        for c in range(S // kc):
            s = lax.dot_general(q, k_ref[c * kc:(c + 1) * kc, :], _NT, preferred_element_type=F32)
            for sl in lanes:
                m_t = jnp.maximum(m_t, s[:, sl])
        m = jnp.max(m_t, axis=-1, keepdims=True)
        l_t = jnp.zeros((tq, LANES), F32)
        acc = jnp.zeros((tq, HEAD_DIM), F32)
        for c in range(S // kc):
            rows = slice(c * kc, (c + 1) * kc)
            p = jnp.exp2(lax.dot_general(q, k_ref[rows, :], _NT, preferred_element_type=F32) - m)
            for sl in lanes:
                l_t = l_t + p[:, sl]
            acc = acc + lax.dot_general(p.astype(BF16), v_ref[rows, :], _NN, preferred_element_type=F32)
        l = jnp.sum(l_t, axis=-1, keepdims=True)
        o_ref[...] = (acc * (1.0 / l)).astype(BF16)
        lse_ref[...] = jnp.broadcast_to(m + jnp.log2(l), lse_ref.shape)

    qs = pl.BlockSpec((tq, HEAD_DIM), lambda h, i: (i, h))
    return pl.pallas_call(
        body, name="flash_a_fwd", grid=(A_HEADS, S // tq),
        in_specs=[qs,
                  pl.BlockSpec((S, HEAD_DIM), lambda h, i: (0, A_HEADS + h // grp)),
                  pl.BlockSpec((S, HEAD_DIM), lambda h, i: (0, A_HEADS + A_KV_HEADS + h // grp))]
        + [pl.BlockSpec(memory_space=pl.ANY)] * len(after),
        out_specs=[qs, qs],
        out_shape=[jax.ShapeDtypeStruct((S, A_HEADS * HEAD_DIM), BF16),
                   jax.ShapeDtypeStruct((S, A_HEADS * HEAD_DIM), F32)],
        compiler_params=_cp("arbitrary", "arbitrary"))(qkvh, qkvh, qkvh, *after)


def _flash_a_bwd(qkvh, do, o, lse):
    S = qkvh.shape[0]
    grp = A_HEADS // A_KV_HEADS
    tq = _pick(S, (256, 128, 64, 32, 16))
    nq = S // tq

    def body(q_ref, k_ref, v_ref, do_ref, o_ref, lse_ref, dq_ref, dk_ref, dv_ref):
        g, i = pl.program_id(1), pl.program_id(2)

        @pl.when((g == 0) & (i == 0))
        def _():
            dk_ref[...] = jnp.zeros_like(dk_ref)
            dv_ref[...] = jnp.zeros_like(dv_ref)
        q = q_ref[...]
        k = k_ref[...]
        do_f = do_ref[...]
        do_b = do_f.astype(BF16)
        delta = jnp.sum(do_f * o_ref[...].astype(F32), axis=-1, keepdims=True)
        p = jnp.exp2(lax.dot_general(q, k, _NT, preferred_element_type=F32) - lse_ref[:, 0:1])
        dp = lax.dot_general(do_b, v_ref[...], _NT, preferred_element_type=F32)
        ds_b = (p * (dp - delta)).astype(BF16)
        dq_ref[...] = lax.dot_general(ds_b, k, _NN, preferred_element_type=F32) * A_SCALE
        dk_ref[...] += lax.dot_general(ds_b, q, _TN, preferred_element_type=F32)
        dv_ref[...] += lax.dot_general(p.astype(BF16), do_b, _TN, preferred_element_type=F32)

        @pl.when((g == grp - 1) & (i == nq - 1))
        def _():
            dk_ref[...] = dk_ref[...] * (A_SCALE / A_QSCALE)

    qs = pl.BlockSpec((tq, HEAD_DIM), lambda kv, g, i: (i, kv * grp + g))
    kvs = lambda off: pl.BlockSpec((S, HEAD_DIM), lambda kv, g, i: (0, off + kv))
    return pl.pallas_call(
        body, name="flash_a_bwd", grid=(A_KV_HEADS, grp, S // tq),
        in_specs=[qs, kvs(A_HEADS), kvs(A_HEADS + A_KV_HEADS), qs, qs, qs],
        out_specs=[qs, kvs(0), kvs(0)],
        out_shape=[jax.ShapeDtypeStruct((S, A_HEADS * HEAD_DIM), F32),
                   jax.ShapeDtypeStruct((S, A_KV_HEADS * HEAD_DIM), F32),
                   jax.ShapeDtypeStruct((S, A_KV_HEADS * HEAD_DIM), F32)],
        compiler_params=_cp("arbitrary", "arbitrary", "arbitrary"))(qkvh, qkvh, qkvh, do, o, lse)


def _bucket_tables(transposed):
    hs = _half_span()
    tq, kv = 2 * hs, 4 * hs
    nb = REL_BUCKETS // 2
    max_exact = nb // 2
    shape = (kv, tq) if transposed else (tq, kv)
    out = np.zeros((len(B_GROUPS), 3) + shape, np.int32)
    win = np.arange(kv) - hs
    blk = np.arange(tq)
    for g, (_, dil) in enumerate(B_GROUPS):
        for case in range(3):
            inside = ((win >= 0) | (case != 0)) & ((win < tq) | (case != 2))
            if transposed:
                rel = blk[None, :] - win[:, None]
                ok = inside[:, None]
            else:
                rel = win[None, :] - blk[:, None]
                ok = inside[None, :]
            r = rel * dil
            n = np.abs(r)
            nf = np.maximum(n, 1).astype(np.float32)
            large = max_exact + (np.log(nf / np.float32(max_exact)) / np.float32(math.log(REL_MAX_DISTANCE / max_exact))
                                 * np.float32(nb - max_exact)).astype(np.int32)
            large = np.minimum(large, nb - 1)
            bucket = np.where(r > 0, nb, 0) + np.where(n < max_exact, n, large)
            out[g, case] = np.where((np.abs(rel) <= hs) & ok, bucket, -1)
    return out


def _bias_build(rel_bias, buckets):
    G, _, tq, kv = buckets.shape
    hg = B_HEADS_PER_GROUP

    def body(rb_ref, bk_ref, o_ref):
        col = pl.program_id(0) * hg + pl.program_id(2)
        bk = bk_ref[...]
        acc = jnp.full((tq, kv), NEG_INF, F32)
        for b in range(REL_BUCKETS):
            acc = jnp.where(bk == b, rb_ref[b, col], acc)
        o_ref[...] = acc

    return pl.pallas_call(
        body, name="bias_build", grid=(G, 3, hg),
        in_specs=[pl.BlockSpec(memory_space=pltpu.SMEM),
                  pl.BlockSpec((None, None, tq, kv), lambda g, c, h: (g, c, 0, 0))],
        out_specs=pl.BlockSpec((None, None, None, tq, kv), lambda g, c, h: (g, c, h, 0, 0)),
        out_shape=jax.ShapeDtypeStruct((G, 3, hg, tq, kv), F32),
        compiler_params=_cp("arbitrary", "arbitrary", "arbitrary"))(rel_bias, buckets)


def _bias_reduce(dbias_list, buckets):
    G, _, tq, kv = buckets.shape
    hg = B_HEADS_PER_GROUP
    n = len(dbias_list)

    def body(*refs):
        bk_ref, o_ref = refs[n], refs[n + 1]
        first = (pl.program_id(0) == 0) & (pl.program_id(1) == 0) & (pl.program_id(2) == 0)

        @pl.when(first)
        def _():
            o_ref[...] = jnp.zeros_like(o_ref)
        col = pl.program_id(0) * hg + pl.program_id(2)
        db = refs[0][...]
        for r in refs[1:n]:
            db = db + r[...]
        bk = bk_ref[...]
        rows = lax.broadcasted_iota(jnp.int32, (REL_BUCKETS, LANES), 0)
        cols = lax.broadcasted_iota(jnp.int32, (REL_BUCKETS, LANES), 1)
        acc = jnp.zeros((REL_BUCKETS, LANES), F32)
        for b in range(REL_BUCKETS):
            val = jnp.sum(jnp.sum(jnp.where(bk == b, db, 0.0), axis=1, keepdims=True), axis=0, keepdims=True)
            acc = acc + jnp.where((rows == b) & (cols == col), val, 0.0)
        o_ref[...] += acc

    tile = pl.BlockSpec((None, None, None, tq, kv), lambda g, c, h: (g, c, h, 0, 0))
    return pl.pallas_call(
        body, name="bias_reduce", grid=(G, 3, hg),
        in_specs=[tile] * n + [pl.BlockSpec((None, None, tq, kv), lambda g, c, h: (g, c, 0, 0))],
        out_specs=pl.BlockSpec((REL_BUCKETS, LANES), lambda g, c, h: (0, 0)),
        out_shape=jax.ShapeDtypeStruct((REL_BUCKETS, LANES), F32),
        compiler_params=_cp("arbitrary", "arbitrary", "arbitrary"))(*dbias_list, buckets)


def _mm_nn_perm(a, w, g):
    S, K = a.shape
    nq = w.shape[3]
    dil = B_GROUPS[g][1]
    wg3 = 3 * B_HEADS_PER_GROUP * HEAD_DIM
    tn = _pick(nq, (256, 128))
    assert wg3 % tn == 0
    nps, ntile = nq // tn, wg3 // tn
    tm = _pick(S, (1024, 512, 256))
    rows = tm // dil

    def body(a_ref, w_ref, o_ref, acc_ref):
        acc = lax.dot_general(a_ref[...], w_ref[...], _NN, preferred_element_type=F32)
        if dil == 1:
            o_ref[0] = acc.astype(BF16)
        else:
            for k in range(tn // LANES):
                acc_ref[k] = acc[:, k * LANES:(k + 1) * LANES]
            for c in range(dil):
                for k in range(tn // LANES):
                    o_ref[c, :, k * LANES:(k + 1) * LANES] = acc_ref[k, pl.ds(c, rows, stride=dil), :].astype(BF16)

    def w_map(i, j):
        t = g * ntile + j
        return (0, t // nps, 0, t % nps)

    return pl.pallas_call(
        body, name="b_qkv_g%d" % g, grid=(S // tm, ntile),
        in_specs=[pl.BlockSpec((tm, K), lambda i, j: (i, 0)), pl.BlockSpec((None, None, K, tn), w_map)],
        out_specs=pl.BlockSpec((dil, rows, tn), lambda i, j: (0, i, j)),
        out_shape=jax.ShapeDtypeStruct((dil, S // dil, wg3), BF16),
        scratch_shapes=[pltpu.VMEM((tn // LANES, tm, LANES), F32)],
        compiler_params=_cp("arbitrary", "arbitrary"))(a, w)


def _window_specs(S, wg, col):
    hs = _half_span()
    tq = 2 * hs
    per = tq // hs
    return (pl.BlockSpec((tq, wg), lambda i: (i, col)),
            pl.BlockSpec((hs, wg), lambda i: (jnp.maximum(i * per - 1, 0), col)),
            pl.BlockSpec((hs, wg), lambda i: (jnp.minimum((i + 1) * per, S // hs - 1), col)))


def _window_case(i, L):
    per = L // (2 * _half_span())
    r = i % per
    return jnp.where(r == 0, 0, jnp.where(r == per - 1, 2, 1))


def _window(prev_ref, main_ref, next_ref, sl):
    return jnp.concatenate([prev_ref[:, sl], main_ref[:, sl], next_ref[:, sl]], axis=0)


def _battn_fwd(qkvp, bias, g):
    dil, L, wg3 = qkvp.shape
    S = dil * L
    hs = _half_span()
    tq, kvl = 2 * hs, 4 * hs
    hg = B_HEADS_PER_GROUP
    wg = hg * HEAD_DIM
    scale = HEAD_DIM ** -0.5
    flat = qkvp.reshape(S, wg3)

    def body(q_ref, km, kp, kn, vm, vp, vn, b_ref, o_ref, lz_ref):
        case = _window_case(pl.program_id(0), L)
        for h in range(hg):
            sl = slice(h * HEAD_DIM, (h + 1) * HEAD_DIM)
            s = lax.dot_general(q_ref[:, sl], _window(kp, km, kn, sl), _NT, preferred_element_type=F32) * scale
            s = s + b_ref[case, h]
            m = jnp.max(s, axis=-1, keepdims=True)
            p = jnp.exp(s - m)
            l = jnp.sum(p, axis=-1, keepdims=True)
            o_ref[:, sl] = lax.dot_general(p.astype(BF16), _window(vp, vm, vn, sl), _NN, preferred_element_type=F32) / l
            lz_ref[:, sl] = jnp.broadcast_to(m + jnp.log(l), (tq, HEAD_DIM))

    blk = pl.BlockSpec((tq, wg), lambda i: (i, 0))
    o, lz = pl.pallas_call(
        body, name="battn_fwd_g%d" % g, grid=(S // tq,),
        in_specs=[_window_specs(S, wg, 0)[0], *_window_specs(S, wg, 1), *_window_specs(S, wg, 2),
                  pl.BlockSpec((None, 3, hg, tq, kvl), lambda i: (g, 0, 0, 0, 0))],
        out_specs=[blk, blk], out_shape=[jax.ShapeDtypeStruct((S, wg), F32)] * 2,
        compiler_params=_cp("arbitrary"))(flat, flat, flat, flat, flat, flat, flat, bias)
    return o, lz


def _battn_bwd_dq(qkvp, bias, do, o, lz, dlz, g):
    dil, L, wg3 = qkvp.shape
    S = dil * L
    hs = _half_span()
    tq, kvl = 2 * hs, 4 * hs
    hg = B_HEADS_PER_GROUP
    wg = hg * HEAD_DIM
    scale = HEAD_DIM ** -0.5
    flat = qkvp.reshape(S, wg3)

    def body(q_ref, km, kp, kn, vm, vp, vn, b_ref, do_ref, o_ref, lz_ref, dlz_ref, dq_ref, rt_ref, db_ref):
        i = pl.program_id(0)

        @pl.when(i == 0)
        def _():
            db_ref[...] = jnp.zeros_like(db_ref)
        case = _window_case(i, L)
        for h in range(hg):
            sl = slice(h * HEAD_DIM, (h + 1) * HEAD_DIM)
            kw = _window(kp, km, kn, sl)
            do_f = do_ref[:, sl]
            s = lax.dot_general(q_ref[:, sl], kw, _NT, preferred_element_type=F32) * scale + b_ref[case, h]
            p = jnp.exp(s - lz_ref[:, sl][:, 0:1])
            dp = lax.dot_general(do_f.astype(BF16), _window(vp, vm, vn, sl), _NT, preferred_element_type=F32)
            rt = dlz_ref[:, sl][:, 0:1] - jnp.sum(do_f * o_ref[:, sl], axis=-1, keepdims=True)
            ds = p * (dp + rt)
            db_ref[case, h] += ds
            dq_ref[:, sl] = lax.dot_general((ds * scale).astype(BF16), kw, _NN, preferred_element_type=F32)
            rt_ref[:, sl] = jnp.broadcast_to(rt, (tq, HEAD_DIM))

    blk = pl.BlockSpec((tq, wg), lambda i: (i, 0))
    row = jax.ShapeDtypeStruct((S, wg), F32)
    return pl.pallas_call(
        body, name="battn_bwd_dq_g%d" % g, grid=(S // tq,),
        in_specs=[_window_specs(S, wg, 0)[0], *_window_specs(S, wg, 1), *_window_specs(S, wg, 2),
                  pl.BlockSpec((None, 3, hg, tq, kvl), lambda i: (g, 0, 0, 0, 0)), blk, blk, blk, blk],
        out_specs=[blk, blk, pl.BlockSpec((3, hg, tq, kvl), lambda i: (0, 0, 0, 0))],
        out_shape=[row, row, jax.ShapeDtypeStruct((3, hg, tq, kvl), F32)],
        compiler_params=_cp("arbitrary"))(flat, flat, flat, flat, flat, flat, flat, bias, do, o, lz, dlz)


def _battn_bwd_dkv(qkvp, bias_t, do, lz, rt, g):
    dil, L, wg3 = qkvp.shape
    S = dil * L
    hs = _half_span()
    tq, kvl = 2 * hs, 4 * hs
    hg = B_HEADS_PER_GROUP
    wg = hg * HEAD_DIM
    scale = HEAD_DIM ** -0.5
    flat = qkvp.reshape(S, wg3)

    def body(k_ref, v_ref, qm, qp, qn, dom, dop, don, lzm, lzp, lzn, rtm, rtp, rtn, b_ref, dk_ref, dv_ref):
        case = _window_case(pl.program_id(0), L)
        for h in range(hg):
            sl = slice(h * HEAD_DIM, (h + 1) * HEAD_DIM)
            qw = _window(qp, qm, qn, sl)
            dow = _window(dop, dom, don, sl).astype(BF16)
            s = lax.dot_general(qw, k_ref[:, sl], _NT, preferred_element_type=F32) * scale + b_ref[case, h]
            p = jnp.exp(s - _window(lzp, lzm, lzn, sl)[:, 0:1])
            dp = lax.dot_general(dow, v_ref[:, sl], _NT, preferred_element_type=F32)
            ds_b = (p * (dp + _window(rtp, rtm, rtn, sl)[:, 0:1]) * scale).astype(BF16)
            dk_ref[:, sl] = lax.dot_general(ds_b, qw, _TN, preferred_element_type=F32)
            dv_ref[:, sl] = lax.dot_general(p.astype(BF16), dow, _TN, preferred_element_type=F32)

    blk = pl.BlockSpec((tq, wg), lambda i: (i, 0))
    row = jax.ShapeDtypeStruct((S, wg), F32)
    return pl.pallas_call(
        body, name="battn_bwd_dkv_g%d" % g, grid=(S // tq,),
        in_specs=[_window_specs(S, wg, 1)[0], _window_specs(S, wg, 2)[0], *_window_specs(S, wg, 0),
                  *_window_specs(S, wg, 0), *_window_specs(S, wg, 0), *_window_specs(S, wg, 0),
                  pl.BlockSpec((None, 3, hg, kvl, tq), lambda i: (g, 0, 0, 0, 0))],
        out_specs=[blk, blk], out_shape=[row, row],
        compiler_params=_cp("arbitrary"))(flat, flat, flat, flat, flat, do, do, do, lz, lz, lz, rt, rt, rt, bias_t)


def _group_weights(lz_refs, h):
    z = [r[h] for r in lz_refs]
    mx = functools.reduce(jnp.maximum, z)
    e = [jnp.exp(v - mx) for v in z]
    inv = 1.0 / functools.reduce(lambda a, b: a + b, e)
    return [v * inv for v in e]


def _to_token_order(src_ref, dst_ref, dil):
    rows = src_ref.shape[1]
    for k in range(dst_ref.shape[0]):
        sl = slice(k * LANES, (k + 1) * LANES)
        if dil == 1:
            dst_ref[k] = src_ref[0, :, sl]
        else:
            for c in range(dil):
                dst_ref[k, pl.ds(c, rows, stride=dil), :] = src_ref[c, :, sl]


def _to_subsequence_order(src_ref, dst_ref, dil):
    rows = dst_ref.shape[1]
    for k in range(src_ref.shape[0]):
        sl = slice(k * LANES, (k + 1) * LANES)
        if dil == 1:
            dst_ref[0, :, sl] = src_ref[k]
        else:
            for c in range(dil):
                dst_ref[c, :, sl] = src_ref[k, pl.ds(c, rows, stride=dil), :]


def _sub_view(a, dil):
    S, w = a.shape
    return a.reshape(dil, S // dil, w)


def _sub_spec(dil, ts, w):
    return pl.BlockSpec((dil, ts // dil, w), lambda i: (0, i, 0))


def _combine_fwd(os_, lzs):
    G = len(os_)
    S, Wg = os_[0].shape
    hg = B_HEADS_PER_GROUP
    dils = [d for _, d in B_GROUPS]
    ts = _pick(S, (256, 128))

    def body(*refs):
        o_in, lz_in, y_ref = refs[:G], refs[G:2 * G], refs[2 * G]
        o_nat, lz_nat = refs[2 * G + 1:3 * G + 1], refs[3 * G + 1:4 * G + 1]
        for g in range(G):
            _to_token_order(o_in[g], o_nat[g], dils[g])
            _to_token_order(lz_in[g], lz_nat[g], dils[g])
        for h in range(hg):
            w = _group_weights(lz_nat, h)
            for g in range(G):
                y_ref[:, (g * hg + h) * HEAD_DIM:(g * hg + h + 1) * HEAD_DIM] = (w[g] * o_nat[g][h]).astype(BF16)

    specs = [_sub_spec(d, ts, Wg) for d in dils]
    return pl.pallas_call(
        body, name="combine_fwd", grid=(S // ts,), in_specs=specs + specs,
        out_specs=pl.BlockSpec((ts, G * Wg), lambda i: (i, 0)),
        out_shape=jax.ShapeDtypeStruct((S, G * Wg), BF16),
        scratch_shapes=[pltpu.VMEM((hg, ts, HEAD_DIM), F32)] * (2 * G),
        compiler_params=_cp("arbitrary"))(*[_sub_view(a, d) for a, d in zip(os_, dils)],
                                          *[_sub_view(a, d) for a, d in zip(lzs, dils)])


def _combine_bwd(dy, os_, lzs):
    G = len(os_)
    S, Wg = os_[0].shape
    hg = B_HEADS_PER_GROUP
    dils = [d for _, d in B_GROUPS]
    ts = _pick(S, (128,))

    def body(*refs):
        dy_ref, o_in, lz_in = refs[0], refs[1:1 + G], refs[1 + G:1 + 2 * G]
        do_out, dlz_out = refs[1 + 2 * G:1 + 3 * G], refs[1 + 3 * G:1 + 4 * G]
        scr = refs[1 + 4 * G:]
        o_nat, lz_nat, do_nat, dlz_nat = scr[:G], scr[G:2 * G], scr[2 * G:3 * G], scr[3 * G:4 * G]
        for g in range(G):
            _to_token_order(o_in[g], o_nat[g], dils[g])
            _to_token_order(lz_in[g], lz_nat[g], dils[g])
        for h in range(hg):
            w = _group_weights(lz_nat, h)
            dw = []
            for g in range(G):
                dyg = dy_ref[:, (g * hg + h) * HEAD_DIM:(g * hg + h + 1) * HEAD_DIM]
                dw.append(jnp.sum(dyg * o_nat[g][h], axis=-1, keepdims=True))
                do_nat[g][h] = w[g] * dyg
            tot = functools.reduce(lambda a, b: a + b, [w[g] * dw[g] for g in range(G)])
            for g in range(G):
                dlz_nat[g][h] = w[g] * (dw[g] - tot)
        for g in range(G):
            _to_subsequence_order(do_nat[g], do_out[g], dils[g])
            _to_subsequence_order(dlz_nat[g], dlz_out[g], dils[g])

    specs = [_sub_spec(d, ts, Wg) for d in dils]
    outs = pl.pallas_call(
        body, name="combine_bwd", grid=(S // ts,),
        in_specs=[pl.BlockSpec((ts, G * Wg), lambda i: (i, 0))] + specs + specs,
        out_specs=specs + specs,
        out_shape=[jax.ShapeDtypeStruct((d, S // d, Wg), F32) for d in dils] * 2,
        scratch_shapes=[pltpu.VMEM((hg, ts, HEAD_DIM), F32)] * (4 * G),
        compiler_params=_cp("arbitrary"))(dy, *[_sub_view(a, d) for a, d in zip(os_, dils)],
                                          *[_sub_view(a, d) for a, d in zip(lzs, dils)])
    flat = [a.reshape(S, Wg) for a in outs]
    return flat[:G], flat[G:]


def _concat_cast(parts, dils):
    S = parts[0].shape[0]
    widths = [p.shape[1] for p in parts]
    n = len(parts)
    ts = _pick(S, (256, 128))

    def body(*refs):
        o_ref, nat = refs[n], refs[n + 1]
        off = 0
        for r, w, d in zip(refs, widths, dils):
            _to_token_order(r, nat, d)
            for k in range(w // LANES):
                o_ref[:, off + k * LANES:off + (k + 1) * LANES] = nat[k].astype(BF16)
            off += w

    assert len(set(widths)) == 1
    return pl.pallas_call(
        body, name="concat_cast", grid=(S // ts,),
        in_specs=[_sub_spec(d, ts, w) for w, d in zip(widths, dils)],
        out_specs=pl.BlockSpec((ts, sum(widths)), lambda i: (i, 0)),
        out_shape=jax.ShapeDtypeStruct((S, sum(widths)), BF16),
        scratch_shapes=[pltpu.VMEM((widths[0] // LANES, ts, LANES), F32)],
        compiler_params=_cp("arbitrary"))(*[_sub_view(p, d) for p, d in zip(parts, dils)])


def _ffn_specs(S, dff, cq, ts, tc, layer, order):
    nfc = dff // tc
    nps = cq // tc
    hb = ts // SUBLANES
    nrow8 = S // SUBLANES

    def u_main(half):
        return pl.BlockSpec((ts, tc), lambda *g: (order(*g)[0], order(*g)[1] % nfc + half * nfc))

    def u_prev(half):
        return pl.BlockSpec((SUBLANES, tc), lambda *g: (jnp.maximum(order(*g)[0] * hb - 1, 0),
                                                         order(*g)[1] % nfc + half * nfc))

    def u_next(half):
        return pl.BlockSpec((SUBLANES, tc), lambda *g: (jnp.minimum((order(*g)[0] + 1) * hb, nrow8 - 1),
                                                         order(*g)[1] % nfc + half * nfc))

    def cw(half):
        def im(*g):
            jj = order(*g)[1] % nfc + half * nfc
            return (layer, jj // nps, 0, jj % nps)
        return pl.BlockSpec((None, None, 3, tc), im)

    def cb(half):
        return pl.BlockSpec((None, 1, tc), lambda *g: (layer, 0, order(*g)[1] % nfc + half * nfc))

    return nfc, u_main, u_prev, u_next, cw, cb


def _ffn_act_fwd(u, cw_full, cb3, layer):
    S, two_dff = u.shape
    dff = two_dff // 2
    cq = cw_full.shape[3]
    ts = _pick(S, (1024, 512, 256, 128, 64, 32, 16))
    tc = _pick(cq, (256, 128))
    order = lambda j, i: (i, j)
    nfc, u_main, u_prev, u_next, cw, cb = _ffn_specs(S, dff, cq, ts, tc, layer, order)
    nrow = S // ts

    def body(ug, ugp, ugn, uv, uvp, uvn, wg, wv, bg, bv, a_ref):
        i = pl.program_id(1)
        row = lax.broadcasted_iota(jnp.int32, (ts, tc), 0)

        def conv(x_ref, p_ref, n_ref, w_ref, b_ref):
            x = x_ref[...]
            prev = jnp.where(i > 0, p_ref[SUBLANES - 1:SUBLANES, :], 0.0)
            nxt = jnp.where(i < nrow - 1, n_ref[0:1, :], 0.0)
            xm = jnp.where(row == 0, prev, pltpu.roll(x, 1, 0))
            xp = jnp.where(row == ts - 1, nxt, pltpu.roll(x, ts - 1, 0))
            return w_ref[0:1, :] * xm + w_ref[1:2, :] * x + w_ref[2:3, :] * xp + b_ref[...]

        gc = conv(ug, ugp, ugn, wg, bg)
        vc = conv(uv, uvp, uvn, wv, bv)
        a_ref[...] = (gc * (1.0 / (1.0 + jnp.exp(-gc))) * vc).astype(BF16)

    return pl.pallas_call(
        body, name="ffn_act_fwd", grid=(nfc, nrow),
        in_specs=[u_main(0), u_prev(0), u_next(0), u_main(1), u_prev(1), u_next(1), cw(0), cw(1), cb(0), cb(1)],
        out_specs=pl.BlockSpec((ts, tc), lambda j, i: (i, j)),
        out_shape=jax.ShapeDtypeStruct((S, dff), BF16),
        compiler_params=_cp("arbitrary", "arbitrary"))(u, u, u, u, u, u, cw_full, cw_full, cb3, cb3)


def _ffn_act_bwd(u, da, cw_full, cb3, layer):
    S, two_dff = u.shape
    dff = two_dff // 2
    cq = cw_full.shape[3]
    ts = _pick(S, (1024, 512, 256, 128, 64, 32, 16))
    tc = _pick(cq, (256, 128))
    order = lambda j, i: (i, j)
    nfc, u_main, u_prev, u_next, cw, cb = _ffn_specs(S, dff, cq, ts, tc, layer, order)
    nrow = S // ts
    hb = ts // SUBLANES
    te = ts + 2 * SUBLANES
    da_main = pl.BlockSpec((ts, tc), lambda j, i: (i, j))
    da_prev = pl.BlockSpec((SUBLANES, tc), lambda j, i: (jnp.maximum(i * hb - 1, 0), j))
    da_next = pl.BlockSpec((SUBLANES, tc), lambda j, i: (jnp.minimum((i + 1) * hb, S // SUBLANES - 1), j))
    main = slice(SUBLANES, SUBLANES + ts)

    def body(ug, ugp, ugn, uv, uvp, uvn, dam, dap, dan, wg, wv, bg, bv, dug_ref, duv_ref, accg_ref, accv_ref):
        i = pl.program_id(1)

        @pl.when(i == 0)
        def _():
            accg_ref[...] = jnp.zeros_like(accg_ref)
            accv_ref[...] = jnp.zeros_like(accv_ref)

        def ext(m, p, n):
            return jnp.concatenate([jnp.where(i > 0, p[...], 0.0), m[...], jnp.where(i < nrow - 1, n[...], 0.0)], axis=0)

        def shift(x):
            return pltpu.roll(x, 1, 0), pltpu.roll(x, te - 1, 0)

        xg, xv, dae = ext(ug, ugp, ugn), ext(uv, uvp, uvn), ext(dam, dap, dan)
        xgm, xgp = shift(xg)
        xvm, xvp = shift(xv)
        gc = wg[0:1, :] * xgm + wg[1:2, :] * xg + wg[2:3, :] * xgp + bg[...]
        vc = wv[0:1, :] * xvm + wv[1:2, :] * xv + wv[2:3, :] * xvp + bv[...]
        sig = 1.0 / (1.0 + jnp.exp(-gc))
        silu = gc * sig
        dcg = dae * vc * (sig * (1.0 + gc * (1.0 - sig)))
        dcv = dae * silu

        def finish(dc, x, xm, xp, w_ref, du_ref, acc_ref):
            dm, dp = shift(dc)
            du = w_ref[0:1, :] * dp + w_ref[1:2, :] * dc + w_ref[2:3, :] * dm
            du_ref[...] = du[main, :].astype(BF16)
            dcm = dc[main, :]
            acc_ref[0:1, :] += jnp.sum(dcm * xm[main, :], axis=0, keepdims=True)
            acc_ref[1:2, :] += jnp.sum(dcm * x[main, :], axis=0, keepdims=True)
            acc_ref[2:3, :] += jnp.sum(dcm * xp[main, :], axis=0, keepdims=True)
            acc_ref[3:4, :] += jnp.sum(dcm, axis=0, keepdims=True)

        finish(dcg, xg, xgm, xgp, wg, dug_ref, accg_ref)
        finish(dcv, xv, xvm, xvp, wv, duv_ref, accv_ref)

    blk = pl.BlockSpec((ts, tc), lambda j, i: (i, j))
    acc = pl.BlockSpec((SUBLANES, tc), lambda j, i: (0, j))
    dug, duv, accg, accv = pl.pallas_call(
        body, name="ffn_act_bwd", grid=(nfc, nrow),
        in_specs=[u_main(0), u_prev(0), u_next(0), u_main(1), u_prev(1), u_next(1), da_main, da_prev, da_next,
                  cw(0), cw(1), cb(0), cb(1)],
        out_specs=[blk, blk, acc, acc],
        out_shape=[jax.ShapeDtypeStruct((S, dff), BF16)] * 2 + [jax.ShapeDtypeStruct((SUBLANES, dff), F32)] * 2,
        compiler_params=_cp("arbitrary", "arbitrary"))(u, u, u, u, u, u, da, da, da, cw_full, cw_full, cb3, cb3)
    return (dug, duv), jnp.concatenate([accg, accv], axis=1)


def _my_chip():
    return 2 * lax.axis_index("x") + lax.axis_index("y")


def _into_full(w, layer, dtype):
    L, a, b = w.shape
    tr = _pick(a, (512, 256, 128, 64, 32, 16, 8))

    def body(w_ref, o_ref):
        o_ref[...] = w_ref[...].astype(dtype)

    return pl.pallas_call(
        body, name="into_full", grid=(a // tr,),
        in_specs=[pl.BlockSpec((None, tr, b), lambda i: (layer, i, 0))],
        out_specs=pl.BlockSpec((None, None, tr, b), lambda i: (0, _my_chip(), i, 0)),
        out_shape=jax.ShapeDtypeStruct((1, N_CHIPS, a, b), dtype),
        compiler_params=_cp("arbitrary"))(w)


def _adam_math(w, g, m, v):
    m = ADAM_B1 * m + (1.0 - ADAM_B1) * g
    v = ADAM_B2 * v + (1.0 - ADAM_B2) * (g * g)
    m_hat = m / (1.0 - ADAM_B1 ** ADAM_STEP)
    v_hat = v / (1.0 - ADAM_B2 ** ADAM_STEP)
    delta = -ADAM_LR * (m_hat / (jnp.sqrt(v_hat) + ADAM_EPS) + ADAM_WD * w)
    return delta, m, v


def _adamw(w, g, m, v):
    R, C = w.shape
    tr = _pick(R, (128, 64, 32, 16, 8)) if R % SUBLANES == 0 and C % LANES == 0 else R

    def body(w_ref, g_ref, m_ref, v_ref, d_ref, nm_ref, nv_ref):
        d, nm, nv = _adam_math(w_ref[...], g_ref[...], m_ref[...], v_ref[...])
        d_ref[...] = d
        nm_ref[...] = nm
        nv_ref[...] = nv

    spec = pl.BlockSpec((tr, C), lambda i: (i, 0))
    return pl.pallas_call(
        body, name="adamw", grid=(R // tr,), in_specs=[spec] * 4, out_specs=[spec] * 3,
        out_shape=[jax.ShapeDtypeStruct((R, C), F32)] * 3, compiler_params=_cp("arbitrary"))(w, g, m, v)


ANY = pl.BlockSpec(memory_space=pl.ANY)


def _position():
    x, y, c = lax.axis_index("x"), lax.axis_index("y"), lax.axis_index("c")
    chips = [(1 - x, y), (x, 1 - y), (1 - x, 1 - y)]
    return x, y, c, chips


HBM = pl.BlockSpec(memory_space=pltpu.HBM)
SEM = pl.BlockSpec(memory_space=pltpu.SEMAPHORE)
EFFECT = pltpu.SideEffectType.DATAFLOW_SIDE_EFFECTING


def _in_hbm(a):
    return pltpu.with_memory_space_constraint(a, pltpu.HBM)


def _shard_half(buf, shape, chip, half):
    _, _, a, b = shape
    p = 2 * chip[0] + chip[1]
    if a % (4 * SUBLANES) == 0:
        return buf.at[0, p, pl.ds(half * (a // 2), a // 2)]
    return buf.at[0, p, :, pl.ds(half * (b // 2), b // 2)]


def _gather_copy(buf, shape, chip, half, to, send, recv, k):
    part = _shard_half(buf, shape, chip, half)
    return pltpu.make_async_remote_copy(src_ref=part, dst_ref=part, send_sem=send.at[k], recv_sem=recv.at[k],
                                        device_id=to, device_id_type=MESH)


def _gather_hop(buf, shape, hop, j, incoming, send, recv, k):
    x, y, c, chips = _position()
    if hop == "chips":
        chip, half, to = (chips[j] if incoming else (x, y)), c, (*chips[j], c)
    else:
        chip, half, to = chips[j], (1 - c if incoming else c), (x, y, 1 - c)
    return _gather_copy(buf, shape, chip, half, to, send, recv, k)


def _gather_start(fulls, name, hop, after=()):
    n = len(fulls)
    na = len(after)

    def body(*refs):
        send, recv = refs[n + na], refs[n + na + 1]
        buf, token = refs[n + na + 2:2 * n + na + 2], refs[2 * n + na + 2]
        for t in range(n):
            for j in range(3):
                _gather_hop(buf[t], fulls[t].shape, hop, j, False, send, recv, 3 * t + j).start()
        token[...] = jnp.zeros_like(token)

    outs = pl.pallas_call(
        body, name=name, in_specs=[HBM] * n + [ANY] * na,
        out_specs=[SEM, SEM] + [HBM] * n + [pl.BlockSpec(memory_space=pltpu.VMEM)],
        out_shape=[pltpu.SemaphoreType.DMA((3 * n,)), pltpu.SemaphoreType.DMA((3 * n,))]
        + [pltpu.HBM(f.shape, f.dtype) for f in fulls] + [jax.ShapeDtypeStruct((SUBLANES, LANES), F32)],
        input_output_aliases={t: 2 + t for t in range(n)},
        compiler_params=pltpu.CompilerParams(has_side_effects=EFFECT))(*[_in_hbm(f) for f in fulls], *after)
    return outs[0], outs[1], list(outs[2:2 + n]), outs[2 + n]


def _gather_wait(send, recv, fulls, after, name, hop):
    n = len(fulls)

    def body(*refs):
        buf, send_ref, recv_ref = refs[:n], refs[n], refs[n + 1]
        for t in range(n):
            for j in range(3):
                _gather_hop(buf[t], fulls[t].shape, hop, j, False, send_ref, recv_ref, 3 * t + j).wait_send()
                _gather_hop(buf[t], fulls[t].shape, hop, j, True, send_ref, recv_ref, 3 * t + j).wait_recv()

    outs = pl.pallas_call(
        body, name=name, in_specs=[HBM] * n + [SEM, SEM] + [ANY] * len(after), out_specs=[HBM] * n,
        out_shape=[pltpu.HBM(f.shape, f.dtype) for f in fulls],
        input_output_aliases={t: t for t in range(n)},
        compiler_params=pltpu.CompilerParams(has_side_effects=EFFECT))(*fulls, send, recv, *after)
    return list(outs)


def _allreduce_small(part):
    M, C = part.shape
    n_dev = 2 * N_CHIPS

    def body(x_ref, sum_ref, all_ref, send, recv, local):
        x, y, c, chips = _position()
        me, sib = (x, y, c), (x, y, 1 - c)

        def rows(px, py, pc):
            return all_ref.at[pl.ds((4 * px + 2 * py + pc) * M, M), :]

        def copy(k, block, to, src=None):
            return pltpu.make_async_remote_copy(
                src_ref=rows(*block) if src is None else src, dst_ref=rows(*block),
                send_sem=send.at[k], recv_sem=recv.at[k], device_id=to, device_id_type=MESH)

        mine = pltpu.make_async_copy(x_ref, rows(*me), local)
        mine.start()
        first = [copy(0, me, sib, src=x_ref)] + [copy(1 + j, me, (*chip, c), src=x_ref) for j, chip in enumerate(chips)]
        for cp in first:
            cp.start()
        passed = [copy(4 + j, (*chip, c), sib) for j, chip in enumerate(chips)]
        for j, chip in enumerate(chips):
            copy(1 + j, (*chip, c), me).wait_recv()
            passed[j].start()
        copy(0, sib, me).wait_recv()
        for j, chip in enumerate(chips):
            copy(4 + j, (*chip, 1 - c), me).wait_recv()
        for cp in first + passed:
            cp.wait_send()
        mine.wait()
        acc = all_ref[0:M, :]
        for d in range(1, n_dev):
            acc = acc + all_ref[d * M:(d + 1) * M, :]
        sum_ref[...] = acc

    vm = pl.BlockSpec(memory_space=pltpu.VMEM)
    return pl.pallas_call(
        body, name="allreduce_small", in_specs=[vm], out_specs=[vm],
        out_shape=[jax.ShapeDtypeStruct((M, C), F32)],
        scratch_shapes=[pltpu.VMEM((n_dev * M, C), F32), pltpu.SemaphoreType.DMA((7,)),
                        pltpu.SemaphoreType.DMA((7,)), pltpu.SemaphoreType.DMA],
        compiler_params=pltpu.CompilerParams(vmem_limit_bytes=VMEM_LIMIT))(part)[0]


N_PEERS = 2 * N_CHIPS - 1


def _peers():
    x, y, c, chips = _position()
    return [(x, y, 1 - c)] + [(*ch, c) for ch in chips] + [(*ch, 1 - c) for ch in chips]


def _reduce_copy(src, dst, peers, send, recv, t, r):
    px, py, pc = peers[r]
    return pltpu.make_async_remote_copy(
        src_ref=src.at[2 * px + py, pc], dst_ref=dst.at[r], send_sem=send.at[N_PEERS * t + r],
        recv_sem=recv.at[N_PEERS * t + r], device_id=peers[r], device_id_type=MESH)


def _reduce_start(grads, name, after=()):
    n = len(grads)
    na = len(after)
    lands = [lax.empty((N_PEERS,) + g.shape[2:], BF16) for g in grads]

    def body(*refs):
        send, recv = refs[2 * n + na], refs[2 * n + na + 1]
        src, dst = refs[2 * n + na + 2:3 * n + na + 2], refs[3 * n + na + 2:4 * n + na + 2]
        token = refs[4 * n + na + 2]
        peers = _peers()
        for t in range(n):
            for r in range(N_PEERS):
                _reduce_copy(src[t], dst[t], peers, send, recv, t, r).start()
        token[...] = jnp.zeros_like(token)

    outs = pl.pallas_call(
        body, name=name, in_specs=[HBM] * (2 * n) + [ANY] * na,
        out_specs=[SEM, SEM] + [HBM] * (2 * n) + [pl.BlockSpec(memory_space=pltpu.VMEM)],
        out_shape=[pltpu.SemaphoreType.DMA((N_PEERS * n,)), pltpu.SemaphoreType.DMA((N_PEERS * n,))]
        + [pltpu.HBM(a.shape, a.dtype) for a in grads + lands] + [jax.ShapeDtypeStruct((SUBLANES, LANES), F32)],
        input_output_aliases={t: 2 + t for t in range(2 * n)},
        compiler_params=pltpu.CompilerParams(has_side_effects=EFFECT))(*[_in_hbm(a) for a in grads + lands], *after)
    return outs[0], outs[1], list(outs[2:2 + n]), list(outs[2 + n:2 + 2 * n]), outs[2 + 2 * n]


def _reduce_wait(send, recv, grads, lands, after, name):
    n = len(grads)

    def body(*refs):
        src, dst, send_ref, recv_ref = refs[:n], refs[n:2 * n], refs[2 * n], refs[2 * n + 1]
        peers = _peers()
        for t in range(n):
            for r in range(N_PEERS):
                cp = _reduce_copy(src[t], dst[t], peers, send_ref, recv_ref, t, r)
                cp.wait_send()
                cp.wait_recv()

    outs = pl.pallas_call(
        body, name=name, in_specs=[HBM] * (2 * n) + [SEM, SEM] + [ANY] * len(after), out_specs=[HBM] * (2 * n),
        out_shape=[pltpu.HBM(a.shape, a.dtype) for a in grads + lands],
        input_output_aliases={t: t for t in range(2 * n)},
        compiler_params=pltpu.CompilerParams(has_side_effects=EFFECT))(*grads, *lands, send, recv, *after)
    return list(outs[:n]), list(outs[n:])


def _add_pieces(grad, land, stack, layer):
    _, _, R, C = grad.shape
    tr = _pick(R, (256, 128, 64, 32, 16))

    def body(g_ref, r_ref, stack_ref, o_ref):
        acc = g_ref[...].astype(F32)
        for r in range(N_PEERS):
            acc = acc + r_ref[r].astype(F32)
        o_ref[...] = acc

    return pl.pallas_call(
        body, name="add_pieces", grid=(R // tr,),
        in_specs=[pl.BlockSpec((None, None, tr, C), lambda i: (_my_chip(), lax.axis_index("c"), i, 0)),
                  pl.BlockSpec((N_PEERS, tr, C), lambda i: (0, i, 0)),
                  ANY],
        out_specs=pl.BlockSpec((None, None, tr, C), lambda i: (layer, lax.axis_index("c"), i, 0)),
        out_shape=jax.ShapeDtypeStruct(stack.shape, F32), input_output_aliases={2: 0},
        compiler_params=_cp("arbitrary"))(grad, land, stack)


def _ag_sibling(stacks):
    n = len(stacks)
    offs = np.cumsum([0] + [s.shape[0] for s in stacks])

    def body(*refs):
        buf, send, recv = refs[n:2 * n], refs[2 * n], refs[2 * n + 1]
        x, y, c, _ = _position()

        def copy(t, l, half):
            part = buf[t].at[l, half]
            return pltpu.make_async_remote_copy(
                src_ref=part, dst_ref=part, send_sem=send.at[int(offs[t]) + l], recv_sem=recv.at[int(offs[t]) + l],
                device_id=(x, y, 1 - c), device_id_type=MESH)

        cps = [copy(t, l, c) for t in range(n) for l in range(stacks[t].shape[0])]
        for cp in cps:
            cp.start()
        for t in range(n):
            for l in range(stacks[t].shape[0]):
                copy(t, l, 1 - c).wait_recv()
        for cp in cps:
            cp.wait_send()

    return pl.pallas_call(
        body, name="ag_sibling", in_specs=[ANY] * n, out_specs=[ANY] * n,
        out_shape=[jax.ShapeDtypeStruct(s.shape, F32) for s in stacks],
        input_output_aliases={t: t for t in range(n)},
        scratch_shapes=[pltpu.SemaphoreType.DMA((int(offs[-1]),)), pltpu.SemaphoreType.DMA((int(offs[-1]),))])(*stacks)


def _split8(dw, blocked):
    if blocked:
        p, k, nq = dw.shape
        return dw.reshape(p, 2, k // 2, nq)
    k, n = dw.shape
    return dw.reshape(N_CHIPS, 2, k // (2 * N_CHIPS), n)


def kernel(x, a_w_qkv, a_w_o, a_q_gain, a_k_gain, b_w_qkv, b_w_o, rel_bias, mix_norm, ffn_norm, w_up, conv_w, conv_b, w_down, final_norm, loss_target, m_a_w_qkv, m_a_w_o, m_a_q_gain, m_a_k_gain, m_b_w_qkv, m_b_w_o, m_rel_bias, m_mix_norm, m_ffn_norm, m_w_up, m_conv_w, m_conv_b, m_w_down, m_final_norm, v_a_w_qkv, v_a_w_o, v_a_q_gain, v_a_k_gain, v_b_w_qkv, v_b_w_o, v_rel_bias, v_mix_norm, v_ffn_norm, v_w_up, v_conv_w, v_conv_b, v_w_down, v_final_norm):
    S, D = x.shape[1], x.shape[2]
    h = x.reshape(S, D)
    target = loss_target.reshape(S, D)
    hg = B_HEADS_PER_GROUP
    G = len(B_GROUPS)
    n_a, n_b = a_w_qkv.shape[0], b_w_qkv.shape[0]
    depth = w_up.shape[0]
    cx, cy = lax.axis_index("x"), lax.axis_index("y")

    big = dict(a_w_qkv=a_w_qkv, a_w_o=a_w_o, b_w_qkv=b_w_qkv, b_w_o=b_w_o, w_up=w_up, w_down=w_down)
    blocked = dict(a_w_qkv=True, a_w_o=False, b_w_qkv=True, b_w_o=False, w_up=True, w_down=False)
    names = list(big)
    srcs = dict(big, conv_w=conv_w)
    started = []
    for i in range(depth):
        mix = [("a_w_qkv", i // 2), ("a_w_o", i // 2)] if i % 2 == 0 else [("b_w_qkv", i // 2), ("b_w_o", i // 2)]
        rest = [("w_up", i), ("conv_w", i), ("w_down", i)]
        stages = [mix[:1], mix[1:], rest] if i == 0 else [mix + rest]
        started.append([])
        for s, keys in enumerate(stages):
            bufs = [_into_full(srcs[k], l, F32 if k == "conv_w" else BF16) for k, l in keys]
            started[i].append((keys,) + _gather_start(bufs, "gather_start_%d_%d" % (i, s), "chips"))
    cb3 = conv_b.reshape(depth, 1, conv_b.shape[1])

    cos, sin = _rope_tables(S)
    buckets = jnp.asarray(_bucket_tables(False))
    bias = _bias_build(rel_bias, buckets)
    bias_t = _bias_build(rel_bias, jnp.asarray(_bucket_tables(True)))

    saved = []
    passing = {}

    def land(i, s, after):
        keys, send, recv, bufs, _ = started[i][s]
        bufs = _gather_wait(send, recv, bufs, after, "gather_wait_%d_%d" % (i, s), "chips")
        send, recv, bufs, token = _gather_start(bufs, "pass_start_%d_%d" % (i, s), "sibling")
        passing[i, s] = (keys, send, recv, bufs)
        return token

    def arrive(i, s, after, wl):
        keys, send, recv, bufs = passing.pop((i, s))
        bufs = _gather_wait(send, recv, bufs, after, "pass_wait_%d_%d" % (i, s), "sibling")
        for (k, _), buf in zip(keys, bufs):
            _, _, a, b = buf.shape
            wl[k] = buf if k == "conv_w" or blocked[k] else buf.reshape(1, N_CHIPS * a, b)

    first = [land(0, 0, [h])]
    for i in range(depth):
        j = i // 2
        wl = {}
        arrive(i, 0, [h], wl)
        sv = dict(h0=h, w=wl)
        hn = _rms_fwd(h, mix_norm[i:i + 1], after=[st[4] for layer in started for st in layer] + first if i == 0 else ())
        sv["hn"] = hn
        if i % 2 == 0:
            qkv = _mm_nn(hn, wl["a_w_qkv"], 0, blocked=True, name="a_qkv")
            qkvh = _prep_a_fwd(qkv, cos, sin, a_q_gain[j:j + 1], a_k_gain[j:j + 1])
            staged = len(started[i]) > 1
            o, lse = _flash_a_fwd(qkvh, after=[land(i, 1, [qkvh])] if staged else ())
            tok = ()
            if staged:
                arrive(i, 1, [o], wl)
                tok = [land(i, 2, [o])]
            sv.update(qkv=qkv, qkvh=qkvh, o=o, lse=lse)
            h = _mm_nn(o, wl["a_w_o"], 0, blocked=False, res=h, name="a_out", after=tok)
        else:
            qkvp = [_mm_nn_perm(hn, wl["b_w_qkv"], g) for g in range(G)]
            os_, lzs = [], []
            for g in range(G):
                o_g, lz_g = _battn_fwd(qkvp[g], bias, g)
                os_.append(o_g)
                lzs.append(lz_g)
            y = _combine_fwd(os_, lzs)
            sv.update(qkvp=qkvp, os=os_, lzs=lzs, y=y)
            h = _mm_nn(y, wl["b_w_o"], 0, blocked=False, res=h, name="b_out")
        sv["h1"] = h
        hf = _rms_fwd(h, ffn_norm[i:i + 1])
        if len(started[i]) > 2:
            arrive(i, 2, [hf], wl)
        u = _mm_nn(hf, wl["w_up"], 0, blocked=True, name="ffn_up")
        act = _ffn_act_fwd(u, wl["conv_w"], cb3[i:i + 1], 0)
        sv.update(hf=hf, u=u, act=act)
        nxt = [land(i + 1, 0, [act])] if i + 1 < depth else ()
        h = _mm_nn(act, wl["w_down"], 0, blocked=False, res=h, name="ffn_down", after=nxt)
        saved.append(sv)

    loss_blk, dh, dh_b, dg_final = _final_loss(h, final_norm.reshape(1, D), target)

    dws = {k: [None] * big[k].shape[0] for k in names}
    d_mix, d_ffn, d_convw, d_convb = [None] * depth, [None] * depth, [None] * depth, [None] * depth
    d_gq, d_gk = [None] * n_a, [None] * n_a
    dbias_list = []
    pending = []

    def start_reduce(keys, tag, after=()):
        pieces = [_split8(dws[k][l], blocked[k]) for k, l in keys]
        send, recv, pieces, lands, token = _reduce_start(pieces, "reduce_start_" + tag, after)
        pending.append((keys, send, recv, pieces, lands, tag))
        return (token,)

    tok = ()
    for i in reversed(range(depth)):
        j = i // 2
        sv = saved[i]
        wl = sv["w"]
        da = _mm_nt(dh_b, wl["w_down"], 0, blocked=False, name="ffn_down_dx", after=tok)
        dws["w_down"][i] = _mm_tn(sv["act"], dh_b, blocked=False, name="ffn_down_dw")
        du, dconv = _ffn_act_bwd(sv["u"], da, wl["conv_w"], cb3[i:i + 1], 0)
        d_convw[i], d_convb[i] = dconv[0:3], dconv[3]
        dhf = _mm_nt(du, wl["w_up"], 0, blocked=True, name="ffn_up_dx")
        dws["w_up"][i] = _mm_tn(sv["hf"], du, blocked=True, name="ffn_up_dw")
        dh, dh_b, dg = _rms_bwd(dhf, sv["h1"], ffn_norm[i:i + 1], dh)
        d_ffn[i] = dg[0]
        tok = start_reduce([("w_down", i), ("w_up", i)], "ffn%d" % i)
        if i % 2 == 0:
            do = _mm_nt(dh_b, wl["a_w_o"], 0, blocked=False, name="a_out_dx", after=tok)
            dws["a_w_o"][j] = _mm_tn(sv["o"], dh_b, blocked=False, name="a_out_dw")
            dq, dk, dv = _flash_a_bwd(sv["qkvh"], do, sv["o"], sv["lse"])
            dqkv, dgain = _prep_a_bwd(dq, dk, dv, sv["qkv"], cos, sin, a_q_gain[j:j + 1], a_k_gain[j:j + 1])
            d_gq[j], d_gk[j] = dgain[0], dgain[1]
            dhn = _mm_nt(dqkv, wl["a_w_qkv"], 0, blocked=True, name="a_qkv_dx")
            dws["a_w_qkv"][j] = _mm_tn(sv["hn"], dqkv, blocked=True, name="a_qkv_dw")
            mix_keys = [("a_w_o", j), ("a_w_qkv", j)]
        else:
            dy = _mm_nt(dh_b, wl["b_w_o"], 0, blocked=False, name="b_out_dx", after=tok)
            dws["b_w_o"][j] = _mm_tn(sv["y"], dh_b, blocked=False, name="b_out_dw")
            dos, dlzs = _combine_bwd(dy, sv["os"], sv["lzs"])
            parts = []
            for g in range(G):
                dq, rt, db = _battn_bwd_dq(sv["qkvp"][g], bias, dos[g], sv["os"][g], sv["lzs"][g], dlzs[g], g)
                dk, dv = _battn_bwd_dkv(sv["qkvp"][g], bias_t, dos[g], sv["lzs"][g], rt, g)
                parts += [dq, dk, dv]
                dbias_list.append((g, db))
            dqkv = _concat_cast(parts, [d for _, d in B_GROUPS for _ in range(3)])
            dhn = _mm_nt(dqkv, wl["b_w_qkv"], 0, blocked=True, name="b_qkv_dx")
            dws["b_w_qkv"][j] = _mm_tn(sv["hn"], dqkv, blocked=True, name="b_qkv_dw")
            mix_keys = [("b_w_o", j), ("b_w_qkv", j)]
        dh, dh_b, dg = _rms_bwd(dhn, sv["h0"], mix_norm[i:i + 1], dh)
        d_mix[i] = dg[0]
        if i > 0:
            tok = start_reduce(mix_keys, "mix%d" % i)
    grad_x = dh.reshape(x.shape)

    dbias_layers = [jnp.stack([db for g2, db in dbias_list[l * G:(l + 1) * G]]) for l in range(n_b)]
    d_rel = _bias_reduce(dbias_layers, buckets)[:, :G * hg]

    small = [jnp.stack(d_gq), jnp.stack(d_gk), d_rel, jnp.stack(d_mix), jnp.stack(d_ffn), jnp.stack(d_convw),
             jnp.stack(d_convb), dg_final[0]]
    sizes = [int(np.prod(s.shape)) for s in small]
    flat = jnp.concatenate([s.reshape(-1) for s in small])
    rows = -(-flat.shape[0] // (LANES * SUBLANES)) * SUBLANES
    flat = jnp.pad(flat, (0, rows * LANES - flat.shape[0])).reshape(rows, LANES)
    tot = _allreduce_small(flat)
    start_reduce(mix_keys, "mix0", after=[tot])
    tot = tot.reshape(-1)
    offs = np.cumsum([0] + sizes)
    g_gq, g_gk, g_rel, g_mix, g_ffn, g_convw_full, g_convb, g_final = [
        tot[offs[k]:offs[k + 1]].reshape(small[k].shape) for k in range(len(small))]
    cq = conv_w.shape[2]
    g_convw = lax.dynamic_slice_in_dim(g_convw_full, (2 * cx + cy) * cq, cq, axis=2)

    grads = dict(a_q_gain=g_gq, a_k_gain=g_gk, rel_bias=g_rel, mix_norm=g_mix, ffn_norm=g_ffn,
                 conv_w=g_convw, conv_b=g_convb, final_norm=g_final)
    weights = dict(a_w_qkv=a_w_qkv, a_w_o=a_w_o, a_q_gain=a_q_gain, a_k_gain=a_k_gain, b_w_qkv=b_w_qkv, b_w_o=b_w_o,
                   rel_bias=rel_bias, mix_norm=mix_norm, ffn_norm=ffn_norm, w_up=w_up, conv_w=conv_w, conv_b=conv_b,
                   w_down=w_down, final_norm=final_norm)
    ms = dict(a_w_qkv=m_a_w_qkv, a_w_o=m_a_w_o, a_q_gain=m_a_q_gain, a_k_gain=m_a_k_gain, b_w_qkv=m_b_w_qkv,
              b_w_o=m_b_w_o, rel_bias=m_rel_bias, mix_norm=m_mix_norm, ffn_norm=m_ffn_norm, w_up=m_w_up,
              conv_w=m_conv_w, conv_b=m_conv_b, w_down=m_w_down, final_norm=m_final_norm)
    vs = dict(a_w_qkv=v_a_w_qkv, a_w_o=v_a_w_o, a_q_gain=v_a_q_gain, a_k_gain=v_a_k_gain, b_w_qkv=v_b_w_qkv,
              b_w_o=v_b_w_o, rel_bias=v_rel_bias, mix_norm=v_mix_norm, ffn_norm=v_ffn_norm, w_up=v_w_up,
              conv_w=v_conv_w, conv_b=v_conv_b, w_down=v_w_down, final_norm=v_final_norm)
    deltas, new_m, new_v, stacks = {}, {}, {}, {}

    def update(k):
        w = weights[k]
        two_d = (-1, w.shape[-1])
        d, nm, nv = _adamw(w.reshape(two_d), grads[k].reshape(two_d), ms[k].reshape(two_d), vs[k].reshape(two_d))
        deltas[k], new_m[k], new_v[k] = d.reshape(w.shape), nm.reshape(w.shape), nv.reshape(w.shape)
        return nv

    def collect(items, after):
        for keys, send, recv, pieces, lands, tag in items:
            pieces, lands = _reduce_wait(send, recv, pieces, lands, after, "reduce_wait_" + tag)
            for (k, l), p, land in zip(keys, pieces, lands):
                if k not in stacks:
                    stacks[k] = lax.empty((big[k].shape[0], 2) + p.shape[2:], F32)
                stacks[k] = _add_pieces(p, land, stacks[k], l)

    def share(ks):
        for k, gs in zip(ks, _ag_sibling([stacks[k] for k in ks])):
            grads[k] = gs.reshape(big[k].shape)

    late = [k for k in names if k in {kk for kk, _ in pending[-1][0]}]
    collect(pending[:-1], [dh])
    share([k for k in names if k not in late])
    done = [update(k) for k in weights if k not in late]
    collect(pending[-1:], done)
    share(late)
    for k in late:
        update(k)

    loss = lax.psum(loss_blk[0, 0], ("x", "y", "c"))
    keys = list(weights)
    return (loss, grad_x, *[grads[k].reshape(weights[k].shape) for k in keys], *[deltas[k] for k in keys],
            *[new_m[k] for k in keys], *[new_v[k] for k in keys])
```

```python
import functools
import math

import numpy as np
import jax
import jax.numpy as jnp
from jax import lax
from jax.experimental import pallas as pl
from jax.experimental.pallas import tpu as pltpu

F32 = jnp.float32
BF16 = jnp.bfloat16

HEAD_DIM = 128
A_HEADS = 16
A_KV_HEADS = 4
GRID_W = 64
ROPE_THETA = 10000.0
B_GROUPS = ((128, 1), (512, 4), (2048, 16))
B_HEADS_PER_GROUP = 8
REL_BUCKETS = 32
REL_MAX_DISTANCE = 1024
EPS = 1e-6
NEG_INF = -1e30
DEPTH = 4
ADAM_LR = 0.001
ADAM_B1 = 0.9
ADAM_B2 = 0.999
ADAM_EPS = 1e-08
ADAM_WD = 0.01
ADAM_STEP = 10

N_CHIPS = 4
LANES = 128
SUBLANES = 8
VMEM_LIMIT = 52 * 1024 * 1024
MESH = pl.DeviceIdType.MESH


def _pick(n, cands):
    for c in cands:
        if c <= n and n % c == 0:
            return c
    return n


def _lane_tile(n, cap):
    best = None
    for t in range(LANES, min(n, cap) + 1, LANES):
        if n % t == 0:
            best = t
    return best or n


def _cp(*sem):
    return pltpu.CompilerParams(dimension_semantics=sem if sem else None, vmem_limit_bytes=VMEM_LIMIT)


def _half_span():
    hs = {w // (2 * d) for w, d in B_GROUPS}
    assert len(hs) == 1
    return hs.pop()


def _rms_fwd(h, gain, after=()):
    S, D = h.shape
    ts = _pick(S, (512, 256, 128, 64, 32, 16))

    def body(h_ref, g_ref, *rest):
        o_ref = rest[-1]
        x = h_ref[...]
        r = lax.rsqrt(jnp.mean(x * x, axis=-1, keepdims=True) + EPS)
        o_ref[...] = (x * r * g_ref[...]).astype(o_ref.dtype)

    return pl.pallas_call(
        body, name="rms_fwd", grid=(S // ts,),
        in_specs=[pl.BlockSpec((ts, D), lambda i: (i, 0)), pl.BlockSpec((1, D), lambda i: (0, 0))]
        + [pl.BlockSpec(memory_space=pl.ANY)] * len(after),
        out_specs=pl.BlockSpec((ts, D), lambda i: (i, 0)),
        out_shape=jax.ShapeDtypeStruct((S, D), BF16), compiler_params=_cp("arbitrary"))(h, gain, *after)


def _rms_bwd(dy, h, gain, dres):
    S, D = h.shape
    ts = _pick(S, (256, 128, 64, 32, 16))

    def body(dy_ref, h_ref, g_ref, dres_ref, dh_ref, dhb_ref, dg_ref):
        @pl.when(pl.program_id(0) == 0)
        def _():
            dg_ref[...] = jnp.zeros_like(dg_ref)
        x = h_ref[...]
        dy = dy_ref[...]
        r = lax.rsqrt(jnp.mean(x * x, axis=-1, keepdims=True) + EPS)
        xn = x * r
        dg_ref[0:1, :] += jnp.sum(dy * xn, axis=0, keepdims=True)
        dxn = dy * g_ref[...]
        dx = r * (dxn - xn * jnp.mean(dxn * xn, axis=-1, keepdims=True))
        dh = dres_ref[...] + dx
        dh_ref[...] = dh
        dhb_ref[...] = dh.astype(BF16)

    row = pl.BlockSpec((ts, D), lambda i: (i, 0))
    return pl.pallas_call(
        body, name="rms_bwd", grid=(S // ts,),
        in_specs=[row, row, pl.BlockSpec((1, D), lambda i: (0, 0)), row],
        out_specs=[row, row, pl.BlockSpec((SUBLANES, D), lambda i: (0, 0))],
        out_shape=[jax.ShapeDtypeStruct((S, D), F32), jax.ShapeDtypeStruct((S, D), BF16),
                   jax.ShapeDtypeStruct((SUBLANES, D), F32)],
        compiler_params=_cp("arbitrary"))(dy, h, gain, dres)


def _final_loss(h, gain, target):
    S, D = h.shape
    ts = _pick(S, (256, 128, 64, 32, 16))

    def body(h_ref, g_ref, t_ref, loss_ref, dh_ref, dhb_ref, dg_ref):
        @pl.when(pl.program_id(0) == 0)
        def _():
            dg_ref[...] = jnp.zeros_like(dg_ref)
            loss_ref[...] = jnp.zeros_like(loss_ref)
        x = h_ref[...]
        g = g_ref[...]
        r = lax.rsqrt(jnp.mean(x * x, axis=-1, keepdims=True) + EPS)
        xn = x * r
        err = xn * g - t_ref[...]
        part = 0.5 * jnp.sum(jnp.mean(err * err, axis=-1, keepdims=True), axis=0, keepdims=True)
        loss_ref[0:1, 0:1] += part
        dy = err * (1.0 / D)
        dg_ref[0:1, :] += jnp.sum(dy * xn, axis=0, keepdims=True)
        dxn = dy * g
        dh = r * (dxn - xn * jnp.mean(dxn * xn, axis=-1, keepdims=True))
        dh_ref[...] = dh
        dhb_ref[...] = dh.astype(BF16)

    row = pl.BlockSpec((ts, D), lambda i: (i, 0))
    return pl.pallas_call(
        body, name="final_loss", grid=(S // ts,),
        in_specs=[row, pl.BlockSpec((1, D), lambda i: (0, 0)), row],
        out_specs=[pl.BlockSpec((SUBLANES, LANES), lambda i: (0, 0)), row, row,
                   pl.BlockSpec((SUBLANES, D), lambda i: (0, 0))],
        out_shape=[jax.ShapeDtypeStruct((SUBLANES, LANES), F32), jax.ShapeDtypeStruct((S, D), F32),
                   jax.ShapeDtypeStruct((S, D), BF16), jax.ShapeDtypeStruct((SUBLANES, D), F32)],
        compiler_params=_cp("arbitrary"))(h, gain, target)


_NN = (((1,), (0,)), ((), ()))
_NT = (((1,), (1,)), ((), ()))
_TN = (((0,), (0,)), ((), ()))


def _mm_nn(a, w, layer, *, blocked, out_dtype=F32, res=None, name, after=()):
    M, K = a.shape
    if blocked:
        nq = w.shape[3]
        N = N_CHIPS * nq
        tn = _lane_tile(nq, 1408)
        nps = nq // tn
        w_spec = pl.BlockSpec((None, None, K, tn), lambda i, j: (layer, j // nps, 0, j % nps))
    else:
        N = w.shape[2]
        tn = _lane_tile(N, 1024)
        w_spec = pl.BlockSpec((None, K, tn), lambda i, j: (layer, 0, j))
    tm = _pick(M, (1024, 512, 256, 128, 64, 32, 16)) if K <= 3072 else _pick(M, (512, 256, 128, 64, 32, 16))

    def body(*refs):
        a_ref, w_ref, o_ref = refs[0], refs[1], refs[-1]
        acc = lax.dot_general(a_ref[...], w_ref[...], _NN, preferred_element_type=F32)
        if res is not None:
            acc = refs[2][...] + acc
        o_ref[...] = acc.astype(o_ref.dtype)

    in_specs = [pl.BlockSpec((tm, K), lambda i, j: (i, 0)), w_spec]
    args = [a, w]
    if res is not None:
        in_specs.append(pl.BlockSpec((tm, tn), lambda i, j: (i, j)))
        args.append(res)
    return pl.pallas_call(
        body, name=name, grid=(M // tm, N // tn), in_specs=in_specs + [pl.BlockSpec(memory_space=pl.ANY)] * len(after),
        out_specs=pl.BlockSpec((tm, tn), lambda i, j: (i, j)),
        out_shape=jax.ShapeDtypeStruct((M, N), out_dtype),
        compiler_params=_cp("arbitrary", "arbitrary"))(*args, *after)


def _mm_nt(a, w, layer, *, blocked, name, after=()):
    pair = isinstance(a, tuple)
    M = a[0].shape[0] if pair else a.shape[0]
    tm = _pick(M, (1024, 512, 256, 128, 64, 32, 16))
    if blocked:
        K, nq = w.shape[2], w.shape[3]
        tk = _pick(K, (1024, 512, 256, 128))
        half = N_CHIPS // 2

        def body(*refs):
            a_refs, (w_ref, o_ref, acc_ref) = refs[:-3], refs[-3:]
            p = pl.program_id(2)

            @pl.when(p == 0)
            def _():
                acc_ref[...] = jnp.zeros_like(acc_ref)
            if pair:
                @pl.when(p < half)
                def _():
                    acc_ref[...] += lax.dot_general(a_refs[0][...], w_ref[...], _NT, preferred_element_type=F32)

                @pl.when(p >= half)
                def _():
                    acc_ref[...] += lax.dot_general(a_refs[1][...], w_ref[...], _NT, preferred_element_type=F32)
            else:
                acc_ref[...] += lax.dot_general(a_refs[0][...], w_ref[...], _NT, preferred_element_type=F32)

            @pl.when(p == N_CHIPS - 1)
            def _():
                o_ref[...] = acc_ref[...]

        if pair:
            a_specs = [pl.BlockSpec((tm, nq), lambda i, j, p: (i, jnp.minimum(p, half - 1))),
                       pl.BlockSpec((tm, nq), lambda i, j, p: (i, jnp.maximum(p - half, 0)))]
            a_args = list(a)
        else:
            a_specs = [pl.BlockSpec((tm, nq), lambda i, j, p: (i, p))]
            a_args = [a]
        return pl.pallas_call(
            body, name=name, grid=(M // tm, K // tk, N_CHIPS),
            in_specs=a_specs + [pl.BlockSpec((None, None, tk, nq), lambda i, j, p: (layer, p, j, 0))],
            out_specs=pl.BlockSpec((tm, tk), lambda i, j, p: (i, j)),
            out_shape=jax.ShapeDtypeStruct((M, K), F32),
            scratch_shapes=[pltpu.VMEM((tm, tk), F32)],
            compiler_params=_cp("arbitrary", "arbitrary", "arbitrary"))(*a_args, w)
    K, N = w.shape[1], w.shape[2]
    tk = _pick(K, (1024, 512, 256, 128))

    def body(a_ref, w_ref, *rest):
        rest[-1][...] = lax.dot_general(a_ref[...], w_ref[...], _NT, preferred_element_type=F32)

    return pl.pallas_call(
        body, name=name, grid=(M // tm, K // tk),
        in_specs=[pl.BlockSpec((tm, N), lambda i, j: (i, 0)),
                  pl.BlockSpec((None, tk, N), lambda i, j: (layer, j, 0))]
        + [pl.BlockSpec(memory_space=pl.ANY)] * len(after),
        out_specs=pl.BlockSpec((tm, tk), lambda i, j: (i, j)),
        out_shape=jax.ShapeDtypeStruct((M, K), F32),
        compiler_params=_cp("arbitrary", "arbitrary"))(a, w, *after)


def _mm_tn(x, dy, *, blocked, name):
    pair = isinstance(dy, tuple)
    S, K = x.shape
    N = 2 * dy[0].shape[1] if pair else dy.shape[1]
    tk = _pick(K, (512, 256, 128))
    if blocked:
        nq = N // N_CHIPS
        tn = _lane_tile(nq, 1408)
        nps = nq // tn
        out_spec = pl.BlockSpec((None, tk, tn), lambda i, j, s: (j // nps, i, j % nps))
        out_shape = jax.ShapeDtypeStruct((N_CHIPS, K, nq), BF16)
    else:
        tn = _lane_tile(N, 1024)
        out_spec = pl.BlockSpec((tk, tn), lambda i, j, s: (i, j))
        out_shape = jax.ShapeDtypeStruct((K, N), BF16)
    nj = N // tn
    njh = nj // 2
    ns = 2 if pair else 1
    sh = S // ns

    def body(x_ref, *refs):
        o_ref, acc_ref = refs[-2], refs[-1]

        def product(dy_ref):
            part = lax.dot_general(x_ref[...], dy_ref[...], _TN, preferred_element_type=F32)
            if ns == 1:
                o_ref[...] = part.astype(o_ref.dtype)
            else:
                s = pl.program_id(2)

                @pl.when(s == 0)
                def _():
                    acc_ref[...] = part

                @pl.when(s == ns - 1)
                def _():
                    o_ref[...] = (acc_ref[...] + part).astype(o_ref.dtype)

        if pair:
            j = pl.program_id(1)
            pl.when(j < njh)(lambda: product(refs[0]))
            pl.when(j >= njh)(lambda: product(refs[1]))
        else:
            product(refs[0])

    if pair:
        assert nj % 2 == 0
        dy_specs = [pl.BlockSpec((sh, tn), lambda i, j, s: (jnp.where(j < njh, s, ns - 1), jnp.minimum(j, njh - 1))),
                    pl.BlockSpec((sh, tn), lambda i, j, s: (jnp.where(j < njh, 0, s), jnp.maximum(j - njh, 0)))]
        dy_args = list(dy)
    else:
        dy_specs = [pl.BlockSpec((sh, tn), lambda i, j, s: (s, j))]
        dy_args = [dy]
    return pl.pallas_call(
        body, name=name, grid=(K // tk, nj, ns),
        in_specs=[pl.BlockSpec((sh, tk), lambda i, j, s: (s, i))] + dy_specs,
        out_specs=out_spec, out_shape=out_shape,
        scratch_shapes=[pltpu.VMEM((tk, tn) if ns > 1 else (SUBLANES, LANES), F32)],
        compiler_params=_cp("arbitrary", "arbitrary", "arbitrary"))(x, *dy_args)


def _rope_tables(S):
    rows = S // GRID_W
    row_ids = jnp.repeat(jnp.arange(rows, dtype=F32), GRID_W)
    col_ids = jnp.tile(jnp.arange(GRID_W, dtype=F32), rows)
    quarter = HEAD_DIM // 4
    inv_freq = ROPE_THETA ** (-jnp.arange(quarter, dtype=F32) / quarter)
    ang_r = row_ids[:, None] * inv_freq[None, :]
    ang_c = col_ids[:, None] * inv_freq[None, :]
    cos = jnp.concatenate([jnp.cos(ang_r)] * 2 + [jnp.cos(ang_c)] * 2, axis=-1)
    sin = jnp.concatenate([-jnp.sin(ang_r), jnp.sin(ang_r), -jnp.sin(ang_c), jnp.sin(ang_c)], axis=-1)
    return cos, sin


def _swap_quarters(x):
    lane = lax.broadcasted_iota(jnp.int32, x.shape, 1)
    first = (lane % (HEAD_DIM // 2)) < (HEAD_DIM // 4)
    return jnp.where(first, pltpu.roll(x, HEAD_DIM - HEAD_DIM // 4, 1), pltpu.roll(x, HEAD_DIM // 4, 1))


A_SCALE = HEAD_DIM ** -0.5
A_QSCALE = A_SCALE * math.log2(math.e)


def _prep_a_fwd(qkv, cos, sin, gq, gk):
    S, W = qkv.shape
    nrm = A_HEADS + A_KV_HEADS
    ts = _pick(S, (256, 128, 64, 32, 16))

    def body(qkv_ref, cos_ref, sin_ref, gq_ref, gk_ref, o_ref):
        cos_t = cos_ref[...]
        sin_t = sin_ref[...]
        for j in range(nrm):
            sl = slice(j * HEAD_DIM, (j + 1) * HEAD_DIM)
            x = qkv_ref[:, sl]
            g = gq_ref[...] if j < A_HEADS else gk_ref[...]
            r = lax.rsqrt(jnp.mean(x * x, axis=-1, keepdims=True) + EPS)
            n = x * r * g
            y = n * cos_t + _swap_quarters(n) * sin_t
            o_ref[:, sl] = (y * A_QSCALE if j < A_HEADS else y).astype(BF16)
        o_ref[:, nrm * HEAD_DIM:] = qkv_ref[:, nrm * HEAD_DIM:].astype(BF16)

    row = lambda w: pl.BlockSpec((ts, w), lambda i: (i, 0))
    one = pl.BlockSpec((1, HEAD_DIM), lambda i: (0, 0))
    return pl.pallas_call(
        body, name="prep_a_fwd", grid=(S // ts,),
        in_specs=[row(W), row(HEAD_DIM), row(HEAD_DIM), one, one], out_specs=row(W),
        out_shape=jax.ShapeDtypeStruct((S, W), BF16), compiler_params=_cp("arbitrary"))(qkv, cos, sin, gq, gk)


def _prep_a_bwd(dq, dk, dv, qkv, cos, sin, gq, gk):
    S, W = qkv.shape
    nrm = A_HEADS + A_KV_HEADS
    nq, nk = A_HEADS * HEAD_DIM, A_KV_HEADS * HEAD_DIM
    ts = _pick(S, (256, 128, 64, 32, 16))

    def body(dq_ref, dk_ref, dv_ref, qkv_ref, cos_ref, sin_ref, gq_ref, gk_ref, o_ref, dg_ref):
        @pl.when(pl.program_id(0) == 0)
        def _():
            dg_ref[...] = jnp.zeros_like(dg_ref)
        cos_t = cos_ref[...]
        sin_t = sin_ref[...]
        for j in range(nrm):
            sl = slice(j * HEAD_DIM, (j + 1) * HEAD_DIM)
            x = qkv_ref[:, sl]
            if j < A_HEADS:
                dy, g, grow = dq_ref[:, sl], gq_ref[...], 0
            else:
                jj = j - A_HEADS
                dy, g, grow = dk_ref[:, jj * HEAD_DIM:(jj + 1) * HEAD_DIM], gk_ref[...], 1
            r = lax.rsqrt(jnp.mean(x * x, axis=-1, keepdims=True) + EPS)
            xn = x * r
            dn = dy * cos_t + _swap_quarters(dy * sin_t)
            dg_ref[grow:grow + 1, :] += jnp.sum(dn * xn, axis=0, keepdims=True)
            dxn = dn * g
            o_ref[:, sl] = (r * (dxn - xn * jnp.mean(dxn * xn, axis=-1, keepdims=True))).astype(BF16)
        o_ref[:, nrm * HEAD_DIM:] = dv_ref[...].astype(BF16)

    row = lambda w: pl.BlockSpec((ts, w), lambda i: (i, 0))
    one = pl.BlockSpec((1, HEAD_DIM), lambda i: (0, 0))
    return pl.pallas_call(
        body, name="prep_a_bwd", grid=(S // ts,),
        in_specs=[row(nq), row(nk), row(nk), row(W), row(HEAD_DIM), row(HEAD_DIM), one, one],
        out_specs=[row(W), pl.BlockSpec((SUBLANES, HEAD_DIM), lambda i: (0, 0))],
        out_shape=[jax.ShapeDtypeStruct((S, W), BF16), jax.ShapeDtypeStruct((SUBLANES, HEAD_DIM), F32)],
        compiler_params=_cp("arbitrary"))(dq, dk, dv, qkv, cos, sin, gq, gk)


def _flash_a_fwd(qkvh, after=()):
    S = qkvh.shape[0]
    grp = A_HEADS // A_KV_HEADS
    tq = _pick(S, (256, 128, 64, 32, 16))
    kc = _pick(S, (512, 256, 128))
    lanes = [slice(b * LANES, (b + 1) * LANES) for b in range(kc // LANES)]

    def body(q_ref, k_ref, v_ref, *rest):
        o_ref, lse_ref = rest[-2], rest[-1]
        q = q_ref[...]
        m_t = jnp.full((tq, LANES), -jnp.inf, F32)
        for c in range(S // kc):
            s = lax.dot_general(q, k_ref[c * kc:(c + 1) * kc, :], _NT, preferred_element_type=F32)
            for sl in lanes:
                m_t = jnp.maximum(m_t, s[:, sl])
        m = jnp.max(m_t, axis=-1, keepdims=True)
        l_t = jnp.zeros((tq, LANES), F32)
        acc = jnp.zeros((tq, HEAD_DIM), F32)
        for c in range(S // kc):
            rows = slice(c * kc, (c + 1) * kc)
            p = jnp.exp2(lax.dot_general(q, k_ref[rows, :], _NT, preferred_element_type=F32) - m)
            for sl in lanes:
                l_t = l_t + p[:, sl]
            acc = acc + lax.dot_general(p.astype(BF16), v_ref[rows, :], _NN, preferred_element_type=F32)
        l = jnp.sum(l_t, axis=-1, keepdims=True)
        o_ref[...] = (acc * (1.0 / l)).astype(BF16)
        lse_ref[...] = jnp.broadcast_to(m + jnp.log2(l), lse_ref.shape)

    qs = pl.BlockSpec((tq, HEAD_DIM), lambda h, i: (i, h))
    return pl.pallas_call(
        body, name="flash_a_fwd", grid=(A_HEADS, S // tq),
        in_specs=[qs,
                  pl.BlockSpec((S, HEAD_DIM), lambda h, i: (0, A_HEADS + h // grp)),
                  pl.BlockSpec((S, HEAD_DIM), lambda h, i: (0, A_HEADS + A_KV_HEADS + h // grp))]
        + [pl.BlockSpec(memory_space=pl.ANY)] * len(after),
        out_specs=[qs, qs],
        out_shape=[jax.ShapeDtypeStruct((S, A_HEADS * HEAD_DIM), BF16),
                   jax.ShapeDtypeStruct((S, A_HEADS * HEAD_DIM), F32)],
        compiler_params=_cp("arbitrary", "arbitrary"))(qkvh, qkvh, qkvh, *after)


def _flash_a_bwd(qkvh, do, o, lse):
    S = qkvh.shape[0]
    grp = A_HEADS // A_KV_HEADS
    tq = _pick(S, (256, 128, 64, 32, 16))
    nq = S // tq

    def body(q_ref, k_ref, v_ref, do_ref, o_ref, lse_ref, dq_ref, dk_ref, dv_ref):
        g, i = pl.program_id(1), pl.program_id(2)

        @pl.when((g == 0) & (i == 0))
        def _():
            dk_ref[...] = jnp.zeros_like(dk_ref)
            dv_ref[...] = jnp.zeros_like(dv_ref)
        q = q_ref[...]
        k = k_ref[...]
        do_f = do_ref[...]
        do_b = do_f.astype(BF16)
        delta = jnp.sum(do_f * o_ref[...].astype(F32), axis=-1, keepdims=True)
        p = jnp.exp2(lax.dot_general(q, k, _NT, preferred_element_type=F32) - lse_ref[:, 0:1])
        dp = lax.dot_general(do_b, v_ref[...], _NT, preferred_element_type=F32)
        ds_b = (p * (dp - delta)).astype(BF16)
        dq_ref[...] = lax.dot_general(ds_b, k, _NN, preferred_element_type=F32) * A_SCALE
        dk_ref[...] += lax.dot_general(ds_b, q, _TN, preferred_element_type=F32)
        dv_ref[...] += lax.dot_general(p.astype(BF16), do_b, _TN, preferred_element_type=F32)

        @pl.when((g == grp - 1) & (i == nq - 1))
        def _():
            dk_ref[...] = dk_ref[...] * (A_SCALE / A_QSCALE)

    qs = pl.BlockSpec((tq, HEAD_DIM), lambda kv, g, i: (i, kv * grp + g))
    kvs = lambda off: pl.BlockSpec((S, HEAD_DIM), lambda kv, g, i: (0, off + kv))
    return pl.pallas_call(
        body, name="flash_a_bwd", grid=(A_KV_HEADS, grp, S // tq),
        in_specs=[qs, kvs(A_HEADS), kvs(A_HEADS + A_KV_HEADS), qs, qs, qs],
        out_specs=[qs, kvs(0), kvs(0)],
        out_shape=[jax.ShapeDtypeStruct((S, A_HEADS * HEAD_DIM), F32),
                   jax.ShapeDtypeStruct((S, A_KV_HEADS * HEAD_DIM), F32),
                   jax.ShapeDtypeStruct((S, A_KV_HEADS * HEAD_DIM), F32)],
        compiler_params=_cp("arbitrary", "arbitrary", "arbitrary"))(qkvh, qkvh, qkvh, do, o, lse)


def _bucket_tables(transposed):
    hs = _half_span()
    tq, kv = 2 * hs, 4 * hs
    nb = REL_BUCKETS // 2
    max_exact = nb // 2
    shape = (kv, tq) if transposed else (tq, kv)
    out = np.zeros((len(B_GROUPS), 3) + shape, np.int32)
    win = np.arange(kv) - hs
    blk = np.arange(tq)
    for g, (_, dil) in enumerate(B_GROUPS):
        for case in range(3):
            inside = ((win >= 0) | (case != 0)) & ((win < tq) | (case != 2))
            if transposed:
                rel = blk[None, :] - win[:, None]
                ok = inside[:, None]
            else:
                rel = win[None, :] - blk[:, None]
                ok = inside[None, :]
            r = rel * dil
            n = np.abs(r)
            nf = np.maximum(n, 1).astype(np.float32)
            large = max_exact + (np.log(nf / np.float32(max_exact)) / np.float32(math.log(REL_MAX_DISTANCE / max_exact))
                                 * np.float32(nb - max_exact)).astype(np.int32)
            large = np.minimum(large, nb - 1)
            bucket = np.where(r > 0, nb, 0) + np.where(n < max_exact, n, large)
            out[g, case] = np.where((np.abs(rel) <= hs) & ok, bucket, -1)
    return out


def _bias_build(rel_bias, buckets):
    G, _, tq, kv = buckets.shape
    hg = B_HEADS_PER_GROUP

    def body(rb_ref, bk_ref, o_ref):
        col = pl.program_id(0) * hg + pl.program_id(2)
        bk = bk_ref[...]
        acc = jnp.full((tq, kv), NEG_INF, F32)
        for b in range(REL_BUCKETS):
            acc = jnp.where(bk == b, rb_ref[b, col], acc)
        o_ref[...] = acc

    return pl.pallas_call(
        body, name="bias_build", grid=(G, 3, hg),
        in_specs=[pl.BlockSpec(memory_space=pltpu.SMEM),
                  pl.BlockSpec((None, None, tq, kv), lambda g, c, h: (g, c, 0, 0))],
        out_specs=pl.BlockSpec((None, None, None, tq, kv), lambda g, c, h: (g, c, h, 0, 0)),
        out_shape=jax.ShapeDtypeStruct((G, 3, hg, tq, kv), F32),
        compiler_params=_cp("arbitrary", "arbitrary", "arbitrary"))(rel_bias, buckets)


def _bias_reduce(dbias_list, buckets):
    G, _, tq, kv = buckets.shape
    hg = B_HEADS_PER_GROUP
    n = len(dbias_list)

    def body(*refs):
        bk_ref, o_ref = refs[n], refs[n + 1]
        first = (pl.program_id(0) == 0) & (pl.program_id(1) == 0) & (pl.program_id(2) == 0)

        @pl.when(first)
        def _():
            o_ref[...] = jnp.zeros_like(o_ref)
        col = pl.program_id(0) * hg + pl.program_id(2)
        db = refs[0][...]
        for r in refs[1:n]:
            db = db + r[...]
        bk = bk_ref[...]
        rows = lax.broadcasted_iota(jnp.int32, (REL_BUCKETS, LANES), 0)
        cols = lax.broadcasted_iota(jnp.int32, (REL_BUCKETS, LANES), 1)
        acc = jnp.zeros((REL_BUCKETS, LANES), F32)
        for b in range(REL_BUCKETS):
            val = jnp.sum(jnp.sum(jnp.where(bk == b, db, 0.0), axis=1, keepdims=True), axis=0, keepdims=True)
            acc = acc + jnp.where((rows == b) & (cols == col), val, 0.0)
        o_ref[...] += acc

    tile = pl.BlockSpec((None, None, None, tq, kv), lambda g, c, h: (g, c, h, 0, 0))
    return pl.pallas_call(
        body, name="bias_reduce", grid=(G, 3, hg),
        in_specs=[tile] * n + [pl.BlockSpec((None, None, tq, kv), lambda g, c, h: (g, c, 0, 0))],
        out_specs=pl.BlockSpec((REL_BUCKETS, LANES), lambda g, c, h: (0, 0)),
        out_shape=jax.ShapeDtypeStruct((REL_BUCKETS, LANES), F32),
        compiler_params=_cp("arbitrary", "arbitrary", "arbitrary"))(*dbias_list, buckets)


def _mm_nn_perm(a, w, g):
    S, K = a.shape
    nq = w.shape[3]
    dil = B_GROUPS[g][1]
    wg3 = 3 * B_HEADS_PER_GROUP * HEAD_DIM
    tn = _lane_tile(math.gcd(nq, wg3), 768)
    nps, ntile = nq // tn, wg3 // tn
    tm = _pick(S, (1024, 512, 256))
    rows = tm // dil

    def body(a_ref, w_ref, o_ref, acc_ref):
        acc = lax.dot_general(a_ref[...], w_ref[...], _NN, preferred_element_type=F32)
        if dil == 1:
            o_ref[0] = acc.astype(BF16)
        else:
            for k in range(tn // LANES):
                acc_ref[k] = acc[:, k * LANES:(k + 1) * LANES]
            for c in range(dil):
                for k in range(tn // LANES):
                    o_ref[c, :, k * LANES:(k + 1) * LANES] = acc_ref[k, pl.ds(c, rows, stride=dil), :].astype(BF16)

    def w_map(i, j):
        t = g * ntile + j
        return (0, t // nps, 0, t % nps)

    return pl.pallas_call(
        body, name="b_qkv_g%d" % g, grid=(S // tm, ntile),
        in_specs=[pl.BlockSpec((tm, K), lambda i, j: (i, 0)), pl.BlockSpec((None, None, K, tn), w_map)],
        out_specs=pl.BlockSpec((dil, rows, tn), lambda i, j: (0, i, j)),
        out_shape=jax.ShapeDtypeStruct((dil, S // dil, wg3), BF16),
        scratch_shapes=[pltpu.VMEM((tn // LANES, tm, LANES), F32)],
        compiler_params=_cp("arbitrary", "arbitrary"))(a, w)


def _window_specs(S, wg, col):
    hs = _half_span()
    tq = 2 * hs
    per = tq // hs
    return (pl.BlockSpec((tq, wg), lambda i: (i, col)),
            pl.BlockSpec((hs, wg), lambda i: (jnp.maximum(i * per - 1, 0), col)),
            pl.BlockSpec((hs, wg), lambda i: (jnp.minimum((i + 1) * per, S // hs - 1), col)))


def _window_case(i, L):
    per = L // (2 * _half_span())
    r = i % per
    return jnp.where(r == 0, 0, jnp.where(r == per - 1, 2, 1))


def _window(prev_ref, main_ref, next_ref, sl):
    return jnp.concatenate([prev_ref[:, sl], main_ref[:, sl], next_ref[:, sl]], axis=0)


def _battn_fwd(qkvp, bias, g):
    dil, L, wg3 = qkvp.shape
    S = dil * L
    hs = _half_span()
    tq, kvl = 2 * hs, 4 * hs
    hg = B_HEADS_PER_GROUP
    wg = hg * HEAD_DIM
    scale = HEAD_DIM ** -0.5
    flat = qkvp.reshape(S, wg3)

    def body(q_ref, km, kp, kn, vm, vp, vn, b_ref, o_ref, lz_ref):
        case = _window_case(pl.program_id(0), L)
        for h in range(hg):
            sl = slice(h * HEAD_DIM, (h + 1) * HEAD_DIM)
            s = lax.dot_general(q_ref[:, sl], _window(kp, km, kn, sl), _NT, preferred_element_type=F32) * scale
            s = s + b_ref[case, h]
            m = jnp.max(s, axis=-1, keepdims=True)
            p = jnp.exp(s - m)
            l = jnp.sum(p, axis=-1, keepdims=True)
            o_ref[:, sl] = lax.dot_general(p.astype(BF16), _window(vp, vm, vn, sl), _NN, preferred_element_type=F32) / l
            lz_ref[:, sl] = jnp.broadcast_to(m + jnp.log(l), (tq, HEAD_DIM))

    blk = pl.BlockSpec((tq, wg), lambda i: (i, 0))
    o, lz = pl.pallas_call(
        body, name="battn_fwd_g%d" % g, grid=(S // tq,),
        in_specs=[_window_specs(S, wg, 0)[0], *_window_specs(S, wg, 1), *_window_specs(S, wg, 2),
                  pl.BlockSpec((None, 3, hg, tq, kvl), lambda i: (g, 0, 0, 0, 0))],
        out_specs=[blk, blk], out_shape=[jax.ShapeDtypeStruct((S, wg), F32)] * 2,
        compiler_params=_cp("arbitrary"))(flat, flat, flat, flat, flat, flat, flat, bias)
    return o, lz


def _battn_bwd_dq(qkvp, bias, do, o, lz, dlz, g):
    dil, L, wg3 = qkvp.shape
    S = dil * L
    hs = _half_span()
    tq, kvl = 2 * hs, 4 * hs
    hg = B_HEADS_PER_GROUP
    wg = hg * HEAD_DIM
    scale = HEAD_DIM ** -0.5
    flat = qkvp.reshape(S, wg3)

    def body(q_ref, km, kp, kn, vm, vp, vn, b_ref, do_ref, o_ref, lz_ref, dlz_ref, dq_ref, rt_ref, db_ref):
        i = pl.program_id(0)

        @pl.when(i == 0)
        def _():
            db_ref[...] = jnp.zeros_like(db_ref)
        case = _window_case(i, L)
        for h in range(hg):
            sl = slice(h * HEAD_DIM, (h + 1) * HEAD_DIM)
            kw = _window(kp, km, kn, sl)
            do_f = do_ref[:, sl]
            s = lax.dot_general(q_ref[:, sl], kw, _NT, preferred_element_type=F32) * scale + b_ref[case, h]
            p = jnp.exp(s - lz_ref[:, sl][:, 0:1])
            dp = lax.dot_general(do_f.astype(BF16), _window(vp, vm, vn, sl), _NT, preferred_element_type=F32)
            rt = dlz_ref[:, sl][:, 0:1] - jnp.sum(do_f * o_ref[:, sl], axis=-1, keepdims=True)
            ds = p * (dp + rt)
            db_ref[case, h] += ds
            dq_ref[:, sl] = lax.dot_general((ds * scale).astype(BF16), kw, _NN, preferred_element_type=F32)
            rt_ref[:, sl] = jnp.broadcast_to(rt, (tq, HEAD_DIM))

    blk = pl.BlockSpec((tq, wg), lambda i: (i, 0))
    row = jax.ShapeDtypeStruct((S, wg), F32)
    return pl.pallas_call(
        body, name="battn_bwd_dq_g%d" % g, grid=(S // tq,),
        in_specs=[_window_specs(S, wg, 0)[0], *_window_specs(S, wg, 1), *_window_specs(S, wg, 2),
                  pl.BlockSpec((None, 3, hg, tq, kvl), lambda i: (g, 0, 0, 0, 0)), blk, blk, blk, blk],
        out_specs=[blk, blk, pl.BlockSpec((3, hg, tq, kvl), lambda i: (0, 0, 0, 0))],
        out_shape=[row, row, jax.ShapeDtypeStruct((3, hg, tq, kvl), F32)],
        compiler_params=_cp("arbitrary"))(flat, flat, flat, flat, flat, flat, flat, bias, do, o, lz, dlz)


def _battn_bwd_dkv(qkvp, bias_t, do, lz, rt, g):
    dil, L, wg3 = qkvp.shape
    S = dil * L
    hs = _half_span()
    tq, kvl = 2 * hs, 4 * hs
    hg = B_HEADS_PER_GROUP
    wg = hg * HEAD_DIM
    scale = HEAD_DIM ** -0.5
    flat = qkvp.reshape(S, wg3)

    def body(k_ref, v_ref, qm, qp, qn, dom, dop, don, lzm, lzp, lzn, rtm, rtp, rtn, b_ref, dk_ref, dv_ref):
        case = _window_case(pl.program_id(0), L)
        for h in range(hg):
            sl = slice(h * HEAD_DIM, (h + 1) * HEAD_DIM)
            qw = _window(qp, qm, qn, sl)
            dow = _window(dop, dom, don, sl).astype(BF16)
            s = lax.dot_general(qw, k_ref[:, sl], _NT, preferred_element_type=F32) * scale + b_ref[case, h]
            p = jnp.exp(s - _window(lzp, lzm, lzn, sl)[:, 0:1])
            dp = lax.dot_general(dow, v_ref[:, sl], _NT, preferred_element_type=F32)
            ds_b = (p * (dp + _window(rtp, rtm, rtn, sl)[:, 0:1]) * scale).astype(BF16)
            dk_ref[:, sl] = lax.dot_general(ds_b, qw, _TN, preferred_element_type=F32)
            dv_ref[:, sl] = lax.dot_general(p.astype(BF16), dow, _TN, preferred_element_type=F32)

    blk = pl.BlockSpec((tq, wg), lambda i: (i, 0))
    row = jax.ShapeDtypeStruct((S, wg), F32)
    return pl.pallas_call(
        body, name="battn_bwd_dkv_g%d" % g, grid=(S // tq,),
        in_specs=[_window_specs(S, wg, 1)[0], _window_specs(S, wg, 2)[0], *_window_specs(S, wg, 0),
                  *_window_specs(S, wg, 0), *_window_specs(S, wg, 0), *_window_specs(S, wg, 0),
                  pl.BlockSpec((None, 3, hg, kvl, tq), lambda i: (g, 0, 0, 0, 0))],
        out_specs=[blk, blk], out_shape=[row, row],
        compiler_params=_cp("arbitrary"))(flat, flat, flat, flat, flat, do, do, do, lz, lz, lz, rt, rt, rt, bias_t)


def _group_weights(lz_refs, h):
    z = [r[h] for r in lz_refs]
    mx = functools.reduce(jnp.maximum, z)
    e = [jnp.exp(v - mx) for v in z]
    inv = 1.0 / functools.reduce(lambda a, b: a + b, e)
    return [v * inv for v in e]


def _to_token_order(src_ref, dst_ref, dil):
    rows = src_ref.shape[1]
    for k in range(dst_ref.shape[0]):
        sl = slice(k * LANES, (k + 1) * LANES)
        if dil == 1:
            dst_ref[k] = src_ref[0, :, sl]
        else:
            for c in range(dil):
                dst_ref[k, pl.ds(c, rows, stride=dil), :] = src_ref[c, :, sl]


def _to_subsequence_order(src_ref, dst_ref, dil):
    rows = dst_ref.shape[1]
    for k in range(src_ref.shape[0]):
        sl = slice(k * LANES, (k + 1) * LANES)
        if dil == 1:
            dst_ref[0, :, sl] = src_ref[k]
        else:
            for c in range(dil):
                dst_ref[c, :, sl] = src_ref[k, pl.ds(c, rows, stride=dil), :]


def _sub_view(a, dil):
    S, w = a.shape
    return a.reshape(dil, S // dil, w)


def _sub_spec(dil, ts, w):
    return pl.BlockSpec((dil, ts // dil, w), lambda i: (0, i, 0))


def _combine_fwd(os_, lzs):
    G = len(os_)
    S, Wg = os_[0].shape
    hg = B_HEADS_PER_GROUP
    dils = [d for _, d in B_GROUPS]
    ts = _pick(S, (256, 128))

    def body(*refs):
        o_in, lz_in, y_ref = refs[:G], refs[G:2 * G], refs[2 * G]
        o_nat, lz_nat = refs[2 * G + 1:3 * G + 1], refs[3 * G + 1:4 * G + 1]
        for g in range(G):
            _to_token_order(o_in[g], o_nat[g], dils[g])
            _to_token_order(lz_in[g], lz_nat[g], dils[g])
        for h in range(hg):
            w = _group_weights(lz_nat, h)
            for g in range(G):
                y_ref[:, (g * hg + h) * HEAD_DIM:(g * hg + h + 1) * HEAD_DIM] = (w[g] * o_nat[g][h]).astype(BF16)

    specs = [_sub_spec(d, ts, Wg) for d in dils]
    return pl.pallas_call(
        body, name="combine_fwd", grid=(S // ts,), in_specs=specs + specs,
        out_specs=pl.BlockSpec((ts, G * Wg), lambda i: (i, 0)),
        out_shape=jax.ShapeDtypeStruct((S, G * Wg), BF16),
        scratch_shapes=[pltpu.VMEM((hg, ts, HEAD_DIM), F32)] * (2 * G),
        compiler_params=_cp("arbitrary"))(*[_sub_view(a, d) for a, d in zip(os_, dils)],
                                          *[_sub_view(a, d) for a, d in zip(lzs, dils)])


def _combine_bwd(dy, os_, lzs):
    G = len(os_)
    S, Wg = os_[0].shape
    hg = B_HEADS_PER_GROUP
    dils = [d for _, d in B_GROUPS]
    ts = _pick(S, (128,))

    def body(*refs):
        dy_ref, o_in, lz_in = refs[0], refs[1:1 + G], refs[1 + G:1 + 2 * G]
        do_out, dlz_out = refs[1 + 2 * G:1 + 3 * G], refs[1 + 3 * G:1 + 4 * G]
        scr = refs[1 + 4 * G:]
        o_nat, lz_nat, do_nat, dlz_nat = scr[:G], scr[G:2 * G], scr[2 * G:3 * G], scr[3 * G:4 * G]
        for g in range(G):
            _to_token_order(o_in[g], o_nat[g], dils[g])
            _to_token_order(lz_in[g], lz_nat[g], dils[g])
        for h in range(hg):
            w = _group_weights(lz_nat, h)
            dw = []
            for g in range(G):
                dyg = dy_ref[:, (g * hg + h) * HEAD_DIM:(g * hg + h + 1) * HEAD_DIM]
                dw.append(jnp.sum(dyg * o_nat[g][h], axis=-1, keepdims=True))
                do_nat[g][h] = w[g] * dyg
            tot = functools.reduce(lambda a, b: a + b, [w[g] * dw[g] for g in range(G)])
            for g in range(G):
                dlz_nat[g][h] = w[g] * (dw[g] - tot)
        for g in range(G):
            _to_subsequence_order(do_nat[g], do_out[g], dils[g])
            _to_subsequence_order(dlz_nat[g], dlz_out[g], dils[g])

    specs = [_sub_spec(d, ts, Wg) for d in dils]
    outs = pl.pallas_call(
        body, name="combine_bwd", grid=(S // ts,),
        in_specs=[pl.BlockSpec((ts, G * Wg), lambda i: (i, 0))] + specs + specs,
        out_specs=specs + specs,
        out_shape=[jax.ShapeDtypeStruct((d, S // d, Wg), F32) for d in dils] * 2,
        scratch_shapes=[pltpu.VMEM((hg, ts, HEAD_DIM), F32)] * (4 * G),
        compiler_params=_cp("arbitrary"))(dy, *[_sub_view(a, d) for a, d in zip(os_, dils)],
                                          *[_sub_view(a, d) for a, d in zip(lzs, dils)])
    flat = [a.reshape(S, Wg) for a in outs]
    return flat[:G], flat[G:]


def _concat_cast(parts, dils):
    S = parts[0].shape[0]
    widths = [p.shape[1] for p in parts]
    n = len(parts)
    ts = _pick(S, (256, 128))

    def body(*refs):
        o_ref, nat = refs[n], refs[n + 1]
        off = 0
        for r, w, d in zip(refs, widths, dils):
            _to_token_order(r, nat, d)
            for k in range(w // LANES):
                o_ref[:, off + k * LANES:off + (k + 1) * LANES] = nat[k].astype(BF16)
            off += w

    assert len(set(widths)) == 1
    return pl.pallas_call(
        body, name="concat_cast", grid=(S // ts,),
        in_specs=[_sub_spec(d, ts, w) for w, d in zip(widths, dils)],
        out_specs=pl.BlockSpec((ts, sum(widths)), lambda i: (i, 0)),
        out_shape=jax.ShapeDtypeStruct((S, sum(widths)), BF16),
        scratch_shapes=[pltpu.VMEM((widths[0] // LANES, ts, LANES), F32)],
        compiler_params=_cp("arbitrary"))(*[_sub_view(p, d) for p, d in zip(parts, dils)])


def _ffn_specs(S, dff, cq, ts, tc, layer, order):
    nfc = dff // tc
    nps = cq // tc
    hb = ts // SUBLANES
    nrow8 = S // SUBLANES

    def u_main(half):
        return pl.BlockSpec((ts, tc), lambda *g: (order(*g)[0], order(*g)[1] % nfc + half * nfc))

    def u_prev(half):
        return pl.BlockSpec((SUBLANES, tc), lambda *g: (jnp.maximum(order(*g)[0] * hb - 1, 0),
                                                         order(*g)[1] % nfc + half * nfc))

    def u_next(half):
        return pl.BlockSpec((SUBLANES, tc), lambda *g: (jnp.minimum((order(*g)[0] + 1) * hb, nrow8 - 1),
                                                         order(*g)[1] % nfc + half * nfc))

    def cw(half):
        def im(*g):
            jj = order(*g)[1] % nfc + half * nfc
            return (layer, jj // nps, 0, jj % nps)
        return pl.BlockSpec((None, None, 3, tc), im)

    def cb(half):
        return pl.BlockSpec((None, 1, tc), lambda *g: (layer, 0, order(*g)[1] % nfc + half * nfc))

    return nfc, u_main, u_prev, u_next, cw, cb


def _ffn_act_fwd(u, cw_full, cb3, layer):
    S, two_dff = u.shape
    dff = two_dff // 2
    cq = cw_full.shape[3]
    ts = _pick(S, (1024, 512, 256, 128, 64, 32, 16))
    tc = _pick(cq, (256, 128))
    order = lambda j, i: (i, j)
    nfc, u_main, u_prev, u_next, cw, cb = _ffn_specs(S, dff, cq, ts, tc, layer, order)
    nrow = S // ts

    def body(ug, ugp, ugn, uv, uvp, uvn, wg, wv, bg, bv, a_ref):
        i = pl.program_id(1)
        row = lax.broadcasted_iota(jnp.int32, (ts, tc), 0)

        def conv(x_ref, p_ref, n_ref, w_ref, b_ref):
            x = x_ref[...]
            prev = jnp.where(i > 0, p_ref[SUBLANES - 1:SUBLANES, :], 0.0)
            nxt = jnp.where(i < nrow - 1, n_ref[0:1, :], 0.0)
            xm = jnp.where(row == 0, prev, pltpu.roll(x, 1, 0))
            xp = jnp.where(row == ts - 1, nxt, pltpu.roll(x, ts - 1, 0))
            return w_ref[0:1, :] * xm + w_ref[1:2, :] * x + w_ref[2:3, :] * xp + b_ref[...]

        gc = conv(ug, ugp, ugn, wg, bg)
        vc = conv(uv, uvp, uvn, wv, bv)
        a_ref[...] = (gc * (1.0 / (1.0 + jnp.exp(-gc))) * vc).astype(BF16)

    return pl.pallas_call(
        body, name="ffn_act_fwd", grid=(nfc, nrow),
        in_specs=[u_main(0), u_prev(0), u_next(0), u_main(1), u_prev(1), u_next(1), cw(0), cw(1), cb(0), cb(1)],
        out_specs=pl.BlockSpec((ts, tc), lambda j, i: (i, j)),
        out_shape=jax.ShapeDtypeStruct((S, dff), BF16),
        compiler_params=_cp("arbitrary", "arbitrary"))(u, u, u, u, u, u, cw_full, cw_full, cb3, cb3)


def _ffn_act_bwd(u, da, cw_full, cb3, layer):
    S, two_dff = u.shape
    dff = two_dff // 2
    cq = cw_full.shape[3]
    ts = _pick(S, (1024, 512, 256, 128, 64, 32, 16))
    tc = _pick(cq, (256, 128))
    order = lambda j, i: (i, j)
    nfc, u_main, u_prev, u_next, cw, cb = _ffn_specs(S, dff, cq, ts, tc, layer, order)
    nrow = S // ts
    hb = ts // SUBLANES
    te = ts + 2 * SUBLANES
    da_main = pl.BlockSpec((ts, tc), lambda j, i: (i, j))
    da_prev = pl.BlockSpec((SUBLANES, tc), lambda j, i: (jnp.maximum(i * hb - 1, 0), j))
    da_next = pl.BlockSpec((SUBLANES, tc), lambda j, i: (jnp.minimum((i + 1) * hb, S // SUBLANES - 1), j))
    main = slice(SUBLANES, SUBLANES + ts)

    def body(ug, ugp, ugn, uv, uvp, uvn, dam, dap, dan, wg, wv, bg, bv, dug_ref, duv_ref, accg_ref, accv_ref):
        i = pl.program_id(1)

        @pl.when(i == 0)
        def _():
            accg_ref[...] = jnp.zeros_like(accg_ref)
            accv_ref[...] = jnp.zeros_like(accv_ref)

        def ext(m, p, n):
            return jnp.concatenate([jnp.where(i > 0, p[...], 0.0), m[...], jnp.where(i < nrow - 1, n[...], 0.0)], axis=0)

        def shift(x):
            return pltpu.roll(x, 1, 0), pltpu.roll(x, te - 1, 0)

        xg, xv, dae = ext(ug, ugp, ugn), ext(uv, uvp, uvn), ext(dam, dap, dan)
        xgm, xgp = shift(xg)
        xvm, xvp = shift(xv)
        gc = wg[0:1, :] * xgm + wg[1:2, :] * xg + wg[2:3, :] * xgp + bg[...]
        vc = wv[0:1, :] * xvm + wv[1:2, :] * xv + wv[2:3, :] * xvp + bv[...]
        sig = 1.0 / (1.0 + jnp.exp(-gc))
        silu = gc * sig
        dcg = dae * vc * (sig * (1.0 + gc * (1.0 - sig)))
        dcv = dae * silu

        def finish(dc, x, xm, xp, w_ref, du_ref, acc_ref):
            dm, dp = shift(dc)
            du = w_ref[0:1, :] * dp + w_ref[1:2, :] * dc + w_ref[2:3, :] * dm
            du_ref[...] = du[main, :].astype(BF16)
            dcm = dc[main, :]
            acc_ref[0:1, :] += jnp.sum(dcm * xm[main, :], axis=0, keepdims=True)
            acc_ref[1:2, :] += jnp.sum(dcm * x[main, :], axis=0, keepdims=True)
            acc_ref[2:3, :] += jnp.sum(dcm * xp[main, :], axis=0, keepdims=True)
            acc_ref[3:4, :] += jnp.sum(dcm, axis=0, keepdims=True)

        finish(dcg, xg, xgm, xgp, wg, dug_ref, accg_ref)
        finish(dcv, xv, xvm, xvp, wv, duv_ref, accv_ref)

    blk = pl.BlockSpec((ts, tc), lambda j, i: (i, j))
    acc = pl.BlockSpec((SUBLANES, tc), lambda j, i: (0, j))
    dug, duv, accg, accv = pl.pallas_call(
        body, name="ffn_act_bwd", grid=(nfc, nrow),
        in_specs=[u_main(0), u_prev(0), u_next(0), u_main(1), u_prev(1), u_next(1), da_main, da_prev, da_next,
                  cw(0), cw(1), cb(0), cb(1)],
        out_specs=[blk, blk, acc, acc],
        out_shape=[jax.ShapeDtypeStruct((S, dff), BF16)] * 2 + [jax.ShapeDtypeStruct((SUBLANES, dff), F32)] * 2,
        compiler_params=_cp("arbitrary", "arbitrary"))(u, u, u, u, u, u, da, da, da, cw_full, cw_full, cb3, cb3)
    return (dug, duv), jnp.concatenate([accg, accv], axis=1)


def _my_chip():
    return 2 * lax.axis_index("x") + lax.axis_index("y")


def _into_full(w, layer, dtype):
    L, a, b = w.shape
    tr = _pick(a, (512, 256, 128, 64, 32, 16, 8))

    def body(w_ref, o_ref):
        o_ref[...] = w_ref[...].astype(dtype)

    return pl.pallas_call(
        body, name="into_full", grid=(a // tr,),
        in_specs=[pl.BlockSpec((None, tr, b), lambda i: (layer, i, 0))],
        out_specs=pl.BlockSpec((None, None, tr, b), lambda i: (0, _my_chip(), i, 0)),
        out_shape=jax.ShapeDtypeStruct((1, N_CHIPS, a, b), dtype),
        compiler_params=_cp("arbitrary"))(w)


def _adam_math(w, g, m, v):
    m = ADAM_B1 * m + (1.0 - ADAM_B1) * g
    v = ADAM_B2 * v + (1.0 - ADAM_B2) * (g * g)
    m_hat = m / (1.0 - ADAM_B1 ** ADAM_STEP)
    v_hat = v / (1.0 - ADAM_B2 ** ADAM_STEP)
    delta = -ADAM_LR * (m_hat / (jnp.sqrt(v_hat) + ADAM_EPS) + ADAM_WD * w)
    return delta, m, v


def _adamw(w, g, m, v, keep_g=False):
    R, C = w.shape
    tr = _pick(R, (128, 64, 32, 16, 8)) if R % SUBLANES == 0 and C % LANES == 0 else R
    n_out = 4 if keep_g else 3

    def body(w_ref, g_ref, m_ref, v_ref, d_ref, nm_ref, nv_ref, *rest):
        g = g_ref[...]
        d, nm, nv = _adam_math(w_ref[...], g, m_ref[...], v_ref[...])
        d_ref[...] = d
        nm_ref[...] = nm
        nv_ref[...] = nv
        if keep_g:
            rest[0][...] = g

    spec = pl.BlockSpec((tr, C), lambda i: (i, 0))
    return pl.pallas_call(
        body, name="adamw", grid=(R // tr,), in_specs=[spec] * 4, out_specs=[spec] * n_out,
        out_shape=[jax.ShapeDtypeStruct((R, C), F32)] * n_out, compiler_params=_cp("arbitrary"))(w, g, m, v)


ANY = pl.BlockSpec(memory_space=pl.ANY)


def _position():
    x, y, c = lax.axis_index("x"), lax.axis_index("y"), lax.axis_index("c")
    chips = [(1 - x, y), (x, 1 - y), (1 - x, 1 - y)]
    return x, y, c, chips


HBM = pl.BlockSpec(memory_space=pltpu.HBM)
SEM = pl.BlockSpec(memory_space=pltpu.SEMAPHORE)
EFFECT = pltpu.SideEffectType.DATAFLOW_SIDE_EFFECTING


def _in_hbm(a):
    return pltpu.with_memory_space_constraint(a, pltpu.HBM)


def _shard_half(buf, shape, chip, half):
    _, _, a, b = shape
    p = 2 * chip[0] + chip[1]
    if a % (4 * SUBLANES) == 0:
        return buf.at[0, p, pl.ds(half * (a // 2), a // 2)]
    return buf.at[0, p, :, pl.ds(half * (b // 2), b // 2)]


def _gather_copy(buf, shape, chip, half, to, send, recv, k):
    part = _shard_half(buf, shape, chip, half)
    return pltpu.make_async_remote_copy(src_ref=part, dst_ref=part, send_sem=send.at[k], recv_sem=recv.at[k],
                                        device_id=to, device_id_type=MESH)


def _gather_hop(buf, shape, hop, j, incoming, send, recv, k):
    x, y, c, chips = _position()
    if hop == "chips":
        chip, half, to = (chips[j] if incoming else (x, y)), c, (*chips[j], c)
    else:
        chip, half, to = chips[j], (1 - c if incoming else c), (x, y, 1 - c)
    return _gather_copy(buf, shape, chip, half, to, send, recv, k)


def _gather_start(fulls, name, hop, after=()):
    n = len(fulls)
    na = len(after)

    def body(*refs):
        send, recv = refs[n + na], refs[n + na + 1]
        buf, token = refs[n + na + 2:2 * n + na + 2], refs[2 * n + na + 2]
        for t in range(n):
            for j in range(3):
                _gather_hop(buf[t], fulls[t].shape, hop, j, False, send, recv, 3 * t + j).start()
        token[...] = jnp.zeros_like(token)

    outs = pl.pallas_call(
        body, name=name, in_specs=[HBM] * n + [ANY] * na,
        out_specs=[SEM, SEM] + [HBM] * n + [pl.BlockSpec(memory_space=pltpu.VMEM)],
        out_shape=[pltpu.SemaphoreType.DMA((3 * n,)), pltpu.SemaphoreType.DMA((3 * n,))]
        + [pltpu.HBM(f.shape, f.dtype) for f in fulls] + [jax.ShapeDtypeStruct((SUBLANES, LANES), F32)],
        input_output_aliases={t: 2 + t for t in range(n)},
        compiler_params=pltpu.CompilerParams(has_side_effects=EFFECT))(*[_in_hbm(f) for f in fulls], *after)
    return outs[0], outs[1], list(outs[2:2 + n]), outs[2 + n]


def _gather_wait(send, recv, fulls, after, name, hop):
    n = len(fulls)

    def body(*refs):
        buf, send_ref, recv_ref = refs[:n], refs[n], refs[n + 1]
        for t in range(n):
            for j in range(3):
                _gather_hop(buf[t], fulls[t].shape, hop, j, False, send_ref, recv_ref, 3 * t + j).wait_send()
                _gather_hop(buf[t], fulls[t].shape, hop, j, True, send_ref, recv_ref, 3 * t + j).wait_recv()

    outs = pl.pallas_call(
        body, name=name, in_specs=[HBM] * n + [SEM, SEM] + [ANY] * len(after), out_specs=[HBM] * n,
        out_shape=[pltpu.HBM(f.shape, f.dtype) for f in fulls],
        input_output_aliases={t: t for t in range(n)},
        compiler_params=pltpu.CompilerParams(has_side_effects=EFFECT))(*fulls, send, recv, *after)
    return list(outs)


def _allreduce_small(part):
    M, C = part.shape
    n_dev = 2 * N_CHIPS

    def body(x_ref, sum_ref, all_ref, send, recv, local):
        x, y, c, chips = _position()
        me, sib = (x, y, c), (x, y, 1 - c)

        def rows(px, py, pc):
            return all_ref.at[pl.ds((4 * px + 2 * py + pc) * M, M), :]

        def copy(k, block, to, src=None):
            return pltpu.make_async_remote_copy(
                src_ref=rows(*block) if src is None else src, dst_ref=rows(*block),
                send_sem=send.at[k], recv_sem=recv.at[k], device_id=to, device_id_type=MESH)

        mine = pltpu.make_async_copy(x_ref, rows(*me), local)
        mine.start()
        first = [copy(0, me, sib, src=x_ref)] + [copy(1 + j, me, (*chip, c), src=x_ref) for j, chip in enumerate(chips)]
        for cp in first:
            cp.start()
        passed = [copy(4 + j, (*chip, c), sib) for j, chip in enumerate(chips)]
        for j, chip in enumerate(chips):
            copy(1 + j, (*chip, c), me).wait_recv()
            passed[j].start()
        copy(0, sib, me).wait_recv()
        for j, chip in enumerate(chips):
            copy(4 + j, (*chip, 1 - c), me).wait_recv()
        for cp in first + passed:
            cp.wait_send()
        mine.wait()
        acc = all_ref[0:M, :]
        for d in range(1, n_dev):
            acc = acc + all_ref[d * M:(d + 1) * M, :]
        sum_ref[...] = acc

    vm = pl.BlockSpec(memory_space=pltpu.VMEM)
    return pl.pallas_call(
        body, name="allreduce_small", in_specs=[vm], out_specs=[vm],
        out_shape=[jax.ShapeDtypeStruct((M, C), F32)],
        scratch_shapes=[pltpu.VMEM((n_dev * M, C), F32), pltpu.SemaphoreType.DMA((7,)),
                        pltpu.SemaphoreType.DMA((7,)), pltpu.SemaphoreType.DMA],
        compiler_params=pltpu.CompilerParams(vmem_limit_bytes=VMEM_LIMIT))(part)[0]


N_PEERS = 2 * N_CHIPS - 1


def _peers():
    x, y, c, chips = _position()
    return [(x, y, 1 - c)] + [(*ch, c) for ch in chips] + [(*ch, 1 - c) for ch in chips]


def _reduce_copy(src, dst, peers, send, recv, t, r):
    px, py, pc = peers[r]
    return pltpu.make_async_remote_copy(
        src_ref=src.at[2 * px + py, pc], dst_ref=dst.at[r], send_sem=send.at[N_PEERS * t + r],
        recv_sem=recv.at[N_PEERS * t + r], device_id=peers[r], device_id_type=MESH)


def _reduce_start(grads, name, after=()):
    n = len(grads)
    na = len(after)
    lands = [lax.empty((N_PEERS,) + g.shape[2:], BF16) for g in grads]

    def body(*refs):
        send, recv = refs[2 * n + na], refs[2 * n + na + 1]
        src, dst = refs[2 * n + na + 2:3 * n + na + 2], refs[3 * n + na + 2:4 * n + na + 2]
        token = refs[4 * n + na + 2]
        peers = _peers()
        for t in range(n):
            for r in range(N_PEERS):
                _reduce_copy(src[t], dst[t], peers, send, recv, t, r).start()
        token[...] = jnp.zeros_like(token)

    outs = pl.pallas_call(
        body, name=name, in_specs=[HBM] * (2 * n) + [ANY] * na,
        out_specs=[SEM, SEM] + [HBM] * (2 * n) + [pl.BlockSpec(memory_space=pltpu.VMEM)],
        out_shape=[pltpu.SemaphoreType.DMA((N_PEERS * n,)), pltpu.SemaphoreType.DMA((N_PEERS * n,))]
        + [pltpu.HBM(a.shape, a.dtype) for a in grads + lands] + [jax.ShapeDtypeStruct((SUBLANES, LANES), F32)],
        input_output_aliases={t: 2 + t for t in range(2 * n)},
        compiler_params=pltpu.CompilerParams(has_side_effects=EFFECT))(*[_in_hbm(a) for a in grads + lands], *after)
    return outs[0], outs[1], list(outs[2:2 + n]), list(outs[2 + n:2 + 2 * n]), outs[2 + 2 * n]


def _reduce_wait(send, recv, grads, lands, after, name):
    n = len(grads)

    def body(*refs):
        src, dst, send_ref, recv_ref = refs[:n], refs[n:2 * n], refs[2 * n], refs[2 * n + 1]
        peers = _peers()
        for t in range(n):
            for r in range(N_PEERS):
                cp = _reduce_copy(src[t], dst[t], peers, send_ref, recv_ref, t, r)
                cp.wait_send()
                cp.wait_recv()

    outs = pl.pallas_call(
        body, name=name, in_specs=[HBM] * (2 * n) + [SEM, SEM] + [ANY] * len(after), out_specs=[HBM] * (2 * n),
        out_shape=[pltpu.HBM(a.shape, a.dtype) for a in grads + lands],
        input_output_aliases={t: t for t in range(2 * n)},
        compiler_params=pltpu.CompilerParams(has_side_effects=EFFECT))(*grads, *lands, send, recv, *after)
    return list(outs[:n]), list(outs[n:])


def _add_pieces(grad, land, stack, layer):
    _, _, R, C = grad.shape
    tr = _pick(R, (256, 128, 64, 32, 16))

    def body(g_ref, r_ref, stack_ref, o_ref):
        acc = g_ref[...].astype(F32)
        for r in range(N_PEERS):
            acc = acc + r_ref[r].astype(F32)
        o_ref[...] = acc

    return pl.pallas_call(
        body, name="add_pieces", grid=(R // tr,),
        in_specs=[pl.BlockSpec((None, None, tr, C), lambda i: (_my_chip(), lax.axis_index("c"), i, 0)),
                  pl.BlockSpec((N_PEERS, tr, C), lambda i: (0, i, 0)),
                  ANY],
        out_specs=pl.BlockSpec((None, None, tr, C), lambda i: (layer, lax.axis_index("c"), i, 0)),
        out_shape=jax.ShapeDtypeStruct(stack.shape, F32), input_output_aliases={2: 0},
        compiler_params=_cp("arbitrary"))(grad, land, stack)


def _ag_sibling(stacks):
    n = len(stacks)
    offs = np.cumsum([0] + [s.shape[0] for s in stacks])

    def body(*refs):
        buf, send, recv = refs[n:2 * n], refs[2 * n], refs[2 * n + 1]
        x, y, c, _ = _position()

        def copy(t, l, half):
            part = buf[t].at[l, half]
            return pltpu.make_async_remote_copy(
                src_ref=part, dst_ref=part, send_sem=send.at[int(offs[t]) + l], recv_sem=recv.at[int(offs[t]) + l],
                device_id=(x, y, 1 - c), device_id_type=MESH)

        cps = [copy(t, l, c) for t in range(n) for l in range(stacks[t].shape[0])]
        for cp in cps:
            cp.start()
        for t in range(n):
            for l in range(stacks[t].shape[0]):
                copy(t, l, 1 - c).wait_recv()
        for cp in cps:
            cp.wait_send()

    return pl.pallas_call(
        body, name="ag_sibling", in_specs=[ANY] * n, out_specs=[ANY] * n,
        out_shape=[jax.ShapeDtypeStruct(s.shape, F32) for s in stacks],
        input_output_aliases={t: t for t in range(n)},
        scratch_shapes=[pltpu.SemaphoreType.DMA((int(offs[-1]),)), pltpu.SemaphoreType.DMA((int(offs[-1]),))])(*stacks)


def _split8(dw, blocked):
    if blocked:
        p, k, nq = dw.shape
        return dw.reshape(p, 2, k // 2, nq)
    k, n = dw.shape
    return dw.reshape(N_CHIPS, 2, k // (2 * N_CHIPS), n)


def kernel(x, a_w_qkv, a_w_o, a_q_gain, a_k_gain, b_w_qkv, b_w_o, rel_bias, mix_norm, ffn_norm, w_up, conv_w, conv_b, w_down, final_norm, loss_target, m_a_w_qkv, m_a_w_o, m_a_q_gain, m_a_k_gain, m_b_w_qkv, m_b_w_o, m_rel_bias, m_mix_norm, m_ffn_norm, m_w_up, m_conv_w, m_conv_b, m_w_down, m_final_norm, v_a_w_qkv, v_a_w_o, v_a_q_gain, v_a_k_gain, v_b_w_qkv, v_b_w_o, v_rel_bias, v_mix_norm, v_ffn_norm, v_w_up, v_conv_w, v_conv_b, v_w_down, v_final_norm):
    S, D = x.shape[1], x.shape[2]
    h = x.reshape(S, D)
    target = loss_target.reshape(S, D)
    hg = B_HEADS_PER_GROUP
    G = len(B_GROUPS)
    n_a, n_b = a_w_qkv.shape[0], b_w_qkv.shape[0]
    depth = w_up.shape[0]
    cx, cy = lax.axis_index("x"), lax.axis_index("y")

    big = dict(a_w_qkv=a_w_qkv, a_w_o=a_w_o, b_w_qkv=b_w_qkv, b_w_o=b_w_o, w_up=w_up, w_down=w_down)
    blocked = dict(a_w_qkv=True, a_w_o=False, b_w_qkv=True, b_w_o=False, w_up=True, w_down=False)
    names = list(big)
    srcs = dict(big, conv_w=conv_w)
    started = []
    for i in range(depth):
        mix = [("a_w_qkv", i // 2), ("a_w_o", i // 2)] if i % 2 == 0 else [("b_w_qkv", i // 2), ("b_w_o", i // 2)]
        rest = [("w_up", i), ("conv_w", i), ("w_down", i)]
        stages = [mix[:1], mix[1:], rest] if i == 0 else [mix + rest]
        started.append([])
        for s, keys in enumerate(stages):
            bufs = [_into_full(srcs[k], l, F32 if k == "conv_w" else BF16) for k, l in keys]
            started[i].append((keys,) + _gather_start(bufs, "gather_start_%d_%d" % (i, s), "chips"))
    cb3 = conv_b.reshape(depth, 1, conv_b.shape[1])

    cos, sin = _rope_tables(S)
    buckets = jnp.asarray(_bucket_tables(False))
    bias = _bias_build(rel_bias, buckets)
    bias_t = _bias_build(rel_bias, jnp.asarray(_bucket_tables(True)))

    saved = []
    passing = {}

    def land(i, s, after):
        keys, send, recv, bufs, _ = started[i][s]
        bufs = _gather_wait(send, recv, bufs, after, "gather_wait_%d_%d" % (i, s), "chips")
        send, recv, bufs, token = _gather_start(bufs, "pass_start_%d_%d" % (i, s), "sibling")
        passing[i, s] = (keys, send, recv, bufs)
        return token

    def arrive(i, s, after, wl):
        keys, send, recv, bufs = passing.pop((i, s))
        bufs = _gather_wait(send, recv, bufs, after, "pass_wait_%d_%d" % (i, s), "sibling")
        for (k, _), buf in zip(keys, bufs):
            _, _, a, b = buf.shape
            wl[k] = buf if k == "conv_w" or blocked[k] else buf.reshape(1, N_CHIPS * a, b)

    first = [land(0, 0, [h])]
    for i in range(depth):
        j = i // 2
        wl = {}
        arrive(i, 0, [h], wl)
        sv = dict(h0=h, w=wl)
        hn = _rms_fwd(h, mix_norm[i:i + 1], after=[st[4] for layer in started for st in layer] + first if i == 0 else ())
        sv["hn"] = hn
        if i % 2 == 0:
            qkv = _mm_nn(hn, wl["a_w_qkv"], 0, blocked=True, name="a_qkv")
            qkvh = _prep_a_fwd(qkv, cos, sin, a_q_gain[j:j + 1], a_k_gain[j:j + 1])
            staged = len(started[i]) > 1
            o, lse = _flash_a_fwd(qkvh, after=[land(i, 1, [qkvh])] if staged else ())
            tok = ()
            if staged:
                arrive(i, 1, [o], wl)
                tok = [land(i, 2, [o])]
            sv.update(qkv=qkv, qkvh=qkvh, o=o, lse=lse)
            h = _mm_nn(o, wl["a_w_o"], 0, blocked=False, res=h, name="a_out", after=tok)
        else:
            qkvp = [_mm_nn_perm(hn, wl["b_w_qkv"], g) for g in range(G)]
            os_, lzs = [], []
            for g in range(G):
                o_g, lz_g = _battn_fwd(qkvp[g], bias, g)
                os_.append(o_g)
                lzs.append(lz_g)
            y = _combine_fwd(os_, lzs)
            sv.update(qkvp=qkvp, os=os_, lzs=lzs, y=y)
            h = _mm_nn(y, wl["b_w_o"], 0, blocked=False, res=h, name="b_out")
        sv["h1"] = h
        hf = _rms_fwd(h, ffn_norm[i:i + 1])
        if len(started[i]) > 2:
            arrive(i, 2, [hf], wl)
        u = _mm_nn(hf, wl["w_up"], 0, blocked=True, name="ffn_up")
        act = _ffn_act_fwd(u, wl["conv_w"], cb3[i:i + 1], 0)
        sv.update(hf=hf, u=u, act=act)
        nxt = [land(i + 1, 0, [act])] if i + 1 < depth else ()
        h = _mm_nn(act, wl["w_down"], 0, blocked=False, res=h, name="ffn_down", after=nxt)
        saved.append(sv)

    loss_blk, dh, dh_b, dg_final = _final_loss(h, final_norm.reshape(1, D), target)

    dws = {k: [None] * big[k].shape[0] for k in names}
    d_mix, d_ffn, d_convw, d_convb = [None] * depth, [None] * depth, [None] * depth, [None] * depth
    d_gq, d_gk = [None] * n_a, [None] * n_a
    dbias_list = []
    pending = []

    def start_reduce(keys, tag, after=()):
        pieces = [_split8(dws[k][l], blocked[k]) for k, l in keys]
        send, recv, pieces, lands, token = _reduce_start(pieces, "reduce_start_" + tag, after)
        pending.append((keys, send, recv, pieces, lands, tag))
        return (token,)

    tok = ()
    for i in reversed(range(depth)):
        j = i // 2
        sv = saved[i]
        wl = sv["w"]
        da = _mm_nt(dh_b, wl["w_down"], 0, blocked=False, name="ffn_down_dx", after=tok)
        dws["w_down"][i] = _mm_tn(sv["act"], dh_b, blocked=False, name="ffn_down_dw")
        du, dconv = _ffn_act_bwd(sv["u"], da, wl["conv_w"], cb3[i:i + 1], 0)
        d_convw[i], d_convb[i] = dconv[0:3], dconv[3]
        dhf = _mm_nt(du, wl["w_up"], 0, blocked=True, name="ffn_up_dx")
        dws["w_up"][i] = _mm_tn(sv["hf"], du, blocked=True, name="ffn_up_dw")
        dh, dh_b, dg = _rms_bwd(dhf, sv["h1"], ffn_norm[i:i + 1], dh)
        d_ffn[i] = dg[0]
        tok = start_reduce([("w_down", i), ("w_up", i)], "ffn%d" % i)
        if i % 2 == 0:
            do = _mm_nt(dh_b, wl["a_w_o"], 0, blocked=False, name="a_out_dx", after=tok)
            dws["a_w_o"][j] = _mm_tn(sv["o"], dh_b, blocked=False, name="a_out_dw")
            dq, dk, dv = _flash_a_bwd(sv["qkvh"], do, sv["o"], sv["lse"])
            dqkv, dgain = _prep_a_bwd(dq, dk, dv, sv["qkv"], cos, sin, a_q_gain[j:j + 1], a_k_gain[j:j + 1])
            d_gq[j], d_gk[j] = dgain[0], dgain[1]
            dhn = _mm_nt(dqkv, wl["a_w_qkv"], 0, blocked=True, name="a_qkv_dx")
            dws["a_w_qkv"][j] = _mm_tn(sv["hn"], dqkv, blocked=True, name="a_qkv_dw")
            mix_keys = [("a_w_o", j), ("a_w_qkv", j)]
        else:
            dy = _mm_nt(dh_b, wl["b_w_o"], 0, blocked=False, name="b_out_dx", after=tok)
            dws["b_w_o"][j] = _mm_tn(sv["y"], dh_b, blocked=False, name="b_out_dw")
            dos, dlzs = _combine_bwd(dy, sv["os"], sv["lzs"])
            parts = []
            for g in range(G):
                dq, rt, db = _battn_bwd_dq(sv["qkvp"][g], bias, dos[g], sv["os"][g], sv["lzs"][g], dlzs[g], g)
                dk, dv = _battn_bwd_dkv(sv["qkvp"][g], bias_t, dos[g], sv["lzs"][g], rt, g)
                parts += [dq, dk, dv]
                dbias_list.append((g, db))
            dqkv = _concat_cast(parts, [d for _, d in B_GROUPS for _ in range(3)])
            dhn = _mm_nt(dqkv, wl["b_w_qkv"], 0, blocked=True, name="b_qkv_dx")
            dws["b_w_qkv"][j] = _mm_tn(sv["hn"], dqkv, blocked=True, name="b_qkv_dw")
            mix_keys = [("b_w_o", j), ("b_w_qkv", j)]
        dh, dh_b, dg = _rms_bwd(dhn, sv["h0"], mix_norm[i:i + 1], dh)
        d_mix[i] = dg[0]
        if i > 0:
            tok = start_reduce(mix_keys, "mix%d" % i)
    grad_x = dh.reshape(x.shape)

    dbias_layers = [jnp.stack([db for g2, db in dbias_list[l * G:(l + 1) * G]]) for l in range(n_b)]
    d_rel = _bias_reduce(dbias_layers, buckets)[:, :G * hg]

    small = [jnp.stack(d_gq), jnp.stack(d_gk), d_rel, jnp.stack(d_mix), jnp.stack(d_ffn), jnp.stack(d_convw),
             jnp.stack(d_convb), dg_final[0]]
    sizes = [int(np.prod(s.shape)) for s in small]
    flat = jnp.concatenate([s.reshape(-1) for s in small])
    rows = -(-flat.shape[0] // (LANES * SUBLANES)) * SUBLANES
    flat = jnp.pad(flat, (0, rows * LANES - flat.shape[0])).reshape(rows, LANES)
    tot = _allreduce_small(flat)
    start_reduce(mix_keys, "mix0", after=[tot])
    tot = tot.reshape(-1)
    offs = np.cumsum([0] + sizes)
    g_gq, g_gk, g_rel, g_mix, g_ffn, g_convw_full, g_convb, g_final = [
        tot[offs[k]:offs[k + 1]].reshape(small[k].shape) for k in range(len(small))]
    cq = conv_w.shape[2]
    g_convw = lax.dynamic_slice_in_dim(g_convw_full, (2 * cx + cy) * cq, cq, axis=2)

    grads = dict(a_q_gain=g_gq, a_k_gain=g_gk, rel_bias=g_rel, mix_norm=g_mix, ffn_norm=g_ffn,
                 conv_w=g_convw, conv_b=g_convb, final_norm=g_final)
    weights = dict(a_w_qkv=a_w_qkv, a_w_o=a_w_o, a_q_gain=a_q_gain, a_k_gain=a_k_gain, b_w_qkv=b_w_qkv, b_w_o=b_w_o,
                   rel_bias=rel_bias, mix_norm=mix_norm, ffn_norm=ffn_norm, w_up=w_up, conv_w=conv_w, conv_b=conv_b,
                   w_down=w_down, final_norm=final_norm)
    ms = dict(a_w_qkv=m_a_w_qkv, a_w_o=m_a_w_o, a_q_gain=m_a_q_gain, a_k_gain=m_a_k_gain, b_w_qkv=m_b_w_qkv,
              b_w_o=m_b_w_o, rel_bias=m_rel_bias, mix_norm=m_mix_norm, ffn_norm=m_ffn_norm, w_up=m_w_up,
              conv_w=m_conv_w, conv_b=m_conv_b, w_down=m_w_down, final_norm=m_final_norm)
    vs = dict(a_w_qkv=v_a_w_qkv, a_w_o=v_a_w_o, a_q_gain=v_a_q_gain, a_k_gain=v_a_k_gain, b_w_qkv=v_b_w_qkv,
              b_w_o=v_b_w_o, rel_bias=v_rel_bias, mix_norm=v_mix_norm, ffn_norm=v_ffn_norm, w_up=v_w_up,
              conv_w=v_conv_w, conv_b=v_conv_b, w_down=v_w_down, final_norm=v_final_norm)
    deltas, new_m, new_v, stacks = {}, {}, {}, {}

    def update(k):
        w = weights[k]
        two_d = (-1, w.shape[-1])
        outs = _adamw(w.reshape(two_d), grads[k].reshape(two_d), ms[k].reshape(two_d), vs[k].reshape(two_d),
                      keep_g=k in big)
        deltas[k], new_m[k], new_v[k] = [a.reshape(w.shape) for a in outs[:3]]
        if k in big:
            grads[k] = outs[3]
        return outs[2]

    def collect(items, after):
        for keys, send, recv, pieces, lands, tag in items:
            pieces, lands = _reduce_wait(send, recv, pieces, lands, after, "reduce_wait_" + tag)
            for (k, l), p, land in zip(keys, pieces, lands):
                if k not in stacks:
                    stacks[k] = lax.empty((big[k].shape[0], 2) + p.shape[2:], F32)
                stacks[k] = _add_pieces(p, land, stacks[k], l)

    def share(ks):
        for k, gs in zip(ks, _ag_sibling([stacks[k] for k in ks])):
            grads[k] = gs.reshape(big[k].shape)

    late = [k for k in names if k in {kk for kk, _ in pending[-1][0]}]
    collect(pending[:-1], [dh])
    share([k for k in names if k not in late])
    done = [update(k) for k in weights if k not in late]
    collect(pending[-1:], done)
    share(late)
    for k in late:
        update(k)

    loss = lax.psum(loss_blk[0, 0], ("x", "y", "c"))
    keys = list(weights)
    return (loss, grad_x, *[grads[k].reshape(weights[k].shape) for k in keys], *[deltas[k] for k in keys],
            *[new_m[k] for k in keys], *[new_v[k] for k in keys])
```

```python
import functools
import math

import numpy as np
import jax
import jax.numpy as jnp
from jax import lax
from jax.experimental import pallas as pl
from jax.experimental.pallas import tpu as pltpu

F32 = jnp.float32
BF16 = jnp.bfloat16

HEAD_DIM = 128
A_HEADS = 16
A_KV_HEADS = 4
GRID_W = 64
ROPE_THETA = 10000.0
B_GROUPS = ((128, 1), (512, 4), (2048, 16))
B_HEADS_PER_GROUP = 8
REL_BUCKETS = 32
REL_MAX_DISTANCE = 1024
EPS = 1e-6
NEG_INF = -1e30
DEPTH = 4
ADAM_LR = 0.001
ADAM_B1 = 0.9
ADAM_B2 = 0.999
ADAM_EPS = 1e-08
ADAM_WD = 0.01
ADAM_STEP = 10

N_CHIPS = 4
LANES = 128
SUBLANES = 8
VMEM_LIMIT = 52 * 1024 * 1024
MESH = pl.DeviceIdType.MESH


def _pick(n, cands):
    for c in cands:
        if c <= n and n % c == 0:
            return c
    return n


def _lane_tile(n, cap):
    best = None
    for t in range(LANES, min(n, cap) + 1, LANES):
        if n % t == 0:
            best = t
    return best or n


def _cp(*sem):
    return pltpu.CompilerParams(dimension_semantics=sem if sem else None, vmem_limit_bytes=VMEM_LIMIT)


def _half_span():
    hs = {w // (2 * d) for w, d in B_GROUPS}
    assert len(hs) == 1
    return hs.pop()


def _rms_fwd(h, gain, after=()):
    S, D = h.shape
    ts = _pick(S, (512, 256, 128, 64, 32, 16))

    def body(h_ref, g_ref, *rest):
        o_ref = rest[-1]
        x = h_ref[...]
        r = lax.rsqrt(jnp.mean(x * x, axis=-1, keepdims=True) + EPS)
        o_ref[...] = (x * r * g_ref[...]).astype(o_ref.dtype)

    return pl.pallas_call(
        body, name="rms_fwd", grid=(S // ts,),
        in_specs=[pl.BlockSpec((ts, D), lambda i: (i, 0)), pl.BlockSpec((1, D), lambda i: (0, 0))]
        + [pl.BlockSpec(memory_space=pl.ANY)] * len(after),
        out_specs=pl.BlockSpec((ts, D), lambda i: (i, 0)),
        out_shape=jax.ShapeDtypeStruct((S, D), BF16), compiler_params=_cp("arbitrary"))(h, gain, *after)


def _rms_bwd(dy, h, gain, dres):
    S, D = h.shape
    ts = _pick(S, (256, 128, 64, 32, 16))

    def body(dy_ref, h_ref, g_ref, dres_ref, dh_ref, dhb_ref, dg_ref):
        @pl.when(pl.program_id(0) == 0)
        def _():
            dg_ref[...] = jnp.zeros_like(dg_ref)
        x = h_ref[...]
        dy = dy_ref[...]
        r = lax.rsqrt(jnp.mean(x * x, axis=-1, keepdims=True) + EPS)
        xn = x * r
        dg_ref[0:1, :] += jnp.sum(dy * xn, axis=0, keepdims=True)
        dxn = dy * g_ref[...]
        dx = r * (dxn - xn * jnp.mean(dxn * xn, axis=-1, keepdims=True))
        dh = dres_ref[...] + dx
        dh_ref[...] = dh
        dhb_ref[...] = dh.astype(BF16)

    row = pl.BlockSpec((ts, D), lambda i: (i, 0))
    return pl.pallas_call(
        body, name="rms_bwd", grid=(S // ts,),
        in_specs=[row, row, pl.BlockSpec((1, D), lambda i: (0, 0)), row],
        out_specs=[row, row, pl.BlockSpec((SUBLANES, D), lambda i: (0, 0))],
        out_shape=[jax.ShapeDtypeStruct((S, D), F32), jax.ShapeDtypeStruct((S, D), BF16),
                   jax.ShapeDtypeStruct((SUBLANES, D), F32)],
        compiler_params=_cp("arbitrary"))(dy, h, gain, dres)


def _final_loss(h, gain, target):
    S, D = h.shape
    ts = _pick(S, (256, 128, 64, 32, 16))

    def body(h_ref, g_ref, t_ref, loss_ref, dh_ref, dhb_ref, dg_ref):
        @pl.when(pl.program_id(0) == 0)
        def _():
            dg_ref[...] = jnp.zeros_like(dg_ref)
            loss_ref[...] = jnp.zeros_like(loss_ref)
        x = h_ref[...]
        g = g_ref[...]
        r = lax.rsqrt(jnp.mean(x * x, axis=-1, keepdims=True) + EPS)
        xn = x * r
        err = xn * g - t_ref[...]
        part = 0.5 * jnp.sum(jnp.mean(err * err, axis=-1, keepdims=True), axis=0, keepdims=True)
        loss_ref[0:1, 0:1] += part
        dy = err * (1.0 / D)
        dg_ref[0:1, :] += jnp.sum(dy * xn, axis=0, keepdims=True)
        dxn = dy * g
        dh = r * (dxn - xn * jnp.mean(dxn * xn, axis=-1, keepdims=True))
        dh_ref[...] = dh
        dhb_ref[...] = dh.astype(BF16)

    row = pl.BlockSpec((ts, D), lambda i: (i, 0))
    return pl.pallas_call(
        body, name="final_loss", grid=(S // ts,),
        in_specs=[row, pl.BlockSpec((1, D), lambda i: (0, 0)), row],
        out_specs=[pl.BlockSpec((SUBLANES, LANES), lambda i: (0, 0)), row, row,
                   pl.BlockSpec((SUBLANES, D), lambda i: (0, 0))],
        out_shape=[jax.ShapeDtypeStruct((SUBLANES, LANES), F32), jax.ShapeDtypeStruct((S, D), F32),
                   jax.ShapeDtypeStruct((S, D), BF16), jax.ShapeDtypeStruct((SUBLANES, D), F32)],
        compiler_params=_cp("arbitrary"))(h, gain, target)


_NN = (((1,), (0,)), ((), ()))
_NT = (((1,), (1,)), ((), ()))
_TN = (((0,), (0,)), ((), ()))


def _mm_nn(a, w, layer, *, blocked, out_dtype=F32, res=None, name, after=()):
    M, K = a.shape
    if blocked:
        nq = w.shape[3]
        N = N_CHIPS * nq
        tn = _lane_tile(nq, 1408)
        nps = nq // tn
        w_spec = pl.BlockSpec((None, None, K, tn), lambda i, j: (layer, j // nps, 0, j % nps))
    else:
        N = w.shape[2]
        tn = _lane_tile(N, 1024)
        w_spec = pl.BlockSpec((None, K, tn), lambda i, j: (layer, 0, j))
    tm = _pick(M, (1024, 512, 256, 128, 64, 32, 16)) if K <= 3072 else _pick(M, (512, 256, 128, 64, 32, 16))

    def body(*refs):
        a_ref, w_ref, o_ref = refs[0], refs[1], refs[-1]
        acc = lax.dot_general(a_ref[...], w_ref[...], _NN, preferred_element_type=F32)
        if res is not None:
            acc = refs[2][...] + acc
        o_ref[...] = acc.astype(o_ref.dtype)

    in_specs = [pl.BlockSpec((tm, K), lambda i, j: (i, 0)), w_spec]
    args = [a, w]
    if res is not None:
        in_specs.append(pl.BlockSpec((tm, tn), lambda i, j: (i, j)))
        args.append(res)
    return pl.pallas_call(
        body, name=name, grid=(M // tm, N // tn), in_specs=in_specs + [pl.BlockSpec(memory_space=pl.ANY)] * len(after),
        out_specs=pl.BlockSpec((tm, tn), lambda i, j: (i, j)),
        out_shape=jax.ShapeDtypeStruct((M, N), out_dtype),
        compiler_params=_cp("arbitrary", "arbitrary"))(*args, *after)


def _mm_nt(a, w, layer, *, blocked, name, after=()):
    pair = isinstance(a, tuple)
    M = a[0].shape[0] if pair else a.shape[0]
    tm = _pick(M, (1024, 512, 256, 128, 64, 32, 16))
    if blocked:
        K, nq = w.shape[2], w.shape[3]
        tk = _pick(K, (1024, 512, 256, 128))
        half = N_CHIPS // 2

        def body(*refs):
            a_refs, (w_ref, o_ref, acc_ref) = refs[:-3], refs[-3:]
            p = pl.program_id(2)

            @pl.when(p == 0)
            def _():
                acc_ref[...] = jnp.zeros_like(acc_ref)
            if pair:
                @pl.when(p < half)
                def _():
                    acc_ref[...] += lax.dot_general(a_refs[0][...], w_ref[...], _NT, preferred_element_type=F32)

                @pl.when(p >= half)
                def _():
                    acc_ref[...] += lax.dot_general(a_refs[1][...], w_ref[...], _NT, preferred_element_type=F32)
            else:
                acc_ref[...] += lax.dot_general(a_refs[0][...], w_ref[...], _NT, preferred_element_type=F32)

            @pl.when(p == N_CHIPS - 1)
            def _():
                o_ref[...] = acc_ref[...]

        if pair:
            a_specs = [pl.BlockSpec((tm, nq), lambda i, j, p: (i, jnp.minimum(p, half - 1))),
                       pl.BlockSpec((tm, nq), lambda i, j, p: (i, jnp.maximum(p - half, 0)))]
            a_args = list(a)
        else:
            a_specs = [pl.BlockSpec((tm, nq), lambda i, j, p: (i, p))]
            a_args = [a]
        return pl.pallas_call(
            body, name=name, grid=(M // tm, K // tk, N_CHIPS),
            in_specs=a_specs + [pl.BlockSpec((None, None, tk, nq), lambda i, j, p: (layer, p, j, 0))],
            out_specs=pl.BlockSpec((tm, tk), lambda i, j, p: (i, j)),
            out_shape=jax.ShapeDtypeStruct((M, K), F32),
            scratch_shapes=[pltpu.VMEM((tm, tk), F32)],
            compiler_params=_cp("arbitrary", "arbitrary", "arbitrary"))(*a_args, w)
    K, N = w.shape[1], w.shape[2]
    tk = _pick(K, (1024, 512, 256, 128))

    def body(a_ref, w_ref, *rest):
        rest[-1][...] = lax.dot_general(a_ref[...], w_ref[...], _NT, preferred_element_type=F32)

    return pl.pallas_call(
        body, name=name, grid=(M // tm, K // tk),
        in_specs=[pl.BlockSpec((tm, N), lambda i, j: (i, 0)),
                  pl.BlockSpec((None, tk, N), lambda i, j: (layer, j, 0))]
        + [pl.BlockSpec(memory_space=pl.ANY)] * len(after),
        out_specs=pl.BlockSpec((tm, tk), lambda i, j: (i, j)),
        out_shape=jax.ShapeDtypeStruct((M, K), F32),
        compiler_params=_cp("arbitrary", "arbitrary"))(a, w, *after)


def _mm_tn(x, dy, *, blocked, name):
    pair = isinstance(dy, tuple)
    S, K = x.shape
    N = 2 * dy[0].shape[1] if pair else dy.shape[1]
    tk = _pick(K, (512, 256, 128))
    if blocked:
        nq = N // N_CHIPS
        tn = _lane_tile(nq, 1408)
        nps = nq // tn
        out_spec = pl.BlockSpec((None, tk, tn), lambda i, j, s: (j // nps, i, j % nps))
        out_shape = jax.ShapeDtypeStruct((N_CHIPS, K, nq), BF16)
    else:
        tn = _lane_tile(N, 1024)
        out_spec = pl.BlockSpec((tk, tn), lambda i, j, s: (i, j))
        out_shape = jax.ShapeDtypeStruct((K, N), BF16)
    nj = N // tn
    njh = nj // 2
    ns = 2 if pair else 1
    sh = S // ns

    def body(x_ref, *refs):
        o_ref, acc_ref = refs[-2], refs[-1]

        def product(dy_ref):
            part = lax.dot_general(x_ref[...], dy_ref[...], _TN, preferred_element_type=F32)
            if ns == 1:
                o_ref[...] = part.astype(o_ref.dtype)
            else:
                s = pl.program_id(2)

                @pl.when(s == 0)
                def _():
                    acc_ref[...] = part

                @pl.when(s == ns - 1)
                def _():
                    o_ref[...] = (acc_ref[...] + part).astype(o_ref.dtype)

        if pair:
            j = pl.program_id(1)
            pl.when(j < njh)(lambda: product(refs[0]))
            pl.when(j >= njh)(lambda: product(refs[1]))
        else:
            product(refs[0])

    if pair:
        assert nj % 2 == 0
        dy_specs = [pl.BlockSpec((sh, tn), lambda i, j, s: (jnp.where(j < njh, s, ns - 1), jnp.minimum(j, njh - 1))),
                    pl.BlockSpec((sh, tn), lambda i, j, s: (jnp.where(j < njh, 0, s), jnp.maximum(j - njh, 0)))]
        dy_args = list(dy)
    else:
        dy_specs = [pl.BlockSpec((sh, tn), lambda i, j, s: (s, j))]
        dy_args = [dy]
    return pl.pallas_call(
        body, name=name, grid=(K // tk, nj, ns),
        in_specs=[pl.BlockSpec((sh, tk), lambda i, j, s: (s, i))] + dy_specs,
        out_specs=out_spec, out_shape=out_shape,
        scratch_shapes=[pltpu.VMEM((tk, tn) if ns > 1 else (SUBLANES, LANES), F32)],
        compiler_params=_cp("arbitrary", "arbitrary", "arbitrary"))(x, *dy_args)


def _rope_tables(S):
    rows = S // GRID_W
    row_ids = jnp.repeat(jnp.arange(rows, dtype=F32), GRID_W)
    col_ids = jnp.tile(jnp.arange(GRID_W, dtype=F32), rows)
    quarter = HEAD_DIM // 4
    inv_freq = ROPE_THETA ** (-jnp.arange(quarter, dtype=F32) / quarter)
    ang_r = row_ids[:, None] * inv_freq[None, :]
    ang_c = col_ids[:, None] * inv_freq[None, :]
    cos = jnp.concatenate([jnp.cos(ang_r)] * 2 + [jnp.cos(ang_c)] * 2, axis=-1)
    sin = jnp.concatenate([-jnp.sin(ang_r), jnp.sin(ang_r), -jnp.sin(ang_c), jnp.sin(ang_c)], axis=-1)
    return cos, sin


def _swap_quarters(x):
    lane = lax.broadcasted_iota(jnp.int32, x.shape, 1)
    first = (lane % (HEAD_DIM // 2)) < (HEAD_DIM // 4)
    return jnp.where(first, pltpu.roll(x, HEAD_DIM - HEAD_DIM // 4, 1), pltpu.roll(x, HEAD_DIM // 4, 1))


A_SCALE = HEAD_DIM ** -0.5
A_QSCALE = A_SCALE * math.log2(math.e)


def _prep_a_fwd(qkv, cos, sin, gq, gk):
    S, W = qkv.shape
    nrm = A_HEADS + A_KV_HEADS
    ts = _pick(S, (256, 128, 64, 32, 16))

    def body(qkv_ref, cos_ref, sin_ref, gq_ref, gk_ref, o_ref):
        cos_t = cos_ref[...]
        sin_t = sin_ref[...]
        for j in range(nrm):
            sl = slice(j * HEAD_DIM, (j + 1) * HEAD_DIM)
            x = qkv_ref[:, sl]
            g = gq_ref[...] if j < A_HEADS else gk_ref[...]
            r = lax.rsqrt(jnp.mean(x * x, axis=-1, keepdims=True) + EPS)
            n = x * r * g
            y = n * cos_t + _swap_quarters(n) * sin_t
            o_ref[:, sl] = (y * A_QSCALE if j < A_HEADS else y).astype(BF16)
        o_ref[:, nrm * HEAD_DIM:] = qkv_ref[:, nrm * HEAD_DIM:].astype(BF16)

    row = lambda w: pl.BlockSpec((ts, w), lambda i: (i, 0))
    one = pl.BlockSpec((1, HEAD_DIM), lambda i: (0, 0))
    return pl.pallas_call(
        body, name="prep_a_fwd", grid=(S // ts,),
        in_specs=[row(W), row(HEAD_DIM), row(HEAD_DIM), one, one], out_specs=row(W),
        out_shape=jax.ShapeDtypeStruct((S, W), BF16), compiler_params=_cp("arbitrary"))(qkv, cos, sin, gq, gk)


def _prep_a_bwd(dq, dk, dv, qkv, cos, sin, gq, gk):
    S, W = qkv.shape
    nrm = A_HEADS + A_KV_HEADS
    nq, nk = A_HEADS * HEAD_DIM, A_KV_HEADS * HEAD_DIM
    ts = _pick(S, (256, 128, 64, 32, 16))

    def body(dq_ref, dk_ref, dv_ref, qkv_ref, cos_ref, sin_ref, gq_ref, gk_ref, o_ref, dg_ref):
        @pl.when(pl.program_id(0) == 0)
        def _():
            dg_ref[...] = jnp.zeros_like(dg_ref)
        cos_t = cos_ref[...]
        sin_t = sin_ref[...]
        for j in range(nrm):
            sl = slice(j * HEAD_DIM, (j + 1) * HEAD_DIM)
            x = qkv_ref[:, sl]
            if j < A_HEADS:
                dy, g, grow = dq_ref[:, sl], gq_ref[...], 0
            else:
                jj = j - A_HEADS
                dy, g, grow = dk_ref[:, jj * HEAD_DIM:(jj + 1) * HEAD_DIM], gk_ref[...], 1
            r = lax.rsqrt(jnp.mean(x * x, axis=-1, keepdims=True) + EPS)
            xn = x * r
            dn = dy * cos_t + _swap_quarters(dy * sin_t)
            dg_ref[grow:grow + 1, :] += jnp.sum(dn * xn, axis=0, keepdims=True)
            dxn = dn * g
            o_ref[:, sl] = (r * (dxn - xn * jnp.mean(dxn * xn, axis=-1, keepdims=True))).astype(BF16)
        o_ref[:, nrm * HEAD_DIM:] = dv_ref[...].astype(BF16)

    row = lambda w: pl.BlockSpec((ts, w), lambda i: (i, 0))
    one = pl.BlockSpec((1, HEAD_DIM), lambda i: (0, 0))
    return pl.pallas_call(
        body, name="prep_a_bwd", grid=(S // ts,),
        in_specs=[row(nq), row(nk), row(nk), row(W), row(HEAD_DIM), row(HEAD_DIM), one, one],
        out_specs=[row(W), pl.BlockSpec((SUBLANES, HEAD_DIM), lambda i: (0, 0))],
        out_shape=[jax.ShapeDtypeStruct((S, W), BF16), jax.ShapeDtypeStruct((SUBLANES, HEAD_DIM), F32)],
        compiler_params=_cp("arbitrary"))(dq, dk, dv, qkv, cos, sin, gq, gk)


def _flash_a_fwd(qkvh, after=()):
    S = qkvh.shape[0]
    grp = A_HEADS // A_KV_HEADS
    tq = _pick(S, (256, 128, 64, 32, 16))
    kc = _pick(S, (512, 256, 128))
    lanes = [slice(b * LANES, (b + 1) * LANES) for b in range(kc // LANES)]

    def body(q_ref, k_ref, v_ref, *rest):
        o_ref, lse_ref = rest[-2], rest[-1]
        q = q_ref[...]
        m_t = jnp.full((tq, LANES), -jnp.inf, F32)
        for c in range(S // kc):
            s = lax.dot_general(q, k_ref[c * kc:(c + 1) * kc, :], _NT, preferred_element_type=F32)
            for sl in lanes:
                m_t = jnp.maximum(m_t, s[:, sl])
        m = jnp.max(m_t, axis=-1, keepdims=True)
        l_t = jnp.zeros((tq, LANES), F32)
        acc = jnp.zeros((tq, HEAD_DIM), F32)
        for c in range(S // kc):
            rows = slice(c * kc, (c + 1) * kc)
            p = jnp.exp2(lax.dot_general(q, k_ref[rows, :], _NT, preferred_element_type=F32) - m)
            for sl in lanes:
                l_t = l_t + p[:, sl]
            acc = acc + lax.dot_general(p.astype(BF16), v_ref[rows, :], _NN, preferred_element_type=F32)
        l = jnp.sum(l_t, axis=-1, keepdims=True)
        o_ref[...] = (acc * (1.0 / l)).astype(BF16)
        lse_ref[...] = jnp.broadcast_to(m + jnp.log2(l), lse_ref.shape)

    qs = pl.BlockSpec((tq, HEAD_DIM), lambda h, i: (i, h))
    return pl.pallas_call(
        body, name="flash_a_fwd", grid=(A_HEADS, S // tq),
        in_specs=[qs,
                  pl.BlockSpec((S, HEAD_DIM), lambda h, i: (0, A_HEADS + h // grp)),
                  pl.BlockSpec((S, HEAD_DIM), lambda h, i: (0, A_HEADS + A_KV_HEADS + h // grp))]
        + [pl.BlockSpec(memory_space=pl.ANY)] * len(after),
        out_specs=[qs, qs],
        out_shape=[jax.ShapeDtypeStruct((S, A_HEADS * HEAD_DIM), BF16),
                   jax.ShapeDtypeStruct((S, A_HEADS * HEAD_DIM), F32)],
        compiler_params=_cp("arbitrary", "arbitrary"))(qkvh, qkvh, qkvh, *after)


def _flash_a_bwd(qkvh, do, o, lse):
    S = qkvh.shape[0]
    grp = A_HEADS // A_KV_HEADS
    tq = _pick(S, (1024, 512, 256, 128, 64, 32, 16))
    nq = S // tq
    sub = 4 if tq % 64 == 0 else 1
    ts = tq // sub

    def body(q_ref, k_ref, v_ref, do_ref, o_ref, lse_ref, dq_ref, dk_ref, dv_ref):
        g, i = pl.program_id(1), pl.program_id(2)

        @pl.when((g == 0) & (i == 0))
        def _():
            dk_ref[...] = jnp.zeros_like(dk_ref)
            dv_ref[...] = jnp.zeros_like(dv_ref)
        k = k_ref[...]
        dk = dv = None
        for b in range(sub):
            rows = slice(b * ts, (b + 1) * ts)
            q = q_ref[rows, :]
            do_f = do_ref[rows, :]
            do_b = do_f.astype(BF16)
            delta = jnp.sum(do_f * o_ref[rows, :].astype(F32), axis=-1, keepdims=True)
            p = jnp.exp2(lax.dot_general(q, k, _NT, preferred_element_type=F32) - lse_ref[rows, 0:1])
            dp = lax.dot_general(do_b, v_ref[...], _NT, preferred_element_type=F32)
            ds_b = (p * (dp - delta)).astype(BF16)
            dq_ref[rows, :] = lax.dot_general(ds_b, k, _NN, preferred_element_type=F32) * A_SCALE
            dk_b = lax.dot_general(ds_b, q, _TN, preferred_element_type=F32)
            dv_b = lax.dot_general(p.astype(BF16), do_b, _TN, preferred_element_type=F32)
            dk = dk_b if dk is None else dk + dk_b
            dv = dv_b if dv is None else dv + dv_b
        dk_ref[...] += dk
        dv_ref[...] += dv

        @pl.when((g == grp - 1) & (i == nq - 1))
        def _():
            dk_ref[...] = dk_ref[...] * (A_SCALE / A_QSCALE)

    qs = pl.BlockSpec((tq, HEAD_DIM), lambda kv, g, i: (i, kv * grp + g))
    kvs = lambda off: pl.BlockSpec((S, HEAD_DIM), lambda kv, g, i: (0, off + kv))
    return pl.pallas_call(
        body, name="flash_a_bwd", grid=(A_KV_HEADS, grp, S // tq),
        in_specs=[qs, kvs(A_HEADS), kvs(A_HEADS + A_KV_HEADS), qs, qs, qs],
        out_specs=[qs, kvs(0), kvs(0)],
        out_shape=[jax.ShapeDtypeStruct((S, A_HEADS * HEAD_DIM), F32),
                   jax.ShapeDtypeStruct((S, A_KV_HEADS * HEAD_DIM), F32),
                   jax.ShapeDtypeStruct((S, A_KV_HEADS * HEAD_DIM), F32)],
        compiler_params=_cp("arbitrary", "arbitrary", "arbitrary"))(qkvh, qkvh, qkvh, do, o, lse)


def _bucket_tables(transposed):
    hs = _half_span()
    tq, kv = 2 * hs, 4 * hs
    nb = REL_BUCKETS // 2
    max_exact = nb // 2
    shape = (kv, tq) if transposed else (tq, kv)
    out = np.zeros((len(B_GROUPS), 3) + shape, np.int32)
    win = np.arange(kv) - hs
    blk = np.arange(tq)
    for g, (_, dil) in enumerate(B_GROUPS):
        for case in range(3):
            inside = ((win >= 0) | (case != 0)) & ((win < tq) | (case != 2))
            if transposed:
                rel = blk[None, :] - win[:, None]
                ok = inside[:, None]
            else:
                rel = win[None, :] - blk[:, None]
                ok = inside[None, :]
            r = rel * dil
            n = np.abs(r)
            nf = np.maximum(n, 1).astype(np.float32)
            large = max_exact + (np.log(nf / np.float32(max_exact)) / np.float32(math.log(REL_MAX_DISTANCE / max_exact))
                                 * np.float32(nb - max_exact)).astype(np.int32)
            large = np.minimum(large, nb - 1)
            bucket = np.where(r > 0, nb, 0) + np.where(n < max_exact, n, large)
            out[g, case] = np.where((np.abs(rel) <= hs) & ok, bucket, -1)
    return out


def _bias_build(rel_bias, buckets):
    G, _, tq, kv = buckets.shape
    hg = B_HEADS_PER_GROUP

    def body(rb_ref, bk_ref, o_ref):
        col = pl.program_id(0) * hg + pl.program_id(2)
        bk = bk_ref[...]
        acc = jnp.full((tq, kv), NEG_INF, F32)
        for b in range(REL_BUCKETS):
            acc = jnp.where(bk == b, rb_ref[b, col], acc)
        o_ref[...] = acc

    return pl.pallas_call(
        body, name="bias_build", grid=(G, 3, hg),
        in_specs=[pl.BlockSpec(memory_space=pltpu.SMEM),
                  pl.BlockSpec((None, None, tq, kv), lambda g, c, h: (g, c, 0, 0))],
        out_specs=pl.BlockSpec((None, None, None, tq, kv), lambda g, c, h: (g, c, h, 0, 0)),
        out_shape=jax.ShapeDtypeStruct((G, 3, hg, tq, kv), F32),
        compiler_params=_cp("arbitrary", "arbitrary", "arbitrary"))(rel_bias, buckets)


def _bias_reduce(dbias_list, buckets):
    G, _, tq, kv = buckets.shape
    hg = B_HEADS_PER_GROUP
    n = len(dbias_list)

    def body(*refs):
        bk_ref, o_ref = refs[n], refs[n + 1]
        first = (pl.program_id(0) == 0) & (pl.program_id(1) == 0) & (pl.program_id(2) == 0)

        @pl.when(first)
        def _():
            o_ref[...] = jnp.zeros_like(o_ref)
        col = pl.program_id(0) * hg + pl.program_id(2)
        db = refs[0][...]
        for r in refs[1:n]:
            db = db + r[...]
        bk = bk_ref[...]
        rows = lax.broadcasted_iota(jnp.int32, (REL_BUCKETS, LANES), 0)
        cols = lax.broadcasted_iota(jnp.int32, (REL_BUCKETS, LANES), 1)
        acc = jnp.zeros((REL_BUCKETS, LANES), F32)
        for b in range(REL_BUCKETS):
            val = jnp.sum(jnp.sum(jnp.where(bk == b, db, 0.0), axis=1, keepdims=True), axis=0, keepdims=True)
            acc = acc + jnp.where((rows == b) & (cols == col), val, 0.0)
        o_ref[...] += acc

    tile = pl.BlockSpec((None, None, None, tq, kv), lambda g, c, h: (g, c, h, 0, 0))
    return pl.pallas_call(
        body, name="bias_reduce", grid=(G, 3, hg),
        in_specs=[tile] * n + [pl.BlockSpec((None, None, tq, kv), lambda g, c, h: (g, c, 0, 0))],
        out_specs=pl.BlockSpec((REL_BUCKETS, LANES), lambda g, c, h: (0, 0)),
        out_shape=jax.ShapeDtypeStruct((REL_BUCKETS, LANES), F32),
        compiler_params=_cp("arbitrary", "arbitrary", "arbitrary"))(*dbias_list, buckets)


def _mm_nn_perm(a, w, g):
    S, K = a.shape
    nq = w.shape[3]
    dil = B_GROUPS[g][1]
    wg3 = 3 * B_HEADS_PER_GROUP * HEAD_DIM
    tn = _lane_tile(math.gcd(nq, wg3), 768)
    nps, ntile = nq // tn, wg3 // tn
    tm = _pick(S, (1024, 512, 256))
    rows = tm // dil

    def body(a_ref, w_ref, o_ref, acc_ref):
        acc = lax.dot_general(a_ref[...], w_ref[...], _NN, preferred_element_type=F32)
        if dil == 1:
            o_ref[0] = acc.astype(BF16)
        else:
            for k in range(tn // LANES):
                acc_ref[k] = acc[:, k * LANES:(k + 1) * LANES]
            for c in range(dil):
                for k in range(tn // LANES):
                    o_ref[c, :, k * LANES:(k + 1) * LANES] = acc_ref[k, pl.ds(c, rows, stride=dil), :].astype(BF16)

    def w_map(i, j):
        t = g * ntile + j
        return (0, t // nps, 0, t % nps)

    return pl.pallas_call(
        body, name="b_qkv_g%d" % g, grid=(S // tm, ntile),
        in_specs=[pl.BlockSpec((tm, K), lambda i, j: (i, 0)), pl.BlockSpec((None, None, K, tn), w_map)],
        out_specs=pl.BlockSpec((dil, rows, tn), lambda i, j: (0, i, j)),
        out_shape=jax.ShapeDtypeStruct((dil, S // dil, wg3), BF16),
        scratch_shapes=[pltpu.VMEM((tn // LANES, tm, LANES), F32)],
        compiler_params=_cp("arbitrary", "arbitrary"))(a, w)


def _window_specs(S, wg, col):
    hs = _half_span()
    tq = 2 * hs
    per = tq // hs
    return (pl.BlockSpec((tq, wg), lambda i: (i, col)),
            pl.BlockSpec((hs, wg), lambda i: (jnp.maximum(i * per - 1, 0), col)),
            pl.BlockSpec((hs, wg), lambda i: (jnp.minimum((i + 1) * per, S // hs - 1), col)))


def _window_case(i, L):
    per = L // (2 * _half_span())
    r = i % per
    return jnp.where(r == 0, 0, jnp.where(r == per - 1, 2, 1))


def _window(prev_ref, main_ref, next_ref, sl):
    return jnp.concatenate([prev_ref[:, sl], main_ref[:, sl], next_ref[:, sl]], axis=0)


def _battn_fwd(qkvp, bias, g):
    dil, L, wg3 = qkvp.shape
    S = dil * L
    hs = _half_span()
    tq, kvl = 2 * hs, 4 * hs
    hg = B_HEADS_PER_GROUP
    wg = hg * HEAD_DIM
    scale = HEAD_DIM ** -0.5
    flat = qkvp.reshape(S, wg3)

    def body(q_ref, km, kp, kn, vm, vp, vn, b_ref, o_ref, lz_ref):
        case = _window_case(pl.program_id(0), L)
        for h in range(hg):
            sl = slice(h * HEAD_DIM, (h + 1) * HEAD_DIM)
            s = lax.dot_general(q_ref[:, sl], _window(kp, km, kn, sl), _NT, preferred_element_type=F32) * scale
            s = s + b_ref[case, h]
            m = jnp.max(s, axis=-1, keepdims=True)
            p = jnp.exp(s - m)
            l = jnp.sum(p, axis=-1, keepdims=True)
            o_ref[:, sl] = lax.dot_general(p.astype(BF16), _window(vp, vm, vn, sl), _NN, preferred_element_type=F32) / l
            lz_ref[:, sl] = jnp.broadcast_to(m + jnp.log(l), (tq, HEAD_DIM))

    blk = pl.BlockSpec((tq, wg), lambda i: (i, 0))
    o, lz = pl.pallas_call(
        body, name="battn_fwd_g%d" % g, grid=(S // tq,),
        in_specs=[_window_specs(S, wg, 0)[0], *_window_specs(S, wg, 1), *_window_specs(S, wg, 2),
                  pl.BlockSpec((None, 3, hg, tq, kvl), lambda i: (g, 0, 0, 0, 0))],
        out_specs=[blk, blk], out_shape=[jax.ShapeDtypeStruct((S, wg), F32)] * 2,
        compiler_params=_cp("arbitrary"))(flat, flat, flat, flat, flat, flat, flat, bias)
    return o, lz


def _battn_bwd_dq(qkvp, bias, do, o, lz, dlz, g):
    dil, L, wg3 = qkvp.shape
    S = dil * L
    hs = _half_span()
    tq, kvl = 2 * hs, 4 * hs
    hg = B_HEADS_PER_GROUP
    wg = hg * HEAD_DIM
    scale = HEAD_DIM ** -0.5
    flat = qkvp.reshape(S, wg3)

    def body(q_ref, km, kp, kn, vm, vp, vn, b_ref, do_ref, o_ref, lz_ref, dlz_ref, dq_ref, rt_ref, db_ref):
        i = pl.program_id(0)

        @pl.when(i == 0)
        def _():
            db_ref[...] = jnp.zeros_like(db_ref)
        case = _window_case(i, L)
        for h in range(hg):
            sl = slice(h * HEAD_DIM, (h + 1) * HEAD_DIM)
            kw = _window(kp, km, kn, sl)
            do_f = do_ref[:, sl]
            s = lax.dot_general(q_ref[:, sl], kw, _NT, preferred_element_type=F32) * scale + b_ref[case, h]
            p = jnp.exp(s - lz_ref[:, sl][:, 0:1])
            dp = lax.dot_general(do_f.astype(BF16), _window(vp, vm, vn, sl), _NT, preferred_element_type=F32)
            rt = dlz_ref[:, sl][:, 0:1] - jnp.sum(do_f * o_ref[:, sl], axis=-1, keepdims=True)
            ds = p * (dp + rt)
            db_ref[case, h] += ds
            dq_ref[:, sl] = lax.dot_general((ds * scale).astype(BF16), kw, _NN, preferred_element_type=F32)
            rt_ref[:, sl] = jnp.broadcast_to(rt, (tq, HEAD_DIM))

    blk = pl.BlockSpec((tq, wg), lambda i: (i, 0))
    row = jax.ShapeDtypeStruct((S, wg), F32)
    return pl.pallas_call(
        body, name="battn_bwd_dq_g%d" % g, grid=(S // tq,),
        in_specs=[_window_specs(S, wg, 0)[0], *_window_specs(S, wg, 1), *_window_specs(S, wg, 2),
                  pl.BlockSpec((None, 3, hg, tq, kvl), lambda i: (g, 0, 0, 0, 0)), blk, blk, blk, blk],
        out_specs=[blk, blk, pl.BlockSpec((3, hg, tq, kvl), lambda i: (0, 0, 0, 0))],
        out_shape=[row, row, jax.ShapeDtypeStruct((3, hg, tq, kvl), F32)],
        compiler_params=_cp("arbitrary"))(flat, flat, flat, flat, flat, flat, flat, bias, do, o, lz, dlz)


def _battn_bwd_dkv(qkvp, bias_t, do, lz, rt, g):
    dil, L, wg3 = qkvp.shape
    S = dil * L
    hs = _half_span()
    tq, kvl = 2 * hs, 4 * hs
    hg = B_HEADS_PER_GROUP
    wg = hg * HEAD_DIM
    scale = HEAD_DIM ** -0.5
    flat = qkvp.reshape(S, wg3)

    def body(k_ref, v_ref, qm, qp, qn, dom, dop, don, lzm, lzp, lzn, rtm, rtp, rtn, b_ref, dk_ref, dv_ref):
        case = _window_case(pl.program_id(0), L)
        for h in range(hg):
            sl = slice(h * HEAD_DIM, (h + 1) * HEAD_DIM)
            qw = _window(qp, qm, qn, sl)
            dow = _window(dop, dom, don, sl).astype(BF16)
            s = lax.dot_general(qw, k_ref[:, sl], _NT, preferred_element_type=F32) * scale + b_ref[case, h]
            p = jnp.exp(s - _window(lzp, lzm, lzn, sl)[:, 0:1])
            dp = lax.dot_general(dow, v_ref[:, sl], _NT, preferred_element_type=F32)
            ds_b = (p * (dp + _window(rtp, rtm, rtn, sl)[:, 0:1]) * scale).astype(BF16)
            dk_ref[:, sl] = lax.dot_general(ds_b, qw, _TN, preferred_element_type=F32)
            dv_ref[:, sl] = lax.dot_general(p.astype(BF16), dow, _TN, preferred_element_type=F32)

    blk = pl.BlockSpec((tq, wg), lambda i: (i, 0))
    row = jax.ShapeDtypeStruct((S, wg), F32)
    return pl.pallas_call(
        body, name="battn_bwd_dkv_g%d" % g, grid=(S // tq,),
        in_specs=[_window_specs(S, wg, 1)[0], _window_specs(S, wg, 2)[0], *_window_specs(S, wg, 0),
                  *_window_specs(S, wg, 0), *_window_specs(S, wg, 0), *_window_specs(S, wg, 0),
                  pl.BlockSpec((None, 3, hg, kvl, tq), lambda i: (g, 0, 0, 0, 0))],
        out_specs=[blk, blk], out_shape=[row, row],
        compiler_params=_cp("arbitrary"))(flat, flat, flat, flat, flat, do, do, do, lz, lz, lz, rt, rt, rt, bias_t)


def _group_weights(lz_refs, h):
    z = [r[h] for r in lz_refs]
    mx = functools.reduce(jnp.maximum, z)
    e = [jnp.exp(v - mx) for v in z]
    inv = 1.0 / functools.reduce(lambda a, b: a + b, e)
    return [v * inv for v in e]


def _to_token_order(src_ref, dst_ref, dil):
    rows = src_ref.shape[1]
    for k in range(dst_ref.shape[0]):
        sl = slice(k * LANES, (k + 1) * LANES)
        if dil == 1:
            dst_ref[k] = src_ref[0, :, sl]
        else:
            for c in range(dil):
                dst_ref[k, pl.ds(c, rows, stride=dil), :] = src_ref[c, :, sl]


def _to_subsequence_order(src_ref, dst_ref, dil):
    rows = dst_ref.shape[1]
    for k in range(src_ref.shape[0]):
        sl = slice(k * LANES, (k + 1) * LANES)
        if dil == 1:
            dst_ref[0, :, sl] = src_ref[k]
        else:
            for c in range(dil):
                dst_ref[c, :, sl] = src_ref[k, pl.ds(c, rows, stride=dil), :]


def _sub_view(a, dil):
    S, w = a.shape
    return a.reshape(dil, S // dil, w)


def _sub_spec(dil, ts, w):
    return pl.BlockSpec((dil, ts // dil, w), lambda i: (0, i, 0))


def _combine_fwd(os_, lzs):
    G = len(os_)
    S, Wg = os_[0].shape
    hg = B_HEADS_PER_GROUP
    dils = [d for _, d in B_GROUPS]
    ts = _pick(S, (256, 128))

    def body(*refs):
        o_in, lz_in, y_ref = refs[:G], refs[G:2 * G], refs[2 * G]
        o_nat, lz_nat = refs[2 * G + 1:3 * G + 1], refs[3 * G + 1:4 * G + 1]
        for g in range(G):
            _to_token_order(o_in[g], o_nat[g], dils[g])
            _to_token_order(lz_in[g], lz_nat[g], dils[g])
        for h in range(hg):
            w = _group_weights(lz_nat, h)
            for g in range(G):
                y_ref[:, (g * hg + h) * HEAD_DIM:(g * hg + h + 1) * HEAD_DIM] = (w[g] * o_nat[g][h]).astype(BF16)

    specs = [_sub_spec(d, ts, Wg) for d in dils]
    return pl.pallas_call(
        body, name="combine_fwd", grid=(S // ts,), in_specs=specs + specs,
        out_specs=pl.BlockSpec((ts, G * Wg), lambda i: (i, 0)),
        out_shape=jax.ShapeDtypeStruct((S, G * Wg), BF16),
        scratch_shapes=[pltpu.VMEM((hg, ts, HEAD_DIM), F32)] * (2 * G),
        compiler_params=_cp("arbitrary"))(*[_sub_view(a, d) for a, d in zip(os_, dils)],
                                          *[_sub_view(a, d) for a, d in zip(lzs, dils)])


def _combine_bwd(dy, os_, lzs):
    G = len(os_)
    S, Wg = os_[0].shape
    hg = B_HEADS_PER_GROUP
    dils = [d for _, d in B_GROUPS]
    ts = _pick(S, (128,))

    def body(*refs):
        dy_ref, o_in, lz_in = refs[0], refs[1:1 + G], refs[1 + G:1 + 2 * G]
        do_out, dlz_out = refs[1 + 2 * G:1 + 3 * G], refs[1 + 3 * G:1 + 4 * G]
        scr = refs[1 + 4 * G:]
        o_nat, lz_nat, do_nat, dlz_nat = scr[:G], scr[G:2 * G], scr[2 * G:3 * G], scr[3 * G:4 * G]
        for g in range(G):
            _to_token_order(o_in[g], o_nat[g], dils[g])
            _to_token_order(lz_in[g], lz_nat[g], dils[g])
        for h in range(hg):
            w = _group_weights(lz_nat, h)
            dw = []
            for g in range(G):
                dyg = dy_ref[:, (g * hg + h) * HEAD_DIM:(g * hg + h + 1) * HEAD_DIM]
                dw.append(jnp.sum(dyg * o_nat[g][h], axis=-1, keepdims=True))
                do_nat[g][h] = w[g] * dyg
            tot = functools.reduce(lambda a, b: a + b, [w[g] * dw[g] for g in range(G)])
            for g in range(G):
                dlz_nat[g][h] = w[g] * (dw[g] - tot)
        for g in range(G):
            _to_subsequence_order(do_nat[g], do_out[g], dils[g])
            _to_subsequence_order(dlz_nat[g], dlz_out[g], dils[g])

    specs = [_sub_spec(d, ts, Wg) for d in dils]
    outs = pl.pallas_call(
        body, name="combine_bwd", grid=(S // ts,),
        in_specs=[pl.BlockSpec((ts, G * Wg), lambda i: (i, 0))] + specs + specs,
        out_specs=specs + specs,
        out_shape=[jax.ShapeDtypeStruct((d, S // d, Wg), F32) for d in dils] * 2,
        scratch_shapes=[pltpu.VMEM((hg, ts, HEAD_DIM), F32)] * (4 * G),
        compiler_params=_cp("arbitrary"))(dy, *[_sub_view(a, d) for a, d in zip(os_, dils)],
                                          *[_sub_view(a, d) for a, d in zip(lzs, dils)])
    flat = [a.reshape(S, Wg) for a in outs]
    return flat[:G], flat[G:]


def _concat_cast(parts, dils):
    S = parts[0].shape[0]
    widths = [p.shape[1] for p in parts]
    n = len(parts)
    ts = _pick(S, (256, 128))

    def body(*refs):
        o_ref, nat = refs[n], refs[n + 1]
        off = 0
        for r, w, d in zip(refs, widths, dils):
            _to_token_order(r, nat, d)
            for k in range(w // LANES):
                o_ref[:, off + k * LANES:off + (k + 1) * LANES] = nat[k].astype(BF16)
            off += w

    assert len(set(widths)) == 1
    return pl.pallas_call(
        body, name="concat_cast", grid=(S // ts,),
        in_specs=[_sub_spec(d, ts, w) for w, d in zip(widths, dils)],
        out_specs=pl.BlockSpec((ts, sum(widths)), lambda i: (i, 0)),
        out_shape=jax.ShapeDtypeStruct((S, sum(widths)), BF16),
        scratch_shapes=[pltpu.VMEM((widths[0] // LANES, ts, LANES), F32)],
        compiler_params=_cp("arbitrary"))(*[_sub_view(p, d) for p, d in zip(parts, dils)])


def _ffn_specs(S, dff, cq, ts, tc, layer, order):
    nfc = dff // tc
    nps = cq // tc
    hb = ts // SUBLANES
    nrow8 = S // SUBLANES

    def u_main(half):
        return pl.BlockSpec((ts, tc), lambda *g: (order(*g)[0], order(*g)[1] % nfc + half * nfc))

    def u_prev(half):
        return pl.BlockSpec((SUBLANES, tc), lambda *g: (jnp.maximum(order(*g)[0] * hb - 1, 0),
                                                         order(*g)[1] % nfc + half * nfc))

    def u_next(half):
        return pl.BlockSpec((SUBLANES, tc), lambda *g: (jnp.minimum((order(*g)[0] + 1) * hb, nrow8 - 1),
                                                         order(*g)[1] % nfc + half * nfc))

    def cw(half):
        def im(*g):
            jj = order(*g)[1] % nfc + half * nfc
            return (layer, jj // nps, 0, jj % nps)
        return pl.BlockSpec((None, None, 3, tc), im)

    def cb(half):
        return pl.BlockSpec((None, 1, tc), lambda *g: (layer, 0, order(*g)[1] % nfc + half * nfc))

    return nfc, u_main, u_prev, u_next, cw, cb


def _ffn_act_fwd(u, cw_full, cb3, layer):
    S, two_dff = u.shape
    dff = two_dff // 2
    cq = cw_full.shape[3]
    ts = _pick(S, (1024, 512, 256, 128, 64, 32, 16))
    tc = _pick(cq, (256, 128))
    order = lambda j, i: (i, j)
    nfc, u_main, u_prev, u_next, cw, cb = _ffn_specs(S, dff, cq, ts, tc, layer, order)
    nrow = S // ts

    def body(ug, ugp, ugn, uv, uvp, uvn, wg, wv, bg, bv, a_ref):
        i = pl.program_id(1)
        row = lax.broadcasted_iota(jnp.int32, (ts, tc), 0)

        def conv(x_ref, p_ref, n_ref, w_ref, b_ref):
            x = x_ref[...]
            prev = jnp.where(i > 0, p_ref[SUBLANES - 1:SUBLANES, :], 0.0)
            nxt = jnp.where(i < nrow - 1, n_ref[0:1, :], 0.0)
            xm = jnp.where(row == 0, prev, pltpu.roll(x, 1, 0))
            xp = jnp.where(row == ts - 1, nxt, pltpu.roll(x, ts - 1, 0))
            return w_ref[0:1, :] * xm + w_ref[1:2, :] * x + w_ref[2:3, :] * xp + b_ref[...]

        gc = conv(ug, ugp, ugn, wg, bg)
        vc = conv(uv, uvp, uvn, wv, bv)
        a_ref[...] = (gc * (1.0 / (1.0 + jnp.exp(-gc))) * vc).astype(BF16)

    return pl.pallas_call(
        body, name="ffn_act_fwd", grid=(nfc, nrow),
        in_specs=[u_main(0), u_prev(0), u_next(0), u_main(1), u_prev(1), u_next(1), cw(0), cw(1), cb(0), cb(1)],
        out_specs=pl.BlockSpec((ts, tc), lambda j, i: (i, j)),
        out_shape=jax.ShapeDtypeStruct((S, dff), BF16),
        compiler_params=_cp("arbitrary", "arbitrary"))(u, u, u, u, u, u, cw_full, cw_full, cb3, cb3)


def _ffn_act_bwd(u, da, cw_full, cb3, layer):
    S, two_dff = u.shape
    dff = two_dff // 2
    cq = cw_full.shape[3]
    ts = _pick(S, (1024, 512, 256, 128, 64, 32, 16))
    tc = _pick(cq, (256, 128))
    order = lambda j, i: (i, j)
    nfc, u_main, u_prev, u_next, cw, cb = _ffn_specs(S, dff, cq, ts, tc, layer, order)
    nrow = S // ts
    hb = ts // SUBLANES
    te = ts + 2 * SUBLANES
    da_main = pl.BlockSpec((ts, tc), lambda j, i: (i, j))
    da_prev = pl.BlockSpec((SUBLANES, tc), lambda j, i: (jnp.maximum(i * hb - 1, 0), j))
    da_next = pl.BlockSpec((SUBLANES, tc), lambda j, i: (jnp.minimum((i + 1) * hb, S // SUBLANES - 1), j))
    main = slice(SUBLANES, SUBLANES + ts)

    def body(ug, ugp, ugn, uv, uvp, uvn, dam, dap, dan, wg, wv, bg, bv, dug_ref, duv_ref, accg_ref, accv_ref):
        i = pl.program_id(1)

        @pl.when(i == 0)
        def _():
            accg_ref[...] = jnp.zeros_like(accg_ref)
            accv_ref[...] = jnp.zeros_like(accv_ref)

        def ext(m, p, n):
            return jnp.concatenate([jnp.where(i > 0, p[...], 0.0), m[...], jnp.where(i < nrow - 1, n[...], 0.0)], axis=0)

        def shift(x):
            return pltpu.roll(x, 1, 0), pltpu.roll(x, te - 1, 0)

        xg, xv, dae = ext(ug, ugp, ugn), ext(uv, uvp, uvn), ext(dam, dap, dan)
        xgm, xgp = shift(xg)
        xvm, xvp = shift(xv)
        gc = wg[0:1, :] * xgm + wg[1:2, :] * xg + wg[2:3, :] * xgp + bg[...]
        vc = wv[0:1, :] * xvm + wv[1:2, :] * xv + wv[2:3, :] * xvp + bv[...]
        sig = 1.0 / (1.0 + jnp.exp(-gc))
        silu = gc * sig
        dcg = dae * vc * (sig * (1.0 + gc * (1.0 - sig)))
        dcv = dae * silu

        def finish(dc, x, xm, xp, w_ref, du_ref, acc_ref):
            dm, dp = shift(dc)
            du = w_ref[0:1, :] * dp + w_ref[1:2, :] * dc + w_ref[2:3, :] * dm
            du_ref[...] = du[main, :].astype(BF16)
            dcm = dc[main, :]
            acc_ref[0:1, :] += jnp.sum(dcm * xm[main, :], axis=0, keepdims=True)
            acc_ref[1:2, :] += jnp.sum(dcm * x[main, :], axis=0, keepdims=True)
            acc_ref[2:3, :] += jnp.sum(dcm * xp[main, :], axis=0, keepdims=True)
            acc_ref[3:4, :] += jnp.sum(dcm, axis=0, keepdims=True)

        finish(dcg, xg, xgm, xgp, wg, dug_ref, accg_ref)
        finish(dcv, xv, xvm, xvp, wv, duv_ref, accv_ref)

    blk = pl.BlockSpec((ts, tc), lambda j, i: (i, j))
    acc = pl.BlockSpec((SUBLANES, tc), lambda j, i: (0, j))
    dug, duv, accg, accv = pl.pallas_call(
        body, name="ffn_act_bwd", grid=(nfc, nrow),
        in_specs=[u_main(0), u_prev(0), u_next(0), u_main(1), u_prev(1), u_next(1), da_main, da_prev, da_next,
                  cw(0), cw(1), cb(0), cb(1)],
        out_specs=[blk, blk, acc, acc],
        out_shape=[jax.ShapeDtypeStruct((S, dff), BF16)] * 2 + [jax.ShapeDtypeStruct((SUBLANES, dff), F32)] * 2,
        compiler_params=_cp("arbitrary", "arbitrary"))(u, u, u, u, u, u, da, da, da, cw_full, cw_full, cb3, cb3)
    return (dug, duv), jnp.concatenate([accg, accv], axis=1)


def _my_chip():
    return 2 * lax.axis_index("x") + lax.axis_index("y")


def _into_full(w, layer, dtype):
    L, a, b = w.shape
    tr = _pick(a, (512, 256, 128, 64, 32, 16, 8))

    def body(w_ref, o_ref):
        o_ref[...] = w_ref[...].astype(dtype)

    return pl.pallas_call(
        body, name="into_full", grid=(a // tr,),
        in_specs=[pl.BlockSpec((None, tr, b), lambda i: (layer, i, 0))],
        out_specs=pl.BlockSpec((None, None, tr, b), lambda i: (0, _my_chip(), i, 0)),
        out_shape=jax.ShapeDtypeStruct((1, N_CHIPS, a, b), dtype),
        compiler_params=_cp("arbitrary"))(w)


def _adam_math(w, g, m, v):
    m = ADAM_B1 * m + (1.0 - ADAM_B1) * g
    v = ADAM_B2 * v + (1.0 - ADAM_B2) * (g * g)
    m_hat = m / (1.0 - ADAM_B1 ** ADAM_STEP)
    v_hat = v / (1.0 - ADAM_B2 ** ADAM_STEP)
    delta = -ADAM_LR * (m_hat / (jnp.sqrt(v_hat) + ADAM_EPS) + ADAM_WD * w)
    return delta, m, v


def _adamw(w, g, m, v, keep_g=False):
    R, C = w.shape
    tr = _pick(R, (128, 64, 32, 16, 8)) if R % SUBLANES == 0 and C % LANES == 0 else R
    n_out = 4 if keep_g else 3

    def body(w_ref, g_ref, m_ref, v_ref, d_ref, nm_ref, nv_ref, *rest):
        g = g_ref[...]
        d, nm, nv = _adam_math(w_ref[...], g, m_ref[...], v_ref[...])
        d_ref[...] = d
        nm_ref[...] = nm
        nv_ref[...] = nv
        if keep_g:
            rest[0][...] = g

    spec = pl.BlockSpec((tr, C), lambda i: (i, 0))
    return pl.pallas_call(
        body, name="adamw", grid=(R // tr,), in_specs=[spec] * 4, out_specs=[spec] * n_out,
        out_shape=[jax.ShapeDtypeStruct((R, C), F32)] * n_out, compiler_params=_cp("arbitrary"))(w, g, m, v)


ANY = pl.BlockSpec(memory_space=pl.ANY)


def _position():
    x, y, c = lax.axis_index("x"), lax.axis_index("y"), lax.axis_index("c")
    chips = [(1 - x, y), (x, 1 - y), (1 - x, 1 - y)]
    return x, y, c, chips


HBM = pl.BlockSpec(memory_space=pltpu.HBM)
SEM = pl.BlockSpec(memory_space=pltpu.SEMAPHORE)
EFFECT = pltpu.SideEffectType.DATAFLOW_SIDE_EFFECTING


def _in_hbm(a):
    return pltpu.with_memory_space_constraint(a, pltpu.HBM)


def _shard_half(buf, shape, chip, half):
    _, _, a, b = shape
    p = 2 * chip[0] + chip[1]
    if a % (4 * SUBLANES) == 0:
        return buf.at[0, p, pl.ds(half * (a // 2), a // 2)]
    return buf.at[0, p, :, pl.ds(half * (b // 2), b // 2)]


def _gather_copy(buf, shape, chip, half, to, send, recv, k):
    part = _shard_half(buf, shape, chip, half)
    return pltpu.make_async_remote_copy(src_ref=part, dst_ref=part, send_sem=send.at[k], recv_sem=recv.at[k],
                                        device_id=to, device_id_type=MESH)


def _gather_hop(buf, shape, hop, j, incoming, send, recv, k):
    x, y, c, chips = _position()
    if hop == "chips":
        chip, half, to = (chips[j] if incoming else (x, y)), c, (*chips[j], c)
    else:
        chip, half, to = chips[j], (1 - c if incoming else c), (x, y, 1 - c)
    return _gather_copy(buf, shape, chip, half, to, send, recv, k)


def _gather_start(fulls, name, hop, after=()):
    n = len(fulls)
    na = len(after)

    def body(*refs):
        send, recv = refs[n + na], refs[n + na + 1]
        buf, token = refs[n + na + 2:2 * n + na + 2], refs[2 * n + na + 2]
        for t in range(n):
            for j in range(3):
                _gather_hop(buf[t], fulls[t].shape, hop, j, False, send, recv, 3 * t + j).start()
        token[...] = jnp.zeros_like(token)

    outs = pl.pallas_call(
        body, name=name, in_specs=[HBM] * n + [ANY] * na,
        out_specs=[SEM, SEM] + [HBM] * n + [pl.BlockSpec(memory_space=pltpu.VMEM)],
        out_shape=[pltpu.SemaphoreType.DMA((3 * n,)), pltpu.SemaphoreType.DMA((3 * n,))]
        + [pltpu.HBM(f.shape, f.dtype) for f in fulls] + [jax.ShapeDtypeStruct((SUBLANES, LANES), F32)],
        input_output_aliases={t: 2 + t for t in range(n)},
        compiler_params=pltpu.CompilerParams(has_side_effects=EFFECT))(*[_in_hbm(f) for f in fulls], *after)
    return outs[0], outs[1], list(outs[2:2 + n]), outs[2 + n]


def _gather_wait(send, recv, fulls, after, name, hop):
    n = len(fulls)

    def body(*refs):
        buf, send_ref, recv_ref = refs[:n], refs[n], refs[n + 1]
        for t in range(n):
            for j in range(3):
                _gather_hop(buf[t], fulls[t].shape, hop, j, False, send_ref, recv_ref, 3 * t + j).wait_send()
                _gather_hop(buf[t], fulls[t].shape, hop, j, True, send_ref, recv_ref, 3 * t + j).wait_recv()

    outs = pl.pallas_call(
        body, name=name, in_specs=[HBM] * n + [SEM, SEM] + [ANY] * len(after), out_specs=[HBM] * n,
        out_shape=[pltpu.HBM(f.shape, f.dtype) for f in fulls],
        input_output_aliases={t: t for t in range(n)},
        compiler_params=pltpu.CompilerParams(has_side_effects=EFFECT))(*fulls, send, recv, *after)
    return list(outs)


def _allreduce_small(part):
    M, C = part.shape
    n_dev = 2 * N_CHIPS

    def body(x_ref, sum_ref, all_ref, send, recv, local):
        x, y, c, chips = _position()
        me, sib = (x, y, c), (x, y, 1 - c)

        def rows(px, py, pc):
            return all_ref.at[pl.ds((4 * px + 2 * py + pc) * M, M), :]

        def copy(k, block, to, src=None):
            return pltpu.make_async_remote_copy(
                src_ref=rows(*block) if src is None else src, dst_ref=rows(*block),
                send_sem=send.at[k], recv_sem=recv.at[k], device_id=to, device_id_type=MESH)

        mine = pltpu.make_async_copy(x_ref, rows(*me), local)
        mine.start()
        first = [copy(0, me, sib, src=x_ref)] + [copy(1 + j, me, (*chip, c), src=x_ref) for j, chip in enumerate(chips)]
        for cp in first:
            cp.start()
        passed = [copy(4 + j, (*chip, c), sib) for j, chip in enumerate(chips)]
        for j, chip in enumerate(chips):
            copy(1 + j, (*chip, c), me).wait_recv()
            passed[j].start()
        copy(0, sib, me).wait_recv()
        for j, chip in enumerate(chips):
            copy(4 + j, (*chip, 1 - c), me).wait_recv()
        for cp in first + passed:
            cp.wait_send()
        mine.wait()
        acc = all_ref[0:M, :]
        for d in range(1, n_dev):
            acc = acc + all_ref[d * M:(d + 1) * M, :]
        sum_ref[...] = acc

    vm = pl.BlockSpec(memory_space=pltpu.VMEM)
    return pl.pallas_call(
        body, name="allreduce_small", in_specs=[vm], out_specs=[vm],
        out_shape=[jax.ShapeDtypeStruct((M, C), F32)],
        scratch_shapes=[pltpu.VMEM((n_dev * M, C), F32), pltpu.SemaphoreType.DMA((7,)),
                        pltpu.SemaphoreType.DMA((7,)), pltpu.SemaphoreType.DMA],
        compiler_params=pltpu.CompilerParams(vmem_limit_bytes=VMEM_LIMIT))(part)[0]


N_PEERS = 2 * N_CHIPS - 1


def _peers():
    x, y, c, chips = _position()
    return [(x, y, 1 - c)] + [(*ch, c) for ch in chips] + [(*ch, 1 - c) for ch in chips]


def _reduce_copy(src, dst, peers, send, recv, t, r):
    px, py, pc = peers[r]
    return pltpu.make_async_remote_copy(
        src_ref=src.at[2 * px + py, pc], dst_ref=dst.at[r], send_sem=send.at[N_PEERS * t + r],
        recv_sem=recv.at[N_PEERS * t + r], device_id=peers[r], device_id_type=MESH)


def _reduce_start(grads, name, after=()):
    n = len(grads)
    na = len(after)
    lands = [lax.empty((N_PEERS,) + g.shape[2:], BF16) for g in grads]

    def body(*refs):
        send, recv = refs[2 * n + na], refs[2 * n + na + 1]
        src, dst = refs[2 * n + na + 2:3 * n + na + 2], refs[3 * n + na + 2:4 * n + na + 2]
        token = refs[4 * n + na + 2]
        peers = _peers()
        for t in range(n):
            for r in range(N_PEERS):
                _reduce_copy(src[t], dst[t], peers, send, recv, t, r).start()
        token[...] = jnp.zeros_like(token)

    outs = pl.pallas_call(
        body, name=name, in_specs=[HBM] * (2 * n) + [ANY] * na,
        out_specs=[SEM, SEM] + [HBM] * (2 * n) + [pl.BlockSpec(memory_space=pltpu.VMEM)],
        out_shape=[pltpu.SemaphoreType.DMA((N_PEERS * n,)), pltpu.SemaphoreType.DMA((N_PEERS * n,))]
        + [pltpu.HBM(a.shape, a.dtype) for a in grads + lands] + [jax.ShapeDtypeStruct((SUBLANES, LANES), F32)],
        input_output_aliases={t: 2 + t for t in range(2 * n)},
        compiler_params=pltpu.CompilerParams(has_side_effects=EFFECT))(*[_in_hbm(a) for a in grads + lands], *after)
    return outs[0], outs[1], list(outs[2:2 + n]), list(outs[2 + n:2 + 2 * n]), outs[2 + 2 * n]


def _reduce_wait(send, recv, grads, lands, after, name):
    n = len(grads)

    def body(*refs):
        src, dst, send_ref, recv_ref = refs[:n], refs[n:2 * n], refs[2 * n], refs[2 * n + 1]
        peers = _peers()
        for t in range(n):
            for r in range(N_PEERS):
                cp = _reduce_copy(src[t], dst[t], peers, send_ref, recv_ref, t, r)
                cp.wait_send()
                cp.wait_recv()

    outs = pl.pallas_call(
        body, name=name, in_specs=[HBM] * (2 * n) + [SEM, SEM] + [ANY] * len(after), out_specs=[HBM] * (2 * n),
        out_shape=[pltpu.HBM(a.shape, a.dtype) for a in grads + lands],
        input_output_aliases={t: t for t in range(2 * n)},
        compiler_params=pltpu.CompilerParams(has_side_effects=EFFECT))(*grads, *lands, send, recv, *after)
    return list(outs[:n]), list(outs[n:])


def _add_pieces(grad, land, stack, layer):
    _, _, R, C = grad.shape
    tr = _pick(R, (256, 128, 64, 32, 16))

    def body(g_ref, r_ref, stack_ref, o_ref):
        acc = g_ref[...].astype(F32)
        for r in range(N_PEERS):
            acc = acc + r_ref[r].astype(F32)
        o_ref[...] = acc

    return pl.pallas_call(
        body, name="add_pieces", grid=(R // tr,),
        in_specs=[pl.BlockSpec((None, None, tr, C), lambda i: (_my_chip(), lax.axis_index("c"), i, 0)),
                  pl.BlockSpec((N_PEERS, tr, C), lambda i: (0, i, 0)),
                  ANY],
        out_specs=pl.BlockSpec((None, None, tr, C), lambda i: (layer, lax.axis_index("c"), i, 0)),
        out_shape=jax.ShapeDtypeStruct(stack.shape, F32), input_output_aliases={2: 0},
        compiler_params=_cp("arbitrary"))(grad, land, stack)


def _ag_sibling(stacks):
    n = len(stacks)
    offs = np.cumsum([0] + [s.shape[0] for s in stacks])

    def body(*refs):
        buf, send, recv = refs[n:2 * n], refs[2 * n], refs[2 * n + 1]
        x, y, c, _ = _position()

        def copy(t, l, half):
            part = buf[t].at[l, half]
            return pltpu.make_async_remote_copy(
                src_ref=part, dst_ref=part, send_sem=send.at[int(offs[t]) + l], recv_sem=recv.at[int(offs[t]) + l],
                device_id=(x, y, 1 - c), device_id_type=MESH)

        cps = [copy(t, l, c) for t in range(n) for l in range(stacks[t].shape[0])]
        for cp in cps:
            cp.start()
        for t in range(n):
            for l in range(stacks[t].shape[0]):
                copy(t, l, 1 - c).wait_recv()
        for cp in cps:
            cp.wait_send()

    return pl.pallas_call(
        body, name="ag_sibling", in_specs=[ANY] * n, out_specs=[ANY] * n,
        out_shape=[jax.ShapeDtypeStruct(s.shape, F32) for s in stacks],
        input_output_aliases={t: t for t in range(n)},
        scratch_shapes=[pltpu.SemaphoreType.DMA((int(offs[-1]),)), pltpu.SemaphoreType.DMA((int(offs[-1]),))])(*stacks)


def _split8(dw, blocked):
    if blocked:
        p, k, nq = dw.shape
        return dw.reshape(p, 2, k // 2, nq)
    k, n = dw.shape
    return dw.reshape(N_CHIPS, 2, k // (2 * N_CHIPS), n)


def kernel(x, a_w_qkv, a_w_o, a_q_gain, a_k_gain, b_w_qkv, b_w_o, rel_bias, mix_norm, ffn_norm, w_up, conv_w, conv_b, w_down, final_norm, loss_target, m_a_w_qkv, m_a_w_o, m_a_q_gain, m_a_k_gain, m_b_w_qkv, m_b_w_o, m_rel_bias, m_mix_norm, m_ffn_norm, m_w_up, m_conv_w, m_conv_b, m_w_down, m_final_norm, v_a_w_qkv, v_a_w_o, v_a_q_gain, v_a_k_gain, v_b_w_qkv, v_b_w_o, v_rel_bias, v_mix_norm, v_ffn_norm, v_w_up, v_conv_w, v_conv_b, v_w_down, v_final_norm):
    S, D = x.shape[1], x.shape[2]
    h = x.reshape(S, D)
    target = loss_target.reshape(S, D)
    hg = B_HEADS_PER_GROUP
    G = len(B_GROUPS)
    n_a, n_b = a_w_qkv.shape[0], b_w_qkv.shape[0]
    depth = w_up.shape[0]
    cx, cy = lax.axis_index("x"), lax.axis_index("y")

    big = dict(a_w_qkv=a_w_qkv, a_w_o=a_w_o, b_w_qkv=b_w_qkv, b_w_o=b_w_o, w_up=w_up, w_down=w_down)
    blocked = dict(a_w_qkv=True, a_w_o=False, b_w_qkv=True, b_w_o=False, w_up=True, w_down=False)
    names = list(big)
    srcs = dict(big, conv_w=conv_w)
    started = []
    for i in range(depth):
        mix = [("a_w_qkv", i // 2), ("a_w_o", i // 2)] if i % 2 == 0 else [("b_w_qkv", i // 2), ("b_w_o", i // 2)]
        rest = [("w_up", i), ("conv_w", i), ("w_down", i)]
        stages = [mix[:1], mix[1:], rest] if i == 0 else [mix + rest]
        started.append([])
        for s, keys in enumerate(stages):
            bufs = [_into_full(srcs[k], l, F32 if k == "conv_w" else BF16) for k, l in keys]
            started[i].append((keys,) + _gather_start(bufs, "gather_start_%d_%d" % (i, s), "chips"))
    cb3 = conv_b.reshape(depth, 1, conv_b.shape[1])

    cos, sin = _rope_tables(S)
    buckets = jnp.asarray(_bucket_tables(False))
    bias = _bias_build(rel_bias, buckets)
    bias_t = _bias_build(rel_bias, jnp.asarray(_bucket_tables(True)))

    saved = []
    passing = {}

    def land(i, s, after):
        keys, send, recv, bufs, _ = started[i][s]
        bufs = _gather_wait(send, recv, bufs, after, "gather_wait_%d_%d" % (i, s), "chips")
        send, recv, bufs, token = _gather_start(bufs, "pass_start_%d_%d" % (i, s), "sibling")
        passing[i, s] = (keys, send, recv, bufs)
        return token

    def arrive(i, s, after, wl):
        keys, send, recv, bufs = passing.pop((i, s))
        bufs = _gather_wait(send, recv, bufs, after, "pass_wait_%d_%d" % (i, s), "sibling")
        for (k, _), buf in zip(keys, bufs):
            _, _, a, b = buf.shape
            wl[k] = buf if k == "conv_w" or blocked[k] else buf.reshape(1, N_CHIPS * a, b)

    first = [land(0, 0, [h])]
    for i in range(depth):
        j = i // 2
        wl = {}
        arrive(i, 0, [h], wl)
        sv = dict(h0=h, w=wl)
        hn = _rms_fwd(h, mix_norm[i:i + 1], after=[st[4] for layer in started for st in layer] + first if i == 0 else ())
        sv["hn"] = hn
        if i % 2 == 0:
            qkv = _mm_nn(hn, wl["a_w_qkv"], 0, blocked=True, name="a_qkv")
            qkvh = _prep_a_fwd(qkv, cos, sin, a_q_gain[j:j + 1], a_k_gain[j:j + 1])
            staged = len(started[i]) > 1
            o, lse = _flash_a_fwd(qkvh, after=[land(i, 1, [qkvh])] if staged else ())
            tok = ()
            if staged:
                arrive(i, 1, [o], wl)
                tok = [land(i, 2, [o])]
            sv.update(qkv=qkv, qkvh=qkvh, o=o, lse=lse)
            h = _mm_nn(o, wl["a_w_o"], 0, blocked=False, res=h, name="a_out", after=tok)
        else:
            qkvp = [_mm_nn_perm(hn, wl["b_w_qkv"], g) for g in range(G)]
            os_, lzs = [], []
            for g in range(G):
                o_g, lz_g = _battn_fwd(qkvp[g], bias, g)
                os_.append(o_g)
                lzs.append(lz_g)
            y = _combine_fwd(os_, lzs)
            sv.update(qkvp=qkvp, os=os_, lzs=lzs, y=y)
            h = _mm_nn(y, wl["b_w_o"], 0, blocked=False, res=h, name="b_out")
        sv["h1"] = h
        hf = _rms_fwd(h, ffn_norm[i:i + 1])
        if len(started[i]) > 2:
            arrive(i, 2, [hf], wl)
        u = _mm_nn(hf, wl["w_up"], 0, blocked=True, name="ffn_up")
        act = _ffn_act_fwd(u, wl["conv_w"], cb3[i:i + 1], 0)
        sv.update(hf=hf, u=u, act=act)
        nxt = [land(i + 1, 0, [act])] if i + 1 < depth else ()
        h = _mm_nn(act, wl["w_down"], 0, blocked=False, res=h, name="ffn_down", after=nxt)
        saved.append(sv)

    loss_blk, dh, dh_b, dg_final = _final_loss(h, final_norm.reshape(1, D), target)

    dws = {k: [None] * big[k].shape[0] for k in names}
    d_mix, d_ffn, d_convw, d_convb = [None] * depth, [None] * depth, [None] * depth, [None] * depth
    d_gq, d_gk = [None] * n_a, [None] * n_a
    dbias_list = []
    pending = []

    def start_reduce(keys, tag, after=()):
        pieces = [_split8(dws[k][l], blocked[k]) for k, l in keys]
        send, recv, pieces, lands, token = _reduce_start(pieces, "reduce_start_" + tag, after)
        pending.append((keys, send, recv, pieces, lands, tag))
        return (token,)

    tok = ()
    for i in reversed(range(depth)):
        j = i // 2
        sv = saved[i]
        wl = sv["w"]
        da = _mm_nt(dh_b, wl["w_down"], 0, blocked=False, name="ffn_down_dx", after=tok)
        dws["w_down"][i] = _mm_tn(sv["act"], dh_b, blocked=False, name="ffn_down_dw")
        du, dconv = _ffn_act_bwd(sv["u"], da, wl["conv_w"], cb3[i:i + 1], 0)
        d_convw[i], d_convb[i] = dconv[0:3], dconv[3]
        dhf = _mm_nt(du, wl["w_up"], 0, blocked=True, name="ffn_up_dx")
        dws["w_up"][i] = _mm_tn(sv["hf"], du, blocked=True, name="ffn_up_dw")
        dh, dh_b, dg = _rms_bwd(dhf, sv["h1"], ffn_norm[i:i + 1], dh)
        d_ffn[i] = dg[0]
        tok = start_reduce([("w_down", i), ("w_up", i)], "ffn%d" % i)
        if i % 2 == 0:
            do = _mm_nt(dh_b, wl["a_w_o"], 0, blocked=False, name="a_out_dx", after=tok)
            dws["a_w_o"][j] = _mm_tn(sv["o"], dh_b, blocked=False, name="a_out_dw")
            dq, dk, dv = _flash_a_bwd(sv["qkvh"], do, sv["o"], sv["lse"])
            dqkv, dgain = _prep_a_bwd(dq, dk, dv, sv["qkv"], cos, sin, a_q_gain[j:j + 1], a_k_gain[j:j + 1])
            d_gq[j], d_gk[j] = dgain[0], dgain[1]
            dhn = _mm_nt(dqkv, wl["a_w_qkv"], 0, blocked=True, name="a_qkv_dx")
            dws["a_w_qkv"][j] = _mm_tn(sv["hn"], dqkv, blocked=True, name="a_qkv_dw")
            mix_keys = [("a_w_o", j), ("a_w_qkv", j)]
        else:
            dy = _mm_nt(dh_b, wl["b_w_o"], 0, blocked=False, name="b_out_dx", after=tok)
            dws["b_w_o"][j] = _mm_tn(sv["y"], dh_b, blocked=False, name="b_out_dw")
            dos, dlzs = _combine_bwd(dy, sv["os"], sv["lzs"])
            parts = []
            for g in range(G):
                dq, rt, db = _battn_bwd_dq(sv["qkvp"][g], bias, dos[g], sv["os"][g], sv["lzs"][g], dlzs[g], g)
                dk, dv = _battn_bwd_dkv(sv["qkvp"][g], bias_t, dos[g], sv["lzs"][g], rt, g)
                parts += [dq, dk, dv]
                dbias_list.append((g, db))
            dqkv = _concat_cast(parts, [d for _, d in B_GROUPS for _ in range(3)])
            dhn = _mm_nt(dqkv, wl["b_w_qkv"], 0, blocked=True, name="b_qkv_dx")
            dws["b_w_qkv"][j] = _mm_tn(sv["hn"], dqkv, blocked=True, name="b_qkv_dw")
            mix_keys = [("b_w_o", j), ("b_w_qkv", j)]
        dh, dh_b, dg = _rms_bwd(dhn, sv["h0"], mix_norm[i:i + 1], dh)
        d_mix[i] = dg[0]
        if i > 0:
            tok = start_reduce(mix_keys, "mix%d" % i)
    grad_x = dh.reshape(x.shape)

    dbias_layers = [jnp.stack([db for g2, db in dbias_list[l * G:(l + 1) * G]]) for l in range(n_b)]
    d_rel = _bias_reduce(dbias_layers, buckets)[:, :G * hg]

    small = [jnp.stack(d_gq), jnp.stack(d_gk), d_rel, jnp.stack(d_mix), jnp.stack(d_ffn), jnp.stack(d_convw),
             jnp.stack(d_convb), dg_final[0]]
    sizes = [int(np.prod(s.shape)) for s in small]
    flat = jnp.concatenate([s.reshape(-1) for s in small])
    rows = -(-flat.shape[0] // (LANES * SUBLANES)) * SUBLANES
    flat = jnp.pad(flat, (0, rows * LANES - flat.shape[0])).reshape(rows, LANES)
    tot = _allreduce_small(flat)
    start_reduce(mix_keys, "mix0", after=[tot])
    tot = tot.reshape(-1)
    offs = np.cumsum([0] + sizes)
    g_gq, g_gk, g_rel, g_mix, g_ffn, g_convw_full, g_convb, g_final = [
        tot[offs[k]:offs[k + 1]].reshape(small[k].shape) for k in range(len(small))]
    cq = conv_w.shape[2]
    g_convw = lax.dynamic_slice_in_dim(g_convw_full, (2 * cx + cy) * cq, cq, axis=2)

    grads = dict(a_q_gain=g_gq, a_k_gain=g_gk, rel_bias=g_rel, mix_norm=g_mix, ffn_norm=g_ffn,
                 conv_w=g_convw, conv_b=g_convb, final_norm=g_final)
    weights = dict(a_w_qkv=a_w_qkv, a_w_o=a_w_o, a_q_gain=a_q_gain, a_k_gain=a_k_gain, b_w_qkv=b_w_qkv, b_w_o=b_w_o,
                   rel_bias=rel_bias, mix_norm=mix_norm, ffn_norm=ffn_norm, w_up=w_up, conv_w=conv_w, conv_b=conv_b,
                   w_down=w_down, final_norm=final_norm)
    ms = dict(a_w_qkv=m_a_w_qkv, a_w_o=m_a_w_o, a_q_gain=m_a_q_gain, a_k_gain=m_a_k_gain, b_w_qkv=m_b_w_qkv,
              b_w_o=m_b_w_o, rel_bias=m_rel_bias, mix_norm=m_mix_norm, ffn_norm=m_ffn_norm, w_up=m_w_up,
              conv_w=m_conv_w, conv_b=m_conv_b, w_down=m_w_down, final_norm=m_final_norm)
    vs = dict(a_w_qkv=v_a_w_qkv, a_w_o=v_a_w_o, a_q_gain=v_a_q_gain, a_k_gain=v_a_k_gain, b_w_qkv=v_b_w_qkv,
              b_w_o=v_b_w_o, rel_bias=v_rel_bias, mix_norm=v_mix_norm, ffn_norm=v_ffn_norm, w_up=v_w_up,
              conv_w=v_conv_w, conv_b=v_conv_b, w_down=v_w_down, final_norm=v_final_norm)
    deltas, new_m, new_v, stacks = {}, {}, {}, {}

    def update(k):
        w = weights[k]
        two_d = (-1, w.shape[-1])
        outs = _adamw(w.reshape(two_d), grads[k].reshape(two_d), ms[k].reshape(two_d), vs[k].reshape(two_d),
                      keep_g=k in big)
        deltas[k], new_m[k], new_v[k] = [a.reshape(w.shape) for a in outs[:3]]
        if k in big:
            grads[k] = outs[3]
        return outs[2]

    def collect(items, after):
        for keys, send, recv, pieces, lands, tag in items:
            pieces, lands = _reduce_wait(send, recv, pieces, lands, after, "reduce_wait_" + tag)
            for (k, l), p, land in zip(keys, pieces, lands):
                if k not in stacks:
                    stacks[k] = lax.empty((big[k].shape[0], 2) + p.shape[2:], F32)
                stacks[k] = _add_pieces(p, land, stacks[k], l)

    def share(ks):
        for k, gs in zip(ks, _ag_sibling([stacks[k] for k in ks])):
            grads[k] = gs.reshape(big[k].shape)

    late = [k for k in names if k in {kk for kk, _ in pending[-1][0]}]
    collect(pending[:-1], [dh])
    share([k for k in names if k not in late])
    done = [update(k) for k in weights if k not in late]
    collect(pending[-1:], done)
    share(late)
    for k in late:
        update(k)

    loss = lax.psum(loss_blk[0, 0], ("x", "y", "c"))
    keys = list(weights)
    return (loss, grad_x, *[grads[k].reshape(weights[k].shape) for k in keys], *[deltas[k] for k in keys],
            *[new_m[k] for k in keys], *[new_v[k] for k in keys])
```

```python
import functools
import math

import numpy as np
import jax
import jax.numpy as jnp
from jax import lax
from jax.experimental import pallas as pl
from jax.experimental.pallas import tpu as pltpu

F32 = jnp.float32
BF16 = jnp.bfloat16

HEAD_DIM = 128
A_HEADS = 16
A_KV_HEADS = 4
GRID_W = 64
ROPE_THETA = 10000.0
B_GROUPS = ((128, 1), (512, 4), (2048, 16))
B_HEADS_PER_GROUP = 8
REL_BUCKETS = 32
REL_MAX_DISTANCE = 1024
EPS = 1e-6
NEG_INF = -1e30
DEPTH = 4
ADAM_LR = 0.001
ADAM_B1 = 0.9
ADAM_B2 = 0.999
ADAM_EPS = 1e-08
ADAM_WD = 0.01
ADAM_STEP = 10

N_CHIPS = 4
LANES = 128
SUBLANES = 8
VMEM_LIMIT = 52 * 1024 * 1024
MESH = pl.DeviceIdType.MESH


def _pick(n, cands):
    for c in cands:
        if c <= n and n % c == 0:
            return c
    return n


def _lane_tile(n, cap):
    best = None
    for t in range(LANES, min(n, cap) + 1, LANES):
        if n % t == 0:
            best = t
    return best or n


def _cp(*sem):
    return pltpu.CompilerParams(dimension_semantics=sem if sem else None, vmem_limit_bytes=VMEM_LIMIT)


def _half_span():
    hs = {w // (2 * d) for w, d in B_GROUPS}
    assert len(hs) == 1
    return hs.pop()


def _rms_fwd(h, gain, after=()):
    S, D = h.shape
    ts = _pick(S, (512, 256, 128, 64, 32, 16))

    def body(h_ref, g_ref, *rest):
        o_ref = rest[-1]
        x = h_ref[...]
        r = lax.rsqrt(jnp.mean(x * x, axis=-1, keepdims=True) + EPS)
        o_ref[...] = (x * r * g_ref[...]).astype(o_ref.dtype)

    return pl.pallas_call(
        body, name="rms_fwd", grid=(S // ts,),
        in_specs=[pl.BlockSpec((ts, D), lambda i: (i, 0)), pl.BlockSpec((1, D), lambda i: (0, 0))]
        + [pl.BlockSpec(memory_space=pl.ANY)] * len(after),
        out_specs=pl.BlockSpec((ts, D), lambda i: (i, 0)),
        out_shape=jax.ShapeDtypeStruct((S, D), BF16), compiler_params=_cp("arbitrary"))(h, gain, *after)


def _rms_bwd(dy, h, gain, dres):
    S, D = h.shape
    ts = _pick(S, (256, 128, 64, 32, 16))

    def body(dy_ref, h_ref, g_ref, dres_ref, dh_ref, dhb_ref, dg_ref):
        @pl.when(pl.program_id(0) == 0)
        def _():
            dg_ref[...] = jnp.zeros_like(dg_ref)
        x = h_ref[...]
        dy = dy_ref[...]
        r = lax.rsqrt(jnp.mean(x * x, axis=-1, keepdims=True) + EPS)
        xn = x * r
        dg_ref[0:1, :] += jnp.sum(dy * xn, axis=0, keepdims=True)
        dxn = dy * g_ref[...]
        dx = r * (dxn - xn * jnp.mean(dxn * xn, axis=-1, keepdims=True))
        dh = dres_ref[...] + dx
        dh_ref[...] = dh
        dhb_ref[...] = dh.astype(BF16)

    row = pl.BlockSpec((ts, D), lambda i: (i, 0))
    return pl.pallas_call(
        body, name="rms_bwd", grid=(S // ts,),
        in_specs=[row, row, pl.BlockSpec((1, D), lambda i: (0, 0)), row],
        out_specs=[row, row, pl.BlockSpec((SUBLANES, D), lambda i: (0, 0))],
        out_shape=[jax.ShapeDtypeStruct((S, D), F32), jax.ShapeDtypeStruct((S, D), BF16),
                   jax.ShapeDtypeStruct((SUBLANES, D), F32)],
        compiler_params=_cp("arbitrary"))(dy, h, gain, dres)


def _final_loss(h, gain, target):
    S, D = h.shape
    ts = _pick(S, (256, 128, 64, 32, 16))

    def body(h_ref, g_ref, t_ref, loss_ref, dh_ref, dhb_ref, dg_ref):
        @pl.when(pl.program_id(0) == 0)
        def _():
            dg_ref[...] = jnp.zeros_like(dg_ref)
            loss_ref[...] = jnp.zeros_like(loss_ref)
        x = h_ref[...]
        g = g_ref[...]
        r = lax.rsqrt(jnp.mean(x * x, axis=-1, keepdims=True) + EPS)
        xn = x * r
        err = xn * g - t_ref[...]
        part = 0.5 * jnp.sum(jnp.mean(err * err, axis=-1, keepdims=True), axis=0, keepdims=True)
        loss_ref[0:1, 0:1] += part
        dy = err * (1.0 / D)
        dg_ref[0:1, :] += jnp.sum(dy * xn, axis=0, keepdims=True)
        dxn = dy * g
        dh = r * (dxn - xn * jnp.mean(dxn * xn, axis=-1, keepdims=True))
        dh_ref[...] = dh
        dhb_ref[...] = dh.astype(BF16)

    row = pl.BlockSpec((ts, D), lambda i: (i, 0))
    return pl.pallas_call(
        body, name="final_loss", grid=(S // ts,),
        in_specs=[row, pl.BlockSpec((1, D), lambda i: (0, 0)), row],
        out_specs=[pl.BlockSpec((SUBLANES, LANES), lambda i: (0, 0)), row, row,
                   pl.BlockSpec((SUBLANES, D), lambda i: (0, 0))],
        out_shape=[jax.ShapeDtypeStruct((SUBLANES, LANES), F32), jax.ShapeDtypeStruct((S, D), F32),
                   jax.ShapeDtypeStruct((S, D), BF16), jax.ShapeDtypeStruct((SUBLANES, D), F32)],
        compiler_params=_cp("arbitrary"))(h, gain, target)


_NN = (((1,), (0,)), ((), ()))
_NT = (((1,), (1,)), ((), ()))
_TN = (((0,), (0,)), ((), ()))


def _mm_nn(a, w, layer, *, blocked, out_dtype=F32, res=None, name, after=()):
    M, K = a.shape
    if blocked:
        nq = w.shape[3]
        N = N_CHIPS * nq
        tn = _lane_tile(nq, 1408)
        nps = nq // tn
        w_spec = pl.BlockSpec((None, None, K, tn), lambda i, j: (layer, j // nps, 0, j % nps))
    else:
        N = w.shape[2]
        tn = _lane_tile(N, 1024)
        w_spec = pl.BlockSpec((None, K, tn), lambda i, j: (layer, 0, j))
    tm = _pick(M, (1024, 512, 256, 128, 64, 32, 16)) if K <= 3072 else _pick(M, (512, 256, 128, 64, 32, 16))

    def body(*refs):
        a_ref, w_ref, o_ref = refs[0], refs[1], refs[-1]
        acc = lax.dot_general(a_ref[...], w_ref[...], _NN, preferred_element_type=F32)
        if res is not None:
            acc = refs[2][...] + acc
        o_ref[...] = acc.astype(o_ref.dtype)

    in_specs = [pl.BlockSpec((tm, K), lambda i, j: (i, 0)), w_spec]
    args = [a, w]
    if res is not None:
        in_specs.append(pl.BlockSpec((tm, tn), lambda i, j: (i, j)))
        args.append(res)
    return pl.pallas_call(
        body, name=name, grid=(M // tm, N // tn), in_specs=in_specs + [pl.BlockSpec(memory_space=pl.ANY)] * len(after),
        out_specs=pl.BlockSpec((tm, tn), lambda i, j: (i, j)),
        out_shape=jax.ShapeDtypeStruct((M, N), out_dtype),
        compiler_params=_cp("arbitrary", "arbitrary"))(*args, *after)


def _mm_nt(a, w, layer, *, blocked, name, after=()):
    pair = isinstance(a, tuple)
    M = a[0].shape[0] if pair else a.shape[0]
    tm = _pick(M, (1024, 512, 256, 128, 64, 32, 16))
    if blocked:
        K, nq = w.shape[2], w.shape[3]
        tk = _pick(K, (1024, 512, 256, 128))
        half = N_CHIPS // 2

        def body(*refs):
            a_refs, (w_ref, o_ref, acc_ref) = refs[:-3], refs[-3:]
            p = pl.program_id(2)

            @pl.when(p == 0)
            def _():
                acc_ref[...] = jnp.zeros_like(acc_ref)
            if pair:
                @pl.when(p < half)
                def _():
                    acc_ref[...] += lax.dot_general(a_refs[0][...], w_ref[...], _NT, preferred_element_type=F32)

                @pl.when(p >= half)
                def _():
                    acc_ref[...] += lax.dot_general(a_refs[1][...], w_ref[...], _NT, preferred_element_type=F32)
            else:
                acc_ref[...] += lax.dot_general(a_refs[0][...], w_ref[...], _NT, preferred_element_type=F32)

            @pl.when(p == N_CHIPS - 1)
            def _():
                o_ref[...] = acc_ref[...]

        if pair:
            a_specs = [pl.BlockSpec((tm, nq), lambda i, j, p: (i, jnp.minimum(p, half - 1))),
                       pl.BlockSpec((tm, nq), lambda i, j, p: (i, jnp.maximum(p - half, 0)))]
            a_args = list(a)
        else:
            a_specs = [pl.BlockSpec((tm, nq), lambda i, j, p: (i, p))]
            a_args = [a]
        return pl.pallas_call(
            body, name=name, grid=(M // tm, K // tk, N_CHIPS),
            in_specs=a_specs + [pl.BlockSpec((None, None, tk, nq), lambda i, j, p: (layer, p, j, 0))],
            out_specs=pl.BlockSpec((tm, tk), lambda i, j, p: (i, j)),
            out_shape=jax.ShapeDtypeStruct((M, K), F32),
            scratch_shapes=[pltpu.VMEM((tm, tk), F32)],
            compiler_params=_cp("arbitrary", "arbitrary", "arbitrary"))(*a_args, w)
    K, N = w.shape[1], w.shape[2]
    tk = _pick(K, (1024, 512, 256, 128))

    def body(a_ref, w_ref, *rest):
        rest[-1][...] = lax.dot_general(a_ref[...], w_ref[...], _NT, preferred_element_type=F32)

    return pl.pallas_call(
        body, name=name, grid=(M // tm, K // tk),
        in_specs=[pl.BlockSpec((tm, N), lambda i, j: (i, 0)),
                  pl.BlockSpec((None, tk, N), lambda i, j: (layer, j, 0))]
        + [pl.BlockSpec(memory_space=pl.ANY)] * len(after),
        out_specs=pl.BlockSpec((tm, tk), lambda i, j: (i, j)),
        out_shape=jax.ShapeDtypeStruct((M, K), F32),
        compiler_params=_cp("arbitrary", "arbitrary"))(a, w, *after)


def _mm_tn(x, dy, *, blocked, name):
    pair = isinstance(dy, tuple)
    S, K = x.shape
    N = 2 * dy[0].shape[1] if pair else dy.shape[1]
    tk = _pick(K, (512, 256, 128))
    if blocked:
        nq = N // N_CHIPS
        tn = _lane_tile(nq, 1408)
        nps = nq // tn
        out_spec = pl.BlockSpec((None, tk, tn), lambda i, j, s: (j // nps, i, j % nps))
        out_shape = jax.ShapeDtypeStruct((N_CHIPS, K, nq), BF16)
    else:
        tn = _lane_tile(N, 1024)
        out_spec = pl.BlockSpec((tk, tn), lambda i, j, s: (i, j))
        out_shape = jax.ShapeDtypeStruct((K, N), BF16)
    nj = N // tn
    njh = nj // 2
    ns = 2 if pair else 1
    sh = S // ns

    def body(x_ref, *refs):
        o_ref, acc_ref = refs[-2], refs[-1]

        def product(dy_ref):
            part = lax.dot_general(x_ref[...], dy_ref[...], _TN, preferred_element_type=F32)
            if ns == 1:
                o_ref[...] = part.astype(o_ref.dtype)
            else:
                s = pl.program_id(2)

                @pl.when(s == 0)
                def _():
                    acc_ref[...] = part

                @pl.when(s == ns - 1)
                def _():
                    o_ref[...] = (acc_ref[...] + part).astype(o_ref.dtype)

        if pair:
            j = pl.program_id(1)
            pl.when(j < njh)(lambda: product(refs[0]))
            pl.when(j >= njh)(lambda: product(refs[1]))
        else:
            product(refs[0])

    if pair:
        assert nj % 2 == 0
        dy_specs = [pl.BlockSpec((sh, tn), lambda i, j, s: (jnp.where(j < njh, s, ns - 1), jnp.minimum(j, njh - 1))),
                    pl.BlockSpec((sh, tn), lambda i, j, s: (jnp.where(j < njh, 0, s), jnp.maximum(j - njh, 0)))]
        dy_args = list(dy)
    else:
        dy_specs = [pl.BlockSpec((sh, tn), lambda i, j, s: (s, j))]
        dy_args = [dy]
    return pl.pallas_call(
        body, name=name, grid=(K // tk, nj, ns),
        in_specs=[pl.BlockSpec((sh, tk), lambda i, j, s: (s, i))] + dy_specs,
        out_specs=out_spec, out_shape=out_shape,
        scratch_shapes=[pltpu.VMEM((tk, tn) if ns > 1 else (SUBLANES, LANES), F32)],
        compiler_params=_cp("arbitrary", "arbitrary", "arbitrary"))(x, *dy_args)


def _rope_tables(S):
    rows = S // GRID_W
    row_ids = jnp.repeat(jnp.arange(rows, dtype=F32), GRID_W)
    col_ids = jnp.tile(jnp.arange(GRID_W, dtype=F32), rows)
    quarter = HEAD_DIM // 4
    inv_freq = ROPE_THETA ** (-jnp.arange(quarter, dtype=F32) / quarter)
    ang_r = row_ids[:, None] * inv_freq[None, :]
    ang_c = col_ids[:, None] * inv_freq[None, :]
    cos = jnp.concatenate([jnp.cos(ang_r)] * 2 + [jnp.cos(ang_c)] * 2, axis=-1)
    sin = jnp.concatenate([-jnp.sin(ang_r), jnp.sin(ang_r), -jnp.sin(ang_c), jnp.sin(ang_c)], axis=-1)
    return cos, sin


def _swap_quarters(x):
    lane = lax.broadcasted_iota(jnp.int32, x.shape, 1)
    first = (lane % (HEAD_DIM // 2)) < (HEAD_DIM // 4)
    return jnp.where(first, pltpu.roll(x, HEAD_DIM - HEAD_DIM // 4, 1), pltpu.roll(x, HEAD_DIM // 4, 1))


A_SCALE = HEAD_DIM ** -0.5
A_QSCALE = A_SCALE * math.log2(math.e)


def _prep_a_fwd(qkv, cos, sin, gq, gk):
    S, W = qkv.shape
    nrm = A_HEADS + A_KV_HEADS
    ts = _pick(S, (256, 128, 64, 32, 16))

    def body(qkv_ref, cos_ref, sin_ref, gq_ref, gk_ref, o_ref):
        cos_t = cos_ref[...]
        sin_t = sin_ref[...]
        for j in range(nrm):
            sl = slice(j * HEAD_DIM, (j + 1) * HEAD_DIM)
            x = qkv_ref[:, sl]
            g = gq_ref[...] if j < A_HEADS else gk_ref[...]
            r = lax.rsqrt(jnp.mean(x * x, axis=-1, keepdims=True) + EPS)
            n = x * r * g
            y = n * cos_t + _swap_quarters(n) * sin_t
            o_ref[:, sl] = (y * A_QSCALE if j < A_HEADS else y).astype(BF16)
        o_ref[:, nrm * HEAD_DIM:] = qkv_ref[:, nrm * HEAD_DIM:].astype(BF16)

    row = lambda w: pl.BlockSpec((ts, w), lambda i: (i, 0))
    one = pl.BlockSpec((1, HEAD_DIM), lambda i: (0, 0))
    return pl.pallas_call(
        body, name="prep_a_fwd", grid=(S // ts,),
        in_specs=[row(W), row(HEAD_DIM), row(HEAD_DIM), one, one], out_specs=row(W),
        out_shape=jax.ShapeDtypeStruct((S, W), BF16), compiler_params=_cp("arbitrary"))(qkv, cos, sin, gq, gk)


def _prep_a_bwd(dq, dk, dv, qkv, cos, sin, gq, gk):
    S, W = qkv.shape
    nrm = A_HEADS + A_KV_HEADS
    nq, nk = A_HEADS * HEAD_DIM, A_KV_HEADS * HEAD_DIM
    ts = _pick(S, (256, 128, 64, 32, 16))

    def body(dq_ref, dk_ref, dv_ref, qkv_ref, cos_ref, sin_ref, gq_ref, gk_ref, o_ref, dg_ref):
        @pl.when(pl.program_id(0) == 0)
        def _():
            dg_ref[...] = jnp.zeros_like(dg_ref)
        cos_t = cos_ref[...]
        sin_t = sin_ref[...]
        for j in range(nrm):
            sl = slice(j * HEAD_DIM, (j + 1) * HEAD_DIM)
            x = qkv_ref[:, sl]
            if j < A_HEADS:
                dy, g, grow = dq_ref[:, sl], gq_ref[...], 0
            else:
                jj = j - A_HEADS
                dy, g, grow = dk_ref[:, jj * HEAD_DIM:(jj + 1) * HEAD_DIM], gk_ref[...], 1
            r = lax.rsqrt(jnp.mean(x * x, axis=-1, keepdims=True) + EPS)
            xn = x * r
            dn = dy * cos_t + _swap_quarters(dy * sin_t)
            dg_ref[grow:grow + 1, :] += jnp.sum(dn * xn, axis=0, keepdims=True)
            dxn = dn * g
            o_ref[:, sl] = (r * (dxn - xn * jnp.mean(dxn * xn, axis=-1, keepdims=True))).astype(BF16)
        o_ref[:, nrm * HEAD_DIM:] = dv_ref[...].astype(BF16)

    row = lambda w: pl.BlockSpec((ts, w), lambda i: (i, 0))
    one = pl.BlockSpec((1, HEAD_DIM), lambda i: (0, 0))
    return pl.pallas_call(
        body, name="prep_a_bwd", grid=(S // ts,),
        in_specs=[row(nq), row(nk), row(nk), row(W), row(HEAD_DIM), row(HEAD_DIM), one, one],
        out_specs=[row(W), pl.BlockSpec((SUBLANES, HEAD_DIM), lambda i: (0, 0))],
        out_shape=[jax.ShapeDtypeStruct((S, W), BF16), jax.ShapeDtypeStruct((SUBLANES, HEAD_DIM), F32)],
        compiler_params=_cp("arbitrary"))(dq, dk, dv, qkv, cos, sin, gq, gk)


def _flash_a_fwd(qkvh, after=()):
    S = qkvh.shape[0]
    grp = A_HEADS // A_KV_HEADS
    tq = _pick(S, (512, 256, 128, 64, 32, 16))
    kc = _pick(S, (512, 256, 128))
    lanes = [slice(b * LANES, (b + 1) * LANES) for b in range(kc // LANES)]
    sub = 2 if tq % 32 == 0 else 1
    ts = tq // sub
    subs = [slice(b * ts, (b + 1) * ts) for b in range(sub)]

    def body(q_ref, k_ref, v_ref, *rest):
        o_ref, lse_ref = rest[-2], rest[-1]
        qs_ = [q_ref[r, :] for r in subs]
        m_t = [jnp.full((ts, LANES), -jnp.inf, F32) for _ in subs]
        for c in range(S // kc):
            k_c = k_ref[c * kc:(c + 1) * kc, :]
            for b in range(sub):
                s = lax.dot_general(qs_[b], k_c, _NT, preferred_element_type=F32)
                for sl in lanes:
                    m_t[b] = jnp.maximum(m_t[b], s[:, sl])
        m = [jnp.max(t, axis=-1, keepdims=True) for t in m_t]
        l_t = [jnp.zeros((ts, LANES), F32) for _ in subs]
        acc = [jnp.zeros((ts, HEAD_DIM), F32) for _ in subs]
        for c in range(S // kc):
            rows = slice(c * kc, (c + 1) * kc)
            k_c, v_c = k_ref[rows, :], v_ref[rows, :]
            for b in range(sub):
                p = jnp.exp2(lax.dot_general(qs_[b], k_c, _NT, preferred_element_type=F32) - m[b])
                for sl in lanes:
                    l_t[b] = l_t[b] + p[:, sl]
                acc[b] = acc[b] + lax.dot_general(p.astype(BF16), v_c, _NN, preferred_element_type=F32)
        for b, r in enumerate(subs):
            l = jnp.sum(l_t[b], axis=-1, keepdims=True)
            o_ref[r, :] = (acc[b] * (1.0 / l)).astype(BF16)
            lse_ref[r, :] = jnp.broadcast_to(m[b] + jnp.log2(l), (ts, HEAD_DIM))

    qs = pl.BlockSpec((tq, HEAD_DIM), lambda h, i: (i, h))
    return pl.pallas_call(
        body, name="flash_a_fwd", grid=(A_HEADS, S // tq),
        in_specs=[qs,
                  pl.BlockSpec((S, HEAD_DIM), lambda h, i: (0, A_HEADS + h // grp)),
                  pl.BlockSpec((S, HEAD_DIM), lambda h, i: (0, A_HEADS + A_KV_HEADS + h // grp))]
        + [pl.BlockSpec(memory_space=pl.ANY)] * len(after),
        out_specs=[qs, qs],
        out_shape=[jax.ShapeDtypeStruct((S, A_HEADS * HEAD_DIM), BF16),
                   jax.ShapeDtypeStruct((S, A_HEADS * HEAD_DIM), F32)],
        compiler_params=_cp("arbitrary", "arbitrary"))(qkvh, qkvh, qkvh, *after)


def _flash_a_bwd(qkvh, do, o, lse):
    S = qkvh.shape[0]
    grp = A_HEADS // A_KV_HEADS
    tq = _pick(S, (1024, 512, 256, 128, 64, 32, 16))
    nq = S // tq
    sub = 4 if tq % 64 == 0 else 1
    ts = tq // sub

    def body(q_ref, k_ref, v_ref, do_ref, o_ref, lse_ref, dq_ref, dk_ref, dv_ref):
        g, i = pl.program_id(1), pl.program_id(2)

        @pl.when((g == 0) & (i == 0))
        def _():
            dk_ref[...] = jnp.zeros_like(dk_ref)
            dv_ref[...] = jnp.zeros_like(dv_ref)
        k = k_ref[...]
        dk = dv = None
        for b in range(sub):
            rows = slice(b * ts, (b + 1) * ts)
            q = q_ref[rows, :]
            do_f = do_ref[rows, :]
            do_b = do_f.astype(BF16)
            delta = jnp.sum(do_f * o_ref[rows, :].astype(F32), axis=-1, keepdims=True)
            p = jnp.exp2(lax.dot_general(q, k, _NT, preferred_element_type=F32) - lse_ref[rows, 0:1])
            dp = lax.dot_general(do_b, v_ref[...], _NT, preferred_element_type=F32)
            ds_b = (p * (dp - delta)).astype(BF16)
            dq_ref[rows, :] = lax.dot_general(ds_b, k, _NN, preferred_element_type=F32) * A_SCALE
            dk_b = lax.dot_general(ds_b, q, _TN, preferred_element_type=F32)
            dv_b = lax.dot_general(p.astype(BF16), do_b, _TN, preferred_element_type=F32)
            dk = dk_b if dk is None else dk + dk_b
            dv = dv_b if dv is None else dv + dv_b
        dk_ref[...] += dk
        dv_ref[...] += dv

        @pl.when((g == grp - 1) & (i == nq - 1))
        def _():
            dk_ref[...] = dk_ref[...] * (A_SCALE / A_QSCALE)

    qs = pl.BlockSpec((tq, HEAD_DIM), lambda kv, g, i: (i, kv * grp + g))
    kvs = lambda off: pl.BlockSpec((S, HEAD_DIM), lambda kv, g, i: (0, off + kv))
    return pl.pallas_call(
        body, name="flash_a_bwd", grid=(A_KV_HEADS, grp, S // tq),
        in_specs=[qs, kvs(A_HEADS), kvs(A_HEADS + A_KV_HEADS), qs, qs, qs],
        out_specs=[qs, kvs(0), kvs(0)],
        out_shape=[jax.ShapeDtypeStruct((S, A_HEADS * HEAD_DIM), F32),
                   jax.ShapeDtypeStruct((S, A_KV_HEADS * HEAD_DIM), F32),
                   jax.ShapeDtypeStruct((S, A_KV_HEADS * HEAD_DIM), F32)],
        compiler_params=_cp("arbitrary", "arbitrary", "arbitrary"))(qkvh, qkvh, qkvh, do, o, lse)


def _bucket_tables(transposed):
    hs = _half_span()
    tq, kv = 2 * hs, 4 * hs
    nb = REL_BUCKETS // 2
    max_exact = nb // 2
    shape = (kv, tq) if transposed else (tq, kv)
    out = np.zeros((len(B_GROUPS), 3) + shape, np.int32)
    win = np.arange(kv) - hs
    blk = np.arange(tq)
    for g, (_, dil) in enumerate(B_GROUPS):
        for case in range(3):
            inside = ((win >= 0) | (case != 0)) & ((win < tq) | (case != 2))
            if transposed:
                rel = blk[None, :] - win[:, None]
                ok = inside[:, None]
            else:
                rel = win[None, :] - blk[:, None]
                ok = inside[None, :]
            r = rel * dil
            n = np.abs(r)
            nf = np.maximum(n, 1).astype(np.float32)
            large = max_exact + (np.log(nf / np.float32(max_exact)) / np.float32(math.log(REL_MAX_DISTANCE / max_exact))
                                 * np.float32(nb - max_exact)).astype(np.int32)
            large = np.minimum(large, nb - 1)
            bucket = np.where(r > 0, nb, 0) + np.where(n < max_exact, n, large)
            out[g, case] = np.where((np.abs(rel) <= hs) & ok, bucket, -1)
    return out


def _bias_build(rel_bias, buckets):
    G, _, tq, kv = buckets.shape
    hg = B_HEADS_PER_GROUP

    def body(rb_ref, bk_ref, o_ref):
        col = pl.program_id(0) * hg + pl.program_id(2)
        bk = bk_ref[...]
        acc = jnp.full((tq, kv), NEG_INF, F32)
        for b in range(REL_BUCKETS):
            acc = jnp.where(bk == b, rb_ref[b, col], acc)
        o_ref[...] = acc

    return pl.pallas_call(
        body, name="bias_build", grid=(G, 3, hg),
        in_specs=[pl.BlockSpec(memory_space=pltpu.SMEM),
                  pl.BlockSpec((None, None, tq, kv), lambda g, c, h: (g, c, 0, 0))],
        out_specs=pl.BlockSpec((None, None, None, tq, kv), lambda g, c, h: (g, c, h, 0, 0)),
        out_shape=jax.ShapeDtypeStruct((G, 3, hg, tq, kv), F32),
        compiler_params=_cp("arbitrary", "arbitrary", "arbitrary"))(rel_bias, buckets)


def _bias_reduce(dbias_list, buckets):
    G, _, tq, kv = buckets.shape
    hg = B_HEADS_PER_GROUP
    n = len(dbias_list)

    def body(*refs):
        bk_ref, o_ref = refs[n], refs[n + 1]
        first = (pl.program_id(0) == 0) & (pl.program_id(1) == 0) & (pl.program_id(2) == 0)

        @pl.when(first)
        def _():
            o_ref[...] = jnp.zeros_like(o_ref)
        col = pl.program_id(0) * hg + pl.program_id(2)
        db = refs[0][...]
        for r in refs[1:n]:
            db = db + r[...]
        bk = bk_ref[...]
        rows = lax.broadcasted_iota(jnp.int32, (REL_BUCKETS, LANES), 0)
        cols = lax.broadcasted_iota(jnp.int32, (REL_BUCKETS, LANES), 1)
        acc = jnp.zeros((REL_BUCKETS, LANES), F32)
        for b in range(REL_BUCKETS):
            val = jnp.sum(jnp.sum(jnp.where(bk == b, db, 0.0), axis=1, keepdims=True), axis=0, keepdims=True)
            acc = acc + jnp.where((rows == b) & (cols == col), val, 0.0)
        o_ref[...] += acc

    tile = pl.BlockSpec((None, None, None, tq, kv), lambda g, c, h: (g, c, h, 0, 0))
    return pl.pallas_call(
        body, name="bias_reduce", grid=(G, 3, hg),
        in_specs=[tile] * n + [pl.BlockSpec((None, None, tq, kv), lambda g, c, h: (g, c, 0, 0))],
        out_specs=pl.BlockSpec((REL_BUCKETS, LANES), lambda g, c, h: (0, 0)),
        out_shape=jax.ShapeDtypeStruct((REL_BUCKETS, LANES), F32),
        compiler_params=_cp("arbitrary", "arbitrary", "arbitrary"))(*dbias_list, buckets)


def _mm_nn_perm(a, w, g):
    S, K = a.shape
    nq = w.shape[3]
    dil = B_GROUPS[g][1]
    wg3 = 3 * B_HEADS_PER_GROUP * HEAD_DIM
    tn = _lane_tile(math.gcd(nq, wg3), 768)
    nps, ntile = nq // tn, wg3 // tn
    tm = _pick(S, (1024, 512, 256))
    rows = tm // dil

    def body(a_ref, w_ref, o_ref, acc_ref):
        acc = lax.dot_general(a_ref[...], w_ref[...], _NN, preferred_element_type=F32)
        if dil == 1:
            o_ref[0] = acc.astype(BF16)
        else:
            for k in range(tn // LANES):
                acc_ref[k] = acc[:, k * LANES:(k + 1) * LANES]
            for c in range(dil):
                for k in range(tn // LANES):
                    o_ref[c, :, k * LANES:(k + 1) * LANES] = acc_ref[k, pl.ds(c, rows, stride=dil), :].astype(BF16)

    def w_map(i, j):
        t = g * ntile + j
        return (0, t // nps, 0, t % nps)

    return pl.pallas_call(
        body, name="b_qkv_g%d" % g, grid=(S // tm, ntile),
        in_specs=[pl.BlockSpec((tm, K), lambda i, j: (i, 0)), pl.BlockSpec((None, None, K, tn), w_map)],
        out_specs=pl.BlockSpec((dil, rows, tn), lambda i, j: (0, i, j)),
        out_shape=jax.ShapeDtypeStruct((dil, S // dil, wg3), BF16),
        scratch_shapes=[pltpu.VMEM((tn // LANES, tm, LANES), F32)],
        compiler_params=_cp("arbitrary", "arbitrary"))(a, w)


def _window_specs(S, wg, col):
    hs = _half_span()
    tq = 2 * hs
    per = tq // hs
    return (pl.BlockSpec((tq, wg), lambda i: (i, col)),
            pl.BlockSpec((hs, wg), lambda i: (jnp.maximum(i * per - 1, 0), col)),
            pl.BlockSpec((hs, wg), lambda i: (jnp.minimum((i + 1) * per, S // hs - 1), col)))


def _window_case(i, L):
    per = L // (2 * _half_span())
    r = i % per
    return jnp.where(r == 0, 0, jnp.where(r == per - 1, 2, 1))


def _window(prev_ref, main_ref, next_ref, sl):
    return jnp.concatenate([prev_ref[:, sl], main_ref[:, sl], next_ref[:, sl]], axis=0)


def _battn_fwd(qkvp, bias, g):
    dil, L, wg3 = qkvp.shape
    S = dil * L
    hs = _half_span()
    tq, kvl = 2 * hs, 4 * hs
    hg = B_HEADS_PER_GROUP
    wg = hg * HEAD_DIM
    scale = HEAD_DIM ** -0.5
    flat = qkvp.reshape(S, wg3)

    def body(q_ref, km, kp, kn, vm, vp, vn, b_ref, o_ref, lz_ref):
        case = _window_case(pl.program_id(0), L)
        for h in range(hg):
            sl = slice(h * HEAD_DIM, (h + 1) * HEAD_DIM)
            s = lax.dot_general(q_ref[:, sl], _window(kp, km, kn, sl), _NT, preferred_element_type=F32) * scale
            s = s + b_ref[case, h]
            m = jnp.max(s, axis=-1, keepdims=True)
            p = jnp.exp(s - m)
            l = jnp.sum(p, axis=-1, keepdims=True)
            o_ref[:, sl] = lax.dot_general(p.astype(BF16), _window(vp, vm, vn, sl), _NN, preferred_element_type=F32) / l
            lz_ref[:, sl] = jnp.broadcast_to(m + jnp.log(l), (tq, HEAD_DIM))

    blk = pl.BlockSpec((tq, wg), lambda i: (i, 0))
    o, lz = pl.pallas_call(
        body, name="battn_fwd_g%d" % g, grid=(S // tq,),
        in_specs=[_window_specs(S, wg, 0)[0], *_window_specs(S, wg, 1), *_window_specs(S, wg, 2),
                  pl.BlockSpec((None, 3, hg, tq, kvl), lambda i: (g, 0, 0, 0, 0))],
        out_specs=[blk, blk], out_shape=[jax.ShapeDtypeStruct((S, wg), F32)] * 2,
        compiler_params=_cp("arbitrary"))(flat, flat, flat, flat, flat, flat, flat, bias)
    return o, lz


def _battn_bwd_dq(qkvp, bias, do, o, lz, dlz, g):
    dil, L, wg3 = qkvp.shape
    S = dil * L
    hs = _half_span()
    tq, kvl = 2 * hs, 4 * hs
    hg = B_HEADS_PER_GROUP
    wg = hg * HEAD_DIM
    scale = HEAD_DIM ** -0.5
    flat = qkvp.reshape(S, wg3)

    def body(q_ref, km, kp, kn, vm, vp, vn, b_ref, do_ref, o_ref, lz_ref, dlz_ref, dq_ref, rt_ref, db_ref):
        i = pl.program_id(0)

        @pl.when(i == 0)
        def _():
            db_ref[...] = jnp.zeros_like(db_ref)
        case = _window_case(i, L)
        for h in range(hg):
            sl = slice(h * HEAD_DIM, (h + 1) * HEAD_DIM)
            kw = _window(kp, km, kn, sl)
            do_f = do_ref[:, sl]
            s = lax.dot_general(q_ref[:, sl], kw, _NT, preferred_element_type=F32) * scale + b_ref[case, h]
            p = jnp.exp(s - lz_ref[:, sl][:, 0:1])
            dp = lax.dot_general(do_f.astype(BF16), _window(vp, vm, vn, sl), _NT, preferred_element_type=F32)
            rt = dlz_ref[:, sl][:, 0:1] - jnp.sum(do_f * o_ref[:, sl], axis=-1, keepdims=True)
            ds = p * (dp + rt)
            db_ref[case, h] += ds
            dq_ref[:, sl] = lax.dot_general((ds * scale).astype(BF16), kw, _NN, preferred_element_type=F32)
            rt_ref[:, sl] = jnp.broadcast_to(rt, (tq, HEAD_DIM))

    blk = pl.BlockSpec((tq, wg), lambda i: (i, 0))
    row = jax.ShapeDtypeStruct((S, wg), F32)
    return pl.pallas_call(
        body, name="battn_bwd_dq_g%d" % g, grid=(S // tq,),
        in_specs=[_window_specs(S, wg, 0)[0], *_window_specs(S, wg, 1), *_window_specs(S, wg, 2),
                  pl.BlockSpec((None, 3, hg, tq, kvl), lambda i: (g, 0, 0, 0, 0)), blk, blk, blk, blk],
        out_specs=[blk, blk, pl.BlockSpec((3, hg, tq, kvl), lambda i: (0, 0, 0, 0))],
        out_shape=[row, row, jax.ShapeDtypeStruct((3, hg, tq, kvl), F32)],
        compiler_params=_cp("arbitrary"))(flat, flat, flat, flat, flat, flat, flat, bias, do, o, lz, dlz)


def _battn_bwd_dkv(qkvp, bias_t, do, lz, rt, g):
    dil, L, wg3 = qkvp.shape
    S = dil * L
    hs = _half_span()
    tq, kvl = 2 * hs, 4 * hs
    hg = B_HEADS_PER_GROUP
    wg = hg * HEAD_DIM
    scale = HEAD_DIM ** -0.5
    flat = qkvp.reshape(S, wg3)

    def body(k_ref, v_ref, qm, qp, qn, dom, dop, don, lzm, lzp, lzn, rtm, rtp, rtn, b_ref, dk_ref, dv_ref):
        case = _window_case(pl.program_id(0), L)
        for h in range(hg):
            sl = slice(h * HEAD_DIM, (h + 1) * HEAD_DIM)
            qw = _window(qp, qm, qn, sl)
            dow = _window(dop, dom, don, sl).astype(BF16)
            s = lax.dot_general(qw, k_ref[:, sl], _NT, preferred_element_type=F32) * scale + b_ref[case, h]
            p = jnp.exp(s - _window(lzp, lzm, lzn, sl)[:, 0:1])
            dp = lax.dot_general(dow, v_ref[:, sl], _NT, preferred_element_type=F32)
            ds_b = (p * (dp + _window(rtp, rtm, rtn, sl)[:, 0:1]) * scale).astype(BF16)
            dk_ref[:, sl] = lax.dot_general(ds_b, qw, _TN, preferred_element_type=F32)
            dv_ref[:, sl] = lax.dot_general(p.astype(BF16), dow, _TN, preferred_element_type=F32)

    blk = pl.BlockSpec((tq, wg), lambda i: (i, 0))
    row = jax.ShapeDtypeStruct((S, wg), F32)
    return pl.pallas_call(
        body, name="battn_bwd_dkv_g%d" % g, grid=(S // tq,),
        in_specs=[_window_specs(S, wg, 1)[0], _window_specs(S, wg, 2)[0], *_window_specs(S, wg, 0),
                  *_window_specs(S, wg, 0), *_window_specs(S, wg, 0), *_window_specs(S, wg, 0),
                  pl.BlockSpec((None, 3, hg, kvl, tq), lambda i: (g, 0, 0, 0, 0))],
        out_specs=[blk, blk], out_shape=[row, row],
        compiler_params=_cp("arbitrary"))(flat, flat, flat, flat, flat, do, do, do, lz, lz, lz, rt, rt, rt, bias_t)


def _group_weights(lz_refs, h):
    z = [r[h] for r in lz_refs]
    mx = functools.reduce(jnp.maximum, z)
    e = [jnp.exp(v - mx) for v in z]
    inv = 1.0 / functools.reduce(lambda a, b: a + b, e)
    return [v * inv for v in e]


def _to_token_order(src_ref, dst_ref, dil):
    rows = src_ref.shape[1]
    for k in range(dst_ref.shape[0]):
        sl = slice(k * LANES, (k + 1) * LANES)
        if dil == 1:
            dst_ref[k] = src_ref[0, :, sl]
        else:
            for c in range(dil):
                dst_ref[k, pl.ds(c, rows, stride=dil), :] = src_ref[c, :, sl]


def _to_subsequence_order(src_ref, dst_ref, dil):
    rows = dst_ref.shape[1]
    for k in range(src_ref.shape[0]):
        sl = slice(k * LANES, (k + 1) * LANES)
        if dil == 1:
            dst_ref[0, :, sl] = src_ref[k]
        else:
            for c in range(dil):
                dst_ref[c, :, sl] = src_ref[k, pl.ds(c, rows, stride=dil), :]


def _sub_view(a, dil):
    S, w = a.shape
    return a.reshape(dil, S // dil, w)


def _sub_spec(dil, ts, w):
    return pl.BlockSpec((dil, ts // dil, w), lambda i: (0, i, 0))


def _combine_fwd(os_, lzs):
    G = len(os_)
    S, Wg = os_[0].shape
    hg = B_HEADS_PER_GROUP
    dils = [d for _, d in B_GROUPS]
    ts = _pick(S, (256, 128))

    def body(*refs):
        o_in, lz_in, y_ref = refs[:G], refs[G:2 * G], refs[2 * G]
        o_nat, lz_nat = refs[2 * G + 1:3 * G + 1], refs[3 * G + 1:4 * G + 1]
        for g in range(G):
            _to_token_order(o_in[g], o_nat[g], dils[g])
            _to_token_order(lz_in[g], lz_nat[g], dils[g])
        for h in range(hg):
            w = _group_weights(lz_nat, h)
            for g in range(G):
                y_ref[:, (g * hg + h) * HEAD_DIM:(g * hg + h + 1) * HEAD_DIM] = (w[g] * o_nat[g][h]).astype(BF16)

    specs = [_sub_spec(d, ts, Wg) for d in dils]
    return pl.pallas_call(
        body, name="combine_fwd", grid=(S // ts,), in_specs=specs + specs,
        out_specs=pl.BlockSpec((ts, G * Wg), lambda i: (i, 0)),
        out_shape=jax.ShapeDtypeStruct((S, G * Wg), BF16),
        scratch_shapes=[pltpu.VMEM((hg, ts, HEAD_DIM), F32)] * (2 * G),
        compiler_params=_cp("arbitrary"))(*[_sub_view(a, d) for a, d in zip(os_, dils)],
                                          *[_sub_view(a, d) for a, d in zip(lzs, dils)])


def _combine_bwd(dy, os_, lzs):
    G = len(os_)
    S, Wg = os_[0].shape
    hg = B_HEADS_PER_GROUP
    dils = [d for _, d in B_GROUPS]
    ts = _pick(S, (128,))

    def body(*refs):
        dy_ref, o_in, lz_in = refs[0], refs[1:1 + G], refs[1 + G:1 + 2 * G]
        do_out, dlz_out = refs[1 + 2 * G:1 + 3 * G], refs[1 + 3 * G:1 + 4 * G]
        scr = refs[1 + 4 * G:]
        o_nat, lz_nat, do_nat, dlz_nat = scr[:G], scr[G:2 * G], scr[2 * G:3 * G], scr[3 * G:4 * G]
        for g in range(G):
            _to_token_order(o_in[g], o_nat[g], dils[g])
            _to_token_order(lz_in[g], lz_nat[g], dils[g])
        for h in range(hg):
            w = _group_weights(lz_nat, h)
            dw = []
            for g in range(G):
                dyg = dy_ref[:, (g * hg + h) * HEAD_DIM:(g * hg + h + 1) * HEAD_DIM]
                dw.append(jnp.sum(dyg * o_nat[g][h], axis=-1, keepdims=True))
                do_nat[g][h] = w[g] * dyg
            tot = functools.reduce(lambda a, b: a + b, [w[g] * dw[g] for g in range(G)])
            for g in range(G):
                dlz_nat[g][h] = w[g] * (dw[g] - tot)
        for g in range(G):
            _to_subsequence_order(do_nat[g], do_out[g], dils[g])
            _to_subsequence_order(dlz_nat[g], dlz_out[g], dils[g])

    specs = [_sub_spec(d, ts, Wg) for d in dils]
    outs = pl.pallas_call(
        body, name="combine_bwd", grid=(S // ts,),
        in_specs=[pl.BlockSpec((ts, G * Wg), lambda i: (i, 0))] + specs + specs,
        out_specs=specs + specs,
        out_shape=[jax.ShapeDtypeStruct((d, S // d, Wg), F32) for d in dils] * 2,
        scratch_shapes=[pltpu.VMEM((hg, ts, HEAD_DIM), F32)] * (4 * G),
        compiler_params=_cp("arbitrary"))(dy, *[_sub_view(a, d) for a, d in zip(os_, dils)],
                                          *[_sub_view(a, d) for a, d in zip(lzs, dils)])
    flat = [a.reshape(S, Wg) for a in outs]
    return flat[:G], flat[G:]


def _concat_cast(parts, dils):
    S = parts[0].shape[0]
    widths = [p.shape[1] for p in parts]
    n = len(parts)
    ts = _pick(S, (256, 128))

    def body(*refs):
        o_ref, nat = refs[n], refs[n + 1]
        off = 0
        for r, w, d in zip(refs, widths, dils):
            _to_token_order(r, nat, d)
            for k in range(w // LANES):
                o_ref[:, off + k * LANES:off + (k + 1) * LANES] = nat[k].astype(BF16)
            off += w

    assert len(set(widths)) == 1
    return pl.pallas_call(
        body, name="concat_cast", grid=(S // ts,),
        in_specs=[_sub_spec(d, ts, w) for w, d in zip(widths, dils)],
        out_specs=pl.BlockSpec((ts, sum(widths)), lambda i: (i, 0)),
        out_shape=jax.ShapeDtypeStruct((S, sum(widths)), BF16),
        scratch_shapes=[pltpu.VMEM((widths[0] // LANES, ts, LANES), F32)],
        compiler_params=_cp("arbitrary"))(*[_sub_view(p, d) for p, d in zip(parts, dils)])


def _ffn_specs(S, dff, cq, ts, tc, layer, order):
    nfc = dff // tc
    nps = cq // tc
    hb = ts // SUBLANES
    nrow8 = S // SUBLANES

    def u_main(half):
        return pl.BlockSpec((ts, tc), lambda *g: (order(*g)[0], order(*g)[1] % nfc + half * nfc))

    def u_prev(half):
        return pl.BlockSpec((SUBLANES, tc), lambda *g: (jnp.maximum(order(*g)[0] * hb - 1, 0),
                                                         order(*g)[1] % nfc + half * nfc))

    def u_next(half):
        return pl.BlockSpec((SUBLANES, tc), lambda *g: (jnp.minimum((order(*g)[0] + 1) * hb, nrow8 - 1),
                                                         order(*g)[1] % nfc + half * nfc))

    def cw(half):
        def im(*g):
            jj = order(*g)[1] % nfc + half * nfc
            return (layer, jj // nps, 0, jj % nps)
        return pl.BlockSpec((None, None, 3, tc), im)

    def cb(half):
        return pl.BlockSpec((None, 1, tc), lambda *g: (layer, 0, order(*g)[1] % nfc + half * nfc))

    return nfc, u_main, u_prev, u_next, cw, cb


def _ffn_act_fwd(u, cw_full, cb3, layer):
    S, two_dff = u.shape
    dff = two_dff // 2
    cq = cw_full.shape[3]
    ts = _pick(S, (1024, 512, 256, 128, 64, 32, 16))
    tc = _pick(cq, (256, 128))
    order = lambda j, i: (i, j)
    nfc, u_main, u_prev, u_next, cw, cb = _ffn_specs(S, dff, cq, ts, tc, layer, order)
    nrow = S // ts

    def body(ug, ugp, ugn, uv, uvp, uvn, wg, wv, bg, bv, a_ref):
        i = pl.program_id(1)
        row = lax.broadcasted_iota(jnp.int32, (ts, tc), 0)

        def conv(x_ref, p_ref, n_ref, w_ref, b_ref):
            x = x_ref[...]
            prev = jnp.where(i > 0, p_ref[SUBLANES - 1:SUBLANES, :], 0.0)
            nxt = jnp.where(i < nrow - 1, n_ref[0:1, :], 0.0)
            xm = jnp.where(row == 0, prev, pltpu.roll(x, 1, 0))
            xp = jnp.where(row == ts - 1, nxt, pltpu.roll(x, ts - 1, 0))
            return w_ref[0:1, :] * xm + w_ref[1:2, :] * x + w_ref[2:3, :] * xp + b_ref[...]

        gc = conv(ug, ugp, ugn, wg, bg)
        vc = conv(uv, uvp, uvn, wv, bv)
        a_ref[...] = (gc * (1.0 / (1.0 + jnp.exp(-gc))) * vc).astype(BF16)

    return pl.pallas_call(
        body, name="ffn_act_fwd", grid=(nfc, nrow),
        in_specs=[u_main(0), u_prev(0), u_next(0), u_main(1), u_prev(1), u_next(1), cw(0), cw(1), cb(0), cb(1)],
        out_specs=pl.BlockSpec((ts, tc), lambda j, i: (i, j)),
        out_shape=jax.ShapeDtypeStruct((S, dff), BF16),
        compiler_params=_cp("arbitrary", "arbitrary"))(u, u, u, u, u, u, cw_full, cw_full, cb3, cb3)


def _ffn_act_bwd(u, da, cw_full, cb3, layer):
    S, two_dff = u.shape
    dff = two_dff // 2
    cq = cw_full.shape[3]
    ts = _pick(S, (1024, 512, 256, 128, 64, 32, 16))
    tc = _pick(cq, (256, 128))
    order = lambda j, i: (i, j)
    nfc, u_main, u_prev, u_next, cw, cb = _ffn_specs(S, dff, cq, ts, tc, layer, order)
    nrow = S // ts
    hb = ts // SUBLANES
    te = ts + 2 * SUBLANES
    da_main = pl.BlockSpec((ts, tc), lambda j, i: (i, j))
    da_prev = pl.BlockSpec((SUBLANES, tc), lambda j, i: (jnp.maximum(i * hb - 1, 0), j))
    da_next = pl.BlockSpec((SUBLANES, tc), lambda j, i: (jnp.minimum((i + 1) * hb, S // SUBLANES - 1), j))
    main = slice(SUBLANES, SUBLANES + ts)

    def body(ug, ugp, ugn, uv, uvp, uvn, dam, dap, dan, wg, wv, bg, bv, dug_ref, duv_ref, accg_ref, accv_ref):
        i = pl.program_id(1)

        @pl.when(i == 0)
        def _():
            accg_ref[...] = jnp.zeros_like(accg_ref)
            accv_ref[...] = jnp.zeros_like(accv_ref)

        def ext(m, p, n):
            return jnp.concatenate([jnp.where(i > 0, p[...], 0.0), m[...], jnp.where(i < nrow - 1, n[...], 0.0)], axis=0)

        def shift(x):
            return pltpu.roll(x, 1, 0), pltpu.roll(x, te - 1, 0)

        xg, xv, dae = ext(ug, ugp, ugn), ext(uv, uvp, uvn), ext(dam, dap, dan)
        xgm, xgp = shift(xg)
        xvm, xvp = shift(xv)
        gc = wg[0:1, :] * xgm + wg[1:2, :] * xg + wg[2:3, :] * xgp + bg[...]
        vc = wv[0:1, :] * xvm + wv[1:2, :] * xv + wv[2:3, :] * xvp + bv[...]
        sig = 1.0 / (1.0 + jnp.exp(-gc))
        silu = gc * sig
        dcg = dae * vc * (sig * (1.0 + gc * (1.0 - sig)))
        dcv = dae * silu

        def finish(dc, x, xm, xp, w_ref, du_ref, acc_ref):
            dm, dp = shift(dc)
            du = w_ref[0:1, :] * dp + w_ref[1:2, :] * dc + w_ref[2:3, :] * dm
            du_ref[...] = du[main, :].astype(BF16)
            dcm = dc[main, :]
            acc_ref[0:1, :] += jnp.sum(dcm * xm[main, :], axis=0, keepdims=True)
            acc_ref[1:2, :] += jnp.sum(dcm * x[main, :], axis=0, keepdims=True)
            acc_ref[2:3, :] += jnp.sum(dcm * xp[main, :], axis=0, keepdims=True)
            acc_ref[3:4, :] += jnp.sum(dcm, axis=0, keepdims=True)

        finish(dcg, xg, xgm, xgp, wg, dug_ref, accg_ref)
        finish(dcv, xv, xvm, xvp, wv, duv_ref, accv_ref)

    blk = pl.BlockSpec((ts, tc), lambda j, i: (i, j))
    acc = pl.BlockSpec((SUBLANES, tc), lambda j, i: (0, j))
    dug, duv, accg, accv = pl.pallas_call(
        body, name="ffn_act_bwd", grid=(nfc, nrow),
        in_specs=[u_main(0), u_prev(0), u_next(0), u_main(1), u_prev(1), u_next(1), da_main, da_prev, da_next,
                  cw(0), cw(1), cb(0), cb(1)],
        out_specs=[blk, blk, acc, acc],
        out_shape=[jax.ShapeDtypeStruct((S, dff), BF16)] * 2 + [jax.ShapeDtypeStruct((SUBLANES, dff), F32)] * 2,
        compiler_params=_cp("arbitrary", "arbitrary"))(u, u, u, u, u, u, da, da, da, cw_full, cw_full, cb3, cb3)
    return (dug, duv), jnp.concatenate([accg, accv], axis=1)


def _my_chip():
    return 2 * lax.axis_index("x") + lax.axis_index("y")


def _into_full(w, layer, dtype):
    L, a, b = w.shape
    tr = _pick(a, (512, 256, 128, 64, 32, 16, 8))

    def body(w_ref, o_ref):
        o_ref[...] = w_ref[...].astype(dtype)

    return pl.pallas_call(
        body, name="into_full", grid=(a // tr,),
        in_specs=[pl.BlockSpec((None, tr, b), lambda i: (layer, i, 0))],
        out_specs=pl.BlockSpec((None, None, tr, b), lambda i: (0, _my_chip(), i, 0)),
        out_shape=jax.ShapeDtypeStruct((1, N_CHIPS, a, b), dtype),
        compiler_params=_cp("arbitrary"))(w)


def _adam_math(w, g, m, v):
    m = ADAM_B1 * m + (1.0 - ADAM_B1) * g
    v = ADAM_B2 * v + (1.0 - ADAM_B2) * (g * g)
    m_hat = m / (1.0 - ADAM_B1 ** ADAM_STEP)
    v_hat = v / (1.0 - ADAM_B2 ** ADAM_STEP)
    delta = -ADAM_LR * (m_hat / (jnp.sqrt(v_hat) + ADAM_EPS) + ADAM_WD * w)
    return delta, m, v


def _adamw(w, g, m, v, keep_g=False):
    R, C = w.shape
    tr = _pick(R, (128, 64, 32, 16, 8)) if R % SUBLANES == 0 and C % LANES == 0 else R
    n_out = 4 if keep_g else 3

    def body(w_ref, g_ref, m_ref, v_ref, d_ref, nm_ref, nv_ref, *rest):
        g = g_ref[...]
        d, nm, nv = _adam_math(w_ref[...], g, m_ref[...], v_ref[...])
        d_ref[...] = d
        nm_ref[...] = nm
        nv_ref[...] = nv
        if keep_g:
            rest[0][...] = g

    spec = pl.BlockSpec((tr, C), lambda i: (i, 0))
    return pl.pallas_call(
        body, name="adamw", grid=(R // tr,), in_specs=[spec] * 4, out_specs=[spec] * n_out,
        out_shape=[jax.ShapeDtypeStruct((R, C), F32)] * n_out, compiler_params=_cp("arbitrary"))(w, g, m, v)


ANY = pl.BlockSpec(memory_space=pl.ANY)


def _position():
    x, y, c = lax.axis_index("x"), lax.axis_index("y"), lax.axis_index("c")
    chips = [(1 - x, y), (x, 1 - y), (1 - x, 1 - y)]
    return x, y, c, chips


HBM = pl.BlockSpec(memory_space=pltpu.HBM)
SEM = pl.BlockSpec(memory_space=pltpu.SEMAPHORE)
EFFECT = pltpu.SideEffectType.DATAFLOW_SIDE_EFFECTING


def _in_hbm(a):
    return pltpu.with_memory_space_constraint(a, pltpu.HBM)


def _shard_half(buf, shape, chip, half):
    _, _, a, b = shape
    p = 2 * chip[0] + chip[1]
    if a % (4 * SUBLANES) == 0:
        return buf.at[0, p, pl.ds(half * (a // 2), a // 2)]
    return buf.at[0, p, :, pl.ds(half * (b // 2), b // 2)]


def _gather_copy(buf, shape, chip, half, to, send, recv, k):
    part = _shard_half(buf, shape, chip, half)
    return pltpu.make_async_remote_copy(src_ref=part, dst_ref=part, send_sem=send.at[k], recv_sem=recv.at[k],
                                        device_id=to, device_id_type=MESH)


def _gather_hop(buf, shape, hop, j, incoming, send, recv, k):
    x, y, c, chips = _position()
    if hop == "chips":
        chip, half, to = (chips[j] if incoming else (x, y)), c, (*chips[j], c)
    else:
        chip, half, to = chips[j], (1 - c if incoming else c), (x, y, 1 - c)
    return _gather_copy(buf, shape, chip, half, to, send, recv, k)


def _gather_start(fulls, name, hop, after=()):
    n = len(fulls)
    na = len(after)

    def body(*refs):
        send, recv = refs[n + na], refs[n + na + 1]
        buf, token = refs[n + na + 2:2 * n + na + 2], refs[2 * n + na + 2]
        for t in range(n):
            for j in range(3):
                _gather_hop(buf[t], fulls[t].shape, hop, j, False, send, recv, 3 * t + j).start()
        token[...] = jnp.zeros_like(token)

    outs = pl.pallas_call(
        body, name=name, in_specs=[HBM] * n + [ANY] * na,
        out_specs=[SEM, SEM] + [HBM] * n + [pl.BlockSpec(memory_space=pltpu.VMEM)],
        out_shape=[pltpu.SemaphoreType.DMA((3 * n,)), pltpu.SemaphoreType.DMA((3 * n,))]
        + [pltpu.HBM(f.shape, f.dtype) for f in fulls] + [jax.ShapeDtypeStruct((SUBLANES, LANES), F32)],
        input_output_aliases={t: 2 + t for t in range(n)},
        compiler_params=pltpu.CompilerParams(has_side_effects=EFFECT))(*[_in_hbm(f) for f in fulls], *after)
    return outs[0], outs[1], list(outs[2:2 + n]), outs[2 + n]


def _gather_wait(send, recv, fulls, after, name, hop):
    n = len(fulls)

    def body(*refs):
        buf, send_ref, recv_ref = refs[:n], refs[n], refs[n + 1]
        for t in range(n):
            for j in range(3):
                _gather_hop(buf[t], fulls[t].shape, hop, j, False, send_ref, recv_ref, 3 * t + j).wait_send()
                _gather_hop(buf[t], fulls[t].shape, hop, j, True, send_ref, recv_ref, 3 * t + j).wait_recv()

    outs = pl.pallas_call(
        body, name=name, in_specs=[HBM] * n + [SEM, SEM] + [ANY] * len(after), out_specs=[HBM] * n,
        out_shape=[pltpu.HBM(f.shape, f.dtype) for f in fulls],
        input_output_aliases={t: t for t in range(n)},
        compiler_params=pltpu.CompilerParams(has_side_effects=EFFECT))(*fulls, send, recv, *after)
    return list(outs)


def _allreduce_small(part):
    M, C = part.shape
    n_dev = 2 * N_CHIPS

    def body(x_ref, sum_ref, all_ref, send, recv, local):
        x, y, c, chips = _position()
        me, sib = (x, y, c), (x, y, 1 - c)

        def rows(px, py, pc):
            return all_ref.at[pl.ds((4 * px + 2 * py + pc) * M, M), :]

        def copy(k, block, to, src=None):
            return pltpu.make_async_remote_copy(
                src_ref=rows(*block) if src is None else src, dst_ref=rows(*block),
                send_sem=send.at[k], recv_sem=recv.at[k], device_id=to, device_id_type=MESH)

        mine = pltpu.make_async_copy(x_ref, rows(*me), local)
        mine.start()
        first = [copy(0, me, sib, src=x_ref)] + [copy(1 + j, me, (*chip, c), src=x_ref) for j, chip in enumerate(chips)]
        for cp in first:
            cp.start()
        passed = [copy(4 + j, (*chip, c), sib) for j, chip in enumerate(chips)]
        for j, chip in enumerate(chips):
            copy(1 + j, (*chip, c), me).wait_recv()
            passed[j].start()
        copy(0, sib, me).wait_recv()
        for j, chip in enumerate(chips):
            copy(4 + j, (*chip, 1 - c), me).wait_recv()
        for cp in first + passed:
            cp.wait_send()
        mine.wait()
        acc = all_ref[0:M, :]
        for d in range(1, n_dev):
            acc = acc + all_ref[d * M:(d + 1) * M, :]
        sum_ref[...] = acc

    vm = pl.BlockSpec(memory_space=pltpu.VMEM)
    return pl.pallas_call(
        body, name="allreduce_small", in_specs=[vm], out_specs=[vm],
        out_shape=[jax.ShapeDtypeStruct((M, C), F32)],
        scratch_shapes=[pltpu.VMEM((n_dev * M, C), F32), pltpu.SemaphoreType.DMA((7,)),
                        pltpu.SemaphoreType.DMA((7,)), pltpu.SemaphoreType.DMA],
        compiler_params=pltpu.CompilerParams(vmem_limit_bytes=VMEM_LIMIT))(part)[0]


N_PEERS = 2 * N_CHIPS - 1


def _peers():
    x, y, c, chips = _position()
    return [(x, y, 1 - c)] + [(*ch, c) for ch in chips] + [(*ch, 1 - c) for ch in chips]


def _reduce_copy(src, dst, peers, send, recv, t, r):
    px, py, pc = peers[r]
    return pltpu.make_async_remote_copy(
        src_ref=src.at[2 * px + py, pc], dst_ref=dst.at[r], send_sem=send.at[N_PEERS * t + r],
        recv_sem=recv.at[N_PEERS * t + r], device_id=peers[r], device_id_type=MESH)


def _reduce_start(grads, name, after=()):
    n = len(grads)
    na = len(after)
    lands = [lax.empty((N_PEERS,) + g.shape[2:], BF16) for g in grads]

    def body(*refs):
        send, recv = refs[2 * n + na], refs[2 * n + na + 1]
        src, dst = refs[2 * n + na + 2:3 * n + na + 2], refs[3 * n + na + 2:4 * n + na + 2]
        token = refs[4 * n + na + 2]
        peers = _peers()
        for t in range(n):
            for r in range(N_PEERS):
                _reduce_copy(src[t], dst[t], peers, send, recv, t, r).start()
        token[...] = jnp.zeros_like(token)

    outs = pl.pallas_call(
        body, name=name, in_specs=[HBM] * (2 * n) + [ANY] * na,
        out_specs=[SEM, SEM] + [HBM] * (2 * n) + [pl.BlockSpec(memory_space=pltpu.VMEM)],
        out_shape=[pltpu.SemaphoreType.DMA((N_PEERS * n,)), pltpu.SemaphoreType.DMA((N_PEERS * n,))]
        + [pltpu.HBM(a.shape, a.dtype) for a in grads + lands] + [jax.ShapeDtypeStruct((SUBLANES, LANES), F32)],
        input_output_aliases={t: 2 + t for t in range(2 * n)},
        compiler_params=pltpu.CompilerParams(has_side_effects=EFFECT))(*[_in_hbm(a) for a in grads + lands], *after)
    return outs[0], outs[1], list(outs[2:2 + n]), list(outs[2 + n:2 + 2 * n]), outs[2 + 2 * n]


def _reduce_wait(send, recv, grads, lands, after, name):
    n = len(grads)

    def body(*refs):
        src, dst, send_ref, recv_ref = refs[:n], refs[n:2 * n], refs[2 * n], refs[2 * n + 1]
        peers = _peers()
        for t in range(n):
            for r in range(N_PEERS):
                cp = _reduce_copy(src[t], dst[t], peers, send_ref, recv_ref, t, r)
                cp.wait_send()
                cp.wait_recv()

    outs = pl.pallas_call(
        body, name=name, in_specs=[HBM] * (2 * n) + [SEM, SEM] + [ANY] * len(after), out_specs=[HBM] * (2 * n),
        out_shape=[pltpu.HBM(a.shape, a.dtype) for a in grads + lands],
        input_output_aliases={t: t for t in range(2 * n)},
        compiler_params=pltpu.CompilerParams(has_side_effects=EFFECT))(*grads, *lands, send, recv, *after)
    return list(outs[:n]), list(outs[n:])


def _add_pieces(grad, land, stack, layer):
    _, _, R, C = grad.shape
    tr = _pick(R, (256, 128, 64, 32, 16))

    def body(g_ref, r_ref, stack_ref, o_ref):
        acc = g_ref[...].astype(F32)
        for r in range(N_PEERS):
            acc = acc + r_ref[r].astype(F32)
        o_ref[...] = acc

    return pl.pallas_call(
        body, name="add_pieces", grid=(R // tr,),
        in_specs=[pl.BlockSpec((None, None, tr, C), lambda i: (_my_chip(), lax.axis_index("c"), i, 0)),
                  pl.BlockSpec((N_PEERS, tr, C), lambda i: (0, i, 0)),
                  ANY],
        out_specs=pl.BlockSpec((None, None, tr, C), lambda i: (layer, lax.axis_index("c"), i, 0)),
        out_shape=jax.ShapeDtypeStruct(stack.shape, F32), input_output_aliases={2: 0},
        compiler_params=_cp("arbitrary"))(grad, land, stack)


def _ag_sibling(stacks):
    n = len(stacks)
    offs = np.cumsum([0] + [s.shape[0] for s in stacks])

    def body(*refs):
        buf, send, recv = refs[n:2 * n], refs[2 * n], refs[2 * n + 1]
        x, y, c, _ = _position()

        def copy(t, l, half):
            part = buf[t].at[l, half]
            return pltpu.make_async_remote_copy(
                src_ref=part, dst_ref=part, send_sem=send.at[int(offs[t]) + l], recv_sem=recv.at[int(offs[t]) + l],
                device_id=(x, y, 1 - c), device_id_type=MESH)

        cps = [copy(t, l, c) for t in range(n) for l in range(stacks[t].shape[0])]
        for cp in cps:
            cp.start()
        for t in range(n):
            for l in range(stacks[t].shape[0]):
                copy(t, l, 1 - c).wait_recv()
        for cp in cps:
            cp.wait_send()

    return pl.pallas_call(
        body, name="ag_sibling", in_specs=[ANY] * n, out_specs=[ANY] * n,
        out_shape=[jax.ShapeDtypeStruct(s.shape, F32) for s in stacks],
        input_output_aliases={t: t for t in range(n)},
        scratch_shapes=[pltpu.SemaphoreType.DMA((int(offs[-1]),)), pltpu.SemaphoreType.DMA((int(offs[-1]),))])(*stacks)


def _split8(dw, blocked):
    if blocked:
        p, k, nq = dw.shape
        return dw.reshape(p, 2, k // 2, nq)
    k, n = dw.shape
    return dw.reshape(N_CHIPS, 2, k // (2 * N_CHIPS), n)


def kernel(x, a_w_qkv, a_w_o, a_q_gain, a_k_gain, b_w_qkv, b_w_o, rel_bias, mix_norm, ffn_norm, w_up, conv_w, conv_b, w_down, final_norm, loss_target, m_a_w_qkv, m_a_w_o, m_a_q_gain, m_a_k_gain, m_b_w_qkv, m_b_w_o, m_rel_bias, m_mix_norm, m_ffn_norm, m_w_up, m_conv_w, m_conv_b, m_w_down, m_final_norm, v_a_w_qkv, v_a_w_o, v_a_q_gain, v_a_k_gain, v_b_w_qkv, v_b_w_o, v_rel_bias, v_mix_norm, v_ffn_norm, v_w_up, v_conv_w, v_conv_b, v_w_down, v_final_norm):
    S, D = x.shape[1], x.shape[2]
    h = x.reshape(S, D)
    target = loss_target.reshape(S, D)
    hg = B_HEADS_PER_GROUP
    G = len(B_GROUPS)
    n_a, n_b = a_w_qkv.shape[0], b_w_qkv.shape[0]
    depth = w_up.shape[0]
    cx, cy = lax.axis_index("x"), lax.axis_index("y")

    big = dict(a_w_qkv=a_w_qkv, a_w_o=a_w_o, b_w_qkv=b_w_qkv, b_w_o=b_w_o, w_up=w_up, w_down=w_down)
    blocked = dict(a_w_qkv=True, a_w_o=False, b_w_qkv=True, b_w_o=False, w_up=True, w_down=False)
    names = list(big)
    srcs = dict(big, conv_w=conv_w)
    started = []
    for i in range(depth):
        mix = [("a_w_qkv", i // 2), ("a_w_o", i // 2)] if i % 2 == 0 else [("b_w_qkv", i // 2), ("b_w_o", i // 2)]
        rest = [("w_up", i), ("conv_w", i), ("w_down", i)]
        stages = [mix[:1], mix[1:], rest] if i == 0 else [mix + rest]
        started.append([])
        for s, keys in enumerate(stages):
            bufs = [_into_full(srcs[k], l, F32 if k == "conv_w" else BF16) for k, l in keys]
            started[i].append((keys,) + _gather_start(bufs, "gather_start_%d_%d" % (i, s), "chips"))
    cb3 = conv_b.reshape(depth, 1, conv_b.shape[1])

    cos, sin = _rope_tables(S)
    buckets = jnp.asarray(_bucket_tables(False))
    bias = _bias_build(rel_bias, buckets)
    bias_t = _bias_build(rel_bias, jnp.asarray(_bucket_tables(True)))

    saved = []
    passing = {}

    def land(i, s, after):
        keys, send, recv, bufs, _ = started[i][s]
        bufs = _gather_wait(send, recv, bufs, after, "gather_wait_%d_%d" % (i, s), "chips")
        send, recv, bufs, token = _gather_start(bufs, "pass_start_%d_%d" % (i, s), "sibling")
        passing[i, s] = (keys, send, recv, bufs)
        return token

    def arrive(i, s, after, wl):
        keys, send, recv, bufs = passing.pop((i, s))
        bufs = _gather_wait(send, recv, bufs, after, "pass_wait_%d_%d" % (i, s), "sibling")
        for (k, _), buf in zip(keys, bufs):
            _, _, a, b = buf.shape
            wl[k] = buf if k == "conv_w" or blocked[k] else buf.reshape(1, N_CHIPS * a, b)

    first = [land(0, 0, [h])]
    for i in range(depth):
        j = i // 2
        wl = {}
        arrive(i, 0, [h], wl)
        sv = dict(h0=h, w=wl)
        hn = _rms_fwd(h, mix_norm[i:i + 1], after=[st[4] for layer in started for st in layer] + first if i == 0 else ())
        sv["hn"] = hn
        if i % 2 == 0:
            qkv = _mm_nn(hn, wl["a_w_qkv"], 0, blocked=True, name="a_qkv")
            qkvh = _prep_a_fwd(qkv, cos, sin, a_q_gain[j:j + 1], a_k_gain[j:j + 1])
            staged = len(started[i]) > 1
            o, lse = _flash_a_fwd(qkvh, after=[land(i, 1, [qkvh])] if staged else ())
            tok = ()
            if staged:
                arrive(i, 1, [o], wl)
                tok = [land(i, 2, [o])]
            sv.update(qkv=qkv, qkvh=qkvh, o=o, lse=lse)
            h = _mm_nn(o, wl["a_w_o"], 0, blocked=False, res=h, name="a_out", after=tok)
        else:
            qkvp = [_mm_nn_perm(hn, wl["b_w_qkv"], g) for g in range(G)]
            os_, lzs = [], []
            for g in range(G):
                o_g, lz_g = _battn_fwd(qkvp[g], bias, g)
                os_.append(o_g)
                lzs.append(lz_g)
            y = _combine_fwd(os_, lzs)
            sv.update(qkvp=qkvp, os=os_, lzs=lzs, y=y)
            h = _mm_nn(y, wl["b_w_o"], 0, blocked=False, res=h, name="b_out")
        sv["h1"] = h
        hf = _rms_fwd(h, ffn_norm[i:i + 1])
        if len(started[i]) > 2:
            arrive(i, 2, [hf], wl)
        u = _mm_nn(hf, wl["w_up"], 0, blocked=True, name="ffn_up")
        act = _ffn_act_fwd(u, wl["conv_w"], cb3[i:i + 1], 0)
        sv.update(hf=hf, u=u, act=act)
        nxt = [land(i + 1, 0, [act])] if i + 1 < depth else ()
        h = _mm_nn(act, wl["w_down"], 0, blocked=False, res=h, name="ffn_down", after=nxt)
        saved.append(sv)

    loss_blk, dh, dh_b, dg_final = _final_loss(h, final_norm.reshape(1, D), target)

    dws = {k: [None] * big[k].shape[0] for k in names}
    d_mix, d_ffn, d_convw, d_convb = [None] * depth, [None] * depth, [None] * depth, [None] * depth
    d_gq, d_gk = [None] * n_a, [None] * n_a
    dbias_list = []
    pending = []

    def start_reduce(keys, tag, after=()):
        pieces = [_split8(dws[k][l], blocked[k]) for k, l in keys]
        send, recv, pieces, lands, token = _reduce_start(pieces, "reduce_start_" + tag, after)
        pending.append((keys, send, recv, pieces, lands, tag))
        return (token,)

    tok = ()
    for i in reversed(range(depth)):
        j = i // 2
        sv = saved[i]
        wl = sv["w"]
        da = _mm_nt(dh_b, wl["w_down"], 0, blocked=False, name="ffn_down_dx", after=tok)
        dws["w_down"][i] = _mm_tn(sv["act"], dh_b, blocked=False, name="ffn_down_dw")
        du, dconv = _ffn_act_bwd(sv["u"], da, wl["conv_w"], cb3[i:i + 1], 0)
        d_convw[i], d_convb[i] = dconv[0:3], dconv[3]
        dhf = _mm_nt(du, wl["w_up"], 0, blocked=True, name="ffn_up_dx")
        dws["w_up"][i] = _mm_tn(sv["hf"], du, blocked=True, name="ffn_up_dw")
        dh, dh_b, dg = _rms_bwd(dhf, sv["h1"], ffn_norm[i:i + 1], dh)
        d_ffn[i] = dg[0]
        tok = start_reduce([("w_down", i), ("w_up", i)], "ffn%d" % i)
        if i % 2 == 0:
            do = _mm_nt(dh_b, wl["a_w_o"], 0, blocked=False, name="a_out_dx", after=tok)
            dws["a_w_o"][j] = _mm_tn(sv["o"], dh_b, blocked=False, name="a_out_dw")
            dq, dk, dv = _flash_a_bwd(sv["qkvh"], do, sv["o"], sv["lse"])
            dqkv, dgain = _prep_a_bwd(dq, dk, dv, sv["qkv"], cos, sin, a_q_gain[j:j + 1], a_k_gain[j:j + 1])
            d_gq[j], d_gk[j] = dgain[0], dgain[1]
            dhn = _mm_nt(dqkv, wl["a_w_qkv"], 0, blocked=True, name="a_qkv_dx")
            dws["a_w_qkv"][j] = _mm_tn(sv["hn"], dqkv, blocked=True, name="a_qkv_dw")
            mix_keys = [("a_w_o", j), ("a_w_qkv", j)]
        else:
            dy = _mm_nt(dh_b, wl["b_w_o"], 0, blocked=False, name="b_out_dx", after=tok)
            dws["b_w_o"][j] = _mm_tn(sv["y"], dh_b, blocked=False, name="b_out_dw")
            dos, dlzs = _combine_bwd(dy, sv["os"], sv["lzs"])
            parts = []
            for g in range(G):
                dq, rt, db = _battn_bwd_dq(sv["qkvp"][g], bias, dos[g], sv["os"][g], sv["lzs"][g], dlzs[g], g)
                dk, dv = _battn_bwd_dkv(sv["qkvp"][g], bias_t, dos[g], sv["lzs"][g], rt, g)
                parts += [dq, dk, dv]
                dbias_list.append((g, db))
            dqkv = _concat_cast(parts, [d for _, d in B_GROUPS for _ in range(3)])
            dhn = _mm_nt(dqkv, wl["b_w_qkv"], 0, blocked=True, name="b_qkv_dx")
            dws["b_w_qkv"][j] = _mm_tn(sv["hn"], dqkv, blocked=True, name="b_qkv_dw")
            mix_keys = [("b_w_o", j), ("b_w_qkv", j)]
        dh, dh_b, dg = _rms_bwd(dhn, sv["h0"], mix_norm[i:i + 1], dh)
        d_mix[i] = dg[0]
        if i > 0:
            tok = start_reduce(mix_keys, "mix%d" % i)
    grad_x = dh.reshape(x.shape)

    dbias_layers = [jnp.stack([db for g2, db in dbias_list[l * G:(l + 1) * G]]) for l in range(n_b)]
    d_rel = _bias_reduce(dbias_layers, buckets)[:, :G * hg]

    small = [jnp.stack(d_gq), jnp.stack(d_gk), d_rel, jnp.stack(d_mix), jnp.stack(d_ffn), jnp.stack(d_convw),
             jnp.stack(d_convb), dg_final[0]]
    sizes = [int(np.prod(s.shape)) for s in small]
    flat = jnp.concatenate([s.reshape(-1) for s in small])
    rows = -(-flat.shape[0] // (LANES * SUBLANES)) * SUBLANES
    flat = jnp.pad(flat, (0, rows * LANES - flat.shape[0])).reshape(rows, LANES)
    tot = _allreduce_small(flat)
    start_reduce(mix_keys, "mix0", after=[tot])
    tot = tot.reshape(-1)
    offs = np.cumsum([0] + sizes)
    g_gq, g_gk, g_rel, g_mix, g_ffn, g_convw_full, g_convb, g_final = [
        tot[offs[k]:offs[k + 1]].reshape(small[k].shape) for k in range(len(small))]
    cq = conv_w.shape[2]
    g_convw = lax.dynamic_slice_in_dim(g_convw_full, (2 * cx + cy) * cq, cq, axis=2)

    grads = dict(a_q_gain=g_gq, a_k_gain=g_gk, rel_bias=g_rel, mix_norm=g_mix, ffn_norm=g_ffn,
                 conv_w=g_convw, conv_b=g_convb, final_norm=g_final)
    weights = dict(a_w_qkv=a_w_qkv, a_w_o=a_w_o, a_q_gain=a_q_gain, a_k_gain=a_k_gain, b_w_qkv=b_w_qkv, b_w_o=b_w_o,
                   rel_bias=rel_bias, mix_norm=mix_norm, ffn_norm=ffn_norm, w_up=w_up, conv_w=conv_w, conv_b=conv_b,
                   w_down=w_down, final_norm=final_norm)
    ms = dict(a_w_qkv=m_a_w_qkv, a_w_o=m_a_w_o, a_q_gain=m_a_q_gain, a_k_gain=m_a_k_gain, b_w_qkv=m_b_w_qkv,
              b_w_o=m_b_w_o, rel_bias=m_rel_bias, mix_norm=m_mix_norm, ffn_norm=m_ffn_norm, w_up=m_w_up,
              conv_w=m_conv_w, conv_b=m_conv_b, w_down=m_w_down, final_norm=m_final_norm)
    vs = dict(a_w_qkv=v_a_w_qkv, a_w_o=v_a_w_o, a_q_gain=v_a_q_gain, a_k_gain=v_a_k_gain, b_w_qkv=v_b_w_qkv,
              b_w_o=v_b_w_o, rel_bias=v_rel_bias, mix_norm=v_mix_norm, ffn_norm=v_ffn_norm, w_up=v_w_up,
              conv_w=v_conv_w, conv_b=v_conv_b, w_down=v_w_down, final_norm=v_final_norm)
    deltas, new_m, new_v, stacks = {}, {}, {}, {}

    def update(k):
        w = weights[k]
        two_d = (-1, w.shape[-1])
        outs = _adamw(w.reshape(two_d), grads[k].reshape(two_d), ms[k].reshape(two_d), vs[k].reshape(two_d),
                      keep_g=k in big)
        deltas[k], new_m[k], new_v[k] = [a.reshape(w.shape) for a in outs[:3]]
        if k in big:
            grads[k] = outs[3]
        return outs[2]

    def collect(items, after):
        for keys, send, recv, pieces, lands, tag in items:
            pieces, lands = _reduce_wait(send, recv, pieces, lands, after, "reduce_wait_" + tag)
            for (k, l), p, land in zip(keys, pieces, lands):
                if k not in stacks:
                    stacks[k] = lax.empty((big[k].shape[0], 2) + p.shape[2:], F32)
                stacks[k] = _add_pieces(p, land, stacks[k], l)

    def share(ks):
        for k, gs in zip(ks, _ag_sibling([stacks[k] for k in ks])):
            grads[k] = gs.reshape(big[k].shape)

    late = [k for k in names if k in {kk for kk, _ in pending[-1][0]}]
    collect(pending[:-1], [dh])
    share([k for k in names if k not in late])
    done = [update(k) for k in weights if k not in late]
    collect(pending[-1:], done)
    share(late)
    for k in late:
        update(k)

    loss = lax.psum(loss_blk[0, 0], ("x", "y", "c"))
    keys = list(weights)
    return (loss, grad_x, *[grads[k].reshape(weights[k].shape) for k in keys], *[deltas[k] for k in keys],
            *[new_m[k] for k in keys], *[new_v[k] for k in keys])
```

```python
import functools
import math

import numpy as np
import jax
import jax.numpy as jnp
from jax import lax
from jax.experimental import pallas as pl
from jax.experimental.pallas import tpu as pltpu

F32 = jnp.float32
BF16 = jnp.bfloat16

HEAD_DIM = 128
A_HEADS = 16
A_KV_HEADS = 4
GRID_W = 64
ROPE_THETA = 10000.0
B_GROUPS = ((128, 1), (512, 4), (2048, 16))
B_HEADS_PER_GROUP = 8
REL_BUCKETS = 32
REL_MAX_DISTANCE = 1024
EPS = 1e-6
NEG_INF = -1e30
DEPTH = 4
ADAM_LR = 0.001
ADAM_B1 = 0.9
ADAM_B2 = 0.999
ADAM_EPS = 1e-08
ADAM_WD = 0.01
ADAM_STEP = 10

N_CHIPS = 4
LANES = 128
SUBLANES = 8
VMEM_LIMIT = 52 * 1024 * 1024
MESH = pl.DeviceIdType.MESH


def _pick(n, cands):
    for c in cands:
        if c <= n and n % c == 0:
            return c
    return n


def _lane_tile(n, cap):
    best = None
    for t in range(LANES, min(n, cap) + 1, LANES):
        if n % t == 0:
            best = t
    return best or n


def _cp(*sem):
    return pltpu.CompilerParams(dimension_semantics=sem if sem else None, vmem_limit_bytes=VMEM_LIMIT)


def _half_span():
    hs = {w // (2 * d) for w, d in B_GROUPS}
    assert len(hs) == 1
    return hs.pop()


def _rms_fwd(h, gain, after=()):
    S, D = h.shape
    ts = _pick(S, (512, 256, 128, 64, 32, 16))

    def body(h_ref, g_ref, *rest):
        o_ref = rest[-1]
        x = h_ref[...]
        r = lax.rsqrt(jnp.mean(x * x, axis=-1, keepdims=True) + EPS)
        o_ref[...] = (x * r * g_ref[...]).astype(o_ref.dtype)

    return pl.pallas_call(
        body, name="rms_fwd", grid=(S // ts,),
        in_specs=[pl.BlockSpec((ts, D), lambda i: (i, 0)), pl.BlockSpec((1, D), lambda i: (0, 0))]
        + [pl.BlockSpec(memory_space=pl.ANY)] * len(after),
        out_specs=pl.BlockSpec((ts, D), lambda i: (i, 0)),
        out_shape=jax.ShapeDtypeStruct((S, D), BF16), compiler_params=_cp("arbitrary"))(h, gain, *after)


def _rms_bwd(dy, h, gain, dres):
    S, D = h.shape
    ts = _pick(S, (256, 128, 64, 32, 16))

    def body(dy_ref, h_ref, g_ref, dres_ref, dh_ref, dhb_ref, dg_ref):
        @pl.when(pl.program_id(0) == 0)
        def _():
            dg_ref[...] = jnp.zeros_like(dg_ref)
        x = h_ref[...]
        dy = dy_ref[...]
        r = lax.rsqrt(jnp.mean(x * x, axis=-1, keepdims=True) + EPS)
        xn = x * r
        dg_ref[0:1, :] += jnp.sum(dy * xn, axis=0, keepdims=True)
        dxn = dy * g_ref[...]
        dx = r * (dxn - xn * jnp.mean(dxn * xn, axis=-1, keepdims=True))
        dh = dres_ref[...] + dx
        dh_ref[...] = dh
        dhb_ref[...] = dh.astype(BF16)

    row = pl.BlockSpec((ts, D), lambda i: (i, 0))
    return pl.pallas_call(
        body, name="rms_bwd", grid=(S // ts,),
        in_specs=[row, row, pl.BlockSpec((1, D), lambda i: (0, 0)), row],
        out_specs=[row, row, pl.BlockSpec((SUBLANES, D), lambda i: (0, 0))],
        out_shape=[jax.ShapeDtypeStruct((S, D), F32), jax.ShapeDtypeStruct((S, D), BF16),
                   jax.ShapeDtypeStruct((SUBLANES, D), F32)],
        compiler_params=_cp("arbitrary"))(dy, h, gain, dres)


def _final_loss(h, gain, target):
    S, D = h.shape
    ts = _pick(S, (256, 128, 64, 32, 16))

    def body(h_ref, g_ref, t_ref, loss_ref, dh_ref, dhb_ref, dg_ref):
        @pl.when(pl.program_id(0) == 0)
        def _():
            dg_ref[...] = jnp.zeros_like(dg_ref)
            loss_ref[...] = jnp.zeros_like(loss_ref)
        x = h_ref[...]
        g = g_ref[...]
        r = lax.rsqrt(jnp.mean(x * x, axis=-1, keepdims=True) + EPS)
        xn = x * r
        err = xn * g - t_ref[...]
        part = 0.5 * jnp.sum(jnp.mean(err * err, axis=-1, keepdims=True), axis=0, keepdims=True)
        loss_ref[0:1, 0:1] += part
        dy = err * (1.0 / D)
        dg_ref[0:1, :] += jnp.sum(dy * xn, axis=0, keepdims=True)
        dxn = dy * g
        dh = r * (dxn - xn * jnp.mean(dxn * xn, axis=-1, keepdims=True))
        dh_ref[...] = dh
        dhb_ref[...] = dh.astype(BF16)

    row = pl.BlockSpec((ts, D), lambda i: (i, 0))
    return pl.pallas_call(
        body, name="final_loss", grid=(S // ts,),
        in_specs=[row, pl.BlockSpec((1, D), lambda i: (0, 0)), row],
        out_specs=[pl.BlockSpec((SUBLANES, LANES), lambda i: (0, 0)), row, row,
                   pl.BlockSpec((SUBLANES, D), lambda i: (0, 0))],
        out_shape=[jax.ShapeDtypeStruct((SUBLANES, LANES), F32), jax.ShapeDtypeStruct((S, D), F32),
                   jax.ShapeDtypeStruct((S, D), BF16), jax.ShapeDtypeStruct((SUBLANES, D), F32)],
        compiler_params=_cp("arbitrary"))(h, gain, target)


_NN = (((1,), (0,)), ((), ()))
_NT = (((1,), (1,)), ((), ()))
_TN = (((0,), (0,)), ((), ()))


def _mm_nn(a, w, layer, *, blocked, out_dtype=F32, res=None, name, after=()):
    M, K = a.shape
    if blocked:
        nq = w.shape[3]
        N = N_CHIPS * nq
        tn = _lane_tile(nq, 1408)
        nps = nq // tn
        w_spec = pl.BlockSpec((None, None, K, tn), lambda i, j: (layer, j // nps, 0, j % nps))
    else:
        N = w.shape[2]
        tn = _lane_tile(N, 1024)
        w_spec = pl.BlockSpec((None, K, tn), lambda i, j: (layer, 0, j))
    tm = _pick(M, (1024, 512, 256, 128, 64, 32, 16)) if K <= 3072 else _pick(M, (512, 256, 128, 64, 32, 16))

    def body(*refs):
        a_ref, w_ref, o_ref = refs[0], refs[1], refs[-1]
        acc = lax.dot_general(a_ref[...], w_ref[...], _NN, preferred_element_type=F32)
        if res is not None:
            acc = refs[2][...] + acc
        o_ref[...] = acc.astype(o_ref.dtype)

    in_specs = [pl.BlockSpec((tm, K), lambda i, j: (i, 0)), w_spec]
    args = [a, w]
    if res is not None:
        in_specs.append(pl.BlockSpec((tm, tn), lambda i, j: (i, j)))
        args.append(res)
    return pl.pallas_call(
        body, name=name, grid=(M // tm, N // tn), in_specs=in_specs + [pl.BlockSpec(memory_space=pl.ANY)] * len(after),
        out_specs=pl.BlockSpec((tm, tn), lambda i, j: (i, j)),
        out_shape=jax.ShapeDtypeStruct((M, N), out_dtype),
        compiler_params=_cp("arbitrary", "arbitrary"))(*args, *after)


def _mm_nt(a, w, layer, *, blocked, name, after=()):
    pair = isinstance(a, tuple)
    M = a[0].shape[0] if pair else a.shape[0]
    tm = _pick(M, (1024, 512, 256, 128, 64, 32, 16))
    if blocked:
        K, nq = w.shape[2], w.shape[3]
        tk = _pick(K, (1024, 512, 256, 128))
        half = N_CHIPS // 2

        def body(*refs):
            a_refs, (w_ref, o_ref, acc_ref) = refs[:-3], refs[-3:]
            p = pl.program_id(2)

            @pl.when(p == 0)
            def _():
                acc_ref[...] = jnp.zeros_like(acc_ref)
            if pair:
                @pl.when(p < half)
                def _():
                    acc_ref[...] += lax.dot_general(a_refs[0][...], w_ref[...], _NT, preferred_element_type=F32)

                @pl.when(p >= half)
                def _():
                    acc_ref[...] += lax.dot_general(a_refs[1][...], w_ref[...], _NT, preferred_element_type=F32)
            else:
                acc_ref[...] += lax.dot_general(a_refs[0][...], w_ref[...], _NT, preferred_element_type=F32)

            @pl.when(p == N_CHIPS - 1)
            def _():
                o_ref[...] = acc_ref[...]

        if pair:
            a_specs = [pl.BlockSpec((tm, nq), lambda i, j, p: (i, jnp.minimum(p, half - 1))),
                       pl.BlockSpec((tm, nq), lambda i, j, p: (i, jnp.maximum(p - half, 0)))]
            a_args = list(a)
        else:
            a_specs = [pl.BlockSpec((tm, nq), lambda i, j, p: (i, p))]
            a_args = [a]
        return pl.pallas_call(
            body, name=name, grid=(M // tm, K // tk, N_CHIPS),
            in_specs=a_specs + [pl.BlockSpec((None, None, tk, nq), lambda i, j, p: (layer, p, j, 0))],
            out_specs=pl.BlockSpec((tm, tk), lambda i, j, p: (i, j)),
            out_shape=jax.ShapeDtypeStruct((M, K), F32),
            scratch_shapes=[pltpu.VMEM((tm, tk), F32)],
            compiler_params=_cp("arbitrary", "arbitrary", "arbitrary"))(*a_args, w)
    K, N = w.shape[1], w.shape[2]
    tk = _pick(K, (1024, 512, 256, 128))

    def body(a_ref, w_ref, *rest):
        rest[-1][...] = lax.dot_general(a_ref[...], w_ref[...], _NT, preferred_element_type=F32)

    return pl.pallas_call(
        body, name=name, grid=(M // tm, K // tk),
        in_specs=[pl.BlockSpec((tm, N), lambda i, j: (i, 0)),
                  pl.BlockSpec((None, tk, N), lambda i, j: (layer, j, 0))]
        + [pl.BlockSpec(memory_space=pl.ANY)] * len(after),
        out_specs=pl.BlockSpec((tm, tk), lambda i, j: (i, j)),
        out_shape=jax.ShapeDtypeStruct((M, K), F32),
        compiler_params=_cp("arbitrary", "arbitrary"))(a, w, *after)


def _mm_tn(x, dy, *, blocked, name):
    pair = isinstance(dy, tuple)
    S, K = x.shape
    N = 2 * dy[0].shape[1] if pair else dy.shape[1]
    tk = _pick(K, (512, 256, 128))
    if blocked:
        nq = N // N_CHIPS
        tn = _lane_tile(nq, 1408)
        nps = nq // tn
        out_spec = pl.BlockSpec((None, tk, tn), lambda i, j, s: (j // nps, i, j % nps))
        out_shape = jax.ShapeDtypeStruct((N_CHIPS, K, nq), BF16)
    else:
        tn = _lane_tile(N, 1024)
        out_spec = pl.BlockSpec((tk, tn), lambda i, j, s: (i, j))
        out_shape = jax.ShapeDtypeStruct((K, N), BF16)
    nj = N // tn
    njh = nj // 2
    ns = 2 if pair else 1
    sh = S // ns

    def body(x_ref, *refs):
        o_ref, acc_ref = refs[-2], refs[-1]

        def product(dy_ref):
            part = lax.dot_general(x_ref[...], dy_ref[...], _TN, preferred_element_type=F32)
            if ns == 1:
                o_ref[...] = part.astype(o_ref.dtype)
            else:
                s = pl.program_id(2)

                @pl.when(s == 0)
                def _():
                    acc_ref[...] = part

                @pl.when(s == ns - 1)
                def _():
                    o_ref[...] = (acc_ref[...] + part).astype(o_ref.dtype)

        if pair:
            j = pl.program_id(1)
            pl.when(j < njh)(lambda: product(refs[0]))
            pl.when(j >= njh)(lambda: product(refs[1]))
        else:
            product(refs[0])

    if pair:
        assert nj % 2 == 0
        dy_specs = [pl.BlockSpec((sh, tn), lambda i, j, s: (jnp.where(j < njh, s, ns - 1), jnp.minimum(j, njh - 1))),
                    pl.BlockSpec((sh, tn), lambda i, j, s: (jnp.where(j < njh, 0, s), jnp.maximum(j - njh, 0)))]
        dy_args = list(dy)
    else:
        dy_specs = [pl.BlockSpec((sh, tn), lambda i, j, s: (s, j))]
        dy_args = [dy]
    return pl.pallas_call(
        body, name=name, grid=(K // tk, nj, ns),
        in_specs=[pl.BlockSpec((sh, tk), lambda i, j, s: (s, i))] + dy_specs,
        out_specs=out_spec, out_shape=out_shape,
        scratch_shapes=[pltpu.VMEM((tk, tn) if ns > 1 else (SUBLANES, LANES), F32)],
        compiler_params=_cp("arbitrary", "arbitrary", "arbitrary"))(x, *dy_args)


def _rope_tables(S):
    rows = S // GRID_W
    row_ids = jnp.repeat(jnp.arange(rows, dtype=F32), GRID_W)
    col_ids = jnp.tile(jnp.arange(GRID_W, dtype=F32), rows)
    quarter = HEAD_DIM // 4
    inv_freq = ROPE_THETA ** (-jnp.arange(quarter, dtype=F32) / quarter)
    ang_r = row_ids[:, None] * inv_freq[None, :]
    ang_c = col_ids[:, None] * inv_freq[None, :]
    cos = jnp.concatenate([jnp.cos(ang_r)] * 2 + [jnp.cos(ang_c)] * 2, axis=-1)
    sin = jnp.concatenate([-jnp.sin(ang_r), jnp.sin(ang_r), -jnp.sin(ang_c), jnp.sin(ang_c)], axis=-1)
    return cos, sin


def _swap_quarters(x):
    lane = lax.broadcasted_iota(jnp.int32, x.shape, 1)
    first = (lane % (HEAD_DIM // 2)) < (HEAD_DIM // 4)
    return jnp.where(first, pltpu.roll(x, HEAD_DIM - HEAD_DIM // 4, 1), pltpu.roll(x, HEAD_DIM // 4, 1))


A_SCALE = HEAD_DIM ** -0.5
A_QSCALE = A_SCALE * math.log2(math.e)


def _prep_a_fwd(qkv, cos, sin, gq, gk):
    S, W = qkv.shape
    nrm = A_HEADS + A_KV_HEADS
    ts = _pick(S, (256, 128, 64, 32, 16))

    def body(qkv_ref, cos_ref, sin_ref, gq_ref, gk_ref, o_ref):
        cos_t = cos_ref[...]
        sin_t = sin_ref[...]
        for j in range(nrm):
            sl = slice(j * HEAD_DIM, (j + 1) * HEAD_DIM)
            x = qkv_ref[:, sl]
            g = gq_ref[...] if j < A_HEADS else gk_ref[...]
            r = lax.rsqrt(jnp.mean(x * x, axis=-1, keepdims=True) + EPS)
            n = x * r * g
            y = n * cos_t + _swap_quarters(n) * sin_t
            o_ref[:, sl] = (y * A_QSCALE if j < A_HEADS else y).astype(BF16)
        o_ref[:, nrm * HEAD_DIM:] = qkv_ref[:, nrm * HEAD_DIM:].astype(BF16)

    row = lambda w: pl.BlockSpec((ts, w), lambda i: (i, 0))
    one = pl.BlockSpec((1, HEAD_DIM), lambda i: (0, 0))
    return pl.pallas_call(
        body, name="prep_a_fwd", grid=(S // ts,),
        in_specs=[row(W), row(HEAD_DIM), row(HEAD_DIM), one, one], out_specs=row(W),
        out_shape=jax.ShapeDtypeStruct((S, W), BF16), compiler_params=_cp("arbitrary"))(qkv, cos, sin, gq, gk)


def _prep_a_bwd(dq, dk, dv, qkv, cos, sin, gq, gk):
    S, W = qkv.shape
    nrm = A_HEADS + A_KV_HEADS
    nq, nk = A_HEADS * HEAD_DIM, A_KV_HEADS * HEAD_DIM
    ts = _pick(S, (256, 128, 64, 32, 16))

    def body(dq_ref, dk_ref, dv_ref, qkv_ref, cos_ref, sin_ref, gq_ref, gk_ref, o_ref, dg_ref):
        @pl.when(pl.program_id(0) == 0)
        def _():
            dg_ref[...] = jnp.zeros_like(dg_ref)
        cos_t = cos_ref[...]
        sin_t = sin_ref[...]
        for j in range(nrm):
            sl = slice(j * HEAD_DIM, (j + 1) * HEAD_DIM)
            x = qkv_ref[:, sl]
            if j < A_HEADS:
                dy, g, grow = dq_ref[:, sl], gq_ref[...], 0
            else:
                jj = j - A_HEADS
                dy, g, grow = dk_ref[:, jj * HEAD_DIM:(jj + 1) * HEAD_DIM], gk_ref[...], 1
            r = lax.rsqrt(jnp.mean(x * x, axis=-1, keepdims=True) + EPS)
            xn = x * r
            dn = dy * cos_t + _swap_quarters(dy * sin_t)
            dg_ref[grow:grow + 1, :] += jnp.sum(dn * xn, axis=0, keepdims=True)
            dxn = dn * g
            o_ref[:, sl] = (r * (dxn - xn * jnp.mean(dxn * xn, axis=-1, keepdims=True))).astype(BF16)
        o_ref[:, nrm * HEAD_DIM:] = dv_ref[...].astype(BF16)

    row = lambda w: pl.BlockSpec((ts, w), lambda i: (i, 0))
    one = pl.BlockSpec((1, HEAD_DIM), lambda i: (0, 0))
    return pl.pallas_call(
        body, name="prep_a_bwd", grid=(S // ts,),
        in_specs=[row(nq), row(nk), row(nk), row(W), row(HEAD_DIM), row(HEAD_DIM), one, one],
        out_specs=[row(W), pl.BlockSpec((SUBLANES, HEAD_DIM), lambda i: (0, 0))],
        out_shape=[jax.ShapeDtypeStruct((S, W), BF16), jax.ShapeDtypeStruct((SUBLANES, HEAD_DIM), F32)],
        compiler_params=_cp("arbitrary"))(dq, dk, dv, qkv, cos, sin, gq, gk)


def _flash_a_fwd(qkvh, after=()):
    S = qkvh.shape[0]
    grp = A_HEADS // A_KV_HEADS
    tq = _pick(S, (1024, 512, 256, 128, 64, 32, 16))
    kc = _pick(S, (512, 256, 128))
    lanes = [slice(b * LANES, (b + 1) * LANES) for b in range(kc // LANES)]
    sub = 4 if tq % 64 == 0 else 1
    ts = tq // sub
    subs = [slice(b * ts, (b + 1) * ts) for b in range(sub)]

    def body(q_ref, k_ref, v_ref, *rest):
        o_ref, lse_ref = rest[-2], rest[-1]
        qs_ = [q_ref[r, :] for r in subs]
        m_t = [jnp.full((ts, LANES), -jnp.inf, F32) for _ in subs]
        for c in range(S // kc):
            k_c = k_ref[c * kc:(c + 1) * kc, :]
            for b in range(sub):
                s = lax.dot_general(qs_[b], k_c, _NT, preferred_element_type=F32)
                for sl in lanes:
                    m_t[b] = jnp.maximum(m_t[b], s[:, sl])
        m = [jnp.max(t, axis=-1, keepdims=True) for t in m_t]
        l_t = [jnp.zeros((ts, LANES), F32) for _ in subs]
        acc = [jnp.zeros((ts, HEAD_DIM), F32) for _ in subs]
        for c in range(S // kc):
            rows = slice(c * kc, (c + 1) * kc)
            k_c, v_c = k_ref[rows, :], v_ref[rows, :]
            for b in range(sub):
                p = jnp.exp2(lax.dot_general(qs_[b], k_c, _NT, preferred_element_type=F32) - m[b])
                for sl in lanes:
                    l_t[b] = l_t[b] + p[:, sl]
                acc[b] = acc[b] + lax.dot_general(p.astype(BF16), v_c, _NN, preferred_element_type=F32)
        for b, r in enumerate(subs):
            l = jnp.sum(l_t[b], axis=-1, keepdims=True)
            o_ref[r, :] = (acc[b] * (1.0 / l)).astype(BF16)
            lse_ref[r, :] = jnp.broadcast_to(m[b] + jnp.log2(l), (ts, HEAD_DIM))

    qs = pl.BlockSpec((tq, HEAD_DIM), lambda h, i: (i, h))
    return pl.pallas_call(
        body, name="flash_a_fwd", grid=(A_HEADS, S // tq),
        in_specs=[qs,
                  pl.BlockSpec((S, HEAD_DIM), lambda h, i: (0, A_HEADS + h // grp)),
                  pl.BlockSpec((S, HEAD_DIM), lambda h, i: (0, A_HEADS + A_KV_HEADS + h // grp))]
        + [pl.BlockSpec(memory_space=pl.ANY)] * len(after),
        out_specs=[qs, qs],
        out_shape=[jax.ShapeDtypeStruct((S, A_HEADS * HEAD_DIM), BF16),
                   jax.ShapeDtypeStruct((S, A_HEADS * HEAD_DIM), F32)],
        compiler_params=_cp("arbitrary", "arbitrary"))(qkvh, qkvh, qkvh, *after)


def _flash_a_bwd(qkvh, do, o, lse):
    S = qkvh.shape[0]
    grp = A_HEADS // A_KV_HEADS
    tq = _pick(S, (1024, 512, 256, 128, 64, 32, 16))
    nq = S // tq
    sub = 4 if tq % 64 == 0 else 1
    ts = tq // sub

    def body(q_ref, k_ref, v_ref, do_ref, o_ref, lse_ref, dq_ref, dk_ref, dv_ref):
        g, i = pl.program_id(1), pl.program_id(2)

        @pl.when((g == 0) & (i == 0))
        def _():
            dk_ref[...] = jnp.zeros_like(dk_ref)
            dv_ref[...] = jnp.zeros_like(dv_ref)
        k = k_ref[...]
        dk = dv = None
        for b in range(sub):
            rows = slice(b * ts, (b + 1) * ts)
            q = q_ref[rows, :]
            do_f = do_ref[rows, :]
            do_b = do_f.astype(BF16)
            delta = jnp.sum(do_f * o_ref[rows, :].astype(F32), axis=-1, keepdims=True)
            p = jnp.exp2(lax.dot_general(q, k, _NT, preferred_element_type=F32) - lse_ref[rows, 0:1])
            dp = lax.dot_general(do_b, v_ref[...], _NT, preferred_element_type=F32)
            ds_b = (p * (dp - delta)).astype(BF16)
            dq_ref[rows, :] = lax.dot_general(ds_b, k, _NN, preferred_element_type=F32) * A_SCALE
            dk_b = lax.dot_general(ds_b, q, _TN, preferred_element_type=F32)
            dv_b = lax.dot_general(p.astype(BF16), do_b, _TN, preferred_element_type=F32)
            dk = dk_b if dk is None else dk + dk_b
            dv = dv_b if dv is None else dv + dv_b
        dk_ref[...] += dk
        dv_ref[...] += dv

        @pl.when((g == grp - 1) & (i == nq - 1))
        def _():
            dk_ref[...] = dk_ref[...] * (A_SCALE / A_QSCALE)

    qs = pl.BlockSpec((tq, HEAD_DIM), lambda kv, g, i: (i, kv * grp + g))
    kvs = lambda off: pl.BlockSpec((S, HEAD_DIM), lambda kv, g, i: (0, off + kv))
    return pl.pallas_call(
        body, name="flash_a_bwd", grid=(A_KV_HEADS, grp, S // tq),
        in_specs=[qs, kvs(A_HEADS), kvs(A_HEADS + A_KV_HEADS), qs, qs, qs],
        out_specs=[qs, kvs(0), kvs(0)],
        out_shape=[jax.ShapeDtypeStruct((S, A_HEADS * HEAD_DIM), F32),
                   jax.ShapeDtypeStruct((S, A_KV_HEADS * HEAD_DIM), F32),
                   jax.ShapeDtypeStruct((S, A_KV_HEADS * HEAD_DIM), F32)],
        compiler_params=_cp("arbitrary", "arbitrary", "arbitrary"))(qkvh, qkvh, qkvh, do, o, lse)


def _bucket_tables(transposed):
    hs = _half_span()
    tq, kv = 2 * hs, 4 * hs
    nb = REL_BUCKETS // 2
    max_exact = nb // 2
    shape = (kv, tq) if transposed else (tq, kv)
    out = np.zeros((len(B_GROUPS), 3) + shape, np.int32)
    win = np.arange(kv) - hs
    blk = np.arange(tq)
    for g, (_, dil) in enumerate(B_GROUPS):
        for case in range(3):
            inside = ((win >= 0) | (case != 0)) & ((win < tq) | (case != 2))
            if transposed:
                rel = blk[None, :] - win[:, None]
                ok = inside[:, None]
            else:
                rel = win[None, :] - blk[:, None]
                ok = inside[None, :]
            r = rel * dil
            n = np.abs(r)
            nf = np.maximum(n, 1).astype(np.float32)
            large = max_exact + (np.log(nf / np.float32(max_exact)) / np.float32(math.log(REL_MAX_DISTANCE / max_exact))
                                 * np.float32(nb - max_exact)).astype(np.int32)
            large = np.minimum(large, nb - 1)
            bucket = np.where(r > 0, nb, 0) + np.where(n < max_exact, n, large)
            out[g, case] = np.where((np.abs(rel) <= hs) & ok, bucket, -1)
    return out


def _bias_build(rel_bias, buckets):
    G, _, tq, kv = buckets.shape
    hg = B_HEADS_PER_GROUP

    def body(rb_ref, bk_ref, o_ref):
        col = pl.program_id(0) * hg + pl.program_id(2)
        bk = bk_ref[...]
        acc = jnp.full((tq, kv), NEG_INF, F32)
        for b in range(REL_BUCKETS):
            acc = jnp.where(bk == b, rb_ref[b, col], acc)
        o_ref[...] = acc

    return pl.pallas_call(
        body, name="bias_build", grid=(G, 3, hg),
        in_specs=[pl.BlockSpec(memory_space=pltpu.SMEM),
                  pl.BlockSpec((None, None, tq, kv), lambda g, c, h: (g, c, 0, 0))],
        out_specs=pl.BlockSpec((None, None, None, tq, kv), lambda g, c, h: (g, c, h, 0, 0)),
        out_shape=jax.ShapeDtypeStruct((G, 3, hg, tq, kv), F32),
        compiler_params=_cp("arbitrary", "arbitrary", "arbitrary"))(rel_bias, buckets)


def _bias_reduce(dbias_list, buckets):
    G, _, tq, kv = buckets.shape
    hg = B_HEADS_PER_GROUP
    n = len(dbias_list)

    def body(*refs):
        bk_ref, o_ref = refs[n], refs[n + 1]
        first = (pl.program_id(0) == 0) & (pl.program_id(1) == 0) & (pl.program_id(2) == 0)

        @pl.when(first)
        def _():
            o_ref[...] = jnp.zeros_like(o_ref)
        col = pl.program_id(0) * hg + pl.program_id(2)
        db = refs[0][...]
        for r in refs[1:n]:
            db = db + r[...]
        bk = bk_ref[...]
        rows = lax.broadcasted_iota(jnp.int32, (REL_BUCKETS, LANES), 0)
        cols = lax.broadcasted_iota(jnp.int32, (REL_BUCKETS, LANES), 1)
        acc = jnp.zeros((REL_BUCKETS, LANES), F32)
        for b in range(REL_BUCKETS):
            val = jnp.sum(jnp.sum(jnp.where(bk == b, db, 0.0), axis=1, keepdims=True), axis=0, keepdims=True)
            acc = acc + jnp.where((rows == b) & (cols == col), val, 0.0)
        o_ref[...] += acc

    tile = pl.BlockSpec((None, None, None, tq, kv), lambda g, c, h: (g, c, h, 0, 0))
    return pl.pallas_call(
        body, name="bias_reduce", grid=(G, 3, hg),
        in_specs=[tile] * n + [pl.BlockSpec((None, None, tq, kv), lambda g, c, h: (g, c, 0, 0))],
        out_specs=pl.BlockSpec((REL_BUCKETS, LANES), lambda g, c, h: (0, 0)),
        out_shape=jax.ShapeDtypeStruct((REL_BUCKETS, LANES), F32),
        compiler_params=_cp("arbitrary", "arbitrary", "arbitrary"))(*dbias_list, buckets)


def _mm_nn_perm(a, w, g):
    S, K = a.shape
    nq = w.shape[3]
    dil = B_GROUPS[g][1]
    wg3 = 3 * B_HEADS_PER_GROUP * HEAD_DIM
    tn = _lane_tile(math.gcd(nq, wg3), 768)
    nps, ntile = nq // tn, wg3 // tn
    tm = _pick(S, (1024, 512, 256))
    rows = tm // dil

    def body(a_ref, w_ref, o_ref, acc_ref):
        acc = lax.dot_general(a_ref[...], w_ref[...], _NN, preferred_element_type=F32)
        if dil == 1:
            o_ref[0] = acc.astype(BF16)
        else:
            for k in range(tn // LANES):
                acc_ref[k] = acc[:, k * LANES:(k + 1) * LANES]
            for c in range(dil):
                for k in range(tn // LANES):
                    o_ref[c, :, k * LANES:(k + 1) * LANES] = acc_ref[k, pl.ds(c, rows, stride=dil), :].astype(BF16)

    def w_map(i, j):
        t = g * ntile + j
        return (0, t // nps, 0, t % nps)

    return pl.pallas_call(
        body, name="b_qkv_g%d" % g, grid=(S // tm, ntile),
        in_specs=[pl.BlockSpec((tm, K), lambda i, j: (i, 0)), pl.BlockSpec((None, None, K, tn), w_map)],
        out_specs=pl.BlockSpec((dil, rows, tn), lambda i, j: (0, i, j)),
        out_shape=jax.ShapeDtypeStruct((dil, S // dil, wg3), BF16),
        scratch_shapes=[pltpu.VMEM((tn // LANES, tm, LANES), F32)],
        compiler_params=_cp("arbitrary", "arbitrary"))(a, w)


def _window_specs(S, wg, col):
    hs = _half_span()
    tq = 2 * hs
    per = tq // hs
    return (pl.BlockSpec((tq, wg), lambda i: (i, col)),
            pl.BlockSpec((hs, wg), lambda i: (jnp.maximum(i * per - 1, 0), col)),
            pl.BlockSpec((hs, wg), lambda i: (jnp.minimum((i + 1) * per, S // hs - 1), col)))


def _window_case(i, L):
    per = L // (2 * _half_span())
    r = i % per
    return jnp.where(r == 0, 0, jnp.where(r == per - 1, 2, 1))


def _window(prev_ref, main_ref, next_ref, sl):
    return jnp.concatenate([prev_ref[:, sl], main_ref[:, sl], next_ref[:, sl]], axis=0)


def _battn_fwd(qkvp, bias, g):
    dil, L, wg3 = qkvp.shape
    S = dil * L
    hs = _half_span()
    tq, kvl = 2 * hs, 4 * hs
    hg = B_HEADS_PER_GROUP
    wg = hg * HEAD_DIM
    scale = HEAD_DIM ** -0.5
    flat = qkvp.reshape(S, wg3)

    def body(q_ref, km, kp, kn, vm, vp, vn, b_ref, o_ref, lz_ref):
        case = _window_case(pl.program_id(0), L)
        for h in range(hg):
            sl = slice(h * HEAD_DIM, (h + 1) * HEAD_DIM)
            s = lax.dot_general(q_ref[:, sl], _window(kp, km, kn, sl), _NT, preferred_element_type=F32) * scale
            s = s + b_ref[case, h]
            m = jnp.max(s, axis=-1, keepdims=True)
            p = jnp.exp(s - m)
            l = jnp.sum(p, axis=-1, keepdims=True)
            o_ref[:, sl] = lax.dot_general(p.astype(BF16), _window(vp, vm, vn, sl), _NN, preferred_element_type=F32) / l
            lz_ref[:, sl] = jnp.broadcast_to(m + jnp.log(l), (tq, HEAD_DIM))

    blk = pl.BlockSpec((tq, wg), lambda i: (i, 0))
    o, lz = pl.pallas_call(
        body, name="battn_fwd_g%d" % g, grid=(S // tq,),
        in_specs=[_window_specs(S, wg, 0)[0], *_window_specs(S, wg, 1), *_window_specs(S, wg, 2),
                  pl.BlockSpec((None, 3, hg, tq, kvl), lambda i: (g, 0, 0, 0, 0))],
        out_specs=[blk, blk], out_shape=[jax.ShapeDtypeStruct((S, wg), F32)] * 2,
        compiler_params=_cp("arbitrary"))(flat, flat, flat, flat, flat, flat, flat, bias)
    return o, lz


def _battn_bwd_dq(qkvp, bias, do, o, lz, dlz, g):
    dil, L, wg3 = qkvp.shape
    S = dil * L
    hs = _half_span()
    tq, kvl = 2 * hs, 4 * hs
    hg = B_HEADS_PER_GROUP
    wg = hg * HEAD_DIM
    scale = HEAD_DIM ** -0.5
    flat = qkvp.reshape(S, wg3)

    def body(q_ref, km, kp, kn, vm, vp, vn, b_ref, do_ref, o_ref, lz_ref, dlz_ref, dq_ref, rt_ref, db_ref):
        i = pl.program_id(0)

        @pl.when(i == 0)
        def _():
            db_ref[...] = jnp.zeros_like(db_ref)
        case = _window_case(i, L)
        for h in range(hg):
            sl = slice(h * HEAD_DIM, (h + 1) * HEAD_DIM)
            kw = _window(kp, km, kn, sl)
            do_f = do_ref[:, sl]
            s = lax.dot_general(q_ref[:, sl], kw, _NT, preferred_element_type=F32) * scale + b_ref[case, h]
            p = jnp.exp(s - lz_ref[:, sl][:, 0:1])
            dp = lax.dot_general(do_f.astype(BF16), _window(vp, vm, vn, sl), _NT, preferred_element_type=F32)
            rt = dlz_ref[:, sl][:, 0:1] - jnp.sum(do_f * o_ref[:, sl], axis=-1, keepdims=True)
            ds = p * (dp + rt)
            db_ref[case, h] += ds
            dq_ref[:, sl] = lax.dot_general((ds * scale).astype(BF16), kw, _NN, preferred_element_type=F32)
            rt_ref[:, sl] = jnp.broadcast_to(rt, (tq, HEAD_DIM))

    blk = pl.BlockSpec((tq, wg), lambda i: (i, 0))
    row = jax.ShapeDtypeStruct((S, wg), F32)
    return pl.pallas_call(
        body, name="battn_bwd_dq_g%d" % g, grid=(S // tq,),
        in_specs=[_window_specs(S, wg, 0)[0], *_window_specs(S, wg, 1), *_window_specs(S, wg, 2),
                  pl.BlockSpec((None, 3, hg, tq, kvl), lambda i: (g, 0, 0, 0, 0)), blk, blk, blk, blk],
        out_specs=[blk, blk, pl.BlockSpec((3, hg, tq, kvl), lambda i: (0, 0, 0, 0))],
        out_shape=[row, row, jax.ShapeDtypeStruct((3, hg, tq, kvl), F32)],
        compiler_params=_cp("arbitrary"))(flat, flat, flat, flat, flat, flat, flat, bias, do, o, lz, dlz)


def _battn_bwd_dkv(qkvp, bias_t, do, lz, rt, g):
    dil, L, wg3 = qkvp.shape
    S = dil * L
    hs = _half_span()
    tq, kvl = 2 * hs, 4 * hs
    hg = B_HEADS_PER_GROUP
    wg = hg * HEAD_DIM
    scale = HEAD_DIM ** -0.5
    flat = qkvp.reshape(S, wg3)

    def body(k_ref, v_ref, qm, qp, qn, dom, dop, don, lzm, lzp, lzn, rtm, rtp, rtn, b_ref, dk_ref, dv_ref):
        case = _window_case(pl.program_id(0), L)
        for h in range(hg):
            sl = slice(h * HEAD_DIM, (h + 1) * HEAD_DIM)
            qw = _window(qp, qm, qn, sl)
            dow = _window(dop, dom, don, sl).astype(BF16)
            s = lax.dot_general(qw, k_ref[:, sl], _NT, preferred_element_type=F32) * scale + b_ref[case, h]
            p = jnp.exp(s - _window(lzp, lzm, lzn, sl)[:, 0:1])
            dp = lax.dot_general(dow, v_ref[:, sl], _NT, preferred_element_type=F32)
            ds_b = (p * (dp + _window(rtp, rtm, rtn, sl)[:, 0:1]) * scale).astype(BF16)
            dk_ref[:, sl] = lax.dot_general(ds_b, qw, _TN, preferred_element_type=F32)
            dv_ref[:, sl] = lax.dot_general(p.astype(BF16), dow, _TN, preferred_element_type=F32)

    blk = pl.BlockSpec((tq, wg), lambda i: (i, 0))
    row = jax.ShapeDtypeStruct((S, wg), F32)
    return pl.pallas_call(
        body, name="battn_bwd_dkv_g%d" % g, grid=(S // tq,),
        in_specs=[_window_specs(S, wg, 1)[0], _window_specs(S, wg, 2)[0], *_window_specs(S, wg, 0),
                  *_window_specs(S, wg, 0), *_window_specs(S, wg, 0), *_window_specs(S, wg, 0),
                  pl.BlockSpec((None, 3, hg, kvl, tq), lambda i: (g, 0, 0, 0, 0))],
        out_specs=[blk, blk], out_shape=[row, row],
        compiler_params=_cp("arbitrary"))(flat, flat, flat, flat, flat, do, do, do, lz, lz, lz, rt, rt, rt, bias_t)


def _group_weights(lz_refs, h):
    z = [r[h] for r in lz_refs]
    mx = functools.reduce(jnp.maximum, z)
    e = [jnp.exp(v - mx) for v in z]
    inv = 1.0 / functools.reduce(lambda a, b: a + b, e)
    return [v * inv for v in e]


def _to_token_order(src_ref, dst_ref, dil):
    rows = src_ref.shape[1]
    for k in range(dst_ref.shape[0]):
        sl = slice(k * LANES, (k + 1) * LANES)
        if dil == 1:
            dst_ref[k] = src_ref[0, :, sl]
        else:
            for c in range(dil):
                dst_ref[k, pl.ds(c, rows, stride=dil), :] = src_ref[c, :, sl]


def _to_subsequence_order(src_ref, dst_ref, dil):
    rows = dst_ref.shape[1]
    for k in range(src_ref.shape[0]):
        sl = slice(k * LANES, (k + 1) * LANES)
        if dil == 1:
            dst_ref[0, :, sl] = src_ref[k]
        else:
            for c in range(dil):
                dst_ref[c, :, sl] = src_ref[k, pl.ds(c, rows, stride=dil), :]


def _sub_view(a, dil):
    S, w = a.shape
    return a.reshape(dil, S // dil, w)


def _sub_spec(dil, ts, w):
    return pl.BlockSpec((dil, ts // dil, w), lambda i: (0, i, 0))


def _combine_fwd(os_, lzs):
    G = len(os_)
    S, Wg = os_[0].shape
    hg = B_HEADS_PER_GROUP
    dils = [d for _, d in B_GROUPS]
    ts = _pick(S, (256, 128))

    def body(*refs):
        o_in, lz_in, y_ref = refs[:G], refs[G:2 * G], refs[2 * G]
        o_nat, lz_nat = refs[2 * G + 1:3 * G + 1], refs[3 * G + 1:4 * G + 1]
        for g in range(G):
            _to_token_order(o_in[g], o_nat[g], dils[g])
            _to_token_order(lz_in[g], lz_nat[g], dils[g])
        for h in range(hg):
            w = _group_weights(lz_nat, h)
            for g in range(G):
                y_ref[:, (g * hg + h) * HEAD_DIM:(g * hg + h + 1) * HEAD_DIM] = (w[g] * o_nat[g][h]).astype(BF16)

    specs = [_sub_spec(d, ts, Wg) for d in dils]
    return pl.pallas_call(
        body, name="combine_fwd", grid=(S // ts,), in_specs=specs + specs,
        out_specs=pl.BlockSpec((ts, G * Wg), lambda i: (i, 0)),
        out_shape=jax.ShapeDtypeStruct((S, G * Wg), BF16),
        scratch_shapes=[pltpu.VMEM((hg, ts, HEAD_DIM), F32)] * (2 * G),
        compiler_params=_cp("arbitrary"))(*[_sub_view(a, d) for a, d in zip(os_, dils)],
                                          *[_sub_view(a, d) for a, d in zip(lzs, dils)])


def _combine_bwd(dy, os_, lzs):
    G = len(os_)
    S, Wg = os_[0].shape
    hg = B_HEADS_PER_GROUP
    dils = [d for _, d in B_GROUPS]
    ts = _pick(S, (128,))

    def body(*refs):
        dy_ref, o_in, lz_in = refs[0], refs[1:1 + G], refs[1 + G:1 + 2 * G]
        do_out, dlz_out = refs[1 + 2 * G:1 + 3 * G], refs[1 + 3 * G:1 + 4 * G]
        scr = refs[1 + 4 * G:]
        o_nat, lz_nat, do_nat, dlz_nat = scr[:G], scr[G:2 * G], scr[2 * G:3 * G], scr[3 * G:4 * G]
        for g in range(G):
            _to_token_order(o_in[g], o_nat[g], dils[g])
            _to_token_order(lz_in[g], lz_nat[g], dils[g])
        for h in range(hg):
            w = _group_weights(lz_nat, h)
            dw = []
            for g in range(G):
                dyg = dy_ref[:, (g * hg + h) * HEAD_DIM:(g * hg + h + 1) * HEAD_DIM]
                dw.append(jnp.sum(dyg * o_nat[g][h], axis=-1, keepdims=True))
                do_nat[g][h] = w[g] * dyg
            tot = functools.reduce(lambda a, b: a + b, [w[g] * dw[g] for g in range(G)])
            for g in range(G):
                dlz_nat[g][h] = w[g] * (dw[g] - tot)
        for g in range(G):
            _to_subsequence_order(do_nat[g], do_out[g], dils[g])
            _to_subsequence_order(dlz_nat[g], dlz_out[g], dils[g])

    specs = [_sub_spec(d, ts, Wg) for d in dils]
    outs = pl.pallas_call(
        body, name="combine_bwd", grid=(S // ts,),
        in_specs=[pl.BlockSpec((ts, G * Wg), lambda i: (i, 0))] + specs + specs,
        out_specs=specs + specs,
        out_shape=[jax.ShapeDtypeStruct((d, S // d, Wg), F32) for d in dils] * 2,
        scratch_shapes=[pltpu.VMEM((hg, ts, HEAD_DIM), F32)] * (4 * G),
        compiler_params=_cp("arbitrary"))(dy, *[_sub_view(a, d) for a, d in zip(os_, dils)],
                                          *[_sub_view(a, d) for a, d in zip(lzs, dils)])
    flat = [a.reshape(S, Wg) for a in outs]
    return flat[:G], flat[G:]


def _concat_cast(parts, dils):
    S = parts[0].shape[0]
    widths = [p.shape[1] for p in parts]
    n = len(parts)
    ts = _pick(S, (256, 128))

    def body(*refs):
        o_ref, nat = refs[n], refs[n + 1]
        off = 0
        for r, w, d in zip(refs, widths, dils):
            _to_token_order(r, nat, d)
            for k in range(w // LANES):
                o_ref[:, off + k * LANES:off + (k + 1) * LANES] = nat[k].astype(BF16)
            off += w

    assert len(set(widths)) == 1
    return pl.pallas_call(
        body, name="concat_cast", grid=(S // ts,),
        in_specs=[_sub_spec(d, ts, w) for w, d in zip(widths, dils)],
        out_specs=pl.BlockSpec((ts, sum(widths)), lambda i: (i, 0)),
        out_shape=jax.ShapeDtypeStruct((S, sum(widths)), BF16),
        scratch_shapes=[pltpu.VMEM((widths[0] // LANES, ts, LANES), F32)],
        compiler_params=_cp("arbitrary"))(*[_sub_view(p, d) for p, d in zip(parts, dils)])


def _ffn_specs(S, dff, cq, ts, tc, layer, order):
    nfc = dff // tc
    nps = cq // tc
    hb = ts // SUBLANES
    nrow8 = S // SUBLANES

    def u_main(half):
        return pl.BlockSpec((ts, tc), lambda *g: (order(*g)[0], order(*g)[1] % nfc + half * nfc))

    def u_prev(half):
        return pl.BlockSpec((SUBLANES, tc), lambda *g: (jnp.maximum(order(*g)[0] * hb - 1, 0),
                                                         order(*g)[1] % nfc + half * nfc))

    def u_next(half):
        return pl.BlockSpec((SUBLANES, tc), lambda *g: (jnp.minimum((order(*g)[0] + 1) * hb, nrow8 - 1),
                                                         order(*g)[1] % nfc + half * nfc))

    def cw(half):
        def im(*g):
            jj = order(*g)[1] % nfc + half * nfc
            return (layer, jj // nps, 0, jj % nps)
        return pl.BlockSpec((None, None, 3, tc), im)

    def cb(half):
        return pl.BlockSpec((None, 1, tc), lambda *g: (layer, 0, order(*g)[1] % nfc + half * nfc))

    return nfc, u_main, u_prev, u_next, cw, cb


def _ffn_act_fwd(u, cw_full, cb3, layer):
    S, two_dff = u.shape
    dff = two_dff // 2
    cq = cw_full.shape[3]
    ts = _pick(S, (1024, 512, 256, 128, 64, 32, 16))
    tc = _pick(cq, (256, 128))
    order = lambda j, i: (i, j)
    nfc, u_main, u_prev, u_next, cw, cb = _ffn_specs(S, dff, cq, ts, tc, layer, order)
    nrow = S // ts

    def body(ug, ugp, ugn, uv, uvp, uvn, wg, wv, bg, bv, a_ref):
        i = pl.program_id(1)
        row = lax.broadcasted_iota(jnp.int32, (ts, tc), 0)

        def conv(x_ref, p_ref, n_ref, w_ref, b_ref):
            x = x_ref[...]
            prev = jnp.where(i > 0, p_ref[SUBLANES - 1:SUBLANES, :], 0.0)
            nxt = jnp.where(i < nrow - 1, n_ref[0:1, :], 0.0)
            xm = jnp.where(row == 0, prev, pltpu.roll(x, 1, 0))
            xp = jnp.where(row == ts - 1, nxt, pltpu.roll(x, ts - 1, 0))
            return w_ref[0:1, :] * xm + w_ref[1:2, :] * x + w_ref[2:3, :] * xp + b_ref[...]

        gc = conv(ug, ugp, ugn, wg, bg)
        vc = conv(uv, uvp, uvn, wv, bv)
        a_ref[...] = (gc * (1.0 / (1.0 + jnp.exp(-gc))) * vc).astype(BF16)

    return pl.pallas_call(
        body, name="ffn_act_fwd", grid=(nfc, nrow),
        in_specs=[u_main(0), u_prev(0), u_next(0), u_main(1), u_prev(1), u_next(1), cw(0), cw(1), cb(0), cb(1)],
        out_specs=pl.BlockSpec((ts, tc), lambda j, i: (i, j)),
        out_shape=jax.ShapeDtypeStruct((S, dff), BF16),
        compiler_params=_cp("arbitrary", "arbitrary"))(u, u, u, u, u, u, cw_full, cw_full, cb3, cb3)


def _ffn_act_bwd(u, da, cw_full, cb3, layer):
    S, two_dff = u.shape
    dff = two_dff // 2
    cq = cw_full.shape[3]
    ts = _pick(S, (1024, 512, 256, 128, 64, 32, 16))
    tc = _pick(cq, (256, 128))
    order = lambda j, i: (i, j)
    nfc, u_main, u_prev, u_next, cw, cb = _ffn_specs(S, dff, cq, ts, tc, layer, order)
    nrow = S // ts
    hb = ts // SUBLANES
    te = ts + 2 * SUBLANES
    da_main = pl.BlockSpec((ts, tc), lambda j, i: (i, j))
    da_prev = pl.BlockSpec((SUBLANES, tc), lambda j, i: (jnp.maximum(i * hb - 1, 0), j))
    da_next = pl.BlockSpec((SUBLANES, tc), lambda j, i: (jnp.minimum((i + 1) * hb, S // SUBLANES - 1), j))
    main = slice(SUBLANES, SUBLANES + ts)

    def body(ug, ugp, ugn, uv, uvp, uvn, dam, dap, dan, wg, wv, bg, bv, dug_ref, duv_ref, accg_ref, accv_ref):
        i = pl.program_id(1)

        @pl.when(i == 0)
        def _():
            accg_ref[...] = jnp.zeros_like(accg_ref)
            accv_ref[...] = jnp.zeros_like(accv_ref)

        def ext(m, p, n):
            return jnp.concatenate([jnp.where(i > 0, p[...], 0.0), m[...], jnp.where(i < nrow - 1, n[...], 0.0)], axis=0)

        def shift(x):
            return pltpu.roll(x, 1, 0), pltpu.roll(x, te - 1, 0)

        xg, xv, dae = ext(ug, ugp, ugn), ext(uv, uvp, uvn), ext(dam, dap, dan)
        xgm, xgp = shift(xg)
        xvm, xvp = shift(xv)
        gc = wg[0:1, :] * xgm + wg[1:2, :] * xg + wg[2:3, :] * xgp + bg[...]
        vc = wv[0:1, :] * xvm + wv[1:2, :] * xv + wv[2:3, :] * xvp + bv[...]
        sig = 1.0 / (1.0 + jnp.exp(-gc))
        silu = gc * sig
        dcg = dae * vc * (sig * (1.0 + gc * (1.0 - sig)))
        dcv = dae * silu

        def finish(dc, x, xm, xp, w_ref, du_ref, acc_ref):
            dm, dp = shift(dc)
            du = w_ref[0:1, :] * dp + w_ref[1:2, :] * dc + w_ref[2:3, :] * dm
            du_ref[...] = du[main, :].astype(BF16)
            dcm = dc[main, :]
            acc_ref[0:1, :] += jnp.sum(dcm * xm[main, :], axis=0, keepdims=True)
            acc_ref[1:2, :] += jnp.sum(dcm * x[main, :], axis=0, keepdims=True)
            acc_ref[2:3, :] += jnp.sum(dcm * xp[main, :], axis=0, keepdims=True)
            acc_ref[3:4, :] += jnp.sum(dcm, axis=0, keepdims=True)

        finish(dcg, xg, xgm, xgp, wg, dug_ref, accg_ref)
        finish(dcv, xv, xvm, xvp, wv, duv_ref, accv_ref)

    blk = pl.BlockSpec((ts, tc), lambda j, i: (i, j))
    acc = pl.BlockSpec((SUBLANES, tc), lambda j, i: (0, j))
    dug, duv, accg, accv = pl.pallas_call(
        body, name="ffn_act_bwd", grid=(nfc, nrow),
        in_specs=[u_main(0), u_prev(0), u_next(0), u_main(1), u_prev(1), u_next(1), da_main, da_prev, da_next,
                  cw(0), cw(1), cb(0), cb(1)],
        out_specs=[blk, blk, acc, acc],
        out_shape=[jax.ShapeDtypeStruct((S, dff), BF16)] * 2 + [jax.ShapeDtypeStruct((SUBLANES, dff), F32)] * 2,
        compiler_params=_cp("arbitrary", "arbitrary"))(u, u, u, u, u, u, da, da, da, cw_full, cw_full, cb3, cb3)
    return (dug, duv), jnp.concatenate([accg, accv], axis=1)


def _my_chip():
    return 2 * lax.axis_index("x") + lax.axis_index("y")


def _into_full(w, layer, dtype):
    L, a, b = w.shape
    tr = _pick(a, (512, 256, 128, 64, 32, 16, 8))

    def body(w_ref, o_ref):
        o_ref[...] = w_ref[...].astype(dtype)

    return pl.pallas_call(
        body, name="into_full", grid=(a // tr,),
        in_specs=[pl.BlockSpec((None, tr, b), lambda i: (layer, i, 0))],
        out_specs=pl.BlockSpec((None, None, tr, b), lambda i: (0, _my_chip(), i, 0)),
        out_shape=jax.ShapeDtypeStruct((1, N_CHIPS, a, b), dtype),
        compiler_params=_cp("arbitrary"))(w)


def _adam_math(w, g, m, v):
    m = ADAM_B1 * m + (1.0 - ADAM_B1) * g
    v = ADAM_B2 * v + (1.0 - ADAM_B2) * (g * g)
    m_hat = m / (1.0 - ADAM_B1 ** ADAM_STEP)
    v_hat = v / (1.0 - ADAM_B2 ** ADAM_STEP)
    delta = -ADAM_LR * (m_hat / (jnp.sqrt(v_hat) + ADAM_EPS) + ADAM_WD * w)
    return delta, m, v


def _adamw(w, g, m, v, keep_g=False):
    R, C = w.shape
    tr = _pick(R, (128, 64, 32, 16, 8)) if R % SUBLANES == 0 and C % LANES == 0 else R
    n_out = 4 if keep_g else 3

    def body(w_ref, g_ref, m_ref, v_ref, d_ref, nm_ref, nv_ref, *rest):
        g = g_ref[...]
        d, nm, nv = _adam_math(w_ref[...], g, m_ref[...], v_ref[...])
        d_ref[...] = d
        nm_ref[...] = nm
        nv_ref[...] = nv
        if keep_g:
            rest[0][...] = g

    spec = pl.BlockSpec((tr, C), lambda i: (i, 0))
    return pl.pallas_call(
        body, name="adamw", grid=(R // tr,), in_specs=[spec] * 4, out_specs=[spec] * n_out,
        out_shape=[jax.ShapeDtypeStruct((R, C), F32)] * n_out, compiler_params=_cp("arbitrary"))(w, g, m, v)


ANY = pl.BlockSpec(memory_space=pl.ANY)


def _position():
    x, y, c = lax.axis_index("x"), lax.axis_index("y"), lax.axis_index("c")
    chips = [(1 - x, y), (x, 1 - y), (1 - x, 1 - y)]
    return x, y, c, chips


HBM = pl.BlockSpec(memory_space=pltpu.HBM)
SEM = pl.BlockSpec(memory_space=pltpu.SEMAPHORE)
EFFECT = pltpu.SideEffectType.DATAFLOW_SIDE_EFFECTING


def _in_hbm(a):
    return pltpu.with_memory_space_constraint(a, pltpu.HBM)


def _shard_half(buf, shape, chip, half):
    _, _, a, b = shape
    p = 2 * chip[0] + chip[1]
    if a % (4 * SUBLANES) == 0:
        return buf.at[0, p, pl.ds(half * (a // 2), a // 2)]
    return buf.at[0, p, :, pl.ds(half * (b // 2), b // 2)]


def _gather_copy(buf, shape, chip, half, to, send, recv, k):
    part = _shard_half(buf, shape, chip, half)
    return pltpu.make_async_remote_copy(src_ref=part, dst_ref=part, send_sem=send.at[k], recv_sem=recv.at[k],
                                        device_id=to, device_id_type=MESH)


def _gather_hop(buf, shape, hop, j, incoming, send, recv, k):
    x, y, c, chips = _position()
    if hop == "chips":
        chip, half, to = (chips[j] if incoming else (x, y)), c, (*chips[j], c)
    else:
        chip, half, to = chips[j], (1 - c if incoming else c), (x, y, 1 - c)
    return _gather_copy(buf, shape, chip, half, to, send, recv, k)


def _gather_start(fulls, name, hop, after=()):
    n = len(fulls)
    na = len(after)

    def body(*refs):
        send, recv = refs[n + na], refs[n + na + 1]
        buf, token = refs[n + na + 2:2 * n + na + 2], refs[2 * n + na + 2]
        for t in range(n):
            for j in range(3):
                _gather_hop(buf[t], fulls[t].shape, hop, j, False, send, recv, 3 * t + j).start()
        token[...] = jnp.zeros_like(token)

    outs = pl.pallas_call(
        body, name=name, in_specs=[HBM] * n + [ANY] * na,
        out_specs=[SEM, SEM] + [HBM] * n + [pl.BlockSpec(memory_space=pltpu.VMEM)],
        out_shape=[pltpu.SemaphoreType.DMA((3 * n,)), pltpu.SemaphoreType.DMA((3 * n,))]
        + [pltpu.HBM(f.shape, f.dtype) for f in fulls] + [jax.ShapeDtypeStruct((SUBLANES, LANES), F32)],
        input_output_aliases={t: 2 + t for t in range(n)},
        compiler_params=pltpu.CompilerParams(has_side_effects=EFFECT))(*[_in_hbm(f) for f in fulls], *after)
    return outs[0], outs[1], list(outs[2:2 + n]), outs[2 + n]


def _gather_wait(send, recv, fulls, after, name, hop):
    n = len(fulls)

    def body(*refs):
        buf, send_ref, recv_ref = refs[:n], refs[n], refs[n + 1]
        for t in range(n):
            for j in range(3):
                _gather_hop(buf[t], fulls[t].shape, hop, j, False, send_ref, recv_ref, 3 * t + j).wait_send()
                _gather_hop(buf[t], fulls[t].shape, hop, j, True, send_ref, recv_ref, 3 * t + j).wait_recv()

    outs = pl.pallas_call(
        body, name=name, in_specs=[HBM] * n + [SEM, SEM] + [ANY] * len(after), out_specs=[HBM] * n,
        out_shape=[pltpu.HBM(f.shape, f.dtype) for f in fulls],
        input_output_aliases={t: t for t in range(n)},
        compiler_params=pltpu.CompilerParams(has_side_effects=EFFECT))(*fulls, send, recv, *after)
    return list(outs)


def _allreduce_small(part):
    M, C = part.shape
    n_dev = 2 * N_CHIPS

    def body(x_ref, sum_ref, all_ref, send, recv, local):
        x, y, c, chips = _position()
        me, sib = (x, y, c), (x, y, 1 - c)

        def rows(px, py, pc):
            return all_ref.at[pl.ds((4 * px + 2 * py + pc) * M, M), :]

        def copy(k, block, to, src=None):
            return pltpu.make_async_remote_copy(
                src_ref=rows(*block) if src is None else src, dst_ref=rows(*block),
                send_sem=send.at[k], recv_sem=recv.at[k], device_id=to, device_id_type=MESH)

        mine = pltpu.make_async_copy(x_ref, rows(*me), local)
        mine.start()
        first = [copy(0, me, sib, src=x_ref)] + [copy(1 + j, me, (*chip, c), src=x_ref) for j, chip in enumerate(chips)]
        for cp in first:
            cp.start()
        passed = [copy(4 + j, (*chip, c), sib) for j, chip in enumerate(chips)]
        for j, chip in enumerate(chips):
            copy(1 + j, (*chip, c), me).wait_recv()
            passed[j].start()
        copy(0, sib, me).wait_recv()
        for j, chip in enumerate(chips):
            copy(4 + j, (*chip, 1 - c), me).wait_recv()
        for cp in first + passed:
            cp.wait_send()
        mine.wait()
        acc = all_ref[0:M, :]
        for d in range(1, n_dev):
            acc = acc + all_ref[d * M:(d + 1) * M, :]
        sum_ref[...] = acc

    vm = pl.BlockSpec(memory_space=pltpu.VMEM)
    return pl.pallas_call(
        body, name="allreduce_small", in_specs=[vm], out_specs=[vm],
        out_shape=[jax.ShapeDtypeStruct((M, C), F32)],
        scratch_shapes=[pltpu.VMEM((n_dev * M, C), F32), pltpu.SemaphoreType.DMA((7,)),
                        pltpu.SemaphoreType.DMA((7,)), pltpu.SemaphoreType.DMA],
        compiler_params=pltpu.CompilerParams(vmem_limit_bytes=VMEM_LIMIT))(part)[0]


N_PEERS = 2 * N_CHIPS - 1


def _peers():
    x, y, c, chips = _position()
    return [(x, y, 1 - c)] + [(*ch, c) for ch in chips] + [(*ch, 1 - c) for ch in chips]


def _reduce_copy(src, dst, peers, send, recv, t, r):
    px, py, pc = peers[r]
    return pltpu.make_async_remote_copy(
        src_ref=src.at[2 * px + py, pc], dst_ref=dst.at[r], send_sem=send.at[N_PEERS * t + r],
        recv_sem=recv.at[N_PEERS * t + r], device_id=peers[r], device_id_type=MESH)


def _reduce_start(grads, name, after=()):
    n = len(grads)
    na = len(after)
    lands = [lax.empty((N_PEERS,) + g.shape[2:], BF16) for g in grads]

    def body(*refs):
        send, recv = refs[2 * n + na], refs[2 * n + na + 1]
        src, dst = refs[2 * n + na + 2:3 * n + na + 2], refs[3 * n + na + 2:4 * n + na + 2]
        token = refs[4 * n + na + 2]
        peers = _peers()
        for t in range(n):
            for r in range(N_PEERS):
                _reduce_copy(src[t], dst[t], peers, send, recv, t, r).start()
        token[...] = jnp.zeros_like(token)

    outs = pl.pallas_call(
        body, name=name, in_specs=[HBM] * (2 * n) + [ANY] * na,
        out_specs=[SEM, SEM] + [HBM] * (2 * n) + [pl.BlockSpec(memory_space=pltpu.VMEM)],
        out_shape=[pltpu.SemaphoreType.DMA((N_PEERS * n,)), pltpu.SemaphoreType.DMA((N_PEERS * n,))]
        + [pltpu.HBM(a.shape, a.dtype) for a in grads + lands] + [jax.ShapeDtypeStruct((SUBLANES, LANES), F32)],
        input_output_aliases={t: 2 + t for t in range(2 * n)},
        compiler_params=pltpu.CompilerParams(has_side_effects=EFFECT))(*[_in_hbm(a) for a in grads + lands], *after)
    return outs[0], outs[1], list(outs[2:2 + n]), list(outs[2 + n:2 + 2 * n]), outs[2 + 2 * n]


def _reduce_wait(send, recv, grads, lands, after, name):
    n = len(grads)

    def body(*refs):
        src, dst, send_ref, recv_ref = refs[:n], refs[n:2 * n], refs[2 * n], refs[2 * n + 1]
        peers = _peers()
        for t in range(n):
            for r in range(N_PEERS):
                cp = _reduce_copy(src[t], dst[t], peers, send_ref, recv_ref, t, r)
                cp.wait_send()
                cp.wait_recv()

    outs = pl.pallas_call(
        body, name=name, in_specs=[HBM] * (2 * n) + [SEM, SEM] + [ANY] * len(after), out_specs=[HBM] * (2 * n),
        out_shape=[pltpu.HBM(a.shape, a.dtype) for a in grads + lands],
        input_output_aliases={t: t for t in range(2 * n)},
        compiler_params=pltpu.CompilerParams(has_side_effects=EFFECT))(*grads, *lands, send, recv, *after)
    return list(outs[:n]), list(outs[n:])


def _add_pieces(grad, land, stack, layer):
    _, _, R, C = grad.shape
    tr = _pick(R, (256, 128, 64, 32, 16))

    def body(g_ref, r_ref, stack_ref, o_ref):
        acc = g_ref[...].astype(F32)
        for r in range(N_PEERS):
            acc = acc + r_ref[r].astype(F32)
        o_ref[...] = acc

    return pl.pallas_call(
        body, name="add_pieces", grid=(R // tr,),
        in_specs=[pl.BlockSpec((None, None, tr, C), lambda i: (_my_chip(), lax.axis_index("c"), i, 0)),
                  pl.BlockSpec((N_PEERS, tr, C), lambda i: (0, i, 0)),
                  ANY],
        out_specs=pl.BlockSpec((None, None, tr, C), lambda i: (layer, lax.axis_index("c"), i, 0)),
        out_shape=jax.ShapeDtypeStruct(stack.shape, F32), input_output_aliases={2: 0},
        compiler_params=_cp("arbitrary"))(grad, land, stack)


def _ag_sibling(stacks):
    n = len(stacks)
    offs = np.cumsum([0] + [s.shape[0] for s in stacks])

    def body(*refs):
        buf, send, recv = refs[n:2 * n], refs[2 * n], refs[2 * n + 1]
        x, y, c, _ = _position()

        def copy(t, l, half):
            part = buf[t].at[l, half]
            return pltpu.make_async_remote_copy(
                src_ref=part, dst_ref=part, send_sem=send.at[int(offs[t]) + l], recv_sem=recv.at[int(offs[t]) + l],
                device_id=(x, y, 1 - c), device_id_type=MESH)

        cps = [copy(t, l, c) for t in range(n) for l in range(stacks[t].shape[0])]
        for cp in cps:
            cp.start()
        for t in range(n):
            for l in range(stacks[t].shape[0]):
                copy(t, l, 1 - c).wait_recv()
        for cp in cps:
            cp.wait_send()

    return pl.pallas_call(
        body, name="ag_sibling", in_specs=[ANY] * n, out_specs=[ANY] * n,
        out_shape=[jax.ShapeDtypeStruct(s.shape, F32) for s in stacks],
        input_output_aliases={t: t for t in range(n)},
        scratch_shapes=[pltpu.SemaphoreType.DMA((int(offs[-1]),)), pltpu.SemaphoreType.DMA((int(offs[-1]),))])(*stacks)


def _split8(dw, blocked):
    if blocked:
        p, k, nq = dw.shape
        return dw.reshape(p, 2, k // 2, nq)
    k, n = dw.shape
    return dw.reshape(N_CHIPS, 2, k // (2 * N_CHIPS), n)


def kernel(x, a_w_qkv, a_w_o, a_q_gain, a_k_gain, b_w_qkv, b_w_o, rel_bias, mix_norm, ffn_norm, w_up, conv_w, conv_b, w_down, final_norm, loss_target, m_a_w_qkv, m_a_w_o, m_a_q_gain, m_a_k_gain, m_b_w_qkv, m_b_w_o, m_rel_bias, m_mix_norm, m_ffn_norm, m_w_up, m_conv_w, m_conv_b, m_w_down, m_final_norm, v_a_w_qkv, v_a_w_o, v_a_q_gain, v_a_k_gain, v_b_w_qkv, v_b_w_o, v_rel_bias, v_mix_norm, v_ffn_norm, v_w_up, v_conv_w, v_conv_b, v_w_down, v_final_norm):
    S, D = x.shape[1], x.shape[2]
    h = x.reshape(S, D)
    target = loss_target.reshape(S, D)
    hg = B_HEADS_PER_GROUP
    G = len(B_GROUPS)
    n_a, n_b = a_w_qkv.shape[0], b_w_qkv.shape[0]
    depth = w_up.shape[0]
    cx, cy = lax.axis_index("x"), lax.axis_index("y")

    big = dict(a_w_qkv=a_w_qkv, a_w_o=a_w_o, b_w_qkv=b_w_qkv, b_w_o=b_w_o, w_up=w_up, w_down=w_down)
    blocked = dict(a_w_qkv=True, a_w_o=False, b_w_qkv=True, b_w_o=False, w_up=True, w_down=False)
    names = list(big)
    srcs = dict(big, conv_w=conv_w)
    started = []
    for i in range(depth):
        mix = [("a_w_qkv", i // 2), ("a_w_o", i // 2)] if i % 2 == 0 else [("b_w_qkv", i // 2), ("b_w_o", i // 2)]
        rest = [("w_up", i), ("conv_w", i), ("w_down", i)]
        stages = [mix[:1], mix[1:], rest] if i == 0 else [mix + rest]
        started.append([])
        for s, keys in enumerate(stages):
            bufs = [_into_full(srcs[k], l, F32 if k == "conv_w" else BF16) for k, l in keys]
            started[i].append((keys,) + _gather_start(bufs, "gather_start_%d_%d" % (i, s), "chips"))
    cb3 = conv_b.reshape(depth, 1, conv_b.shape[1])

    cos, sin = _rope_tables(S)
    buckets = jnp.asarray(_bucket_tables(False))
    bias = _bias_build(rel_bias, buckets)
    bias_t = _bias_build(rel_bias, jnp.asarray(_bucket_tables(True)))

    saved = []
    passing = {}

    def land(i, s, after):
        keys, send, recv, bufs, _ = started[i][s]
        bufs = _gather_wait(send, recv, bufs, after, "gather_wait_%d_%d" % (i, s), "chips")
        send, recv, bufs, token = _gather_start(bufs, "pass_start_%d_%d" % (i, s), "sibling")
        passing[i, s] = (keys, send, recv, bufs)
        return token

    def arrive(i, s, after, wl):
        keys, send, recv, bufs = passing.pop((i, s))
        bufs = _gather_wait(send, recv, bufs, after, "pass_wait_%d_%d" % (i, s), "sibling")
        for (k, _), buf in zip(keys, bufs):
            _, _, a, b = buf.shape
            wl[k] = buf if k == "conv_w" or blocked[k] else buf.reshape(1, N_CHIPS * a, b)

    first = [land(0, 0, [h])]
    for i in range(depth):
        j = i // 2
        wl = {}
        arrive(i, 0, [h], wl)
        sv = dict(h0=h, w=wl)
        hn = _rms_fwd(h, mix_norm[i:i + 1], after=[st[4] for layer in started for st in layer] + first if i == 0 else ())
        sv["hn"] = hn
        if i % 2 == 0:
            qkv = _mm_nn(hn, wl["a_w_qkv"], 0, blocked=True, name="a_qkv")
            qkvh = _prep_a_fwd(qkv, cos, sin, a_q_gain[j:j + 1], a_k_gain[j:j + 1])
            staged = len(started[i]) > 1
            o, lse = _flash_a_fwd(qkvh, after=[land(i, 1, [qkvh])] if staged else ())
            tok = ()
            if staged:
                arrive(i, 1, [o], wl)
                tok = [land(i, 2, [o])]
            sv.update(qkv=qkv, qkvh=qkvh, o=o, lse=lse)
            h = _mm_nn(o, wl["a_w_o"], 0, blocked=False, res=h, name="a_out", after=tok)
        else:
            qkvp = [_mm_nn_perm(hn, wl["b_w_qkv"], g) for g in range(G)]
            os_, lzs = [], []
            for g in range(G):
                o_g, lz_g = _battn_fwd(qkvp[g], bias, g)
                os_.append(o_g)
                lzs.append(lz_g)
            y = _combine_fwd(os_, lzs)
            sv.update(qkvp=qkvp, os=os_, lzs=lzs, y=y)
            h = _mm_nn(y, wl["b_w_o"], 0, blocked=False, res=h, name="b_out")
        sv["h1"] = h
        hf = _rms_fwd(h, ffn_norm[i:i + 1])
        if len(started[i]) > 2:
            arrive(i, 2, [hf], wl)
        u = _mm_nn(hf, wl["w_up"], 0, blocked=True, name="ffn_up")
        act = _ffn_act_fwd(u, wl["conv_w"], cb3[i:i + 1], 0)
        sv.update(hf=hf, u=u, act=act)
        nxt = [land(i + 1, 0, [act])] if i + 1 < depth else ()
        h = _mm_nn(act, wl["w_down"], 0, blocked=False, res=h, name="ffn_down", after=nxt)
        saved.append(sv)

    loss_blk, dh, dh_b, dg_final = _final_loss(h, final_norm.reshape(1, D), target)

    dws = {k: [None] * big[k].shape[0] for k in names}
    d_mix, d_ffn, d_convw, d_convb = [None] * depth, [None] * depth, [None] * depth, [None] * depth
    d_gq, d_gk = [None] * n_a, [None] * n_a
    dbias_list = []
    pending = []

    def start_reduce(keys, tag, after=()):
        pieces = [_split8(dws[k][l], blocked[k]) for k, l in keys]
        send, recv, pieces, lands, token = _reduce_start(pieces, "reduce_start_" + tag, after)
        pending.append((keys, send, recv, pieces, lands, tag))
        return (token,)

    tok = ()
    for i in reversed(range(depth)):
        j = i // 2
        sv = saved[i]
        wl = sv["w"]
        da = _mm_nt(dh_b, wl["w_down"], 0, blocked=False, name="ffn_down_dx", after=tok)
        dws["w_down"][i] = _mm_tn(sv["act"], dh_b, blocked=False, name="ffn_down_dw")
        du, dconv = _ffn_act_bwd(sv["u"], da, wl["conv_w"], cb3[i:i + 1], 0)
        d_convw[i], d_convb[i] = dconv[0:3], dconv[3]
        dhf = _mm_nt(du, wl["w_up"], 0, blocked=True, name="ffn_up_dx")
        dws["w_up"][i] = _mm_tn(sv["hf"], du, blocked=True, name="ffn_up_dw")
        dh, dh_b, dg = _rms_bwd(dhf, sv["h1"], ffn_norm[i:i + 1], dh)
        d_ffn[i] = dg[0]
        tok = start_reduce([("w_down", i), ("w_up", i)], "ffn%d" % i)
        if i % 2 == 0:
            do = _mm_nt(dh_b, wl["a_w_o"], 0, blocked=False, name="a_out_dx", after=tok)
            dws["a_w_o"][j] = _mm_tn(sv["o"], dh_b, blocked=False, name="a_out_dw")
            dq, dk, dv = _flash_a_bwd(sv["qkvh"], do, sv["o"], sv["lse"])
            dqkv, dgain = _prep_a_bwd(dq, dk, dv, sv["qkv"], cos, sin, a_q_gain[j:j + 1], a_k_gain[j:j + 1])
            d_gq[j], d_gk[j] = dgain[0], dgain[1]
            dhn = _mm_nt(dqkv, wl["a_w_qkv"], 0, blocked=True, name="a_qkv_dx")
            dws["a_w_qkv"][j] = _mm_tn(sv["hn"], dqkv, blocked=True, name="a_qkv_dw")
            mix_keys = [("a_w_o", j), ("a_w_qkv", j)]
        else:
            dy = _mm_nt(dh_b, wl["b_w_o"], 0, blocked=False, name="b_out_dx", after=tok)
            dws["b_w_o"][j] = _mm_tn(sv["y"], dh_b, blocked=False, name="b_out_dw")
            dos, dlzs = _combine_bwd(dy, sv["os"], sv["lzs"])
            parts = []
            for g in range(G):
                dq, rt, db = _battn_bwd_dq(sv["qkvp"][g], bias, dos[g], sv["os"][g], sv["lzs"][g], dlzs[g], g)
                dk, dv = _battn_bwd_dkv(sv["qkvp"][g], bias_t, dos[g], sv["lzs"][g], rt, g)
                parts += [dq, dk, dv]
                dbias_list.append((g, db))
            dqkv = _concat_cast(parts, [d for _, d in B_GROUPS for _ in range(3)])
            dhn = _mm_nt(dqkv, wl["b_w_qkv"], 0, blocked=True, name="b_qkv_dx")
            dws["b_w_qkv"][j] = _mm_tn(sv["hn"], dqkv, blocked=True, name="b_qkv_dw")
            mix_keys = [("b_w_o", j), ("b_w_qkv", j)]
        dh, dh_b, dg = _rms_bwd(dhn, sv["h0"], mix_norm[i:i + 1], dh)
        d_mix[i] = dg[0]
        if i > 0:
            tok = start_reduce(mix_keys, "mix%d" % i)
    grad_x = dh.reshape(x.shape)

    dbias_layers = [jnp.stack([db for g2, db in dbias_list[l * G:(l + 1) * G]]) for l in range(n_b)]
    d_rel = _bias_reduce(dbias_layers, buckets)[:, :G * hg]

    small = [jnp.stack(d_gq), jnp.stack(d_gk), d_rel, jnp.stack(d_mix), jnp.stack(d_ffn), jnp.stack(d_convw),
             jnp.stack(d_convb), dg_final[0]]
    sizes = [int(np.prod(s.shape)) for s in small]
    flat = jnp.concatenate([s.reshape(-1) for s in small])
    rows = -(-flat.shape[0] // (LANES * SUBLANES)) * SUBLANES
    flat = jnp.pad(flat, (0, rows * LANES - flat.shape[0])).reshape(rows, LANES)
    tot = _allreduce_small(flat)
    start_reduce(mix_keys, "mix0", after=[tot])
    tot = tot.reshape(-1)
    offs = np.cumsum([0] + sizes)
    g_gq, g_gk, g_rel, g_mix, g_ffn, g_convw_full, g_convb, g_final = [
        tot[offs[k]:offs[k + 1]].reshape(small[k].shape) for k in range(len(small))]
    cq = conv_w.shape[2]
    g_convw = lax.dynamic_slice_in_dim(g_convw_full, (2 * cx + cy) * cq, cq, axis=2)

    grads = dict(a_q_gain=g_gq, a_k_gain=g_gk, rel_bias=g_rel, mix_norm=g_mix, ffn_norm=g_ffn,
                 conv_w=g_convw, conv_b=g_convb, final_norm=g_final)
    weights = dict(a_w_qkv=a_w_qkv, a_w_o=a_w_o, a_q_gain=a_q_gain, a_k_gain=a_k_gain, b_w_qkv=b_w_qkv, b_w_o=b_w_o,
                   rel_bias=rel_bias, mix_norm=mix_norm, ffn_norm=ffn_norm, w_up=w_up, conv_w=conv_w, conv_b=conv_b,
                   w_down=w_down, final_norm=final_norm)
    ms = dict(a_w_qkv=m_a_w_qkv, a_w_o=m_a_w_o, a_q_gain=m_a_q_gain, a_k_gain=m_a_k_gain, b_w_qkv=m_b_w_qkv,
              b_w_o=m_b_w_o, rel_bias=m_rel_bias, mix_norm=m_mix_norm, ffn_norm=m_ffn_norm, w_up=m_w_up,
              conv_w=m_conv_w, conv_b=m_conv_b, w_down=m_w_down, final_norm=m_final_norm)
    vs = dict(a_w_qkv=v_a_w_qkv, a_w_o=v_a_w_o, a_q_gain=v_a_q_gain, a_k_gain=v_a_k_gain, b_w_qkv=v_b_w_qkv,
              b_w_o=v_b_w_o, rel_bias=v_rel_bias, mix_norm=v_mix_norm, ffn_norm=v_ffn_norm, w_up=v_w_up,
              conv_w=v_conv_w, conv_b=v_conv_b, w_down=v_w_down, final_norm=v_final_norm)
    deltas, new_m, new_v, stacks = {}, {}, {}, {}

    def update(k):
        w = weights[k]
        two_d = (-1, w.shape[-1])
        outs = _adamw(w.reshape(two_d), grads[k].reshape(two_d), ms[k].reshape(two_d), vs[k].reshape(two_d),
                      keep_g=k in big)
        deltas[k], new_m[k], new_v[k] = [a.reshape(w.shape) for a in outs[:3]]
        if k in big:
            grads[k] = outs[3]
        return outs[2]

    def collect(items, after):
        for keys, send, recv, pieces, lands, tag in items:
            pieces, lands = _reduce_wait(send, recv, pieces, lands, after, "reduce_wait_" + tag)
            for (k, l), p, land in zip(keys, pieces, lands):
                if k not in stacks:
                    stacks[k] = lax.empty((big[k].shape[0], 2) + p.shape[2:], F32)
                stacks[k] = _add_pieces(p, land, stacks[k], l)

    def share(ks):
        for k, gs in zip(ks, _ag_sibling([stacks[k] for k in ks])):
            grads[k] = gs.reshape(big[k].shape)

    late = [k for k in names if k in {kk for kk, _ in pending[-1][0]}]
    collect(pending[:-1], [dh])
    share([k for k in names if k not in late])
    done = [update(k) for k in weights if k not in late]
    collect(pending[-1:], done)
    share(late)
    for k in late:
        update(k)

    loss = lax.psum(loss_blk[0, 0], ("x", "y", "c"))
    keys = list(weights)
    return (loss, grad_x, *[grads[k].reshape(weights[k].shape) for k in keys], *[deltas[k] for k in keys],
            *[new_m[k] for k in keys], *[new_v[k] for k in keys])
```
